```python
import jax, jax.numpy as jnp
from jax import lax
import numpy as np

D_MODEL = 1024
BATCH = 8
SEQ = 8192
DEPTH = 4

N_META = 16
EPS = 1e-6
NEG_INF = -1e30
FOX_HEADS = 8
FOX_HEAD_DIM = 64
FOX_BLOCK = 128
FOX_W = FOX_HEADS * FOX_HEAD_DIM
GDN_HEADS = 8
GDN_HEAD_DIM = 128
GDN_CHUNK = 64
GDN_CONV = 4
GDN_W = GDN_HEADS * GDN_HEAD_DIM
N_BRANCH = 2
D_FF = 2816
FFN_CONV = 3
IN_SIZES = [FOX_W, FOX_W, FOX_W, FOX_HEADS,
            GDN_W, GDN_W, GDN_W, GDN_HEADS, GDN_HEADS,
            GDN_W,
            N_BRANCH * D_MODEL]
D_IN = int(sum(IN_SIZES))
IN_SPLIT = [int(c) for c in np.cumsum(IN_SIZES)[:-1]]

kernel_name = "hybrid_fox_gdn_convffn_trunk"


def rmsnorm(x, g):
    xf = x.astype(jnp.float32)
    y = xf * lax.rsqrt(jnp.mean(xf * xf, axis=-1, keepdims=True) + EPS)
    return (y * g.astype(jnp.float32)).astype(x.dtype)


def l2norm(x):
    return x * lax.rsqrt(jnp.sum(x * x, axis=-1, keepdims=True) + EPS)


def causal_dwconv(x, w):
    K, C = w.shape
    return lax.conv_general_dilated(x, w[:, None, :].astype(x.dtype), window_strides=(1,),
                                    padding=[(K - 1, 0)], dimension_numbers=('NWC', 'WIO', 'NWC'),
                                    feature_group_count=C)


def fox_attention(q, k, v, log_f):
    B_, L, H, Dh = q.shape
    pad = (-L) % FOX_BLOCK
    Lp = L + pad
    nb = Lp // FOX_BLOCK
    p4 = ((0, 0), (pad, 0), (0, 0), (0, 0))
    qp, kp, vp = jnp.pad(q, p4), jnp.pad(k, p4), jnp.pad(v, p4)
    F = jnp.cumsum(jnp.pad(log_f, ((0, 0), (pad, 0), (0, 0))), axis=1)
    Fk = F.transpose(0, 2, 1)
    kpos = jnp.arange(Lp)
    key_valid = kpos >= pad
    qb = qp.reshape(B_, nb, FOX_BLOCK, H, Dh).transpose(1, 0, 2, 3, 4)
    Fq = Fk.reshape(B_, H, nb, FOX_BLOCK).transpose(2, 0, 1, 3)
    scale = Dh ** -0.5

    def block(args):
        i, q_blk, F_blk = args
        qpos = i * FOX_BLOCK + jnp.arange(FOX_BLOCK)
        s = jnp.einsum('bqhd,bkhd->bhqk', q_blk, kp, preferred_element_type=jnp.float32) * scale
        s = s + F_blk[..., :, None] - Fk[:, :, None, :]
        mask = (kpos[None, :] <= qpos[:, None]) & key_valid[None, :]
        p = jax.nn.softmax(jnp.where(mask, s, NEG_INF), axis=-1)
        return jnp.einsum('bhqk,bkhd->bqhd', p.astype(vp.dtype), vp)

    out = lax.map(block, (jnp.arange(nb), qb, Fq))
    return out.transpose(1, 0, 2, 3, 4).reshape(B_, Lp, H, Dh)[:, pad:]


def gated_delta_rule(q, k, v, beta, g):
    B_, L, H, Dk = q.shape
    Dv = v.shape[-1]
    C = GDN_CHUNK
    pad = (-L) % C
    Lp = L + pad
    N = Lp // C
    p4 = ((0, 0), (pad, 0), (0, 0), (0, 0))
    p3 = ((0, 0), (pad, 0), (0, 0))

    def chunks4(t):
        return jnp.pad(t, p4).reshape(B_, N, C, H, t.shape[-1]).transpose(0, 3, 1, 2, 4)

    def chunks3(t):
        return jnp.pad(t, p3).reshape(B_, N, C, H).transpose(0, 3, 1, 2)

    qc, kc, vc = chunks4(q), chunks4(k), chunks4(v)
    bc, gc = chunks3(beta), chunks3(g)
    G = jnp.cumsum(gc, axis=-1)
    idx = jnp.arange(C)
    incl = idx[:, None] >= idx[None, :]
    strict = idx[:, None] > idx[None, :]
    diff = G[..., :, None] - G[..., None, :]
    decay_incl = jnp.exp(jnp.where(incl, diff, -jnp.inf))
    decay_strict = jnp.where(strict, decay_incl, 0.0)
    kb = kc * bc[..., None]
    M = jnp.einsum('bhncd,bhnsd->bhncs', kb, kc) * decay_strict
    A = M + jnp.eye(C, dtype=M.dtype)
    u_hat = lax.linalg.triangular_solve(A, vc * bc[..., None], left_side=True, lower=True, unit_diagonal=True)
    w = lax.linalg.triangular_solve(A, kb * jnp.exp(G)[..., None], left_side=True, lower=True, unit_diagonal=True)
    qg = qc * jnp.exp(G)[..., None]
    aqk = jnp.einsum('bhncd,bhnsd->bhncs', qc, kc) * decay_incl
    kd = kc * jnp.exp(G[..., -1:] - G)[..., None]
    gC = jnp.exp(G[..., -1])

    def to_front(t):
        return jnp.moveaxis(t, 2, 0)

    def step(S, xs):
        u_c, w_c, qg_c, a_c, kd_c, g_c = xs
        U = u_c - jnp.einsum('bhck,bhkv->bhcv', w_c, S)
        o = jnp.einsum('bhck,bhkv->bhcv', qg_c, S) + jnp.einsum('bhcs,bhsv->bhcv', a_c, U)
        S = S * g_c[..., None, None] + jnp.einsum('bhck,bhcv->bhkv', kd_c, U)
        return S, o

    S0 = jnp.zeros((B_, H, Dk, Dv), jnp.float32)
    _, o = lax.scan(step, S0, (to_front(u_hat), to_front(w), to_front(qg), to_front(aqk), to_front(kd), to_front(gC)))
    return o.transpose(1, 0, 3, 2, 4).reshape(B_, Lp, H, Dv)[:, pad:]


def _fwd_setup_inputs(seed: int = 0) -> dict:
    key = jax.random.key(seed)
    ks = jax.random.split(key, 20)
    f32 = jnp.float32

    def nrm(k, shape, scale):
        return jax.random.normal(k, shape, f32) * scale

    dt = jnp.exp(jax.random.uniform(ks[7], (DEPTH, GDN_HEADS), f32, np.log(1e-3), np.log(1e-1)))
    return {
        "x": nrm(ks[0], (BATCH, SEQ, D_MODEL), 1.0),
        "meta_tokens": nrm(ks[1], (N_META, D_MODEL), 1.0),
        "norm1_g": 1.0 + nrm(ks[2], (DEPTH, D_MODEL), 0.02),
        "w_in": nrm(ks[3], (DEPTH, D_MODEL, D_IN), D_MODEL ** -0.5),
        "fox_f_bias": 2.0 + nrm(ks[4], (DEPTH, FOX_HEADS), 0.5),
        "fox_q_norm_g": 1.0 + nrm(ks[5], (DEPTH, FOX_HEAD_DIM), 0.02),
        "fox_k_norm_g": 1.0 + nrm(ks[6], (DEPTH, FOX_HEAD_DIM), 0.02),
        "gdn_conv_w": nrm(ks[8], (DEPTH, GDN_CONV, 3 * GDN_W), GDN_CONV ** -0.5),
        "gdn_a_log": jnp.log(jax.random.uniform(ks[9], (DEPTH, GDN_HEADS), f32, 1.0, 16.0)),
        "gdn_dt_bias": jnp.log(jnp.expm1(dt)),
        "gdn_norm_g": 1.0 + nrm(ks[10], (DEPTH, GDN_HEAD_DIM), 0.02),
        "w_branch_a": nrm(ks[11], (DEPTH, FOX_W, D_MODEL), FOX_W ** -0.5),
        "w_branch_b": nrm(ks[12], (DEPTH, GDN_W, D_MODEL), GDN_W ** -0.5),
        "w_out": nrm(ks[13], (DEPTH, D_MODEL, D_MODEL), D_MODEL ** -0.5),
        "norm2_g": 1.0 + nrm(ks[14], (DEPTH, D_MODEL), 0.02),
        "w_up": nrm(ks[15], (DEPTH, D_MODEL, 2 * D_FF), D_MODEL ** -0.5),
        "ffn_conv_w": nrm(ks[16], (DEPTH, FFN_CONV, 2 * D_FF), FFN_CONV ** -0.5),
        "w_down": nrm(ks[17], (DEPTH, D_FF, D_MODEL), D_FF ** -0.5),
    }


def _fwd_reference(x, meta_tokens, norm1_g, w_in, fox_f_bias, fox_q_norm_g, fox_k_norm_g, gdn_conv_w,
              gdn_a_log, gdn_dt_bias, gdn_norm_g, w_branch_a, w_branch_b, w_out, norm2_g, w_up,
              ffn_conv_w, w_down):
    B_, S_, D = x.shape
    meta = jnp.broadcast_to(meta_tokens.astype(x.dtype)[None], (B_, N_META, D))
    h_res = jnp.concatenate([meta, x], axis=1)
    L = h_res.shape[1]
    f32 = jnp.float32
    for l in range(DEPTH):
        h = rmsnorm(h_res, norm1_g[l])
        proj = h @ w_in[l]
        (fq, fk, fv, f_logit, gq, gk, gv, b_logit, a_logit, gz, gate_logit) = jnp.split(proj, IN_SPLIT, axis=-1)

        fq = rmsnorm(fq.reshape(B_, L, FOX_HEADS, FOX_HEAD_DIM), fox_q_norm_g[l])
        fk = rmsnorm(fk.reshape(B_, L, FOX_HEADS, FOX_HEAD_DIM), fox_k_norm_g[l])
        fv = fv.reshape(B_, L, FOX_HEADS, FOX_HEAD_DIM)
        log_f = jax.nn.log_sigmoid(f_logit.astype(f32) + fox_f_bias[l].astype(f32))
        y_a = fox_attention(fq, fk, fv, log_f).reshape(B_, L, FOX_W) @ w_branch_a[l]

        qkv = jax.nn.silu(causal_dwconv(jnp.concatenate([gq, gk, gv], axis=-1), gdn_conv_w[l]))
        gq, gk, gv = jnp.split(qkv.astype(f32), 3, axis=-1)
        gq = l2norm(gq.reshape(B_, L, GDN_HEADS, GDN_HEAD_DIM)) * (GDN_HEAD_DIM ** -0.5)
        gk = l2norm(gk.reshape(B_, L, GDN_HEADS, GDN_HEAD_DIM))
        gv = gv.reshape(B_, L, GDN_HEADS, GDN_HEAD_DIM)
        beta = jax.nn.sigmoid(b_logit.astype(f32))
        g = -jnp.exp(gdn_a_log[l].astype(f32)) * jax.nn.softplus(a_logit.astype(f32) + gdn_dt_bias[l].astype(f32))
        o_b = gated_delta_rule(gq, gk, gv, beta, g)
        o_b = rmsnorm(o_b, gdn_norm_g[l]).astype(x.dtype) * jax.nn.silu(gz.reshape(B_, L, GDN_HEADS, GDN_HEAD_DIM))
        y_b = o_b.reshape(B_, L, GDN_W) @ w_branch_b[l]

        gates = jax.nn.sigmoid(gate_logit).reshape(B_, L, N_BRANCH, D)
        mixed = gates[:, :, 0] * y_a + gates[:, :, 1] * y_b
        h_res = h_res + mixed @ w_out[l]

        h = rmsnorm(h_res, norm2_g[l])
        up = causal_dwconv(h @ w_up[l], ffn_conv_w[l])
        u_gate, u_val = jnp.split(up, 2, axis=-1)
        h_res = h_res + (jax.nn.silu(u_gate) * u_val) @ w_down[l]
    return h_res[:, N_META:]


import jax as _jax
import jax.numpy as _jnp

TWIN_FORMAT = 'train_step'
FWD_PARAMS = ['x', 'meta_tokens', 'norm1_g', 'w_in', 'fox_f_bias', 'fox_q_norm_g', 'fox_k_norm_g', 'gdn_conv_w', 'gdn_a_log', 'gdn_dt_bias', 'gdn_norm_g', 'w_branch_a', 'w_branch_b', 'w_out', 'norm2_g', 'w_up', 'ffn_conv_w', 'w_down']
TWIN_WEIGHTS = ['meta_tokens', 'norm1_g', 'w_in', 'fox_f_bias', 'fox_q_norm_g', 'fox_k_norm_g', 'gdn_conv_w', 'gdn_a_log', 'gdn_dt_bias', 'gdn_norm_g', 'w_branch_a', 'w_branch_b', 'w_out', 'norm2_g', 'w_up', 'ffn_conv_w', 'w_down']
TWIN_DIFF_INPUT = 'x'
TWIN_INPUTS = ['x', 'meta_tokens', 'norm1_g', 'w_in', 'fox_f_bias', 'fox_q_norm_g', 'fox_k_norm_g', 'gdn_conv_w', 'gdn_a_log', 'gdn_dt_bias', 'gdn_norm_g', 'w_branch_a', 'w_branch_b', 'w_out', 'norm2_g', 'w_up', 'ffn_conv_w', 'w_down', 'loss_target', 'm_meta_tokens', 'm_norm1_g', 'm_w_in', 'm_fox_f_bias', 'm_fox_q_norm_g', 'm_fox_k_norm_g', 'm_gdn_conv_w', 'm_gdn_a_log', 'm_gdn_dt_bias', 'm_gdn_norm_g', 'm_w_branch_a', 'm_w_branch_b', 'm_w_out', 'm_norm2_g', 'm_w_up', 'm_ffn_conv_w', 'm_w_down', 'v_meta_tokens', 'v_norm1_g', 'v_w_in', 'v_fox_f_bias', 'v_fox_q_norm_g', 'v_fox_k_norm_g', 'v_gdn_conv_w', 'v_gdn_a_log', 'v_gdn_dt_bias', 'v_gdn_norm_g', 'v_w_branch_a', 'v_w_branch_b', 'v_w_out', 'v_norm2_g', 'v_w_up', 'v_ffn_conv_w', 'v_w_down']
TWIN_OUTPUTS = ['loss', 'grad_x', 'grad_meta_tokens', 'grad_norm1_g', 'grad_w_in', 'grad_fox_f_bias', 'grad_fox_q_norm_g', 'grad_fox_k_norm_g', 'grad_gdn_conv_w', 'grad_gdn_a_log', 'grad_gdn_dt_bias', 'grad_gdn_norm_g', 'grad_w_branch_a', 'grad_w_branch_b', 'grad_w_out', 'grad_norm2_g', 'grad_w_up', 'grad_ffn_conv_w', 'grad_w_down', 'delta_meta_tokens', 'delta_norm1_g', 'delta_w_in', 'delta_fox_f_bias', 'delta_fox_q_norm_g', 'delta_fox_k_norm_g', 'delta_gdn_conv_w', 'delta_gdn_a_log', 'delta_gdn_dt_bias', 'delta_gdn_norm_g', 'delta_w_branch_a', 'delta_w_branch_b', 'delta_w_out', 'delta_norm2_g', 'delta_w_up', 'delta_ffn_conv_w', 'delta_w_down', 'new_m_meta_tokens', 'new_m_norm1_g', 'new_m_w_in', 'new_m_fox_f_bias', 'new_m_fox_q_norm_g', 'new_m_fox_k_norm_g', 'new_m_gdn_conv_w', 'new_m_gdn_a_log', 'new_m_gdn_dt_bias', 'new_m_gdn_norm_g', 'new_m_w_branch_a', 'new_m_w_branch_b', 'new_m_w_out', 'new_m_norm2_g', 'new_m_w_up', 'new_m_ffn_conv_w', 'new_m_w_down', 'new_v_meta_tokens', 'new_v_norm1_g', 'new_v_w_in', 'new_v_fox_f_bias', 'new_v_fox_q_norm_g', 'new_v_fox_k_norm_g', 'new_v_gdn_conv_w', 'new_v_gdn_a_log', 'new_v_gdn_dt_bias', 'new_v_gdn_norm_g', 'new_v_w_branch_a', 'new_v_w_branch_b', 'new_v_w_out', 'new_v_norm2_g', 'new_v_w_up', 'new_v_ffn_conv_w', 'new_v_w_down']
TWIN_LEAF_KINDS = {'loss': 'loss', 'grad_x': 'grad_x', 'grad_meta_tokens': 'grad_w', 'grad_norm1_g': 'grad_w', 'grad_w_in': 'grad_w', 'grad_fox_f_bias': 'grad_w', 'grad_fox_q_norm_g': 'grad_w', 'grad_fox_k_norm_g': 'grad_w', 'grad_gdn_conv_w': 'grad_w', 'grad_gdn_a_log': 'grad_w', 'grad_gdn_dt_bias': 'grad_w', 'grad_gdn_norm_g': 'grad_w', 'grad_w_branch_a': 'grad_w', 'grad_w_branch_b': 'grad_w', 'grad_w_out': 'grad_w', 'grad_norm2_g': 'grad_w', 'grad_w_up': 'grad_w', 'grad_ffn_conv_w': 'grad_w', 'grad_w_down': 'grad_w', 'delta_meta_tokens': 'delta_w', 'delta_norm1_g': 'delta_w', 'delta_w_in': 'delta_w', 'delta_fox_f_bias': 'delta_w', 'delta_fox_q_norm_g': 'delta_w', 'delta_fox_k_norm_g': 'delta_w', 'delta_gdn_conv_w': 'delta_w', 'delta_gdn_a_log': 'delta_w', 'delta_gdn_dt_bias': 'delta_w', 'delta_gdn_norm_g': 'delta_w', 'delta_w_branch_a': 'delta_w', 'delta_w_branch_b': 'delta_w', 'delta_w_out': 'delta_w', 'delta_norm2_g': 'delta_w', 'delta_w_up': 'delta_w', 'delta_ffn_conv_w': 'delta_w', 'delta_w_down': 'delta_w', 'new_m_meta_tokens': 'new_m', 'new_m_norm1_g': 'new_m', 'new_m_w_in': 'new_m', 'new_m_fox_f_bias': 'new_m', 'new_m_fox_q_norm_g': 'new_m', 'new_m_fox_k_norm_g': 'new_m', 'new_m_gdn_conv_w': 'new_m', 'new_m_gdn_a_log': 'new_m', 'new_m_gdn_dt_bias': 'new_m', 'new_m_gdn_norm_g': 'new_m', 'new_m_w_branch_a': 'new_m', 'new_m_w_branch_b': 'new_m', 'new_m_w_out': 'new_m', 'new_m_norm2_g': 'new_m', 'new_m_w_up': 'new_m', 'new_m_ffn_conv_w': 'new_m', 'new_m_w_down': 'new_m', 'new_v_meta_tokens': 'new_v', 'new_v_norm1_g': 'new_v', 'new_v_w_in': 'new_v', 'new_v_fox_f_bias': 'new_v', 'new_v_fox_q_norm_g': 'new_v', 'new_v_fox_k_norm_g': 'new_v', 'new_v_gdn_conv_w': 'new_v', 'new_v_gdn_a_log': 'new_v', 'new_v_gdn_dt_bias': 'new_v', 'new_v_gdn_norm_g': 'new_v', 'new_v_w_branch_a': 'new_v', 'new_v_w_branch_b': 'new_v', 'new_v_w_out': 'new_v', 'new_v_norm2_g': 'new_v', 'new_v_w_up': 'new_v', 'new_v_ffn_conv_w': 'new_v', 'new_v_w_down': 'new_v'}


def _forward(args):
    return _fwd_reference(*[args[k] for k in FWD_PARAMS])


def _output_shape():
    out = _jax.eval_shape(lambda: _forward(_fwd_setup_inputs(0)))
    return out.shape, out.dtype

N_MICROBATCH = 1
ADAM_LR = 0.001
ADAM_B1 = 0.9
ADAM_B2 = 0.999
ADAM_EPS = 1e-08
ADAM_WD = 0.01
ADAM_STEP = 10
PER_EXAMPLE_BATCH_AXIS = {'x': 0, 'loss_target': 0}
SHARED_INPUTS = []
_WEIGHT_DTYPES = {'meta_tokens': _jnp.float32, 'norm1_g': _jnp.float32, 'w_in': _jnp.float32, 'fox_f_bias': _jnp.float32, 'fox_q_norm_g': _jnp.float32, 'fox_k_norm_g': _jnp.float32, 'gdn_conv_w': _jnp.float32, 'gdn_a_log': _jnp.float32, 'gdn_dt_bias': _jnp.float32, 'gdn_norm_g': _jnp.float32, 'w_branch_a': _jnp.float32, 'w_branch_b': _jnp.float32, 'w_out': _jnp.float32, 'norm2_g': _jnp.float32, 'w_up': _jnp.float32, 'ffn_conv_w': _jnp.float32, 'w_down': _jnp.float32}
MOMENT_SCALE = {'meta_tokens': 7.938389e-02, 'norm1_g': 1.177739e+01, 'w_in': 3.666062e-01, 'fox_f_bias': 1.312471e+02, 'fox_q_norm_g': 1.681998e+01, 'fox_k_norm_g': 1.681775e+01, 'gdn_conv_w': 5.291658e-01, 'gdn_a_log': 1.954922e+01, 'gdn_dt_bias': 1.861468e+01, 'gdn_norm_g': 5.368436e+01, 'w_branch_a': 4.708992e-01, 'w_branch_b': 1.267060e+00, 'w_out': 1.300751e+00, 'norm2_g': 5.057913e+01, 'w_up': 5.757124e-01, 'ffn_conv_w': 6.623852e+00, 'w_down': 7.407473e-01}


def _to_microbatches(a, axis):
    t = _jnp.moveaxis(a, axis, 0)
    t = t.reshape((N_MICROBATCH, t.shape[0] // N_MICROBATCH) + t.shape[1:])
    return _jnp.moveaxis(t, 1, axis + 1)


def setup_inputs(seed: int = 0) -> dict:
    inp = _fwd_setup_inputs(seed)
    key = _jax.random.fold_in(_jax.random.key(seed), 7919)
    shape, _ = _output_shape()
    out = dict(inp)
    out["loss_target"] = _jax.random.normal(_jax.random.fold_in(key, 0), shape, _jnp.float32)
    for i, name in enumerate(TWIN_WEIGHTS):
        w = inp[name].astype(_jnp.float32)
        if MOMENT_SCALE is None:
            s = _jnp.sqrt(_jnp.mean(_jnp.square(w)) + 1e-30)
        else:
            s = MOMENT_SCALE[name]
        km, kv = _jax.random.split(_jax.random.fold_in(key, i + 1))
        out[name] = w
        out["m_" + name] = s * _jax.random.normal(km, w.shape, _jnp.float32)
        out["v_" + name] = (s * s) * _jax.random.uniform(kv, w.shape, _jnp.float32, 0.5, 1.5)
    if N_MICROBATCH > 1:
        for name, axis in PER_EXAMPLE_BATCH_AXIS.items():
            out[name] = _to_microbatches(out[name], axis)
    return {'x': out['x'], 'meta_tokens': out['meta_tokens'], 'norm1_g': out['norm1_g'], 'w_in': out['w_in'], 'fox_f_bias': out['fox_f_bias'], 'fox_q_norm_g': out['fox_q_norm_g'], 'fox_k_norm_g': out['fox_k_norm_g'], 'gdn_conv_w': out['gdn_conv_w'], 'gdn_a_log': out['gdn_a_log'], 'gdn_dt_bias': out['gdn_dt_bias'], 'gdn_norm_g': out['gdn_norm_g'], 'w_branch_a': out['w_branch_a'], 'w_branch_b': out['w_branch_b'], 'w_out': out['w_out'], 'norm2_g': out['norm2_g'], 'w_up': out['w_up'], 'ffn_conv_w': out['ffn_conv_w'], 'w_down': out['w_down'], 'loss_target': out['loss_target'], 'm_meta_tokens': out['m_meta_tokens'], 'm_norm1_g': out['m_norm1_g'], 'm_w_in': out['m_w_in'], 'm_fox_f_bias': out['m_fox_f_bias'], 'm_fox_q_norm_g': out['m_fox_q_norm_g'], 'm_fox_k_norm_g': out['m_fox_k_norm_g'], 'm_gdn_conv_w': out['m_gdn_conv_w'], 'm_gdn_a_log': out['m_gdn_a_log'], 'm_gdn_dt_bias': out['m_gdn_dt_bias'], 'm_gdn_norm_g': out['m_gdn_norm_g'], 'm_w_branch_a': out['m_w_branch_a'], 'm_w_branch_b': out['m_w_branch_b'], 'm_w_out': out['m_w_out'], 'm_norm2_g': out['m_norm2_g'], 'm_w_up': out['m_w_up'], 'm_ffn_conv_w': out['m_ffn_conv_w'], 'm_w_down': out['m_w_down'], 'v_meta_tokens': out['v_meta_tokens'], 'v_norm1_g': out['v_norm1_g'], 'v_w_in': out['v_w_in'], 'v_fox_f_bias': out['v_fox_f_bias'], 'v_fox_q_norm_g': out['v_fox_q_norm_g'], 'v_fox_k_norm_g': out['v_fox_k_norm_g'], 'v_gdn_conv_w': out['v_gdn_conv_w'], 'v_gdn_a_log': out['v_gdn_a_log'], 'v_gdn_dt_bias': out['v_gdn_dt_bias'], 'v_gdn_norm_g': out['v_gdn_norm_g'], 'v_w_branch_a': out['v_w_branch_a'], 'v_w_branch_b': out['v_w_branch_b'], 'v_w_out': out['v_w_out'], 'v_norm2_g': out['v_norm2_g'], 'v_w_up': out['v_w_up'], 'v_ffn_conv_w': out['v_ffn_conv_w'], 'v_w_down': out['v_w_down']}


def _loss(weights, diff, rest, loss_target):
    with _jax.named_scope("forward"):
        args = {**rest, TWIN_DIFF_INPUT: diff, **{k: w.astype(_WEIGHT_DTYPES[k]) for k, w in weights.items()}}
        y = _forward(args)
    with _jax.named_scope("loss_head"):
        err = _jnp.square(y.astype(_jnp.float32) - loss_target)
        return 0.5 * _jnp.sum(_jnp.mean(err, axis=-1)) if err.ndim else 0.5 * err


def _adamw(w, g, m, v):
    m = ADAM_B1 * m + (1.0 - ADAM_B1) * g
    v = ADAM_B2 * v + (1.0 - ADAM_B2) * _jnp.square(g)
    m_hat = m / (1.0 - ADAM_B1 ** ADAM_STEP)
    v_hat = v / (1.0 - ADAM_B2 ** ADAM_STEP)
    delta = -ADAM_LR * (m_hat / (_jnp.sqrt(v_hat) + ADAM_EPS) + ADAM_WD * w)
    return delta, m, v


def reference(x, meta_tokens, norm1_g, w_in, fox_f_bias, fox_q_norm_g, fox_k_norm_g, gdn_conv_w, gdn_a_log, gdn_dt_bias, gdn_norm_g, w_branch_a, w_branch_b, w_out, norm2_g, w_up, ffn_conv_w, w_down, loss_target, m_meta_tokens, m_norm1_g, m_w_in, m_fox_f_bias, m_fox_q_norm_g, m_fox_k_norm_g, m_gdn_conv_w, m_gdn_a_log, m_gdn_dt_bias, m_gdn_norm_g, m_w_branch_a, m_w_branch_b, m_w_out, m_norm2_g, m_w_up, m_ffn_conv_w, m_w_down, v_meta_tokens, v_norm1_g, v_w_in, v_fox_f_bias, v_fox_q_norm_g, v_fox_k_norm_g, v_gdn_conv_w, v_gdn_a_log, v_gdn_dt_bias, v_gdn_norm_g, v_w_branch_a, v_w_branch_b, v_w_out, v_norm2_g, v_w_up, v_ffn_conv_w, v_w_down):
    given = dict(x=x, meta_tokens=meta_tokens, norm1_g=norm1_g, w_in=w_in, fox_f_bias=fox_f_bias, fox_q_norm_g=fox_q_norm_g, fox_k_norm_g=fox_k_norm_g, gdn_conv_w=gdn_conv_w, gdn_a_log=gdn_a_log, gdn_dt_bias=gdn_dt_bias, gdn_norm_g=gdn_norm_g, w_branch_a=w_branch_a, w_branch_b=w_branch_b, w_out=w_out, norm2_g=norm2_g, w_up=w_up, ffn_conv_w=ffn_conv_w, w_down=w_down, loss_target=loss_target, m_meta_tokens=m_meta_tokens, m_norm1_g=m_norm1_g, m_w_in=m_w_in, m_fox_f_bias=m_fox_f_bias, m_fox_q_norm_g=m_fox_q_norm_g, m_fox_k_norm_g=m_fox_k_norm_g, m_gdn_conv_w=m_gdn_conv_w, m_gdn_a_log=m_gdn_a_log, m_gdn_dt_bias=m_gdn_dt_bias, m_gdn_norm_g=m_gdn_norm_g, m_w_branch_a=m_w_branch_a, m_w_branch_b=m_w_branch_b, m_w_out=m_w_out, m_norm2_g=m_norm2_g, m_w_up=m_w_up, m_ffn_conv_w=m_ffn_conv_w, m_w_down=m_w_down, v_meta_tokens=v_meta_tokens, v_norm1_g=v_norm1_g, v_w_in=v_w_in, v_fox_f_bias=v_fox_f_bias, v_fox_q_norm_g=v_fox_q_norm_g, v_fox_k_norm_g=v_fox_k_norm_g, v_gdn_conv_w=v_gdn_conv_w, v_gdn_a_log=v_gdn_a_log, v_gdn_dt_bias=v_gdn_dt_bias, v_gdn_norm_g=v_gdn_norm_g, v_w_branch_a=v_w_branch_a, v_w_branch_b=v_w_branch_b, v_w_out=v_w_out, v_norm2_g=v_norm2_g, v_w_up=v_w_up, v_ffn_conv_w=v_ffn_conv_w, v_w_down=v_w_down)
    weights = {n: given[n] for n in TWIN_WEIGHTS}
    shared = {n: given[n] for n in SHARED_INPUTS}
    per_example = {n: given[n] for n in ['x']}
    grad_fn = _jax.value_and_grad(_loss, argnums=(0, 1))

    def one_microbatch(ex, loss_target):
        ex = dict(ex)
        diff = ex.pop(TWIN_DIFF_INPUT)
        return grad_fn(weights, diff, {**shared, **ex}, loss_target)

    if N_MICROBATCH == 1:
        loss, (grad_w, grad_x) = one_microbatch(per_example, given["loss_target"])
    else:
        def body(carry, xs):
            loss_sum, grad_sum = carry
            l_k, (gw_k, gx_k) = one_microbatch(xs[0], xs[1])
            with _jax.named_scope("update"):
                return (loss_sum + l_k, _jax.tree.map(_jnp.add, grad_sum, gw_k)), gx_k

        init = (_jnp.zeros((), _jnp.float32), _jax.tree.map(_jnp.zeros_like, weights))
        (loss, grad_w), grad_x = _jax.lax.scan(body, init, (per_example, given["loss_target"]))
    with _jax.named_scope("update"):
        delta_w, new_m, new_v = {}, {}, {}
        for n in TWIN_WEIGHTS:
            delta_w[n], new_m[n], new_v[n] = _adamw(weights[n], grad_w[n], given["m_" + n], given["v_" + n])
    return (loss, grad_x, *[grad_w[n] for n in TWIN_WEIGHTS], *[delta_w[n] for n in TWIN_WEIGHTS],
            *[new_m[n] for n in TWIN_WEIGHTS], *[new_v[n] for n in TWIN_WEIGHTS])
```

```python
import functools

import jax
import jax.numpy as jnp
from jax import lax
from jax.experimental import pallas as pl
from jax.experimental.pallas import tpu as pltpu

F32 = jnp.float32
BF16 = jnp.bfloat16
HI = lax.Precision.HIGHEST
MESH = pl.DeviceIdType.MESH

D_MODEL = 1024
N_META = 16
EPS = 1e-6
FOX_HEADS, FOX_HD = 8, 64
FOX_W = FOX_HEADS * FOX_HD
GDN_HEADS, GDN_HD, GDN_CHUNK, GDN_CONV = 8, 128, 64, 4
GDN_W = GDN_HEADS * GDN_HD
D_FF = 2816
FFN_CONV = 3
D_IN = 7704
D_IN_PAD = 8192
IN_SEGS = ((0, 1536, 0), (1536, 8, 1536), (1544, 3072, 1664), (4616, 16, 4736), (4632, 1024, 4864), (5656, 2048, 5888))
ROW_ALIGN = 256
ATT_BLK = 256
VMEM_LIMIT = 48 * 1024 * 1024
LANE = 128

ADAM_LR, ADAM_B1, ADAM_B2, ADAM_EPS, ADAM_WD, ADAM_STEP = 0.001, 0.9, 0.999, 1e-08, 0.01, 10


def _call(body, **kw):
    return pl.pallas_call(body, **kw)


def _tile(n, target, mult):
    best, t = None, mult
    while t <= min(n, target):
        if n % t == 0:
            best = t
        t += mult
    assert best is not None, (n, target, mult)
    return best


def _cparams(sem):
    return pltpu.CompilerParams(dimension_semantics=sem, vmem_limit_bytes=VMEM_LIMIT)


def _raw_dot(a, b, ca, cb, precise):
    dims = (((ca,), (cb,)), ((), ()))
    if precise:
        return lax.dot_general(a.astype(F32), b.astype(F32), dims, precision=HI, preferred_element_type=F32)
    return lax.dot_general(a.astype(BF16), b.astype(BF16), dims, preferred_element_type=F32)


def _make_dot(ca, cb, precise):
    @jax.custom_vjp
    def f(a, b):
        return _raw_dot(a, b, ca, cb, precise)

    def fwd(a, b):
        return f(a, b), (a, b)

    def bwd(res, ct):
        a, b = res
        if ca == 1:
            da = _raw_dot(ct, b, 1, 1 if cb == 0 else 0, precise)
        else:
            da = _raw_dot(b, ct, 1 if cb == 0 else 0, 1, precise)
        if cb == 0:
            db = _raw_dot(a, ct, 0 if ca == 1 else 1, 0, precise)
        else:
            db = _raw_dot(ct, a, 0, 0 if ca == 1 else 1, precise)
        return da, db

    f.defvjp(fwd, bwd)
    return f


_DOTS = {(ca, cb, p): _make_dot(ca, cb, p) for ca in (0, 1) for cb in (0, 1) for p in (False, True)}


def _dot(a, b, ca=1, cb=0, precise=False):
    return _DOTS[(ca, cb, precise)](a, b)


def _mm_call(a, b, ta, tb, name):
    m, k = (a.shape[1], a.shape[0]) if ta else a.shape
    n = b.shape[0] if tb else b.shape[1]
    assert k == (b.shape[1] if tb else b.shape[0]), (a.shape, b.shape, ta, tb)
    tm = _tile(m, 768, LANE if ta else 8)
    tn = _tile(n, 512, LANE)
    tk = _tile(k, 1408, LANE)
    nk = k // tk
    dims = (((0 if ta else 1,), (1 if tb else 0,)), ((), ()))

    def body(a_ref, b_ref, o_ref, acc_ref):
        kk = pl.program_id(2)

        @pl.when(kk == 0)
        def _():
            acc_ref[...] = jnp.zeros_like(acc_ref)

        acc_ref[...] += lax.dot_general(a_ref[...].astype(BF16), b_ref[...].astype(BF16), dims,
                                        preferred_element_type=F32)

        @pl.when(kk == nk - 1)
        def _():
            o_ref[...] = acc_ref[...]

    a_spec = pl.BlockSpec((tk, tm), lambda i, j, kk: (kk, i)) if ta else pl.BlockSpec((tm, tk), lambda i, j, kk: (i, kk))
    b_spec = pl.BlockSpec((tn, tk), lambda i, j, kk: (j, kk)) if tb else pl.BlockSpec((tk, tn), lambda i, j, kk: (kk, j))
    return _call(
        body, name=name, grid=(m // tm, n // tn, nk), in_specs=[a_spec, b_spec],
        out_specs=pl.BlockSpec((tm, tn), lambda i, j, kk: (i, j)),
        out_shape=jax.ShapeDtypeStruct((m, n), F32), scratch_shapes=[pltpu.VMEM((tm, tn), F32)],
        compiler_params=_cparams(("parallel", "parallel", "arbitrary")),
    )(a, b)


@jax.custom_vjp
def mm(a, w):
    return _mm_call(a, w, False, False, "mm_nn")


def _mm_fwd(a, w):
    return mm(a, w), (a, w)


def _mm_bwd(res, ct):
    a, w = res
    return _mm_call(ct, w, False, True, "mm_nt"), _mm_call(a, ct, True, False, "mm_tn")


mm.defvjp(_mm_fwd, _mm_bwd)


def _rows_specs(rows, tm, ncb, bc):
    specs = []
    for idx, r in enumerate(rows):
        if idx in bc:
            specs.append(pl.BlockSpec((tm, r.shape[1]), lambda i, j: (i, 0)))
        else:
            specs.append(pl.BlockSpec((tm, r.shape[1] // ncb), lambda i, j: (i, j)))
    return specs


def _param_specs(params):
    return [pl.BlockSpec(p.shape, lambda i, j: (0, 0)) for p in params]


def _rows_fwd_call(fn, rows, params, outs, tm, ncb, bc, name):
    r_total = rows[0].shape[0]
    n_in = len(rows) + len(params)

    def body(*refs):
        res = fn(*[r[...] for r in refs[:n_in]])
        for o_ref, val in zip(refs[n_in:], res):
            o_ref[...] = val.astype(o_ref.dtype)

    return _call(
        body, name=name, grid=(r_total // tm, ncb),
        in_specs=_rows_specs(rows, tm, ncb, bc) + _param_specs(params),
        out_specs=[pl.BlockSpec((tm, w), lambda i, j: (i, j)) for w in outs],
        out_shape=[jax.ShapeDtypeStruct((r_total, w * ncb), F32) for w in outs],
        compiler_params=_cparams(("parallel", "parallel")),
    )(*rows, *params)


def _rows_bwd_call(fn, rows, params, cts, tm, ncb, bc, name):
    r_total = rows[0].shape[0]
    nr, npar, nct = len(rows), len(params), len(cts)

    def body(*refs):
        i, j = pl.program_id(0), pl.program_id(1)
        ins = [r[...] for r in refs[:nr + npar]]
        ct_vals = tuple(r[...] for r in refs[nr + npar:nr + npar + nct])
        d_refs = refs[nr + npar + nct:]
        _, vjp = jax.vjp(lambda *a: tuple(fn(*a)), *ins)
        grads = vjp(ct_vals)
        for idx in range(nr):
            if idx in bc:
                @pl.when(j == 0)
                def _(idx=idx):
                    d_refs[idx][...] = jnp.zeros_like(d_refs[idx])
                d_refs[idx][...] += grads[idx]
            else:
                d_refs[idx][...] = grads[idx]
        for idx in range(nr, nr + npar):
            @pl.when((i == 0) & (j == 0))
            def _(idx=idx):
                d_refs[idx][...] = jnp.zeros_like(d_refs[idx])
            d_refs[idx][...] += grads[idx]

    ct_specs = [pl.BlockSpec((tm, c.shape[1] // ncb), lambda i, j: (i, j)) for c in cts]
    return _call(
        body, name=name + "_bwd", grid=(r_total // tm, ncb),
        in_specs=_rows_specs(rows, tm, ncb, bc) + _param_specs(params) + ct_specs,
        out_specs=_rows_specs(rows, tm, ncb, bc) + _param_specs(params),
        out_shape=[jax.ShapeDtypeStruct(a.shape, F32) for a in list(rows) + list(params)],
        compiler_params=_cparams(("arbitrary", "arbitrary")),
    )(*rows, *params, *cts)


def rowop(fn, name, outs, tm, ncb=1, bc=()):
    @jax.custom_vjp
    def op(rows, params):
        return tuple(_rows_fwd_call(fn, rows, params, outs, tm, ncb, bc, name))

    def fwd(rows, params):
        return op(rows, params), (rows, params)

    def bwd(res, cts):
        rows, params = res
        d = _rows_bwd_call(fn, rows, params, cts, tm, ncb, bc, name)
        return tuple(d[:len(rows)]), tuple(d[len(rows):])

    op.defvjp(fwd, bwd)
    return op


def _sigmoid(x):
    return 1.0 / (1.0 + jnp.exp(-x))


def _silu(x):
    return x * _sigmoid(x)


def _softplus(x):
    return jnp.maximum(x, 0.0) + jnp.log(1.0 + jnp.exp(-jnp.abs(x)))


def _f_rmsnorm(x, g):
    return (x * lax.rsqrt(jnp.mean(x * x, axis=-1, keepdims=True) + EPS) * g,)


def _f_qnorm(x, g):
    return (x * lax.rsqrt(jnp.mean(x * x, axis=-1, keepdims=True) + EPS) * (g * (FOX_HD ** -0.5)),)


def _f_logsig(x, b):
    return (-_softplus(-(x + b)),)


def _f_gdn_q(x):
    y = _silu(x)
    return (y * lax.rsqrt(jnp.sum(y * y, axis=-1, keepdims=True) + EPS) * (GDN_HD ** -0.5),)


def _f_gdn_k(x):
    y = _silu(x)
    return (y * lax.rsqrt(jnp.sum(y * y, axis=-1, keepdims=True) + EPS),)


def _f_silu(x):
    return (_silu(x),)


def _f_gdn_gates(bl, al, a_log, dt_bias):
    return _sigmoid(bl), -jnp.exp(a_log) * _softplus(al + dt_bias)


def _f_gdn_out(o, z, g):
    return (o * lax.rsqrt(jnp.mean(o * o, axis=-1, keepdims=True) + EPS) * g * _silu(z),)


def _f_merge(g0, g1, ya, yb):
    return (_sigmoid(g0) * ya + _sigmoid(g1) * yb,)


def _f_glu(a, b):
    return (_silu(a) * b,)


def _f_delta(do, o):
    return (jnp.sum(do * o, axis=-1, keepdims=True),)


def _shift_down(x, halo, s, row8):
    rx = pltpu.roll(x, s, 0)
    top = jnp.where(row8 < s, pltpu.roll(halo, s, 0), rx[:8])
    return jnp.concatenate([top, rx[8:]], axis=0)


def _shift_up(x, nxt, s, row8):
    tm = x.shape[0]
    rx = pltpu.roll(x, tm - s, 0)
    bot = jnp.where(row8 >= 8 - s, pltpu.roll(nxt, 8 - s, 0), rx[tm - 8:])
    return jnp.concatenate([rx[:tm - 8], bot], axis=0)


def _conv_tiles(r_total, c_total):
    return _tile(r_total, 256, 8), _tile(c_total, 1024, LANE)


def _conv_fwd_call(x, w8, k_taps):
    r_total, c_total = x.shape
    tm, tc = _conv_tiles(r_total, c_total)
    hb = tm // 8

    def body(x_ref, halo_ref, w_ref, y_ref):
        i = pl.program_id(1)
        xt = x_ref[...]
        halo = jnp.where(i > 0, halo_ref[...], 0.0)
        row8 = lax.broadcasted_iota(jnp.int32, (8, tc), 0)
        acc = w_ref[k_taps - 1:k_taps, :] * xt
        for k in range(k_taps - 1):
            acc += w_ref[k:k + 1, :] * _shift_down(xt, halo, k_taps - 1 - k, row8)
        y_ref[...] = acc

    return _call(
        body, name="dwconv_fwd", grid=(c_total // tc, r_total // tm),
        in_specs=[pl.BlockSpec((tm, tc), lambda c, i: (i, c)),
                  pl.BlockSpec((8, tc), lambda c, i: (jnp.maximum(i * hb - 1, 0), c)),
                  pl.BlockSpec((8, tc), lambda c, i: (0, c))],
        out_specs=pl.BlockSpec((tm, tc), lambda c, i: (i, c)),
        out_shape=jax.ShapeDtypeStruct(x.shape, F32),
        compiler_params=_cparams(("parallel", "parallel")),
    )(x, x, w8)


def _conv_bwd_call(x, w8, dy, k_taps):
    r_total, c_total = x.shape
    tm, tc = _conv_tiles(r_total, c_total)
    hb = tm // 8
    n_i = r_total // tm

    def body(x_ref, halo_ref, w_ref, dy_ref, nxt_ref, dx_ref, dw_ref):
        i = pl.program_id(1)
        xt, dyt = x_ref[...], dy_ref[...]
        halo = jnp.where(i > 0, halo_ref[...], 0.0)
        nxt = jnp.where(i < n_i - 1, nxt_ref[...], 0.0)
        row8 = lax.broadcasted_iota(jnp.int32, (8, tc), 0)
        dx = w_ref[k_taps - 1:k_taps, :] * dyt
        upd = jnp.where(row8 == k_taps - 1, jnp.sum(dyt * xt, axis=0, keepdims=True), 0.0)
        for k in range(k_taps - 1):
            s = k_taps - 1 - k
            dx += w_ref[k:k + 1, :] * _shift_up(dyt, nxt, s, row8)
            upd = jnp.where(row8 == k, jnp.sum(dyt * _shift_down(xt, halo, s, row8), axis=0, keepdims=True), upd)
        dx_ref[...] = dx

        @pl.when(i == 0)
        def _():
            dw_ref[...] = jnp.zeros_like(dw_ref)

        dw_ref[...] += upd

    return _call(
        body, name="dwconv_bwd", grid=(c_total // tc, n_i),
        in_specs=[pl.BlockSpec((tm, tc), lambda c, i: (i, c)),
                  pl.BlockSpec((8, tc), lambda c, i: (jnp.maximum(i * hb - 1, 0), c)),
                  pl.BlockSpec((8, tc), lambda c, i: (0, c)),
                  pl.BlockSpec((tm, tc), lambda c, i: (i, c)),
                  pl.BlockSpec((8, tc), lambda c, i: (jnp.minimum((i + 1) * hb, r_total // 8 - 1), c))],
        out_specs=[pl.BlockSpec((tm, tc), lambda c, i: (i, c)), pl.BlockSpec((8, tc), lambda c, i: (0, c))],
        out_shape=[jax.ShapeDtypeStruct(x.shape, F32), jax.ShapeDtypeStruct(w8.shape, F32)],
        compiler_params=_cparams(("parallel", "arbitrary")),
    )(x, x, w8, dy, dy)


def make_dwconv(k_taps):
    @jax.custom_vjp
    def op(x, w8):
        return _conv_fwd_call(x, w8, k_taps)

    def fwd(x, w8):
        return op(x, w8), (x, w8)

    def bwd(res, dy):
        x, w8 = res
        dx, dw = _conv_bwd_call(x, w8, dy, k_taps)
        return dx, dw

    op.defvjp(fwd, bwd)
    return op


def _cumsum_call(x, reverse):
    h, t_total = x.shape
    tb = _tile(t_total, 256, LANE)
    nb = t_total // tb

    def body(x_ref, o_ref, carry_ref):
        i = pl.program_id(0)

        @pl.when(i == 0)
        def _():
            carry_ref[...] = jnp.zeros_like(carry_ref)

        r = lax.broadcasted_iota(jnp.int32, (tb, tb), 0)
        c = lax.broadcasted_iota(jnp.int32, (tb, tb), 1)
        tri = jnp.where((r >= c) if reverse else (r <= c), 1.0, 0.0).astype(F32)
        xv = x_ref[...]
        carry = jnp.max(carry_ref[...], axis=1, keepdims=True)
        o_ref[...] = _raw_dot(xv, tri, 1, 0, True) + carry
        carry_ref[...] = jnp.broadcast_to(carry + jnp.sum(xv, axis=1, keepdims=True), carry_ref.shape)

    imap = (lambda i: (0, nb - 1 - i)) if reverse else (lambda i: (0, i))
    return _call(
        body, name="cumsum_rev" if reverse else "cumsum", grid=(nb,),
        in_specs=[pl.BlockSpec((h, tb), imap)], out_specs=pl.BlockSpec((h, tb), imap),
        out_shape=jax.ShapeDtypeStruct(x.shape, F32), scratch_shapes=[pltpu.VMEM((h, LANE), F32)],
        compiler_params=_cparams(("arbitrary",)),
    )(x)


@jax.custom_vjp
def cumsum_lanes(x):
    return _cumsum_call(x, False)


cumsum_lanes.defvjp(lambda x: (cumsum_lanes(x), None), lambda _, ct: (_cumsum_call(ct, True),))


NEG_BIG = -1e30


def _attn_fwd_call(q, k, v, f_col, f_row):
    h_total, t_total, hd = q.shape
    blk = f_row.shape[-1]
    nb = t_total // blk

    def body(q_ref, k_ref, v_ref, fc_ref, fr_ref, o_ref, lse_ref):
        i = pl.program_id(1)
        qi = q_ref[0].astype(BF16)
        fq = fc_ref[0]
        r = lax.broadcasted_iota(jnp.int32, (blk, blk), 0)
        c = lax.broadcasted_iota(jnp.int32, (blk, blk), 1)

        def kv_step(j, carry):
            m, l, acc = carry
            off = pl.multiple_of(j * blk, blk)
            kj = k_ref[0, pl.ds(off, blk), :].astype(BF16)
            vj = v_ref[0, pl.ds(off, blk), :].astype(BF16)
            s = _raw_dot(qi, kj, 1, 1, False) + fq - fr_ref[0, j]
            s = jnp.where(c + j * blk <= r + i * blk, s, NEG_BIG)
            m_new = jnp.maximum(m, jnp.max(s, axis=1, keepdims=True))
            p = jnp.exp(s - m_new)
            alpha = jnp.exp(m - m_new)
            l = alpha * l + jnp.sum(p, axis=1, keepdims=True)
            acc = alpha * acc + _raw_dot(p, vj, 1, 0, False)
            return m_new, l, acc

        init = (jnp.full((blk, 1), NEG_BIG, F32), jnp.zeros((blk, 1), F32), jnp.zeros((blk, hd), F32))
        m, l, acc = lax.fori_loop(0, i + 1, kv_step, init)
        o_ref[0] = acc / l
        lse = m + jnp.log(l)
        lse_ref[0, 0] = jnp.sum(jnp.where(r == c, lse, 0.0), axis=0, keepdims=True)

    return _call(
        body, name="fox_fwd", grid=(h_total, nb),
        in_specs=[pl.BlockSpec((1, blk, hd), lambda h, i: (h, i, 0)),
                  pl.BlockSpec((1, t_total, hd), lambda h, i: (h, 0, 0)),
                  pl.BlockSpec((1, t_total, hd), lambda h, i: (h, 0, 0)),
                  pl.BlockSpec((1, blk, 1), lambda h, i: (h, i, 0)),
                  pl.BlockSpec((1, nb, 1, blk), lambda h, i: (h, 0, 0, 0))],
        out_specs=[pl.BlockSpec((1, blk, hd), lambda h, i: (h, i, 0)),
                   pl.BlockSpec((1, 1, 1, blk), lambda h, i: (h, i, 0, 0))],
        out_shape=[jax.ShapeDtypeStruct(q.shape, F32), jax.ShapeDtypeStruct(f_row.shape, F32)],
        compiler_params=_cparams(("parallel", "parallel")),
    )(q, k, v, f_col, f_row)


def _attn_bwd_call(q, k, v, f_col, f_row, lse_row, delta_row, do):
    h_total, t_total, hd = q.shape
    blk = f_row.shape[-1]
    nb = t_total // blk

    def body(q_ref, do_ref, k_ref, v_ref, fc_ref, fr_ref, lse_ref, dl_ref, dq_ref, dk_ref, dv_ref, dfk_ref, dfq_ref):
        j = pl.program_id(1)

        @pl.when(j == 0)
        def _():
            dq_ref[...] = jnp.zeros_like(dq_ref)
            dfq_ref[...] = jnp.zeros_like(dfq_ref)

        kj = k_ref[0].astype(BF16)
        vj = v_ref[0].astype(BF16)
        fk = fc_ref[0]
        r = lax.broadcasted_iota(jnp.int32, (blk, blk), 0)
        c = lax.broadcasted_iota(jnp.int32, (blk, blk), 1)

        def q_step(i, carry):
            dk, dv, dfk = carry
            off = pl.multiple_of(i * blk, blk)
            qi = q_ref[0, pl.ds(off, blk), :]
            doi = do_ref[0, pl.ds(off, blk), :]
            st = _raw_dot(kj, qi, 1, 1, False) + fr_ref[0, i] - fk
            pt = jnp.where(r + j * blk <= c + i * blk, jnp.exp(st - lse_ref[0, i]), 0.0)
            dv = dv + _raw_dot(pt, doi, 1, 0, False)
            dpt = _raw_dot(vj, doi, 1, 1, False)
            dst = pt * (dpt - dl_ref[0, i])
            dk = dk + _raw_dot(dst, qi, 1, 0, False)
            dfk = dfk - jnp.sum(dst, axis=1, keepdims=True)
            dfq_ref[0, i] += jnp.sum(dst, axis=0, keepdims=True)
            dq_ref[0, pl.ds(off, blk), :] += _raw_dot(dst, kj, 0, 0, False)
            return dk, dv, dfk

        init = (jnp.zeros((blk, hd), F32), jnp.zeros((blk, hd), F32), jnp.zeros((blk, 1), F32))
        dk, dv, dfk = lax.fori_loop(j, nb, q_step, init)
        dk_ref[0] = dk
        dv_ref[0] = dv
        dfk_ref[0] = dfk

    full = pl.BlockSpec((1, t_total, hd), lambda h, j: (h, 0, 0))
    tile = pl.BlockSpec((1, blk, hd), lambda h, j: (h, j, 0))
    col = pl.BlockSpec((1, blk, 1), lambda h, j: (h, j, 0))
    rows = pl.BlockSpec((1, nb, 1, blk), lambda h, j: (h, 0, 0, 0))
    return _call(
        body, name="fox_bwd", grid=(h_total, nb),
        in_specs=[full, full, tile, tile, col, rows, rows, rows],
        out_specs=[full, tile, tile, col, rows],
        out_shape=[jax.ShapeDtypeStruct(q.shape, F32), jax.ShapeDtypeStruct(q.shape, F32),
                   jax.ShapeDtypeStruct(q.shape, F32), jax.ShapeDtypeStruct(f_col.shape, F32),
                   jax.ShapeDtypeStruct(f_row.shape, F32)],
        compiler_params=_cparams(("parallel", "arbitrary")),
    )(q.astype(BF16), do.astype(BF16), k, v, f_col, f_row, lse_row, delta_row)


@jax.custom_vjp
def fox_attention(q, k, v, f_col, f_row):
    return _attn_fwd_call(q, k, v, f_col, f_row)[0]


def _fox_fwd(q, k, v, f_col, f_row):
    o, lse_row = _attn_fwd_call(q, k, v, f_col, f_row)
    return o, (q, k, v, f_col, f_row, o, lse_row)


def _fox_bwd(res, do):
    q, k, v, f_col, f_row, o, lse_row = res
    h_total, t_total, hd = q.shape
    tm = _tile(h_total * t_total, 2048, 8)
    delta = _rows_fwd_call(_f_delta, (do.reshape(-1, hd), o.reshape(-1, hd)), (), (1,), tm, 1, (), "fox_delta")[0]
    return tuple(_attn_bwd_call(q, k, v, f_col, f_row, lse_row, delta.reshape(f_row.shape), do))


fox_attention.defvjp(_fox_fwd, _fox_bwd)


def _gdn_chunk(s, q, k, v, b, g):
    n = q.shape[0]
    r = lax.broadcasted_iota(jnp.int32, (n, n), 0)
    c = lax.broadcasted_iota(jnp.int32, (n, n), 1)
    incl = r >= c
    g_row = jnp.sum(jnp.where(r == c, g, 0.0), axis=0, keepdims=True)
    big_g = jnp.sum(jnp.where(incl, g_row, 0.0), axis=1, keepdims=True)
    big_g_row = jnp.sum(jnp.where(r <= c, g, 0.0), axis=0, keepdims=True)
    g_last = jnp.sum(g, axis=0, keepdims=True)
    dec = jnp.where(incl, jnp.exp(jnp.where(incl, big_g - big_g_row, 0.0)), 0.0)
    dec_strict = jnp.where(r > c, dec, 0.0)
    e_g = jnp.exp(big_g)
    kb = k * b
    m = _dot(kb, k, 1, 1) * dec_strict
    aqk = _dot(q, k, 1, 1) * dec
    u = v * b
    w = kb * e_g
    u = u - _dot(m, u, 1, 0, True)
    w = w - _dot(m, w, 1, 0, True)
    p = m
    steps = 1
    while 2 * steps < n:
        p = _dot(p, p, 1, 0, True)
        u = u + _dot(p, u, 1, 0, True)
        w = w + _dot(p, w, 1, 0, True)
        steps *= 2
    big_u = u - _dot(w, s)
    o = _dot(q * e_g, s) + _dot(aqk, big_u)
    kd = k * jnp.exp(g_last - big_g)
    s_next = s * jnp.exp(g_last) + _dot(kd, big_u, 0, 0)
    return o, s_next


def _gdn_specs(n_chunks, rev):
    pos = (lambda n: n_chunks - 1 - n) if rev else (lambda n: n)
    tok = pl.BlockSpec((GDN_CHUNK, GDN_HD), lambda h, n: (pos(n), h))
    gate = pl.BlockSpec((1, GDN_CHUNK, 1), lambda h, n: (h, pos(n), 0))
    state = pl.BlockSpec((1, 1, GDN_HD, GDN_HD), lambda h, n: (h, pos(n), 0, 0))
    return tok, gate, state


def _gdn_fwd_call(q, k, v, b, g):
    t_total = q.shape[0]
    n_chunks = t_total // GDN_CHUNK
    tok, gate, state = _gdn_specs(n_chunks, False)

    def body(q_ref, k_ref, v_ref, b_ref, g_ref, o_ref, s_all_ref, s_ref):
        @pl.when(pl.program_id(1) == 0)
        def _():
            s_ref[...] = jnp.zeros_like(s_ref)

        s = s_ref[...]
        s_all_ref[0, 0] = s
        o, s_next = _gdn_chunk(s, q_ref[...], k_ref[...], v_ref[...], b_ref[0], g_ref[0])
        o_ref[...] = o
        s_ref[...] = s_next

    return _call(
        body, name="gdn_fwd", grid=(GDN_HEADS, n_chunks), in_specs=[tok, tok, tok, gate, gate],
        out_specs=[tok, state],
        out_shape=[jax.ShapeDtypeStruct(q.shape, F32),
                   jax.ShapeDtypeStruct((GDN_HEADS, n_chunks, GDN_HD, GDN_HD), F32)],
        scratch_shapes=[pltpu.VMEM((GDN_HD, GDN_HD), F32)],
        compiler_params=_cparams(("parallel", "arbitrary")),
    )(q, k, v, b, g)


def _gdn_bwd_call(q, k, v, b, g, s_all, do):
    t_total = q.shape[0]
    n_chunks = t_total // GDN_CHUNK
    tok, gate, state = _gdn_specs(n_chunks, True)

    def body(q_ref, k_ref, v_ref, b_ref, g_ref, s_all_ref, do_ref, dq_ref, dk_ref, dv_ref, db_ref, dg_ref, ds_ref):
        @pl.when(pl.program_id(1) == 0)
        def _():
            ds_ref[...] = jnp.zeros_like(ds_ref)

        _, vjp = jax.vjp(_gdn_chunk, s_all_ref[0, 0], q_ref[...], k_ref[...], v_ref[...], b_ref[0], g_ref[0])
        ds, dq, dk, dv, db, dg = vjp((do_ref[...], ds_ref[...]))
        ds_ref[...] = ds
        dq_ref[...] = dq
        dk_ref[...] = dk
        dv_ref[...] = dv
        db_ref[0] = db
        dg_ref[0] = dg

    return _call(
        body, name="gdn_bwd", grid=(GDN_HEADS, n_chunks), in_specs=[tok, tok, tok, gate, gate, state, tok],
        out_specs=[tok, tok, tok, gate, gate],
        out_shape=[jax.ShapeDtypeStruct(q.shape, F32)] * 3 + [jax.ShapeDtypeStruct(b.shape, F32)] * 2,
        scratch_shapes=[pltpu.VMEM((GDN_HD, GDN_HD), F32)],
        compiler_params=_cparams(("parallel", "arbitrary")),
    )(q, k, v, b, g, s_all, do)


@jax.custom_vjp
def gated_delta(q, k, v, b, g):
    return _gdn_fwd_call(q, k, v, b, g)[0]


def _gd_fwd(q, k, v, b, g):
    o, s_all = _gdn_fwd_call(q, k, v, b, g)
    return o, (q, k, v, b, g, s_all)


def _gd_bwd(res, do):
    return tuple(_gdn_bwd_call(*res, do))


gated_delta.defvjp(_gd_fwd, _gd_bwd)


def _loss_call(y, tgt, first, last):
    r_total, d = y.shape
    tm = _tile(r_total, 256, 8)

    def body(y_ref, t_ref, loss_ref, dy_ref):
        i = pl.program_id(0)

        @pl.when(i == 0)
        def _():
            loss_ref[...] = jnp.zeros_like(loss_ref)

        row = lax.broadcasted_iota(jnp.int32, (tm, d), 0) + i * tm
        err = jnp.where((row >= first) & (row < last), y_ref[...] - t_ref[...], 0.0)
        dy_ref[...] = err * (1.0 / d)
        part = jnp.sum(jnp.sum(err * err, axis=1, keepdims=True), axis=0, keepdims=True) * (0.5 / d)
        loss_ref[...] += jnp.broadcast_to(part, loss_ref.shape)

    return _call(
        body, name="loss_head", grid=(r_total // tm,),
        in_specs=[pl.BlockSpec((tm, d), lambda i: (i, 0))] * 2,
        out_specs=[pl.BlockSpec((8, LANE), lambda i: (0, 0)), pl.BlockSpec((tm, d), lambda i: (i, 0))],
        out_shape=[jax.ShapeDtypeStruct((8, LANE), F32), jax.ShapeDtypeStruct(y.shape, F32)],
        compiler_params=_cparams(("arbitrary",)),
    )(y, tgt)


def make_loss(first, last):
    @jax.custom_vjp
    def op(y, tgt):
        return _loss_call(y, tgt, first, last)[0][0, 0]

    def fwd(y, tgt):
        loss, dy = _loss_call(y, tgt, first, last)
        return loss[0, 0], (dy,)

    def bwd(res, ct):
        return res[0] * ct, jnp.zeros_like(res[0])

    op.defvjp(fwd, bwd)
    return op


def _pad_rows8(w):
    return jnp.concatenate([w, jnp.zeros((8 - w.shape[0], w.shape[1]), w.dtype)], axis=0)


def local_loss(wts, x, tgt):
    seq = x.shape[0]
    n_tok = N_META + seq
    t_pad = -(-n_tok // ROW_ALIGN) * ROW_ALIGN
    depth = wts["norm1_g"].shape[0]
    blk = _tile(t_pad, ATT_BLK, LANE)
    nb = t_pad // blk
    tm = _tile(t_pad, 256, 8)
    tm_wide = _tile(t_pad, 128, 8)

    rms = rowop(_f_rmsnorm, "rmsnorm", (D_MODEL,), tm)
    qnorm = rowop(_f_qnorm, "fox_q_norm", (FOX_HD,), _tile(FOX_HEADS * t_pad, 2048, 8))
    knorm = rowop(_f_rmsnorm, "fox_k_norm", (FOX_HD,), _tile(FOX_HEADS * t_pad, 2048, 8))
    logsig = rowop(_f_logsig, "fox_log_forget", (FOX_HEADS,), tm)
    gdn_q = rowop(_f_gdn_q, "gdn_q_act", (GDN_HD,), tm, ncb=GDN_HEADS)
    gdn_k = rowop(_f_gdn_k, "gdn_k_act", (GDN_HD,), tm, ncb=GDN_HEADS)
    silu = rowop(_f_silu, "gdn_v_act", (GDN_W,), tm)
    gates = rowop(_f_gdn_gates, "gdn_gates", (GDN_HEADS, GDN_HEADS), tm)
    gdn_out = rowop(_f_gdn_out, "gdn_out_norm", (GDN_HD,), tm, ncb=GDN_HEADS)
    merge = rowop(_f_merge, "branch_merge", (D_MODEL,), tm)
    glu = rowop(_f_glu, "ffn_glu", (D_FF // 2,), tm_wide, ncb=2)
    conv4 = make_dwconv(GDN_CONV)
    conv3 = make_dwconv(FFN_CONV)
    loss_op = make_loss(N_META, n_tok)

    zeros = jnp.zeros((t_pad - n_tok, D_MODEL), F32)
    h_res = jnp.concatenate([wts["meta_tokens"], x, zeros], axis=0)
    tgt_rows = jnp.concatenate([jnp.zeros((N_META, D_MODEL), F32), tgt, zeros], axis=0)

    def heads(a):
        return a.reshape(t_pad, FOX_HEADS, FOX_HD).transpose(1, 0, 2).reshape(FOX_HEADS * t_pad, FOX_HD)

    for l in range(depth):
        h = rms((h_res,), (wts["norm1_g"][l][None],))[0]
        proj = mm(h, wts["w_in"][l])
        qn = qnorm((heads(proj[:, 0:512]),), (wts["fox_q_norm_g"][l][None],))[0]
        kn = knorm((heads(proj[:, 512:1024]),), (wts["fox_k_norm_g"][l][None],))[0]
        vh = heads(proj[:, 1024:1536])
        log_f = logsig((proj[:, 1536:1544],), (wts["fox_f_bias"][l][None],))[0]
        f_cum = cumsum_lanes(log_f.T)
        o_a = fox_attention(qn.reshape(FOX_HEADS, t_pad, FOX_HD), kn.reshape(FOX_HEADS, t_pad, FOX_HD),
                            vh.reshape(FOX_HEADS, t_pad, FOX_HD), f_cum[:, :, None],
                            f_cum.reshape(FOX_HEADS, nb, 1, blk))
        y_a = mm(o_a.transpose(1, 0, 2).reshape(t_pad, FOX_W), wts["w_branch_a"][l])
        cv = conv4(proj[:, 1664:4736], _pad_rows8(wts["gdn_conv_w"][l]))
        gq = gdn_q((cv[:, 0:GDN_W],), ())[0]
        gk = gdn_k((cv[:, GDN_W:2 * GDN_W],), ())[0]
        gv = silu((cv[:, 2 * GDN_W:],), ())[0]
        beta, gdec = gates((proj[:, 4736:4744], proj[:, 4744:4752]),
                           (wts["gdn_a_log"][l][None], wts["gdn_dt_bias"][l][None]))
        o_b = gated_delta(gq, gk, gv, beta.T[:, :, None], gdec.T[:, :, None])
        o_b = gdn_out((o_b, proj[:, 4864:5888]), (wts["gdn_norm_g"][l][None],))[0]
        y_b = mm(o_b, wts["w_branch_b"][l])
        mixed = merge((proj[:, 5888:6912], proj[:, 6912:7936], y_a, y_b), ())[0]
        h_res = h_res + mm(mixed, wts["w_out"][l])
        h = rms((h_res,), (wts["norm2_g"][l][None],))[0]
        up = conv3(mm(h, wts["w_up"][l]), _pad_rows8(wts["ffn_conv_w"][l]))
        act = glu((up[:, :D_FF], up[:, D_FF:]), ())[0]
        h_res = h_res + mm(act, wts["w_down"][l])
    return loss_op(h_res, tgt_rows)


def pad_w_in(w):
    parts, pos = [], 0
    for src, width, dst in IN_SEGS:
        if dst > pos:
            parts.append(jnp.zeros(w.shape[:-1] + (dst - pos,), w.dtype))
        parts.append(w[..., src:src + width])
        pos = dst + width
    parts.append(jnp.zeros(w.shape[:-1] + (D_IN_PAD - pos,), w.dtype))
    return jnp.concatenate(parts, axis=-1)


def unpad_w_in(w):
    return jnp.concatenate([w[..., dst:dst + width] for _, width, dst in IN_SEGS], axis=-1)


ANY = pl.BlockSpec(memory_space=pl.ANY)
N_CHIPS = 4
N_DEV = 8
COMM_COLS = 1024
COMM_ROW_ALIGN = 512
COMM_ROW_ALIGN_SMALL = 32


def _place():
    return lax.axis_index("x"), lax.axis_index("y"), lax.axis_index("c")


def _other_chips(x, y):
    return [(1 - x, y), (x, 1 - y), (1 - x, 1 - y)]


def _remote(src, dst, send_sem, recv_sem, dev):
    return pltpu.make_async_remote_copy(src_ref=src, dst_ref=dst, send_sem=send_sem, recv_sem=recv_sem,
                                        device_id=dev, device_id_type=MESH)


def chip_all_gather(buf):
    rows, cols = buf.shape
    half = rows // 2

    def body(x_ref, out_ref, send_sems, recv_sems, pass_send, pass_recv, local_sem):
        x, y, c = _place()
        me = 2 * x + y
        mine, other = pl.ds(c * half, half), pl.ds((1 - c) * half, half)
        sibling = (x, y, 1 - c)
        chips = _other_chips(x, y)
        local = pltpu.make_async_copy(x_ref, out_ref.at[me], local_sem)
        local.start()
        started = []
        for k, (px, py) in enumerate(chips):
            cp = _remote(x_ref.at[mine], out_ref.at[me, mine], send_sems.at[k], recv_sems.at[k], (px, py, c))
            cp.start()
            started.append(cp)
        for k, (px, py) in enumerate(chips):
            landed = out_ref.at[2 * px + py, mine]
            _remote(landed, landed, send_sems.at[k], recv_sems.at[k], (px, py, c)).wait_recv()
            cp = _remote(landed, landed, pass_send.at[k], pass_recv.at[k], sibling)
            cp.start()
            started.append(cp)
        for k, (px, py) in enumerate(chips):
            passed = out_ref.at[2 * px + py, other]
            _remote(passed, passed, pass_send.at[k], pass_recv.at[k], sibling).wait_recv()
        for cp in started:
            cp.wait_send()
        local.wait()

    return _call(
        body, name="chip_all_gather", in_specs=[ANY], out_specs=ANY,
        out_shape=jax.ShapeDtypeStruct((N_CHIPS, rows, cols), buf.dtype),
        scratch_shapes=[pltpu.SemaphoreType.DMA((3,)), pltpu.SemaphoreType.DMA((3,)),
                        pltpu.SemaphoreType.DMA((3,)), pltpu.SemaphoreType.DMA((3,)), pltpu.SemaphoreType.DMA],
    )(buf)


def sibling_swap_halves(g4):
    n, rows, cols = g4.shape
    half = rows // 2

    def body(g_ref, got_ref, send_sem, recv_sem):
        x, y, c = _place()
        cp = _remote(g_ref.at[:, pl.ds((1 - c) * half, half), :], got_ref, send_sem, recv_sem, (x, y, 1 - c))
        cp.start()
        cp.wait()

    return _call(
        body, name="sibling_swap_halves", in_specs=[ANY], out_specs=ANY,
        out_shape=jax.ShapeDtypeStruct((n, half, cols), g4.dtype),
        scratch_shapes=[pltpu.SemaphoreType.DMA, pltpu.SemaphoreType.DMA],
    )(g4)


def add_own_half(g4, got, c):
    n, rows, cols = g4.shape
    half = rows // 2
    tm = _tile(half, 128, 8)
    nt = half // tm

    def body(c_ref, a_ref, b_ref, o_ref):
        o_ref[...] = a_ref[...] + b_ref[...]

    return _call(
        body, name="add_own_half",
        grid_spec=pltpu.PrefetchScalarGridSpec(
            num_scalar_prefetch=1, grid=(n, nt),
            in_specs=[pl.BlockSpec((1, tm, cols), lambda j, i, c_ref: (j, c_ref[0] * nt + i, 0)),
                      pl.BlockSpec((1, tm, cols), lambda j, i, c_ref: (j, i, 0))],
            out_specs=pl.BlockSpec((1, tm, cols), lambda j, i, c_ref: (j, i, 0))),
        out_shape=jax.ShapeDtypeStruct(got.shape, F32),
        compiler_params=_cparams(("parallel", "parallel")),
    )(c.reshape(1).astype(jnp.int32), g4, got)


def chip_scatter(p4):
    n, rows, cols = p4.shape

    def body(p_ref, out_ref, send_sems, recv_sems, local_sem):
        x, y, c = _place()
        me = 2 * x + y
        chips = _other_chips(x, y)
        local = pltpu.make_async_copy(p_ref.at[me], out_ref.at[me], local_sem)
        local.start()
        started = []
        for k, (px, py) in enumerate(chips):
            cp = _remote(p_ref.at[2 * px + py], out_ref.at[me], send_sems.at[k], recv_sems.at[k], (px, py, c))
            cp.start()
            started.append(cp)
        for k, (px, py) in enumerate(chips):
            landed = out_ref.at[2 * px + py]
            _remote(landed, landed, send_sems.at[k], recv_sems.at[k], (px, py, c)).wait_recv()
        for cp in started:
            cp.wait_send()
        local.wait()

    return _call(
        body, name="chip_scatter", in_specs=[ANY], out_specs=ANY,
        out_shape=jax.ShapeDtypeStruct(p4.shape, p4.dtype),
        scratch_shapes=[pltpu.SemaphoreType.DMA((3,)), pltpu.SemaphoreType.DMA((3,)), pltpu.SemaphoreType.DMA],
    )(p4)


def sum_slots(a):
    n, rows, cols = a.shape
    tm = _tile(rows, 128, 8)

    def body(a_ref, o_ref):
        acc = a_ref[0]
        for k in range(1, n):
            acc = acc + a_ref[k]
        o_ref[...] = acc

    return _call(
        body, name="sum_slots_%d" % n, grid=(rows // tm,),
        in_specs=[pl.BlockSpec((n, tm, cols), lambda i: (0, i, 0))],
        out_specs=pl.BlockSpec((tm, cols), lambda i: (i, 0)),
        out_shape=jax.ShapeDtypeStruct((rows, cols), F32),
        compiler_params=_cparams(("parallel",)),
    )(a)


def sibling_join(s):
    half, cols = s.shape

    def body(s_ref, out_ref, send_sem, recv_sem, local_sem):
        x, y, c = _place()
        mine, other = pl.ds(c * half, half), pl.ds((1 - c) * half, half)
        local = pltpu.make_async_copy(s_ref, out_ref.at[mine], local_sem)
        local.start()
        cp = _remote(s_ref, out_ref.at[mine], send_sem, recv_sem, (x, y, 1 - c))
        cp.start()
        _remote(s_ref, out_ref.at[other], send_sem, recv_sem, (x, y, 1 - c)).wait_recv()
        cp.wait_send()
        local.wait()

    return _call(
        body, name="sibling_join", in_specs=[ANY], out_specs=ANY,
        out_shape=jax.ShapeDtypeStruct((2 * half, cols), s.dtype),
        scratch_shapes=[pltpu.SemaphoreType.DMA, pltpu.SemaphoreType.DMA, pltpu.SemaphoreType.DMA],
    )(s)


def all_devices_gather(buf):
    rows, cols = buf.shape

    def body(b_ref, out_ref, send_sems, recv_sems, local_sem):
        x, y, c = _place()
        me = 4 * x + 2 * y + c
        local = pltpu.make_async_copy(b_ref, out_ref.at[me], local_sem)
        local.start()
        peers = [((x + dx) % 2, (y + dy) % 2, (c + dc) % 2)
                 for dx in (0, 1) for dy in (0, 1) for dc in (0, 1) if dx + dy + dc > 0]
        started = []
        for k, peer in enumerate(peers):
            cp = _remote(b_ref, out_ref.at[me], send_sems.at[k], recv_sems.at[k], peer)
            cp.start()
            started.append(cp)
        for k, (px, py, pc) in enumerate(peers):
            landed = out_ref.at[4 * px + 2 * py + pc]
            _remote(landed, landed, send_sems.at[k], recv_sems.at[k], (px, py, pc)).wait_recv()
        for cp in started:
            cp.wait_send()
        local.wait()

    return _call(
        body, name="all_devices_gather", in_specs=[ANY], out_specs=ANY,
        out_shape=jax.ShapeDtypeStruct((N_DEV, rows, cols), buf.dtype),
        scratch_shapes=[pltpu.SemaphoreType.DMA((7,)), pltpu.SemaphoreType.DMA((7,)), pltpu.SemaphoreType.DMA],
    )(buf)


def adamw(w, g, m, v):
    shape = w.shape
    w2, g2, m2, v2 = [a.reshape(-1, shape[-1]) for a in (w, g, m, v)]
    rows, cols = w2.shape
    tm = _tile(rows, 256, 8) if rows % 8 == 0 else rows

    def body(w_ref, g_ref, m_ref, v_ref, d_ref, nm_ref, nv_ref):
        gv = g_ref[...]
        nm = ADAM_B1 * m_ref[...] + (1.0 - ADAM_B1) * gv
        nv = ADAM_B2 * v_ref[...] + (1.0 - ADAM_B2) * (gv * gv)
        m_hat = nm / (1.0 - ADAM_B1 ** ADAM_STEP)
        v_hat = nv / (1.0 - ADAM_B2 ** ADAM_STEP)
        d_ref[...] = -ADAM_LR * (m_hat / (jnp.sqrt(v_hat) + ADAM_EPS) + ADAM_WD * w_ref[...])
        nm_ref[...] = nm
        nv_ref[...] = nv

    spec = pl.BlockSpec((tm, cols), lambda i: (i, 0))
    outs = _call(
        body, name="adamw", grid=(rows // tm,), in_specs=[spec] * 4, out_specs=[spec] * 3,
        out_shape=[jax.ShapeDtypeStruct((rows, cols), F32)] * 3,
        compiler_params=_cparams(("parallel",)),
    )(w2, g2, m2, v2)
    return [o.reshape(shape) for o in outs]


WEIGHTS = ("meta_tokens", "norm1_g", "w_in", "fox_f_bias", "fox_q_norm_g", "fox_k_norm_g", "gdn_conv_w",
           "gdn_a_log", "gdn_dt_bias", "gdn_norm_g", "w_branch_a", "w_branch_b", "w_out", "norm2_g", "w_up",
           "ffn_conv_w", "w_down")
SHARD_AXIS = {"meta_tokens": -1, "w_in": -1, "gdn_conv_w": -1, "w_branch_a": -1, "w_branch_b": -2, "w_out": -2,
              "w_up": -1, "ffn_conv_w": -1, "w_down": -2}
MATMUL_WEIGHTS = ("w_in", "w_branch_a", "w_branch_b", "w_out", "w_up", "w_down")
SMALL_SHARDED = ("meta_tokens", "gdn_conv_w", "ffn_conv_w")
REPLICATED = tuple(n for n in WEIGHTS if n not in SHARD_AXIS)


def _pack(arrays, dtype, row_align):
    flat = jnp.concatenate([a.reshape(-1).astype(dtype) for a in arrays])
    block = row_align * COMM_COLS
    total = -(-flat.shape[0] // block) * block
    flat = jnp.concatenate([flat, jnp.zeros((total - flat.shape[0],), dtype)])
    return flat.reshape(-1, COMM_COLS)


def _unpack(buf, shapes):
    flat, out, pos = buf.reshape(-1), [], 0
    for shape in shapes:
        size = 1
        for d in shape:
            size *= d
        out.append(flat[pos:pos + size].reshape(shape))
        pos += size
    return out


def _gather_full(shards, names, dtype, row_align):
    got = chip_all_gather(_pack([shards[n] for n in names], dtype, row_align))
    per_chip = [_unpack(got[j], [shards[n].shape for n in names]) for j in range(N_CHIPS)]
    return {n: jnp.concatenate([per_chip[j][i] for j in range(N_CHIPS)], axis=SHARD_AXIS[n]).astype(F32)
            for i, n in enumerate(names)}


def _shard_of(full, name, j):
    axis = SHARD_AXIS[name] % full.ndim
    size = full.shape[axis] // N_CHIPS
    return lax.slice_in_dim(full, j * size, (j + 1) * size, axis=axis)


def kernel(x, meta_tokens, norm1_g, w_in, fox_f_bias, fox_q_norm_g, fox_k_norm_g, gdn_conv_w, gdn_a_log, gdn_dt_bias, gdn_norm_g, w_branch_a, w_branch_b, w_out, norm2_g, w_up, ffn_conv_w, w_down, loss_target, m_meta_tokens, m_norm1_g, m_w_in, m_fox_f_bias, m_fox_q_norm_g, m_fox_k_norm_g, m_gdn_conv_w, m_gdn_a_log, m_gdn_dt_bias, m_gdn_norm_g, m_w_branch_a, m_w_branch_b, m_w_out, m_norm2_g, m_w_up, m_ffn_conv_w, m_w_down, v_meta_tokens, v_norm1_g, v_w_in, v_fox_f_bias, v_fox_q_norm_g, v_fox_k_norm_g, v_gdn_conv_w, v_gdn_a_log, v_gdn_dt_bias, v_gdn_norm_g, v_w_branch_a, v_w_branch_b, v_w_out, v_norm2_g, v_w_up, v_ffn_conv_w, v_w_down):
    w_loc = dict(zip(WEIGHTS, (meta_tokens, norm1_g, w_in, fox_f_bias, fox_q_norm_g, fox_k_norm_g, gdn_conv_w,
                               gdn_a_log, gdn_dt_bias, gdn_norm_g, w_branch_a, w_branch_b, w_out, norm2_g, w_up,
                               ffn_conv_w, w_down)))
    m_loc = dict(zip(WEIGHTS, (m_meta_tokens, m_norm1_g, m_w_in, m_fox_f_bias, m_fox_q_norm_g, m_fox_k_norm_g,
                               m_gdn_conv_w, m_gdn_a_log, m_gdn_dt_bias, m_gdn_norm_g, m_w_branch_a, m_w_branch_b,
                               m_w_out, m_norm2_g, m_w_up, m_ffn_conv_w, m_w_down)))
    v_loc = dict(zip(WEIGHTS, (v_meta_tokens, v_norm1_g, v_w_in, v_fox_f_bias, v_fox_q_norm_g, v_fox_k_norm_g,
                               v_gdn_conv_w, v_gdn_a_log, v_gdn_dt_bias, v_gdn_norm_g, v_w_branch_a, v_w_branch_b,
                               v_w_out, v_norm2_g, v_w_up, v_ffn_conv_w, v_w_down)))
    c = lax.axis_index("c")

    full = {n: w_loc[n] for n in REPLICATED}
    full.update(_gather_full(w_loc, MATMUL_WEIGHTS, BF16, COMM_ROW_ALIGN))
    full.update(_gather_full(w_loc, SMALL_SHARDED, F32, COMM_ROW_ALIGN_SMALL))
    full["w_in"] = pad_w_in(full["w_in"])

    loss, (g_full, g_x) = jax.value_and_grad(local_loss, argnums=(0, 1))(full, x[0], loss_target[0])
    g_full = dict(g_full)
    g_full["w_in"] = unpad_w_in(g_full["w_in"])

    sharded = MATMUL_WEIGHTS + SMALL_SHARDED
    g4 = jnp.stack([_pack([_shard_of(g_full[n], n, j) for n in sharded], F32, COMM_ROW_ALIGN)
                    for j in range(N_CHIPS)])
    pair_sum = add_own_half(g4, sibling_swap_halves(g4), c)
    g_shard = sibling_join(sum_slots(chip_scatter(pair_sum)))
    grads = dict(zip(sharded, _unpack(g_shard, [w_loc[n].shape for n in sharded])))
    g_rep = sum_slots(all_devices_gather(_pack([g_full[n] for n in REPLICATED], F32, 8)))
    grads.update(zip(REPLICATED, _unpack(g_rep, [w_loc[n].shape for n in REPLICATED])))

    loss = lax.psum(loss, ("x", "y", "c"))
    upd = {n: adamw(w_loc[n], grads[n], m_loc[n], v_loc[n]) for n in WEIGHTS}
    return (loss, g_x[None], *[grads[n] for n in WEIGHTS], *[upd[n][0] for n in WEIGHTS],
            *[upd[n][1] for n in WEIGHTS], *[upd[n][2] for n in WEIGHTS])
```

```python
import functools

import jax
import jax.numpy as jnp
from jax import lax
from jax.experimental import pallas as pl
from jax.experimental.pallas import tpu as pltpu

F32 = jnp.float32
BF16 = jnp.bfloat16
HI = lax.Precision.HIGHEST
MESH = pl.DeviceIdType.MESH

D_MODEL = 1024
N_META = 16
EPS = 1e-6
FOX_HEADS, FOX_HD = 8, 64
FOX_W = FOX_HEADS * FOX_HD
GDN_HEADS, GDN_HD, GDN_CHUNK, GDN_CONV = 8, 128, 64, 4
GDN_W = GDN_HEADS * GDN_HD
D_FF = 2816
FFN_CONV = 3
D_IN = 7704
D_IN_PAD = 8192
IN_SEGS = ((0, 1536, 0), (1536, 8, 1536), (1544, 3072, 1664), (4616, 16, 4736), (4632, 1024, 4864), (5656, 2048, 5888))
ROW_ALIGN = 256
ATT_BLK = 256
VMEM_LIMIT = 48 * 1024 * 1024
LANE = 128

ADAM_LR, ADAM_B1, ADAM_B2, ADAM_EPS, ADAM_WD, ADAM_STEP = 0.001, 0.9, 0.999, 1e-08, 0.01, 10


def _call(body, **kw):
    return pl.pallas_call(body, **kw)


def _tile(n, target, mult):
    best, t = None, mult
    while t <= min(n, target):
        if n % t == 0:
            best = t
        t += mult
    assert best is not None, (n, target, mult)
    return best


def _cparams(sem):
    return pltpu.CompilerParams(dimension_semantics=sem, vmem_limit_bytes=VMEM_LIMIT)


def _raw_dot(a, b, ca, cb, precise):
    dims = (((ca,), (cb,)), ((), ()))
    a_hi, b_hi = a.astype(BF16), b.astype(BF16)
    out = lax.dot_general(a_hi, b_hi, dims, preferred_element_type=F32)
    if precise:
        a_lo = (a - a_hi.astype(F32)).astype(BF16)
        b_lo = (b - b_hi.astype(F32)).astype(BF16)
        out = out + (lax.dot_general(a_hi, b_lo, dims, preferred_element_type=F32)
                     + lax.dot_general(a_lo, b_hi, dims, preferred_element_type=F32))
    return out


def _make_dot(ca, cb, precise):
    @jax.custom_vjp
    def f(a, b):
        return _raw_dot(a, b, ca, cb, precise)

    def fwd(a, b):
        return f(a, b), (a, b)

    def bwd(res, ct):
        a, b = res
        if ca == 1:
            da = _raw_dot(ct, b, 1, 1 if cb == 0 else 0, precise)
        else:
            da = _raw_dot(b, ct, 1 if cb == 0 else 0, 1, precise)
        if cb == 0:
            db = _raw_dot(a, ct, 0 if ca == 1 else 1, 0, precise)
        else:
            db = _raw_dot(ct, a, 0, 0 if ca == 1 else 1, precise)
        return da, db

    f.defvjp(fwd, bwd)
    return f


_DOTS = {(ca, cb, p): _make_dot(ca, cb, p) for ca in (0, 1) for cb in (0, 1) for p in (False, True)}


def _dot(a, b, ca=1, cb=0, precise=False):
    return _DOTS[(ca, cb, precise)](a, b)


def _mm_call(a, b, name):
    m, k = a.shape
    n = b.shape[1]
    assert k == b.shape[0], (a.shape, b.shape)
    tm = _tile(m, 768, 16)
    tn = _tile(n, 1408, LANE)
    tk = _tile(k, 1408, LANE)
    nk = k // tk

    def body(a_ref, b_ref, o_ref, *scratch):
        part = jnp.dot(a_ref[...], b_ref[...], preferred_element_type=F32)
        if nk == 1:
            o_ref[...] = part
            return
        acc_ref = scratch[0]
        kk = pl.program_id(2)

        @pl.when(kk == 0)
        def _():
            acc_ref[...] = part

        @pl.when(kk > 0)
        def _():
            acc_ref[...] += part

        @pl.when(kk == nk - 1)
        def _():
            o_ref[...] = acc_ref[...]

    return _call(
        body, name=name, grid=(m // tm, n // tn, nk),
        in_specs=[pl.BlockSpec((tm, tk), lambda i, j, kk: (i, kk)), pl.BlockSpec((tk, tn), lambda i, j, kk: (kk, j))],
        out_specs=pl.BlockSpec((tm, tn), lambda i, j, kk: (i, j)),
        out_shape=jax.ShapeDtypeStruct((m, n), F32),
        scratch_shapes=[pltpu.VMEM((tm, tn), F32)] if nk > 1 else [],
        compiler_params=_cparams(("parallel", "parallel", "arbitrary")),
    )(a, b)


@jax.custom_vjp
def mm(a, w):
    return _mm_call(a.astype(BF16), w.astype(BF16), "mm_fwd")


def _mm_fwd(a, w):
    a_b, w_b = a.astype(BF16), w.astype(BF16)
    return _mm_call(a_b, w_b, "mm_fwd"), (a_b, w_b)


def _mm_bwd(res, ct):
    a_b, w_b = res
    ct_b = ct.astype(BF16)
    return _mm_call(ct_b, w_b.T, "mm_dx"), _mm_call(a_b.T, ct_b, "mm_dw")


mm.defvjp(_mm_fwd, _mm_bwd)


def _rows_specs(rows, tm, ncb, bc):
    specs = []
    for idx, r in enumerate(rows):
        if idx in bc:
            specs.append(pl.BlockSpec((tm, r.shape[1]), lambda i, j: (i, 0)))
        else:
            specs.append(pl.BlockSpec((tm, r.shape[1] // ncb), lambda i, j: (i, j)))
    return specs


def _param_specs(params):
    return [pl.BlockSpec(p.shape, lambda i, j: (0, 0)) for p in params]


def _rows_fwd_call(fn, rows, params, outs, tm, ncb, bc, name):
    r_total = rows[0].shape[0]
    n_in = len(rows) + len(params)

    def body(*refs):
        res = fn(*[r[...] for r in refs[:n_in]])
        for o_ref, val in zip(refs[n_in:], res):
            o_ref[...] = val.astype(o_ref.dtype)

    return _call(
        body, name=name, grid=(r_total // tm, ncb),
        in_specs=_rows_specs(rows, tm, ncb, bc) + _param_specs(params),
        out_specs=[pl.BlockSpec((tm, w), lambda i, j: (i, j)) for w in outs],
        out_shape=[jax.ShapeDtypeStruct((r_total, w * ncb), F32) for w in outs],
        compiler_params=_cparams(("parallel", "parallel")),
    )(*rows, *params)


def _rows_bwd_call(fn, rows, params, cts, tm, ncb, bc, name):
    r_total = rows[0].shape[0]
    nr, npar, nct = len(rows), len(params), len(cts)

    def body(*refs):
        i, j = pl.program_id(0), pl.program_id(1)
        ins = [r[...] for r in refs[:nr + npar]]
        ct_vals = tuple(r[...] for r in refs[nr + npar:nr + npar + nct])
        d_refs = refs[nr + npar + nct:]
        _, vjp = jax.vjp(lambda *a: tuple(fn(*a)), *ins)
        grads = vjp(ct_vals)
        for idx in range(nr):
            if idx in bc:
                @pl.when(j == 0)
                def _(idx=idx):
                    d_refs[idx][...] = jnp.zeros_like(d_refs[idx])
                d_refs[idx][...] += grads[idx]
            else:
                d_refs[idx][...] = grads[idx]
        for idx in range(nr, nr + npar):
            @pl.when((i == 0) & (j == 0))
            def _(idx=idx):
                d_refs[idx][...] = jnp.zeros_like(d_refs[idx])
            d_refs[idx][...] += grads[idx]

    ct_specs = [pl.BlockSpec((tm, c.shape[1] // ncb), lambda i, j: (i, j)) for c in cts]
    return _call(
        body, name=name + "_bwd", grid=(r_total // tm, ncb),
        in_specs=_rows_specs(rows, tm, ncb, bc) + _param_specs(params) + ct_specs,
        out_specs=_rows_specs(rows, tm, ncb, bc) + _param_specs(params),
        out_shape=[jax.ShapeDtypeStruct(a.shape, F32) for a in list(rows) + list(params)],
        compiler_params=_cparams(("arbitrary", "arbitrary")),
    )(*rows, *params, *cts)


def rowop(fn, name, outs, tm, ncb=1, bc=()):
    @jax.custom_vjp
    def op(rows, params):
        return tuple(_rows_fwd_call(fn, rows, params, outs, tm, ncb, bc, name))

    def fwd(rows, params):
        return op(rows, params), (rows, params)

    def bwd(res, cts):
        rows, params = res
        d = _rows_bwd_call(fn, rows, params, cts, tm, ncb, bc, name)
        return tuple(d[:len(rows)]), tuple(d[len(rows):])

    op.defvjp(fwd, bwd)
    return op


def _sigmoid(x):
    return 1.0 / (1.0 + jnp.exp(-x))


def _silu(x):
    return x * _sigmoid(x)


def _softplus(x):
    return jnp.maximum(x, 0.0) + jnp.log(1.0 + jnp.exp(-jnp.abs(x)))


def _f_rmsnorm(x, g):
    return (x * lax.rsqrt(jnp.mean(x * x, axis=-1, keepdims=True) + EPS) * g,)


def _f_qnorm(x, g):
    return (x * lax.rsqrt(jnp.mean(x * x, axis=-1, keepdims=True) + EPS) * (g * (FOX_HD ** -0.5)),)


def _f_logsig(x, b):
    return (-_softplus(-(x + b)),)


def _f_gdn_q(x):
    y = _silu(x)
    return (y * lax.rsqrt(jnp.sum(y * y, axis=-1, keepdims=True) + EPS) * (GDN_HD ** -0.5),)


def _f_gdn_k(x):
    y = _silu(x)
    return (y * lax.rsqrt(jnp.sum(y * y, axis=-1, keepdims=True) + EPS),)


def _f_silu(x):
    return (_silu(x),)


def _f_gdn_gates(bl, al, a_log, dt_bias):
    return _sigmoid(bl), -jnp.exp(a_log) * _softplus(al + dt_bias)


def _f_gdn_out(o, z, g):
    return (o * lax.rsqrt(jnp.mean(o * o, axis=-1, keepdims=True) + EPS) * g * _silu(z),)


def _f_merge(g0, g1, ya, yb):
    return (_sigmoid(g0) * ya + _sigmoid(g1) * yb,)


def _f_residual(a, b, keep):
    return ((a + b) * keep,)


def _f_glu(a, b):
    return (_silu(a) * b,)


def _f_delta(do, o):
    return (jnp.sum(do * o, axis=-1, keepdims=True),)


def _shift_down(x, halo, s, row8):
    rx = pltpu.roll(x, s, 0)
    top = jnp.where(row8 < s, pltpu.roll(halo, s, 0), rx[:8])
    return jnp.concatenate([top, rx[8:]], axis=0)


def _shift_up(x, nxt, s, row8):
    tm = x.shape[0]
    rx = pltpu.roll(x, tm - s, 0)
    bot = jnp.where(row8 >= 8 - s, pltpu.roll(nxt, 8 - s, 0), rx[tm - 8:])
    return jnp.concatenate([rx[:tm - 8], bot], axis=0)


def _conv_tiles(r_total, c_total):
    return _tile(r_total, 256, 8), _tile(c_total, 1024, LANE)


def _conv_fwd_call(x, w8, k_taps):
    r_total, c_total = x.shape
    tm, tc = _conv_tiles(r_total, c_total)
    hb = tm // 8

    def body(x_ref, halo_ref, w_ref, y_ref):
        i = pl.program_id(1)
        xt = x_ref[...]
        halo = jnp.where(i > 0, halo_ref[...], 0.0)
        row8 = lax.broadcasted_iota(jnp.int32, (8, tc), 0)
        acc = w_ref[k_taps - 1:k_taps, :] * xt
        for k in range(k_taps - 1):
            acc += w_ref[k:k + 1, :] * _shift_down(xt, halo, k_taps - 1 - k, row8)
        y_ref[...] = acc

    return _call(
        body, name="dwconv_fwd", grid=(c_total // tc, r_total // tm),
        in_specs=[pl.BlockSpec((tm, tc), lambda c, i: (i, c)),
                  pl.BlockSpec((8, tc), lambda c, i: (jnp.maximum(i * hb - 1, 0), c)),
                  pl.BlockSpec((8, tc), lambda c, i: (0, c))],
        out_specs=pl.BlockSpec((tm, tc), lambda c, i: (i, c)),
        out_shape=jax.ShapeDtypeStruct(x.shape, F32),
        compiler_params=_cparams(("parallel", "parallel")),
    )(x, x, w8)


def _conv_bwd_call(x, w8, dy, k_taps):
    r_total, c_total = x.shape
    tm, tc = _conv_tiles(r_total, c_total)
    hb = tm // 8
    n_i = r_total // tm

    def body(x_ref, halo_ref, w_ref, dy_ref, nxt_ref, dx_ref, dw_ref):
        i = pl.program_id(1)
        xt, dyt = x_ref[...], dy_ref[...]
        halo = jnp.where(i > 0, halo_ref[...], 0.0)
        nxt = jnp.where(i < n_i - 1, nxt_ref[...], 0.0)
        row8 = lax.broadcasted_iota(jnp.int32, (8, tc), 0)
        dx = w_ref[k_taps - 1:k_taps, :] * dyt
        upd = jnp.where(row8 == k_taps - 1, jnp.sum(dyt * xt, axis=0, keepdims=True), 0.0)
        for k in range(k_taps - 1):
            s = k_taps - 1 - k
            dx += w_ref[k:k + 1, :] * _shift_up(dyt, nxt, s, row8)
            upd = jnp.where(row8 == k, jnp.sum(dyt * _shift_down(xt, halo, s, row8), axis=0, keepdims=True), upd)
        dx_ref[...] = dx

        @pl.when(i == 0)
        def _():
            dw_ref[...] = jnp.zeros_like(dw_ref)

        dw_ref[...] += upd

    return _call(
        body, name="dwconv_bwd", grid=(c_total // tc, n_i),
        in_specs=[pl.BlockSpec((tm, tc), lambda c, i: (i, c)),
                  pl.BlockSpec((8, tc), lambda c, i: (jnp.maximum(i * hb - 1, 0), c)),
                  pl.BlockSpec((8, tc), lambda c, i: (0, c)),
                  pl.BlockSpec((tm, tc), lambda c, i: (i, c)),
                  pl.BlockSpec((8, tc), lambda c, i: (jnp.minimum((i + 1) * hb, r_total // 8 - 1), c))],
        out_specs=[pl.BlockSpec((tm, tc), lambda c, i: (i, c)), pl.BlockSpec((8, tc), lambda c, i: (0, c))],
        out_shape=[jax.ShapeDtypeStruct(x.shape, F32), jax.ShapeDtypeStruct(w8.shape, F32)],
        compiler_params=_cparams(("parallel", "arbitrary")),
    )(x, x, w8, dy, dy)


def make_dwconv(k_taps):
    @jax.custom_vjp
    def op(x, w8):
        return _conv_fwd_call(x, w8, k_taps)

    def fwd(x, w8):
        return op(x, w8), (x, w8)

    def bwd(res, dy):
        x, w8 = res
        dx, dw = _conv_bwd_call(x, w8, dy, k_taps)
        return dx, dw

    op.defvjp(fwd, bwd)
    return op


def _cumsum_call(x, reverse):
    h, t_total = x.shape
    tb = _tile(t_total, 256, LANE)
    nb = t_total // tb

    def body(x_ref, o_ref, carry_ref):
        i = pl.program_id(0)

        @pl.when(i == 0)
        def _():
            carry_ref[...] = jnp.zeros_like(carry_ref)

        r = lax.broadcasted_iota(jnp.int32, (tb, tb), 0)
        c = lax.broadcasted_iota(jnp.int32, (tb, tb), 1)
        tri = jnp.where((r >= c) if reverse else (r <= c), 1.0, 0.0).astype(F32)
        xv = x_ref[...]
        carry = jnp.max(carry_ref[...], axis=1, keepdims=True)
        o_ref[...] = _raw_dot(xv, tri, 1, 0, True) + carry
        carry_ref[...] = jnp.broadcast_to(carry + jnp.sum(xv, axis=1, keepdims=True), carry_ref.shape)

    imap = (lambda i: (0, nb - 1 - i)) if reverse else (lambda i: (0, i))
    return _call(
        body, name="cumsum_rev" if reverse else "cumsum", grid=(nb,),
        in_specs=[pl.BlockSpec((h, tb), imap)], out_specs=pl.BlockSpec((h, tb), imap),
        out_shape=jax.ShapeDtypeStruct(x.shape, F32), scratch_shapes=[pltpu.VMEM((h, LANE), F32)],
        compiler_params=_cparams(("arbitrary",)),
    )(x)


@jax.custom_vjp
def cumsum_lanes(x):
    return _cumsum_call(x, False)


cumsum_lanes.defvjp(lambda x: (cumsum_lanes(x), None), lambda _, ct: (_cumsum_call(ct, True),))


NEG_BIG = -1e30


def _attn_sub_tiles(nb):
    return max(s for s in (3, 2, 1) if nb % s == 0)


def _attn_fwd_call(q, k, v, f_col, f_row):
    h_total, t_total, hd = q.shape
    blk = f_row.shape[-1]
    nb = t_total // blk
    nsub = _attn_sub_tiles(nb)
    tq = nsub * blk

    def body(q_ref, k_ref, v_ref, fc_ref, fr_ref, o_ref, lse_ref):
        i = pl.program_id(1)
        r = lax.broadcasted_iota(jnp.int32, (blk, blk), 0)
        c = lax.broadcasted_iota(jnp.int32, (blk, blk), 1)
        qs = [q_ref[0, s * blk:(s + 1) * blk, :].astype(BF16) for s in range(nsub)]
        fqs = [fc_ref[0, s * blk:(s + 1) * blk, :] for s in range(nsub)]

        def load_kv(j):
            off = pl.multiple_of(j * blk, blk)
            return (k_ref[0, pl.ds(off, blk), :].astype(BF16), v_ref[0, pl.ds(off, blk), :].astype(BF16),
                    fr_ref[0, j])

        def tile(kv, s, carry, diagonal):
            kj, vj, fk = kv
            m, l, acc = carry
            sc = _raw_dot(qs[s], kj, 1, 1, False) + fqs[s] - fk
            if diagonal:
                sc = jnp.where(c <= r, sc, NEG_BIG)
            m_new = jnp.maximum(m, jnp.max(sc, axis=1, keepdims=True))
            p = jnp.exp(sc - m_new)
            alpha = jnp.exp(m - m_new)
            l = alpha * l + jnp.sum(p, axis=1, keepdims=True)
            acc = alpha * acc + _raw_dot(p, vj, 1, 0, False)
            return m_new, l, acc

        def below_diagonal(j, carry):
            kv = load_kv(j)
            return tuple(tile(kv, s, carry[s], False) for s in range(nsub))

        init = tuple((jnp.full((blk, 1), NEG_BIG, F32), jnp.zeros((blk, 1), F32), jnp.zeros((blk, hd), F32))
                     for _ in range(nsub))
        carry = list(lax.fori_loop(0, i * nsub, below_diagonal, init))
        for d in range(nsub):
            kv = load_kv(i * nsub + d)
            for s in range(d, nsub):
                carry[s] = tile(kv, s, carry[s], s == d)
        for s, (m, l, acc) in enumerate(carry):
            o_ref[0, s * blk:(s + 1) * blk, :] = acc / l
            lse = m + jnp.log(l)
            lse_ref[0, s] = jnp.sum(jnp.where(r == c, lse, 0.0), axis=0, keepdims=True)

    return _call(
        body, name="fox_fwd", grid=(h_total, nb // nsub),
        in_specs=[pl.BlockSpec((1, tq, hd), lambda h, i: (h, i, 0)),
                  pl.BlockSpec((1, t_total, hd), lambda h, i: (h, 0, 0)),
                  pl.BlockSpec((1, t_total, hd), lambda h, i: (h, 0, 0)),
                  pl.BlockSpec((1, tq, 1), lambda h, i: (h, i, 0)),
                  pl.BlockSpec((1, nb, 1, blk), lambda h, i: (h, 0, 0, 0))],
        out_specs=[pl.BlockSpec((1, tq, hd), lambda h, i: (h, i, 0)),
                   pl.BlockSpec((1, nsub, 1, blk), lambda h, i: (h, i, 0, 0))],
        out_shape=[jax.ShapeDtypeStruct(q.shape, F32), jax.ShapeDtypeStruct(f_row.shape, F32)],
        compiler_params=_cparams(("parallel", "parallel")),
    )(q, k, v, f_col, f_row)


def _attn_bwd_call(q, k, v, f_col, f_row, lse_row, delta_row, do):
    h_total, t_total, hd = q.shape
    blk = f_row.shape[-1]
    nb = t_total // blk
    nsub = _attn_sub_tiles(nb)
    tkv = nsub * blk

    def body(q_ref, do_ref, k_ref, v_ref, fc_ref, fr_ref, lse_ref, dl_ref, dq_ref, dk_ref, dv_ref, dfk_ref, dfq_ref):
        j = pl.program_id(1)

        @pl.when(j == 0)
        def _():
            dq_ref[...] = jnp.zeros_like(dq_ref)
            dfq_ref[...] = jnp.zeros_like(dfq_ref)

        ks = [k_ref[0, s * blk:(s + 1) * blk, :].astype(BF16) for s in range(nsub)]
        vs = [v_ref[0, s * blk:(s + 1) * blk, :].astype(BF16) for s in range(nsub)]
        fks = [fc_ref[0, s * blk:(s + 1) * blk, :] for s in range(nsub)]
        r = lax.broadcasted_iota(jnp.int32, (blk, blk), 0)
        c = lax.broadcasted_iota(jnp.int32, (blk, blk), 1)

        def q_step(i, accs, subs):
            off = pl.multiple_of(i * blk, blk)
            qi = q_ref[0, pl.ds(off, blk), :]
            doi = do_ref[0, pl.ds(off, blk), :]
            fq, lse, dl = fr_ref[0, i], lse_ref[0, i], dl_ref[0, i]
            accs = list(accs)
            dq_i, dfq_i = None, None
            for s, diagonal in subs:
                dk, dv, dfk = accs[s]
                st = _raw_dot(ks[s], qi, 1, 1, False) + fq - fks[s] - lse
                if diagonal:
                    st = jnp.where(r <= c, st, NEG_BIG)
                pt = jnp.exp(st)
                dv = dv + _raw_dot(pt, doi, 1, 0, False)
                dst = pt * (_raw_dot(vs[s], doi, 1, 1, False) - dl)
                dk = dk + _raw_dot(dst, qi, 1, 0, False)
                dfk = dfk - jnp.sum(dst, axis=1, keepdims=True)
                accs[s] = (dk, dv, dfk)
                dq_s = _raw_dot(dst, ks[s], 0, 0, False)
                dfq_s = jnp.sum(dst, axis=0, keepdims=True)
                dq_i = dq_s if dq_i is None else dq_i + dq_s
                dfq_i = dfq_s if dfq_i is None else dfq_i + dfq_s
            dfq_ref[0, i] += dfq_i
            dq_ref[0, pl.ds(off, blk), :] += dq_i
            return tuple(accs)

        accs = tuple((jnp.zeros((blk, hd), F32), jnp.zeros((blk, hd), F32), jnp.zeros((blk, 1), F32))
                     for _ in range(nsub))
        for d in range(nsub):
            accs = q_step(j * nsub + d, accs, [(s, s == d) for s in range(d + 1)])
        accs = lax.fori_loop((j + 1) * nsub, nb,
                             lambda i, a: q_step(i, a, [(s, False) for s in range(nsub)]), accs)
        for s, (dk, dv, dfk) in enumerate(accs):
            dk_ref[0, s * blk:(s + 1) * blk, :] = dk
            dv_ref[0, s * blk:(s + 1) * blk, :] = dv
            dfk_ref[0, s * blk:(s + 1) * blk, :] = dfk

    full = pl.BlockSpec((1, t_total, hd), lambda h, j: (h, 0, 0))
    tile = pl.BlockSpec((1, tkv, hd), lambda h, j: (h, j, 0))
    col = pl.BlockSpec((1, tkv, 1), lambda h, j: (h, j, 0))
    rows = pl.BlockSpec((1, nb, 1, blk), lambda h, j: (h, 0, 0, 0))
    return _call(
        body, name="fox_bwd", grid=(h_total, nb // nsub),
        in_specs=[full, full, tile, tile, col, rows, rows, rows],
        out_specs=[full, tile, tile, col, rows],
        out_shape=[jax.ShapeDtypeStruct(q.shape, F32), jax.ShapeDtypeStruct(q.shape, F32),
                   jax.ShapeDtypeStruct(q.shape, F32), jax.ShapeDtypeStruct(f_col.shape, F32),
                   jax.ShapeDtypeStruct(f_row.shape, F32)],
        compiler_params=_cparams(("parallel", "arbitrary")),
    )(q.astype(BF16), do.astype(BF16), k, v, f_col, f_row, lse_row, delta_row)


@jax.custom_vjp
def fox_attention(q, k, v, f_col, f_row):
    return _attn_fwd_call(q, k, v, f_col, f_row)[0]


def _fox_fwd(q, k, v, f_col, f_row):
    o, lse_row = _attn_fwd_call(q, k, v, f_col, f_row)
    return o, (q, k, v, f_col, f_row, o, lse_row)


def _fox_bwd(res, do):
    q, k, v, f_col, f_row, o, lse_row = res
    h_total, t_total, hd = q.shape
    tm = _tile(h_total * t_total, 2048, 8)
    delta = _rows_fwd_call(_f_delta, (do.reshape(-1, hd), o.reshape(-1, hd)), (), (1,), tm, 1, (), "fox_delta")[0]
    return tuple(_attn_bwd_call(q, k, v, f_col, f_row, lse_row, delta.reshape(f_row.shape), do))


fox_attention.defvjp(_fox_fwd, _fox_bwd)


def _head_col(blk, h):
    lane = lax.broadcasted_iota(jnp.int32, blk.shape, 1)
    return jnp.sum(jnp.where(lane == h, blk, 0.0), axis=1, keepdims=True)


@jax.custom_vjp
def _cat2(a, b):
    return jnp.concatenate([a, b], axis=1)


_cat2.defvjp(lambda a, b: (_cat2(a, b), a.shape[1]), lambda na, ct: (ct[:, :na], ct[:, na:]))


@jax.custom_vjp
def _split2(x):
    half = x.shape[1] // 2
    return x[:, :half], x[:, half:]


_split2.defvjp(lambda x: (_split2(x), None), lambda _, cts: (jnp.concatenate(cts, axis=1),))


def _gdn_intra(h, q, k, v, b_blk, g_blk):
    n = q.shape[0]
    b, g = _head_col(b_blk, h), _head_col(g_blk, h)
    r = lax.broadcasted_iota(jnp.int32, (n, n), 0)
    c = lax.broadcasted_iota(jnp.int32, (n, n), 1)
    same = (r // GDN_CHUNK) == (c // GDN_CHUNK)
    incl = same & (r >= c)
    g_row = jnp.sum(jnp.where(r == c, g, 0.0), axis=0, keepdims=True)
    big_g = jnp.sum(jnp.where(incl, g_row, 0.0), axis=1, keepdims=True)
    big_g_row = jnp.sum(jnp.where(same & (r <= c), g, 0.0), axis=0, keepdims=True)
    g_tot = jnp.sum(jnp.where(same, g_row, 0.0), axis=1, keepdims=True)
    dec = jnp.where(incl, jnp.exp(jnp.where(incl, big_g - big_g_row, 0.0)), 0.0)
    dec_strict = jnp.where(r > c, dec, 0.0)
    e_g = jnp.exp(big_g)
    kb = k * b
    m = _dot(kb, k, 1, 1) * dec_strict
    rs = lax.broadcasted_iota(jnp.int32, (n, GDN_CHUNK), 0)
    cs = lax.broadcasted_iota(jnp.int32, (n, GDN_CHUNK), 1)
    fold = jnp.where(rs % GDN_CHUNK == cs, 1.0, 0.0).astype(F32)
    aqk = _dot(_dot(q, k, 1, 1) * dec, fold, 1, 0, True)
    x = _cat2(v * b, kb * e_g)
    x = x - _dot(m, x)
    p = m
    steps = 1
    while 2 * steps < GDN_CHUNK:
        p = _dot(p, p)
        x = x + _dot(p, x)
        steps *= 2
    u, w = _split2(x)
    lane = lax.broadcasted_iota(jnp.int32, b_blk.shape, 1)
    return u, w, q * e_g, k * jnp.exp(g_tot - big_g), aqk, jnp.where(lane == h, g_tot, 0.0)


def _gdn_rec(h, s, u, w, qg, kd, aqk, gl_blk):
    g_last = jnp.max(_head_col(gl_blk, h), axis=0, keepdims=True)
    big_u = u - _dot(w, s)
    o = _dot(qg, s) + _dot(aqk, big_u)
    s_next = s * jnp.exp(g_last) + _dot(kd, big_u, 0, 0)
    return o, s_next


GDN_TOK_BLK = 256


def _gdn_layout(t_total, rev):
    tb = _tile(t_total, GDN_TOK_BLK, GDN_CHUNK)
    cb, nblk = tb // GDN_CHUNK, t_total // tb
    pos = (lambda i: nblk - 1 - i) if rev else (lambda i: i)
    specs = dict(
        tok=pl.BlockSpec((tb, GDN_W), lambda i: (pos(i), 0)),
        gate=pl.BlockSpec((tb, GDN_HEADS), lambda i: (pos(i), 0)),
        aqk=pl.BlockSpec((GDN_HEADS, tb, GDN_CHUNK), lambda i: (0, pos(i), 0)),
        state=pl.BlockSpec((GDN_HEADS, cb, GDN_HD, GDN_HD), lambda i: (0, pos(i), 0, 0)))
    return cb, nblk, specs


def _gdn_shapes(t_total):
    n_chunks = t_total // GDN_CHUNK
    return dict(tok=jax.ShapeDtypeStruct((t_total, GDN_W), F32),
                gate=jax.ShapeDtypeStruct((t_total, GDN_HEADS), F32),
                aqk=jax.ShapeDtypeStruct((GDN_HEADS, t_total, GDN_CHUNK), F32),
                state=jax.ShapeDtypeStruct((GDN_HEADS, n_chunks, GDN_HD, GDN_HD), F32))


def _chunk_rows(ci):
    return pl.ds(pl.multiple_of(ci * GDN_CHUNK, GDN_CHUNK), GDN_CHUNK)


def _head_cols(h):
    return pl.ds(h * GDN_HD, GDN_HD)


def _gdn_intra_fwd_call(q, k, v, b, g):
    cb, nblk, sp = _gdn_layout(q.shape[0], False)
    sh = _gdn_shapes(q.shape[0])

    def body(q_ref, k_ref, v_ref, b_ref, g_ref, u_ref, w_ref, qg_ref, kd_ref, aqk_ref, gl_ref):
        b_blk, g_blk = b_ref[...], g_ref[...]
        gl = jnp.zeros(b_blk.shape, F32)
        for h in range(GDN_HEADS):
            cols = _head_cols(h)
            u, w, qg, kd, aqk, gl_h = _gdn_intra(h, q_ref[:, cols], k_ref[:, cols], v_ref[:, cols], b_blk, g_blk)
            u_ref[:, cols] = u
            w_ref[:, cols] = w
            qg_ref[:, cols] = qg
            kd_ref[:, cols] = kd
            aqk_ref[h] = aqk
            gl = gl + gl_h
        gl_ref[...] = gl

    return _call(
        body, name="gdn_intra_fwd", grid=(nblk,),
        in_specs=[sp["tok"]] * 3 + [sp["gate"]] * 2,
        out_specs=[sp["tok"]] * 4 + [sp["aqk"], sp["gate"]],
        out_shape=[sh["tok"]] * 4 + [sh["aqk"], sh["gate"]],
        compiler_params=_cparams(("parallel",)),
    )(q, k, v, b, g)


def _gdn_intra_bwd_call(q, k, v, b, g, du, dw, dqg, dkd, daqk, dgl):
    cb, nblk, sp = _gdn_layout(q.shape[0], False)
    sh = _gdn_shapes(q.shape[0])

    def body(q_ref, k_ref, v_ref, b_ref, g_ref, du_ref, dw_ref, dqg_ref, dkd_ref, daqk_ref, dgl_ref,
             dq_ref, dk_ref, dv_ref, db_ref, dg_ref):
        b_blk, g_blk, dgl = b_ref[...], g_ref[...], dgl_ref[...]
        db = jnp.zeros(b_blk.shape, F32)
        dg = jnp.zeros(b_blk.shape, F32)
        for h in range(GDN_HEADS):
            cols = _head_cols(h)
            _, vjp = jax.vjp(functools.partial(_gdn_intra, h), q_ref[:, cols], k_ref[:, cols], v_ref[:, cols],
                             b_blk, g_blk)
            dq, dk, dv, db_h, dg_h = vjp((du_ref[:, cols], dw_ref[:, cols], dqg_ref[:, cols], dkd_ref[:, cols],
                                          daqk_ref[h], dgl))
            dq_ref[:, cols] = dq
            dk_ref[:, cols] = dk
            dv_ref[:, cols] = dv
            db = db + db_h
            dg = dg + dg_h
        db_ref[...] = db
        dg_ref[...] = dg

    return _call(
        body, name="gdn_intra_bwd", grid=(nblk,),
        in_specs=[sp["tok"]] * 3 + [sp["gate"]] * 2 + [sp["tok"]] * 4 + [sp["aqk"], sp["gate"]],
        out_specs=[sp["tok"]] * 3 + [sp["gate"]] * 2,
        out_shape=[sh["tok"]] * 3 + [sh["gate"]] * 2,
        compiler_params=_cparams(("parallel",)),
    )(q, k, v, b, g, du, dw, dqg, dkd, daqk, dgl)


def _gdn_rec_fwd_call(u, w, qg, kd, aqk, gl):
    cb, nblk, sp = _gdn_layout(u.shape[0], False)
    sh = _gdn_shapes(u.shape[0])

    def body(u_ref, w_ref, qg_ref, kd_ref, aqk_ref, gl_ref, o_ref, s_all_ref, s_ref):
        @pl.when(pl.program_id(0) == 0)
        def _():
            s_ref[...] = jnp.zeros_like(s_ref)

        def chunk(ci, carry):
            rows = _chunk_rows(ci)
            gl_row = gl_ref[rows, :]
            states = [s_ref[h] for h in range(GDN_HEADS)]
            res = [_gdn_rec(h, states[h], u_ref[rows, _head_cols(h)], w_ref[rows, _head_cols(h)],
                            qg_ref[rows, _head_cols(h)], kd_ref[rows, _head_cols(h)], aqk_ref[h, rows, :], gl_row)
                   for h in range(GDN_HEADS)]
            for h, (o, s_next) in enumerate(res):
                s_all_ref[h, ci] = states[h]
                o_ref[rows, _head_cols(h)] = o
                s_ref[h] = s_next
            return carry

        lax.fori_loop(0, cb, chunk, 0)

    return _call(
        body, name="gdn_rec_fwd", grid=(nblk,),
        in_specs=[sp["tok"]] * 4 + [sp["aqk"], sp["gate"]],
        out_specs=[sp["tok"], sp["state"]], out_shape=[sh["tok"], sh["state"]],
        scratch_shapes=[pltpu.VMEM((GDN_HEADS, GDN_HD, GDN_HD), F32)],
        compiler_params=_cparams(("arbitrary",)),
    )(u, w, qg, kd, aqk, gl)


def _gdn_rec_bwd_call(u, w, qg, kd, aqk, gl, s_all, do):
    cb, nblk, sp = _gdn_layout(u.shape[0], True)
    sh = _gdn_shapes(u.shape[0])

    def body(u_ref, w_ref, qg_ref, kd_ref, aqk_ref, gl_ref, s_all_ref, do_ref,
             du_ref, dw_ref, dqg_ref, dkd_ref, daqk_ref, dgl_ref, ds_ref):
        @pl.when(pl.program_id(0) == 0)
        def _():
            ds_ref[...] = jnp.zeros_like(ds_ref)

        def chunk(step, carry):
            ci = cb - 1 - step
            rows = _chunk_rows(ci)
            gl_row = gl_ref[rows, :]
            res = []
            for h in range(GDN_HEADS):
                cols = _head_cols(h)
                _, vjp = jax.vjp(functools.partial(_gdn_rec, h), s_all_ref[h, ci], u_ref[rows, cols],
                                 w_ref[rows, cols], qg_ref[rows, cols], kd_ref[rows, cols], aqk_ref[h, rows, :],
                                 gl_row)
                res.append(vjp((do_ref[rows, cols], ds_ref[h])))
            dgl = jnp.zeros((GDN_CHUNK, GDN_HEADS), F32)
            for h, (ds, du, dw, dqg, dkd, daqk, dgl_h) in enumerate(res):
                cols = _head_cols(h)
                ds_ref[h] = ds
                du_ref[rows, cols] = du
                dw_ref[rows, cols] = dw
                dqg_ref[rows, cols] = dqg
                dkd_ref[rows, cols] = dkd
                daqk_ref[h, rows, :] = daqk
                dgl = dgl + dgl_h
            dgl_ref[rows, :] = dgl
            return carry

        lax.fori_loop(0, cb, chunk, 0)

    return _call(
        body, name="gdn_rec_bwd", grid=(nblk,),
        in_specs=[sp["tok"]] * 4 + [sp["aqk"], sp["gate"], sp["state"], sp["tok"]],
        out_specs=[sp["tok"]] * 4 + [sp["aqk"], sp["gate"]],
        out_shape=[sh["tok"]] * 4 + [sh["aqk"], sh["gate"]],
        scratch_shapes=[pltpu.VMEM((GDN_HEADS, GDN_HD, GDN_HD), F32)],
        compiler_params=_cparams(("arbitrary",)),
    )(u, w, qg, kd, aqk, gl, s_all, do)


@jax.custom_vjp
def gdn_intra(q, k, v, b, g):
    return tuple(_gdn_intra_fwd_call(q, k, v, b, g))


gdn_intra.defvjp(lambda *a: (gdn_intra(*a), a), lambda res, cts: tuple(_gdn_intra_bwd_call(*res, *cts)))


@jax.custom_vjp
def gdn_rec(u, w, qg, kd, aqk, gl):
    return _gdn_rec_fwd_call(u, w, qg, kd, aqk, gl)[0]


def _gdn_rec_fwd(*a):
    o, s_all = _gdn_rec_fwd_call(*a)
    return o, a + (s_all,)


gdn_rec.defvjp(_gdn_rec_fwd, lambda res, do: tuple(_gdn_rec_bwd_call(*res, do)))


def gated_delta(q, k, v, b, g):
    return gdn_rec(*gdn_intra(q, k, v, b, g))


def _loss_call(y, tgt, first, last):
    r_total, d = y.shape
    tm = _tile(r_total, 256, 8)

    def body(y_ref, t_ref, loss_ref, dy_ref):
        i = pl.program_id(0)

        @pl.when(i == 0)
        def _():
            loss_ref[...] = jnp.zeros_like(loss_ref)

        row = lax.broadcasted_iota(jnp.int32, (tm, d), 0) + i * tm
        err = jnp.where((row >= first) & (row < last), y_ref[...] - t_ref[...], 0.0)
        dy_ref[...] = err * (1.0 / d)
        part = jnp.sum(jnp.sum(err * err, axis=1, keepdims=True), axis=0, keepdims=True) * (0.5 / d)
        loss_ref[...] += jnp.broadcast_to(part, loss_ref.shape)

    return _call(
        body, name="loss_head", grid=(r_total // tm,),
        in_specs=[pl.BlockSpec((tm, d), lambda i: (i, 0))] * 2,
        out_specs=[pl.BlockSpec((8, LANE), lambda i: (0, 0)), pl.BlockSpec((tm, d), lambda i: (i, 0))],
        out_shape=[jax.ShapeDtypeStruct((8, LANE), F32), jax.ShapeDtypeStruct(y.shape, F32)],
        compiler_params=_cparams(("arbitrary",)),
    )(y, tgt)


def make_loss(first, last):
    @jax.custom_vjp
    def op(y, tgt):
        return _loss_call(y, tgt, first, last)[0][0, 0]

    def fwd(y, tgt):
        loss, dy = _loss_call(y, tgt, first, last)
        return loss[0, 0], (dy,)

    def bwd(res, ct):
        return res[0] * ct, jnp.zeros_like(res[0])

    op.defvjp(fwd, bwd)
    return op


def _pad_rows8(w):
    return jnp.concatenate([w, jnp.zeros((8 - w.shape[0], w.shape[1]), w.dtype)], axis=0)


def local_loss(wts, x, tgt):
    seq = x.shape[0]
    n_tok = N_META + seq
    t_pad = -(-n_tok // ROW_ALIGN) * ROW_ALIGN
    depth = wts["norm1_g"].shape[0]
    blk = _tile(t_pad, ATT_BLK, LANE)
    nb = t_pad // blk
    tm = _tile(t_pad, 256, 8)
    tm_wide = _tile(t_pad, 128, 8)

    rms = rowop(_f_rmsnorm, "rmsnorm", (D_MODEL,), tm)
    qnorm = rowop(_f_qnorm, "fox_q_norm", (FOX_HD,), _tile(FOX_HEADS * t_pad, 2048, 8))
    knorm = rowop(_f_rmsnorm, "fox_k_norm", (FOX_HD,), _tile(FOX_HEADS * t_pad, 2048, 8))
    logsig = rowop(_f_logsig, "fox_log_forget", (FOX_HEADS,), tm)
    gdn_q = rowop(_f_gdn_q, "gdn_q_act", (GDN_HD,), tm, ncb=GDN_HEADS)
    gdn_k = rowop(_f_gdn_k, "gdn_k_act", (GDN_HD,), tm, ncb=GDN_HEADS)
    silu = rowop(_f_silu, "gdn_v_act", (GDN_W,), tm)
    gates = rowop(_f_gdn_gates, "gdn_gates", (GDN_HEADS, GDN_HEADS), tm)
    gdn_out = rowop(_f_gdn_out, "gdn_out_norm", (GDN_HD,), tm, ncb=GDN_HEADS)
    merge = rowop(_f_merge, "branch_merge", (D_MODEL,), tm)
    glu = rowop(_f_glu, "ffn_glu", (D_FF // 2,), tm_wide, ncb=2)
    residual = rowop(_f_residual, "residual_add", (D_MODEL,), tm, bc=(2,))
    keep = (jnp.arange(t_pad)[:, None] < n_tok).astype(F32)
    conv4 = make_dwconv(GDN_CONV)
    conv3 = make_dwconv(FFN_CONV)
    loss_op = make_loss(N_META, n_tok)

    zeros = jnp.zeros((t_pad - n_tok, D_MODEL), F32)
    h_res = jnp.concatenate([wts["meta_tokens"], x, zeros], axis=0)
    tgt_rows = jnp.concatenate([jnp.zeros((N_META, D_MODEL), F32), tgt, zeros], axis=0)

    def heads(a):
        return a.reshape(t_pad, FOX_HEADS, FOX_HD).transpose(1, 0, 2).reshape(FOX_HEADS * t_pad, FOX_HD)

    for l in range(depth):
        h = rms((h_res,), (wts["norm1_g"][l][None],))[0]
        proj = mm(h, wts["w_in"][l])
        qn = qnorm((heads(proj[:, 0:512]),), (wts["fox_q_norm_g"][l][None],))[0]
        kn = knorm((heads(proj[:, 512:1024]),), (wts["fox_k_norm_g"][l][None],))[0]
        vh = heads(proj[:, 1024:1536])
        log_f = logsig((proj[:, 1536:1544],), (wts["fox_f_bias"][l][None],))[0]
        f_cum = cumsum_lanes(log_f.T)
        o_a = fox_attention(qn.reshape(FOX_HEADS, t_pad, FOX_HD), kn.reshape(FOX_HEADS, t_pad, FOX_HD),
                            vh.reshape(FOX_HEADS, t_pad, FOX_HD), f_cum[:, :, None],
                            f_cum.reshape(FOX_HEADS, nb, 1, blk))
        y_a = mm(o_a.transpose(1, 0, 2).reshape(t_pad, FOX_W), wts["w_branch_a"][l])
        cv = conv4(proj[:, 1664:4736], _pad_rows8(wts["gdn_conv_w"][l]))
        gq = gdn_q((cv[:, 0:GDN_W],), ())[0]
        gk = gdn_k((cv[:, GDN_W:2 * GDN_W],), ())[0]
        gv = silu((cv[:, 2 * GDN_W:],), ())[0]
        beta, gdec = gates((proj[:, 4736:4744], proj[:, 4744:4752]),
                           (wts["gdn_a_log"][l][None], wts["gdn_dt_bias"][l][None]))
        o_b = gated_delta(gq, gk, gv, beta, gdec)
        o_b = gdn_out((o_b, proj[:, 4864:5888]), (wts["gdn_norm_g"][l][None],))[0]
        y_b = mm(o_b, wts["w_branch_b"][l])
        mixed = merge((proj[:, 5888:6912], proj[:, 6912:7936], y_a, y_b), ())[0]
        h_res = residual((h_res, mm(mixed, wts["w_out"][l]), keep), ())[0]
        h = rms((h_res,), (wts["norm2_g"][l][None],))[0]
        up = conv3(mm(h, wts["w_up"][l]), _pad_rows8(wts["ffn_conv_w"][l]))
        act = glu((up[:, :D_FF], up[:, D_FF:]), ())[0]
        h_res = residual((h_res, mm(act, wts["w_down"][l]), keep), ())[0]
    return loss_op(h_res, tgt_rows)


def pad_w_in(w):
    parts, pos = [], 0
    for src, width, dst in IN_SEGS:
        if dst > pos:
            parts.append(jnp.zeros(w.shape[:-1] + (dst - pos,), w.dtype))
        parts.append(w[..., src:src + width])
        pos = dst + width
    parts.append(jnp.zeros(w.shape[:-1] + (D_IN_PAD - pos,), w.dtype))
    return jnp.concatenate(parts, axis=-1)


def unpad_w_in(w):
    return jnp.concatenate([w[..., dst:dst + width] for _, width, dst in IN_SEGS], axis=-1)


ANY = pl.BlockSpec(memory_space=pl.ANY)
N_CHIPS = 4
N_DEV = 8
COMM_COLS = 1024
COMM_ROW_ALIGN = 512
COMM_ROW_ALIGN_SMALL = 32


def _place():
    return lax.axis_index("x"), lax.axis_index("y"), lax.axis_index("c")


def _other_chips(x, y):
    return [(1 - x, y), (x, 1 - y), (1 - x, 1 - y)]


def _remote(src, dst, send_sem, recv_sem, dev):
    return pltpu.make_async_remote_copy(src_ref=src, dst_ref=dst, send_sem=send_sem, recv_sem=recv_sem,
                                        device_id=dev, device_id_type=MESH)


def chip_all_gather(buf):
    rows, cols = buf.shape
    half = rows // 2

    def body(x_ref, out_ref, send_sems, recv_sems, pass_send, pass_recv, local_sem):
        x, y, c = _place()
        me = 2 * x + y
        mine, other = pl.ds(c * half, half), pl.ds((1 - c) * half, half)
        sibling = (x, y, 1 - c)
        chips = _other_chips(x, y)
        local = pltpu.make_async_copy(x_ref, out_ref.at[me], local_sem)
        local.start()
        started = []
        for k, (px, py) in enumerate(chips):
            cp = _remote(x_ref.at[mine], out_ref.at[me, mine], send_sems.at[k], recv_sems.at[k], (px, py, c))
            cp.start()
            started.append(cp)
        for k, (px, py) in enumerate(chips):
            landed = out_ref.at[2 * px + py, mine]
            _remote(landed, landed, send_sems.at[k], recv_sems.at[k], (px, py, c)).wait_recv()
            cp = _remote(landed, landed, pass_send.at[k], pass_recv.at[k], sibling)
            cp.start()
            started.append(cp)
        for k, (px, py) in enumerate(chips):
            passed = out_ref.at[2 * px + py, other]
            _remote(passed, passed, pass_send.at[k], pass_recv.at[k], sibling).wait_recv()
        for cp in started:
            cp.wait_send()
        local.wait()

    return _call(
        body, name="chip_all_gather", in_specs=[ANY], out_specs=ANY,
        out_shape=jax.ShapeDtypeStruct((N_CHIPS, rows, cols), buf.dtype),
        scratch_shapes=[pltpu.SemaphoreType.DMA((3,)), pltpu.SemaphoreType.DMA((3,)),
                        pltpu.SemaphoreType.DMA((3,)), pltpu.SemaphoreType.DMA((3,)), pltpu.SemaphoreType.DMA],
    )(buf)


def sibling_swap_halves(g4):
    n, rows, cols = g4.shape
    half = rows // 2

    def body(g_ref, got_ref, send_sem, recv_sem):
        x, y, c = _place()
        cp = _remote(g_ref.at[:, pl.ds((1 - c) * half, half), :], got_ref, send_sem, recv_sem, (x, y, 1 - c))
        cp.start()
        cp.wait()

    return _call(
        body, name="sibling_swap_halves", in_specs=[ANY], out_specs=ANY,
        out_shape=jax.ShapeDtypeStruct((n, half, cols), g4.dtype),
        scratch_shapes=[pltpu.SemaphoreType.DMA, pltpu.SemaphoreType.DMA],
    )(g4)


def add_own_half(g4, got, c):
    n, rows, cols = g4.shape
    half = rows // 2
    tm = _tile(half, 128, 8)
    nt = half // tm

    def body(c_ref, a_ref, b_ref, o_ref):
        o_ref[...] = a_ref[...] + b_ref[...]

    return _call(
        body, name="add_own_half",
        grid_spec=pltpu.PrefetchScalarGridSpec(
            num_scalar_prefetch=1, grid=(n, nt),
            in_specs=[pl.BlockSpec((1, tm, cols), lambda j, i, c_ref: (j, c_ref[0] * nt + i, 0)),
                      pl.BlockSpec((1, tm, cols), lambda j, i, c_ref: (j, i, 0))],
            out_specs=pl.BlockSpec((1, tm, cols), lambda j, i, c_ref: (j, i, 0))),
        out_shape=jax.ShapeDtypeStruct(got.shape, F32),
        compiler_params=_cparams(("parallel", "parallel")),
    )(c.reshape(1).astype(jnp.int32), g4, got)


def chip_scatter(p4):
    n, rows, cols = p4.shape

    def body(p_ref, out_ref, send_sems, recv_sems, local_sem):
        x, y, c = _place()
        me = 2 * x + y
        chips = _other_chips(x, y)
        local = pltpu.make_async_copy(p_ref.at[me], out_ref.at[me], local_sem)
        local.start()
        started = []
        for k, (px, py) in enumerate(chips):
            cp = _remote(p_ref.at[2 * px + py], out_ref.at[me], send_sems.at[k], recv_sems.at[k], (px, py, c))
            cp.start()
            started.append(cp)
        for k, (px, py) in enumerate(chips):
            landed = out_ref.at[2 * px + py]
            _remote(landed, landed, send_sems.at[k], recv_sems.at[k], (px, py, c)).wait_recv()
        for cp in started:
            cp.wait_send()
        local.wait()

    return _call(
        body, name="chip_scatter", in_specs=[ANY], out_specs=ANY,
        out_shape=jax.ShapeDtypeStruct(p4.shape, p4.dtype),
        scratch_shapes=[pltpu.SemaphoreType.DMA((3,)), pltpu.SemaphoreType.DMA((3,)), pltpu.SemaphoreType.DMA],
    )(p4)


def sum_slots(a):
    n, rows, cols = a.shape
    tm = _tile(rows, 128, 8)

    def body(a_ref, o_ref):
        acc = a_ref[0]
        for k in range(1, n):
            acc = acc + a_ref[k]
        o_ref[...] = acc

    return _call(
        body, name="sum_slots_%d" % n, grid=(rows // tm,),
        in_specs=[pl.BlockSpec((n, tm, cols), lambda i: (0, i, 0))],
        out_specs=pl.BlockSpec((tm, cols), lambda i: (i, 0)),
        out_shape=jax.ShapeDtypeStruct((rows, cols), F32),
        compiler_params=_cparams(("parallel",)),
    )(a)


def sibling_join(s):
    half, cols = s.shape

    def body(s_ref, out_ref, send_sem, recv_sem, local_sem):
        x, y, c = _place()
        mine, other = pl.ds(c * half, half), pl.ds((1 - c) * half, half)
        local = pltpu.make_async_copy(s_ref, out_ref.at[mine], local_sem)
        local.start()
        cp = _remote(s_ref, out_ref.at[mine], send_sem, recv_sem, (x, y, 1 - c))
        cp.start()
        _remote(s_ref, out_ref.at[other], send_sem, recv_sem, (x, y, 1 - c)).wait_recv()
        cp.wait_send()
        local.wait()

    return _call(
        body, name="sibling_join", in_specs=[ANY], out_specs=ANY,
        out_shape=jax.ShapeDtypeStruct((2 * half, cols), s.dtype),
        scratch_shapes=[pltpu.SemaphoreType.DMA, pltpu.SemaphoreType.DMA, pltpu.SemaphoreType.DMA],
    )(s)


def all_devices_gather(buf):
    rows, cols = buf.shape

    def body(b_ref, out_ref, send_sems, recv_sems, local_sem):
        x, y, c = _place()
        me = 4 * x + 2 * y + c
        local = pltpu.make_async_copy(b_ref, out_ref.at[me], local_sem)
        local.start()
        peers = [((x + dx) % 2, (y + dy) % 2, (c + dc) % 2)
                 for dx in (0, 1) for dy in (0, 1) for dc in (0, 1) if dx + dy + dc > 0]
        started = []
        for k, peer in enumerate(peers):
            cp = _remote(b_ref, out_ref.at[me], send_sems.at[k], recv_sems.at[k], peer)
            cp.start()
            started.append(cp)
        for k, (px, py, pc) in enumerate(peers):
            landed = out_ref.at[4 * px + 2 * py + pc]
            _remote(landed, landed, send_sems.at[k], recv_sems.at[k], (px, py, pc)).wait_recv()
        for cp in started:
            cp.wait_send()
        local.wait()

    return _call(
        body, name="all_devices_gather", in_specs=[ANY], out_specs=ANY,
        out_shape=jax.ShapeDtypeStruct((N_DEV, rows, cols), buf.dtype),
        scratch_shapes=[pltpu.SemaphoreType.DMA((7,)), pltpu.SemaphoreType.DMA((7,)), pltpu.SemaphoreType.DMA],
    )(buf)


def adamw(w, g, m, v):
    shape = w.shape
    w2, g2, m2, v2 = [a.reshape(-1, shape[-1]) for a in (w, g, m, v)]
    rows, cols = w2.shape
    tm = _tile(rows, 256, 8) if rows % 8 == 0 else rows

    def body(w_ref, g_ref, m_ref, v_ref, d_ref, nm_ref, nv_ref):
        gv = g_ref[...]
        nm = ADAM_B1 * m_ref[...] + (1.0 - ADAM_B1) * gv
        nv = ADAM_B2 * v_ref[...] + (1.0 - ADAM_B2) * (gv * gv)
        m_hat = nm / (1.0 - ADAM_B1 ** ADAM_STEP)
        v_hat = nv / (1.0 - ADAM_B2 ** ADAM_STEP)
        d_ref[...] = -ADAM_LR * (m_hat / (jnp.sqrt(v_hat) + ADAM_EPS) + ADAM_WD * w_ref[...])
        nm_ref[...] = nm
        nv_ref[...] = nv

    spec = pl.BlockSpec((tm, cols), lambda i: (i, 0))
    outs = _call(
        body, name="adamw", grid=(rows // tm,), in_specs=[spec] * 4, out_specs=[spec] * 3,
        out_shape=[jax.ShapeDtypeStruct((rows, cols), F32)] * 3,
        compiler_params=_cparams(("parallel",)),
    )(w2, g2, m2, v2)
    return [o.reshape(shape) for o in outs]


WEIGHTS = ("meta_tokens", "norm1_g", "w_in", "fox_f_bias", "fox_q_norm_g", "fox_k_norm_g", "gdn_conv_w",
           "gdn_a_log", "gdn_dt_bias", "gdn_norm_g", "w_branch_a", "w_branch_b", "w_out", "norm2_g", "w_up",
           "ffn_conv_w", "w_down")
SHARD_AXIS = {"meta_tokens": -1, "w_in": -1, "gdn_conv_w": -1, "w_branch_a": -1, "w_branch_b": -2, "w_out": -2,
              "w_up": -1, "ffn_conv_w": -1, "w_down": -2}
MATMUL_WEIGHTS = ("w_in", "w_branch_a", "w_branch_b", "w_out", "w_up", "w_down")
SMALL_SHARDED = ("meta_tokens", "gdn_conv_w", "ffn_conv_w")
REPLICATED = tuple(n for n in WEIGHTS if n not in SHARD_AXIS)


def _pack(arrays, dtype, row_align):
    flat = jnp.concatenate([a.reshape(-1).astype(dtype) for a in arrays])
    block = row_align * COMM_COLS
    total = -(-flat.shape[0] // block) * block
    flat = jnp.concatenate([flat, jnp.zeros((total - flat.shape[0],), dtype)])
    return flat.reshape(-1, COMM_COLS)


def _unpack(buf, shapes):
    flat, out, pos = buf.reshape(-1), [], 0
    for shape in shapes:
        size = 1
        for d in shape:
            size *= d
        out.append(flat[pos:pos + size].reshape(shape))
        pos += size
    return out


def _gather_full(shards, names, dtype, row_align):
    got = chip_all_gather(_pack([shards[n] for n in names], dtype, row_align))
    per_chip = [_unpack(got[j], [shards[n].shape for n in names]) for j in range(N_CHIPS)]
    return {n: jnp.concatenate([per_chip[j][i] for j in range(N_CHIPS)], axis=SHARD_AXIS[n]).astype(F32)
            for i, n in enumerate(names)}


def _shard_of(full, name, j):
    axis = SHARD_AXIS[name] % full.ndim
    size = full.shape[axis] // N_CHIPS
    return lax.slice_in_dim(full, j * size, (j + 1) * size, axis=axis)


def kernel(x, meta_tokens, norm1_g, w_in, fox_f_bias, fox_q_norm_g, fox_k_norm_g, gdn_conv_w, gdn_a_log, gdn_dt_bias, gdn_norm_g, w_branch_a, w_branch_b, w_out, norm2_g, w_up, ffn_conv_w, w_down, loss_target, m_meta_tokens, m_norm1_g, m_w_in, m_fox_f_bias, m_fox_q_norm_g, m_fox_k_norm_g, m_gdn_conv_w, m_gdn_a_log, m_gdn_dt_bias, m_gdn_norm_g, m_w_branch_a, m_w_branch_b, m_w_out, m_norm2_g, m_w_up, m_ffn_conv_w, m_w_down, v_meta_tokens, v_norm1_g, v_w_in, v_fox_f_bias, v_fox_q_norm_g, v_fox_k_norm_g, v_gdn_conv_w, v_gdn_a_log, v_gdn_dt_bias, v_gdn_norm_g, v_w_branch_a, v_w_branch_b, v_w_out, v_norm2_g, v_w_up, v_ffn_conv_w, v_w_down):
    w_loc = dict(zip(WEIGHTS, (meta_tokens, norm1_g, w_in, fox_f_bias, fox_q_norm_g, fox_k_norm_g, gdn_conv_w,
                               gdn_a_log, gdn_dt_bias, gdn_norm_g, w_branch_a, w_branch_b, w_out, norm2_g, w_up,
                               ffn_conv_w, w_down)))
    m_loc = dict(zip(WEIGHTS, (m_meta_tokens, m_norm1_g, m_w_in, m_fox_f_bias, m_fox_q_norm_g, m_fox_k_norm_g,
                               m_gdn_conv_w, m_gdn_a_log, m_gdn_dt_bias, m_gdn_norm_g, m_w_branch_a, m_w_branch_b,
                               m_w_out, m_norm2_g, m_w_up, m_ffn_conv_w, m_w_down)))
    v_loc = dict(zip(WEIGHTS, (v_meta_tokens, v_norm1_g, v_w_in, v_fox_f_bias, v_fox_q_norm_g, v_fox_k_norm_g,
                               v_gdn_conv_w, v_gdn_a_log, v_gdn_dt_bias, v_gdn_norm_g, v_w_branch_a, v_w_branch_b,
                               v_w_out, v_norm2_g, v_w_up, v_ffn_conv_w, v_w_down)))
    c = lax.axis_index("c")

    full = {n: w_loc[n] for n in REPLICATED}
    full.update(_gather_full(w_loc, MATMUL_WEIGHTS, BF16, COMM_ROW_ALIGN))
    full.update(_gather_full(w_loc, SMALL_SHARDED, F32, COMM_ROW_ALIGN_SMALL))
    full["w_in"] = pad_w_in(full["w_in"])

    loss, (g_full, g_x) = jax.value_and_grad(local_loss, argnums=(0, 1))(full, x[0], loss_target[0])
    g_full = dict(g_full)
    g_full["w_in"] = unpad_w_in(g_full["w_in"])

    sharded = MATMUL_WEIGHTS + SMALL_SHARDED
    g4 = jnp.stack([_pack([_shard_of(g_full[n], n, j) for n in sharded], F32, COMM_ROW_ALIGN)
                    for j in range(N_CHIPS)])
    pair_sum = add_own_half(g4, sibling_swap_halves(g4), c)
    g_shard = sibling_join(sum_slots(chip_scatter(pair_sum)))
    grads = dict(zip(sharded, _unpack(g_shard, [w_loc[n].shape for n in sharded])))
    g_rep = sum_slots(all_devices_gather(_pack([g_full[n] for n in REPLICATED], F32, 8)))
    grads.update(zip(REPLICATED, _unpack(g_rep, [w_loc[n].shape for n in REPLICATED])))

    loss = lax.psum(loss, ("x", "y", "c"))
    upd = {n: adamw(w_loc[n], grads[n], m_loc[n], v_loc[n]) for n in WEIGHTS}
    return (loss, g_x[None], *[grads[n] for n in WEIGHTS], *[upd[n][0] for n in WEIGHTS],
            *[upd[n][1] for n in WEIGHTS], *[upd[n][2] for n in WEIGHTS])
```

```python
import functools

import jax
import jax.numpy as jnp
from jax import lax
from jax.experimental import pallas as pl
from jax.experimental.pallas import tpu as pltpu

F32 = jnp.float32
BF16 = jnp.bfloat16
HI = lax.Precision.HIGHEST
MESH = pl.DeviceIdType.MESH

D_MODEL = 1024
N_META = 16
EPS = 1e-6
FOX_HEADS, FOX_HD = 8, 64
FOX_W = FOX_HEADS * FOX_HD
GDN_HEADS, GDN_HD, GDN_CHUNK, GDN_CONV = 8, 128, 64, 4
GDN_W = GDN_HEADS * GDN_HD
D_FF = 2816
FFN_CONV = 3
D_IN = 7704
D_IN_PAD = 8192
IN_SEGS = ((0, 1536, 0), (1536, 8, 1536), (1544, 3072, 1664), (4616, 16, 4736), (4632, 1024, 4864), (5656, 2048, 5888))
ROW_ALIGN = 256
ATT_BLK = 256
VMEM_LIMIT = 48 * 1024 * 1024
LANE = 128

ADAM_LR, ADAM_B1, ADAM_B2, ADAM_EPS, ADAM_WD, ADAM_STEP = 0.001, 0.9, 0.999, 1e-08, 0.01, 10


def _call(body, **kw):
    return pl.pallas_call(body, **kw)


def _tile(n, target, mult):
    best, t = None, mult
    while t <= min(n, target):
        if n % t == 0:
            best = t
        t += mult
    assert best is not None, (n, target, mult)
    return best


def _cparams(sem):
    return pltpu.CompilerParams(dimension_semantics=sem, vmem_limit_bytes=VMEM_LIMIT)


def _raw_dot(a, b, ca, cb, precise):
    dims = (((ca,), (cb,)), ((), ()))
    a_hi, b_hi = a.astype(BF16), b.astype(BF16)
    out = lax.dot_general(a_hi, b_hi, dims, preferred_element_type=F32)
    if precise:
        a_lo = (a - a_hi.astype(F32)).astype(BF16)
        b_lo = (b - b_hi.astype(F32)).astype(BF16)
        out = out + (lax.dot_general(a_hi, b_lo, dims, preferred_element_type=F32)
                     + lax.dot_general(a_lo, b_hi, dims, preferred_element_type=F32))
    return out


def _make_dot(ca, cb, precise):
    @jax.custom_vjp
    def f(a, b):
        return _raw_dot(a, b, ca, cb, precise)

    def fwd(a, b):
        return f(a, b), (a, b)

    def bwd(res, ct):
        a, b = res
        if ca == 1:
            da = _raw_dot(ct, b, 1, 1 if cb == 0 else 0, precise)
        else:
            da = _raw_dot(b, ct, 1 if cb == 0 else 0, 1, precise)
        if cb == 0:
            db = _raw_dot(a, ct, 0 if ca == 1 else 1, 0, precise)
        else:
            db = _raw_dot(ct, a, 0, 0 if ca == 1 else 1, precise)
        return da, db

    f.defvjp(fwd, bwd)
    return f


_DOTS = {(ca, cb, p): _make_dot(ca, cb, p) for ca in (0, 1) for cb in (0, 1) for p in (False, True)}


def _dot(a, b, ca=1, cb=0, precise=False):
    return _DOTS[(ca, cb, precise)](a, b)


def _mm_call(a, b, name):
    m, k = a.shape
    n = b.shape[1]
    assert k == b.shape[0], (a.shape, b.shape)
    tm = _tile(m, 768, 16)
    tn = _tile(n, 1408, LANE)
    tk = _tile(k, 1408, LANE)
    nk = k // tk

    def body(a_ref, b_ref, o_ref, *scratch):
        part = jnp.dot(a_ref[...], b_ref[...], preferred_element_type=F32)
        if nk == 1:
            o_ref[...] = part
            return
        acc_ref = scratch[0]
        kk = pl.program_id(2)

        @pl.when(kk == 0)
        def _():
            acc_ref[...] = part

        @pl.when(kk > 0)
        def _():
            acc_ref[...] += part

        @pl.when(kk == nk - 1)
        def _():
            o_ref[...] = acc_ref[...]

    return _call(
        body, name=name, grid=(m // tm, n // tn, nk),
        in_specs=[pl.BlockSpec((tm, tk), lambda i, j, kk: (i, kk)), pl.BlockSpec((tk, tn), lambda i, j, kk: (kk, j))],
        out_specs=pl.BlockSpec((tm, tn), lambda i, j, kk: (i, j)),
        out_shape=jax.ShapeDtypeStruct((m, n), F32),
        scratch_shapes=[pltpu.VMEM((tm, tn), F32)] if nk > 1 else [],
        compiler_params=_cparams(("parallel", "parallel", "arbitrary")),
    )(a, b)


@jax.custom_vjp
def mm(a, w):
    return _mm_call(a.astype(BF16), w.astype(BF16), "mm_fwd")


def _mm_fwd(a, w):
    a_b, w_b = a.astype(BF16), w.astype(BF16)
    return _mm_call(a_b, w_b, "mm_fwd"), (a_b, w_b)


def _mm_bwd(res, ct):
    a_b, w_b = res
    ct_b = ct.astype(BF16)
    return _mm_call(ct_b, w_b.T, "mm_dx"), _mm_call(a_b.T, ct_b, "mm_dw")


mm.defvjp(_mm_fwd, _mm_bwd)


def _rows_specs(rows, tm, ncb, bc):
    specs = []
    for idx, r in enumerate(rows):
        if idx in bc:
            specs.append(pl.BlockSpec((tm, r.shape[1]), lambda i, j: (i, 0)))
        else:
            specs.append(pl.BlockSpec((tm, r.shape[1] // ncb), lambda i, j: (i, j)))
    return specs


def _param_specs(params):
    return [pl.BlockSpec(p.shape, lambda i, j: (0, 0)) for p in params]


def _rows_fwd_call(fn, rows, params, outs, tm, ncb, bc, name):
    r_total = rows[0].shape[0]
    n_in = len(rows) + len(params)

    def body(*refs):
        res = fn(*[r[...] for r in refs[:n_in]])
        for o_ref, val in zip(refs[n_in:], res):
            o_ref[...] = val.astype(o_ref.dtype)

    return _call(
        body, name=name, grid=(r_total // tm, ncb),
        in_specs=_rows_specs(rows, tm, ncb, bc) + _param_specs(params),
        out_specs=[pl.BlockSpec((tm, w), lambda i, j: (i, j)) for w in outs],
        out_shape=[jax.ShapeDtypeStruct((r_total, w * ncb), F32) for w in outs],
        compiler_params=_cparams(("parallel", "parallel")),
    )(*rows, *params)


def _rows_bwd_call(fn, rows, params, cts, tm, ncb, bc, name):
    r_total = rows[0].shape[0]
    nr, npar, nct = len(rows), len(params), len(cts)

    def body(*refs):
        i, j = pl.program_id(0), pl.program_id(1)
        ins = [r[...] for r in refs[:nr + npar]]
        ct_vals = tuple(r[...] for r in refs[nr + npar:nr + npar + nct])
        d_refs = refs[nr + npar + nct:]
        _, vjp = jax.vjp(lambda *a: tuple(fn(*a)), *ins)
        grads = vjp(ct_vals)
        for idx in range(nr):
            if idx in bc:
                @pl.when(j == 0)
                def _(idx=idx):
                    d_refs[idx][...] = jnp.zeros_like(d_refs[idx])
                d_refs[idx][...] += grads[idx]
            else:
                d_refs[idx][...] = grads[idx]
        for idx in range(nr, nr + npar):
            @pl.when((i == 0) & (j == 0))
            def _(idx=idx):
                d_refs[idx][...] = jnp.zeros_like(d_refs[idx])
            d_refs[idx][...] += grads[idx]

    ct_specs = [pl.BlockSpec((tm, c.shape[1] // ncb), lambda i, j: (i, j)) for c in cts]
    return _call(
        body, name=name + "_bwd", grid=(r_total // tm, ncb),
        in_specs=_rows_specs(rows, tm, ncb, bc) + _param_specs(params) + ct_specs,
        out_specs=_rows_specs(rows, tm, ncb, bc) + _param_specs(params),
        out_shape=[jax.ShapeDtypeStruct(a.shape, F32) for a in list(rows) + list(params)],
        compiler_params=_cparams(("arbitrary", "arbitrary")),
    )(*rows, *params, *cts)


def rowop(fn, name, outs, tm, ncb=1, bc=()):
    @jax.custom_vjp
    def op(rows, params):
        return tuple(_rows_fwd_call(fn, rows, params, outs, tm, ncb, bc, name))

    def fwd(rows, params):
        return op(rows, params), (rows, params)

    def bwd(res, cts):
        rows, params = res
        d = _rows_bwd_call(fn, rows, params, cts, tm, ncb, bc, name)
        return tuple(d[:len(rows)]), tuple(d[len(rows):])

    op.defvjp(fwd, bwd)
    return op


def _sigmoid(x):
    return 1.0 / (1.0 + jnp.exp(-x))


def _silu(x):
    return x * _sigmoid(x)


def _softplus(x):
    return jnp.maximum(x, 0.0) + jnp.log(1.0 + jnp.exp(-jnp.abs(x)))


def _f_rmsnorm(x, g):
    return (x * lax.rsqrt(jnp.mean(x * x, axis=-1, keepdims=True) + EPS) * g,)


def _f_qnorm(x, g):
    return (x * lax.rsqrt(jnp.mean(x * x, axis=-1, keepdims=True) + EPS) * (g * (FOX_HD ** -0.5)),)


def _f_logsig(x, b):
    return (-_softplus(-(x + b)),)


def _f_gdn_q(x):
    y = _silu(x)
    return (y * lax.rsqrt(jnp.sum(y * y, axis=-1, keepdims=True) + EPS) * (GDN_HD ** -0.5),)


def _f_gdn_k(x):
    y = _silu(x)
    return (y * lax.rsqrt(jnp.sum(y * y, axis=-1, keepdims=True) + EPS),)


def _f_silu(x):
    return (_silu(x),)


def _f_gdn_gates(bl, al, a_log, dt_bias):
    return _sigmoid(bl), -jnp.exp(a_log) * _softplus(al + dt_bias)


def _f_gdn_out(o, z, g):
    return (o * lax.rsqrt(jnp.mean(o * o, axis=-1, keepdims=True) + EPS) * g * _silu(z),)


def _f_merge(g0, g1, ya, yb):
    return (_sigmoid(g0) * ya + _sigmoid(g1) * yb,)


def _f_residual(a, b, keep):
    return ((a + b) * keep,)


def _f_glu(a, b):
    return (_silu(a) * b,)


def _shift_down(x, halo, s, row8):
    rx = pltpu.roll(x, s, 0)
    top = jnp.where(row8 < s, pltpu.roll(halo, s, 0), rx[:8])
    return jnp.concatenate([top, rx[8:]], axis=0)


def _shift_up(x, nxt, s, row8):
    tm = x.shape[0]
    rx = pltpu.roll(x, tm - s, 0)
    bot = jnp.where(row8 >= 8 - s, pltpu.roll(nxt, 8 - s, 0), rx[tm - 8:])
    return jnp.concatenate([rx[:tm - 8], bot], axis=0)


def _conv_tiles(r_total, c_total):
    return _tile(r_total, 768, 8), _tile(c_total, 1408, LANE)


def _conv_fwd_call(x, w8, k_taps):
    r_total, c_total = x.shape
    tm, tc = _conv_tiles(r_total, c_total)
    hb = tm // 8

    def body(x_ref, halo_ref, w_ref, y_ref):
        i = pl.program_id(1)
        xt = x_ref[...]
        halo = jnp.where(i > 0, halo_ref[...], 0.0)
        row8 = lax.broadcasted_iota(jnp.int32, (8, tc), 0)
        acc = w_ref[k_taps - 1:k_taps, :] * xt
        for k in range(k_taps - 1):
            acc += w_ref[k:k + 1, :] * _shift_down(xt, halo, k_taps - 1 - k, row8)
        y_ref[...] = acc

    return _call(
        body, name="dwconv_fwd", grid=(c_total // tc, r_total // tm),
        in_specs=[pl.BlockSpec((tm, tc), lambda c, i: (i, c)),
                  pl.BlockSpec((8, tc), lambda c, i: (jnp.maximum(i * hb - 1, 0), c)),
                  pl.BlockSpec((8, tc), lambda c, i: (0, c))],
        out_specs=pl.BlockSpec((tm, tc), lambda c, i: (i, c)),
        out_shape=jax.ShapeDtypeStruct(x.shape, F32),
        compiler_params=_cparams(("parallel", "parallel")),
    )(x, x, w8)


def _conv_bwd_call(x, w8, dy, k_taps):
    r_total, c_total = x.shape
    tm, tc = _conv_tiles(r_total, c_total)
    hb = tm // 8
    n_i = r_total // tm

    def body(x_ref, halo_ref, w_ref, dy_ref, nxt_ref, dx_ref, dw_ref):
        i = pl.program_id(1)
        xt, dyt = x_ref[...], dy_ref[...]
        halo = jnp.where(i > 0, halo_ref[...], 0.0)
        nxt = jnp.where(i < n_i - 1, nxt_ref[...], 0.0)
        row8 = lax.broadcasted_iota(jnp.int32, (8, tc), 0)
        dx = w_ref[k_taps - 1:k_taps, :] * dyt
        upd = jnp.where(row8 == k_taps - 1, jnp.sum(dyt * xt, axis=0, keepdims=True), 0.0)
        for k in range(k_taps - 1):
            s = k_taps - 1 - k
            dx += w_ref[k:k + 1, :] * _shift_up(dyt, nxt, s, row8)
            upd = jnp.where(row8 == k, jnp.sum(dyt * _shift_down(xt, halo, s, row8), axis=0, keepdims=True), upd)
        dx_ref[...] = dx

        @pl.when(i == 0)
        def _():
            dw_ref[...] = jnp.zeros_like(dw_ref)

        dw_ref[...] += upd

    return _call(
        body, name="dwconv_bwd", grid=(c_total // tc, n_i),
        in_specs=[pl.BlockSpec((tm, tc), lambda c, i: (i, c)),
                  pl.BlockSpec((8, tc), lambda c, i: (jnp.maximum(i * hb - 1, 0), c)),
                  pl.BlockSpec((8, tc), lambda c, i: (0, c)),
                  pl.BlockSpec((tm, tc), lambda c, i: (i, c)),
                  pl.BlockSpec((8, tc), lambda c, i: (jnp.minimum((i + 1) * hb, r_total // 8 - 1), c))],
        out_specs=[pl.BlockSpec((tm, tc), lambda c, i: (i, c)), pl.BlockSpec((8, tc), lambda c, i: (0, c))],
        out_shape=[jax.ShapeDtypeStruct(x.shape, F32), jax.ShapeDtypeStruct(w8.shape, F32)],
        compiler_params=_cparams(("parallel", "arbitrary")),
    )(x, x, w8, dy, dy)


def make_dwconv(k_taps):
    @jax.custom_vjp
    def op(x, w8):
        return _conv_fwd_call(x, w8, k_taps)

    def fwd(x, w8):
        return op(x, w8), (x, w8)

    def bwd(res, dy):
        x, w8 = res
        dx, dw = _conv_bwd_call(x, w8, dy, k_taps)
        return dx, dw

    op.defvjp(fwd, bwd)
    return op


def _cumsum_call(x, reverse):
    h, t_total = x.shape
    tb = _tile(t_total, 256, LANE)
    nb = t_total // tb

    def body(x_ref, o_ref, carry_ref):
        i = pl.program_id(0)

        @pl.when(i == 0)
        def _():
            carry_ref[...] = jnp.zeros_like(carry_ref)

        r = lax.broadcasted_iota(jnp.int32, (tb, tb), 0)
        c = lax.broadcasted_iota(jnp.int32, (tb, tb), 1)
        tri = jnp.where((r >= c) if reverse else (r <= c), 1.0, 0.0).astype(F32)
        xv = x_ref[...]
        carry = jnp.max(carry_ref[...], axis=1, keepdims=True)
        o_ref[...] = _raw_dot(xv, tri, 1, 0, True) + carry
        carry_ref[...] = jnp.broadcast_to(carry + jnp.sum(xv, axis=1, keepdims=True), carry_ref.shape)

    imap = (lambda i: (0, nb - 1 - i)) if reverse else (lambda i: (0, i))
    return _call(
        body, name="cumsum_rev" if reverse else "cumsum", grid=(nb,),
        in_specs=[pl.BlockSpec((h, tb), imap)], out_specs=pl.BlockSpec((h, tb), imap),
        out_shape=jax.ShapeDtypeStruct(x.shape, F32), scratch_shapes=[pltpu.VMEM((h, LANE), F32)],
        compiler_params=_cparams(("arbitrary",)),
    )(x)


@jax.custom_vjp
def cumsum_lanes(x):
    return _cumsum_call(x, False)


cumsum_lanes.defvjp(lambda x: (cumsum_lanes(x), None), lambda _, ct: (_cumsum_call(ct, True),))


NEG_BIG = -1e30


def _attn_sub_tiles(nb):
    return max(s for s in (3, 2, 1) if nb % s == 0)


EXP_ZERO = -100.0
SMEM = pl.BlockSpec(memory_space=pltpu.SMEM)


def _max_row_norm_sq(x):
    h_total, t_total, hd = x.shape
    tb = _tile(t_total, 2816, 8)

    def body(x_ref, o_ref):
        @pl.when(pl.program_id(1) == 0)
        def _():
            o_ref[...] = jnp.zeros_like(o_ref)

        xv = x_ref[0]
        top = jnp.max(jnp.sum(xv * xv, axis=1, keepdims=True), axis=0, keepdims=True)
        o_ref[0] = jnp.maximum(o_ref[0], top)

    return _call(
        body, name="max_row_norm", grid=(h_total, t_total // tb),
        in_specs=[pl.BlockSpec((1, tb, hd), lambda h, i: (h, i, 0))],
        out_specs=pl.BlockSpec((1, 8, LANE), lambda h, i: (h, 0, 0)),
        out_shape=jax.ShapeDtypeStruct((h_total, 8, LANE), F32),
        compiler_params=_cparams(("parallel", "arbitrary")),
    )(x)


def _attn_skip_tables(q, k, f_row):
    bound = 2.0 * jnp.sqrt(_max_row_norm_sq(q)[:, 0, :1] * _max_row_norm_sq(k)[:, 0, :1])
    return EXP_ZERO - bound, f_row[:, :, 0, 0], f_row[:, :, 0, -1]


def _attn_fwd_call(q, k, v, f_col, f_row, tables):
    h_total, t_total, hd = q.shape
    blk = f_row.shape[-1]
    nb = t_total // blk
    nsub = _attn_sub_tiles(nb)
    tq = nsub * blk

    def body(thr_ref, first_ref, last_ref, q_ref, k_ref, vt_ref, fc_ref, fr_ref, o_ref, lse_ref):
        h = pl.program_id(0)
        i = pl.program_id(1)
        gap_needed = thr_ref[h, 0]
        f_tile = first_ref[h, i * nsub]
        j_start = lax.while_loop(lambda j: (j < i * nsub) & (f_tile - last_ref[h, j] < gap_needed),
                                 lambda j: j + 1, 0)
        r = lax.broadcasted_iota(jnp.int32, (blk, blk), 0)
        c = lax.broadcasted_iota(jnp.int32, (blk, blk), 1)
        qs = [q_ref[0, s * blk:(s + 1) * blk, :].astype(BF16) for s in range(nsub)]
        fqs = [fr_ref[0, i * nsub + s] for s in range(nsub)]

        def load_kv(j):
            off = pl.multiple_of(j * blk, blk)
            return k_ref[0, pl.ds(off, blk), :], vt_ref[0, j], fc_ref[0, pl.ds(off, blk), :]

        def tile(kv, s, carry, diagonal):
            kj, vtj, fk = kv
            m, l, acc = carry
            st = _raw_dot(kj, qs[s], 1, 1, False) + fqs[s] - fk
            if diagonal:
                st = jnp.where(r <= c, st, NEG_BIG)
            m_new = jnp.maximum(m, jnp.max(st, axis=0, keepdims=True))
            p = jnp.exp(st - m_new)
            alpha = jnp.exp(m - m_new)
            l = alpha * l + jnp.sum(p, axis=0, keepdims=True)
            acc = alpha * acc + _raw_dot(vtj, p, 1, 0, False)
            return m_new, l, acc

        def below_diagonal(j, carry):
            kv = load_kv(j)
            return tuple(tile(kv, s, carry[s], False) for s in range(nsub))

        init = tuple((jnp.full((1, blk), NEG_BIG, F32), jnp.zeros((1, blk), F32), jnp.zeros((hd, blk), F32))
                     for _ in range(nsub))
        carry = list(lax.fori_loop(j_start, i * nsub, below_diagonal, init))
        for d in range(nsub):
            kv = load_kv(i * nsub + d)
            for s in range(d, nsub):
                carry[s] = tile(kv, s, carry[s], s == d)
        for s, (m, l, acc) in enumerate(carry):
            o_ref[0, :, s * blk:(s + 1) * blk] = acc / l
            lse_ref[0, s] = m + jnp.log(l)

    vt = v.reshape(h_total, nb, blk, hd).transpose(0, 1, 3, 2).astype(BF16)
    return _call(
        body, name="fox_fwd", grid=(h_total, nb // nsub),
        in_specs=[SMEM, SMEM, SMEM,
                  pl.BlockSpec((1, tq, hd), lambda h, i: (h, i, 0)),
                  pl.BlockSpec((1, t_total, hd), lambda h, i: (h, 0, 0)),
                  pl.BlockSpec((1, nb, hd, blk), lambda h, i: (h, 0, 0, 0)),
                  pl.BlockSpec((1, t_total, 1), lambda h, i: (h, 0, 0)),
                  pl.BlockSpec((1, nb, 1, blk), lambda h, i: (h, 0, 0, 0))],
        out_specs=[pl.BlockSpec((1, hd, tq), lambda h, i: (h, 0, i)),
                   pl.BlockSpec((1, nsub, 1, blk), lambda h, i: (h, i, 0, 0))],
        out_shape=[jax.ShapeDtypeStruct((h_total, hd, t_total), F32), jax.ShapeDtypeStruct(f_row.shape, F32)],
        compiler_params=_cparams(("parallel", "parallel")),
    )(*tables, q, k.astype(BF16), vt, f_col, f_row)


def _attn_bwd_call(q, k, v, f_col, f_row, tables, lse_row, delta_row, do_blk):
    h_total, t_total, hd = q.shape
    blk = f_row.shape[-1]
    nb = t_total // blk
    nsub = _attn_sub_tiles(nb)
    tkv = nsub * blk

    def body(thr_ref, first_ref, last_ref, q_ref, do_ref, k_ref, v_ref, fc_ref, fr_ref, lse_ref, dl_ref,
             dq_ref, dk_ref, dv_ref, dfk_ref, dfq_ref):
        h = pl.program_id(0)
        j = pl.program_id(1)
        gap_needed = thr_ref[h, 0]
        f_tile = last_ref[h, j * nsub + nsub - 1]
        i_stop = lax.while_loop(lambda i: (i < nb) & (first_ref[h, jnp.minimum(i, nb - 1)] - f_tile >= gap_needed),
                                lambda i: i + 1, (j + 1) * nsub)

        @pl.when(j == 0)
        def _():
            dq_ref[...] = jnp.zeros_like(dq_ref)
            dfq_ref[...] = jnp.zeros_like(dfq_ref)

        ks = [k_ref[0, s * blk:(s + 1) * blk, :].astype(BF16) for s in range(nsub)]
        vs = [v_ref[0, s * blk:(s + 1) * blk, :].astype(BF16) for s in range(nsub)]
        fks = [fc_ref[0, s * blk:(s + 1) * blk, :] for s in range(nsub)]
        r = lax.broadcasted_iota(jnp.int32, (blk, blk), 0)
        c = lax.broadcasted_iota(jnp.int32, (blk, blk), 1)

        def q_step(i, accs, subs):
            off = pl.multiple_of(i * blk, blk)
            qi = q_ref[0, pl.ds(off, blk), :]
            doi = do_ref[0, i]
            fq, lse, dl = fr_ref[0, i], lse_ref[0, i], dl_ref[0, i]
            accs = list(accs)
            dq_i, dfq_i = None, None
            for s, diagonal in subs:
                dk, dv, dfk = accs[s]
                st = _raw_dot(ks[s], qi, 1, 1, False) + fq - fks[s] - lse
                if diagonal:
                    st = jnp.where(r <= c, st, NEG_BIG)
                pt = jnp.exp(st)
                dv = dv + _raw_dot(pt, doi, 1, 1, False)
                dst = pt * (_raw_dot(vs[s], doi, 1, 0, False) - dl)
                dk = dk + _raw_dot(dst, qi, 1, 0, False)
                dfk = dfk - jnp.sum(dst, axis=1, keepdims=True)
                accs[s] = (dk, dv, dfk)
                dq_s = _raw_dot(dst, ks[s], 0, 0, False)
                dfq_s = jnp.sum(dst, axis=0, keepdims=True)
                dq_i = dq_s if dq_i is None else dq_i + dq_s
                dfq_i = dfq_s if dfq_i is None else dfq_i + dfq_s
            dfq_ref[0, i] += dfq_i
            dq_ref[0, pl.ds(off, blk), :] += dq_i
            return tuple(accs)

        accs = tuple((jnp.zeros((blk, hd), F32), jnp.zeros((blk, hd), F32), jnp.zeros((blk, 1), F32))
                     for _ in range(nsub))
        for d in range(nsub):
            accs = q_step(j * nsub + d, accs, [(s, s == d) for s in range(d + 1)])
        accs = lax.fori_loop((j + 1) * nsub, i_stop,
                             lambda i, a: q_step(i, a, [(s, False) for s in range(nsub)]), accs)
        for s, (dk, dv, dfk) in enumerate(accs):
            dk_ref[0, s * blk:(s + 1) * blk, :] = dk
            dv_ref[0, s * blk:(s + 1) * blk, :] = dv
            dfk_ref[0, s * blk:(s + 1) * blk, :] = dfk

    full = pl.BlockSpec((1, t_total, hd), lambda h, j: (h, 0, 0))
    tile = pl.BlockSpec((1, tkv, hd), lambda h, j: (h, j, 0))
    col = pl.BlockSpec((1, tkv, 1), lambda h, j: (h, j, 0))
    rows = pl.BlockSpec((1, nb, 1, blk), lambda h, j: (h, 0, 0, 0))
    do_blocks = pl.BlockSpec((1, nb, hd, blk), lambda h, j: (h, 0, 0, 0))
    return _call(
        body, name="fox_bwd", grid=(h_total, nb // nsub),
        in_specs=[SMEM, SMEM, SMEM, full, do_blocks, tile, tile, col, rows, rows, rows],
        out_specs=[full, tile, tile, col, rows],
        out_shape=[jax.ShapeDtypeStruct(q.shape, F32), jax.ShapeDtypeStruct(q.shape, F32),
                   jax.ShapeDtypeStruct(q.shape, F32), jax.ShapeDtypeStruct(f_col.shape, F32),
                   jax.ShapeDtypeStruct(f_row.shape, F32)],
        compiler_params=_cparams(("parallel", "arbitrary")),
    )(*tables, q.astype(BF16), do_blk, k, v, f_col, f_row, lse_row, delta_row)


def _attn_delta_call(do_t, o_t):
    h_total, hd, t_total = o_t.shape
    tb = _tile(t_total, 2816, LANE)

    def body(do_ref, o_ref, d_ref):
        d_ref[0] = jnp.sum(do_ref[0] * o_ref[0], axis=0, keepdims=True)

    spec = pl.BlockSpec((1, hd, tb), lambda h, i: (h, 0, i))
    return _call(
        body, name="fox_delta", grid=(h_total, t_total // tb), in_specs=[spec, spec],
        out_specs=pl.BlockSpec((1, 1, tb), lambda h, i: (h, 0, i)),
        out_shape=jax.ShapeDtypeStruct((h_total, 1, t_total), F32),
        compiler_params=_cparams(("parallel", "parallel")),
    )(do_t, o_t)


@jax.custom_vjp
def fox_attention(q, k, v, f_col, f_row):
    return _attn_fwd_call(q, k, v, f_col, f_row, _attn_skip_tables(q, k, f_row))[0]


def _fox_fwd(q, k, v, f_col, f_row):
    tables = _attn_skip_tables(q, k, f_row)
    o_t, lse_row = _attn_fwd_call(q, k, v, f_col, f_row, tables)
    return o_t, (q, k, v, f_col, f_row, tables, o_t, lse_row)


def _fox_bwd(res, do_t):
    q, k, v, f_col, f_row, tables, o_t, lse_row = res
    h_total, t_total, hd = q.shape
    nb, blk = f_row.shape[1], f_row.shape[3]
    delta = _attn_delta_call(do_t, o_t).reshape(f_row.shape)
    do_blk = do_t.reshape(h_total, hd, nb, blk).transpose(0, 2, 1, 3).astype(BF16)
    return tuple(_attn_bwd_call(q, k, v, f_col, f_row, tables, lse_row, delta, do_blk))


fox_attention.defvjp(_fox_fwd, _fox_bwd)


def _head_col(blk, h):
    lane = lax.broadcasted_iota(jnp.int32, blk.shape, 1)
    return jnp.sum(jnp.where(lane == h, blk, 0.0), axis=1, keepdims=True)


@jax.custom_vjp
def _cat2(a, b):
    return jnp.concatenate([a, b], axis=1)


_cat2.defvjp(lambda a, b: (_cat2(a, b), a.shape[1]), lambda na, ct: (ct[:, :na], ct[:, na:]))


@jax.custom_vjp
def _split2(x):
    half = x.shape[1] // 2
    return x[:, :half], x[:, half:]


_split2.defvjp(lambda x: (_split2(x), None), lambda _, cts: (jnp.concatenate(cts, axis=1),))


def _gdn_intra(h, q, k, v, b_blk, g_blk):
    n = q.shape[0]
    b, g = _head_col(b_blk, h), _head_col(g_blk, h)
    r = lax.broadcasted_iota(jnp.int32, (n, n), 0)
    c = lax.broadcasted_iota(jnp.int32, (n, n), 1)
    same = (r // GDN_CHUNK) == (c // GDN_CHUNK)
    incl = same & (r >= c)
    g_row = jnp.sum(jnp.where(r == c, g, 0.0), axis=0, keepdims=True)
    big_g = jnp.sum(jnp.where(incl, g_row, 0.0), axis=1, keepdims=True)
    big_g_row = jnp.sum(jnp.where(same & (r <= c), g, 0.0), axis=0, keepdims=True)
    g_tot = jnp.sum(jnp.where(same, g_row, 0.0), axis=1, keepdims=True)
    dec = jnp.where(incl, jnp.exp(jnp.where(incl, big_g - big_g_row, 0.0)), 0.0)
    dec_strict = jnp.where(r > c, dec, 0.0)
    e_g = jnp.exp(big_g)
    kb = k * b
    m = _dot(kb, k, 1, 1) * dec_strict
    rs = lax.broadcasted_iota(jnp.int32, (n, GDN_CHUNK), 0)
    cs = lax.broadcasted_iota(jnp.int32, (n, GDN_CHUNK), 1)
    fold = jnp.where(rs % GDN_CHUNK == cs, 1.0, 0.0).astype(F32)
    aqk = _dot(_dot(q, k, 1, 1) * dec, fold, 1, 0, True)
    x = _cat2(v * b, kb * e_g)
    x = x - _dot(m, x)
    p = m
    steps = 1
    while 2 * steps < GDN_CHUNK:
        p = _dot(p, p)
        x = x + _dot(p, x)
        steps *= 2
    u, w = _split2(x)
    lane = lax.broadcasted_iota(jnp.int32, b_blk.shape, 1)
    return u, w, q * e_g, k * jnp.exp(g_tot - big_g), aqk, jnp.where(lane == h, g_tot, 0.0)


def _gdn_rec(h, s, u, w, qg, kd, aqk, gl_blk):
    g_last = jnp.max(_head_col(gl_blk, h), axis=0, keepdims=True)
    big_u = u - _dot(w, s)
    o = _dot(qg, s) + _dot(aqk, big_u)
    s_next = s * jnp.exp(g_last) + _dot(kd, big_u, 0, 0)
    return o, s_next


GDN_TOK_BLK = 256


def _gdn_layout(t_total, rev):
    tb = _tile(t_total, GDN_TOK_BLK, GDN_CHUNK)
    cb, nblk = tb // GDN_CHUNK, t_total // tb
    pos = (lambda i: nblk - 1 - i) if rev else (lambda i: i)
    specs = dict(
        tok=pl.BlockSpec((tb, GDN_W), lambda i: (pos(i), 0)),
        gate=pl.BlockSpec((tb, GDN_HEADS), lambda i: (pos(i), 0)),
        aqk=pl.BlockSpec((GDN_HEADS, tb, GDN_CHUNK), lambda i: (0, pos(i), 0)),
        state=pl.BlockSpec((GDN_HEADS, cb, GDN_HD, GDN_HD), lambda i: (0, pos(i), 0, 0)))
    return cb, nblk, specs


def _gdn_shapes(t_total):
    n_chunks = t_total // GDN_CHUNK
    return dict(tok=jax.ShapeDtypeStruct((t_total, GDN_W), F32),
                gate=jax.ShapeDtypeStruct((t_total, GDN_HEADS), F32),
                aqk=jax.ShapeDtypeStruct((GDN_HEADS, t_total, GDN_CHUNK), F32),
                state=jax.ShapeDtypeStruct((GDN_HEADS, n_chunks, GDN_HD, GDN_HD), F32))


def _chunk_rows(ci):
    return pl.ds(pl.multiple_of(ci * GDN_CHUNK, GDN_CHUNK), GDN_CHUNK)


def _head_cols(h):
    return pl.ds(h * GDN_HD, GDN_HD)


def _gdn_intra_fwd_call(q, k, v, b, g):
    cb, nblk, sp = _gdn_layout(q.shape[0], False)
    sh = _gdn_shapes(q.shape[0])

    def body(q_ref, k_ref, v_ref, b_ref, g_ref, u_ref, w_ref, qg_ref, kd_ref, aqk_ref, gl_ref):
        b_blk, g_blk = b_ref[...], g_ref[...]
        gl = jnp.zeros(b_blk.shape, F32)
        for h in range(GDN_HEADS):
            cols = _head_cols(h)
            u, w, qg, kd, aqk, gl_h = _gdn_intra(h, q_ref[:, cols], k_ref[:, cols], v_ref[:, cols], b_blk, g_blk)
            u_ref[:, cols] = u
            w_ref[:, cols] = w
            qg_ref[:, cols] = qg
            kd_ref[:, cols] = kd
            aqk_ref[h] = aqk
            gl = gl + gl_h
        gl_ref[...] = gl

    return _call(
        body, name="gdn_intra_fwd", grid=(nblk,),
        in_specs=[sp["tok"]] * 3 + [sp["gate"]] * 2,
        out_specs=[sp["tok"]] * 4 + [sp["aqk"], sp["gate"]],
        out_shape=[sh["tok"]] * 4 + [sh["aqk"], sh["gate"]],
        compiler_params=_cparams(("parallel",)),
    )(q, k, v, b, g)


def _gdn_intra_bwd_call(q, k, v, b, g, du, dw, dqg, dkd, daqk, dgl):
    cb, nblk, sp = _gdn_layout(q.shape[0], False)
    sh = _gdn_shapes(q.shape[0])

    def body(q_ref, k_ref, v_ref, b_ref, g_ref, du_ref, dw_ref, dqg_ref, dkd_ref, daqk_ref, dgl_ref,
             dq_ref, dk_ref, dv_ref, db_ref, dg_ref):
        b_blk, g_blk, dgl = b_ref[...], g_ref[...], dgl_ref[...]
        db = jnp.zeros(b_blk.shape, F32)
        dg = jnp.zeros(b_blk.shape, F32)
        for h in range(GDN_HEADS):
            cols = _head_cols(h)
            _, vjp = jax.vjp(functools.partial(_gdn_intra, h), q_ref[:, cols], k_ref[:, cols], v_ref[:, cols],
                             b_blk, g_blk)
            dq, dk, dv, db_h, dg_h = vjp((du_ref[:, cols], dw_ref[:, cols], dqg_ref[:, cols], dkd_ref[:, cols],
                                          daqk_ref[h], dgl))
            dq_ref[:, cols] = dq
            dk_ref[:, cols] = dk
            dv_ref[:, cols] = dv
            db = db + db_h
            dg = dg + dg_h
        db_ref[...] = db
        dg_ref[...] = dg

    return _call(
        body, name="gdn_intra_bwd", grid=(nblk,),
        in_specs=[sp["tok"]] * 3 + [sp["gate"]] * 2 + [sp["tok"]] * 4 + [sp["aqk"], sp["gate"]],
        out_specs=[sp["tok"]] * 3 + [sp["gate"]] * 2,
        out_shape=[sh["tok"]] * 3 + [sh["gate"]] * 2,
        compiler_params=_cparams(("parallel",)),
    )(q, k, v, b, g, du, dw, dqg, dkd, daqk, dgl)


def _gdn_rec_fwd_call(u, w, qg, kd, aqk, gl):
    cb, nblk, sp = _gdn_layout(u.shape[0], False)
    sh = _gdn_shapes(u.shape[0])

    def body(u_ref, w_ref, qg_ref, kd_ref, aqk_ref, gl_ref, o_ref, s_all_ref, s_ref):
        @pl.when(pl.program_id(0) == 0)
        def _():
            s_ref[...] = jnp.zeros_like(s_ref)

        def chunk(ci, carry):
            rows = _chunk_rows(ci)
            gl_row = gl_ref[rows, :]
            states = [s_ref[h] for h in range(GDN_HEADS)]
            res = [_gdn_rec(h, states[h], u_ref[rows, _head_cols(h)], w_ref[rows, _head_cols(h)],
                            qg_ref[rows, _head_cols(h)], kd_ref[rows, _head_cols(h)], aqk_ref[h, rows, :], gl_row)
                   for h in range(GDN_HEADS)]
            for h, (o, s_next) in enumerate(res):
                s_all_ref[h, ci] = states[h]
                o_ref[rows, _head_cols(h)] = o
                s_ref[h] = s_next
            return carry

        lax.fori_loop(0, cb, chunk, 0)

    return _call(
        body, name="gdn_rec_fwd", grid=(nblk,),
        in_specs=[sp["tok"]] * 4 + [sp["aqk"], sp["gate"]],
        out_specs=[sp["tok"], sp["state"]], out_shape=[sh["tok"], sh["state"]],
        scratch_shapes=[pltpu.VMEM((GDN_HEADS, GDN_HD, GDN_HD), F32)],
        compiler_params=_cparams(("arbitrary",)),
    )(u, w, qg, kd, aqk, gl)


def _gdn_rec_bwd_call(u, w, qg, kd, aqk, gl, s_all, do):
    cb, nblk, sp = _gdn_layout(u.shape[0], True)
    sh = _gdn_shapes(u.shape[0])

    def body(u_ref, w_ref, qg_ref, kd_ref, aqk_ref, gl_ref, s_all_ref, do_ref,
             du_ref, dw_ref, dqg_ref, dkd_ref, daqk_ref, dgl_ref, ds_ref):
        @pl.when(pl.program_id(0) == 0)
        def _():
            ds_ref[...] = jnp.zeros_like(ds_ref)

        def chunk(step, carry):
            ci = cb - 1 - step
            rows = _chunk_rows(ci)
            gl_row = gl_ref[rows, :]
            res = []
            for h in range(GDN_HEADS):
                cols = _head_cols(h)
                _, vjp = jax.vjp(functools.partial(_gdn_rec, h), s_all_ref[h, ci], u_ref[rows, cols],
                                 w_ref[rows, cols], qg_ref[rows, cols], kd_ref[rows, cols], aqk_ref[h, rows, :],
                                 gl_row)
                res.append(vjp((do_ref[rows, cols], ds_ref[h])))
            dgl = jnp.zeros((GDN_CHUNK, GDN_HEADS), F32)
            for h, (ds, du, dw, dqg, dkd, daqk, dgl_h) in enumerate(res):
                cols = _head_cols(h)
                ds_ref[h] = ds
                du_ref[rows, cols] = du
                dw_ref[rows, cols] = dw
                dqg_ref[rows, cols] = dqg
                dkd_ref[rows, cols] = dkd
                daqk_ref[h, rows, :] = daqk
                dgl = dgl + dgl_h
            dgl_ref[rows, :] = dgl
            return carry

        lax.fori_loop(0, cb, chunk, 0)

    return _call(
        body, name="gdn_rec_bwd", grid=(nblk,),
        in_specs=[sp["tok"]] * 4 + [sp["aqk"], sp["gate"], sp["state"], sp["tok"]],
        out_specs=[sp["tok"]] * 4 + [sp["aqk"], sp["gate"]],
        out_shape=[sh["tok"]] * 4 + [sh["aqk"], sh["gate"]],
        scratch_shapes=[pltpu.VMEM((GDN_HEADS, GDN_HD, GDN_HD), F32)],
        compiler_params=_cparams(("arbitrary",)),
    )(u, w, qg, kd, aqk, gl, s_all, do)


@jax.custom_vjp
def gdn_intra(q, k, v, b, g):
    return tuple(_gdn_intra_fwd_call(q, k, v, b, g))


gdn_intra.defvjp(lambda *a: (gdn_intra(*a), a), lambda res, cts: tuple(_gdn_intra_bwd_call(*res, *cts)))


@jax.custom_vjp
def gdn_rec(u, w, qg, kd, aqk, gl):
    return _gdn_rec_fwd_call(u, w, qg, kd, aqk, gl)[0]


def _gdn_rec_fwd(*a):
    o, s_all = _gdn_rec_fwd_call(*a)
    return o, a + (s_all,)


gdn_rec.defvjp(_gdn_rec_fwd, lambda res, do: tuple(_gdn_rec_bwd_call(*res, do)))


def gated_delta(q, k, v, b, g):
    return gdn_rec(*gdn_intra(q, k, v, b, g))


def _loss_call(y, tgt, first, last):
    r_total, d = y.shape
    tm = _tile(r_total, 256, 8)

    def body(y_ref, t_ref, loss_ref, dy_ref):
        i = pl.program_id(0)

        @pl.when(i == 0)
        def _():
            loss_ref[...] = jnp.zeros_like(loss_ref)

        row = lax.broadcasted_iota(jnp.int32, (tm, d), 0) + i * tm
        err = jnp.where((row >= first) & (row < last), y_ref[...] - t_ref[...], 0.0)
        dy_ref[...] = err * (1.0 / d)
        part = jnp.sum(jnp.sum(err * err, axis=1, keepdims=True), axis=0, keepdims=True) * (0.5 / d)
        loss_ref[...] += jnp.broadcast_to(part, loss_ref.shape)

    return _call(
        body, name="loss_head", grid=(r_total // tm,),
        in_specs=[pl.BlockSpec((tm, d), lambda i: (i, 0))] * 2,
        out_specs=[pl.BlockSpec((8, LANE), lambda i: (0, 0)), pl.BlockSpec((tm, d), lambda i: (i, 0))],
        out_shape=[jax.ShapeDtypeStruct((8, LANE), F32), jax.ShapeDtypeStruct(y.shape, F32)],
        compiler_params=_cparams(("arbitrary",)),
    )(y, tgt)


def make_loss(first, last):
    @jax.custom_vjp
    def op(y, tgt):
        return _loss_call(y, tgt, first, last)[0][0, 0]

    def fwd(y, tgt):
        loss, dy = _loss_call(y, tgt, first, last)
        return loss[0, 0], (dy,)

    def bwd(res, ct):
        return res[0] * ct, jnp.zeros_like(res[0])

    op.defvjp(fwd, bwd)
    return op


def _pad_rows8(w):
    return jnp.concatenate([w, jnp.zeros((8 - w.shape[0], w.shape[1]), w.dtype)], axis=0)


def local_loss(wts, x, tgt):
    seq = x.shape[0]
    n_tok = N_META + seq
    t_pad = -(-n_tok // ROW_ALIGN) * ROW_ALIGN
    depth = wts["norm1_g"].shape[0]
    blk = _tile(t_pad, ATT_BLK, LANE)
    nb = t_pad // blk
    tm = _tile(t_pad, 256, 8)

    rms = rowop(_f_rmsnorm, "rmsnorm", (D_MODEL,), tm)
    qnorm = rowop(_f_qnorm, "fox_q_norm", (FOX_HD,), _tile(FOX_HEADS * t_pad, 2048, 8))
    knorm = rowop(_f_rmsnorm, "fox_k_norm", (FOX_HD,), _tile(FOX_HEADS * t_pad, 2048, 8))
    logsig = rowop(_f_logsig, "fox_log_forget", (FOX_HEADS,), tm)
    tm_head = _tile(t_pad, 768, 8)
    gdn_q = rowop(_f_gdn_q, "gdn_q_act", (GDN_HD,), tm_head, ncb=GDN_HEADS)
    gdn_k = rowop(_f_gdn_k, "gdn_k_act", (GDN_HD,), tm_head, ncb=GDN_HEADS)
    silu = rowop(_f_silu, "gdn_v_act", (GDN_W,), tm)
    gates = rowop(_f_gdn_gates, "gdn_gates", (GDN_HEADS, GDN_HEADS), tm)
    gdn_out = rowop(_f_gdn_out, "gdn_out_norm", (GDN_HD,), tm_head, ncb=GDN_HEADS)
    merge = rowop(_f_merge, "branch_merge", (D_MODEL,), tm)
    glu = rowop(_f_glu, "ffn_glu", (D_FF // 2,), tm, ncb=2)
    residual = rowop(_f_residual, "residual_add", (D_MODEL,), tm, bc=(2,))
    keep = (jnp.arange(t_pad)[:, None] < n_tok).astype(F32)
    conv4 = make_dwconv(GDN_CONV)
    conv3 = make_dwconv(FFN_CONV)
    loss_op = make_loss(N_META, n_tok)

    zeros = jnp.zeros((t_pad - n_tok, D_MODEL), F32)
    h_res = jnp.concatenate([wts["meta_tokens"], x, zeros], axis=0)
    tgt_rows = jnp.concatenate([jnp.zeros((N_META, D_MODEL), F32), tgt, zeros], axis=0)

    def heads(a):
        return a.reshape(t_pad, FOX_HEADS, FOX_HD).transpose(1, 0, 2).reshape(FOX_HEADS * t_pad, FOX_HD)

    for l in range(depth):
        h = rms((h_res,), (wts["norm1_g"][l][None],))[0]
        proj = mm(h, wts["w_in"][l])
        qn = qnorm((heads(proj[:, 0:512]),), (wts["fox_q_norm_g"][l][None],))[0]
        kn = knorm((heads(proj[:, 512:1024]),), (wts["fox_k_norm_g"][l][None],))[0]
        vh = heads(proj[:, 1024:1536])
        log_f = logsig((proj[:, 1536:1544],), (wts["fox_f_bias"][l][None],))[0]
        f_cum = cumsum_lanes(log_f.T)
        o_a = fox_attention(qn.reshape(FOX_HEADS, t_pad, FOX_HD), kn.reshape(FOX_HEADS, t_pad, FOX_HD),
                            vh.reshape(FOX_HEADS, t_pad, FOX_HD), f_cum[:, :, None],
                            f_cum.reshape(FOX_HEADS, nb, 1, blk))
        y_a = mm(o_a.transpose(2, 0, 1).reshape(t_pad, FOX_W), wts["w_branch_a"][l])
        cv = conv4(proj[:, 1664:4736], _pad_rows8(wts["gdn_conv_w"][l]))
        gq = gdn_q((cv[:, 0:GDN_W],), ())[0]
        gk = gdn_k((cv[:, GDN_W:2 * GDN_W],), ())[0]
        gv = silu((cv[:, 2 * GDN_W:],), ())[0]
        beta, gdec = gates((proj[:, 4736:4744], proj[:, 4744:4752]),
                           (wts["gdn_a_log"][l][None], wts["gdn_dt_bias"][l][None]))
        o_b = gated_delta(gq, gk, gv, beta, gdec)
        o_b = gdn_out((o_b, proj[:, 4864:5888]), (wts["gdn_norm_g"][l][None],))[0]
        y_b = mm(o_b, wts["w_branch_b"][l])
        mixed = merge((proj[:, 5888:6912], proj[:, 6912:7936], y_a, y_b), ())[0]
        h_res = residual((h_res, mm(mixed, wts["w_out"][l]), keep), ())[0]
        h = rms((h_res,), (wts["norm2_g"][l][None],))[0]
        up = conv3(mm(h, wts["w_up"][l]), _pad_rows8(wts["ffn_conv_w"][l]))
        act = glu((up[:, :D_FF], up[:, D_FF:]), ())[0]
        h_res = residual((h_res, mm(act, wts["w_down"][l]), keep), ())[0]
    return loss_op(h_res, tgt_rows)


def pad_w_in(w):
    parts, pos = [], 0
    for src, width, dst in IN_SEGS:
        if dst > pos:
            parts.append(jnp.zeros(w.shape[:-1] + (dst - pos,), w.dtype))
        parts.append(w[..., src:src + width])
        pos = dst + width
    parts.append(jnp.zeros(w.shape[:-1] + (D_IN_PAD - pos,), w.dtype))
    return jnp.concatenate(parts, axis=-1)


def unpad_w_in(w):
    return jnp.concatenate([w[..., dst:dst + width] for _, width, dst in IN_SEGS], axis=-1)


ANY = pl.BlockSpec(memory_space=pl.ANY)
N_CHIPS = 4
N_DEV = 8
COMM_COLS = 1024
COMM_ROW_ALIGN = 512
COMM_ROW_ALIGN_SMALL = 32


def _place():
    return lax.axis_index("x"), lax.axis_index("y"), lax.axis_index("c")


def _other_chips(x, y):
    return [(1 - x, y), (x, 1 - y), (1 - x, 1 - y)]


def _remote(src, dst, send_sem, recv_sem, dev):
    return pltpu.make_async_remote_copy(src_ref=src, dst_ref=dst, send_sem=send_sem, recv_sem=recv_sem,
                                        device_id=dev, device_id_type=MESH)


def chip_all_gather(buf):
    rows, cols = buf.shape
    half = rows // 2

    def body(x_ref, out_ref, send_sems, recv_sems, pass_send, pass_recv, local_sem):
        x, y, c = _place()
        me = 2 * x + y
        mine, other = pl.ds(c * half, half), pl.ds((1 - c) * half, half)
        sibling = (x, y, 1 - c)
        chips = _other_chips(x, y)
        local = pltpu.make_async_copy(x_ref, out_ref.at[me], local_sem)
        local.start()
        started = []
        for k, (px, py) in enumerate(chips):
            cp = _remote(x_ref.at[mine], out_ref.at[me, mine], send_sems.at[k], recv_sems.at[k], (px, py, c))
            cp.start()
            started.append(cp)
        for k, (px, py) in enumerate(chips):
            landed = out_ref.at[2 * px + py, mine]
            _remote(landed, landed, send_sems.at[k], recv_sems.at[k], (px, py, c)).wait_recv()
            cp = _remote(landed, landed, pass_send.at[k], pass_recv.at[k], sibling)
            cp.start()
            started.append(cp)
        for k, (px, py) in enumerate(chips):
            passed = out_ref.at[2 * px + py, other]
            _remote(passed, passed, pass_send.at[k], pass_recv.at[k], sibling).wait_recv()
        for cp in started:
            cp.wait_send()
        local.wait()

    return _call(
        body, name="chip_all_gather", in_specs=[ANY], out_specs=ANY,
        out_shape=jax.ShapeDtypeStruct((N_CHIPS, rows, cols), buf.dtype),
        scratch_shapes=[pltpu.SemaphoreType.DMA((3,)), pltpu.SemaphoreType.DMA((3,)),
                        pltpu.SemaphoreType.DMA((3,)), pltpu.SemaphoreType.DMA((3,)), pltpu.SemaphoreType.DMA],
    )(buf)


def sibling_swap_halves(g4):
    n, rows, cols = g4.shape
    half = rows // 2

    def body(g_ref, got_ref, send_sem, recv_sem):
        x, y, c = _place()
        cp = _remote(g_ref.at[:, pl.ds((1 - c) * half, half), :], got_ref, send_sem, recv_sem, (x, y, 1 - c))
        cp.start()
        cp.wait()

    return _call(
        body, name="sibling_swap_halves", in_specs=[ANY], out_specs=ANY,
        out_shape=jax.ShapeDtypeStruct((n, half, cols), g4.dtype),
        scratch_shapes=[pltpu.SemaphoreType.DMA, pltpu.SemaphoreType.DMA],
    )(g4)


def add_own_half(g4, got, c):
    n, rows, cols = g4.shape
    half = rows // 2
    tm = _tile(half, 256, 16)
    nt = half // tm

    def body(c_ref, a_ref, b_ref, o_ref):
        o_ref[...] = (a_ref[...] + b_ref[...]).astype(o_ref.dtype)

    return _call(
        body, name="add_own_half",
        grid_spec=pltpu.PrefetchScalarGridSpec(
            num_scalar_prefetch=1, grid=(n, nt),
            in_specs=[pl.BlockSpec((1, tm, cols), lambda j, i, c_ref: (j, c_ref[0] * nt + i, 0)),
                      pl.BlockSpec((1, tm, cols), lambda j, i, c_ref: (j, i, 0))],
            out_specs=pl.BlockSpec((1, tm, cols), lambda j, i, c_ref: (j, i, 0))),
        out_shape=jax.ShapeDtypeStruct(got.shape, BF16),
        compiler_params=_cparams(("parallel", "parallel")),
    )(c.reshape(1).astype(jnp.int32), g4, got)


def chip_scatter(p4):
    n, rows, cols = p4.shape

    def body(p_ref, out_ref, send_sems, recv_sems, local_sem):
        x, y, c = _place()
        me = 2 * x + y
        chips = _other_chips(x, y)
        local = pltpu.make_async_copy(p_ref.at[me], out_ref.at[me], local_sem)
        local.start()
        started = []
        for k, (px, py) in enumerate(chips):
            cp = _remote(p_ref.at[2 * px + py], out_ref.at[me], send_sems.at[k], recv_sems.at[k], (px, py, c))
            cp.start()
            started.append(cp)
        for k, (px, py) in enumerate(chips):
            landed = out_ref.at[2 * px + py]
            _remote(landed, landed, send_sems.at[k], recv_sems.at[k], (px, py, c)).wait_recv()
        for cp in started:
            cp.wait_send()
        local.wait()

    return _call(
        body, name="chip_scatter", in_specs=[ANY], out_specs=ANY,
        out_shape=jax.ShapeDtypeStruct(p4.shape, p4.dtype),
        scratch_shapes=[pltpu.SemaphoreType.DMA((3,)), pltpu.SemaphoreType.DMA((3,)), pltpu.SemaphoreType.DMA],
    )(p4)


def sum_slots(a):
    n, rows, cols = a.shape
    tm = _tile(rows, 256, 16) if rows % 16 == 0 else rows

    def body(a_ref, o_ref):
        acc = a_ref[0].astype(F32)
        for k in range(1, n):
            acc = acc + a_ref[k].astype(F32)
        o_ref[...] = acc

    return _call(
        body, name="sum_slots_%d" % n, grid=(rows // tm,),
        in_specs=[pl.BlockSpec((n, tm, cols), lambda i: (0, i, 0))],
        out_specs=pl.BlockSpec((tm, cols), lambda i: (i, 0)),
        out_shape=jax.ShapeDtypeStruct((rows, cols), F32),
        compiler_params=_cparams(("parallel",)),
    )(a)


def sibling_join(s):
    half, cols = s.shape

    def body(s_ref, out_ref, send_sem, recv_sem, local_sem):
        x, y, c = _place()
        mine, other = pl.ds(c * half, half), pl.ds((1 - c) * half, half)
        local = pltpu.make_async_copy(s_ref, out_ref.at[mine], local_sem)
        local.start()
        cp = _remote(s_ref, out_ref.at[mine], send_sem, recv_sem, (x, y, 1 - c))
        cp.start()
        _remote(s_ref, out_ref.at[other], send_sem, recv_sem, (x, y, 1 - c)).wait_recv()
        cp.wait_send()
        local.wait()

    return _call(
        body, name="sibling_join", in_specs=[ANY], out_specs=ANY,
        out_shape=jax.ShapeDtypeStruct((2 * half, cols), s.dtype),
        scratch_shapes=[pltpu.SemaphoreType.DMA, pltpu.SemaphoreType.DMA, pltpu.SemaphoreType.DMA],
    )(s)


def all_devices_gather(buf):
    rows, cols = buf.shape

    def body(b_ref, out_ref, send_sems, recv_sems, local_sem):
        x, y, c = _place()
        me = 4 * x + 2 * y + c
        local = pltpu.make_async_copy(b_ref, out_ref.at[me], local_sem)
        local.start()
        peers = [((x + dx) % 2, (y + dy) % 2, (c + dc) % 2)
                 for dx in (0, 1) for dy in (0, 1) for dc in (0, 1) if dx + dy + dc > 0]
        started = []
        for k, peer in enumerate(peers):
            cp = _remote(b_ref, out_ref.at[me], send_sems.at[k], recv_sems.at[k], peer)
            cp.start()
            started.append(cp)
        for k, (px, py, pc) in enumerate(peers):
            landed = out_ref.at[4 * px + 2 * py + pc]
            _remote(landed, landed, send_sems.at[k], recv_sems.at[k], (px, py, pc)).wait_recv()
        for cp in started:
            cp.wait_send()
        local.wait()

    return _call(
        body, name="all_devices_gather", in_specs=[ANY], out_specs=ANY,
        out_shape=jax.ShapeDtypeStruct((N_DEV, rows, cols), buf.dtype),
        scratch_shapes=[pltpu.SemaphoreType.DMA((7,)), pltpu.SemaphoreType.DMA((7,)), pltpu.SemaphoreType.DMA],
    )(buf)


def adamw(w, g, m, v):
    shape = w.shape
    w2, g2, m2, v2 = [a.reshape(-1, shape[-1]) for a in (w, g, m, v)]
    rows, cols = w2.shape
    tm = _tile(rows, 256, 8) if rows % 8 == 0 else rows

    def body(w_ref, g_ref, m_ref, v_ref, d_ref, nm_ref, nv_ref):
        gv = g_ref[...]
        nm = ADAM_B1 * m_ref[...] + (1.0 - ADAM_B1) * gv
        nv = ADAM_B2 * v_ref[...] + (1.0 - ADAM_B2) * (gv * gv)
        m_hat = nm / (1.0 - ADAM_B1 ** ADAM_STEP)
        v_hat = nv / (1.0 - ADAM_B2 ** ADAM_STEP)
        d_ref[...] = -ADAM_LR * (m_hat / (jnp.sqrt(v_hat) + ADAM_EPS) + ADAM_WD * w_ref[...])
        nm_ref[...] = nm
        nv_ref[...] = nv

    spec = pl.BlockSpec((tm, cols), lambda i: (i, 0))
    outs = _call(
        body, name="adamw", grid=(rows // tm,), in_specs=[spec] * 4, out_specs=[spec] * 3,
        out_shape=[jax.ShapeDtypeStruct((rows, cols), F32)] * 3,
        compiler_params=_cparams(("parallel",)),
    )(w2, g2, m2, v2)
    return [o.reshape(shape) for o in outs]


WEIGHTS = ("meta_tokens", "norm1_g", "w_in", "fox_f_bias", "fox_q_norm_g", "fox_k_norm_g", "gdn_conv_w",
           "gdn_a_log", "gdn_dt_bias", "gdn_norm_g", "w_branch_a", "w_branch_b", "w_out", "norm2_g", "w_up",
           "ffn_conv_w", "w_down")
SHARD_AXIS = {"meta_tokens": -1, "w_in": -1, "gdn_conv_w": -1, "w_branch_a": -1, "w_branch_b": -2, "w_out": -2,
              "w_up": -1, "ffn_conv_w": -1, "w_down": -2}
MATMUL_WEIGHTS = ("w_in", "w_branch_a", "w_branch_b", "w_out", "w_up", "w_down")
SMALL_SHARDED = ("meta_tokens", "gdn_conv_w", "ffn_conv_w")
REPLICATED = tuple(n for n in WEIGHTS if n not in SHARD_AXIS)


def _pack(arrays, dtype, row_align):
    flat = jnp.concatenate([a.reshape(-1).astype(dtype) for a in arrays])
    block = row_align * COMM_COLS
    total = -(-flat.shape[0] // block) * block
    flat = jnp.concatenate([flat, jnp.zeros((total - flat.shape[0],), dtype)])
    return flat.reshape(-1, COMM_COLS)


def _unpack(buf, shapes):
    flat, out, pos = buf.reshape(-1), [], 0
    for shape in shapes:
        size = 1
        for d in shape:
            size *= d
        out.append(flat[pos:pos + size].reshape(shape))
        pos += size
    return out


def _gather_full(shards, names, dtype, row_align):
    got = chip_all_gather(_pack([shards[n] for n in names], dtype, row_align))
    per_chip = [_unpack(got[j], [shards[n].shape for n in names]) for j in range(N_CHIPS)]
    return {n: jnp.concatenate([per_chip[j][i] for j in range(N_CHIPS)], axis=SHARD_AXIS[n]).astype(F32)
            for i, n in enumerate(names)}


def _shard_of(full, name, j):
    axis = SHARD_AXIS[name] % full.ndim
    size = full.shape[axis] // N_CHIPS
    return lax.slice_in_dim(full, j * size, (j + 1) * size, axis=axis)


def kernel(x, meta_tokens, norm1_g, w_in, fox_f_bias, fox_q_norm_g, fox_k_norm_g, gdn_conv_w, gdn_a_log, gdn_dt_bias, gdn_norm_g, w_branch_a, w_branch_b, w_out, norm2_g, w_up, ffn_conv_w, w_down, loss_target, m_meta_tokens, m_norm1_g, m_w_in, m_fox_f_bias, m_fox_q_norm_g, m_fox_k_norm_g, m_gdn_conv_w, m_gdn_a_log, m_gdn_dt_bias, m_gdn_norm_g, m_w_branch_a, m_w_branch_b, m_w_out, m_norm2_g, m_w_up, m_ffn_conv_w, m_w_down, v_meta_tokens, v_norm1_g, v_w_in, v_fox_f_bias, v_fox_q_norm_g, v_fox_k_norm_g, v_gdn_conv_w, v_gdn_a_log, v_gdn_dt_bias, v_gdn_norm_g, v_w_branch_a, v_w_branch_b, v_w_out, v_norm2_g, v_w_up, v_ffn_conv_w, v_w_down):
    w_loc = dict(zip(WEIGHTS, (meta_tokens, norm1_g, w_in, fox_f_bias, fox_q_norm_g, fox_k_norm_g, gdn_conv_w,
                               gdn_a_log, gdn_dt_bias, gdn_norm_g, w_branch_a, w_branch_b, w_out, norm2_g, w_up,
                               ffn_conv_w, w_down)))
    m_loc = dict(zip(WEIGHTS, (m_meta_tokens, m_norm1_g, m_w_in, m_fox_f_bias, m_fox_q_norm_g, m_fox_k_norm_g,
                               m_gdn_conv_w, m_gdn_a_log, m_gdn_dt_bias, m_gdn_norm_g, m_w_branch_a, m_w_branch_b,
                               m_w_out, m_norm2_g, m_w_up, m_ffn_conv_w, m_w_down)))
    v_loc = dict(zip(WEIGHTS, (v_meta_tokens, v_norm1_g, v_w_in, v_fox_f_bias, v_fox_q_norm_g, v_fox_k_norm_g,
                               v_gdn_conv_w, v_gdn_a_log, v_gdn_dt_bias, v_gdn_norm_g, v_w_branch_a, v_w_branch_b,
                               v_w_out, v_norm2_g, v_w_up, v_ffn_conv_w, v_w_down)))
    c = lax.axis_index("c")

    full = {n: w_loc[n] for n in REPLICATED}
    full.update(_gather_full(w_loc, MATMUL_WEIGHTS, BF16, COMM_ROW_ALIGN))
    full.update(_gather_full(w_loc, SMALL_SHARDED, F32, COMM_ROW_ALIGN_SMALL))
    full["w_in"] = pad_w_in(full["w_in"])

    loss, (g_full, g_x) = jax.value_and_grad(local_loss, argnums=(0, 1))(full, x[0], loss_target[0])
    g_full = dict(g_full)
    g_full["w_in"] = unpad_w_in(g_full["w_in"])

    sharded = MATMUL_WEIGHTS + SMALL_SHARDED
    g4 = jnp.stack([_pack([_shard_of(g_full[n], n, j) for n in sharded], F32, COMM_ROW_ALIGN)
                    for j in range(N_CHIPS)])
    pair_sum = add_own_half(g4, sibling_swap_halves(g4), c)
    g_shard = sibling_join(sum_slots(chip_scatter(pair_sum)))
    grads = dict(zip(sharded, _unpack(g_shard, [w_loc[n].shape for n in sharded])))
    g_rep = sum_slots(all_devices_gather(_pack([g_full[n] for n in REPLICATED], F32, 8)))
    grads.update(zip(REPLICATED, _unpack(g_rep, [w_loc[n].shape for n in REPLICATED])))

    loss = lax.psum(loss, ("x", "y", "c"))
    upd = {n: adamw(w_loc[n], grads[n], m_loc[n], v_loc[n]) for n in WEIGHTS}
    return (loss, g_x[None], *[grads[n] for n in WEIGHTS], *[upd[n][0] for n in WEIGHTS],
            *[upd[n][1] for n in WEIGHTS], *[upd[n][2] for n in WEIGHTS])
```

```python
import functools

import jax
import jax.numpy as jnp
from jax import lax
from jax.experimental import pallas as pl
from jax.experimental.pallas import tpu as pltpu

F32 = jnp.float32
BF16 = jnp.bfloat16
HI = lax.Precision.HIGHEST
MESH = pl.DeviceIdType.MESH

D_MODEL = 1024
N_META = 16
EPS = 1e-6
FOX_HEADS, FOX_HD = 8, 64
FOX_W = FOX_HEADS * FOX_HD
GDN_HEADS, GDN_HD, GDN_CHUNK, GDN_CONV = 8, 128, 64, 4
GDN_W = GDN_HEADS * GDN_HD
D_FF = 2816
FFN_CONV = 3
D_IN = 7704
D_IN_PAD = 8192
IN_SEGS = ((0, 1536, 0), (1536, 8, 1536), (1544, 3072, 1664), (4616, 16, 4736), (4632, 1024, 4864), (5656, 2048, 5888))
ROW_ALIGN = 256
ATT_BLK = 256
VMEM_LIMIT = 48 * 1024 * 1024
LANE = 128

ADAM_LR, ADAM_B1, ADAM_B2, ADAM_EPS, ADAM_WD, ADAM_STEP = 0.001, 0.9, 0.999, 1e-08, 0.01, 10


def _call(body, **kw):
    return pl.pallas_call(body, **kw)


def _tile(n, target, mult):
    best, t = None, mult
    while t <= min(n, target):
        if n % t == 0:
            best = t
        t += mult
    assert best is not None, (n, target, mult)
    return best


def _cparams(sem):
    return pltpu.CompilerParams(dimension_semantics=sem, vmem_limit_bytes=VMEM_LIMIT)


def _raw_dot(a, b, ca, cb, precise):
    dims = (((ca,), (cb,)), ((), ()))
    a_hi, b_hi = a.astype(BF16), b.astype(BF16)
    out = lax.dot_general(a_hi, b_hi, dims, preferred_element_type=F32)
    if precise:
        a_lo = (a - a_hi.astype(F32)).astype(BF16)
        b_lo = (b - b_hi.astype(F32)).astype(BF16)
        out = out + (lax.dot_general(a_hi, b_lo, dims, preferred_element_type=F32)
                     + lax.dot_general(a_lo, b_hi, dims, preferred_element_type=F32))
    return out


def _make_dot(ca, cb, precise):
    @jax.custom_vjp
    def f(a, b):
        return _raw_dot(a, b, ca, cb, precise)

    def fwd(a, b):
        return f(a, b), (a, b)

    def bwd(res, ct):
        a, b = res
        if ca == 1:
            da = _raw_dot(ct, b, 1, 1 if cb == 0 else 0, precise)
        else:
            da = _raw_dot(b, ct, 1 if cb == 0 else 0, 1, precise)
        if cb == 0:
            db = _raw_dot(a, ct, 0 if ca == 1 else 1, 0, precise)
        else:
            db = _raw_dot(ct, a, 0, 0 if ca == 1 else 1, precise)
        return da, db

    f.defvjp(fwd, bwd)
    return f


_DOTS = {(ca, cb, p): _make_dot(ca, cb, p) for ca in (0, 1) for cb in (0, 1) for p in (False, True)}


def _dot(a, b, ca=1, cb=0, precise=False):
    return _DOTS[(ca, cb, precise)](a, b)


def _mm_call(a, b, name):
    m, k = a.shape
    n = b.shape[1]
    assert k == b.shape[0], (a.shape, b.shape)
    tm = _tile(m, 768, 16)
    tn = _tile(n, 1408, LANE)
    tk = _tile(k, 1408, LANE)
    nk = k // tk

    def body(a_ref, b_ref, o_ref, *scratch):
        part = jnp.dot(a_ref[...], b_ref[...], preferred_element_type=F32)
        if nk == 1:
            o_ref[...] = part
            return
        acc_ref = scratch[0]
        kk = pl.program_id(2)

        @pl.when(kk == 0)
        def _():
            acc_ref[...] = part

        @pl.when(kk > 0)
        def _():
            acc_ref[...] += part

        @pl.when(kk == nk - 1)
        def _():
            o_ref[...] = acc_ref[...]

    return _call(
        body, name=name, grid=(m // tm, n // tn, nk),
        in_specs=[pl.BlockSpec((tm, tk), lambda i, j, kk: (i, kk)), pl.BlockSpec((tk, tn), lambda i, j, kk: (kk, j))],
        out_specs=pl.BlockSpec((tm, tn), lambda i, j, kk: (i, j)),
        out_shape=jax.ShapeDtypeStruct((m, n), F32),
        scratch_shapes=[pltpu.VMEM((tm, tn), F32)] if nk > 1 else [],
        compiler_params=_cparams(("parallel", "parallel", "arbitrary")),
    )(a, b)


@jax.custom_vjp
def mm(a, w):
    return _mm_call(a.astype(BF16), w.astype(BF16), "mm_fwd")


def _mm_fwd(a, w):
    a_b, w_b = a.astype(BF16), w.astype(BF16)
    return _mm_call(a_b, w_b, "mm_fwd"), (a_b, w_b)


def _mm_bwd(res, ct):
    a_b, w_b = res
    ct_b = ct.astype(BF16)
    return _mm_call(ct_b, w_b.T, "mm_dx"), _mm_call(a_b.T, ct_b, "mm_dw")


mm.defvjp(_mm_fwd, _mm_bwd)


def _rows_specs(rows, tm, ncb, bc):
    specs = []
    for idx, r in enumerate(rows):
        if idx in bc:
            specs.append(pl.BlockSpec((tm, r.shape[1]), lambda i, j: (i, 0)))
        else:
            specs.append(pl.BlockSpec((tm, r.shape[1] // ncb), lambda i, j: (i, j)))
    return specs


def _param_specs(params):
    return [pl.BlockSpec(p.shape, lambda i, j: (0, 0)) for p in params]


def _rows_fwd_call(fn, rows, params, outs, tm, ncb, bc, name):
    r_total = rows[0].shape[0]
    n_in = len(rows) + len(params)

    def body(*refs):
        res = fn(*[r[...] for r in refs[:n_in]])
        for o_ref, val in zip(refs[n_in:], res):
            o_ref[...] = val.astype(o_ref.dtype)

    return _call(
        body, name=name, grid=(r_total // tm, ncb),
        in_specs=_rows_specs(rows, tm, ncb, bc) + _param_specs(params),
        out_specs=[pl.BlockSpec((tm, w), lambda i, j: (i, j)) for w in outs],
        out_shape=[jax.ShapeDtypeStruct((r_total, w * ncb), F32) for w in outs],
        compiler_params=_cparams(("parallel", "parallel")),
    )(*rows, *params)


def _rows_bwd_call(fn, rows, params, cts, tm, ncb, bc, name):
    r_total = rows[0].shape[0]
    nr, npar, nct = len(rows), len(params), len(cts)

    def body(*refs):
        i, j = pl.program_id(0), pl.program_id(1)
        ins = [r[...] for r in refs[:nr + npar]]
        ct_vals = tuple(r[...] for r in refs[nr + npar:nr + npar + nct])
        d_refs = refs[nr + npar + nct:]
        _, vjp = jax.vjp(lambda *a: tuple(fn(*a)), *ins)
        grads = vjp(ct_vals)
        for idx in range(nr):
            if idx in bc:
                @pl.when(j == 0)
                def _(idx=idx):
                    d_refs[idx][...] = jnp.zeros_like(d_refs[idx])
                d_refs[idx][...] += grads[idx]
            else:
                d_refs[idx][...] = grads[idx]
        for idx in range(nr, nr + npar):
            @pl.when((i == 0) & (j == 0))
            def _(idx=idx):
                d_refs[idx][...] = jnp.zeros_like(d_refs[idx])
            d_refs[idx][...] += grads[idx]

    ct_specs = [pl.BlockSpec((tm, c.shape[1] // ncb), lambda i, j: (i, j)) for c in cts]
    return _call(
        body, name=name + "_bwd", grid=(r_total // tm, ncb),
        in_specs=_rows_specs(rows, tm, ncb, bc) + _param_specs(params) + ct_specs,
        out_specs=_rows_specs(rows, tm, ncb, bc) + _param_specs(params),
        out_shape=[jax.ShapeDtypeStruct(a.shape, F32) for a in list(rows) + list(params)],
        compiler_params=_cparams(("arbitrary", "arbitrary")),
    )(*rows, *params, *cts)


def rowop(fn, name, outs, tm, ncb=1, bc=()):
    @jax.custom_vjp
    def op(rows, params):
        return tuple(_rows_fwd_call(fn, rows, params, outs, tm, ncb, bc, name))

    def fwd(rows, params):
        return op(rows, params), (rows, params)

    def bwd(res, cts):
        rows, params = res
        d = _rows_bwd_call(fn, rows, params, cts, tm, ncb, bc, name)
        return tuple(d[:len(rows)]), tuple(d[len(rows):])

    op.defvjp(fwd, bwd)
    return op


def _sigmoid(x):
    return 1.0 / (1.0 + jnp.exp(-x))


def _silu(x):
    return x * _sigmoid(x)


def _softplus(x):
    return jnp.maximum(x, 0.0) + jnp.log(1.0 + jnp.exp(-jnp.abs(x)))


def _f_rmsnorm(x, g):
    return (x * lax.rsqrt(jnp.mean(x * x, axis=-1, keepdims=True) + EPS) * g,)


def _f_qnorm(x, g):
    return (x * lax.rsqrt(jnp.mean(x * x, axis=-1, keepdims=True) + EPS) * (g * (FOX_HD ** -0.5)),)


def _f_logsig(x, b):
    return (-_softplus(-(x + b)),)


def _f_gdn_act(x):
    part = pl.program_id(1) // GDN_HEADS
    y = _silu(x)
    unit = y * lax.rsqrt(jnp.sum(y * y, axis=-1, keepdims=True) + EPS)
    return (jnp.where(part == 2, y, unit * jnp.where(part == 0, GDN_HD ** -0.5, 1.0)),)


def _f_gdn_gates(bl, al, a_log, dt_bias):
    return _sigmoid(bl), -jnp.exp(a_log) * _softplus(al + dt_bias)


def _f_gdn_out(o, z, g):
    return (o * lax.rsqrt(jnp.mean(o * o, axis=-1, keepdims=True) + EPS) * g * _silu(z),)


def _f_merge(g0, g1, ya, yb):
    return (_sigmoid(g0) * ya + _sigmoid(g1) * yb,)


def _f_residual(a, b, keep):
    return ((a + b) * keep,)


def _f_glu(a, b):
    return (_silu(a) * b,)


def _glu_call(up, ct):
    t_total, two_f = up.shape
    f = two_f // 2
    tm = _tile(t_total, 128, 8)
    wc = _tile(f, 1408, LANE)

    def body(*refs):
        up_ref, out_ref = refs[0], refs[-1]
        for c0 in range(0, f, wc):
            a, b = up_ref[:, c0:c0 + wc], up_ref[:, f + c0:f + c0 + wc]
            if ct is None:
                out_ref[:, c0:c0 + wc] = _f_glu(a, b)[0]
            else:
                _, vjp = jax.vjp(_f_glu, a, b)
                da, db = vjp((refs[1][:, c0:c0 + wc],))
                out_ref[:, c0:c0 + wc] = da
                out_ref[:, f + c0:f + c0 + wc] = db

    wide = pl.BlockSpec((tm, two_f), lambda i: (i, 0))
    narrow = pl.BlockSpec((tm, f), lambda i: (i, 0))
    return _call(
        body, name="ffn_glu" if ct is None else "ffn_glu_bwd", grid=(t_total // tm,),
        in_specs=[wide] if ct is None else [wide, narrow], out_specs=narrow if ct is None else wide,
        out_shape=jax.ShapeDtypeStruct((t_total, f if ct is None else two_f), F32),
        compiler_params=_cparams(("parallel",)),
    )(*((up,) if ct is None else (up, ct)))


@jax.custom_vjp
def glu(up):
    return _glu_call(up, None)


glu.defvjp(lambda up: (glu(up), up), lambda up, ct: (_glu_call(up, ct),))


def _shift_down(x, halo, s, row8):
    rx = pltpu.roll(x, s, 0)
    top = jnp.where(row8 < s, pltpu.roll(halo, s, 0), rx[:8])
    return jnp.concatenate([top, rx[8:]], axis=0)


def _shift_up(x, nxt, s, row8):
    tm = x.shape[0]
    rx = pltpu.roll(x, tm - s, 0)
    bot = jnp.where(row8 >= 8 - s, pltpu.roll(nxt, 8 - s, 0), rx[tm - 8:])
    return jnp.concatenate([rx[:tm - 8], bot], axis=0)


def _conv_tiles(r_total, c_total):
    return _tile(r_total, 768, 8), _tile(c_total, 1408, LANE)


def _conv_fwd_call(x, w8, k_taps):
    r_total, c_total = x.shape
    tm, tc = _conv_tiles(r_total, c_total)
    hb = tm // 8

    def body(x_ref, halo_ref, w_ref, y_ref):
        i = pl.program_id(1)
        xt = x_ref[...]
        halo = jnp.where(i > 0, halo_ref[...], 0.0)
        row8 = lax.broadcasted_iota(jnp.int32, (8, tc), 0)
        acc = w_ref[k_taps - 1:k_taps, :] * xt
        for k in range(k_taps - 1):
            acc += w_ref[k:k + 1, :] * _shift_down(xt, halo, k_taps - 1 - k, row8)
        y_ref[...] = acc

    return _call(
        body, name="dwconv_fwd", grid=(c_total // tc, r_total // tm),
        in_specs=[pl.BlockSpec((tm, tc), lambda c, i: (i, c)),
                  pl.BlockSpec((8, tc), lambda c, i: (jnp.maximum(i * hb - 1, 0), c)),
                  pl.BlockSpec((8, tc), lambda c, i: (0, c))],
        out_specs=pl.BlockSpec((tm, tc), lambda c, i: (i, c)),
        out_shape=jax.ShapeDtypeStruct(x.shape, F32),
        compiler_params=_cparams(("parallel", "parallel")),
    )(x, x, w8)


def _conv_bwd_call(x, w8, dy, k_taps):
    r_total, c_total = x.shape
    tm, tc = _conv_tiles(r_total, c_total)
    hb = tm // 8
    n_i = r_total // tm

    def body(x_ref, halo_ref, w_ref, dy_ref, nxt_ref, dx_ref, dw_ref):
        i = pl.program_id(1)
        xt, dyt = x_ref[...], dy_ref[...]
        halo = jnp.where(i > 0, halo_ref[...], 0.0)
        nxt = jnp.where(i < n_i - 1, nxt_ref[...], 0.0)
        row8 = lax.broadcasted_iota(jnp.int32, (8, tc), 0)
        dx = w_ref[k_taps - 1:k_taps, :] * dyt
        upd = jnp.where(row8 == k_taps - 1, jnp.sum(dyt * xt, axis=0, keepdims=True), 0.0)
        for k in range(k_taps - 1):
            s = k_taps - 1 - k
            dx += w_ref[k:k + 1, :] * _shift_up(dyt, nxt, s, row8)
            upd = jnp.where(row8 == k, jnp.sum(dyt * _shift_down(xt, halo, s, row8), axis=0, keepdims=True), upd)
        dx_ref[...] = dx

        @pl.when(i == 0)
        def _():
            dw_ref[...] = jnp.zeros_like(dw_ref)

        dw_ref[...] += upd

    return _call(
        body, name="dwconv_bwd", grid=(c_total // tc, n_i),
        in_specs=[pl.BlockSpec((tm, tc), lambda c, i: (i, c)),
                  pl.BlockSpec((8, tc), lambda c, i: (jnp.maximum(i * hb - 1, 0), c)),
                  pl.BlockSpec((8, tc), lambda c, i: (0, c)),
                  pl.BlockSpec((tm, tc), lambda c, i: (i, c)),
                  pl.BlockSpec((8, tc), lambda c, i: (jnp.minimum((i + 1) * hb, r_total // 8 - 1), c))],
        out_specs=[pl.BlockSpec((tm, tc), lambda c, i: (i, c)), pl.BlockSpec((8, tc), lambda c, i: (0, c))],
        out_shape=[jax.ShapeDtypeStruct(x.shape, F32), jax.ShapeDtypeStruct(w8.shape, F32)],
        compiler_params=_cparams(("parallel", "arbitrary")),
    )(x, x, w8, dy, dy)


def make_dwconv(k_taps):
    @jax.custom_vjp
    def op(x, w8):
        return _conv_fwd_call(x, w8, k_taps)

    def fwd(x, w8):
        return op(x, w8), (x, w8)

    def bwd(res, dy):
        x, w8 = res
        dx, dw = _conv_bwd_call(x, w8, dy, k_taps)
        return dx, dw

    op.defvjp(fwd, bwd)
    return op


def _cumsum_call(x, reverse):
    h, t_total = x.shape
    tb = _tile(t_total, 256, LANE)
    nb = t_total // tb

    def body(x_ref, o_ref, carry_ref):
        i = pl.program_id(0)

        @pl.when(i == 0)
        def _():
            carry_ref[...] = jnp.zeros_like(carry_ref)

        r = lax.broadcasted_iota(jnp.int32, (tb, tb), 0)
        c = lax.broadcasted_iota(jnp.int32, (tb, tb), 1)
        tri = jnp.where((r >= c) if reverse else (r <= c), 1.0, 0.0).astype(F32)
        xv = x_ref[...]
        carry = jnp.max(carry_ref[...], axis=1, keepdims=True)
        o_ref[...] = _raw_dot(xv, tri, 1, 0, True) + carry
        carry_ref[...] = jnp.broadcast_to(carry + jnp.sum(xv, axis=1, keepdims=True), carry_ref.shape)

    imap = (lambda i: (0, nb - 1 - i)) if reverse else (lambda i: (0, i))
    return _call(
        body, name="cumsum_rev" if reverse else "cumsum", grid=(nb,),
        in_specs=[pl.BlockSpec((h, tb), imap)], out_specs=pl.BlockSpec((h, tb), imap),
        out_shape=jax.ShapeDtypeStruct(x.shape, F32), scratch_shapes=[pltpu.VMEM((h, LANE), F32)],
        compiler_params=_cparams(("arbitrary",)),
    )(x)


@jax.custom_vjp
def cumsum_lanes(x):
    return _cumsum_call(x, False)


cumsum_lanes.defvjp(lambda x: (cumsum_lanes(x), None), lambda _, ct: (_cumsum_call(ct, True),))


NEG_BIG = -1e30


def _attn_sub_tiles(nb):
    return max(s for s in (3, 2, 1) if nb % s == 0)


EXP_ZERO = -100.0
SMEM = pl.BlockSpec(memory_space=pltpu.SMEM)


def _max_row_norm_sq(x):
    h_total, t_total, hd = x.shape
    tb = _tile(t_total, 2816, 8)

    def body(x_ref, o_ref):
        @pl.when(pl.program_id(1) == 0)
        def _():
            o_ref[...] = jnp.zeros_like(o_ref)

        xv = x_ref[0]
        top = jnp.max(jnp.sum(xv * xv, axis=1, keepdims=True), axis=0, keepdims=True)
        o_ref[0] = jnp.maximum(o_ref[0], top)

    return _call(
        body, name="max_row_norm", grid=(h_total, t_total // tb),
        in_specs=[pl.BlockSpec((1, tb, hd), lambda h, i: (h, i, 0))],
        out_specs=pl.BlockSpec((1, 8, LANE), lambda h, i: (h, 0, 0)),
        out_shape=jax.ShapeDtypeStruct((h_total, 8, LANE), F32),
        compiler_params=_cparams(("parallel", "arbitrary")),
    )(x)


def _attn_skip_tables(q, k, f_row):
    bound = 2.0 * jnp.sqrt(_max_row_norm_sq(q)[:, 0, :1] * _max_row_norm_sq(k)[:, 0, :1])
    return EXP_ZERO - bound, f_row[:, :, 0, 0], f_row[:, :, 0, -1]


def _attn_fwd_call(q, k, v, f_col, f_row, tables):
    h_total, t_total, hd = q.shape
    blk = f_row.shape[-1]
    nb = t_total // blk
    nsub = _attn_sub_tiles(nb)
    tq = nsub * blk

    def body(thr_ref, first_ref, last_ref, q_ref, k_ref, vt_ref, fc_ref, fr_ref, o_ref, lse_ref):
        h = pl.program_id(0)
        i = pl.program_id(1)
        gap_needed = thr_ref[h, 0]
        f_tile = first_ref[h, i * nsub]
        j_start = lax.while_loop(lambda j: (j < i * nsub) & (f_tile - last_ref[h, j] < gap_needed),
                                 lambda j: j + 1, 0)
        r = lax.broadcasted_iota(jnp.int32, (blk, blk), 0)
        c = lax.broadcasted_iota(jnp.int32, (blk, blk), 1)
        qs = [q_ref[0, s * blk:(s + 1) * blk, :].astype(BF16) for s in range(nsub)]
        fqs = [fr_ref[0, i * nsub + s] for s in range(nsub)]

        def load_kv(j):
            off = pl.multiple_of(j * blk, blk)
            return k_ref[0, pl.ds(off, blk), :], vt_ref[0, j], fc_ref[0, pl.ds(off, blk), :]

        def tile(kv, s, carry, diagonal):
            kj, vtj, fk = kv
            m, l, acc = carry
            st = _raw_dot(kj, qs[s], 1, 1, False) + fqs[s] - fk
            if diagonal:
                st = jnp.where(r <= c, st, NEG_BIG)
            m_new = jnp.maximum(m, jnp.max(st, axis=0, keepdims=True))
            p = jnp.exp(st - m_new)
            alpha = jnp.exp(m - m_new)
            l = alpha * l + jnp.sum(p, axis=0, keepdims=True)
            acc = alpha * acc + _raw_dot(vtj, p, 1, 0, False)
            return m_new, l, acc

        def below_diagonal(j, carry):
            kv = load_kv(j)
            return tuple(tile(kv, s, carry[s], False) for s in range(nsub))

        init = tuple((jnp.full((1, blk), NEG_BIG, F32), jnp.zeros((1, blk), F32), jnp.zeros((hd, blk), F32))
                     for _ in range(nsub))
        carry = list(lax.fori_loop(j_start, i * nsub, below_diagonal, init))
        for d in range(nsub):
            kv = load_kv(i * nsub + d)
            for s in range(d, nsub):
                carry[s] = tile(kv, s, carry[s], s == d)
        for s, (m, l, acc) in enumerate(carry):
            o_ref[0, :, s * blk:(s + 1) * blk] = acc / l
            lse_ref[0, s] = m + jnp.log(l)

    vt = v.reshape(h_total, nb, blk, hd).transpose(0, 1, 3, 2).astype(BF16)
    return _call(
        body, name="fox_fwd", grid=(h_total, nb // nsub),
        in_specs=[SMEM, SMEM, SMEM,
                  pl.BlockSpec((1, tq, hd), lambda h, i: (h, i, 0)),
                  pl.BlockSpec((1, t_total, hd), lambda h, i: (h, 0, 0)),
                  pl.BlockSpec((1, nb, hd, blk), lambda h, i: (h, 0, 0, 0)),
                  pl.BlockSpec((1, t_total, 1), lambda h, i: (h, 0, 0)),
                  pl.BlockSpec((1, nb, 1, blk), lambda h, i: (h, 0, 0, 0))],
        out_specs=[pl.BlockSpec((1, hd, tq), lambda h, i: (h, 0, i)),
                   pl.BlockSpec((1, nsub, 1, blk), lambda h, i: (h, i, 0, 0))],
        out_shape=[jax.ShapeDtypeStruct((h_total, hd, t_total), F32), jax.ShapeDtypeStruct(f_row.shape, F32)],
        compiler_params=_cparams(("parallel", "parallel")),
    )(*tables, q, k.astype(BF16), vt, f_col, f_row)


def _attn_bwd_call(q, k, v, f_col, f_row, tables, lse_row, delta_row, do_blk):
    h_total, t_total, hd = q.shape
    blk = f_row.shape[-1]
    nb = t_total // blk
    nsub = _attn_sub_tiles(nb)
    tkv = nsub * blk

    def body(thr_ref, first_ref, last_ref, q_ref, do_ref, k_ref, v_ref, fc_ref, fr_ref, lse_ref, dl_ref,
             dq_ref, dk_ref, dv_ref, dfk_ref, dfq_ref):
        h = pl.program_id(0)
        j = pl.program_id(1)
        gap_needed = thr_ref[h, 0]
        f_tile = last_ref[h, j * nsub + nsub - 1]
        i_stop = lax.while_loop(lambda i: (i < nb) & (first_ref[h, jnp.minimum(i, nb - 1)] - f_tile >= gap_needed),
                                lambda i: i + 1, (j + 1) * nsub)

        @pl.when(j == 0)
        def _():
            dq_ref[...] = jnp.zeros_like(dq_ref)
            dfq_ref[...] = jnp.zeros_like(dfq_ref)

        ks = [k_ref[0, s * blk:(s + 1) * blk, :].astype(BF16) for s in range(nsub)]
        vs = [v_ref[0, s * blk:(s + 1) * blk, :].astype(BF16) for s in range(nsub)]
        fks = [fc_ref[0, s * blk:(s + 1) * blk, :] for s in range(nsub)]
        r = lax.broadcasted_iota(jnp.int32, (blk, blk), 0)
        c = lax.broadcasted_iota(jnp.int32, (blk, blk), 1)

        def q_step(i, accs, subs):
            off = pl.multiple_of(i * blk, blk)
            qi = q_ref[0, pl.ds(off, blk), :]
            doi = do_ref[0, i]
            fq, lse, dl = fr_ref[0, i], lse_ref[0, i], dl_ref[0, i]
            accs = list(accs)
            dq_i, dfq_i = None, None
            for s, diagonal in subs:
                dk, dv, dfk = accs[s]
                st = _raw_dot(ks[s], qi, 1, 1, False) + fq - fks[s] - lse
                if diagonal:
                    st = jnp.where(r <= c, st, NEG_BIG)
                pt = jnp.exp(st)
                dv = dv + _raw_dot(pt, doi, 1, 1, False)
                dst = pt * (_raw_dot(vs[s], doi, 1, 0, False) - dl)
                dk = dk + _raw_dot(dst, qi, 1, 0, False)
                dfk = dfk - jnp.sum(dst, axis=1, keepdims=True)
                accs[s] = (dk, dv, dfk)
                dq_s = _raw_dot(dst, ks[s], 0, 0, False)
                dfq_s = jnp.sum(dst, axis=0, keepdims=True)
                dq_i = dq_s if dq_i is None else dq_i + dq_s
                dfq_i = dfq_s if dfq_i is None else dfq_i + dfq_s
            dfq_ref[0, i] += dfq_i
            dq_ref[0, pl.ds(off, blk), :] += dq_i
            return tuple(accs)

        accs = tuple((jnp.zeros((blk, hd), F32), jnp.zeros((blk, hd), F32), jnp.zeros((blk, 1), F32))
                     for _ in range(nsub))
        for d in range(nsub):
            accs = q_step(j * nsub + d, accs, [(s, s == d) for s in range(d + 1)])
        accs = lax.fori_loop((j + 1) * nsub, i_stop,
                             lambda i, a: q_step(i, a, [(s, False) for s in range(nsub)]), accs)
        for s, (dk, dv, dfk) in enumerate(accs):
            dk_ref[0, s * blk:(s + 1) * blk, :] = dk
            dv_ref[0, s * blk:(s + 1) * blk, :] = dv
            dfk_ref[0, s * blk:(s + 1) * blk, :] = dfk

    full = pl.BlockSpec((1, t_total, hd), lambda h, j: (h, 0, 0))
    tile = pl.BlockSpec((1, tkv, hd), lambda h, j: (h, j, 0))
    col = pl.BlockSpec((1, tkv, 1), lambda h, j: (h, j, 0))
    rows = pl.BlockSpec((1, nb, 1, blk), lambda h, j: (h, 0, 0, 0))
    do_blocks = pl.BlockSpec((1, nb, hd, blk), lambda h, j: (h, 0, 0, 0))
    return _call(
        body, name="fox_bwd", grid=(h_total, nb // nsub),
        in_specs=[SMEM, SMEM, SMEM, full, do_blocks, tile, tile, col, rows, rows, rows],
        out_specs=[full, tile, tile, col, rows],
        out_shape=[jax.ShapeDtypeStruct(q.shape, F32), jax.ShapeDtypeStruct(q.shape, F32),
                   jax.ShapeDtypeStruct(q.shape, F32), jax.ShapeDtypeStruct(f_col.shape, F32),
                   jax.ShapeDtypeStruct(f_row.shape, F32)],
        compiler_params=_cparams(("parallel", "arbitrary")),
    )(*tables, q.astype(BF16), do_blk, k, v, f_col, f_row, lse_row, delta_row)


def _attn_delta_call(do_t, o_t):
    h_total, hd, t_total = o_t.shape
    tb = _tile(t_total, 2816, LANE)

    def body(do_ref, o_ref, d_ref):
        d_ref[0] = jnp.sum(do_ref[0] * o_ref[0], axis=0, keepdims=True)

    spec = pl.BlockSpec((1, hd, tb), lambda h, i: (h, 0, i))
    return _call(
        body, name="fox_delta", grid=(h_total, t_total // tb), in_specs=[spec, spec],
        out_specs=pl.BlockSpec((1, 1, tb), lambda h, i: (h, 0, i)),
        out_shape=jax.ShapeDtypeStruct((h_total, 1, t_total), F32),
        compiler_params=_cparams(("parallel", "parallel")),
    )(do_t, o_t)


@jax.custom_vjp
def fox_attention(q, k, v, f_col, f_row):
    return _attn_fwd_call(q, k, v, f_col, f_row, _attn_skip_tables(q, k, f_row))[0]


def _fox_fwd(q, k, v, f_col, f_row):
    tables = _attn_skip_tables(q, k, f_row)
    o_t, lse_row = _attn_fwd_call(q, k, v, f_col, f_row, tables)
    return o_t, (q, k, v, f_col, f_row, tables, o_t, lse_row)


def _fox_bwd(res, do_t):
    q, k, v, f_col, f_row, tables, o_t, lse_row = res
    h_total, t_total, hd = q.shape
    nb, blk = f_row.shape[1], f_row.shape[3]
    delta = _attn_delta_call(do_t, o_t).reshape(f_row.shape)
    do_blk = do_t.reshape(h_total, hd, nb, blk).transpose(0, 2, 1, 3).astype(BF16)
    return tuple(_attn_bwd_call(q, k, v, f_col, f_row, tables, lse_row, delta, do_blk))


fox_attention.defvjp(_fox_fwd, _fox_bwd)


def _head_col(blk, h):
    lane = lax.broadcasted_iota(jnp.int32, blk.shape, 1)
    return jnp.sum(jnp.where(lane == h, blk, 0.0), axis=1, keepdims=True)


@jax.custom_vjp
def _cat2(a, b):
    return jnp.concatenate([a, b], axis=1)


_cat2.defvjp(lambda a, b: (_cat2(a, b), a.shape[1]), lambda na, ct: (ct[:, :na], ct[:, na:]))


@jax.custom_vjp
def _split2(x):
    half = x.shape[1] // 2
    return x[:, :half], x[:, half:]


_split2.defvjp(lambda x: (_split2(x), None), lambda _, cts: (jnp.concatenate(cts, axis=1),))


def _gdn_intra(h, q, k, v, b_blk, g_blk):
    n = q.shape[0]
    b, g = _head_col(b_blk, h), _head_col(g_blk, h)
    r = lax.broadcasted_iota(jnp.int32, (n, n), 0)
    c = lax.broadcasted_iota(jnp.int32, (n, n), 1)
    same = (r // GDN_CHUNK) == (c // GDN_CHUNK)
    incl = same & (r >= c)
    g_row = jnp.sum(jnp.where(r == c, g, 0.0), axis=0, keepdims=True)
    big_g = jnp.sum(jnp.where(incl, g_row, 0.0), axis=1, keepdims=True)
    big_g_row = jnp.sum(jnp.where(same & (r <= c), g, 0.0), axis=0, keepdims=True)
    g_tot = jnp.sum(jnp.where(same, g_row, 0.0), axis=1, keepdims=True)
    dec = jnp.where(incl, jnp.exp(jnp.where(incl, big_g - big_g_row, 0.0)), 0.0)
    dec_strict = jnp.where(r > c, dec, 0.0)
    e_g = jnp.exp(big_g)
    kb = k * b
    m = _dot(kb, k, 1, 1) * dec_strict
    rs = lax.broadcasted_iota(jnp.int32, (n, GDN_CHUNK), 0)
    cs = lax.broadcasted_iota(jnp.int32, (n, GDN_CHUNK), 1)
    fold = jnp.where(rs % GDN_CHUNK == cs, 1.0, 0.0).astype(F32)
    aqk = _dot(_dot(q, k, 1, 1) * dec, fold, 1, 0, True)
    x = _cat2(v * b, kb * e_g)
    x = x - _dot(m, x)
    p = m
    steps = 1
    while 2 * steps < GDN_CHUNK:
        p = _dot(p, p)
        x = x + _dot(p, x)
        steps *= 2
    u, w = _split2(x)
    lane = lax.broadcasted_iota(jnp.int32, b_blk.shape, 1)
    return u, w, q * e_g, k * jnp.exp(g_tot - big_g), aqk, jnp.where(lane == h, g_tot, 0.0)


def _gdn_rec(h, s, u, w, qg, kd, aqk, gl_blk):
    g_last = jnp.max(_head_col(gl_blk, h), axis=0, keepdims=True)
    big_u = u - _dot(w, s)
    o = _dot(qg, s) + _dot(aqk, big_u)
    s_next = s * jnp.exp(g_last) + _dot(kd, big_u, 0, 0)
    return o, s_next


GDN_TOK_BLK = 256


def _gdn_layout(t_total, rev):
    tb = _tile(t_total, GDN_TOK_BLK, GDN_CHUNK)
    cb, nblk = tb // GDN_CHUNK, t_total // tb
    pos = (lambda i: nblk - 1 - i) if rev else (lambda i: i)
    specs = dict(
        tok=pl.BlockSpec((tb, GDN_W), lambda i: (pos(i), 0)),
        q=pl.BlockSpec((tb, GDN_W), lambda i: (pos(i), 0)),
        k=pl.BlockSpec((tb, GDN_W), lambda i: (pos(i), 1)),
        v=pl.BlockSpec((tb, GDN_W), lambda i: (pos(i), 2)),
        qkv=pl.BlockSpec((tb, 3 * GDN_W), lambda i: (pos(i), 0)),
        gate=pl.BlockSpec((tb, GDN_HEADS), lambda i: (pos(i), 0)),
        aqk=pl.BlockSpec((GDN_HEADS, tb, GDN_CHUNK), lambda i: (0, pos(i), 0)),
        state=pl.BlockSpec((GDN_HEADS, cb, GDN_HD, GDN_HD), lambda i: (0, pos(i), 0, 0)))
    return cb, nblk, specs


def _gdn_shapes(t_total):
    n_chunks = t_total // GDN_CHUNK
    return dict(tok=jax.ShapeDtypeStruct((t_total, GDN_W), F32),
                gate=jax.ShapeDtypeStruct((t_total, GDN_HEADS), F32),
                aqk=jax.ShapeDtypeStruct((GDN_HEADS, t_total, GDN_CHUNK), F32),
                state=jax.ShapeDtypeStruct((GDN_HEADS, n_chunks, GDN_HD, GDN_HD), F32))


def _chunk_rows(ci):
    return pl.ds(pl.multiple_of(ci * GDN_CHUNK, GDN_CHUNK), GDN_CHUNK)


def _head_cols(h):
    return pl.ds(h * GDN_HD, GDN_HD)


def _gdn_intra_fwd_call(qkv, b, g):
    cb, nblk, sp = _gdn_layout(qkv.shape[0], False)
    sh = _gdn_shapes(qkv.shape[0])

    def body(q_ref, k_ref, v_ref, b_ref, g_ref, u_ref, w_ref, qg_ref, kd_ref, aqk_ref, gl_ref):
        b_blk, g_blk = b_ref[...], g_ref[...]
        gl = jnp.zeros(b_blk.shape, F32)
        for h in range(GDN_HEADS):
            cols = _head_cols(h)
            u, w, qg, kd, aqk, gl_h = _gdn_intra(h, q_ref[:, cols], k_ref[:, cols], v_ref[:, cols], b_blk, g_blk)
            u_ref[:, cols] = u
            w_ref[:, cols] = w
            qg_ref[:, cols] = qg
            kd_ref[:, cols] = kd
            aqk_ref[h] = aqk
            gl = gl + gl_h
        gl_ref[...] = gl

    return _call(
        body, name="gdn_intra_fwd", grid=(nblk,),
        in_specs=[sp["q"], sp["k"], sp["v"]] + [sp["gate"]] * 2,
        out_specs=[sp["tok"]] * 4 + [sp["aqk"], sp["gate"]],
        out_shape=[sh["tok"]] * 4 + [sh["aqk"], sh["gate"]],
        compiler_params=_cparams(("parallel",)),
    )(qkv, qkv, qkv, b, g)


def _gdn_intra_bwd_call(qkv, b, g, du, dw, dqg, dkd, daqk, dgl):
    cb, nblk, sp = _gdn_layout(qkv.shape[0], False)
    sh = _gdn_shapes(qkv.shape[0])

    def body(q_ref, k_ref, v_ref, b_ref, g_ref, du_ref, dw_ref, dqg_ref, dkd_ref, daqk_ref, dgl_ref,
             dqkv_ref, db_ref, dg_ref):
        b_blk, g_blk, dgl = b_ref[...], g_ref[...], dgl_ref[...]
        db = jnp.zeros(b_blk.shape, F32)
        dg = jnp.zeros(b_blk.shape, F32)
        for h in range(GDN_HEADS):
            cols = _head_cols(h)
            _, vjp = jax.vjp(functools.partial(_gdn_intra, h), q_ref[:, cols], k_ref[:, cols], v_ref[:, cols],
                             b_blk, g_blk)
            dq, dk, dv, db_h, dg_h = vjp((du_ref[:, cols], dw_ref[:, cols], dqg_ref[:, cols], dkd_ref[:, cols],
                                          daqk_ref[h], dgl))
            dqkv_ref[:, pl.ds(h * GDN_HD, GDN_HD)] = dq
            dqkv_ref[:, pl.ds(GDN_W + h * GDN_HD, GDN_HD)] = dk
            dqkv_ref[:, pl.ds(2 * GDN_W + h * GDN_HD, GDN_HD)] = dv
            db = db + db_h
            dg = dg + dg_h
        db_ref[...] = db
        dg_ref[...] = dg

    return _call(
        body, name="gdn_intra_bwd", grid=(nblk,),
        in_specs=[sp["q"], sp["k"], sp["v"]] + [sp["gate"]] * 2 + [sp["tok"]] * 4 + [sp["aqk"], sp["gate"]],
        out_specs=[sp["qkv"]] + [sp["gate"]] * 2,
        out_shape=[jax.ShapeDtypeStruct(qkv.shape, F32)] + [sh["gate"]] * 2,
        compiler_params=_cparams(("parallel",)),
    )(qkv, qkv, qkv, b, g, du, dw, dqg, dkd, daqk, dgl)


def _gdn_rec_fwd_call(u, w, qg, kd, aqk, gl):
    cb, nblk, sp = _gdn_layout(u.shape[0], False)
    sh = _gdn_shapes(u.shape[0])

    def body(u_ref, w_ref, qg_ref, kd_ref, aqk_ref, gl_ref, o_ref, s_all_ref, s_ref):
        @pl.when(pl.program_id(0) == 0)
        def _():
            s_ref[...] = jnp.zeros_like(s_ref)

        def chunk(ci, carry):
            rows = _chunk_rows(ci)
            gl_row = gl_ref[rows, :]
            states = [s_ref[h] for h in range(GDN_HEADS)]
            res = [_gdn_rec(h, states[h], u_ref[rows, _head_cols(h)], w_ref[rows, _head_cols(h)],
                            qg_ref[rows, _head_cols(h)], kd_ref[rows, _head_cols(h)], aqk_ref[h, rows, :], gl_row)
                   for h in range(GDN_HEADS)]
            for h, (o, s_next) in enumerate(res):
                s_all_ref[h, ci] = states[h]
                o_ref[rows, _head_cols(h)] = o
                s_ref[h] = s_next
            return carry

        lax.fori_loop(0, cb, chunk, 0)

    return _call(
        body, name="gdn_rec_fwd", grid=(nblk,),
        in_specs=[sp["tok"]] * 4 + [sp["aqk"], sp["gate"]],
        out_specs=[sp["tok"], sp["state"]], out_shape=[sh["tok"], sh["state"]],
        scratch_shapes=[pltpu.VMEM((GDN_HEADS, GDN_HD, GDN_HD), F32)],
        compiler_params=_cparams(("arbitrary",)),
    )(u, w, qg, kd, aqk, gl)


def _gdn_rec_bwd_call(u, w, qg, kd, aqk, gl, s_all, do):
    cb, nblk, sp = _gdn_layout(u.shape[0], True)
    sh = _gdn_shapes(u.shape[0])

    def body(u_ref, w_ref, qg_ref, kd_ref, aqk_ref, gl_ref, s_all_ref, do_ref,
             du_ref, dw_ref, dqg_ref, dkd_ref, daqk_ref, dgl_ref, ds_ref):
        @pl.when(pl.program_id(0) == 0)
        def _():
            ds_ref[...] = jnp.zeros_like(ds_ref)

        def chunk(step, carry):
            ci = cb - 1 - step
            rows = _chunk_rows(ci)
            gl_row = gl_ref[rows, :]
            res = []
            for h in range(GDN_HEADS):
                cols = _head_cols(h)
                _, vjp = jax.vjp(functools.partial(_gdn_rec, h), s_all_ref[h, ci], u_ref[rows, cols],
                                 w_ref[rows, cols], qg_ref[rows, cols], kd_ref[rows, cols], aqk_ref[h, rows, :],
                                 gl_row)
                res.append(vjp((do_ref[rows, cols], ds_ref[h])))
            dgl = jnp.zeros((GDN_CHUNK, GDN_HEADS), F32)
            for h, (ds, du, dw, dqg, dkd, daqk, dgl_h) in enumerate(res):
                cols = _head_cols(h)
                ds_ref[h] = ds
                du_ref[rows, cols] = du
                dw_ref[rows, cols] = dw
                dqg_ref[rows, cols] = dqg
                dkd_ref[rows, cols] = dkd
                daqk_ref[h, rows, :] = daqk
                dgl = dgl + dgl_h
            dgl_ref[rows, :] = dgl
            return carry

        lax.fori_loop(0, cb, chunk, 0)

    return _call(
        body, name="gdn_rec_bwd", grid=(nblk,),
        in_specs=[sp["tok"]] * 4 + [sp["aqk"], sp["gate"], sp["state"], sp["tok"]],
        out_specs=[sp["tok"]] * 4 + [sp["aqk"], sp["gate"]],
        out_shape=[sh["tok"]] * 4 + [sh["aqk"], sh["gate"]],
        scratch_shapes=[pltpu.VMEM((GDN_HEADS, GDN_HD, GDN_HD), F32)],
        compiler_params=_cparams(("arbitrary",)),
    )(u, w, qg, kd, aqk, gl, s_all, do)


@jax.custom_vjp
def gdn_intra(qkv, b, g):
    return tuple(_gdn_intra_fwd_call(qkv, b, g))


gdn_intra.defvjp(lambda *a: (gdn_intra(*a), a), lambda res, cts: tuple(_gdn_intra_bwd_call(*res, *cts)))


@jax.custom_vjp
def gdn_rec(u, w, qg, kd, aqk, gl):
    return _gdn_rec_fwd_call(u, w, qg, kd, aqk, gl)[0]


def _gdn_rec_fwd(*a):
    o, s_all = _gdn_rec_fwd_call(*a)
    return o, a + (s_all,)


gdn_rec.defvjp(_gdn_rec_fwd, lambda res, do: tuple(_gdn_rec_bwd_call(*res, do)))


def gated_delta(qkv, b, g):
    return gdn_rec(*gdn_intra(qkv, b, g))


def _loss_call(y, tgt, first, last):
    r_total, d = y.shape
    tm = _tile(r_total, 256, 8)

    def body(y_ref, t_ref, loss_ref, dy_ref):
        i = pl.program_id(0)

        @pl.when(i == 0)
        def _():
            loss_ref[...] = jnp.zeros_like(loss_ref)

        row = lax.broadcasted_iota(jnp.int32, (tm, d), 0) + i * tm
        err = jnp.where((row >= first) & (row < last), y_ref[...] - t_ref[...], 0.0)
        dy_ref[...] = err * (1.0 / d)
        part = jnp.sum(jnp.sum(err * err, axis=1, keepdims=True), axis=0, keepdims=True) * (0.5 / d)
        loss_ref[...] += jnp.broadcast_to(part, loss_ref.shape)

    return _call(
        body, name="loss_head", grid=(r_total // tm,),
        in_specs=[pl.BlockSpec((tm, d), lambda i: (i, 0))] * 2,
        out_specs=[pl.BlockSpec((8, LANE), lambda i: (0, 0)), pl.BlockSpec((tm, d), lambda i: (i, 0))],
        out_shape=[jax.ShapeDtypeStruct((8, LANE), F32), jax.ShapeDtypeStruct(y.shape, F32)],
        compiler_params=_cparams(("arbitrary",)),
    )(y, tgt)


def make_loss(first, last):
    @jax.custom_vjp
    def op(y, tgt):
        return _loss_call(y, tgt, first, last)[0][0, 0]

    def fwd(y, tgt):
        loss, dy = _loss_call(y, tgt, first, last)
        return loss[0, 0], (dy,)

    def bwd(res, ct):
        return res[0] * ct, jnp.zeros_like(res[0])

    op.defvjp(fwd, bwd)
    return op


def _pad_rows8(w):
    return jnp.concatenate([w, jnp.zeros((8 - w.shape[0], w.shape[1]), w.dtype)], axis=0)


def local_loss(wts, x, tgt):
    seq = x.shape[0]
    n_tok = N_META + seq
    t_pad = -(-n_tok // ROW_ALIGN) * ROW_ALIGN
    depth = wts["norm1_g"].shape[0]
    blk = _tile(t_pad, ATT_BLK, LANE)
    nb = t_pad // blk
    tm = _tile(t_pad, 256, 8)

    rms = rowop(_f_rmsnorm, "rmsnorm", (D_MODEL,), tm)
    qnorm = rowop(_f_qnorm, "fox_q_norm", (FOX_HD,), _tile(FOX_HEADS * t_pad, 2048, 8))
    knorm = rowop(_f_rmsnorm, "fox_k_norm", (FOX_HD,), _tile(FOX_HEADS * t_pad, 2048, 8))
    logsig = rowop(_f_logsig, "fox_log_forget", (FOX_HEADS,), tm)
    tm_head = _tile(t_pad, 768, 8)
    gdn_act = rowop(_f_gdn_act, "gdn_qkv_act", (GDN_HD,), tm_head, ncb=3 * GDN_HEADS)
    gates = rowop(_f_gdn_gates, "gdn_gates", (GDN_HEADS, GDN_HEADS), tm)
    gdn_out = rowop(_f_gdn_out, "gdn_out_norm", (GDN_HD,), tm_head, ncb=GDN_HEADS)
    merge = rowop(_f_merge, "branch_merge", (D_MODEL,), tm)
    residual = rowop(_f_residual, "residual_add", (D_MODEL,), tm, bc=(2,))
    keep = (jnp.arange(t_pad)[:, None] < n_tok).astype(F32)
    conv4 = make_dwconv(GDN_CONV)
    conv3 = make_dwconv(FFN_CONV)
    loss_op = make_loss(N_META, n_tok)

    zeros = jnp.zeros((t_pad - n_tok, D_MODEL), F32)
    h_res = jnp.concatenate([wts["meta_tokens"], x, zeros], axis=0)
    tgt_rows = jnp.concatenate([jnp.zeros((N_META, D_MODEL), F32), tgt, zeros], axis=0)

    def heads(a):
        return a.reshape(t_pad, FOX_HEADS, FOX_HD).transpose(1, 0, 2).reshape(FOX_HEADS * t_pad, FOX_HD)

    for l in range(depth):
        h = rms((h_res,), (wts["norm1_g"][l][None],))[0]
        proj = mm(h, wts["w_in"][l])
        qn = qnorm((heads(proj[:, 0:512]),), (wts["fox_q_norm_g"][l][None],))[0]
        kn = knorm((heads(proj[:, 512:1024]),), (wts["fox_k_norm_g"][l][None],))[0]
        vh = heads(proj[:, 1024:1536])
        log_f = logsig((proj[:, 1536:1544],), (wts["fox_f_bias"][l][None],))[0]
        f_cum = cumsum_lanes(log_f.T)
        o_a = fox_attention(qn.reshape(FOX_HEADS, t_pad, FOX_HD), kn.reshape(FOX_HEADS, t_pad, FOX_HD),
                            vh.reshape(FOX_HEADS, t_pad, FOX_HD), f_cum[:, :, None],
                            f_cum.reshape(FOX_HEADS, nb, 1, blk))
        y_a = mm(o_a.transpose(2, 0, 1).reshape(t_pad, FOX_W), wts["w_branch_a"][l])
        cv = conv4(proj[:, 1664:4736], _pad_rows8(wts["gdn_conv_w"][l]))
        qkv = gdn_act((cv,), ())[0]
        beta, gdec = gates((proj[:, 4736:4744], proj[:, 4744:4752]),
                           (wts["gdn_a_log"][l][None], wts["gdn_dt_bias"][l][None]))
        o_b = gated_delta(qkv, beta, gdec)
        o_b = gdn_out((o_b, proj[:, 4864:5888]), (wts["gdn_norm_g"][l][None],))[0]
        y_b = mm(o_b, wts["w_branch_b"][l])
        mixed = merge((proj[:, 5888:6912], proj[:, 6912:7936], y_a, y_b), ())[0]
        h_res = residual((h_res, mm(mixed, wts["w_out"][l]), keep), ())[0]
        h = rms((h_res,), (wts["norm2_g"][l][None],))[0]
        up = conv3(mm(h, wts["w_up"][l]), _pad_rows8(wts["ffn_conv_w"][l]))
        act = glu(up)
        h_res = residual((h_res, mm(act, wts["w_down"][l]), keep), ())[0]
    return loss_op(h_res, tgt_rows)


def pad_w_in(w):
    parts, pos = [], 0
    for src, width, dst in IN_SEGS:
        if dst > pos:
            parts.append(jnp.zeros(w.shape[:-1] + (dst - pos,), w.dtype))
        parts.append(w[..., src:src + width])
        pos = dst + width
    parts.append(jnp.zeros(w.shape[:-1] + (D_IN_PAD - pos,), w.dtype))
    return jnp.concatenate(parts, axis=-1)


def unpad_w_in(w):
    return jnp.concatenate([w[..., dst:dst + width] for _, width, dst in IN_SEGS], axis=-1)


ANY = pl.BlockSpec(memory_space=pl.ANY)
N_CHIPS = 4
N_DEV = 8
COMM_COLS = 1024
COMM_ROW_ALIGN = 512
COMM_ROW_ALIGN_SMALL = 32


def _place():
    return lax.axis_index("x"), lax.axis_index("y"), lax.axis_index("c")


def _other_chips(x, y):
    return [(1 - x, y), (x, 1 - y), (1 - x, 1 - y)]


def _remote(src, dst, send_sem, recv_sem, dev):
    return pltpu.make_async_remote_copy(src_ref=src, dst_ref=dst, send_sem=send_sem, recv_sem=recv_sem,
                                        device_id=dev, device_id_type=MESH)


def chip_all_gather(buf):
    rows, cols = buf.shape
    half = rows // 2

    def body(x_ref, out_ref, send_sems, recv_sems, pass_send, pass_recv):
        x, y, c = _place()
        me = 2 * x + y
        mine, other = pl.ds(c * half, half), pl.ds((1 - c) * half, half)
        sibling = (x, y, 1 - c)
        chips = _other_chips(x, y)
        started = []
        for k, (px, py) in enumerate(chips):
            cp = _remote(x_ref.at[mine], out_ref.at[me, mine], send_sems.at[k], recv_sems.at[k], (px, py, c))
            cp.start()
            started.append(cp)
        for k, (px, py) in enumerate(chips):
            landed = out_ref.at[2 * px + py, mine]
            _remote(landed, landed, send_sems.at[k], recv_sems.at[k], (px, py, c)).wait_recv()
            cp = _remote(landed, landed, pass_send.at[k], pass_recv.at[k], sibling)
            cp.start()
            started.append(cp)
        for k, (px, py) in enumerate(chips):
            passed = out_ref.at[2 * px + py, other]
            _remote(passed, passed, pass_send.at[k], pass_recv.at[k], sibling).wait_recv()
        for cp in started:
            cp.wait_send()

    got = _call(
        body, name="chip_all_gather", in_specs=[ANY], out_specs=ANY,
        out_shape=jax.ShapeDtypeStruct((N_CHIPS, rows, cols), buf.dtype),
        scratch_shapes=[pltpu.SemaphoreType.DMA((3,)), pltpu.SemaphoreType.DMA((3,)),
                        pltpu.SemaphoreType.DMA((3,)), pltpu.SemaphoreType.DMA((3,))],
    )(buf)
    me = 2 * lax.axis_index("x") + lax.axis_index("y")
    return lax.dynamic_update_slice(got, buf[None], (me, 0, 0))


def sibling_swap_halves(g4):
    n, rows, cols = g4.shape
    half = rows // 2

    def body(g_ref, got_ref, send_sem, recv_sem):
        x, y, c = _place()
        cp = _remote(g_ref.at[:, pl.ds((1 - c) * half, half), :], got_ref, send_sem, recv_sem, (x, y, 1 - c))
        cp.start()
        cp.wait()

    return _call(
        body, name="sibling_swap_halves", in_specs=[ANY], out_specs=ANY,
        out_shape=jax.ShapeDtypeStruct((n, half, cols), g4.dtype),
        scratch_shapes=[pltpu.SemaphoreType.DMA, pltpu.SemaphoreType.DMA],
    )(g4)


def add_own_half(g4, got, c):
    n, rows, cols = g4.shape
    half = rows // 2
    tm = _tile(half, 256, 16)
    nt = half // tm

    def body(c_ref, a_ref, b_ref, o_ref):
        o_ref[...] = (a_ref[...] + b_ref[...]).astype(o_ref.dtype)

    return _call(
        body, name="add_own_half",
        grid_spec=pltpu.PrefetchScalarGridSpec(
            num_scalar_prefetch=1, grid=(n, nt),
            in_specs=[pl.BlockSpec((1, tm, cols), lambda j, i, c_ref: (j, c_ref[0] * nt + i, 0)),
                      pl.BlockSpec((1, tm, cols), lambda j, i, c_ref: (j, i, 0))],
            out_specs=pl.BlockSpec((1, tm, cols), lambda j, i, c_ref: (j, i, 0))),
        out_shape=jax.ShapeDtypeStruct(got.shape, BF16),
        compiler_params=_cparams(("parallel", "parallel")),
    )(c.reshape(1).astype(jnp.int32), g4, got)


def chip_scatter(p4):
    n, rows, cols = p4.shape

    def body(p_ref, out_ref, send_sems, recv_sems):
        x, y, c = _place()
        me = 2 * x + y
        chips = _other_chips(x, y)
        started = []
        for k, (px, py) in enumerate(chips):
            cp = _remote(p_ref.at[2 * px + py], out_ref.at[me], send_sems.at[k], recv_sems.at[k], (px, py, c))
            cp.start()
            started.append(cp)
        for k, (px, py) in enumerate(chips):
            landed = out_ref.at[2 * px + py]
            _remote(landed, landed, send_sems.at[k], recv_sems.at[k], (px, py, c)).wait_recv()
        for cp in started:
            cp.wait_send()

    got = _call(
        body, name="chip_scatter", in_specs=[ANY], out_specs=ANY,
        out_shape=jax.ShapeDtypeStruct(p4.shape, p4.dtype),
        scratch_shapes=[pltpu.SemaphoreType.DMA((3,)), pltpu.SemaphoreType.DMA((3,))],
    )(p4)
    me = 2 * lax.axis_index("x") + lax.axis_index("y")
    return lax.dynamic_update_slice(got, lax.dynamic_slice_in_dim(p4, me, 1, axis=0), (me, 0, 0))


def sum_slots(a):
    n, rows, cols = a.shape
    tm = _tile(rows, 256, 16) if rows % 16 == 0 else rows

    def body(a_ref, o_ref):
        acc = a_ref[0].astype(F32)
        for k in range(1, n):
            acc = acc + a_ref[k].astype(F32)
        o_ref[...] = acc

    return _call(
        body, name="sum_slots_%d" % n, grid=(rows // tm,),
        in_specs=[pl.BlockSpec((n, tm, cols), lambda i: (0, i, 0))],
        out_specs=pl.BlockSpec((tm, cols), lambda i: (i, 0)),
        out_shape=jax.ShapeDtypeStruct((rows, cols), F32),
        compiler_params=_cparams(("parallel",)),
    )(a)


def sibling_join(s):
    half, cols = s.shape

    def body(s_ref, got_ref, send_sem, recv_sem):
        x, y, c = _place()
        cp = _remote(s_ref, got_ref, send_sem, recv_sem, (x, y, 1 - c))
        cp.start()
        cp.wait()

    got = _call(
        body, name="sibling_join", in_specs=[ANY], out_specs=ANY,
        out_shape=jax.ShapeDtypeStruct(s.shape, s.dtype),
        scratch_shapes=[pltpu.SemaphoreType.DMA, pltpu.SemaphoreType.DMA],
    )(s)
    c = lax.axis_index("c")
    out = jnp.zeros((2 * half, cols), s.dtype)
    out = lax.dynamic_update_slice(out, s, (c * half, 0))
    return lax.dynamic_update_slice(out, got, ((1 - c) * half, 0))


def all_devices_gather(buf):
    rows, cols = buf.shape

    def body(b_ref, out_ref, send_sems, recv_sems, local_sem):
        x, y, c = _place()
        me = 4 * x + 2 * y + c
        local = pltpu.make_async_copy(b_ref, out_ref.at[me], local_sem)
        local.start()
        peers = [((x + dx) % 2, (y + dy) % 2, (c + dc) % 2)
                 for dx in (0, 1) for dy in (0, 1) for dc in (0, 1) if dx + dy + dc > 0]
        started = []
        for k, peer in enumerate(peers):
            cp = _remote(b_ref, out_ref.at[me], send_sems.at[k], recv_sems.at[k], peer)
            cp.start()
            started.append(cp)
        for k, (px, py, pc) in enumerate(peers):
            landed = out_ref.at[4 * px + 2 * py + pc]
            _remote(landed, landed, send_sems.at[k], recv_sems.at[k], (px, py, pc)).wait_recv()
        for cp in started:
            cp.wait_send()
        local.wait()

    return _call(
        body, name="all_devices_gather", in_specs=[ANY], out_specs=ANY,
        out_shape=jax.ShapeDtypeStruct((N_DEV, rows, cols), buf.dtype),
        scratch_shapes=[pltpu.SemaphoreType.DMA((7,)), pltpu.SemaphoreType.DMA((7,)), pltpu.SemaphoreType.DMA],
    )(buf)


def adamw(w, g, m, v):
    shape = w.shape
    w2, g2, m2, v2 = [a.reshape(-1, shape[-1]) for a in (w, g, m, v)]
    rows, cols = w2.shape
    tm = _tile(rows, 256, 8) if rows % 8 == 0 else rows

    def body(w_ref, g_ref, m_ref, v_ref, d_ref, nm_ref, nv_ref):
        gv = g_ref[...]
        nm = ADAM_B1 * m_ref[...] + (1.0 - ADAM_B1) * gv
        nv = ADAM_B2 * v_ref[...] + (1.0 - ADAM_B2) * (gv * gv)
        m_hat = nm / (1.0 - ADAM_B1 ** ADAM_STEP)
        v_hat = nv / (1.0 - ADAM_B2 ** ADAM_STEP)
        d_ref[...] = -ADAM_LR * (m_hat / (jnp.sqrt(v_hat) + ADAM_EPS) + ADAM_WD * w_ref[...])
        nm_ref[...] = nm
        nv_ref[...] = nv

    spec = pl.BlockSpec((tm, cols), lambda i: (i, 0))
    outs = _call(
        body, name="adamw", grid=(rows // tm,), in_specs=[spec] * 4, out_specs=[spec] * 3,
        out_shape=[jax.ShapeDtypeStruct((rows, cols), F32)] * 3,
        compiler_params=_cparams(("parallel",)),
    )(w2, g2, m2, v2)
    return [o.reshape(shape) for o in outs]


WEIGHTS = ("meta_tokens", "norm1_g", "w_in", "fox_f_bias", "fox_q_norm_g", "fox_k_norm_g", "gdn_conv_w",
           "gdn_a_log", "gdn_dt_bias", "gdn_norm_g", "w_branch_a", "w_branch_b", "w_out", "norm2_g", "w_up",
           "ffn_conv_w", "w_down")
SHARD_AXIS = {"meta_tokens": -1, "w_in": -1, "gdn_conv_w": -1, "w_branch_a": -1, "w_branch_b": -2, "w_out": -2,
              "w_up": -1, "ffn_conv_w": -1, "w_down": -2}
MATMUL_WEIGHTS = ("w_in", "w_branch_a", "w_branch_b", "w_out", "w_up", "w_down")
SMALL_SHARDED = ("meta_tokens", "gdn_conv_w", "ffn_conv_w")
REPLICATED = tuple(n for n in WEIGHTS if n not in SHARD_AXIS)


def _pack(arrays, dtype, row_align):
    flat = jnp.concatenate([a.reshape(-1).astype(dtype) for a in arrays])
    block = row_align * COMM_COLS
    total = -(-flat.shape[0] // block) * block
    flat = jnp.concatenate([flat, jnp.zeros((total - flat.shape[0],), dtype)])
    return flat.reshape(-1, COMM_COLS)


def _unpack(buf, shapes):
    flat, out, pos = buf.reshape(-1), [], 0
    for shape in shapes:
        size = 1
        for d in shape:
            size *= d
        out.append(flat[pos:pos + size].reshape(shape))
        pos += size
    return out


def _gather_full(shards, names, dtype, row_align):
    got = chip_all_gather(_pack([shards[n] for n in names], dtype, row_align))
    per_chip = [_unpack(got[j], [shards[n].shape for n in names]) for j in range(N_CHIPS)]
    return {n: jnp.concatenate([per_chip[j][i] for j in range(N_CHIPS)], axis=SHARD_AXIS[n]).astype(F32)
            for i, n in enumerate(names)}


def _shard_of(full, name, j):
    axis = SHARD_AXIS[name] % full.ndim
    size = full.shape[axis] // N_CHIPS
    return lax.slice_in_dim(full, j * size, (j + 1) * size, axis=axis)


def kernel(x, meta_tokens, norm1_g, w_in, fox_f_bias, fox_q_norm_g, fox_k_norm_g, gdn_conv_w, gdn_a_log, gdn_dt_bias, gdn_norm_g, w_branch_a, w_branch_b, w_out, norm2_g, w_up, ffn_conv_w, w_down, loss_target, m_meta_tokens, m_norm1_g, m_w_in, m_fox_f_bias, m_fox_q_norm_g, m_fox_k_norm_g, m_gdn_conv_w, m_gdn_a_log, m_gdn_dt_bias, m_gdn_norm_g, m_w_branch_a, m_w_branch_b, m_w_out, m_norm2_g, m_w_up, m_ffn_conv_w, m_w_down, v_meta_tokens, v_norm1_g, v_w_in, v_fox_f_bias, v_fox_q_norm_g, v_fox_k_norm_g, v_gdn_conv_w, v_gdn_a_log, v_gdn_dt_bias, v_gdn_norm_g, v_w_branch_a, v_w_branch_b, v_w_out, v_norm2_g, v_w_up, v_ffn_conv_w, v_w_down):
    w_loc = dict(zip(WEIGHTS, (meta_tokens, norm1_g, w_in, fox_f_bias, fox_q_norm_g, fox_k_norm_g, gdn_conv_w,
                               gdn_a_log, gdn_dt_bias, gdn_norm_g, w_branch_a, w_branch_b, w_out, norm2_g, w_up,
                               ffn_conv_w, w_down)))
    m_loc = dict(zip(WEIGHTS, (m_meta_tokens, m_norm1_g, m_w_in, m_fox_f_bias, m_fox_q_norm_g, m_fox_k_norm_g,
                               m_gdn_conv_w, m_gdn_a_log, m_gdn_dt_bias, m_gdn_norm_g, m_w_branch_a, m_w_branch_b,
                               m_w_out, m_norm2_g, m_w_up, m_ffn_conv_w, m_w_down)))
    v_loc = dict(zip(WEIGHTS, (v_meta_tokens, v_norm1_g, v_w_in, v_fox_f_bias, v_fox_q_norm_g, v_fox_k_norm_g,
                               v_gdn_conv_w, v_gdn_a_log, v_gdn_dt_bias, v_gdn_norm_g, v_w_branch_a, v_w_branch_b,
                               v_w_out, v_norm2_g, v_w_up, v_ffn_conv_w, v_w_down)))
    c = lax.axis_index("c")

    full = {n: w_loc[n] for n in REPLICATED}
    full.update(_gather_full(w_loc, MATMUL_WEIGHTS, BF16, COMM_ROW_ALIGN))
    full.update(_gather_full(w_loc, SMALL_SHARDED, F32, COMM_ROW_ALIGN_SMALL))
    full["w_in"] = pad_w_in(full["w_in"])

    loss, (g_full, g_x) = jax.value_and_grad(local_loss, argnums=(0, 1))(full, x[0], loss_target[0])
    g_full = dict(g_full)
    g_full["w_in"] = unpad_w_in(g_full["w_in"])

    sharded = MATMUL_WEIGHTS + SMALL_SHARDED
    g4 = jnp.stack([_pack([_shard_of(g_full[n], n, j) for n in sharded], F32, COMM_ROW_ALIGN)
                    for j in range(N_CHIPS)])
    pair_sum = add_own_half(g4, sibling_swap_halves(g4), c)
    g_shard = sibling_join(sum_slots(chip_scatter(pair_sum)))
    grads = dict(zip(sharded, _unpack(g_shard, [w_loc[n].shape for n in sharded])))
    g_rep = sum_slots(all_devices_gather(_pack([g_full[n] for n in REPLICATED], F32, 8)))
    grads.update(zip(REPLICATED, _unpack(g_rep, [w_loc[n].shape for n in REPLICATED])))

    loss = lax.psum(loss, ("x", "y", "c"))
    upd = {n: adamw(w_loc[n], grads[n], m_loc[n], v_loc[n]) for n in WEIGHTS}
    return (loss, g_x[None], *[grads[n] for n in WEIGHTS], *[upd[n][0] for n in WEIGHTS],
            *[upd[n][1] for n in WEIGHTS], *[upd[n][2] for n in WEIGHTS])
```

```python
import functools

import jax
import jax.numpy as jnp
from jax import lax
from jax.experimental import pallas as pl
from jax.experimental.pallas import tpu as pltpu

F32 = jnp.float32
BF16 = jnp.bfloat16
HI = lax.Precision.HIGHEST
MESH = pl.DeviceIdType.MESH

D_MODEL = 1024
N_META = 16
EPS = 1e-6
FOX_HEADS, FOX_HD = 8, 64
FOX_W = FOX_HEADS * FOX_HD
GDN_HEADS, GDN_HD, GDN_CHUNK, GDN_CONV = 8, 128, 64, 4
GDN_W = GDN_HEADS * GDN_HD
D_FF = 2816
FFN_CONV = 3
D_IN = 7704
D_IN_PAD = 8192
IN_SEGS = ((0, 1536, 0), (1536, 8, 1536), (1544, 3072, 1664), (4616, 16, 4736), (4632, 1024, 4864), (5656, 2048, 5888))
ROW_ALIGN = 256
ATT_BLK = 256
VMEM_LIMIT = 48 * 1024 * 1024
LANE = 128

ADAM_LR, ADAM_B1, ADAM_B2, ADAM_EPS, ADAM_WD, ADAM_STEP = 0.001, 0.9, 0.999, 1e-08, 0.01, 10


def _call(body, **kw):
    return pl.pallas_call(body, **kw)


def _tile(n, target, mult):
    best, t = None, mult
    while t <= min(n, target):
        if n % t == 0:
            best = t
        t += mult
    assert best is not None, (n, target, mult)
    return best


def _cparams(sem):
    return pltpu.CompilerParams(dimension_semantics=sem, vmem_limit_bytes=VMEM_LIMIT)


def _raw_dot(a, b, ca, cb, precise):
    dims = (((ca,), (cb,)), ((), ()))
    a_hi, b_hi = a.astype(BF16), b.astype(BF16)
    out = lax.dot_general(a_hi, b_hi, dims, preferred_element_type=F32)
    if precise:
        a_lo = (a - a_hi.astype(F32)).astype(BF16)
        b_lo = (b - b_hi.astype(F32)).astype(BF16)
        out = out + (lax.dot_general(a_hi, b_lo, dims, preferred_element_type=F32)
                     + lax.dot_general(a_lo, b_hi, dims, preferred_element_type=F32))
    return out


def _make_dot(ca, cb, precise):
    @jax.custom_vjp
    def f(a, b):
        return _raw_dot(a, b, ca, cb, precise)

    def fwd(a, b):
        return f(a, b), (a, b)

    def bwd(res, ct):
        a, b = res
        if ca == 1:
            da = _raw_dot(ct, b, 1, 1 if cb == 0 else 0, precise)
        else:
            da = _raw_dot(b, ct, 1 if cb == 0 else 0, 1, precise)
        if cb == 0:
            db = _raw_dot(a, ct, 0 if ca == 1 else 1, 0, precise)
        else:
            db = _raw_dot(ct, a, 0, 0 if ca == 1 else 1, precise)
        return da, db

    f.defvjp(fwd, bwd)
    return f


_DOTS = {(ca, cb, p): _make_dot(ca, cb, p) for ca in (0, 1) for cb in (0, 1) for p in (False, True)}


def _dot(a, b, ca=1, cb=0, precise=False):
    return _DOTS[(ca, cb, precise)](a, b)


def _mm_call(a, b, name):
    m, k = a.shape
    n = b.shape[1]
    assert k == b.shape[0], (a.shape, b.shape)
    tm = _tile(m, 768, 16)
    tn = _tile(n, 1408, LANE)
    tk = _tile(k, 1408, LANE)
    nk = k // tk

    def body(a_ref, b_ref, o_ref, *scratch):
        part = jnp.dot(a_ref[...], b_ref[...], preferred_element_type=F32)
        if nk == 1:
            o_ref[...] = part
            return
        acc_ref = scratch[0]
        kk = pl.program_id(2)

        @pl.when(kk == 0)
        def _():
            acc_ref[...] = part

        @pl.when(kk > 0)
        def _():
            acc_ref[...] += part

        @pl.when(kk == nk - 1)
        def _():
            o_ref[...] = acc_ref[...]

    return _call(
        body, name=name, grid=(m // tm, n // tn, nk),
        in_specs=[pl.BlockSpec((tm, tk), lambda i, j, kk: (i, kk)), pl.BlockSpec((tk, tn), lambda i, j, kk: (kk, j))],
        out_specs=pl.BlockSpec((tm, tn), lambda i, j, kk: (i, j)),
        out_shape=jax.ShapeDtypeStruct((m, n), F32),
        scratch_shapes=[pltpu.VMEM((tm, tn), F32)] if nk > 1 else [],
        compiler_params=_cparams(("parallel", "parallel", "arbitrary")),
    )(a, b)


@jax.custom_vjp
def mm(a, w):
    return _mm_call(a.astype(BF16), w.astype(BF16), "mm_fwd")


def _mm_fwd(a, w):
    a_b, w_b = a.astype(BF16), w.astype(BF16)
    return _mm_call(a_b, w_b, "mm_fwd"), (a_b, w_b)


def _mm_bwd(res, ct):
    a_b, w_b = res
    ct_b = ct.astype(BF16)
    return _mm_call(ct_b, w_b.T, "mm_dx"), _mm_call(a_b.T, ct_b, "mm_dw")


mm.defvjp(_mm_fwd, _mm_bwd)


def _rows_specs(rows, tm, ncb, bc):
    specs = []
    for idx, r in enumerate(rows):
        if idx in bc:
            specs.append(pl.BlockSpec((tm, r.shape[1]), lambda i, j: (i, 0)))
        else:
            specs.append(pl.BlockSpec((tm, r.shape[1] // ncb), lambda i, j: (i, j)))
    return specs


def _param_specs(params):
    return [pl.BlockSpec(p.shape, lambda i, j: (0, 0)) for p in params]


def _group_slices(refs, groups, g, whole):
    out = []
    for idx, r in enumerate(refs):
        w = r.shape[1] // groups
        out.append(r[...] if idx in whole else r[:, g * w:(g + 1) * w])
    return out


def _rows_fwd_call(fns, rows, params, outs, tm, ncb, bc, name):
    r_total = rows[0].shape[0]
    nr, groups = len(rows), len(fns)

    def body(*refs):
        pvals = [r[...] for r in refs[nr:nr + len(params)]]
        for g, fn in enumerate(fns):
            res = fn(*_group_slices(refs[:nr], groups, g, bc), *pvals)
            for o_ref, val in zip(refs[nr + len(params):], res):
                w = o_ref.shape[1] // groups
                o_ref[:, g * w:(g + 1) * w] = val.astype(o_ref.dtype)

    return _call(
        body, name=name, grid=(r_total // tm, ncb),
        in_specs=_rows_specs(rows, tm, ncb, bc) + _param_specs(params),
        out_specs=[pl.BlockSpec((tm, w * groups), lambda i, j: (i, j)) for w in outs],
        out_shape=[jax.ShapeDtypeStruct((r_total, w * groups * ncb), F32) for w in outs],
        compiler_params=_cparams(("parallel", "parallel")),
    )(*rows, *params)


def _rows_bwd_call(fns, rows, params, cts, tm, ncb, bc, name):
    r_total = rows[0].shape[0]
    nr, npar, nct, groups = len(rows), len(params), len(cts), len(fns)

    def body(*refs):
        i, j = pl.program_id(0), pl.program_id(1)
        pvals = [r[...] for r in refs[nr:nr + npar]]
        ct_refs = refs[nr + npar:nr + npar + nct]
        d_refs = refs[nr + npar + nct:]
        shared = {idx: None for idx in list(bc) + list(range(nr, nr + npar))}
        for g, fn in enumerate(fns):
            _, vjp = jax.vjp(lambda *a, fn=fn: tuple(fn(*a)), *_group_slices(refs[:nr], groups, g, bc), *pvals)
            grads = vjp(tuple(_group_slices(ct_refs, groups, g, ())))
            for idx in range(nr + npar):
                if idx in shared:
                    shared[idx] = grads[idx] if shared[idx] is None else shared[idx] + grads[idx]
                else:
                    w = d_refs[idx].shape[1] // groups
                    d_refs[idx][:, g * w:(g + 1) * w] = grads[idx]
        for idx, total in shared.items():
            first = (j == 0) if idx < nr else ((i == 0) & (j == 0))

            @pl.when(first)
            def _(idx=idx):
                d_refs[idx][...] = jnp.zeros_like(d_refs[idx])
            d_refs[idx][...] += total

    ct_specs = [pl.BlockSpec((tm, c.shape[1] // ncb), lambda i, j: (i, j)) for c in cts]
    return _call(
        body, name=name + "_bwd", grid=(r_total // tm, ncb),
        in_specs=_rows_specs(rows, tm, ncb, bc) + _param_specs(params) + ct_specs,
        out_specs=_rows_specs(rows, tm, ncb, bc) + _param_specs(params),
        out_shape=[jax.ShapeDtypeStruct(a.shape, F32) for a in list(rows) + list(params)],
        compiler_params=_cparams(("arbitrary", "arbitrary")),
    )(*rows, *params, *cts)


def rowop(fn, name, outs, tm, ncb=1, bc=()):
    fns = list(fn) if isinstance(fn, (list, tuple)) else [fn]

    @jax.custom_vjp
    def op(rows, params):
        return tuple(_rows_fwd_call(fns, rows, params, outs, tm, ncb, bc, name))

    def fwd(rows, params):
        return op(rows, params), (rows, params)

    def bwd(res, cts):
        rows, params = res
        d = _rows_bwd_call(fns, rows, params, cts, tm, ncb, bc, name)
        return tuple(d[:len(rows)]), tuple(d[len(rows):])

    op.defvjp(fwd, bwd)
    return op


def _sigmoid(x):
    return 1.0 / (1.0 + jnp.exp(-x))


def _silu(x):
    return x * _sigmoid(x)


def _softplus(x):
    return jnp.maximum(x, 0.0) + jnp.log(1.0 + jnp.exp(-jnp.abs(x)))


def _f_rmsnorm(x, g):
    return (x * lax.rsqrt(jnp.mean(x * x, axis=-1, keepdims=True) + EPS) * g,)


def _f_qnorm(x, g):
    return (x * lax.rsqrt(jnp.mean(x * x, axis=-1, keepdims=True) + EPS) * (g * (FOX_HD ** -0.5)),)


def _f_logsig(x, b):
    return (-_softplus(-(x + b)),)


def _f_gdn_q(x):
    y = _silu(x)
    return (y * lax.rsqrt(jnp.sum(y * y, axis=-1, keepdims=True) + EPS) * (GDN_HD ** -0.5),)


def _f_gdn_k(x):
    y = _silu(x)
    return (y * lax.rsqrt(jnp.sum(y * y, axis=-1, keepdims=True) + EPS),)


def _f_gdn_v(x):
    return (_silu(x),)


def _f_gdn_gates(bl, al, a_log, dt_bias):
    return _sigmoid(bl), -jnp.exp(a_log) * _softplus(al + dt_bias)


def _f_gdn_out(o, z, g):
    return (o * lax.rsqrt(jnp.mean(o * o, axis=-1, keepdims=True) + EPS) * g * _silu(z),)


def _f_merge(g0, g1, ya, yb):
    return (_sigmoid(g0) * ya + _sigmoid(g1) * yb,)


def _f_residual(a, b, keep):
    return ((a + b) * keep,)


def _f_residual_norm(a, b, keep, g):
    r = (a + b) * keep
    return r, _f_rmsnorm(r, g)[0]


def _f_glu(a, b):
    return (_silu(a) * b,)


def _glu_call(up, ct):
    t_total, two_f = up.shape
    f = two_f // 2
    tm = _tile(t_total, 128, 8)
    wc = _tile(f, 1408, LANE)

    def body(*refs):
        up_ref, out_ref = refs[0], refs[-1]
        for c0 in range(0, f, wc):
            a, b = up_ref[:, c0:c0 + wc], up_ref[:, f + c0:f + c0 + wc]
            if ct is None:
                out_ref[:, c0:c0 + wc] = _f_glu(a, b)[0]
            else:
                _, vjp = jax.vjp(_f_glu, a, b)
                da, db = vjp((refs[1][:, c0:c0 + wc],))
                out_ref[:, c0:c0 + wc] = da
                out_ref[:, f + c0:f + c0 + wc] = db

    wide = pl.BlockSpec((tm, two_f), lambda i: (i, 0))
    narrow = pl.BlockSpec((tm, f), lambda i: (i, 0))
    return _call(
        body, name="ffn_glu" if ct is None else "ffn_glu_bwd", grid=(t_total // tm,),
        in_specs=[wide] if ct is None else [wide, narrow], out_specs=narrow if ct is None else wide,
        out_shape=jax.ShapeDtypeStruct((t_total, f if ct is None else two_f), F32),
        compiler_params=_cparams(("parallel",)),
    )(*((up,) if ct is None else (up, ct)))


@jax.custom_vjp
def glu(up):
    return _glu_call(up, None)


glu.defvjp(lambda up: (glu(up), up), lambda up, ct: (_glu_call(up, ct),))


def _shift_down(x, halo, s, row8):
    rx = pltpu.roll(x, s, 0)
    top = jnp.where(row8 < s, pltpu.roll(halo, s, 0), rx[:8])
    return jnp.concatenate([top, rx[8:]], axis=0)


def _shift_up(x, nxt, s, row8):
    tm = x.shape[0]
    rx = pltpu.roll(x, tm - s, 0)
    bot = jnp.where(row8 >= 8 - s, pltpu.roll(nxt, 8 - s, 0), rx[tm - 8:])
    return jnp.concatenate([rx[:tm - 8], bot], axis=0)


def _conv_tiles(r_total, c_total):
    return _tile(r_total, 768, 8), _tile(c_total, 1408, LANE)


def _conv_fwd_call(x, w8, k_taps):
    r_total, c_total = x.shape
    tm, tc = _conv_tiles(r_total, c_total)
    hb = tm // 8

    def body(x_ref, halo_ref, w_ref, y_ref):
        i = pl.program_id(1)
        xt = x_ref[...]
        halo = jnp.where(i > 0, halo_ref[...], 0.0)
        row8 = lax.broadcasted_iota(jnp.int32, (8, tc), 0)
        acc = w_ref[k_taps - 1:k_taps, :] * xt
        for k in range(k_taps - 1):
            acc += w_ref[k:k + 1, :] * _shift_down(xt, halo, k_taps - 1 - k, row8)
        y_ref[...] = acc

    return _call(
        body, name="dwconv_fwd", grid=(c_total // tc, r_total // tm),
        in_specs=[pl.BlockSpec((tm, tc), lambda c, i: (i, c)),
                  pl.BlockSpec((8, tc), lambda c, i: (jnp.maximum(i * hb - 1, 0), c)),
                  pl.BlockSpec((8, tc), lambda c, i: (0, c))],
        out_specs=pl.BlockSpec((tm, tc), lambda c, i: (i, c)),
        out_shape=jax.ShapeDtypeStruct(x.shape, F32),
        compiler_params=_cparams(("parallel", "parallel")),
    )(x, x, w8)


def _conv_bwd_call(x, w8, dy, k_taps):
    r_total, c_total = x.shape
    tm, tc = _conv_tiles(r_total, c_total)
    hb = tm // 8
    n_i = r_total // tm

    def body(x_ref, halo_ref, w_ref, dy_ref, nxt_ref, dx_ref, dw_ref):
        i = pl.program_id(1)
        xt, dyt = x_ref[...], dy_ref[...]
        halo = jnp.where(i > 0, halo_ref[...], 0.0)
        nxt = jnp.where(i < n_i - 1, nxt_ref[...], 0.0)
        row8 = lax.broadcasted_iota(jnp.int32, (8, tc), 0)
        dx = w_ref[k_taps - 1:k_taps, :] * dyt
        upd = jnp.where(row8 == k_taps - 1, jnp.sum(dyt * xt, axis=0, keepdims=True), 0.0)
        for k in range(k_taps - 1):
            s = k_taps - 1 - k
            dx += w_ref[k:k + 1, :] * _shift_up(dyt, nxt, s, row8)
            upd = jnp.where(row8 == k, jnp.sum(dyt * _shift_down(xt, halo, s, row8), axis=0, keepdims=True), upd)
        dx_ref[...] = dx

        @pl.when(i == 0)
        def _():
            dw_ref[...] = jnp.zeros_like(dw_ref)

        dw_ref[...] += upd

    return _call(
        body, name="dwconv_bwd", grid=(c_total // tc, n_i),
        in_specs=[pl.BlockSpec((tm, tc), lambda c, i: (i, c)),
                  pl.BlockSpec((8, tc), lambda c, i: (jnp.maximum(i * hb - 1, 0), c)),
                  pl.BlockSpec((8, tc), lambda c, i: (0, c)),
                  pl.BlockSpec((tm, tc), lambda c, i: (i, c)),
                  pl.BlockSpec((8, tc), lambda c, i: (jnp.minimum((i + 1) * hb, r_total // 8 - 1), c))],
        out_specs=[pl.BlockSpec((tm, tc), lambda c, i: (i, c)), pl.BlockSpec((8, tc), lambda c, i: (0, c))],
        out_shape=[jax.ShapeDtypeStruct(x.shape, F32), jax.ShapeDtypeStruct(w8.shape, F32)],
        compiler_params=_cparams(("parallel", "arbitrary")),
    )(x, x, w8, dy, dy)


def make_dwconv(k_taps):
    @jax.custom_vjp
    def op(x, w8):
        return _conv_fwd_call(x, w8, k_taps)

    def fwd(x, w8):
        return op(x, w8), (x, w8)

    def bwd(res, dy):
        x, w8 = res
        dx, dw = _conv_bwd_call(x, w8, dy, k_taps)
        return dx, dw

    op.defvjp(fwd, bwd)
    return op


def _cumsum_call(x, reverse):
    h, t_total = x.shape
    tb = _tile(t_total, 256, LANE)
    nb = t_total // tb

    def body(x_ref, o_ref, carry_ref):
        i = pl.program_id(0)

        @pl.when(i == 0)
        def _():
            carry_ref[...] = jnp.zeros_like(carry_ref)

        r = lax.broadcasted_iota(jnp.int32, (tb, tb), 0)
        c = lax.broadcasted_iota(jnp.int32, (tb, tb), 1)
        tri = jnp.where((r >= c) if reverse else (r <= c), 1.0, 0.0).astype(F32)
        xv = x_ref[...]
        carry = jnp.max(carry_ref[...], axis=1, keepdims=True)
        o_ref[...] = _raw_dot(xv, tri, 1, 0, True) + carry
        carry_ref[...] = jnp.broadcast_to(carry + jnp.sum(xv, axis=1, keepdims=True), carry_ref.shape)

    imap = (lambda i: (0, nb - 1 - i)) if reverse else (lambda i: (0, i))
    return _call(
        body, name="cumsum_rev" if reverse else "cumsum", grid=(nb,),
        in_specs=[pl.BlockSpec((h, tb), imap)], out_specs=pl.BlockSpec((h, tb), imap),
        out_shape=jax.ShapeDtypeStruct(x.shape, F32), scratch_shapes=[pltpu.VMEM((h, LANE), F32)],
        compiler_params=_cparams(("arbitrary",)),
    )(x)


@jax.custom_vjp
def cumsum_lanes(x):
    return _cumsum_call(x, False)


cumsum_lanes.defvjp(lambda x: (cumsum_lanes(x), None), lambda _, ct: (_cumsum_call(ct, True),))


NEG_BIG = -1e30


def _attn_sub_tiles(nb):
    return max(s for s in (3, 2, 1) if nb % s == 0)


EXP_ZERO = -100.0
SMEM = pl.BlockSpec(memory_space=pltpu.SMEM)


def _max_row_norm_sq(x):
    h_total, t_total, hd = x.shape
    tb = _tile(t_total, 2816, 8)

    def body(x_ref, o_ref):
        @pl.when(pl.program_id(1) == 0)
        def _():
            o_ref[...] = jnp.zeros_like(o_ref)

        xv = x_ref[0]
        top = jnp.max(jnp.sum(xv * xv, axis=1, keepdims=True), axis=0, keepdims=True)
        o_ref[0] = jnp.maximum(o_ref[0], top)

    return _call(
        body, name="max_row_norm", grid=(h_total, t_total // tb),
        in_specs=[pl.BlockSpec((1, tb, hd), lambda h, i: (h, i, 0))],
        out_specs=pl.BlockSpec((1, 8, LANE), lambda h, i: (h, 0, 0)),
        out_shape=jax.ShapeDtypeStruct((h_total, 8, LANE), F32),
        compiler_params=_cparams(("parallel", "arbitrary")),
    )(x)


def _attn_skip_tables(q, k, f_row):
    bound = 2.0 * jnp.sqrt(_max_row_norm_sq(q)[:, 0, :1] * _max_row_norm_sq(k)[:, 0, :1])
    return EXP_ZERO - bound, f_row[:, :, 0, 0], f_row[:, :, 0, -1]


def _attn_fwd_call(q, k, v, f_col, f_row, tables):
    h_total, t_total, hd = q.shape
    blk = f_row.shape[-1]
    nb = t_total // blk
    nsub = _attn_sub_tiles(nb)
    tq = nsub * blk

    def body(thr_ref, first_ref, last_ref, q_ref, k_ref, vt_ref, fc_ref, fr_ref, o_ref, lse_ref):
        h = pl.program_id(0)
        i = pl.program_id(1)
        gap_needed = thr_ref[h, 0]
        f_tile = first_ref[h, i * nsub]
        j_start = lax.while_loop(lambda j: (j < i * nsub) & (f_tile - last_ref[h, j] < gap_needed),
                                 lambda j: j + 1, 0)
        r = lax.broadcasted_iota(jnp.int32, (blk, blk), 0)
        c = lax.broadcasted_iota(jnp.int32, (blk, blk), 1)
        qs = [q_ref[0, s * blk:(s + 1) * blk, :].astype(BF16) for s in range(nsub)]
        fqs = [fr_ref[0, i * nsub + s] for s in range(nsub)]

        def load_kv(j):
            off = pl.multiple_of(j * blk, blk)
            return k_ref[0, pl.ds(off, blk), :], vt_ref[0, j], fc_ref[0, pl.ds(off, blk), :]

        def tile(kv, s, carry, diagonal):
            kj, vtj, fk = kv
            m, l, acc = carry
            st = _raw_dot(kj, qs[s], 1, 1, False) + fqs[s] - fk
            if diagonal:
                st = jnp.where(r <= c, st, NEG_BIG)
            m_new = jnp.maximum(m, jnp.max(st, axis=0, keepdims=True))
            p = jnp.exp(st - m_new)
            alpha = jnp.exp(m - m_new)
            l = alpha * l + jnp.sum(p, axis=0, keepdims=True)
            acc = alpha * acc + _raw_dot(vtj, p, 1, 0, False)
            return m_new, l, acc

        def below_diagonal(j, carry):
            kv = load_kv(j)
            return tuple(tile(kv, s, carry[s], False) for s in range(nsub))

        init = tuple((jnp.full((1, blk), NEG_BIG, F32), jnp.zeros((1, blk), F32), jnp.zeros((hd, blk), F32))
                     for _ in range(nsub))
        carry = list(lax.fori_loop(j_start, i * nsub, below_diagonal, init))
        for d in range(nsub):
            kv = load_kv(i * nsub + d)
            for s in range(d, nsub):
                carry[s] = tile(kv, s, carry[s], s == d)
        for s, (m, l, acc) in enumerate(carry):
            o_ref[0, :, s * blk:(s + 1) * blk] = acc / l
            lse_ref[0, s] = m + jnp.log(l)

    vt = v.reshape(h_total, nb, blk, hd).transpose(0, 1, 3, 2).astype(BF16)
    return _call(
        body, name="fox_fwd", grid=(h_total, nb // nsub),
        in_specs=[SMEM, SMEM, SMEM,
                  pl.BlockSpec((1, tq, hd), lambda h, i: (h, i, 0)),
                  pl.BlockSpec((1, t_total, hd), lambda h, i: (h, 0, 0)),
                  pl.BlockSpec((1, nb, hd, blk), lambda h, i: (h, 0, 0, 0)),
                  pl.BlockSpec((1, t_total, 1), lambda h, i: (h, 0, 0)),
                  pl.BlockSpec((1, nb, 1, blk), lambda h, i: (h, 0, 0, 0))],
        out_specs=[pl.BlockSpec((1, hd, tq), lambda h, i: (h, 0, i)),
                   pl.BlockSpec((1, nsub, 1, blk), lambda h, i: (h, i, 0, 0))],
        out_shape=[jax.ShapeDtypeStruct((h_total, hd, t_total), F32), jax.ShapeDtypeStruct(f_row.shape, F32)],
        compiler_params=_cparams(("parallel", "parallel")),
    )(*tables, q, k.astype(BF16), vt, f_col, f_row)


def _attn_bwd_call(q, k, v, f_col, f_row, tables, lse_row, delta_row, do_blk):
    h_total, t_total, hd = q.shape
    blk = f_row.shape[-1]
    nb = t_total // blk
    nsub = _attn_sub_tiles(nb)
    tkv = nsub * blk

    def body(thr_ref, first_ref, last_ref, q_ref, do_ref, k_ref, v_ref, fc_ref, fr_ref, lse_ref, dl_ref,
             dq_ref, dk_ref, dv_ref, dfk_ref, dfq_ref):
        h = pl.program_id(0)
        j = pl.program_id(1)
        gap_needed = thr_ref[h, 0]
        f_tile = last_ref[h, j * nsub + nsub - 1]
        i_stop = lax.while_loop(lambda i: (i < nb) & (first_ref[h, jnp.minimum(i, nb - 1)] - f_tile >= gap_needed),
                                lambda i: i + 1, (j + 1) * nsub)

        @pl.when(j == 0)
        def _():
            dq_ref[...] = jnp.zeros_like(dq_ref)
            dfq_ref[...] = jnp.zeros_like(dfq_ref)

        ks = [k_ref[0, s * blk:(s + 1) * blk, :].astype(BF16) for s in range(nsub)]
        vs = [v_ref[0, s * blk:(s + 1) * blk, :].astype(BF16) for s in range(nsub)]
        fks = [fc_ref[0, s * blk:(s + 1) * blk, :] for s in range(nsub)]
        r = lax.broadcasted_iota(jnp.int32, (blk, blk), 0)
        c = lax.broadcasted_iota(jnp.int32, (blk, blk), 1)

        def q_step(i, accs, subs):
            off = pl.multiple_of(i * blk, blk)
            qi = q_ref[0, pl.ds(off, blk), :]
            doi = do_ref[0, i]
            fq, lse, dl = fr_ref[0, i], lse_ref[0, i], dl_ref[0, i]
            accs = list(accs)
            dq_i, dfq_i = None, None
            for s, diagonal in subs:
                dk, dv, dfk = accs[s]
                st = _raw_dot(ks[s], qi, 1, 1, False) + fq - fks[s] - lse
                if diagonal:
                    st = jnp.where(r <= c, st, NEG_BIG)
                pt = jnp.exp(st)
                dv = dv + _raw_dot(pt, doi, 1, 1, False)
                dst = pt * (_raw_dot(vs[s], doi, 1, 0, False) - dl)
                dk = dk + _raw_dot(dst, qi, 1, 0, False)
                dfk = dfk - jnp.sum(dst, axis=1, keepdims=True)
                accs[s] = (dk, dv, dfk)
                dq_s = _raw_dot(dst, ks[s], 0, 0, False)
                dfq_s = jnp.sum(dst, axis=0, keepdims=True)
                dq_i = dq_s if dq_i is None else dq_i + dq_s
                dfq_i = dfq_s if dfq_i is None else dfq_i + dfq_s
            dfq_ref[0, i] += dfq_i
            dq_ref[0, pl.ds(off, blk), :] += dq_i
            return tuple(accs)

        accs = tuple((jnp.zeros((blk, hd), F32), jnp.zeros((blk, hd), F32), jnp.zeros((blk, 1), F32))
                     for _ in range(nsub))
        for d in range(nsub):
            accs = q_step(j * nsub + d, accs, [(s, s == d) for s in range(d + 1)])
        accs = lax.fori_loop((j + 1) * nsub, i_stop,
                             lambda i, a: q_step(i, a, [(s, False) for s in range(nsub)]), accs)
        for s, (dk, dv, dfk) in enumerate(accs):
            dk_ref[0, s * blk:(s + 1) * blk, :] = dk
            dv_ref[0, s * blk:(s + 1) * blk, :] = dv
            dfk_ref[0, s * blk:(s + 1) * blk, :] = dfk

    full = pl.BlockSpec((1, t_total, hd), lambda h, j: (h, 0, 0))
    tile = pl.BlockSpec((1, tkv, hd), lambda h, j: (h, j, 0))
    col = pl.BlockSpec((1, tkv, 1), lambda h, j: (h, j, 0))
    rows = pl.BlockSpec((1, nb, 1, blk), lambda h, j: (h, 0, 0, 0))
    do_blocks = pl.BlockSpec((1, nb, hd, blk), lambda h, j: (h, 0, 0, 0))
    return _call(
        body, name="fox_bwd", grid=(h_total, nb // nsub),
        in_specs=[SMEM, SMEM, SMEM, full, do_blocks, tile, tile, col, rows, rows, rows],
        out_specs=[full, tile, tile, col, rows],
        out_shape=[jax.ShapeDtypeStruct(q.shape, F32), jax.ShapeDtypeStruct(q.shape, F32),
                   jax.ShapeDtypeStruct(q.shape, F32), jax.ShapeDtypeStruct(f_col.shape, F32),
                   jax.ShapeDtypeStruct(f_row.shape, F32)],
        compiler_params=_cparams(("parallel", "arbitrary")),
    )(*tables, q.astype(BF16), do_blk, k, v, f_col, f_row, lse_row, delta_row)


def _attn_delta_call(do_t, o_t):
    h_total, hd, t_total = o_t.shape
    tb = _tile(t_total, 2816, LANE)

    def body(do_ref, o_ref, d_ref):
        d_ref[0] = jnp.sum(do_ref[0] * o_ref[0], axis=0, keepdims=True)

    spec = pl.BlockSpec((1, hd, tb), lambda h, i: (h, 0, i))
    return _call(
        body, name="fox_delta", grid=(h_total, t_total // tb), in_specs=[spec, spec],
        out_specs=pl.BlockSpec((1, 1, tb), lambda h, i: (h, 0, i)),
        out_shape=jax.ShapeDtypeStruct((h_total, 1, t_total), F32),
        compiler_params=_cparams(("parallel", "parallel")),
    )(do_t, o_t)


@jax.custom_vjp
def fox_attention(q, k, v, f_col, f_row):
    return _attn_fwd_call(q, k, v, f_col, f_row, _attn_skip_tables(q, k, f_row))[0]


def _fox_fwd(q, k, v, f_col, f_row):
    tables = _attn_skip_tables(q, k, f_row)
    o_t, lse_row = _attn_fwd_call(q, k, v, f_col, f_row, tables)
    return o_t, (q, k, v, f_col, f_row, tables, o_t, lse_row)


def _fox_bwd(res, do_t):
    q, k, v, f_col, f_row, tables, o_t, lse_row = res
    h_total, t_total, hd = q.shape
    nb, blk = f_row.shape[1], f_row.shape[3]
    delta = _attn_delta_call(do_t, o_t).reshape(f_row.shape)
    do_blk = do_t.reshape(h_total, hd, nb, blk).transpose(0, 2, 1, 3).astype(BF16)
    return tuple(_attn_bwd_call(q, k, v, f_col, f_row, tables, lse_row, delta, do_blk))


fox_attention.defvjp(_fox_fwd, _fox_bwd)


def _head_col(blk, h):
    lane = lax.broadcasted_iota(jnp.int32, blk.shape, 1)
    return jnp.sum(jnp.where(lane == h, blk, 0.0), axis=1, keepdims=True)


@jax.custom_vjp
def _cat2(a, b):
    return jnp.concatenate([a, b], axis=1)


_cat2.defvjp(lambda a, b: (_cat2(a, b), a.shape[1]), lambda na, ct: (ct[:, :na], ct[:, na:]))


@jax.custom_vjp
def _split2(x):
    half = x.shape[1] // 2
    return x[:, :half], x[:, half:]


_split2.defvjp(lambda x: (_split2(x), None), lambda _, cts: (jnp.concatenate(cts, axis=1),))


def _neumann_solve(m, b):
    x = b - _raw_dot(m, b, 1, 0, False)
    powers, steps = [m], 1
    while 2 * steps < GDN_CHUNK:
        powers.append(_raw_dot(powers[-1], powers[-1], 1, 0, False))
        x = x + _raw_dot(powers[-1], x, 1, 0, False)
        steps *= 2
    return x, powers


@jax.custom_vjp
def _unit_lower_solve(m, b):
    return _neumann_solve(m, b)[0]


def _unit_lower_solve_fwd(m, b):
    x, powers = _neumann_solve(m, b)
    return x, (powers, x)


def _unit_lower_solve_bwd(res, dx):
    powers, x = res
    db = dx - _raw_dot(powers[0], dx, 0, 0, False)
    for p in powers[1:]:
        db = db + _raw_dot(p, db, 0, 0, False)
    return -_raw_dot(db, x, 1, 1, False), db


_unit_lower_solve.defvjp(_unit_lower_solve_fwd, _unit_lower_solve_bwd)


def _gdn_intra(h, q, k, v, b_blk, g_blk):
    n = q.shape[0]
    b, g = _head_col(b_blk, h), _head_col(g_blk, h)
    r = lax.broadcasted_iota(jnp.int32, (n, n), 0)
    c = lax.broadcasted_iota(jnp.int32, (n, n), 1)
    same = (r // GDN_CHUNK) == (c // GDN_CHUNK)
    incl = same & (r >= c)
    g_row = jnp.sum(jnp.where(r == c, g, 0.0), axis=0, keepdims=True)
    big_g = jnp.sum(jnp.where(incl, g_row, 0.0), axis=1, keepdims=True)
    big_g_row = jnp.sum(jnp.where(same & (r <= c), g, 0.0), axis=0, keepdims=True)
    g_tot = jnp.sum(jnp.where(same, g_row, 0.0), axis=1, keepdims=True)
    dec = jnp.where(incl, jnp.exp(jnp.where(incl, big_g - big_g_row, 0.0)), 0.0)
    dec_strict = jnp.where(r > c, dec, 0.0)
    e_g = jnp.exp(big_g)
    kb = k * b
    m = _dot(kb, k, 1, 1) * dec_strict
    rs = lax.broadcasted_iota(jnp.int32, (n, GDN_CHUNK), 0)
    cs = lax.broadcasted_iota(jnp.int32, (n, GDN_CHUNK), 1)
    fold = jnp.where(rs % GDN_CHUNK == cs, 1.0, 0.0).astype(F32)
    aqk = _dot(_dot(q, k, 1, 1) * dec, fold, 1, 0, True)
    u, w = _split2(_unit_lower_solve(m, _cat2(v * b, kb * e_g)))
    lane = lax.broadcasted_iota(jnp.int32, b_blk.shape, 1)
    return u, w, q * e_g, k * jnp.exp(g_tot - big_g), aqk, jnp.where(lane == h, g_tot, 0.0)


def _gdn_rec(h, s, u, w, qg, kd, aqk, gl_blk):
    g_last = jnp.max(_head_col(gl_blk, h), axis=0, keepdims=True)
    big_u = u - _dot(w, s)
    o = _dot(qg, s) + _dot(aqk, big_u)
    s_next = s * jnp.exp(g_last) + _dot(kd, big_u, 0, 0)
    return o, s_next


GDN_TOK_BLK = 256


def _gdn_layout(t_total, rev):
    tb = _tile(t_total, GDN_TOK_BLK, GDN_CHUNK)
    cb, nblk = tb // GDN_CHUNK, t_total // tb
    pos = (lambda i: nblk - 1 - i) if rev else (lambda i: i)
    specs = dict(
        tok=pl.BlockSpec((tb, GDN_W), lambda i: (pos(i), 0)),
        q=pl.BlockSpec((tb, GDN_W), lambda i: (pos(i), 0)),
        k=pl.BlockSpec((tb, GDN_W), lambda i: (pos(i), 1)),
        v=pl.BlockSpec((tb, GDN_W), lambda i: (pos(i), 2)),
        qkv=pl.BlockSpec((tb, 3 * GDN_W), lambda i: (pos(i), 0)),
        gate=pl.BlockSpec((tb, GDN_HEADS), lambda i: (pos(i), 0)),
        aqk=pl.BlockSpec((GDN_HEADS, tb, GDN_CHUNK), lambda i: (0, pos(i), 0)),
        state=pl.BlockSpec((GDN_HEADS, cb, GDN_HD, GDN_HD), lambda i: (0, pos(i), 0, 0)))
    return cb, nblk, specs


def _gdn_shapes(t_total):
    n_chunks = t_total // GDN_CHUNK
    return dict(tok=jax.ShapeDtypeStruct((t_total, GDN_W), F32),
                gate=jax.ShapeDtypeStruct((t_total, GDN_HEADS), F32),
                aqk=jax.ShapeDtypeStruct((GDN_HEADS, t_total, GDN_CHUNK), F32),
                state=jax.ShapeDtypeStruct((GDN_HEADS, n_chunks, GDN_HD, GDN_HD), F32))


def _chunk_rows(ci):
    return pl.ds(pl.multiple_of(ci * GDN_CHUNK, GDN_CHUNK), GDN_CHUNK)


def _head_cols(h):
    return pl.ds(h * GDN_HD, GDN_HD)


def _gdn_intra_fwd_call(qkv, b, g):
    cb, nblk, sp = _gdn_layout(qkv.shape[0], False)
    sh = _gdn_shapes(qkv.shape[0])

    def body(q_ref, k_ref, v_ref, b_ref, g_ref, u_ref, w_ref, qg_ref, kd_ref, aqk_ref, gl_ref):
        b_blk, g_blk = b_ref[...], g_ref[...]
        gl = jnp.zeros(b_blk.shape, F32)
        for h in range(GDN_HEADS):
            cols = _head_cols(h)
            u, w, qg, kd, aqk, gl_h = _gdn_intra(h, q_ref[:, cols], k_ref[:, cols], v_ref[:, cols], b_blk, g_blk)
            u_ref[:, cols] = u
            w_ref[:, cols] = w
            qg_ref[:, cols] = qg
            kd_ref[:, cols] = kd
            aqk_ref[h] = aqk
            gl = gl + gl_h
        gl_ref[...] = gl

    return _call(
        body, name="gdn_intra_fwd", grid=(nblk,),
        in_specs=[sp["q"], sp["k"], sp["v"]] + [sp["gate"]] * 2,
        out_specs=[sp["tok"]] * 4 + [sp["aqk"], sp["gate"]],
        out_shape=[sh["tok"]] * 4 + [sh["aqk"], sh["gate"]],
        compiler_params=_cparams(("parallel",)),
    )(qkv, qkv, qkv, b, g)


def _gdn_intra_bwd_call(qkv, b, g, du, dw, dqg, dkd, daqk, dgl):
    cb, nblk, sp = _gdn_layout(qkv.shape[0], False)
    sh = _gdn_shapes(qkv.shape[0])

    def body(q_ref, k_ref, v_ref, b_ref, g_ref, du_ref, dw_ref, dqg_ref, dkd_ref, daqk_ref, dgl_ref,
             dqkv_ref, db_ref, dg_ref):
        b_blk, g_blk, dgl = b_ref[...], g_ref[...], dgl_ref[...]
        db = jnp.zeros(b_blk.shape, F32)
        dg = jnp.zeros(b_blk.shape, F32)
        for h in range(GDN_HEADS):
            cols = _head_cols(h)
            _, vjp = jax.vjp(functools.partial(_gdn_intra, h), q_ref[:, cols], k_ref[:, cols], v_ref[:, cols],
                             b_blk, g_blk)
            dq, dk, dv, db_h, dg_h = vjp((du_ref[:, cols], dw_ref[:, cols], dqg_ref[:, cols], dkd_ref[:, cols],
                                          daqk_ref[h], dgl))
            dqkv_ref[:, pl.ds(h * GDN_HD, GDN_HD)] = dq
            dqkv_ref[:, pl.ds(GDN_W + h * GDN_HD, GDN_HD)] = dk
            dqkv_ref[:, pl.ds(2 * GDN_W + h * GDN_HD, GDN_HD)] = dv
            db = db + db_h
            dg = dg + dg_h
        db_ref[...] = db
        dg_ref[...] = dg

    return _call(
        body, name="gdn_intra_bwd", grid=(nblk,),
        in_specs=[sp["q"], sp["k"], sp["v"]] + [sp["gate"]] * 2 + [sp["tok"]] * 4 + [sp["aqk"], sp["gate"]],
        out_specs=[sp["qkv"]] + [sp["gate"]] * 2,
        out_shape=[jax.ShapeDtypeStruct(qkv.shape, F32)] + [sh["gate"]] * 2,
        compiler_params=_cparams(("parallel",)),
    )(qkv, qkv, qkv, b, g, du, dw, dqg, dkd, daqk, dgl)


def _gdn_rec_fwd_call(u, w, qg, kd, aqk, gl):
    cb, nblk, sp = _gdn_layout(u.shape[0], False)
    sh = _gdn_shapes(u.shape[0])

    def body(u_ref, w_ref, qg_ref, kd_ref, aqk_ref, gl_ref, o_ref, s_all_ref, s_ref):
        @pl.when(pl.program_id(0) == 0)
        def _():
            s_ref[...] = jnp.zeros_like(s_ref)

        def chunk(ci, carry):
            rows = _chunk_rows(ci)
            gl_row = gl_ref[rows, :]
            states = [s_ref[h] for h in range(GDN_HEADS)]
            res = [_gdn_rec(h, states[h], u_ref[rows, _head_cols(h)], w_ref[rows, _head_cols(h)],
                            qg_ref[rows, _head_cols(h)], kd_ref[rows, _head_cols(h)], aqk_ref[h, rows, :], gl_row)
                   for h in range(GDN_HEADS)]
            for h, (o, s_next) in enumerate(res):
                s_all_ref[h, ci] = states[h]
                o_ref[rows, _head_cols(h)] = o
                s_ref[h] = s_next
            return carry

        lax.fori_loop(0, cb, chunk, 0)

    return _call(
        body, name="gdn_rec_fwd", grid=(nblk,),
        in_specs=[sp["tok"]] * 4 + [sp["aqk"], sp["gate"]],
        out_specs=[sp["tok"], sp["state"]], out_shape=[sh["tok"], sh["state"]],
        scratch_shapes=[pltpu.VMEM((GDN_HEADS, GDN_HD, GDN_HD), F32)],
        compiler_params=_cparams(("arbitrary",)),
    )(u, w, qg, kd, aqk, gl)


def _gdn_rec_bwd_call(u, w, qg, kd, aqk, gl, s_all, do):
    cb, nblk, sp = _gdn_layout(u.shape[0], True)
    sh = _gdn_shapes(u.shape[0])

    def body(u_ref, w_ref, qg_ref, kd_ref, aqk_ref, gl_ref, s_all_ref, do_ref,
             du_ref, dw_ref, dqg_ref, dkd_ref, daqk_ref, dgl_ref, ds_ref):
        @pl.when(pl.program_id(0) == 0)
        def _():
            ds_ref[...] = jnp.zeros_like(ds_ref)

        def chunk(step, carry):
            ci = cb - 1 - step
            rows = _chunk_rows(ci)
            gl_row = gl_ref[rows, :]
            res = []
            for h in range(GDN_HEADS):
                cols = _head_cols(h)
                _, vjp = jax.vjp(functools.partial(_gdn_rec, h), s_all_ref[h, ci], u_ref[rows, cols],
                                 w_ref[rows, cols], qg_ref[rows, cols], kd_ref[rows, cols], aqk_ref[h, rows, :],
                                 gl_row)
                res.append(vjp((do_ref[rows, cols], ds_ref[h])))
            dgl = jnp.zeros((GDN_CHUNK, GDN_HEADS), F32)
            for h, (ds, du, dw, dqg, dkd, daqk, dgl_h) in enumerate(res):
                cols = _head_cols(h)
                ds_ref[h] = ds
                du_ref[rows, cols] = du
                dw_ref[rows, cols] = dw
                dqg_ref[rows, cols] = dqg
                dkd_ref[rows, cols] = dkd
                daqk_ref[h, rows, :] = daqk
                dgl = dgl + dgl_h
            dgl_ref[rows, :] = dgl
            return carry

        lax.fori_loop(0, cb, chunk, 0)

    return _call(
        body, name="gdn_rec_bwd", grid=(nblk,),
        in_specs=[sp["tok"]] * 4 + [sp["aqk"], sp["gate"], sp["state"], sp["tok"]],
        out_specs=[sp["tok"]] * 4 + [sp["aqk"], sp["gate"]],
        out_shape=[sh["tok"]] * 4 + [sh["aqk"], sh["gate"]],
        scratch_shapes=[pltpu.VMEM((GDN_HEADS, GDN_HD, GDN_HD), F32)],
        compiler_params=_cparams(("arbitrary",)),
    )(u, w, qg, kd, aqk, gl, s_all, do)


@jax.custom_vjp
def gdn_intra(qkv, b, g):
    return tuple(_gdn_intra_fwd_call(qkv, b, g))


gdn_intra.defvjp(lambda *a: (gdn_intra(*a), a), lambda res, cts: tuple(_gdn_intra_bwd_call(*res, *cts)))


@jax.custom_vjp
def gdn_rec(u, w, qg, kd, aqk, gl):
    return _gdn_rec_fwd_call(u, w, qg, kd, aqk, gl)[0]


def _gdn_rec_fwd(*a):
    o, s_all = _gdn_rec_fwd_call(*a)
    return o, a + (s_all,)


gdn_rec.defvjp(_gdn_rec_fwd, lambda res, do: tuple(_gdn_rec_bwd_call(*res, do)))


def gated_delta(qkv, b, g):
    return gdn_rec(*gdn_intra(qkv, b, g))


def _loss_call(y, tgt, first, last):
    r_total, d = y.shape
    tm = _tile(r_total, 256, 8)

    def body(y_ref, t_ref, loss_ref, dy_ref):
        i = pl.program_id(0)

        @pl.when(i == 0)
        def _():
            loss_ref[...] = jnp.zeros_like(loss_ref)

        row = lax.broadcasted_iota(jnp.int32, (tm, d), 0) + i * tm
        err = jnp.where((row >= first) & (row < last), y_ref[...] - t_ref[...], 0.0)
        dy_ref[...] = err * (1.0 / d)
        part = jnp.sum(jnp.sum(err * err, axis=1, keepdims=True), axis=0, keepdims=True) * (0.5 / d)
        loss_ref[...] += jnp.broadcast_to(part, loss_ref.shape)

    return _call(
        body, name="loss_head", grid=(r_total // tm,),
        in_specs=[pl.BlockSpec((tm, d), lambda i: (i, 0))] * 2,
        out_specs=[pl.BlockSpec((8, LANE), lambda i: (0, 0)), pl.BlockSpec((tm, d), lambda i: (i, 0))],
        out_shape=[jax.ShapeDtypeStruct((8, LANE), F32), jax.ShapeDtypeStruct(y.shape, F32)],
        compiler_params=_cparams(("arbitrary",)),
    )(y, tgt)


def make_loss(first, last):
    @jax.custom_vjp
    def op(y, tgt):
        return _loss_call(y, tgt, first, last)[0][0, 0]

    def fwd(y, tgt):
        loss, dy = _loss_call(y, tgt, first, last)
        return loss[0, 0], (dy,)

    def bwd(res, ct):
        return res[0] * ct, jnp.zeros_like(res[0])

    op.defvjp(fwd, bwd)
    return op


def _pad_rows8(w):
    return jnp.concatenate([w, jnp.zeros((8 - w.shape[0], w.shape[1]), w.dtype)], axis=0)


def local_loss(wts, x, tgt):
    seq = x.shape[0]
    n_tok = N_META + seq
    t_pad = -(-n_tok // ROW_ALIGN) * ROW_ALIGN
    depth = wts["norm1_g"].shape[0]
    blk = _tile(t_pad, ATT_BLK, LANE)
    nb = t_pad // blk
    tm = _tile(t_pad, 256, 8)

    rms = rowop(_f_rmsnorm, "rmsnorm", (D_MODEL,), tm)
    qnorm = rowop(_f_qnorm, "fox_q_norm", (FOX_HD,), _tile(FOX_HEADS * t_pad, 2048, 8))
    knorm = rowop(_f_rmsnorm, "fox_k_norm", (FOX_HD,), _tile(FOX_HEADS * t_pad, 2048, 8))
    logsig = rowop(_f_logsig, "fox_log_forget", (FOX_HEADS,), tm)
    gdn_act = rowop([_f_gdn_q] * GDN_HEADS + [_f_gdn_k] * GDN_HEADS + [_f_gdn_v] * GDN_HEADS, "gdn_qkv_act",
                    (GDN_HD,), tm)
    gates = rowop(_f_gdn_gates, "gdn_gates", (GDN_HEADS, GDN_HEADS), tm)
    gdn_out = rowop([_f_gdn_out] * GDN_HEADS, "gdn_out_norm", (GDN_HD,), tm)
    merge = rowop(_f_merge, "branch_merge", (D_MODEL,), tm)
    residual = rowop(_f_residual, "residual_add", (D_MODEL,), tm, bc=(2,))
    residual_norm = rowop(_f_residual_norm, "residual_add_norm", (D_MODEL, D_MODEL), tm, bc=(2,))
    keep = (jnp.arange(t_pad)[:, None] < n_tok).astype(F32)
    conv4 = make_dwconv(GDN_CONV)
    conv3 = make_dwconv(FFN_CONV)
    loss_op = make_loss(N_META, n_tok)

    zeros = jnp.zeros((t_pad - n_tok, D_MODEL), F32)
    h_res = jnp.concatenate([wts["meta_tokens"], x, zeros], axis=0)
    tgt_rows = jnp.concatenate([jnp.zeros((N_META, D_MODEL), F32), tgt, zeros], axis=0)

    def heads(a):
        return a.reshape(t_pad, FOX_HEADS, FOX_HD).transpose(1, 0, 2).reshape(FOX_HEADS * t_pad, FOX_HD)

    h = rms((h_res,), (wts["norm1_g"][0][None],))[0]
    for l in range(depth):
        proj = mm(h, wts["w_in"][l])
        qn = qnorm((heads(proj[:, 0:512]),), (wts["fox_q_norm_g"][l][None],))[0]
        kn = knorm((heads(proj[:, 512:1024]),), (wts["fox_k_norm_g"][l][None],))[0]
        vh = heads(proj[:, 1024:1536])
        log_f = logsig((proj[:, 1536:1544],), (wts["fox_f_bias"][l][None],))[0]
        f_cum = cumsum_lanes(log_f.T)
        o_a = fox_attention(qn.reshape(FOX_HEADS, t_pad, FOX_HD), kn.reshape(FOX_HEADS, t_pad, FOX_HD),
                            vh.reshape(FOX_HEADS, t_pad, FOX_HD), f_cum[:, :, None],
                            f_cum.reshape(FOX_HEADS, nb, 1, blk))
        y_a = mm(o_a.transpose(2, 0, 1).reshape(t_pad, FOX_W), wts["w_branch_a"][l])
        cv = conv4(proj[:, 1664:4736], _pad_rows8(wts["gdn_conv_w"][l]))
        qkv = gdn_act((cv,), ())[0]
        beta, gdec = gates((proj[:, 4736:4744], proj[:, 4744:4752]),
                           (wts["gdn_a_log"][l][None], wts["gdn_dt_bias"][l][None]))
        o_b = gated_delta(qkv, beta, gdec)
        o_b = gdn_out((o_b, proj[:, 4864:5888]), (wts["gdn_norm_g"][l][None],))[0]
        y_b = mm(o_b, wts["w_branch_b"][l])
        mixed = merge((proj[:, 5888:6912], proj[:, 6912:7936], y_a, y_b), ())[0]
        h_res, h = residual_norm((h_res, mm(mixed, wts["w_out"][l]), keep), (wts["norm2_g"][l][None],))
        up = conv3(mm(h, wts["w_up"][l]), _pad_rows8(wts["ffn_conv_w"][l]))
        act = glu(up)
        down = mm(act, wts["w_down"][l])
        if l + 1 < depth:
            h_res, h = residual_norm((h_res, down, keep), (wts["norm1_g"][l + 1][None],))
        else:
            h_res = residual((h_res, down, keep), ())[0]
    return loss_op(h_res, tgt_rows)


def pad_w_in(w):
    parts, pos = [], 0
    for src, width, dst in IN_SEGS:
        if dst > pos:
            parts.append(jnp.zeros(w.shape[:-1] + (dst - pos,), w.dtype))
        parts.append(w[..., src:src + width])
        pos = dst + width
    parts.append(jnp.zeros(w.shape[:-1] + (D_IN_PAD - pos,), w.dtype))
    return jnp.concatenate(parts, axis=-1)


def unpad_w_in(w):
    return jnp.concatenate([w[..., dst:dst + width] for _, width, dst in IN_SEGS], axis=-1)


ANY = pl.BlockSpec(memory_space=pl.ANY)
N_CHIPS = 4
N_DEV = 8
COMM_COLS = 1024
COMM_ROW_ALIGN = 512
COMM_ROW_ALIGN_SMALL = 32


def _place():
    return lax.axis_index("x"), lax.axis_index("y"), lax.axis_index("c")


def _other_chips(x, y):
    return [(1 - x, y), (x, 1 - y), (1 - x, 1 - y)]


def _remote(src, dst, send_sem, recv_sem, dev):
    return pltpu.make_async_remote_copy(src_ref=src, dst_ref=dst, send_sem=send_sem, recv_sem=recv_sem,
                                        device_id=dev, device_id_type=MESH)


def chip_all_gather(buf):
    rows, cols = buf.shape
    half = rows // 2

    def body(x_ref, out_ref, send_sems, recv_sems, pass_send, pass_recv):
        x, y, c = _place()
        me = 2 * x + y
        mine, other = pl.ds(c * half, half), pl.ds((1 - c) * half, half)
        sibling = (x, y, 1 - c)
        chips = _other_chips(x, y)
        started = []
        for k, (px, py) in enumerate(chips):
            cp = _remote(x_ref.at[mine], out_ref.at[me, mine], send_sems.at[k], recv_sems.at[k], (px, py, c))
            cp.start()
            started.append(cp)
        for k, (px, py) in enumerate(chips):
            landed = out_ref.at[2 * px + py, mine]
            _remote(landed, landed, send_sems.at[k], recv_sems.at[k], (px, py, c)).wait_recv()
            cp = _remote(landed, landed, pass_send.at[k], pass_recv.at[k], sibling)
            cp.start()
            started.append(cp)
        for k, (px, py) in enumerate(chips):
            passed = out_ref.at[2 * px + py, other]
            _remote(passed, passed, pass_send.at[k], pass_recv.at[k], sibling).wait_recv()
        for cp in started:
            cp.wait_send()

    got = _call(
        body, name="chip_all_gather", in_specs=[ANY], out_specs=ANY,
        out_shape=jax.ShapeDtypeStruct((N_CHIPS, rows, cols), buf.dtype),
        scratch_shapes=[pltpu.SemaphoreType.DMA((3,)), pltpu.SemaphoreType.DMA((3,)),
                        pltpu.SemaphoreType.DMA((3,)), pltpu.SemaphoreType.DMA((3,))],
    )(buf)
    me = 2 * lax.axis_index("x") + lax.axis_index("y")
    return lax.dynamic_update_slice(got, buf[None], (me, 0, 0))


def sibling_swap_halves(g4):
    n, rows, cols = g4.shape
    half = rows // 2

    def body(g_ref, got_ref, send_sem, recv_sem):
        x, y, c = _place()
        cp = _remote(g_ref.at[:, pl.ds((1 - c) * half, half), :], got_ref, send_sem, recv_sem, (x, y, 1 - c))
        cp.start()
        cp.wait()

    return _call(
        body, name="sibling_swap_halves", in_specs=[ANY], out_specs=ANY,
        out_shape=jax.ShapeDtypeStruct((n, half, cols), g4.dtype),
        scratch_shapes=[pltpu.SemaphoreType.DMA, pltpu.SemaphoreType.DMA],
    )(g4)


def add_own_half(g4, got, c):
    n, rows, cols = g4.shape
    half = rows // 2
    tm = _tile(half, 256, 16)
    nt = half // tm

    def body(c_ref, a_ref, b_ref, o_ref):
        o_ref[...] = (a_ref[...] + b_ref[...]).astype(o_ref.dtype)

    return _call(
        body, name="add_own_half",
        grid_spec=pltpu.PrefetchScalarGridSpec(
            num_scalar_prefetch=1, grid=(n, nt),
            in_specs=[pl.BlockSpec((1, tm, cols), lambda j, i, c_ref: (j, c_ref[0] * nt + i, 0)),
                      pl.BlockSpec((1, tm, cols), lambda j, i, c_ref: (j, i, 0))],
            out_specs=pl.BlockSpec((1, tm, cols), lambda j, i, c_ref: (j, i, 0))),
        out_shape=jax.ShapeDtypeStruct(got.shape, BF16),
        compiler_params=_cparams(("parallel", "parallel")),
    )(c.reshape(1).astype(jnp.int32), g4, got)


def chip_scatter(p4):
    n, rows, cols = p4.shape

    def body(p_ref, out_ref, send_sems, recv_sems):
        x, y, c = _place()
        me = 2 * x + y
        chips = _other_chips(x, y)
        started = []
        for k, (px, py) in enumerate(chips):
            cp = _remote(p_ref.at[2 * px + py], out_ref.at[me], send_sems.at[k], recv_sems.at[k], (px, py, c))
            cp.start()
            started.append(cp)
        for k, (px, py) in enumerate(chips):
            landed = out_ref.at[2 * px + py]
            _remote(landed, landed, send_sems.at[k], recv_sems.at[k], (px, py, c)).wait_recv()
        for cp in started:
            cp.wait_send()

    got = _call(
        body, name="chip_scatter", in_specs=[ANY], out_specs=ANY,
        out_shape=jax.ShapeDtypeStruct(p4.shape, p4.dtype),
        scratch_shapes=[pltpu.SemaphoreType.DMA((3,)), pltpu.SemaphoreType.DMA((3,))],
    )(p4)
    me = 2 * lax.axis_index("x") + lax.axis_index("y")
    return lax.dynamic_update_slice(got, lax.dynamic_slice_in_dim(p4, me, 1, axis=0), (me, 0, 0))


def sum_slots(a):
    n, rows, cols = a.shape
    tm = _tile(rows, 256, 16) if rows % 16 == 0 else rows

    def body(a_ref, o_ref):
        acc = a_ref[0].astype(F32)
        for k in range(1, n):
            acc = acc + a_ref[k].astype(F32)
        o_ref[...] = acc

    return _call(
        body, name="sum_slots_%d" % n, grid=(rows // tm,),
        in_specs=[pl.BlockSpec((n, tm, cols), lambda i: (0, i, 0))],
        out_specs=pl.BlockSpec((tm, cols), lambda i: (i, 0)),
        out_shape=jax.ShapeDtypeStruct((rows, cols), F32),
        compiler_params=_cparams(("parallel",)),
    )(a)


def sibling_join(s):
    half, cols = s.shape

    def body(s_ref, got_ref, send_sem, recv_sem):
        x, y, c = _place()
        cp = _remote(s_ref, got_ref, send_sem, recv_sem, (x, y, 1 - c))
        cp.start()
        cp.wait()

    got = _call(
        body, name="sibling_join", in_specs=[ANY], out_specs=ANY,
        out_shape=jax.ShapeDtypeStruct(s.shape, s.dtype),
        scratch_shapes=[pltpu.SemaphoreType.DMA, pltpu.SemaphoreType.DMA],
    )(s)
    c = lax.axis_index("c")
    out = jnp.zeros((2 * half, cols), s.dtype)
    out = lax.dynamic_update_slice(out, s, (c * half, 0))
    return lax.dynamic_update_slice(out, got, ((1 - c) * half, 0))


def all_devices_gather(buf):
    rows, cols = buf.shape

    def body(b_ref, out_ref, send_sems, recv_sems, local_sem):
        x, y, c = _place()
        me = 4 * x + 2 * y + c
        local = pltpu.make_async_copy(b_ref, out_ref.at[me], local_sem)
        local.start()
        peers = [((x + dx) % 2, (y + dy) % 2, (c + dc) % 2)
                 for dx in (0, 1) for dy in (0, 1) for dc in (0, 1) if dx + dy + dc > 0]
        started = []
        for k, peer in enumerate(peers):
            cp = _remote(b_ref, out_ref.at[me], send_sems.at[k], recv_sems.at[k], peer)
            cp.start()
            started.append(cp)
        for k, (px, py, pc) in enumerate(peers):
            landed = out_ref.at[4 * px + 2 * py + pc]
            _remote(landed, landed, send_sems.at[k], recv_sems.at[k], (px, py, pc)).wait_recv()
        for cp in started:
            cp.wait_send()
        local.wait()

    return _call(
        body, name="all_devices_gather", in_specs=[ANY], out_specs=ANY,
        out_shape=jax.ShapeDtypeStruct((N_DEV, rows, cols), buf.dtype),
        scratch_shapes=[pltpu.SemaphoreType.DMA((7,)), pltpu.SemaphoreType.DMA((7,)), pltpu.SemaphoreType.DMA],
    )(buf)


def adamw(w, g, m, v):
    shape = w.shape
    w2, g2, m2, v2 = [a.reshape(-1, shape[-1]) for a in (w, g, m, v)]
    rows, cols = w2.shape
    tm = _tile(rows, 256, 8) if rows % 8 == 0 else rows

    def body(w_ref, g_ref, m_ref, v_ref, d_ref, nm_ref, nv_ref):
        gv = g_ref[...]
        nm = ADAM_B1 * m_ref[...] + (1.0 - ADAM_B1) * gv
        nv = ADAM_B2 * v_ref[...] + (1.0 - ADAM_B2) * (gv * gv)
        m_hat = nm / (1.0 - ADAM_B1 ** ADAM_STEP)
        v_hat = nv / (1.0 - ADAM_B2 ** ADAM_STEP)
        d_ref[...] = -ADAM_LR * (m_hat / (jnp.sqrt(v_hat) + ADAM_EPS) + ADAM_WD * w_ref[...])
        nm_ref[...] = nm
        nv_ref[...] = nv

    spec = pl.BlockSpec((tm, cols), lambda i: (i, 0))
    outs = _call(
        body, name="adamw", grid=(rows // tm,), in_specs=[spec] * 4, out_specs=[spec] * 3,
        out_shape=[jax.ShapeDtypeStruct((rows, cols), F32)] * 3,
        compiler_params=_cparams(("parallel",)),
    )(w2, g2, m2, v2)
    return [o.reshape(shape) for o in outs]


WEIGHTS = ("meta_tokens", "norm1_g", "w_in", "fox_f_bias", "fox_q_norm_g", "fox_k_norm_g", "gdn_conv_w",
           "gdn_a_log", "gdn_dt_bias", "gdn_norm_g", "w_branch_a", "w_branch_b", "w_out", "norm2_g", "w_up",
           "ffn_conv_w", "w_down")
SHARD_AXIS = {"meta_tokens": -1, "w_in": -1, "gdn_conv_w": -1, "w_branch_a": -1, "w_branch_b": -2, "w_out": -2,
              "w_up": -1, "ffn_conv_w": -1, "w_down": -2}
MATMUL_WEIGHTS = ("w_in", "w_branch_a", "w_branch_b", "w_out", "w_up", "w_down")
SMALL_SHARDED = ("meta_tokens", "gdn_conv_w", "ffn_conv_w")
REPLICATED = tuple(n for n in WEIGHTS if n not in SHARD_AXIS)


def _pack(arrays, dtype, row_align):
    flat = jnp.concatenate([a.reshape(-1).astype(dtype) for a in arrays])
    block = row_align * COMM_COLS
    total = -(-flat.shape[0] // block) * block
    flat = jnp.concatenate([flat, jnp.zeros((total - flat.shape[0],), dtype)])
    return flat.reshape(-1, COMM_COLS)


def _unpack(buf, shapes):
    flat, out, pos = buf.reshape(-1), [], 0
    for shape in shapes:
        size = 1
        for d in shape:
            size *= d
        out.append(flat[pos:pos + size].reshape(shape))
        pos += size
    return out


def _gather_full(shards, names, dtype, row_align):
    got = chip_all_gather(_pack([shards[n] for n in names], dtype, row_align))
    per_chip = [_unpack(got[j], [shards[n].shape for n in names]) for j in range(N_CHIPS)]
    return {n: jnp.concatenate([per_chip[j][i] for j in range(N_CHIPS)], axis=SHARD_AXIS[n]).astype(F32)
            for i, n in enumerate(names)}


def _shard_of(full, name, j):
    axis = SHARD_AXIS[name] % full.ndim
    size = full.shape[axis] // N_CHIPS
    return lax.slice_in_dim(full, j * size, (j + 1) * size, axis=axis)


def kernel(x, meta_tokens, norm1_g, w_in, fox_f_bias, fox_q_norm_g, fox_k_norm_g, gdn_conv_w, gdn_a_log, gdn_dt_bias, gdn_norm_g, w_branch_a, w_branch_b, w_out, norm2_g, w_up, ffn_conv_w, w_down, loss_target, m_meta_tokens, m_norm1_g, m_w_in, m_fox_f_bias, m_fox_q_norm_g, m_fox_k_norm_g, m_gdn_conv_w, m_gdn_a_log, m_gdn_dt_bias, m_gdn_norm_g, m_w_branch_a, m_w_branch_b, m_w_out, m_norm2_g, m_w_up, m_ffn_conv_w, m_w_down, v_meta_tokens, v_norm1_g, v_w_in, v_fox_f_bias, v_fox_q_norm_g, v_fox_k_norm_g, v_gdn_conv_w, v_gdn_a_log, v_gdn_dt_bias, v_gdn_norm_g, v_w_branch_a, v_w_branch_b, v_w_out, v_norm2_g, v_w_up, v_ffn_conv_w, v_w_down):
    w_loc = dict(zip(WEIGHTS, (meta_tokens, norm1_g, w_in, fox_f_bias, fox_q_norm_g, fox_k_norm_g, gdn_conv_w,
                               gdn_a_log, gdn_dt_bias, gdn_norm_g, w_branch_a, w_branch_b, w_out, norm2_g, w_up,
                               ffn_conv_w, w_down)))
    m_loc = dict(zip(WEIGHTS, (m_meta_tokens, m_norm1_g, m_w_in, m_fox_f_bias, m_fox_q_norm_g, m_fox_k_norm_g,
                               m_gdn_conv_w, m_gdn_a_log, m_gdn_dt_bias, m_gdn_norm_g, m_w_branch_a, m_w_branch_b,
                               m_w_out, m_norm2_g, m_w_up, m_ffn_conv_w, m_w_down)))
    v_loc = dict(zip(WEIGHTS, (v_meta_tokens, v_norm1_g, v_w_in, v_fox_f_bias, v_fox_q_norm_g, v_fox_k_norm_g,
                               v_gdn_conv_w, v_gdn_a_log, v_gdn_dt_bias, v_gdn_norm_g, v_w_branch_a, v_w_branch_b,
                               v_w_out, v_norm2_g, v_w_up, v_ffn_conv_w, v_w_down)))
    c = lax.axis_index("c")

    full = {n: w_loc[n] for n in REPLICATED}
    full.update(_gather_full(w_loc, MATMUL_WEIGHTS, BF16, COMM_ROW_ALIGN))
    full.update(_gather_full(w_loc, SMALL_SHARDED, F32, COMM_ROW_ALIGN_SMALL))
    full["w_in"] = pad_w_in(full["w_in"])

    loss, (g_full, g_x) = jax.value_and_grad(local_loss, argnums=(0, 1))(full, x[0], loss_target[0])
    g_full = dict(g_full)
    g_full["w_in"] = unpad_w_in(g_full["w_in"])

    sharded = MATMUL_WEIGHTS + SMALL_SHARDED
    g4 = jnp.stack([_pack([_shard_of(g_full[n], n, j) for n in sharded], F32, COMM_ROW_ALIGN)
                    for j in range(N_CHIPS)])
    pair_sum = add_own_half(g4, sibling_swap_halves(g4), c)
    g_shard = sibling_join(sum_slots(chip_scatter(pair_sum)))
    grads = dict(zip(sharded, _unpack(g_shard, [w_loc[n].shape for n in sharded])))
    g_rep = sum_slots(all_devices_gather(_pack([g_full[n] for n in REPLICATED], F32, 8)))
    grads.update(zip(REPLICATED, _unpack(g_rep, [w_loc[n].shape for n in REPLICATED])))

    loss = lax.psum(loss, ("x", "y", "c"))
    upd = {n: adamw(w_loc[n], grads[n], m_loc[n], v_loc[n]) for n in WEIGHTS}
    return (loss, g_x[None], *[grads[n] for n in WEIGHTS], *[upd[n][0] for n in WEIGHTS],
            *[upd[n][1] for n in WEIGHTS], *[upd[n][2] for n in WEIGHTS])
```

```python
import functools

import jax
import jax.numpy as jnp
from jax import lax
from jax.experimental import pallas as pl
from jax.experimental.pallas import tpu as pltpu

F32 = jnp.float32
BF16 = jnp.bfloat16
HI = lax.Precision.HIGHEST
MESH = pl.DeviceIdType.MESH

D_MODEL = 1024
N_META = 16
EPS = 1e-6
FOX_HEADS, FOX_HD = 8, 64
FOX_W = FOX_HEADS * FOX_HD
GDN_HEADS, GDN_HD, GDN_CHUNK, GDN_CONV = 8, 128, 64, 4
GDN_W = GDN_HEADS * GDN_HD
D_FF = 2816
FFN_CONV = 3
D_IN = 7704
D_IN_PAD = 8192
IN_SEGS = ((0, 1536, 0), (1536, 8, 1536), (1544, 3072, 1664), (4616, 16, 4736), (4632, 1024, 4864), (5656, 2048, 5888))
ROW_ALIGN = 256
ATT_BLK = 256
VMEM_LIMIT = 48 * 1024 * 1024
LANE = 128

ADAM_LR, ADAM_B1, ADAM_B2, ADAM_EPS, ADAM_WD, ADAM_STEP = 0.001, 0.9, 0.999, 1e-08, 0.01, 10


def _call(body, **kw):
    return pl.pallas_call(body, **kw)


def _tile(n, target, mult):
    best, t = None, mult
    while t <= min(n, target):
        if n % t == 0:
            best = t
        t += mult
    assert best is not None, (n, target, mult)
    return best


def _cparams(sem):
    return pltpu.CompilerParams(dimension_semantics=sem, vmem_limit_bytes=VMEM_LIMIT)


def _raw_dot(a, b, ca, cb, precise):
    dims = (((ca,), (cb,)), ((), ()))
    a_hi, b_hi = a.astype(BF16), b.astype(BF16)
    out = lax.dot_general(a_hi, b_hi, dims, preferred_element_type=F32)
    if precise:
        a_lo = (a - a_hi.astype(F32)).astype(BF16)
        b_lo = (b - b_hi.astype(F32)).astype(BF16)
        out = out + (lax.dot_general(a_hi, b_lo, dims, preferred_element_type=F32)
                     + lax.dot_general(a_lo, b_hi, dims, preferred_element_type=F32))
    return out


def _make_dot(ca, cb, precise):
    @jax.custom_vjp
    def f(a, b):
        return _raw_dot(a, b, ca, cb, precise)

    def fwd(a, b):
        return f(a, b), (a, b)

    def bwd(res, ct):
        a, b = res
        if ca == 1:
            da = _raw_dot(ct, b, 1, 1 if cb == 0 else 0, precise)
        else:
            da = _raw_dot(b, ct, 1 if cb == 0 else 0, 1, precise)
        if cb == 0:
            db = _raw_dot(a, ct, 0 if ca == 1 else 1, 0, precise)
        else:
            db = _raw_dot(ct, a, 0, 0 if ca == 1 else 1, precise)
        return da, db

    f.defvjp(fwd, bwd)
    return f


_DOTS = {(ca, cb, p): _make_dot(ca, cb, p) for ca in (0, 1) for cb in (0, 1) for p in (False, True)}


def _dot(a, b, ca=1, cb=0, precise=False):
    return _DOTS[(ca, cb, precise)](a, b)


def _mm_call(a, b, name, ta=False, tb=False):
    k, m = a.shape if ta else a.shape[::-1]
    n, kb = b.shape if tb else b.shape[::-1]
    assert k == kb, (a.shape, b.shape)
    tm = _tile(m, 768, LANE if ta else 16)
    tn = _tile(n, 1408, LANE)
    tk = _tile(k, 1408, LANE)
    nk = k // tk
    dims = (((0 if ta else 1,), (1 if tb else 0,)), ((), ()))

    def body(a_ref, b_ref, o_ref, *scratch):
        part = lax.dot_general(a_ref[...], b_ref[...], dims, preferred_element_type=F32)
        if nk == 1:
            o_ref[...] = part
            return
        acc_ref = scratch[0]
        kk = pl.program_id(2)

        @pl.when(kk == 0)
        def _():
            acc_ref[...] = part

        @pl.when(kk > 0)
        def _():
            acc_ref[...] += part

        @pl.when(kk == nk - 1)
        def _():
            o_ref[...] = acc_ref[...]

    return _call(
        body, name=name, grid=(m // tm, n // tn, nk),
        in_specs=[pl.BlockSpec((tk, tm), lambda i, j, kk: (kk, i)) if ta else
                  pl.BlockSpec((tm, tk), lambda i, j, kk: (i, kk)),
                  pl.BlockSpec((tn, tk), lambda i, j, kk: (j, kk)) if tb else
                  pl.BlockSpec((tk, tn), lambda i, j, kk: (kk, j))],
        out_specs=pl.BlockSpec((tm, tn), lambda i, j, kk: (i, j)),
        out_shape=jax.ShapeDtypeStruct((m, n), F32),
        scratch_shapes=[pltpu.VMEM((tm, tn), F32)] if nk > 1 else [],
        compiler_params=_cparams(("parallel", "parallel", "arbitrary")),
    )(a, b)


@jax.custom_vjp
def mm(a, w):
    return _mm_call(a.astype(BF16), w.astype(BF16), "mm_fwd")


def _mm_fwd(a, w):
    a_b, w_b = a.astype(BF16), w.astype(BF16)
    return _mm_call(a_b, w_b, "mm_fwd"), (a_b, w_b)


def _mm_bwd(res, ct):
    a_b, w_b = res
    ct_b = ct.astype(BF16)
    return _mm_call(ct_b, w_b, "mm_dx", tb=True), _mm_call(a_b, ct_b, "mm_dw", ta=True)


mm.defvjp(_mm_fwd, _mm_bwd)


def _rows_specs(rows, tm, ncb, bc):
    specs = []
    for idx, r in enumerate(rows):
        if idx in bc:
            specs.append(pl.BlockSpec((tm, r.shape[1]), lambda i, j: (i, 0)))
        else:
            specs.append(pl.BlockSpec((tm, r.shape[1] // ncb), lambda i, j: (i, j)))
    return specs


def _param_specs(params):
    return [pl.BlockSpec(p.shape, lambda i, j: (0, 0)) for p in params]


def _group_slices(refs, groups, g, whole):
    out = []
    for idx, r in enumerate(refs):
        w = r.shape[1] // groups
        out.append(r[...] if idx in whole else r[:, g * w:(g + 1) * w])
    return out


def _rows_fwd_call(fns, rows, params, outs, tm, ncb, bc, name):
    r_total = rows[0].shape[0]
    nr, groups = len(rows), len(fns)

    def body(*refs):
        pvals = [r[...] for r in refs[nr:nr + len(params)]]
        for g, fn in enumerate(fns):
            res = fn(*_group_slices(refs[:nr], groups, g, bc), *pvals)
            for o_ref, val in zip(refs[nr + len(params):], res):
                w = o_ref.shape[1] // groups
                o_ref[:, g * w:(g + 1) * w] = val.astype(o_ref.dtype)

    return _call(
        body, name=name, grid=(r_total // tm, ncb),
        in_specs=_rows_specs(rows, tm, ncb, bc) + _param_specs(params),
        out_specs=[pl.BlockSpec((tm, w * groups), lambda i, j: (i, j)) for w in outs],
        out_shape=[jax.ShapeDtypeStruct((r_total, w * groups * ncb), F32) for w in outs],
        compiler_params=_cparams(("parallel", "parallel")),
    )(*rows, *params)


def _rows_bwd_call(fns, rows, params, cts, tm, ncb, bc, name):
    r_total = rows[0].shape[0]
    nr, npar, nct, groups = len(rows), len(params), len(cts), len(fns)

    def body(*refs):
        i, j = pl.program_id(0), pl.program_id(1)
        pvals = [r[...] for r in refs[nr:nr + npar]]
        ct_refs = refs[nr + npar:nr + npar + nct]
        d_refs = refs[nr + npar + nct:]
        shared = {idx: None for idx in list(bc) + list(range(nr, nr + npar))}
        for g, fn in enumerate(fns):
            _, vjp = jax.vjp(lambda *a, fn=fn: tuple(fn(*a)), *_group_slices(refs[:nr], groups, g, bc), *pvals)
            grads = vjp(tuple(_group_slices(ct_refs, groups, g, ())))
            for idx in range(nr + npar):
                if idx in shared:
                    shared[idx] = grads[idx] if shared[idx] is None else shared[idx] + grads[idx]
                else:
                    w = d_refs[idx].shape[1] // groups
                    d_refs[idx][:, g * w:(g + 1) * w] = grads[idx]
        for idx, total in shared.items():
            first = (j == 0) if idx < nr else ((i == 0) & (j == 0))

            @pl.when(first)
            def _(idx=idx):
                d_refs[idx][...] = jnp.zeros_like(d_refs[idx])
            d_refs[idx][...] += total

    ct_specs = [pl.BlockSpec((tm, c.shape[1] // ncb), lambda i, j: (i, j)) for c in cts]
    return _call(
        body, name=name + "_bwd", grid=(r_total // tm, ncb),
        in_specs=_rows_specs(rows, tm, ncb, bc) + _param_specs(params) + ct_specs,
        out_specs=_rows_specs(rows, tm, ncb, bc) + _param_specs(params),
        out_shape=[jax.ShapeDtypeStruct(a.shape, F32) for a in list(rows) + list(params)],
        compiler_params=_cparams(("arbitrary", "arbitrary")),
    )(*rows, *params, *cts)


def rowop(fn, name, outs, tm, ncb=1, bc=()):
    fns = list(fn) if isinstance(fn, (list, tuple)) else [fn]

    @jax.custom_vjp
    def op(rows, params):
        return tuple(_rows_fwd_call(fns, rows, params, outs, tm, ncb, bc, name))

    def fwd(rows, params):
        return op(rows, params), (rows, params)

    def bwd(res, cts):
        rows, params = res
        d = _rows_bwd_call(fns, rows, params, cts, tm, ncb, bc, name)
        return tuple(d[:len(rows)]), tuple(d[len(rows):])

    op.defvjp(fwd, bwd)
    return op


def _sigmoid(x):
    return 1.0 / (1.0 + jnp.exp(-x))


def _silu(x):
    return x * _sigmoid(x)


def _softplus(x):
    return jnp.maximum(x, 0.0) + jnp.log(1.0 + jnp.exp(-jnp.abs(x)))


def _f_rmsnorm(x, g):
    return (x * lax.rsqrt(jnp.mean(x * x, axis=-1, keepdims=True) + EPS) * g,)


def _f_qnorm(x, g):
    return (x * lax.rsqrt(jnp.mean(x * x, axis=-1, keepdims=True) + EPS) * (g * (FOX_HD ** -0.5)),)


def _f_logsig(x, b):
    return (-_softplus(-(x + b)),)


def _f_gdn_q(x):
    y = _silu(x)
    return (y * lax.rsqrt(jnp.sum(y * y, axis=-1, keepdims=True) + EPS) * (GDN_HD ** -0.5),)


def _f_gdn_k(x):
    y = _silu(x)
    return (y * lax.rsqrt(jnp.sum(y * y, axis=-1, keepdims=True) + EPS),)


def _f_gdn_v(x):
    return (_silu(x),)


def _f_gdn_gates(bl, al, a_log, dt_bias):
    return _sigmoid(bl), -jnp.exp(a_log) * _softplus(al + dt_bias)


def _f_gdn_out(o, z, g):
    return (o * lax.rsqrt(jnp.mean(o * o, axis=-1, keepdims=True) + EPS) * g * _silu(z),)


def _f_merge(g0, g1, ya, yb):
    return (_sigmoid(g0) * ya + _sigmoid(g1) * yb,)


def _f_residual(a, b, keep):
    return ((a + b) * keep,)


def _f_residual_norm(a, b, keep, g):
    r = (a + b) * keep
    return r, _f_rmsnorm(r, g)[0]


def _f_glu(a, b):
    return (_silu(a) * b,)


def _glu_call(up, ct):
    t_total, two_f = up.shape
    f = two_f // 2
    tm = _tile(t_total, 128, 8)
    wc = _tile(f, 1408, LANE)

    def body(*refs):
        up_ref, out_ref = refs[0], refs[-1]
        for c0 in range(0, f, wc):
            a, b = up_ref[:, c0:c0 + wc], up_ref[:, f + c0:f + c0 + wc]
            if ct is None:
                out_ref[:, c0:c0 + wc] = _f_glu(a, b)[0]
            else:
                _, vjp = jax.vjp(_f_glu, a, b)
                da, db = vjp((refs[1][:, c0:c0 + wc],))
                out_ref[:, c0:c0 + wc] = da
                out_ref[:, f + c0:f + c0 + wc] = db

    wide = pl.BlockSpec((tm, two_f), lambda i: (i, 0))
    narrow = pl.BlockSpec((tm, f), lambda i: (i, 0))
    return _call(
        body, name="ffn_glu" if ct is None else "ffn_glu_bwd", grid=(t_total // tm,),
        in_specs=[wide] if ct is None else [wide, narrow], out_specs=narrow if ct is None else wide,
        out_shape=jax.ShapeDtypeStruct((t_total, f if ct is None else two_f), F32),
        compiler_params=_cparams(("parallel",)),
    )(*((up,) if ct is None else (up, ct)))


@jax.custom_vjp
def glu(up):
    return _glu_call(up, None)


glu.defvjp(lambda up: (glu(up), up), lambda up, ct: (_glu_call(up, ct),))


def _shift_down(x, halo, s, row8):
    rx = pltpu.roll(x, s, 0)
    top = jnp.where(row8 < s, pltpu.roll(halo, s, 0), rx[:8])
    return jnp.concatenate([top, rx[8:]], axis=0)


def _shift_up(x, nxt, s, row8):
    tm = x.shape[0]
    rx = pltpu.roll(x, tm - s, 0)
    bot = jnp.where(row8 >= 8 - s, pltpu.roll(nxt, 8 - s, 0), rx[tm - 8:])
    return jnp.concatenate([rx[:tm - 8], bot], axis=0)


def _conv_tiles(r_total, c_total):
    return _tile(r_total, 768, 8), _tile(c_total, 1408, LANE)


def _conv_fwd_call(x, w8, k_taps):
    r_total, c_total = x.shape
    tm, tc = _conv_tiles(r_total, c_total)
    hb = tm // 8

    def body(x_ref, halo_ref, w_ref, y_ref):
        i = pl.program_id(1)
        xt = x_ref[...]
        halo = jnp.where(i > 0, halo_ref[...], 0.0)
        row8 = lax.broadcasted_iota(jnp.int32, (8, tc), 0)
        acc = w_ref[k_taps - 1:k_taps, :] * xt
        for k in range(k_taps - 1):
            acc += w_ref[k:k + 1, :] * _shift_down(xt, halo, k_taps - 1 - k, row8)
        y_ref[...] = acc

    return _call(
        body, name="dwconv_fwd", grid=(c_total // tc, r_total // tm),
        in_specs=[pl.BlockSpec((tm, tc), lambda c, i: (i, c)),
                  pl.BlockSpec((8, tc), lambda c, i: (jnp.maximum(i * hb - 1, 0), c)),
                  pl.BlockSpec((8, tc), lambda c, i: (0, c))],
        out_specs=pl.BlockSpec((tm, tc), lambda c, i: (i, c)),
        out_shape=jax.ShapeDtypeStruct(x.shape, F32),
        compiler_params=_cparams(("parallel", "parallel")),
    )(x, x, w8)


def _conv_bwd_call(x, w8, dy, k_taps):
    r_total, c_total = x.shape
    tm, tc = _conv_tiles(r_total, c_total)
    hb = tm // 8
    n_i = r_total // tm

    def body(x_ref, halo_ref, w_ref, dy_ref, nxt_ref, dx_ref, dw_ref):
        i = pl.program_id(1)
        xt, dyt = x_ref[...], dy_ref[...]
        halo = jnp.where(i > 0, halo_ref[...], 0.0)
        nxt = jnp.where(i < n_i - 1, nxt_ref[...], 0.0)
        row8 = lax.broadcasted_iota(jnp.int32, (8, tc), 0)
        dx = w_ref[k_taps - 1:k_taps, :] * dyt
        upd = jnp.where(row8 == k_taps - 1, jnp.sum(dyt * xt, axis=0, keepdims=True), 0.0)
        for k in range(k_taps - 1):
            s = k_taps - 1 - k
            dx += w_ref[k:k + 1, :] * _shift_up(dyt, nxt, s, row8)
            upd = jnp.where(row8 == k, jnp.sum(dyt * _shift_down(xt, halo, s, row8), axis=0, keepdims=True), upd)
        dx_ref[...] = dx

        @pl.when(i == 0)
        def _():
            dw_ref[...] = jnp.zeros_like(dw_ref)

        dw_ref[...] += upd

    return _call(
        body, name="dwconv_bwd", grid=(c_total // tc, n_i),
        in_specs=[pl.BlockSpec((tm, tc), lambda c, i: (i, c)),
                  pl.BlockSpec((8, tc), lambda c, i: (jnp.maximum(i * hb - 1, 0), c)),
                  pl.BlockSpec((8, tc), lambda c, i: (0, c)),
                  pl.BlockSpec((tm, tc), lambda c, i: (i, c)),
                  pl.BlockSpec((8, tc), lambda c, i: (jnp.minimum((i + 1) * hb, r_total // 8 - 1), c))],
        out_specs=[pl.BlockSpec((tm, tc), lambda c, i: (i, c)), pl.BlockSpec((8, tc), lambda c, i: (0, c))],
        out_shape=[jax.ShapeDtypeStruct(x.shape, F32), jax.ShapeDtypeStruct(w8.shape, F32)],
        compiler_params=_cparams(("parallel", "arbitrary")),
    )(x, x, w8, dy, dy)


def make_dwconv(k_taps):
    @jax.custom_vjp
    def op(x, w8):
        return _conv_fwd_call(x, w8, k_taps)

    def fwd(x, w8):
        return op(x, w8), (x, w8)

    def bwd(res, dy):
        x, w8 = res
        dx, dw = _conv_bwd_call(x, w8, dy, k_taps)
        return dx, dw

    op.defvjp(fwd, bwd)
    return op


def _cumsum_call(x, reverse):
    h, t_total = x.shape
    tb = _tile(t_total, 256, LANE)
    nb = t_total // tb

    def body(x_ref, o_ref, carry_ref):
        i = pl.program_id(0)

        @pl.when(i == 0)
        def _():
            carry_ref[...] = jnp.zeros_like(carry_ref)

        r = lax.broadcasted_iota(jnp.int32, (tb, tb), 0)
        c = lax.broadcasted_iota(jnp.int32, (tb, tb), 1)
        tri = jnp.where((r >= c) if reverse else (r <= c), 1.0, 0.0).astype(F32)
        xv = x_ref[...]
        carry = jnp.max(carry_ref[...], axis=1, keepdims=True)
        o_ref[...] = _raw_dot(xv, tri, 1, 0, True) + carry
        carry_ref[...] = jnp.broadcast_to(carry + jnp.sum(xv, axis=1, keepdims=True), carry_ref.shape)

    imap = (lambda i: (0, nb - 1 - i)) if reverse else (lambda i: (0, i))
    return _call(
        body, name="cumsum_rev" if reverse else "cumsum", grid=(nb,),
        in_specs=[pl.BlockSpec((h, tb), imap)], out_specs=pl.BlockSpec((h, tb), imap),
        out_shape=jax.ShapeDtypeStruct(x.shape, F32), scratch_shapes=[pltpu.VMEM((h, LANE), F32)],
        compiler_params=_cparams(("arbitrary",)),
    )(x)


@jax.custom_vjp
def cumsum_lanes(x):
    return _cumsum_call(x, False)


cumsum_lanes.defvjp(lambda x: (cumsum_lanes(x), None), lambda _, ct: (_cumsum_call(ct, True),))


NEG_BIG = -1e30


def _attn_sub_tiles(nb):
    return max(s for s in (3, 2, 1) if nb % s == 0)


EXP_ZERO = -100.0
SMEM = pl.BlockSpec(memory_space=pltpu.SMEM)


def _max_row_norm_sq(x):
    h_total, t_total, hd = x.shape
    tb = _tile(t_total, 2816, 8)

    def body(x_ref, o_ref):
        @pl.when(pl.program_id(1) == 0)
        def _():
            o_ref[...] = jnp.zeros_like(o_ref)

        xv = x_ref[0]
        top = jnp.max(jnp.sum(xv * xv, axis=1, keepdims=True), axis=0, keepdims=True)
        o_ref[0] = jnp.maximum(o_ref[0], top)

    return _call(
        body, name="max_row_norm", grid=(h_total, t_total // tb),
        in_specs=[pl.BlockSpec((1, tb, hd), lambda h, i: (h, i, 0))],
        out_specs=pl.BlockSpec((1, 8, LANE), lambda h, i: (h, 0, 0)),
        out_shape=jax.ShapeDtypeStruct((h_total, 8, LANE), F32),
        compiler_params=_cparams(("parallel", "arbitrary")),
    )(x)


def _attn_skip_tables(q, k, f_row):
    bound = 2.0 * jnp.sqrt(_max_row_norm_sq(q)[:, 0, :1] * _max_row_norm_sq(k)[:, 0, :1])
    return EXP_ZERO - bound, f_row[:, :, 0, 0], f_row[:, :, 0, -1]


def _attn_fwd_call(q, k, v, f_col, f_row, tables):
    h_total, t_total, hd = q.shape
    blk = f_row.shape[-1]
    nb = t_total // blk
    nsub = _attn_sub_tiles(nb)
    tq = nsub * blk

    def body(thr_ref, first_ref, last_ref, q_ref, k_ref, vt_ref, fc_ref, fr_ref, o_ref, lse_ref):
        h = pl.program_id(0)
        i = pl.program_id(1)
        gap_needed = thr_ref[h, 0]
        f_tile = first_ref[h, i * nsub]
        j_start = lax.while_loop(lambda j: (j < i * nsub) & (f_tile - last_ref[h, j] < gap_needed),
                                 lambda j: j + 1, 0)
        r = lax.broadcasted_iota(jnp.int32, (blk, blk), 0)
        c = lax.broadcasted_iota(jnp.int32, (blk, blk), 1)
        qs = [q_ref[0, s * blk:(s + 1) * blk, :].astype(BF16) for s in range(nsub)]
        fqs = [fr_ref[0, i * nsub + s] for s in range(nsub)]

        def load_kv(j):
            off = pl.multiple_of(j * blk, blk)
            return k_ref[0, pl.ds(off, blk), :], vt_ref[0, j], fc_ref[0, pl.ds(off, blk), :]

        def tile(kv, s, carry, diagonal):
            kj, vtj, fk = kv
            m, l, acc = carry
            st = _raw_dot(kj, qs[s], 1, 1, False) + fqs[s] - fk
            if diagonal:
                st = jnp.where(r <= c, st, NEG_BIG)
            m_new = jnp.maximum(m, jnp.max(st, axis=0, keepdims=True))
            p = jnp.exp(st - m_new)
            alpha = jnp.exp(m - m_new)
            l = alpha * l + jnp.sum(p, axis=0, keepdims=True)
            acc = alpha * acc + _raw_dot(vtj, p, 1, 0, False)
            return m_new, l, acc

        def below_diagonal(j, carry):
            kv = load_kv(j)
            return tuple(tile(kv, s, carry[s], False) for s in range(nsub))

        init = tuple((jnp.full((1, blk), NEG_BIG, F32), jnp.zeros((1, blk), F32), jnp.zeros((hd, blk), F32))
                     for _ in range(nsub))
        carry = list(lax.fori_loop(j_start, i * nsub, below_diagonal, init))
        for d in range(nsub):
            kv = load_kv(i * nsub + d)
            for s in range(d, nsub):
                carry[s] = tile(kv, s, carry[s], s == d)
        for s, (m, l, acc) in enumerate(carry):
            o_ref[0, :, s * blk:(s + 1) * blk] = acc / l
            lse_ref[0, s] = m + jnp.log(l)

    vt = v.reshape(h_total, nb, blk, hd).transpose(0, 1, 3, 2).astype(BF16)
    return _call(
        body, name="fox_fwd", grid=(h_total, nb // nsub),
        in_specs=[SMEM, SMEM, SMEM,
                  pl.BlockSpec((1, tq, hd), lambda h, i: (h, i, 0)),
                  pl.BlockSpec((1, t_total, hd), lambda h, i: (h, 0, 0)),
                  pl.BlockSpec((1, nb, hd, blk), lambda h, i: (h, 0, 0, 0)),
                  pl.BlockSpec((1, t_total, 1), lambda h, i: (h, 0, 0)),
                  pl.BlockSpec((1, nb, 1, blk), lambda h, i: (h, 0, 0, 0))],
        out_specs=[pl.BlockSpec((1, hd, tq), lambda h, i: (h, 0, i)),
                   pl.BlockSpec((1, nsub, 1, blk), lambda h, i: (h, i, 0, 0))],
        out_shape=[jax.ShapeDtypeStruct((h_total, hd, t_total), F32), jax.ShapeDtypeStruct(f_row.shape, F32)],
        compiler_params=_cparams(("parallel", "parallel")),
    )(*tables, q, k.astype(BF16), vt, f_col, f_row)


def _attn_bwd_call(q, k, v, f_col, f_row, tables, lse_row, delta_row, do_blk):
    h_total, t_total, hd = q.shape
    blk = f_row.shape[-1]
    nb = t_total // blk
    nsub = _attn_sub_tiles(nb)
    tkv = nsub * blk

    def body(thr_ref, first_ref, last_ref, q_ref, do_ref, k_ref, v_ref, fc_ref, fr_ref, lse_ref, dl_ref,
             dq_ref, dk_ref, dv_ref, dfk_ref, dfq_ref):
        h = pl.program_id(0)
        j = pl.program_id(1)
        gap_needed = thr_ref[h, 0]
        f_tile = last_ref[h, j * nsub + nsub - 1]
        i_stop = lax.while_loop(lambda i: (i < nb) & (first_ref[h, jnp.minimum(i, nb - 1)] - f_tile >= gap_needed),
                                lambda i: i + 1, (j + 1) * nsub)

        @pl.when(j == 0)
        def _():
            dq_ref[...] = jnp.zeros_like(dq_ref)
            dfq_ref[...] = jnp.zeros_like(dfq_ref)

        ks = [k_ref[0, s * blk:(s + 1) * blk, :].astype(BF16) for s in range(nsub)]
        vs = [v_ref[0, s * blk:(s + 1) * blk, :].astype(BF16) for s in range(nsub)]
        fks = [fc_ref[0, s * blk:(s + 1) * blk, :] for s in range(nsub)]
        r = lax.broadcasted_iota(jnp.int32, (blk, blk), 0)
        c = lax.broadcasted_iota(jnp.int32, (blk, blk), 1)

        def q_step(i, accs, subs):
            off = pl.multiple_of(i * blk, blk)
            qi = q_ref[0, pl.ds(off, blk), :]
            doi = do_ref[0, i]
            fq, lse, dl = fr_ref[0, i], lse_ref[0, i], dl_ref[0, i]
            accs = list(accs)
            dq_i, dfq_i = None, None
            for s, diagonal in subs:
                dk, dv, dfk = accs[s]
                st = _raw_dot(ks[s], qi, 1, 1, False) + fq - fks[s] - lse
                if diagonal:
                    st = jnp.where(r <= c, st, NEG_BIG)
                pt = jnp.exp(st)
                dv = dv + _raw_dot(pt, doi, 1, 1, False)
                dst = pt * (_raw_dot(vs[s], doi, 1, 0, False) - dl)
                dk = dk + _raw_dot(dst, qi, 1, 0, False)
                dfk = dfk - jnp.sum(dst, axis=1, keepdims=True)
                accs[s] = (dk, dv, dfk)
                dq_s = _raw_dot(dst, ks[s], 0, 0, False)
                dfq_s = jnp.sum(dst, axis=0, keepdims=True)
                dq_i = dq_s if dq_i is None else dq_i + dq_s
                dfq_i = dfq_s if dfq_i is None else dfq_i + dfq_s
            dfq_ref[0, i] += dfq_i
            dq_ref[0, pl.ds(off, blk), :] += dq_i
            return tuple(accs)

        accs = tuple((jnp.zeros((blk, hd), F32), jnp.zeros((blk, hd), F32), jnp.zeros((blk, 1), F32))
                     for _ in range(nsub))
        for d in range(nsub):
            accs = q_step(j * nsub + d, accs, [(s, s == d) for s in range(d + 1)])
        accs = lax.fori_loop((j + 1) * nsub, i_stop,
                             lambda i, a: q_step(i, a, [(s, False) for s in range(nsub)]), accs)
        for s, (dk, dv, dfk) in enumerate(accs):
            dk_ref[0, s * blk:(s + 1) * blk, :] = dk
            dv_ref[0, s * blk:(s + 1) * blk, :] = dv
            dfk_ref[0, s * blk:(s + 1) * blk, :] = dfk

    full = pl.BlockSpec((1, t_total, hd), lambda h, j: (h, 0, 0))
    tile = pl.BlockSpec((1, tkv, hd), lambda h, j: (h, j, 0))
    col = pl.BlockSpec((1, tkv, 1), lambda h, j: (h, j, 0))
    rows = pl.BlockSpec((1, nb, 1, blk), lambda h, j: (h, 0, 0, 0))
    do_blocks = pl.BlockSpec((1, nb, hd, blk), lambda h, j: (h, 0, 0, 0))
    return _call(
        body, name="fox_bwd", grid=(h_total, nb // nsub),
        in_specs=[SMEM, SMEM, SMEM, full, do_blocks, tile, tile, col, rows, rows, rows],
        out_specs=[full, tile, tile, col, rows],
        out_shape=[jax.ShapeDtypeStruct(q.shape, F32), jax.ShapeDtypeStruct(q.shape, F32),
                   jax.ShapeDtypeStruct(q.shape, F32), jax.ShapeDtypeStruct(f_col.shape, F32),
                   jax.ShapeDtypeStruct(f_row.shape, F32)],
        compiler_params=_cparams(("parallel", "arbitrary")),
    )(*tables, q.astype(BF16), do_blk, k, v, f_col, f_row, lse_row, delta_row)


def _attn_delta_call(do_t, o_t):
    h_total, hd, t_total = o_t.shape
    tb = _tile(t_total, 2816, LANE)

    def body(do_ref, o_ref, d_ref):
        d_ref[0] = jnp.sum(do_ref[0] * o_ref[0], axis=0, keepdims=True)

    spec = pl.BlockSpec((1, hd, tb), lambda h, i: (h, 0, i))
    return _call(
        body, name="fox_delta", grid=(h_total, t_total // tb), in_specs=[spec, spec],
        out_specs=pl.BlockSpec((1, 1, tb), lambda h, i: (h, 0, i)),
        out_shape=jax.ShapeDtypeStruct((h_total, 1, t_total), F32),
        compiler_params=_cparams(("parallel", "parallel")),
    )(do_t, o_t)


@jax.custom_vjp
def fox_attention(q, k, v, f_col, f_row):
    return _attn_fwd_call(q, k, v, f_col, f_row, _attn_skip_tables(q, k, f_row))[0]


def _fox_fwd(q, k, v, f_col, f_row):
    tables = _attn_skip_tables(q, k, f_row)
    o_t, lse_row = _attn_fwd_call(q, k, v, f_col, f_row, tables)
    return o_t, (q, k, v, f_col, f_row, tables, o_t, lse_row)


def _fox_bwd(res, do_t):
    q, k, v, f_col, f_row, tables, o_t, lse_row = res
    h_total, t_total, hd = q.shape
    nb, blk = f_row.shape[1], f_row.shape[3]
    delta = _attn_delta_call(do_t, o_t).reshape(f_row.shape)
    do_blk = do_t.reshape(h_total, hd, nb, blk).transpose(0, 2, 1, 3).astype(BF16)
    return tuple(_attn_bwd_call(q, k, v, f_col, f_row, tables, lse_row, delta, do_blk))


fox_attention.defvjp(_fox_fwd, _fox_bwd)


def _head_col(blk, h):
    lane = lax.broadcasted_iota(jnp.int32, blk.shape, 1)
    return jnp.sum(jnp.where(lane == h, blk, 0.0), axis=1, keepdims=True)


@jax.custom_vjp
def _cat2(a, b):
    return jnp.concatenate([a, b], axis=1)


_cat2.defvjp(lambda a, b: (_cat2(a, b), a.shape[1]), lambda na, ct: (ct[:, :na], ct[:, na:]))


@jax.custom_vjp
def _split2(x):
    half = x.shape[1] // 2
    return x[:, :half], x[:, half:]


_split2.defvjp(lambda x: (_split2(x), None), lambda _, cts: (jnp.concatenate(cts, axis=1),))


def _neumann_solve(m, b):
    x = b - _raw_dot(m, b, 1, 0, False)
    powers, steps = [m], 1
    while 2 * steps < GDN_CHUNK:
        powers.append(_raw_dot(powers[-1], powers[-1], 1, 0, False))
        x = x + _raw_dot(powers[-1], x, 1, 0, False)
        steps *= 2
    return x, powers


@jax.custom_vjp
def _unit_lower_solve(m, b):
    return _neumann_solve(m, b)[0]


def _unit_lower_solve_fwd(m, b):
    x, powers = _neumann_solve(m, b)
    return x, (powers, x)


def _unit_lower_solve_bwd(res, dx):
    powers, x = res
    db = dx - _raw_dot(powers[0], dx, 0, 0, False)
    for p in powers[1:]:
        db = db + _raw_dot(p, db, 0, 0, False)
    return -_raw_dot(db, x, 1, 1, False), db


_unit_lower_solve.defvjp(_unit_lower_solve_fwd, _unit_lower_solve_bwd)


def _gdn_intra(h, q, k, v, b_blk, g_blk):
    n = q.shape[0]
    b, g = _head_col(b_blk, h), _head_col(g_blk, h)
    r = lax.broadcasted_iota(jnp.int32, (n, n), 0)
    c = lax.broadcasted_iota(jnp.int32, (n, n), 1)
    same = (r // GDN_CHUNK) == (c // GDN_CHUNK)
    incl = same & (r >= c)
    g_row = jnp.sum(jnp.where(r == c, g, 0.0), axis=0, keepdims=True)
    big_g = jnp.sum(jnp.where(incl, g_row, 0.0), axis=1, keepdims=True)
    big_g_row = jnp.sum(jnp.where(same & (r <= c), g, 0.0), axis=0, keepdims=True)
    g_tot = jnp.sum(jnp.where(same, g_row, 0.0), axis=1, keepdims=True)
    dec = jnp.where(incl, jnp.exp(jnp.where(incl, big_g - big_g_row, 0.0)), 0.0)
    dec_strict = jnp.where(r > c, dec, 0.0)
    e_g = jnp.exp(big_g)
    kb = k * b
    m = _dot(kb, k, 1, 1) * dec_strict
    rs = lax.broadcasted_iota(jnp.int32, (n, GDN_CHUNK), 0)
    cs = lax.broadcasted_iota(jnp.int32, (n, GDN_CHUNK), 1)
    fold = jnp.where(rs % GDN_CHUNK == cs, 1.0, 0.0).astype(F32)
    aqk = _dot(_dot(q, k, 1, 1) * dec, fold, 1, 0, True)
    u, w = _split2(_unit_lower_solve(m, _cat2(v * b, kb * e_g)))
    lane = lax.broadcasted_iota(jnp.int32, b_blk.shape, 1)
    return u, w, q * e_g, k * jnp.exp(g_tot - big_g), aqk, jnp.where(lane == h, g_tot, 0.0)


def _gdn_rec(h, s, u, w, qg, kd, aqk, gl_blk):
    g_last = jnp.max(_head_col(gl_blk, h), axis=0, keepdims=True)
    big_u = u - _dot(w, s)
    o = _dot(qg, s) + _dot(aqk, big_u)
    s_next = s * jnp.exp(g_last) + _dot(kd, big_u, 0, 0)
    return o, s_next


GDN_TOK_BLK = 256


def _gdn_layout(t_total, rev):
    tb = _tile(t_total, GDN_TOK_BLK, GDN_CHUNK)
    cb, nblk = tb // GDN_CHUNK, t_total // tb
    pos = (lambda i: nblk - 1 - i) if rev else (lambda i: i)
    specs = dict(
        tok=pl.BlockSpec((tb, GDN_W), lambda i: (pos(i), 0)),
        q=pl.BlockSpec((tb, GDN_W), lambda i: (pos(i), 0)),
        k=pl.BlockSpec((tb, GDN_W), lambda i: (pos(i), 1)),
        v=pl.BlockSpec((tb, GDN_W), lambda i: (pos(i), 2)),
        qkv=pl.BlockSpec((tb, 3 * GDN_W), lambda i: (pos(i), 0)),
        gate=pl.BlockSpec((tb, GDN_HEADS), lambda i: (pos(i), 0)),
        aqk=pl.BlockSpec((GDN_HEADS, tb, GDN_CHUNK), lambda i: (0, pos(i), 0)),
        state=pl.BlockSpec((GDN_HEADS, cb, GDN_HD, GDN_HD), lambda i: (0, pos(i), 0, 0)))
    return cb, nblk, specs


def _gdn_shapes(t_total):
    n_chunks = t_total // GDN_CHUNK
    return dict(tok=jax.ShapeDtypeStruct((t_total, GDN_W), F32),
                gate=jax.ShapeDtypeStruct((t_total, GDN_HEADS), F32),
                aqk=jax.ShapeDtypeStruct((GDN_HEADS, t_total, GDN_CHUNK), F32),
                state=jax.ShapeDtypeStruct((GDN_HEADS, n_chunks, GDN_HD, GDN_HD), F32))


def _chunk_rows(ci):
    return pl.ds(pl.multiple_of(ci * GDN_CHUNK, GDN_CHUNK), GDN_CHUNK)


def _head_cols(h):
    return pl.ds(h * GDN_HD, GDN_HD)


def _gdn_intra_fwd_call(qkv, b, g):
    cb, nblk, sp = _gdn_layout(qkv.shape[0], False)
    sh = _gdn_shapes(qkv.shape[0])

    def body(q_ref, k_ref, v_ref, b_ref, g_ref, u_ref, w_ref, qg_ref, kd_ref, aqk_ref, gl_ref):
        b_blk, g_blk = b_ref[...], g_ref[...]
        gl = jnp.zeros(b_blk.shape, F32)
        for h in range(GDN_HEADS):
            cols = _head_cols(h)
            u, w, qg, kd, aqk, gl_h = _gdn_intra(h, q_ref[:, cols], k_ref[:, cols], v_ref[:, cols], b_blk, g_blk)
            u_ref[:, cols] = u
            w_ref[:, cols] = w
            qg_ref[:, cols] = qg
            kd_ref[:, cols] = kd
            aqk_ref[h] = aqk
            gl = gl + gl_h
        gl_ref[...] = gl

    return _call(
        body, name="gdn_intra_fwd", grid=(nblk,),
        in_specs=[sp["q"], sp["k"], sp["v"]] + [sp["gate"]] * 2,
        out_specs=[sp["tok"]] * 4 + [sp["aqk"], sp["gate"]],
        out_shape=[sh["tok"]] * 4 + [sh["aqk"], sh["gate"]],
        compiler_params=_cparams(("parallel",)),
    )(qkv, qkv, qkv, b, g)


def _gdn_intra_bwd_call(qkv, b, g, du, dw, dqg, dkd, daqk, dgl):
    cb, nblk, sp = _gdn_layout(qkv.shape[0], False)
    sh = _gdn_shapes(qkv.shape[0])

    def body(q_ref, k_ref, v_ref, b_ref, g_ref, du_ref, dw_ref, dqg_ref, dkd_ref, daqk_ref, dgl_ref,
             dqkv_ref, db_ref, dg_ref):
        b_blk, g_blk, dgl = b_ref[...], g_ref[...], dgl_ref[...]
        db = jnp.zeros(b_blk.shape, F32)
        dg = jnp.zeros(b_blk.shape, F32)
        for h in range(GDN_HEADS):
            cols = _head_cols(h)
            _, vjp = jax.vjp(functools.partial(_gdn_intra, h), q_ref[:, cols], k_ref[:, cols], v_ref[:, cols],
                             b_blk, g_blk)
            dq, dk, dv, db_h, dg_h = vjp((du_ref[:, cols], dw_ref[:, cols], dqg_ref[:, cols], dkd_ref[:, cols],
                                          daqk_ref[h], dgl))
            dqkv_ref[:, pl.ds(h * GDN_HD, GDN_HD)] = dq
            dqkv_ref[:, pl.ds(GDN_W + h * GDN_HD, GDN_HD)] = dk
            dqkv_ref[:, pl.ds(2 * GDN_W + h * GDN_HD, GDN_HD)] = dv
            db = db + db_h
            dg = dg + dg_h
        db_ref[...] = db
        dg_ref[...] = dg

    return _call(
        body, name="gdn_intra_bwd", grid=(nblk,),
        in_specs=[sp["q"], sp["k"], sp["v"]] + [sp["gate"]] * 2 + [sp["tok"]] * 4 + [sp["aqk"], sp["gate"]],
        out_specs=[sp["qkv"]] + [sp["gate"]] * 2,
        out_shape=[jax.ShapeDtypeStruct(qkv.shape, F32)] + [sh["gate"]] * 2,
        compiler_params=_cparams(("parallel",)),
    )(qkv, qkv, qkv, b, g, du, dw, dqg, dkd, daqk, dgl)


def _gdn_rec_fwd_call(u, w, qg, kd, aqk, gl):
    cb, nblk, sp = _gdn_layout(u.shape[0], False)
    sh = _gdn_shapes(u.shape[0])

    def body(u_ref, w_ref, qg_ref, kd_ref, aqk_ref, gl_ref, o_ref, s_all_ref, s_ref):
        @pl.when(pl.program_id(0) == 0)
        def _():
            s_ref[...] = jnp.zeros_like(s_ref)

        def chunk(ci, carry):
            rows = _chunk_rows(ci)
            gl_row = gl_ref[rows, :]
            states = [s_ref[h] for h in range(GDN_HEADS)]
            res = [_gdn_rec(h, states[h], u_ref[rows, _head_cols(h)], w_ref[rows, _head_cols(h)],
                            qg_ref[rows, _head_cols(h)], kd_ref[rows, _head_cols(h)], aqk_ref[h, rows, :], gl_row)
                   for h in range(GDN_HEADS)]
            for h, (o, s_next) in enumerate(res):
                s_all_ref[h, ci] = states[h]
                o_ref[rows, _head_cols(h)] = o
                s_ref[h] = s_next
            return carry

        lax.fori_loop(0, cb, chunk, 0)

    return _call(
        body, name="gdn_rec_fwd", grid=(nblk,),
        in_specs=[sp["tok"]] * 4 + [sp["aqk"], sp["gate"]],
        out_specs=[sp["tok"], sp["state"]], out_shape=[sh["tok"], sh["state"]],
        scratch_shapes=[pltpu.VMEM((GDN_HEADS, GDN_HD, GDN_HD), F32)],
        compiler_params=_cparams(("arbitrary",)),
    )(u, w, qg, kd, aqk, gl)


def _gdn_rec_bwd_call(u, w, qg, kd, aqk, gl, s_all, do):
    cb, nblk, sp = _gdn_layout(u.shape[0], True)
    sh = _gdn_shapes(u.shape[0])

    def body(u_ref, w_ref, qg_ref, kd_ref, aqk_ref, gl_ref, s_all_ref, do_ref,
             du_ref, dw_ref, dqg_ref, dkd_ref, daqk_ref, dgl_ref, ds_ref):
        @pl.when(pl.program_id(0) == 0)
        def _():
            ds_ref[...] = jnp.zeros_like(ds_ref)

        def chunk(step, carry):
            ci = cb - 1 - step
            rows = _chunk_rows(ci)
            gl_row = gl_ref[rows, :]
            res = []
            for h in range(GDN_HEADS):
                cols = _head_cols(h)
                _, vjp = jax.vjp(functools.partial(_gdn_rec, h), s_all_ref[h, ci], u_ref[rows, cols],
                                 w_ref[rows, cols], qg_ref[rows, cols], kd_ref[rows, cols], aqk_ref[h, rows, :],
                                 gl_row)
                res.append(vjp((do_ref[rows, cols], ds_ref[h])))
            dgl = jnp.zeros((GDN_CHUNK, GDN_HEADS), F32)
            for h, (ds, du, dw, dqg, dkd, daqk, dgl_h) in enumerate(res):
                cols = _head_cols(h)
                ds_ref[h] = ds
                du_ref[rows, cols] = du
                dw_ref[rows, cols] = dw
                dqg_ref[rows, cols] = dqg
                dkd_ref[rows, cols] = dkd
                daqk_ref[h, rows, :] = daqk
                dgl = dgl + dgl_h
            dgl_ref[rows, :] = dgl
            return carry

        lax.fori_loop(0, cb, chunk, 0)

    return _call(
        body, name="gdn_rec_bwd", grid=(nblk,),
        in_specs=[sp["tok"]] * 4 + [sp["aqk"], sp["gate"], sp["state"], sp["tok"]],
        out_specs=[sp["tok"]] * 4 + [sp["aqk"], sp["gate"]],
        out_shape=[sh["tok"]] * 4 + [sh["aqk"], sh["gate"]],
        scratch_shapes=[pltpu.VMEM((GDN_HEADS, GDN_HD, GDN_HD), F32)],
        compiler_params=_cparams(("arbitrary",)),
    )(u, w, qg, kd, aqk, gl, s_all, do)


@jax.custom_vjp
def gdn_intra(qkv, b, g):
    return tuple(_gdn_intra_fwd_call(qkv, b, g))


gdn_intra.defvjp(lambda *a: (gdn_intra(*a), a), lambda res, cts: tuple(_gdn_intra_bwd_call(*res, *cts)))


@jax.custom_vjp
def gdn_rec(u, w, qg, kd, aqk, gl):
    return _gdn_rec_fwd_call(u, w, qg, kd, aqk, gl)[0]


def _gdn_rec_fwd(*a):
    o, s_all = _gdn_rec_fwd_call(*a)
    return o, a + (s_all,)


gdn_rec.defvjp(_gdn_rec_fwd, lambda res, do: tuple(_gdn_rec_bwd_call(*res, do)))


def gated_delta(qkv, b, g):
    return gdn_rec(*gdn_intra(qkv, b, g))


def _loss_call(y, tgt, first, last):
    r_total, d = y.shape
    tm = _tile(r_total, 256, 8)

    def body(y_ref, t_ref, loss_ref, dy_ref):
        i = pl.program_id(0)

        @pl.when(i == 0)
        def _():
            loss_ref[...] = jnp.zeros_like(loss_ref)

        row = lax.broadcasted_iota(jnp.int32, (tm, d), 0) + i * tm
        err = jnp.where((row >= first) & (row < last), y_ref[...] - t_ref[...], 0.0)
        dy_ref[...] = err * (1.0 / d)
        part = jnp.sum(jnp.sum(err * err, axis=1, keepdims=True), axis=0, keepdims=True) * (0.5 / d)
        loss_ref[...] += jnp.broadcast_to(part, loss_ref.shape)

    return _call(
        body, name="loss_head", grid=(r_total // tm,),
        in_specs=[pl.BlockSpec((tm, d), lambda i: (i, 0))] * 2,
        out_specs=[pl.BlockSpec((8, LANE), lambda i: (0, 0)), pl.BlockSpec((tm, d), lambda i: (i, 0))],
        out_shape=[jax.ShapeDtypeStruct((8, LANE), F32), jax.ShapeDtypeStruct(y.shape, F32)],
        compiler_params=_cparams(("arbitrary",)),
    )(y, tgt)


def make_loss(first, last):
    @jax.custom_vjp
    def op(y, tgt):
        return _loss_call(y, tgt, first, last)[0][0, 0]

    def fwd(y, tgt):
        loss, dy = _loss_call(y, tgt, first, last)
        return loss[0, 0], (dy,)

    def bwd(res, ct):
        return res[0] * ct, jnp.zeros_like(res[0])

    op.defvjp(fwd, bwd)
    return op


def _pad_rows8(w):
    return jnp.concatenate([w, jnp.zeros((8 - w.shape[0], w.shape[1]), w.dtype)], axis=0)


def local_loss(wts, x, tgt):
    seq = x.shape[0]
    n_tok = N_META + seq
    t_pad = -(-n_tok // ROW_ALIGN) * ROW_ALIGN
    depth = wts["norm1_g"].shape[0]
    blk = _tile(t_pad, ATT_BLK, LANE)
    nb = t_pad // blk
    tm = _tile(t_pad, 256, 8)

    rms = rowop(_f_rmsnorm, "rmsnorm", (D_MODEL,), tm)
    qnorm = rowop(_f_qnorm, "fox_q_norm", (FOX_HD,), _tile(FOX_HEADS * t_pad, 2048, 8))
    knorm = rowop(_f_rmsnorm, "fox_k_norm", (FOX_HD,), _tile(FOX_HEADS * t_pad, 2048, 8))
    logsig = rowop(_f_logsig, "fox_log_forget", (FOX_HEADS,), tm)
    gdn_act = rowop([_f_gdn_q] * GDN_HEADS + [_f_gdn_k] * GDN_HEADS + [_f_gdn_v] * GDN_HEADS, "gdn_qkv_act",
                    (GDN_HD,), tm)
    gates = rowop(_f_gdn_gates, "gdn_gates", (GDN_HEADS, GDN_HEADS), tm)
    gdn_out = rowop([_f_gdn_out] * GDN_HEADS, "gdn_out_norm", (GDN_HD,), tm)
    merge = rowop(_f_merge, "branch_merge", (D_MODEL,), tm)
    residual = rowop(_f_residual, "residual_add", (D_MODEL,), tm, bc=(2,))
    residual_norm = rowop(_f_residual_norm, "residual_add_norm", (D_MODEL, D_MODEL), tm, bc=(2,))
    keep = (jnp.arange(t_pad)[:, None] < n_tok).astype(F32)
    conv4 = make_dwconv(GDN_CONV)
    conv3 = make_dwconv(FFN_CONV)
    loss_op = make_loss(N_META, n_tok)

    zeros = jnp.zeros((t_pad - n_tok, D_MODEL), F32)
    h_res = jnp.concatenate([wts["meta_tokens"], x, zeros], axis=0)
    tgt_rows = jnp.concatenate([jnp.zeros((N_META, D_MODEL), F32), tgt, zeros], axis=0)

    def heads(a):
        return a.reshape(t_pad, FOX_HEADS, FOX_HD).transpose(1, 0, 2).reshape(FOX_HEADS * t_pad, FOX_HD)

    h = rms((h_res,), (wts["norm1_g"][0][None],))[0]
    for l in range(depth):
        proj = mm(h, wts["w_in"][l])
        qn = qnorm((heads(proj[:, 0:512]),), (wts["fox_q_norm_g"][l][None],))[0]
        kn = knorm((heads(proj[:, 512:1024]),), (wts["fox_k_norm_g"][l][None],))[0]
        vh = heads(proj[:, 1024:1536])
        log_f = logsig((proj[:, 1536:1544],), (wts["fox_f_bias"][l][None],))[0]
        f_cum = cumsum_lanes(log_f.T)
        o_a = fox_attention(qn.reshape(FOX_HEADS, t_pad, FOX_HD), kn.reshape(FOX_HEADS, t_pad, FOX_HD),
                            vh.reshape(FOX_HEADS, t_pad, FOX_HD), f_cum[:, :, None],
                            f_cum.reshape(FOX_HEADS, nb, 1, blk))
        y_a = mm(o_a.transpose(2, 0, 1).reshape(t_pad, FOX_W), wts["w_branch_a"][l])
        cv = conv4(proj[:, 1664:4736], _pad_rows8(wts["gdn_conv_w"][l]))
        qkv = gdn_act((cv,), ())[0]
        beta, gdec = gates((proj[:, 4736:4744], proj[:, 4744:4752]),
                           (wts["gdn_a_log"][l][None], wts["gdn_dt_bias"][l][None]))
        o_b = gated_delta(qkv, beta, gdec)
        o_b = gdn_out((o_b, proj[:, 4864:5888]), (wts["gdn_norm_g"][l][None],))[0]
        y_b = mm(o_b, wts["w_branch_b"][l])
        mixed = merge((proj[:, 5888:6912], proj[:, 6912:7936], y_a, y_b), ())[0]
        h_res, h = residual_norm((h_res, mm(mixed, wts["w_out"][l]), keep), (wts["norm2_g"][l][None],))
        up = conv3(mm(h, wts["w_up"][l]), _pad_rows8(wts["ffn_conv_w"][l]))
        act = glu(up)
        down = mm(act, wts["w_down"][l])
        if l + 1 < depth:
            h_res, h = residual_norm((h_res, down, keep), (wts["norm1_g"][l + 1][None],))
        else:
            h_res = residual((h_res, down, keep), ())[0]
    return loss_op(h_res, tgt_rows)


def pad_w_in(w):
    parts, pos = [], 0
    for src, width, dst in IN_SEGS:
        if dst > pos:
            parts.append(jnp.zeros(w.shape[:-1] + (dst - pos,), w.dtype))
        parts.append(w[..., src:src + width])
        pos = dst + width
    parts.append(jnp.zeros(w.shape[:-1] + (D_IN_PAD - pos,), w.dtype))
    return jnp.concatenate(parts, axis=-1)


def unpad_w_in(w):
    return jnp.concatenate([w[..., dst:dst + width] for _, width, dst in IN_SEGS], axis=-1)


ANY = pl.BlockSpec(memory_space=pl.ANY)
N_CHIPS = 4
N_DEV = 8
COMM_COLS = 1024
COMM_ROW_ALIGN = 512
COMM_ROW_ALIGN_SMALL = 32


def _place():
    return lax.axis_index("x"), lax.axis_index("y"), lax.axis_index("c")


def _other_chips(x, y):
    return [(1 - x, y), (x, 1 - y), (1 - x, 1 - y)]


def _remote(src, dst, send_sem, recv_sem, dev):
    return pltpu.make_async_remote_copy(src_ref=src, dst_ref=dst, send_sem=send_sem, recv_sem=recv_sem,
                                        device_id=dev, device_id_type=MESH)


def chip_all_gather(buf):
    rows, cols = buf.shape
    half = rows // 2

    def body(x_ref, out_ref, send_sems, recv_sems, pass_send, pass_recv):
        x, y, c = _place()
        me = 2 * x + y
        mine, other = pl.ds(c * half, half), pl.ds((1 - c) * half, half)
        sibling = (x, y, 1 - c)
        chips = _other_chips(x, y)
        started = []
        for k, (px, py) in enumerate(chips):
            cp = _remote(x_ref.at[mine], out_ref.at[me, mine], send_sems.at[k], recv_sems.at[k], (px, py, c))
            cp.start()
            started.append(cp)
        for k, (px, py) in enumerate(chips):
            landed = out_ref.at[2 * px + py, mine]
            _remote(landed, landed, send_sems.at[k], recv_sems.at[k], (px, py, c)).wait_recv()
            cp = _remote(landed, landed, pass_send.at[k], pass_recv.at[k], sibling)
            cp.start()
            started.append(cp)
        for k, (px, py) in enumerate(chips):
            passed = out_ref.at[2 * px + py, other]
            _remote(passed, passed, pass_send.at[k], pass_recv.at[k], sibling).wait_recv()
        for cp in started:
            cp.wait_send()

    got = _call(
        body, name="chip_all_gather", in_specs=[ANY], out_specs=ANY,
        out_shape=jax.ShapeDtypeStruct((N_CHIPS, rows, cols), buf.dtype),
        scratch_shapes=[pltpu.SemaphoreType.DMA((3,)), pltpu.SemaphoreType.DMA((3,)),
                        pltpu.SemaphoreType.DMA((3,)), pltpu.SemaphoreType.DMA((3,))],
    )(buf)
    me = 2 * lax.axis_index("x") + lax.axis_index("y")
    return lax.dynamic_update_slice(got, buf[None], (me, 0, 0))


def sibling_swap_halves(g4):
    n, rows, cols = g4.shape
    half = rows // 2

    def body(g_ref, got_ref, send_sem, recv_sem):
        x, y, c = _place()
        cp = _remote(g_ref.at[:, pl.ds((1 - c) * half, half), :], got_ref, send_sem, recv_sem, (x, y, 1 - c))
        cp.start()
        cp.wait()

    return _call(
        body, name="sibling_swap_halves", in_specs=[ANY], out_specs=ANY,
        out_shape=jax.ShapeDtypeStruct((n, half, cols), g4.dtype),
        scratch_shapes=[pltpu.SemaphoreType.DMA, pltpu.SemaphoreType.DMA],
    )(g4)


def add_own_half(g4, got, c):
    n, rows, cols = g4.shape
    half = rows // 2
    tm = _tile(half, 256, 16)
    nt = half // tm

    def body(c_ref, a_ref, b_ref, o_ref):
        o_ref[...] = (a_ref[...] + b_ref[...]).astype(o_ref.dtype)

    return _call(
        body, name="add_own_half",
        grid_spec=pltpu.PrefetchScalarGridSpec(
            num_scalar_prefetch=1, grid=(n, nt),
            in_specs=[pl.BlockSpec((1, tm, cols), lambda j, i, c_ref: (j, c_ref[0] * nt + i, 0)),
                      pl.BlockSpec((1, tm, cols), lambda j, i, c_ref: (j, i, 0))],
            out_specs=pl.BlockSpec((1, tm, cols), lambda j, i, c_ref: (j, i, 0))),
        out_shape=jax.ShapeDtypeStruct(got.shape, BF16),
        compiler_params=_cparams(("parallel", "parallel")),
    )(c.reshape(1).astype(jnp.int32), g4, got)


def chip_scatter(p4):
    n, rows, cols = p4.shape

    def body(p_ref, out_ref, send_sems, recv_sems):
        x, y, c = _place()
        me = 2 * x + y
        chips = _other_chips(x, y)
        started = []
        for k, (px, py) in enumerate(chips):
            cp = _remote(p_ref.at[2 * px + py], out_ref.at[me], send_sems.at[k], recv_sems.at[k], (px, py, c))
            cp.start()
            started.append(cp)
        for k, (px, py) in enumerate(chips):
            landed = out_ref.at[2 * px + py]
            _remote(landed, landed, send_sems.at[k], recv_sems.at[k], (px, py, c)).wait_recv()
        for cp in started:
            cp.wait_send()

    got = _call(
        body, name="chip_scatter", in_specs=[ANY], out_specs=ANY,
        out_shape=jax.ShapeDtypeStruct(p4.shape, p4.dtype),
        scratch_shapes=[pltpu.SemaphoreType.DMA((3,)), pltpu.SemaphoreType.DMA((3,))],
    )(p4)
    me = 2 * lax.axis_index("x") + lax.axis_index("y")
    return lax.dynamic_update_slice(got, lax.dynamic_slice_in_dim(p4, me, 1, axis=0), (me, 0, 0))


def sum_slots(a):
    n, rows, cols = a.shape
    tm = _tile(rows, 256, 16) if rows % 16 == 0 else rows

    def body(a_ref, o_ref):
        acc = a_ref[0].astype(F32)
        for k in range(1, n):
            acc = acc + a_ref[k].astype(F32)
        o_ref[...] = acc

    return _call(
        body, name="sum_slots_%d" % n, grid=(rows // tm,),
        in_specs=[pl.BlockSpec((n, tm, cols), lambda i: (0, i, 0))],
        out_specs=pl.BlockSpec((tm, cols), lambda i: (i, 0)),
        out_shape=jax.ShapeDtypeStruct((rows, cols), F32),
        compiler_params=_cparams(("parallel",)),
    )(a)


def sibling_join(s):
    half, cols = s.shape

    def body(s_ref, got_ref, send_sem, recv_sem):
        x, y, c = _place()
        cp = _remote(s_ref, got_ref, send_sem, recv_sem, (x, y, 1 - c))
        cp.start()
        cp.wait()

    got = _call(
        body, name="sibling_join", in_specs=[ANY], out_specs=ANY,
        out_shape=jax.ShapeDtypeStruct(s.shape, s.dtype),
        scratch_shapes=[pltpu.SemaphoreType.DMA, pltpu.SemaphoreType.DMA],
    )(s)
    c = lax.axis_index("c")
    out = jnp.zeros((2 * half, cols), s.dtype)
    out = lax.dynamic_update_slice(out, s, (c * half, 0))
    return lax.dynamic_update_slice(out, got, ((1 - c) * half, 0))


def all_devices_gather(buf):
    rows, cols = buf.shape

    def body(b_ref, out_ref, send_sems, recv_sems, local_sem):
        x, y, c = _place()
        me = 4 * x + 2 * y + c
        local = pltpu.make_async_copy(b_ref, out_ref.at[me], local_sem)
        local.start()
        peers = [((x + dx) % 2, (y + dy) % 2, (c + dc) % 2)
                 for dx in (0, 1) for dy in (0, 1) for dc in (0, 1) if dx + dy + dc > 0]
        started = []
        for k, peer in enumerate(peers):
            cp = _remote(b_ref, out_ref.at[me], send_sems.at[k], recv_sems.at[k], peer)
            cp.start()
            started.append(cp)
        for k, (px, py, pc) in enumerate(peers):
            landed = out_ref.at[4 * px + 2 * py + pc]
            _remote(landed, landed, send_sems.at[k], recv_sems.at[k], (px, py, pc)).wait_recv()
        for cp in started:
            cp.wait_send()
        local.wait()

    return _call(
        body, name="all_devices_gather", in_specs=[ANY], out_specs=ANY,
        out_shape=jax.ShapeDtypeStruct((N_DEV, rows, cols), buf.dtype),
        scratch_shapes=[pltpu.SemaphoreType.DMA((7,)), pltpu.SemaphoreType.DMA((7,)), pltpu.SemaphoreType.DMA],
    )(buf)


def adamw(w, g, m, v):
    shape = w.shape
    w2, g2, m2, v2 = [a.reshape(-1, shape[-1]) for a in (w, g, m, v)]
    rows, cols = w2.shape
    tm = _tile(rows, 256, 8) if rows % 8 == 0 else rows

    def body(w_ref, g_ref, m_ref, v_ref, d_ref, nm_ref, nv_ref):
        gv = g_ref[...]
        nm = ADAM_B1 * m_ref[...] + (1.0 - ADAM_B1) * gv
        nv = ADAM_B2 * v_ref[...] + (1.0 - ADAM_B2) * (gv * gv)
        m_hat = nm / (1.0 - ADAM_B1 ** ADAM_STEP)
        v_hat = nv / (1.0 - ADAM_B2 ** ADAM_STEP)
        d_ref[...] = -ADAM_LR * (m_hat / (jnp.sqrt(v_hat) + ADAM_EPS) + ADAM_WD * w_ref[...])
        nm_ref[...] = nm
        nv_ref[...] = nv

    spec = pl.BlockSpec((tm, cols), lambda i: (i, 0))
    outs = _call(
        body, name="adamw", grid=(rows // tm,), in_specs=[spec] * 4, out_specs=[spec] * 3,
        out_shape=[jax.ShapeDtypeStruct((rows, cols), F32)] * 3,
        compiler_params=_cparams(("parallel",)),
    )(w2, g2, m2, v2)
    return [o.reshape(shape) for o in outs]


WEIGHTS = ("meta_tokens", "norm1_g", "w_in", "fox_f_bias", "fox_q_norm_g", "fox_k_norm_g", "gdn_conv_w",
           "gdn_a_log", "gdn_dt_bias", "gdn_norm_g", "w_branch_a", "w_branch_b", "w_out", "norm2_g", "w_up",
           "ffn_conv_w", "w_down")
SHARD_AXIS = {"meta_tokens": -1, "w_in": -1, "gdn_conv_w": -1, "w_branch_a": -1, "w_branch_b": -2, "w_out": -2,
              "w_up": -1, "ffn_conv_w": -1, "w_down": -2}
MATMUL_WEIGHTS = ("w_in", "w_branch_a", "w_branch_b", "w_out", "w_up", "w_down")
SMALL_SHARDED = ("meta_tokens", "gdn_conv_w", "ffn_conv_w")
REPLICATED = tuple(n for n in WEIGHTS if n not in SHARD_AXIS)


def _pack(arrays, dtype, row_align):
    flat = jnp.concatenate([a.reshape(-1).astype(dtype) for a in arrays])
    block = row_align * COMM_COLS
    total = -(-flat.shape[0] // block) * block
    flat = jnp.concatenate([flat, jnp.zeros((total - flat.shape[0],), dtype)])
    return flat.reshape(-1, COMM_COLS)


def _unpack(buf, shapes):
    flat, out, pos = buf.reshape(-1), [], 0
    for shape in shapes:
        size = 1
        for d in shape:
            size *= d
        out.append(flat[pos:pos + size].reshape(shape))
        pos += size
    return out


def _gather_full(shards, names, dtype, row_align):
    got = chip_all_gather(_pack([shards[n] for n in names], dtype, row_align))
    per_chip = [_unpack(got[j], [shards[n].shape for n in names]) for j in range(N_CHIPS)]
    return {n: jnp.concatenate([per_chip[j][i] for j in range(N_CHIPS)], axis=SHARD_AXIS[n]).astype(F32)
            for i, n in enumerate(names)}


def _shard_of(full, name, j):
    axis = SHARD_AXIS[name] % full.ndim
    size = full.shape[axis] // N_CHIPS
    return lax.slice_in_dim(full, j * size, (j + 1) * size, axis=axis)


def kernel(x, meta_tokens, norm1_g, w_in, fox_f_bias, fox_q_norm_g, fox_k_norm_g, gdn_conv_w, gdn_a_log, gdn_dt_bias, gdn_norm_g, w_branch_a, w_branch_b, w_out, norm2_g, w_up, ffn_conv_w, w_down, loss_target, m_meta_tokens, m_norm1_g, m_w_in, m_fox_f_bias, m_fox_q_norm_g, m_fox_k_norm_g, m_gdn_conv_w, m_gdn_a_log, m_gdn_dt_bias, m_gdn_norm_g, m_w_branch_a, m_w_branch_b, m_w_out, m_norm2_g, m_w_up, m_ffn_conv_w, m_w_down, v_meta_tokens, v_norm1_g, v_w_in, v_fox_f_bias, v_fox_q_norm_g, v_fox_k_norm_g, v_gdn_conv_w, v_gdn_a_log, v_gdn_dt_bias, v_gdn_norm_g, v_w_branch_a, v_w_branch_b, v_w_out, v_norm2_g, v_w_up, v_ffn_conv_w, v_w_down):
    w_loc = dict(zip(WEIGHTS, (meta_tokens, norm1_g, w_in, fox_f_bias, fox_q_norm_g, fox_k_norm_g, gdn_conv_w,
                               gdn_a_log, gdn_dt_bias, gdn_norm_g, w_branch_a, w_branch_b, w_out, norm2_g, w_up,
                               ffn_conv_w, w_down)))
    m_loc = dict(zip(WEIGHTS, (m_meta_tokens, m_norm1_g, m_w_in, m_fox_f_bias, m_fox_q_norm_g, m_fox_k_norm_g,
                               m_gdn_conv_w, m_gdn_a_log, m_gdn_dt_bias, m_gdn_norm_g, m_w_branch_a, m_w_branch_b,
                               m_w_out, m_norm2_g, m_w_up, m_ffn_conv_w, m_w_down)))
    v_loc = dict(zip(WEIGHTS, (v_meta_tokens, v_norm1_g, v_w_in, v_fox_f_bias, v_fox_q_norm_g, v_fox_k_norm_g,
                               v_gdn_conv_w, v_gdn_a_log, v_gdn_dt_bias, v_gdn_norm_g, v_w_branch_a, v_w_branch_b,
                               v_w_out, v_norm2_g, v_w_up, v_ffn_conv_w, v_w_down)))
    c = lax.axis_index("c")

    full = {n: w_loc[n] for n in REPLICATED}
    full.update(_gather_full(w_loc, MATMUL_WEIGHTS, BF16, COMM_ROW_ALIGN))
    full.update(_gather_full(w_loc, SMALL_SHARDED, F32, COMM_ROW_ALIGN_SMALL))
    full["w_in"] = pad_w_in(full["w_in"])

    loss, (g_full, g_x) = jax.value_and_grad(local_loss, argnums=(0, 1))(full, x[0], loss_target[0])
    g_full = dict(g_full)
    g_full["w_in"] = unpad_w_in(g_full["w_in"])

    sharded = MATMUL_WEIGHTS + SMALL_SHARDED
    g4 = jnp.stack([_pack([_shard_of(g_full[n], n, j) for n in sharded], F32, COMM_ROW_ALIGN)
                    for j in range(N_CHIPS)])
    pair_sum = add_own_half(g4, sibling_swap_halves(g4), c)
    g_shard = sibling_join(sum_slots(chip_scatter(pair_sum)))
    grads = dict(zip(sharded, _unpack(g_shard, [w_loc[n].shape for n in sharded])))
    g_rep = sum_slots(all_devices_gather(_pack([g_full[n] for n in REPLICATED], F32, 8)))
    grads.update(zip(REPLICATED, _unpack(g_rep, [w_loc[n].shape for n in REPLICATED])))

    loss = lax.psum(loss, ("x", "y", "c"))
    upd = {n: adamw(w_loc[n], grads[n], m_loc[n], v_loc[n]) for n in WEIGHTS}
    return (loss, g_x[None], *[grads[n] for n in WEIGHTS], *[upd[n][0] for n in WEIGHTS],
            *[upd[n][1] for n in WEIGHTS], *[upd[n][2] for n in WEIGHTS])
```

```python
import functools

import jax
import jax.numpy as jnp
from jax import lax
from jax.experimental import pallas as pl
from jax.experimental.pallas import tpu as pltpu

F32 = jnp.float32
BF16 = jnp.bfloat16
HI = lax.Precision.HIGHEST
MESH = pl.DeviceIdType.MESH

D_MODEL = 1024
N_META = 16
EPS = 1e-6
FOX_HEADS, FOX_HD = 8, 64
FOX_W = FOX_HEADS * FOX_HD
GDN_HEADS, GDN_HD, GDN_CHUNK, GDN_CONV = 8, 128, 64, 4
GDN_W = GDN_HEADS * GDN_HD
D_FF = 2816
FFN_CONV = 3
D_IN = 7704
D_IN_PAD = 8192
IN_SEGS = ((0, 1536, 0), (1536, 8, 1536), (1544, 3072, 1664), (4616, 16, 4736), (4632, 1024, 4864), (5656, 2048, 5888))
ROW_ALIGN = 256
ATT_BLK = 256
VMEM_LIMIT = 48 * 1024 * 1024
LANE = 128

ADAM_LR, ADAM_B1, ADAM_B2, ADAM_EPS, ADAM_WD, ADAM_STEP = 0.001, 0.9, 0.999, 1e-08, 0.01, 10


def _call(body, **kw):
    return pl.pallas_call(body, **kw)


def _tile(n, target, mult):
    best, t = None, mult
    while t <= min(n, target):
        if n % t == 0:
            best = t
        t += mult
    assert best is not None, (n, target, mult)
    return best


def _cparams(sem):
    return pltpu.CompilerParams(dimension_semantics=sem, vmem_limit_bytes=VMEM_LIMIT)


def _raw_dot(a, b, ca, cb, precise):
    dims = (((ca,), (cb,)), ((), ()))
    a_hi, b_hi = a.astype(BF16), b.astype(BF16)
    out = lax.dot_general(a_hi, b_hi, dims, preferred_element_type=F32)
    if precise:
        a_lo = (a - a_hi.astype(F32)).astype(BF16)
        b_lo = (b - b_hi.astype(F32)).astype(BF16)
        out = out + (lax.dot_general(a_hi, b_lo, dims, preferred_element_type=F32)
                     + lax.dot_general(a_lo, b_hi, dims, preferred_element_type=F32))
    return out


def _make_dot(ca, cb, precise):
    @jax.custom_vjp
    def f(a, b):
        return _raw_dot(a, b, ca, cb, precise)

    def fwd(a, b):
        return f(a, b), (a, b)

    def bwd(res, ct):
        a, b = res
        if ca == 1:
            da = _raw_dot(ct, b, 1, 1 if cb == 0 else 0, precise)
        else:
            da = _raw_dot(b, ct, 1 if cb == 0 else 0, 1, precise)
        if cb == 0:
            db = _raw_dot(a, ct, 0 if ca == 1 else 1, 0, precise)
        else:
            db = _raw_dot(ct, a, 0, 0 if ca == 1 else 1, precise)
        return da, db

    f.defvjp(fwd, bwd)
    return f


_DOTS = {(ca, cb, p): _make_dot(ca, cb, p) for ca in (0, 1) for cb in (0, 1) for p in (False, True)}


def _dot(a, b, ca=1, cb=0, precise=False):
    return _DOTS[(ca, cb, precise)](a, b)


def _mm_call(a, b, name, ta=False, tb=False, out_dtype=F32):
    k, m = a.shape if ta else a.shape[::-1]
    n, kb = b.shape if tb else b.shape[::-1]
    assert k == kb, (a.shape, b.shape)
    tm = _tile(m, 768, LANE if ta else 16)
    tn = _tile(n, 1408, LANE)
    tk = _tile(k, 1408, LANE)
    nk = k // tk
    dims = (((0 if ta else 1,), (1 if tb else 0,)), ((), ()))

    def body(a_ref, b_ref, o_ref, *scratch):
        part = lax.dot_general(a_ref[...], b_ref[...], dims, preferred_element_type=F32)
        if nk == 1:
            o_ref[...] = part.astype(o_ref.dtype)
            return
        acc_ref = scratch[0]
        kk = pl.program_id(2)

        @pl.when(kk == 0)
        def _():
            acc_ref[...] = part

        @pl.when(kk > 0)
        def _():
            acc_ref[...] += part

        @pl.when(kk == nk - 1)
        def _():
            o_ref[...] = acc_ref[...].astype(o_ref.dtype)

    return _call(
        body, name=name, grid=(m // tm, n // tn, nk),
        in_specs=[pl.BlockSpec((tk, tm), lambda i, j, kk: (kk, i)) if ta else
                  pl.BlockSpec((tm, tk), lambda i, j, kk: (i, kk)),
                  pl.BlockSpec((tn, tk), lambda i, j, kk: (j, kk)) if tb else
                  pl.BlockSpec((tk, tn), lambda i, j, kk: (kk, j))],
        out_specs=pl.BlockSpec((tm, tn), lambda i, j, kk: (i, j)),
        out_shape=jax.ShapeDtypeStruct((m, n), out_dtype),
        scratch_shapes=[pltpu.VMEM((tm, tn), F32)] if nk > 1 else [],
        compiler_params=_cparams(("parallel", "parallel", "arbitrary")),
    )(a, b)


def _make_mm(out_dtype):
    @jax.custom_vjp
    def op(a, w):
        return _mm_call(a.astype(BF16), w.astype(BF16), "mm_fwd", out_dtype=out_dtype)

    def fwd(a, w):
        a_b, w_b = a.astype(BF16), w.astype(BF16)
        return _mm_call(a_b, w_b, "mm_fwd", out_dtype=out_dtype), (a_b, w_b, jnp.zeros((), a.dtype))

    def bwd(res, ct):
        a_b, w_b, like_a = res
        ct_b = ct.astype(BF16)
        return (_mm_call(ct_b, w_b, "mm_dx", tb=True, out_dtype=like_a.dtype),
                _mm_call(a_b, ct_b, "mm_dw", ta=True))

    op.defvjp(fwd, bwd)
    return op


mm = _make_mm(F32)
mm_bf16 = _make_mm(BF16)


def _rows_specs(rows, tm, ncb, bc):
    specs = []
    for idx, r in enumerate(rows):
        if idx in bc:
            specs.append(pl.BlockSpec((tm, r.shape[1]), lambda i, j: (i, 0)))
        else:
            specs.append(pl.BlockSpec((tm, r.shape[1] // ncb), lambda i, j: (i, j)))
    return specs


def _param_specs(params):
    return [pl.BlockSpec(p.shape, lambda i, j: (0, 0)) for p in params]


def _group_slices(refs, groups, g, whole):
    out = []
    for idx, r in enumerate(refs):
        w = r.shape[1] // groups
        out.append((r[...] if idx in whole else r[:, g * w:(g + 1) * w]).astype(F32))
    return out


def _rows_fwd_call(fns, rows, params, outs, tm, ncb, bc, name, out_dtypes=None):
    r_total = rows[0].shape[0]
    nr, groups = len(rows), len(fns)
    out_dtypes = out_dtypes or [F32] * len(outs)

    def body(*refs):
        pvals = [r[...] for r in refs[nr:nr + len(params)]]
        for g, fn in enumerate(fns):
            res = fn(*_group_slices(refs[:nr], groups, g, bc), *pvals)
            for o_ref, val in zip(refs[nr + len(params):], res):
                w = o_ref.shape[1] // groups
                o_ref[:, g * w:(g + 1) * w] = val.astype(o_ref.dtype)

    return _call(
        body, name=name, grid=(r_total // tm, ncb),
        in_specs=_rows_specs(rows, tm, ncb, bc) + _param_specs(params),
        out_specs=[pl.BlockSpec((tm, w * groups), lambda i, j: (i, j)) for w in outs],
        out_shape=[jax.ShapeDtypeStruct((r_total, w * groups * ncb), dt) for w, dt in zip(outs, out_dtypes)],
        compiler_params=_cparams(("parallel", "parallel")),
    )(*rows, *params)


def _rows_bwd_call(fns, rows, params, cts, tm, ncb, bc, name):
    r_total = rows[0].shape[0]
    nr, npar, nct, groups = len(rows), len(params), len(cts), len(fns)

    def body(*refs):
        i, j = pl.program_id(0), pl.program_id(1)
        pvals = [r[...] for r in refs[nr:nr + npar]]
        ct_refs = refs[nr + npar:nr + npar + nct]
        d_refs = refs[nr + npar + nct:]
        shared = {idx: None for idx in list(bc) + list(range(nr, nr + npar))}
        for g, fn in enumerate(fns):
            _, vjp = jax.vjp(lambda *a, fn=fn: tuple(fn(*a)), *_group_slices(refs[:nr], groups, g, bc), *pvals)
            grads = vjp(tuple(_group_slices(ct_refs, groups, g, ())))
            for idx in range(nr + npar):
                if idx in shared:
                    shared[idx] = grads[idx] if shared[idx] is None else shared[idx] + grads[idx]
                else:
                    w = d_refs[idx].shape[1] // groups
                    d_refs[idx][:, g * w:(g + 1) * w] = grads[idx].astype(d_refs[idx].dtype)
        for idx, total in shared.items():
            first = (j == 0) if idx < nr else ((i == 0) & (j == 0))

            @pl.when(first)
            def _(idx=idx):
                d_refs[idx][...] = jnp.zeros_like(d_refs[idx])
            d_refs[idx][...] += total

    ct_specs = [pl.BlockSpec((tm, c.shape[1] // ncb), lambda i, j: (i, j)) for c in cts]
    return _call(
        body, name=name + "_bwd", grid=(r_total // tm, ncb),
        in_specs=_rows_specs(rows, tm, ncb, bc) + _param_specs(params) + ct_specs,
        out_specs=_rows_specs(rows, tm, ncb, bc) + _param_specs(params),
        out_shape=[jax.ShapeDtypeStruct(a.shape, a.dtype) for a in list(rows) + list(params)],
        compiler_params=_cparams(("arbitrary", "arbitrary")),
    )(*rows, *params, *cts)


def rowop(fn, name, outs, tm, ncb=1, bc=(), out_dtypes=None):
    fns = list(fn) if isinstance(fn, (list, tuple)) else [fn]

    @jax.custom_vjp
    def op(rows, params):
        return tuple(_rows_fwd_call(fns, rows, params, outs, tm, ncb, bc, name, out_dtypes))

    def fwd(rows, params):
        return op(rows, params), (rows, params)

    def bwd(res, cts):
        rows, params = res
        d = _rows_bwd_call(fns, rows, params, cts, tm, ncb, bc, name)
        return tuple(d[:len(rows)]), tuple(d[len(rows):])

    op.defvjp(fwd, bwd)
    return op


def _sigmoid(x):
    return 1.0 / (1.0 + jnp.exp(-x))


def _silu(x):
    return x * _sigmoid(x)


def _softplus(x):
    return jnp.maximum(x, 0.0) + jnp.log(1.0 + jnp.exp(-jnp.abs(x)))


def _f_rmsnorm(x, g):
    return (x * lax.rsqrt(jnp.mean(x * x, axis=-1, keepdims=True) + EPS) * g,)


def _f_qnorm(x, g):
    return (x * lax.rsqrt(jnp.mean(x * x, axis=-1, keepdims=True) + EPS) * (g * (FOX_HD ** -0.5)),)


def _f_logsig(x, b):
    return (-_softplus(-(x + b)),)


def _f_gdn_q(x):
    y = _silu(x)
    return (y * lax.rsqrt(jnp.sum(y * y, axis=-1, keepdims=True) + EPS) * (GDN_HD ** -0.5),)


def _f_gdn_k(x):
    y = _silu(x)
    return (y * lax.rsqrt(jnp.sum(y * y, axis=-1, keepdims=True) + EPS),)


def _f_gdn_v(x):
    return (_silu(x),)


def _f_gdn_gates(bl, al, a_log, dt_bias):
    return _sigmoid(bl), -jnp.exp(a_log) * _softplus(al + dt_bias)


def _f_gdn_out(o, z, g):
    return (o * lax.rsqrt(jnp.mean(o * o, axis=-1, keepdims=True) + EPS) * g * _silu(z),)


def _f_merge(g0, g1, ya, yb):
    return (_sigmoid(g0) * ya + _sigmoid(g1) * yb,)


def _f_residual(a, b, keep):
    return ((a + b) * keep,)


def _f_residual_norm(a, b, keep, g):
    r = (a + b) * keep
    return r, _f_rmsnorm(r, g)[0]


def _f_glu(a, b):
    return (_silu(a) * b,)


def _glu_call(up, ct):
    t_total, two_f = up.shape
    f = two_f // 2
    tm = _tile(t_total, 128, 16)
    wc = _tile(f, 1408, LANE)

    def body(*refs):
        up_ref, out_ref = refs[0], refs[-1]
        for c0 in range(0, f, wc):
            a, b = up_ref[:, c0:c0 + wc], up_ref[:, f + c0:f + c0 + wc]
            if ct is None:
                out_ref[:, c0:c0 + wc] = _f_glu(a, b)[0].astype(out_ref.dtype)
            else:
                _, vjp = jax.vjp(_f_glu, a, b)
                da, db = vjp((refs[1][:, c0:c0 + wc].astype(F32),))
                out_ref[:, c0:c0 + wc] = da
                out_ref[:, f + c0:f + c0 + wc] = db

    wide = pl.BlockSpec((tm, two_f), lambda i: (i, 0))
    narrow = pl.BlockSpec((tm, f), lambda i: (i, 0))
    return _call(
        body, name="ffn_glu" if ct is None else "ffn_glu_bwd", grid=(t_total // tm,),
        in_specs=[wide] if ct is None else [wide, narrow], out_specs=narrow if ct is None else wide,
        out_shape=jax.ShapeDtypeStruct((t_total, f), BF16) if ct is None else jax.ShapeDtypeStruct(up.shape, F32),
        compiler_params=_cparams(("parallel",)),
    )(*((up,) if ct is None else (up, ct)))


@jax.custom_vjp
def glu(up):
    return _glu_call(up, None)


glu.defvjp(lambda up: (glu(up), up), lambda up, ct: (_glu_call(up, ct),))


def _shift_down(x, halo, s, row8):
    rx = pltpu.roll(x, s, 0)
    top = jnp.where(row8 < s, pltpu.roll(halo, s, 0), rx[:8])
    return jnp.concatenate([top, rx[8:]], axis=0)


def _shift_up(x, nxt, s, row8):
    tm = x.shape[0]
    rx = pltpu.roll(x, tm - s, 0)
    bot = jnp.where(row8 >= 8 - s, pltpu.roll(nxt, 8 - s, 0), rx[tm - 8:])
    return jnp.concatenate([rx[:tm - 8], bot], axis=0)


def _conv_tiles(r_total, c_total):
    return _tile(r_total, 768, 8), _tile(c_total, 1408, LANE)


def _conv_fwd_call(x, w8, k_taps):
    r_total, c_total = x.shape
    tm, tc = _conv_tiles(r_total, c_total)
    hb = tm // 8

    def body(x_ref, halo_ref, w_ref, y_ref):
        i = pl.program_id(1)
        xt = x_ref[...]
        halo = jnp.where(i > 0, halo_ref[...], 0.0)
        row8 = lax.broadcasted_iota(jnp.int32, (8, tc), 0)
        acc = w_ref[k_taps - 1:k_taps, :] * xt
        for k in range(k_taps - 1):
            acc += w_ref[k:k + 1, :] * _shift_down(xt, halo, k_taps - 1 - k, row8)
        y_ref[...] = acc

    return _call(
        body, name="dwconv_fwd", grid=(c_total // tc, r_total // tm),
        in_specs=[pl.BlockSpec((tm, tc), lambda c, i: (i, c)),
                  pl.BlockSpec((8, tc), lambda c, i: (jnp.maximum(i * hb - 1, 0), c)),
                  pl.BlockSpec((8, tc), lambda c, i: (0, c))],
        out_specs=pl.BlockSpec((tm, tc), lambda c, i: (i, c)),
        out_shape=jax.ShapeDtypeStruct(x.shape, F32),
        compiler_params=_cparams(("parallel", "parallel")),
    )(x, x, w8)


def _conv_bwd_call(x, w8, dy, k_taps):
    r_total, c_total = x.shape
    tm, tc = _conv_tiles(r_total, c_total)
    hb = tm // 8
    n_i = r_total // tm

    def body(x_ref, halo_ref, w_ref, dy_ref, nxt_ref, dx_ref, dw_ref):
        i = pl.program_id(1)
        xt, dyt = x_ref[...], dy_ref[...]
        halo = jnp.where(i > 0, halo_ref[...], 0.0)
        nxt = jnp.where(i < n_i - 1, nxt_ref[...], 0.0)
        row8 = lax.broadcasted_iota(jnp.int32, (8, tc), 0)
        dx = w_ref[k_taps - 1:k_taps, :] * dyt
        upd = jnp.where(row8 == k_taps - 1, jnp.sum(dyt * xt, axis=0, keepdims=True), 0.0)
        for k in range(k_taps - 1):
            s = k_taps - 1 - k
            dx += w_ref[k:k + 1, :] * _shift_up(dyt, nxt, s, row8)
            upd = jnp.where(row8 == k, jnp.sum(dyt * _shift_down(xt, halo, s, row8), axis=0, keepdims=True), upd)
        dx_ref[...] = dx

        @pl.when(i == 0)
        def _():
            dw_ref[...] = jnp.zeros_like(dw_ref)

        dw_ref[...] += upd

    return _call(
        body, name="dwconv_bwd", grid=(c_total // tc, n_i),
        in_specs=[pl.BlockSpec((tm, tc), lambda c, i: (i, c)),
                  pl.BlockSpec((8, tc), lambda c, i: (jnp.maximum(i * hb - 1, 0), c)),
                  pl.BlockSpec((8, tc), lambda c, i: (0, c)),
                  pl.BlockSpec((tm, tc), lambda c, i: (i, c)),
                  pl.BlockSpec((8, tc), lambda c, i: (jnp.minimum((i + 1) * hb, r_total // 8 - 1), c))],
        out_specs=[pl.BlockSpec((tm, tc), lambda c, i: (i, c)), pl.BlockSpec((8, tc), lambda c, i: (0, c))],
        out_shape=[jax.ShapeDtypeStruct(x.shape, F32), jax.ShapeDtypeStruct(w8.shape, F32)],
        compiler_params=_cparams(("parallel", "arbitrary")),
    )(x, x, w8, dy, dy)


def make_dwconv(k_taps):
    @jax.custom_vjp
    def op(x, w8):
        return _conv_fwd_call(x, w8, k_taps)

    def fwd(x, w8):
        return op(x, w8), (x, w8)

    def bwd(res, dy):
        x, w8 = res
        dx, dw = _conv_bwd_call(x, w8, dy, k_taps)
        return dx, dw

    op.defvjp(fwd, bwd)
    return op


def _cumsum_call(x, reverse):
    h, t_total = x.shape
    tb = _tile(t_total, 256, LANE)
    nb = t_total // tb

    def body(x_ref, o_ref, carry_ref):
        i = pl.program_id(0)

        @pl.when(i == 0)
        def _():
            carry_ref[...] = jnp.zeros_like(carry_ref)

        r = lax.broadcasted_iota(jnp.int32, (tb, tb), 0)
        c = lax.broadcasted_iota(jnp.int32, (tb, tb), 1)
        tri = jnp.where((r >= c) if reverse else (r <= c), 1.0, 0.0).astype(F32)
        xv = x_ref[...]
        carry = jnp.max(carry_ref[...], axis=1, keepdims=True)
        o_ref[...] = _raw_dot(xv, tri, 1, 0, True) + carry
        carry_ref[...] = jnp.broadcast_to(carry + jnp.sum(xv, axis=1, keepdims=True), carry_ref.shape)

    imap = (lambda i: (0, nb - 1 - i)) if reverse else (lambda i: (0, i))
    return _call(
        body, name="cumsum_rev" if reverse else "cumsum", grid=(nb,),
        in_specs=[pl.BlockSpec((h, tb), imap)], out_specs=pl.BlockSpec((h, tb), imap),
        out_shape=jax.ShapeDtypeStruct(x.shape, F32), scratch_shapes=[pltpu.VMEM((h, LANE), F32)],
        compiler_params=_cparams(("arbitrary",)),
    )(x)


@jax.custom_vjp
def cumsum_lanes(x):
    return _cumsum_call(x, False)


cumsum_lanes.defvjp(lambda x: (cumsum_lanes(x), None), lambda _, ct: (_cumsum_call(ct, True),))


NEG_BIG = -1e30


def _attn_sub_tiles(nb):
    return max(s for s in (3, 2, 1) if nb % s == 0)


EXP_ZERO = -100.0
SMEM = pl.BlockSpec(memory_space=pltpu.SMEM)


def _max_row_norm_sq(x):
    h_total, t_total, hd = x.shape
    tb = _tile(t_total, 2816, 8)

    def body(x_ref, o_ref):
        @pl.when(pl.program_id(1) == 0)
        def _():
            o_ref[...] = jnp.zeros_like(o_ref)

        xv = x_ref[0]
        top = jnp.max(jnp.sum(xv * xv, axis=1, keepdims=True), axis=0, keepdims=True)
        o_ref[0] = jnp.maximum(o_ref[0], top)

    return _call(
        body, name="max_row_norm", grid=(h_total, t_total // tb),
        in_specs=[pl.BlockSpec((1, tb, hd), lambda h, i: (h, i, 0))],
        out_specs=pl.BlockSpec((1, 8, LANE), lambda h, i: (h, 0, 0)),
        out_shape=jax.ShapeDtypeStruct((h_total, 8, LANE), F32),
        compiler_params=_cparams(("parallel", "arbitrary")),
    )(x)


def _attn_skip_tables(q, k, f_row):
    bound = 2.0 * jnp.sqrt(_max_row_norm_sq(q)[:, 0, :1] * _max_row_norm_sq(k)[:, 0, :1])
    return EXP_ZERO - bound, f_row[:, :, 0, 0], f_row[:, :, 0, -1]


def _attn_fwd_call(q, k, v, f_col, f_row, tables):
    h_total, t_total, hd = q.shape
    blk = f_row.shape[-1]
    nb = t_total // blk
    nsub = _attn_sub_tiles(nb)
    tq = nsub * blk

    def body(thr_ref, first_ref, last_ref, q_ref, k_ref, vt_ref, fc_ref, fr_ref, o_ref, lse_ref):
        h = pl.program_id(0)
        i = pl.program_id(1)
        gap_needed = thr_ref[h, 0]
        f_tile = first_ref[h, i * nsub]
        j_start = lax.while_loop(lambda j: (j < i * nsub) & (f_tile - last_ref[h, j] < gap_needed),
                                 lambda j: j + 1, 0)
        r = lax.broadcasted_iota(jnp.int32, (blk, blk), 0)
        c = lax.broadcasted_iota(jnp.int32, (blk, blk), 1)
        qs = [q_ref[0, s * blk:(s + 1) * blk, :].astype(BF16) for s in range(nsub)]
        fqs = [fr_ref[0, i * nsub + s] for s in range(nsub)]

        def load_kv(j):
            off = pl.multiple_of(j * blk, blk)
            return k_ref[0, pl.ds(off, blk), :], vt_ref[0, j], fc_ref[0, pl.ds(off, blk), :]

        def tile(kv, s, carry, diagonal):
            kj, vtj, fk = kv
            m, l, acc = carry
            st = _raw_dot(kj, qs[s], 1, 1, False) + fqs[s] - fk
            if diagonal:
                st = jnp.where(r <= c, st, NEG_BIG)
            m_new = jnp.maximum(m, jnp.max(st, axis=0, keepdims=True))
            p = jnp.exp(st - m_new)
            alpha = jnp.exp(m - m_new)
            l = alpha * l + jnp.sum(p, axis=0, keepdims=True)
            acc = alpha * acc + _raw_dot(vtj, p, 1, 0, False)
            return m_new, l, acc

        def below_diagonal(j, carry):
            kv = load_kv(j)
            return tuple(tile(kv, s, carry[s], False) for s in range(nsub))

        init = tuple((jnp.full((1, blk), NEG_BIG, F32), jnp.zeros((1, blk), F32), jnp.zeros((hd, blk), F32))
                     for _ in range(nsub))
        carry = list(lax.fori_loop(j_start, i * nsub, below_diagonal, init))
        for d in range(nsub):
            kv = load_kv(i * nsub + d)
            for s in range(d, nsub):
                carry[s] = tile(kv, s, carry[s], s == d)
        for s, (m, l, acc) in enumerate(carry):
            o_ref[0, :, s * blk:(s + 1) * blk] = acc / l
            lse_ref[0, s] = m + jnp.log(l)

    vt = v.reshape(h_total, nb, blk, hd).transpose(0, 1, 3, 2).astype(BF16)
    return _call(
        body, name="fox_fwd", grid=(h_total, nb // nsub),
        in_specs=[SMEM, SMEM, SMEM,
                  pl.BlockSpec((1, tq, hd), lambda h, i: (h, i, 0)),
                  pl.BlockSpec((1, t_total, hd), lambda h, i: (h, 0, 0)),
                  pl.BlockSpec((1, nb, hd, blk), lambda h, i: (h, 0, 0, 0)),
                  pl.BlockSpec((1, t_total, 1), lambda h, i: (h, 0, 0)),
                  pl.BlockSpec((1, nb, 1, blk), lambda h, i: (h, 0, 0, 0))],
        out_specs=[pl.BlockSpec((1, hd, tq), lambda h, i: (h, 0, i)),
                   pl.BlockSpec((1, nsub, 1, blk), lambda h, i: (h, i, 0, 0))],
        out_shape=[jax.ShapeDtypeStruct((h_total, hd, t_total), F32), jax.ShapeDtypeStruct(f_row.shape, F32)],
        compiler_params=_cparams(("parallel", "parallel")),
    )(*tables, q, k.astype(BF16), vt, f_col, f_row)


def _attn_bwd_call(q, k, v, f_col, f_row, tables, lse_row, delta_row, do_blk):
    h_total, t_total, hd = q.shape
    blk = f_row.shape[-1]
    nb = t_total // blk
    nsub = _attn_sub_tiles(nb)
    tkv = nsub * blk

    def body(thr_ref, first_ref, last_ref, q_ref, do_ref, k_ref, v_ref, fc_ref, fr_ref, lse_ref, dl_ref,
             dq_ref, dk_ref, dv_ref, dfk_ref, dfq_ref):
        h = pl.program_id(0)
        j = pl.program_id(1)
        gap_needed = thr_ref[h, 0]
        f_tile = last_ref[h, j * nsub + nsub - 1]
        i_stop = lax.while_loop(lambda i: (i < nb) & (first_ref[h, jnp.minimum(i, nb - 1)] - f_tile >= gap_needed),
                                lambda i: i + 1, (j + 1) * nsub)

        @pl.when(j == 0)
        def _():
            dq_ref[...] = jnp.zeros_like(dq_ref)
            dfq_ref[...] = jnp.zeros_like(dfq_ref)

        ks = [k_ref[0, s * blk:(s + 1) * blk, :].astype(BF16) for s in range(nsub)]
        vs = [v_ref[0, s * blk:(s + 1) * blk, :].astype(BF16) for s in range(nsub)]
        fks = [fc_ref[0, s * blk:(s + 1) * blk, :] for s in range(nsub)]
        r = lax.broadcasted_iota(jnp.int32, (blk, blk), 0)
        c = lax.broadcasted_iota(jnp.int32, (blk, blk), 1)

        def q_step(i, accs, subs):
            off = pl.multiple_of(i * blk, blk)
            qi = q_ref[0, pl.ds(off, blk), :]
            doi = do_ref[0, i]
            fq, lse, dl = fr_ref[0, i], lse_ref[0, i], dl_ref[0, i]
            accs = list(accs)
            dq_i, dfq_i = None, None
            for s, diagonal in subs:
                dk, dv, dfk = accs[s]
                st = _raw_dot(ks[s], qi, 1, 1, False) + fq - fks[s] - lse
                if diagonal:
                    st = jnp.where(r <= c, st, NEG_BIG)
                pt = jnp.exp(st)
                dv = dv + _raw_dot(pt, doi, 1, 1, False)
                dst = pt * (_raw_dot(vs[s], doi, 1, 0, False) - dl)
                dk = dk + _raw_dot(dst, qi, 1, 0, False)
                dfk = dfk - jnp.sum(dst, axis=1, keepdims=True)
                accs[s] = (dk, dv, dfk)
                dq_s = _raw_dot(dst, ks[s], 0, 0, False)
                dfq_s = jnp.sum(dst, axis=0, keepdims=True)
                dq_i = dq_s if dq_i is None else dq_i + dq_s
                dfq_i = dfq_s if dfq_i is None else dfq_i + dfq_s
            dfq_ref[0, i] += dfq_i
            dq_ref[0, pl.ds(off, blk), :] += dq_i
            return tuple(accs)

        accs = tuple((jnp.zeros((blk, hd), F32), jnp.zeros((blk, hd), F32), jnp.zeros((blk, 1), F32))
                     for _ in range(nsub))
        for d in range(nsub):
            accs = q_step(j * nsub + d, accs, [(s, s == d) for s in range(d + 1)])
        accs = lax.fori_loop((j + 1) * nsub, i_stop,
                             lambda i, a: q_step(i, a, [(s, False) for s in range(nsub)]), accs)
        for s, (dk, dv, dfk) in enumerate(accs):
            dk_ref[0, s * blk:(s + 1) * blk, :] = dk
            dv_ref[0, s * blk:(s + 1) * blk, :] = dv
            dfk_ref[0, s * blk:(s + 1) * blk, :] = dfk

    full = pl.BlockSpec((1, t_total, hd), lambda h, j: (h, 0, 0))
    tile = pl.BlockSpec((1, tkv, hd), lambda h, j: (h, j, 0))
    col = pl.BlockSpec((1, tkv, 1), lambda h, j: (h, j, 0))
    rows = pl.BlockSpec((1, nb, 1, blk), lambda h, j: (h, 0, 0, 0))
    do_blocks = pl.BlockSpec((1, nb, hd, blk), lambda h, j: (h, 0, 0, 0))
    return _call(
        body, name="fox_bwd", grid=(h_total, nb // nsub),
        in_specs=[SMEM, SMEM, SMEM, full, do_blocks, tile, tile, col, rows, rows, rows],
        out_specs=[full, tile, tile, col, rows],
        out_shape=[jax.ShapeDtypeStruct(q.shape, F32), jax.ShapeDtypeStruct(q.shape, F32),
                   jax.ShapeDtypeStruct(q.shape, F32), jax.ShapeDtypeStruct(f_col.shape, F32),
                   jax.ShapeDtypeStruct(f_row.shape, F32)],
        compiler_params=_cparams(("parallel", "arbitrary")),
    )(*tables, q.astype(BF16), do_blk, k, v, f_col, f_row, lse_row, delta_row)


def _attn_delta_call(do_t, o_t):
    h_total, hd, t_total = o_t.shape
    tb = _tile(t_total, 2816, LANE)

    def body(do_ref, o_ref, d_ref):
        d_ref[0] = jnp.sum(do_ref[0] * o_ref[0], axis=0, keepdims=True)

    spec = pl.BlockSpec((1, hd, tb), lambda h, i: (h, 0, i))
    return _call(
        body, name="fox_delta", grid=(h_total, t_total // tb), in_specs=[spec, spec],
        out_specs=pl.BlockSpec((1, 1, tb), lambda h, i: (h, 0, i)),
        out_shape=jax.ShapeDtypeStruct((h_total, 1, t_total), F32),
        compiler_params=_cparams(("parallel", "parallel")),
    )(do_t, o_t)


@jax.custom_vjp
def fox_attention(q, k, v, f_col, f_row):
    return _attn_fwd_call(q, k, v, f_col, f_row, _attn_skip_tables(q, k, f_row))[0]


def _fox_fwd(q, k, v, f_col, f_row):
    tables = _attn_skip_tables(q, k, f_row)
    o_t, lse_row = _attn_fwd_call(q, k, v, f_col, f_row, tables)
    return o_t, (q, k, v, f_col, f_row, tables, o_t, lse_row)


def _fox_bwd(res, do_t):
    q, k, v, f_col, f_row, tables, o_t, lse_row = res
    h_total, t_total, hd = q.shape
    nb, blk = f_row.shape[1], f_row.shape[3]
    delta = _attn_delta_call(do_t, o_t).reshape(f_row.shape)
    do_blk = do_t.reshape(h_total, hd, nb, blk).transpose(0, 2, 1, 3).astype(BF16)
    return tuple(_attn_bwd_call(q, k, v, f_col, f_row, tables, lse_row, delta, do_blk))


fox_attention.defvjp(_fox_fwd, _fox_bwd)


def _head_col(blk, h):
    lane = lax.broadcasted_iota(jnp.int32, blk.shape, 1)
    return jnp.sum(jnp.where(lane == h, blk, 0.0), axis=1, keepdims=True)


@jax.custom_vjp
def _cat2(a, b):
    return jnp.concatenate([a, b], axis=1)


_cat2.defvjp(lambda a, b: (_cat2(a, b), a.shape[1]), lambda na, ct: (ct[:, :na], ct[:, na:]))


@jax.custom_vjp
def _split2(x):
    half = x.shape[1] // 2
    return x[:, :half], x[:, half:]


_split2.defvjp(lambda x: (_split2(x), None), lambda _, cts: (jnp.concatenate(cts, axis=1),))


def _neumann_solve(m, b):
    x = b - _raw_dot(m, b, 1, 0, False)
    powers, steps = [m], 1
    while 2 * steps < GDN_CHUNK:
        powers.append(_raw_dot(powers[-1], powers[-1], 1, 0, False))
        x = x + _raw_dot(powers[-1], x, 1, 0, False)
        steps *= 2
    return x, powers


@jax.custom_vjp
def _unit_lower_solve(m, b):
    return _neumann_solve(m, b)[0]


def _unit_lower_solve_fwd(m, b):
    x, powers = _neumann_solve(m, b)
    return x, (powers, x)


def _unit_lower_solve_bwd(res, dx):
    powers, x = res
    db = dx - _raw_dot(powers[0], dx, 0, 0, False)
    for p in powers[1:]:
        db = db + _raw_dot(p, db, 0, 0, False)
    return -_raw_dot(db, x, 1, 1, False), db


_unit_lower_solve.defvjp(_unit_lower_solve_fwd, _unit_lower_solve_bwd)


def _gdn_intra(h, q, k, v, b_blk, g_blk):
    n = q.shape[0]
    b, g = _head_col(b_blk, h), _head_col(g_blk, h)
    r = lax.broadcasted_iota(jnp.int32, (n, n), 0)
    c = lax.broadcasted_iota(jnp.int32, (n, n), 1)
    same = (r // GDN_CHUNK) == (c // GDN_CHUNK)
    incl = same & (r >= c)
    g_row = jnp.sum(jnp.where(r == c, g, 0.0), axis=0, keepdims=True)
    big_g = jnp.sum(jnp.where(incl, g_row, 0.0), axis=1, keepdims=True)
    big_g_row = jnp.sum(jnp.where(same & (r <= c), g, 0.0), axis=0, keepdims=True)
    g_tot = jnp.sum(jnp.where(same, g_row, 0.0), axis=1, keepdims=True)
    dec = jnp.where(incl, jnp.exp(jnp.where(incl, big_g - big_g_row, 0.0)), 0.0)
    dec_strict = jnp.where(r > c, dec, 0.0)
    e_g = jnp.exp(big_g)
    kb = k * b
    m = _dot(kb, k, 1, 1) * dec_strict
    rs = lax.broadcasted_iota(jnp.int32, (n, GDN_CHUNK), 0)
    cs = lax.broadcasted_iota(jnp.int32, (n, GDN_CHUNK), 1)
    fold = jnp.where(rs % GDN_CHUNK == cs, 1.0, 0.0).astype(F32)
    aqk = _dot(_dot(q, k, 1, 1) * dec, fold, 1, 0, True)
    u, w = _split2(_unit_lower_solve(m, _cat2(v * b, kb * e_g)))
    lane = lax.broadcasted_iota(jnp.int32, b_blk.shape, 1)
    return u, w, q * e_g, k * jnp.exp(g_tot - big_g), aqk, jnp.where(lane == h, g_tot, 0.0)


def _gdn_rec(h, s, u, w, qg, kd, aqk, gl_blk):
    g_last = jnp.max(_head_col(gl_blk, h), axis=0, keepdims=True)
    big_u = u - _dot(w, s)
    o = _dot(qg, s) + _dot(aqk, big_u)
    s_next = s * jnp.exp(g_last) + _dot(kd, big_u, 0, 0)
    return o, s_next


GDN_TOK_BLK = 256


def _gdn_layout(t_total, rev):
    tb = _tile(t_total, GDN_TOK_BLK, GDN_CHUNK)
    cb, nblk = tb // GDN_CHUNK, t_total // tb
    pos = (lambda i: nblk - 1 - i) if rev else (lambda i: i)
    specs = dict(
        tok=pl.BlockSpec((tb, GDN_W), lambda i: (pos(i), 0)),
        q=pl.BlockSpec((tb, GDN_W), lambda i: (pos(i), 0)),
        k=pl.BlockSpec((tb, GDN_W), lambda i: (pos(i), 1)),
        v=pl.BlockSpec((tb, GDN_W), lambda i: (pos(i), 2)),
        qkv=pl.BlockSpec((tb, 3 * GDN_W), lambda i: (pos(i), 0)),
        gate=pl.BlockSpec((tb, GDN_HEADS), lambda i: (pos(i), 0)),
        aqk=pl.BlockSpec((GDN_HEADS, tb, GDN_CHUNK), lambda i: (0, pos(i), 0)),
        state=pl.BlockSpec((GDN_HEADS, cb, GDN_HD, GDN_HD), lambda i: (0, pos(i), 0, 0)))
    return cb, nblk, specs


def _gdn_shapes(t_total):
    n_chunks = t_total // GDN_CHUNK
    return dict(tok=jax.ShapeDtypeStruct((t_total, GDN_W), F32),
                gate=jax.ShapeDtypeStruct((t_total, GDN_HEADS), F32),
                aqk=jax.ShapeDtypeStruct((GDN_HEADS, t_total, GDN_CHUNK), F32),
                state=jax.ShapeDtypeStruct((GDN_HEADS, n_chunks, GDN_HD, GDN_HD), F32))


def _chunk_rows(ci):
    return pl.ds(pl.multiple_of(ci * GDN_CHUNK, GDN_CHUNK), GDN_CHUNK)


def _head_cols(h):
    return pl.ds(h * GDN_HD, GDN_HD)


def _gdn_intra_fwd_call(qkv, b, g):
    cb, nblk, sp = _gdn_layout(qkv.shape[0], False)
    sh = _gdn_shapes(qkv.shape[0])

    def body(q_ref, k_ref, v_ref, b_ref, g_ref, u_ref, w_ref, qg_ref, kd_ref, aqk_ref, gl_ref):
        b_blk, g_blk = b_ref[...], g_ref[...]
        gl = jnp.zeros(b_blk.shape, F32)
        for h in range(GDN_HEADS):
            cols = _head_cols(h)
            u, w, qg, kd, aqk, gl_h = _gdn_intra(h, q_ref[:, cols], k_ref[:, cols], v_ref[:, cols], b_blk, g_blk)
            u_ref[:, cols] = u
            w_ref[:, cols] = w
            qg_ref[:, cols] = qg
            kd_ref[:, cols] = kd
            aqk_ref[h] = aqk
            gl = gl + gl_h
        gl_ref[...] = gl

    return _call(
        body, name="gdn_intra_fwd", grid=(nblk,),
        in_specs=[sp["q"], sp["k"], sp["v"]] + [sp["gate"]] * 2,
        out_specs=[sp["tok"]] * 4 + [sp["aqk"], sp["gate"]],
        out_shape=[sh["tok"]] * 4 + [sh["aqk"], sh["gate"]],
        compiler_params=_cparams(("parallel",)),
    )(qkv, qkv, qkv, b, g)


def _gdn_intra_bwd_call(qkv, b, g, du, dw, dqg, dkd, daqk, dgl):
    cb, nblk, sp = _gdn_layout(qkv.shape[0], False)
    sh = _gdn_shapes(qkv.shape[0])

    def body(q_ref, k_ref, v_ref, b_ref, g_ref, du_ref, dw_ref, dqg_ref, dkd_ref, daqk_ref, dgl_ref,
             dqkv_ref, db_ref, dg_ref):
        b_blk, g_blk, dgl = b_ref[...], g_ref[...], dgl_ref[...]
        db = jnp.zeros(b_blk.shape, F32)
        dg = jnp.zeros(b_blk.shape, F32)
        for h in range(GDN_HEADS):
            cols = _head_cols(h)
            _, vjp = jax.vjp(functools.partial(_gdn_intra, h), q_ref[:, cols], k_ref[:, cols], v_ref[:, cols],
                             b_blk, g_blk)
            dq, dk, dv, db_h, dg_h = vjp((du_ref[:, cols], dw_ref[:, cols], dqg_ref[:, cols], dkd_ref[:, cols],
                                          daqk_ref[h], dgl))
            dqkv_ref[:, pl.ds(h * GDN_HD, GDN_HD)] = dq
            dqkv_ref[:, pl.ds(GDN_W + h * GDN_HD, GDN_HD)] = dk
            dqkv_ref[:, pl.ds(2 * GDN_W + h * GDN_HD, GDN_HD)] = dv
            db = db + db_h
            dg = dg + dg_h
        db_ref[...] = db
        dg_ref[...] = dg

    return _call(
        body, name="gdn_intra_bwd", grid=(nblk,),
        in_specs=[sp["q"], sp["k"], sp["v"]] + [sp["gate"]] * 2 + [sp["tok"]] * 4 + [sp["aqk"], sp["gate"]],
        out_specs=[sp["qkv"]] + [sp["gate"]] * 2,
        out_shape=[jax.ShapeDtypeStruct(qkv.shape, F32)] + [sh["gate"]] * 2,
        compiler_params=_cparams(("parallel",)),
    )(qkv, qkv, qkv, b, g, du, dw, dqg, dkd, daqk, dgl)


def _gdn_rec_fwd_call(u, w, qg, kd, aqk, gl):
    cb, nblk, sp = _gdn_layout(u.shape[0], False)
    sh = _gdn_shapes(u.shape[0])

    def body(u_ref, w_ref, qg_ref, kd_ref, aqk_ref, gl_ref, o_ref, s_all_ref, s_ref):
        @pl.when(pl.program_id(0) == 0)
        def _():
            s_ref[...] = jnp.zeros_like(s_ref)

        def chunk(ci, carry):
            rows = _chunk_rows(ci)
            gl_row = gl_ref[rows, :]
            states = [s_ref[h] for h in range(GDN_HEADS)]
            res = [_gdn_rec(h, states[h], u_ref[rows, _head_cols(h)], w_ref[rows, _head_cols(h)],
                            qg_ref[rows, _head_cols(h)], kd_ref[rows, _head_cols(h)], aqk_ref[h, rows, :], gl_row)
                   for h in range(GDN_HEADS)]
            for h, (o, s_next) in enumerate(res):
                s_all_ref[h, ci] = states[h]
                o_ref[rows, _head_cols(h)] = o
                s_ref[h] = s_next
            return carry

        lax.fori_loop(0, cb, chunk, 0)

    return _call(
        body, name="gdn_rec_fwd", grid=(nblk,),
        in_specs=[sp["tok"]] * 4 + [sp["aqk"], sp["gate"]],
        out_specs=[sp["tok"], sp["state"]], out_shape=[sh["tok"], sh["state"]],
        scratch_shapes=[pltpu.VMEM((GDN_HEADS, GDN_HD, GDN_HD), F32)],
        compiler_params=_cparams(("arbitrary",)),
    )(u, w, qg, kd, aqk, gl)


def _gdn_rec_bwd_call(u, w, qg, kd, aqk, gl, s_all, do):
    cb, nblk, sp = _gdn_layout(u.shape[0], True)
    sh = _gdn_shapes(u.shape[0])

    def body(u_ref, w_ref, qg_ref, kd_ref, aqk_ref, gl_ref, s_all_ref, do_ref,
             du_ref, dw_ref, dqg_ref, dkd_ref, daqk_ref, dgl_ref, ds_ref):
        @pl.when(pl.program_id(0) == 0)
        def _():
            ds_ref[...] = jnp.zeros_like(ds_ref)

        def chunk(step, carry):
            ci = cb - 1 - step
            rows = _chunk_rows(ci)
            gl_row = gl_ref[rows, :]
            res = []
            for h in range(GDN_HEADS):
                cols = _head_cols(h)
                _, vjp = jax.vjp(functools.partial(_gdn_rec, h), s_all_ref[h, ci], u_ref[rows, cols],
                                 w_ref[rows, cols], qg_ref[rows, cols], kd_ref[rows, cols], aqk_ref[h, rows, :],
                                 gl_row)
                res.append(vjp((do_ref[rows, cols], ds_ref[h])))
            dgl = jnp.zeros((GDN_CHUNK, GDN_HEADS), F32)
            for h, (ds, du, dw, dqg, dkd, daqk, dgl_h) in enumerate(res):
                cols = _head_cols(h)
                ds_ref[h] = ds
                du_ref[rows, cols] = du
                dw_ref[rows, cols] = dw
                dqg_ref[rows, cols] = dqg
                dkd_ref[rows, cols] = dkd
                daqk_ref[h, rows, :] = daqk
                dgl = dgl + dgl_h
            dgl_ref[rows, :] = dgl
            return carry

        lax.fori_loop(0, cb, chunk, 0)

    return _call(
        body, name="gdn_rec_bwd", grid=(nblk,),
        in_specs=[sp["tok"]] * 4 + [sp["aqk"], sp["gate"], sp["state"], sp["tok"]],
        out_specs=[sp["tok"]] * 4 + [sp["aqk"], sp["gate"]],
        out_shape=[sh["tok"]] * 4 + [sh["aqk"], sh["gate"]],
        scratch_shapes=[pltpu.VMEM((GDN_HEADS, GDN_HD, GDN_HD), F32)],
        compiler_params=_cparams(("arbitrary",)),
    )(u, w, qg, kd, aqk, gl, s_all, do)


@jax.custom_vjp
def gdn_intra(qkv, b, g):
    return tuple(_gdn_intra_fwd_call(qkv, b, g))


gdn_intra.defvjp(lambda *a: (gdn_intra(*a), a), lambda res, cts: tuple(_gdn_intra_bwd_call(*res, *cts)))


@jax.custom_vjp
def gdn_rec(u, w, qg, kd, aqk, gl):
    return _gdn_rec_fwd_call(u, w, qg, kd, aqk, gl)[0]


def _gdn_rec_fwd(*a):
    o, s_all = _gdn_rec_fwd_call(*a)
    return o, a + (s_all,)


gdn_rec.defvjp(_gdn_rec_fwd, lambda res, do: tuple(_gdn_rec_bwd_call(*res, do)))


def gated_delta(qkv, b, g):
    return gdn_rec(*gdn_intra(qkv, b, g))


def _loss_call(y, tgt, first, last):
    r_total, d = y.shape
    tm = _tile(r_total, 256, 8)

    def body(y_ref, t_ref, loss_ref, dy_ref):
        i = pl.program_id(0)

        @pl.when(i == 0)
        def _():
            loss_ref[...] = jnp.zeros_like(loss_ref)

        row = lax.broadcasted_iota(jnp.int32, (tm, d), 0) + i * tm
        err = jnp.where((row >= first) & (row < last), y_ref[...] - t_ref[...], 0.0)
        dy_ref[...] = err * (1.0 / d)
        part = jnp.sum(jnp.sum(err * err, axis=1, keepdims=True), axis=0, keepdims=True) * (0.5 / d)
        loss_ref[...] += jnp.broadcast_to(part, loss_ref.shape)

    return _call(
        body, name="loss_head", grid=(r_total // tm,),
        in_specs=[pl.BlockSpec((tm, d), lambda i: (i, 0))] * 2,
        out_specs=[pl.BlockSpec((8, LANE), lambda i: (0, 0)), pl.BlockSpec((tm, d), lambda i: (i, 0))],
        out_shape=[jax.ShapeDtypeStruct((8, LANE), F32), jax.ShapeDtypeStruct(y.shape, F32)],
        compiler_params=_cparams(("arbitrary",)),
    )(y, tgt)


def make_loss(first, last):
    @jax.custom_vjp
    def op(y, tgt):
        return _loss_call(y, tgt, first, last)[0][0, 0]

    def fwd(y, tgt):
        loss, dy = _loss_call(y, tgt, first, last)
        return loss[0, 0], (dy,)

    def bwd(res, ct):
        return res[0] * ct, jnp.zeros_like(res[0])

    op.defvjp(fwd, bwd)
    return op


def _pad_rows8(w):
    return jnp.concatenate([w, jnp.zeros((8 - w.shape[0], w.shape[1]), w.dtype)], axis=0)


def local_loss(wts, x, tgt):
    seq = x.shape[0]
    n_tok = N_META + seq
    t_pad = -(-n_tok // ROW_ALIGN) * ROW_ALIGN
    depth = wts["norm1_g"].shape[0]
    blk = _tile(t_pad, ATT_BLK, LANE)
    nb = t_pad // blk
    tm = _tile(t_pad, 256, 8)

    rms = rowop(_f_rmsnorm, "rmsnorm", (D_MODEL,), tm, out_dtypes=[BF16])
    qnorm = rowop(_f_qnorm, "fox_q_norm", (FOX_HD,), _tile(FOX_HEADS * t_pad, 2048, 8))
    knorm = rowop(_f_rmsnorm, "fox_k_norm", (FOX_HD,), _tile(FOX_HEADS * t_pad, 2048, 8))
    logsig = rowop(_f_logsig, "fox_log_forget", (FOX_HEADS,), tm)
    gdn_act = rowop([_f_gdn_q] * GDN_HEADS + [_f_gdn_k] * GDN_HEADS + [_f_gdn_v] * GDN_HEADS, "gdn_qkv_act",
                    (GDN_HD,), tm)
    gates = rowop(_f_gdn_gates, "gdn_gates", (GDN_HEADS, GDN_HEADS), tm)
    gdn_out = rowop([_f_gdn_out] * GDN_HEADS, "gdn_out_norm", (GDN_HD,), tm, out_dtypes=[BF16])
    merge = rowop(_f_merge, "branch_merge", (D_MODEL,), tm, out_dtypes=[BF16])
    residual = rowop(_f_residual, "residual_add", (D_MODEL,), tm, bc=(2,))
    residual_norm = rowop(_f_residual_norm, "residual_add_norm", (D_MODEL, D_MODEL), tm, bc=(2,),
                          out_dtypes=[F32, BF16])
    keep = (jnp.arange(t_pad)[:, None] < n_tok).astype(F32)
    conv4 = make_dwconv(GDN_CONV)
    conv3 = make_dwconv(FFN_CONV)
    loss_op = make_loss(N_META, n_tok)

    zeros = jnp.zeros((t_pad - n_tok, D_MODEL), F32)
    h_res = jnp.concatenate([wts["meta_tokens"], x, zeros], axis=0)
    tgt_rows = jnp.concatenate([jnp.zeros((N_META, D_MODEL), F32), tgt, zeros], axis=0)

    def heads(a):
        return a.reshape(t_pad, FOX_HEADS, FOX_HD).transpose(1, 0, 2).reshape(FOX_HEADS * t_pad, FOX_HD)

    h = rms((h_res,), (wts["norm1_g"][0][None],))[0]
    for l in range(depth):
        proj = mm(h, wts["w_in"][l])
        qn = qnorm((heads(proj[:, 0:512]),), (wts["fox_q_norm_g"][l][None],))[0]
        kn = knorm((heads(proj[:, 512:1024]),), (wts["fox_k_norm_g"][l][None],))[0]
        vh = heads(proj[:, 1024:1536])
        log_f = logsig((proj[:, 1536:1544],), (wts["fox_f_bias"][l][None],))[0]
        f_cum = cumsum_lanes(log_f.T)
        o_a = fox_attention(qn.reshape(FOX_HEADS, t_pad, FOX_HD), kn.reshape(FOX_HEADS, t_pad, FOX_HD),
                            vh.reshape(FOX_HEADS, t_pad, FOX_HD), f_cum[:, :, None],
                            f_cum.reshape(FOX_HEADS, nb, 1, blk))
        y_a = mm_bf16(o_a.transpose(2, 0, 1).reshape(t_pad, FOX_W).astype(BF16), wts["w_branch_a"][l])
        cv = conv4(proj[:, 1664:4736], _pad_rows8(wts["gdn_conv_w"][l]))
        qkv = gdn_act((cv,), ())[0]
        beta, gdec = gates((proj[:, 4736:4744], proj[:, 4744:4752]),
                           (wts["gdn_a_log"][l][None], wts["gdn_dt_bias"][l][None]))
        o_b = gated_delta(qkv, beta, gdec)
        o_b = gdn_out((o_b, proj[:, 4864:5888]), (wts["gdn_norm_g"][l][None],))[0]
        y_b = mm_bf16(o_b, wts["w_branch_b"][l])
        mixed = merge((proj[:, 5888:6912], proj[:, 6912:7936], y_a, y_b), ())[0]
        h_res, h = residual_norm((h_res, mm(mixed, wts["w_out"][l]), keep), (wts["norm2_g"][l][None],))
        up = conv3(mm(h, wts["w_up"][l]), _pad_rows8(wts["ffn_conv_w"][l]))
        act = glu(up)
        down = mm(act, wts["w_down"][l])
        if l + 1 < depth:
            h_res, h = residual_norm((h_res, down, keep), (wts["norm1_g"][l + 1][None],))
        else:
            h_res = residual((h_res, down, keep), ())[0]
    return loss_op(h_res, tgt_rows)


def pad_w_in(w):
    parts, pos = [], 0
    for src, width, dst in IN_SEGS:
        if dst > pos:
            parts.append(jnp.zeros(w.shape[:-1] + (dst - pos,), w.dtype))
        parts.append(w[..., src:src + width])
        pos = dst + width
    parts.append(jnp.zeros(w.shape[:-1] + (D_IN_PAD - pos,), w.dtype))
    return jnp.concatenate(parts, axis=-1)


def unpad_w_in(w):
    return jnp.concatenate([w[..., dst:dst + width] for _, width, dst in IN_SEGS], axis=-1)


ANY = pl.BlockSpec(memory_space=pl.ANY)
N_CHIPS = 4
N_DEV = 8
COMM_COLS = 1024
COMM_ROW_ALIGN = 512
COMM_ROW_ALIGN_SMALL = 32


def _place():
    return lax.axis_index("x"), lax.axis_index("y"), lax.axis_index("c")


def _other_chips(x, y):
    return [(1 - x, y), (x, 1 - y), (1 - x, 1 - y)]


def _remote(src, dst, send_sem, recv_sem, dev):
    return pltpu.make_async_remote_copy(src_ref=src, dst_ref=dst, send_sem=send_sem, recv_sem=recv_sem,
                                        device_id=dev, device_id_type=MESH)


def chip_all_gather(buf):
    rows, cols = buf.shape
    half = rows // 2

    def body(x_ref, out_ref, send_sems, recv_sems, pass_send, pass_recv):
        x, y, c = _place()
        me = 2 * x + y
        mine, other = pl.ds(c * half, half), pl.ds((1 - c) * half, half)
        sibling = (x, y, 1 - c)
        chips = _other_chips(x, y)
        started = []
        for k, (px, py) in enumerate(chips):
            cp = _remote(x_ref.at[mine], out_ref.at[me, mine], send_sems.at[k], recv_sems.at[k], (px, py, c))
            cp.start()
            started.append(cp)
        for k, (px, py) in enumerate(chips):
            landed = out_ref.at[2 * px + py, mine]
            _remote(landed, landed, send_sems.at[k], recv_sems.at[k], (px, py, c)).wait_recv()
            cp = _remote(landed, landed, pass_send.at[k], pass_recv.at[k], sibling)
            cp.start()
            started.append(cp)
        for k, (px, py) in enumerate(chips):
            passed = out_ref.at[2 * px + py, other]
            _remote(passed, passed, pass_send.at[k], pass_recv.at[k], sibling).wait_recv()
        for cp in started:
            cp.wait_send()

    got = _call(
        body, name="chip_all_gather", in_specs=[ANY], out_specs=ANY,
        out_shape=jax.ShapeDtypeStruct((N_CHIPS, rows, cols), buf.dtype),
        scratch_shapes=[pltpu.SemaphoreType.DMA((3,)), pltpu.SemaphoreType.DMA((3,)),
                        pltpu.SemaphoreType.DMA((3,)), pltpu.SemaphoreType.DMA((3,))],
    )(buf)
    me = 2 * lax.axis_index("x") + lax.axis_index("y")
    return lax.dynamic_update_slice(got, buf[None], (me, 0, 0))


def sibling_swap_halves(g4):
    n, rows, cols = g4.shape
    half = rows // 2

    def body(g_ref, got_ref, send_sem, recv_sem):
        x, y, c = _place()
        cp = _remote(g_ref.at[:, pl.ds((1 - c) * half, half), :], got_ref, send_sem, recv_sem, (x, y, 1 - c))
        cp.start()
        cp.wait()

    return _call(
        body, name="sibling_swap_halves", in_specs=[ANY], out_specs=ANY,
        out_shape=jax.ShapeDtypeStruct((n, half, cols), g4.dtype),
        scratch_shapes=[pltpu.SemaphoreType.DMA, pltpu.SemaphoreType.DMA],
    )(g4)


def add_own_half(g4, got, c):
    n, rows, cols = g4.shape
    half = rows // 2
    tm = _tile(half, 256, 16)
    nt = half // tm

    def body(c_ref, a_ref, b_ref, o_ref):
        o_ref[...] = (a_ref[...] + b_ref[...]).astype(o_ref.dtype)

    return _call(
        body, name="add_own_half",
        grid_spec=pltpu.PrefetchScalarGridSpec(
            num_scalar_prefetch=1, grid=(n, nt),
            in_specs=[pl.BlockSpec((1, tm, cols), lambda j, i, c_ref: (j, c_ref[0] * nt + i, 0)),
                      pl.BlockSpec((1, tm, cols), lambda j, i, c_ref: (j, i, 0))],
            out_specs=pl.BlockSpec((1, tm, cols), lambda j, i, c_ref: (j, i, 0))),
        out_shape=jax.ShapeDtypeStruct(got.shape, BF16),
        compiler_params=_cparams(("parallel", "parallel")),
    )(c.reshape(1).astype(jnp.int32), g4, got)


def chip_scatter(p4):
    n, rows, cols = p4.shape

    def body(p_ref, out_ref, send_sems, recv_sems):
        x, y, c = _place()
        me = 2 * x + y
        chips = _other_chips(x, y)
        started = []
        for k, (px, py) in enumerate(chips):
            cp = _remote(p_ref.at[2 * px + py], out_ref.at[me], send_sems.at[k], recv_sems.at[k], (px, py, c))
            cp.start()
            started.append(cp)
        for k, (px, py) in enumerate(chips):
            landed = out_ref.at[2 * px + py]
            _remote(landed, landed, send_sems.at[k], recv_sems.at[k], (px, py, c)).wait_recv()
        for cp in started:
            cp.wait_send()

    got = _call(
        body, name="chip_scatter", in_specs=[ANY], out_specs=ANY,
        out_shape=jax.ShapeDtypeStruct(p4.shape, p4.dtype),
        scratch_shapes=[pltpu.SemaphoreType.DMA((3,)), pltpu.SemaphoreType.DMA((3,))],
    )(p4)
    me = 2 * lax.axis_index("x") + lax.axis_index("y")
    return lax.dynamic_update_slice(got, lax.dynamic_slice_in_dim(p4, me, 1, axis=0), (me, 0, 0))


def sum_slots(a):
    n, rows, cols = a.shape
    tm = _tile(rows, 256, 16) if rows % 16 == 0 else rows

    def body(a_ref, o_ref):
        acc = a_ref[0].astype(F32)
        for k in range(1, n):
            acc = acc + a_ref[k].astype(F32)
        o_ref[...] = acc

    return _call(
        body, name="sum_slots_%d" % n, grid=(rows // tm,),
        in_specs=[pl.BlockSpec((n, tm, cols), lambda i: (0, i, 0))],
        out_specs=pl.BlockSpec((tm, cols), lambda i: (i, 0)),
        out_shape=jax.ShapeDtypeStruct((rows, cols), F32),
        compiler_params=_cparams(("parallel",)),
    )(a)


def sibling_join(s):
    half, cols = s.shape

    def body(s_ref, got_ref, send_sem, recv_sem):
        x, y, c = _place()
        cp = _remote(s_ref, got_ref, send_sem, recv_sem, (x, y, 1 - c))
        cp.start()
        cp.wait()

    got = _call(
        body, name="sibling_join", in_specs=[ANY], out_specs=ANY,
        out_shape=jax.ShapeDtypeStruct(s.shape, s.dtype),
        scratch_shapes=[pltpu.SemaphoreType.DMA, pltpu.SemaphoreType.DMA],
    )(s)
    c = lax.axis_index("c")
    out = jnp.zeros((2 * half, cols), s.dtype)
    out = lax.dynamic_update_slice(out, s, (c * half, 0))
    return lax.dynamic_update_slice(out, got, ((1 - c) * half, 0))


def all_devices_gather(buf):
    rows, cols = buf.shape

    def body(b_ref, out_ref, send_sems, recv_sems, local_sem):
        x, y, c = _place()
        me = 4 * x + 2 * y + c
        local = pltpu.make_async_copy(b_ref, out_ref.at[me], local_sem)
        local.start()
        peers = [((x + dx) % 2, (y + dy) % 2, (c + dc) % 2)
                 for dx in (0, 1) for dy in (0, 1) for dc in (0, 1) if dx + dy + dc > 0]
        started = []
        for k, peer in enumerate(peers):
            cp = _remote(b_ref, out_ref.at[me], send_sems.at[k], recv_sems.at[k], peer)
            cp.start()
            started.append(cp)
        for k, (px, py, pc) in enumerate(peers):
            landed = out_ref.at[4 * px + 2 * py + pc]
            _remote(landed, landed, send_sems.at[k], recv_sems.at[k], (px, py, pc)).wait_recv()
        for cp in started:
            cp.wait_send()
        local.wait()

    return _call(
        body, name="all_devices_gather", in_specs=[ANY], out_specs=ANY,
        out_shape=jax.ShapeDtypeStruct((N_DEV, rows, cols), buf.dtype),
        scratch_shapes=[pltpu.SemaphoreType.DMA((7,)), pltpu.SemaphoreType.DMA((7,)), pltpu.SemaphoreType.DMA],
    )(buf)


def adamw(w, g, m, v):
    shape = w.shape
    w2, g2, m2, v2 = [a.reshape(-1, shape[-1]) for a in (w, g, m, v)]
    rows, cols = w2.shape
    tm = _tile(rows, 256, 8) if rows % 8 == 0 else rows

    def body(w_ref, g_ref, m_ref, v_ref, d_ref, nm_ref, nv_ref):
        gv = g_ref[...]
        nm = ADAM_B1 * m_ref[...] + (1.0 - ADAM_B1) * gv
        nv = ADAM_B2 * v_ref[...] + (1.0 - ADAM_B2) * (gv * gv)
        m_hat = nm / (1.0 - ADAM_B1 ** ADAM_STEP)
        v_hat = nv / (1.0 - ADAM_B2 ** ADAM_STEP)
        d_ref[...] = -ADAM_LR * (m_hat / (jnp.sqrt(v_hat) + ADAM_EPS) + ADAM_WD * w_ref[...])
        nm_ref[...] = nm
        nv_ref[...] = nv

    spec = pl.BlockSpec((tm, cols), lambda i: (i, 0))
    outs = _call(
        body, name="adamw", grid=(rows // tm,), in_specs=[spec] * 4, out_specs=[spec] * 3,
        out_shape=[jax.ShapeDtypeStruct((rows, cols), F32)] * 3,
        compiler_params=_cparams(("parallel",)),
    )(w2, g2, m2, v2)
    return [o.reshape(shape) for o in outs]


WEIGHTS = ("meta_tokens", "norm1_g", "w_in", "fox_f_bias", "fox_q_norm_g", "fox_k_norm_g", "gdn_conv_w",
           "gdn_a_log", "gdn_dt_bias", "gdn_norm_g", "w_branch_a", "w_branch_b", "w_out", "norm2_g", "w_up",
           "ffn_conv_w", "w_down")
SHARD_AXIS = {"meta_tokens": -1, "w_in": -1, "gdn_conv_w": -1, "w_branch_a": -1, "w_branch_b": -2, "w_out": -2,
              "w_up": -1, "ffn_conv_w": -1, "w_down": -2}
MATMUL_WEIGHTS = ("w_in", "w_branch_a", "w_branch_b", "w_out", "w_up", "w_down")
SMALL_SHARDED = ("meta_tokens", "gdn_conv_w", "ffn_conv_w")
REPLICATED = tuple(n for n in WEIGHTS if n not in SHARD_AXIS)


def _pack(arrays, dtype, row_align):
    flat = jnp.concatenate([a.reshape(-1).astype(dtype) for a in arrays])
    block = row_align * COMM_COLS
    total = -(-flat.shape[0] // block) * block
    flat = jnp.concatenate([flat, jnp.zeros((total - flat.shape[0],), dtype)])
    return flat.reshape(-1, COMM_COLS)


def _unpack(buf, shapes):
    flat, out, pos = buf.reshape(-1), [], 0
    for shape in shapes:
        size = 1
        for d in shape:
            size *= d
        out.append(flat[pos:pos + size].reshape(shape))
        pos += size
    return out


def _gather_full(shards, names, dtype, row_align):
    got = chip_all_gather(_pack([shards[n] for n in names], dtype, row_align))
    per_chip = [_unpack(got[j], [shards[n].shape for n in names]) for j in range(N_CHIPS)]
    return {n: jnp.concatenate([per_chip[j][i] for j in range(N_CHIPS)], axis=SHARD_AXIS[n]).astype(F32)
            for i, n in enumerate(names)}


def _shard_of(full, name, j):
    axis = SHARD_AXIS[name] % full.ndim
    size = full.shape[axis] // N_CHIPS
    return lax.slice_in_dim(full, j * size, (j + 1) * size, axis=axis)


def kernel(x, meta_tokens, norm1_g, w_in, fox_f_bias, fox_q_norm_g, fox_k_norm_g, gdn_conv_w, gdn_a_log, gdn_dt_bias, gdn_norm_g, w_branch_a, w_branch_b, w_out, norm2_g, w_up, ffn_conv_w, w_down, loss_target, m_meta_tokens, m_norm1_g, m_w_in, m_fox_f_bias, m_fox_q_norm_g, m_fox_k_norm_g, m_gdn_conv_w, m_gdn_a_log, m_gdn_dt_bias, m_gdn_norm_g, m_w_branch_a, m_w_branch_b, m_w_out, m_norm2_g, m_w_up, m_ffn_conv_w, m_w_down, v_meta_tokens, v_norm1_g, v_w_in, v_fox_f_bias, v_fox_q_norm_g, v_fox_k_norm_g, v_gdn_conv_w, v_gdn_a_log, v_gdn_dt_bias, v_gdn_norm_g, v_w_branch_a, v_w_branch_b, v_w_out, v_norm2_g, v_w_up, v_ffn_conv_w, v_w_down):
    w_loc = dict(zip(WEIGHTS, (meta_tokens, norm1_g, w_in, fox_f_bias, fox_q_norm_g, fox_k_norm_g, gdn_conv_w,
                               gdn_a_log, gdn_dt_bias, gdn_norm_g, w_branch_a, w_branch_b, w_out, norm2_g, w_up,
                               ffn_conv_w, w_down)))
    m_loc = dict(zip(WEIGHTS, (m_meta_tokens, m_norm1_g, m_w_in, m_fox_f_bias, m_fox_q_norm_g, m_fox_k_norm_g,
                               m_gdn_conv_w, m_gdn_a_log, m_gdn_dt_bias, m_gdn_norm_g, m_w_branch_a, m_w_branch_b,
                               m_w_out, m_norm2_g, m_w_up, m_ffn_conv_w, m_w_down)))
    v_loc = dict(zip(WEIGHTS, (v_meta_tokens, v_norm1_g, v_w_in, v_fox_f_bias, v_fox_q_norm_g, v_fox_k_norm_g,
                               v_gdn_conv_w, v_gdn_a_log, v_gdn_dt_bias, v_gdn_norm_g, v_w_branch_a, v_w_branch_b,
                               v_w_out, v_norm2_g, v_w_up, v_ffn_conv_w, v_w_down)))
    c = lax.axis_index("c")

    full = {n: w_loc[n] for n in REPLICATED}
    full.update(_gather_full(w_loc, MATMUL_WEIGHTS, BF16, COMM_ROW_ALIGN))
    full.update(_gather_full(w_loc, SMALL_SHARDED, F32, COMM_ROW_ALIGN_SMALL))
    full["w_in"] = pad_w_in(full["w_in"])

    loss, (g_full, g_x) = jax.value_and_grad(local_loss, argnums=(0, 1))(full, x[0], loss_target[0])
    g_full = dict(g_full)
    g_full["w_in"] = unpad_w_in(g_full["w_in"])

    sharded = MATMUL_WEIGHTS + SMALL_SHARDED
    g4 = jnp.stack([_pack([_shard_of(g_full[n], n, j) for n in sharded], F32, COMM_ROW_ALIGN)
                    for j in range(N_CHIPS)])
    pair_sum = add_own_half(g4, sibling_swap_halves(g4), c)
    g_shard = sibling_join(sum_slots(chip_scatter(pair_sum)))
    grads = dict(zip(sharded, _unpack(g_shard, [w_loc[n].shape for n in sharded])))
    g_rep = sum_slots(all_devices_gather(_pack([g_full[n] for n in REPLICATED], F32, 8)))
    grads.update(zip(REPLICATED, _unpack(g_rep, [w_loc[n].shape for n in REPLICATED])))

    loss = lax.psum(loss, ("x", "y", "c"))
    upd = {n: adamw(w_loc[n], grads[n], m_loc[n], v_loc[n]) for n in WEIGHTS}
    return (loss, g_x[None], *[grads[n] for n in WEIGHTS], *[upd[n][0] for n in WEIGHTS],
            *[upd[n][1] for n in WEIGHTS], *[upd[n][2] for n in WEIGHTS])
```

```python
import functools

import jax
import jax.numpy as jnp
from jax import lax
from jax.experimental import pallas as pl
from jax.experimental.pallas import tpu as pltpu

F32 = jnp.float32
BF16 = jnp.bfloat16
HI = lax.Precision.HIGHEST
MESH = pl.DeviceIdType.MESH

D_MODEL = 1024
N_META = 16
EPS = 1e-6
FOX_HEADS, FOX_HD = 8, 64
FOX_W = FOX_HEADS * FOX_HD
GDN_HEADS, GDN_HD, GDN_CHUNK, GDN_CONV = 8, 128, 64, 4
GDN_W = GDN_HEADS * GDN_HD
D_FF = 2816
FFN_CONV = 3
D_IN = 7704
D_IN_PAD = 8192
IN_SEGS = ((0, 1536, 0), (1536, 8, 1536), (1544, 3072, 1664), (4616, 16, 4736), (4632, 1024, 4864), (5656, 2048, 5888))
ROW_ALIGN = 256
ATT_BLK = 256
VMEM_LIMIT = 48 * 1024 * 1024
LANE = 128

ADAM_LR, ADAM_B1, ADAM_B2, ADAM_EPS, ADAM_WD, ADAM_STEP = 0.001, 0.9, 0.999, 1e-08, 0.01, 10


def _call(body, **kw):
    return pl.pallas_call(body, **kw)


def _tile(n, target, mult):
    best, t = None, mult
    while t <= min(n, target):
        if n % t == 0:
            best = t
        t += mult
    assert best is not None, (n, target, mult)
    return best


def _cparams(sem):
    return pltpu.CompilerParams(dimension_semantics=sem, vmem_limit_bytes=VMEM_LIMIT)


def _raw_dot(a, b, ca, cb, precise):
    dims = (((ca,), (cb,)), ((), ()))
    a_hi, b_hi = a.astype(BF16), b.astype(BF16)
    out = lax.dot_general(a_hi, b_hi, dims, preferred_element_type=F32)
    if precise:
        a_lo = (a - a_hi.astype(F32)).astype(BF16)
        b_lo = (b - b_hi.astype(F32)).astype(BF16)
        out = out + (lax.dot_general(a_hi, b_lo, dims, preferred_element_type=F32)
                     + lax.dot_general(a_lo, b_hi, dims, preferred_element_type=F32))
    return out


def _make_dot(ca, cb, precise):
    @jax.custom_vjp
    def f(a, b):
        return _raw_dot(a, b, ca, cb, precise)

    def fwd(a, b):
        return f(a, b), (a, b)

    def bwd(res, ct):
        a, b = res
        if ca == 1:
            da = _raw_dot(ct, b, 1, 1 if cb == 0 else 0, precise)
        else:
            da = _raw_dot(b, ct, 1 if cb == 0 else 0, 1, precise)
        if cb == 0:
            db = _raw_dot(a, ct, 0 if ca == 1 else 1, 0, precise)
        else:
            db = _raw_dot(ct, a, 0, 0 if ca == 1 else 1, precise)
        return da, db

    f.defvjp(fwd, bwd)
    return f


_DOTS = {(ca, cb, p): _make_dot(ca, cb, p) for ca in (0, 1) for cb in (0, 1) for p in (False, True)}


def _dot(a, b, ca=1, cb=0, precise=False):
    return _DOTS[(ca, cb, precise)](a, b)


def _mm_call(a, b, name, ta=False, tb=False, out_dtype=F32):
    k, m = a.shape if ta else a.shape[::-1]
    n, kb = b.shape if tb else b.shape[::-1]
    assert k == kb, (a.shape, b.shape)
    tm = _tile(m, 768, LANE if ta else 16)
    tn = _tile(n, 1408, LANE)
    tk = _tile(k, 1408, LANE)
    nk = k // tk
    dims = (((0 if ta else 1,), (1 if tb else 0,)), ((), ()))

    def body(a_ref, b_ref, o_ref, *scratch):
        part = lax.dot_general(a_ref[...], b_ref[...], dims, preferred_element_type=F32)
        if nk == 1:
            o_ref[...] = part.astype(o_ref.dtype)
            return
        acc_ref = scratch[0]
        kk = pl.program_id(2)

        @pl.when(kk == 0)
        def _():
            acc_ref[...] = part

        @pl.when(kk > 0)
        def _():
            acc_ref[...] += part

        @pl.when(kk == nk - 1)
        def _():
            o_ref[...] = acc_ref[...].astype(o_ref.dtype)

    return _call(
        body, name=name, grid=(m // tm, n // tn, nk),
        in_specs=[pl.BlockSpec((tk, tm), lambda i, j, kk: (kk, i)) if ta else
                  pl.BlockSpec((tm, tk), lambda i, j, kk: (i, kk)),
                  pl.BlockSpec((tn, tk), lambda i, j, kk: (j, kk)) if tb else
                  pl.BlockSpec((tk, tn), lambda i, j, kk: (kk, j))],
        out_specs=pl.BlockSpec((tm, tn), lambda i, j, kk: (i, j)),
        out_shape=jax.ShapeDtypeStruct((m, n), out_dtype),
        scratch_shapes=[pltpu.VMEM((tm, tn), F32)] if nk > 1 else [],
        compiler_params=_cparams(("parallel", "parallel", "arbitrary")),
    )(a, b)


def _make_mm(out_dtype):
    @jax.custom_vjp
    def op(a, w):
        return _mm_call(a.astype(BF16), w.astype(BF16), "mm_fwd", out_dtype=out_dtype)

    def fwd(a, w):
        a_b, w_b = a.astype(BF16), w.astype(BF16)
        return _mm_call(a_b, w_b, "mm_fwd", out_dtype=out_dtype), (a_b, w_b, jnp.zeros((), a.dtype))

    def bwd(res, ct):
        a_b, w_b, like_a = res
        ct_b = ct.astype(BF16)
        return (_mm_call(ct_b, w_b, "mm_dx", tb=True, out_dtype=like_a.dtype),
                _mm_call(a_b, ct_b, "mm_dw", ta=True))

    op.defvjp(fwd, bwd)
    return op


mm = _make_mm(F32)
mm_bf16 = _make_mm(BF16)


def _rows_specs(rows, tm, ncb, bc):
    specs = []
    for idx, r in enumerate(rows):
        if idx in bc:
            specs.append(pl.BlockSpec((tm, r.shape[1]), lambda i, j: (i, 0)))
        else:
            specs.append(pl.BlockSpec((tm, r.shape[1] // ncb), lambda i, j: (i, j)))
    return specs


def _param_specs(params):
    return [pl.BlockSpec(p.shape, lambda i, j: (0, 0)) for p in params]


def _group_slices(refs, groups, g, whole):
    out = []
    for idx, r in enumerate(refs):
        w = r.shape[1] // groups
        out.append((r[...] if idx in whole else r[:, g * w:(g + 1) * w]).astype(F32))
    return out


def _rows_fwd_call(fns, rows, params, outs, tm, ncb, bc, name, out_dtypes=None):
    r_total = rows[0].shape[0]
    nr, groups = len(rows), len(fns)
    out_dtypes = out_dtypes or [F32] * len(outs)

    def body(*refs):
        pvals = [r[...] for r in refs[nr:nr + len(params)]]
        for g, fn in enumerate(fns):
            res = fn(*_group_slices(refs[:nr], groups, g, bc), *pvals)
            for o_ref, val in zip(refs[nr + len(params):], res):
                w = o_ref.shape[1] // groups
                o_ref[:, g * w:(g + 1) * w] = val.astype(o_ref.dtype)

    return _call(
        body, name=name, grid=(r_total // tm, ncb),
        in_specs=_rows_specs(rows, tm, ncb, bc) + _param_specs(params),
        out_specs=[pl.BlockSpec((tm, w * groups), lambda i, j: (i, j)) for w in outs],
        out_shape=[jax.ShapeDtypeStruct((r_total, w * groups * ncb), dt) for w, dt in zip(outs, out_dtypes)],
        compiler_params=_cparams(("parallel", "parallel")),
    )(*rows, *params)


def _rows_bwd_call(fns, rows, params, cts, tm, ncb, bc, name):
    r_total = rows[0].shape[0]
    nr, npar, nct, groups = len(rows), len(params), len(cts), len(fns)

    def body(*refs):
        i, j = pl.program_id(0), pl.program_id(1)
        pvals = [r[...] for r in refs[nr:nr + npar]]
        ct_refs = refs[nr + npar:nr + npar + nct]
        d_refs = refs[nr + npar + nct:]
        shared = {idx: None for idx in list(bc) + list(range(nr, nr + npar))}
        for g, fn in enumerate(fns):
            _, vjp = jax.vjp(lambda *a, fn=fn: tuple(fn(*a)), *_group_slices(refs[:nr], groups, g, bc), *pvals)
            grads = vjp(tuple(_group_slices(ct_refs, groups, g, ())))
            for idx in range(nr + npar):
                if idx in shared:
                    shared[idx] = grads[idx] if shared[idx] is None else shared[idx] + grads[idx]
                else:
                    w = d_refs[idx].shape[1] // groups
                    d_refs[idx][:, g * w:(g + 1) * w] = grads[idx].astype(d_refs[idx].dtype)
        for idx, total in shared.items():
            first = (j == 0) if idx < nr else ((i == 0) & (j == 0))

            @pl.when(first)
            def _(idx=idx):
                d_refs[idx][...] = jnp.zeros_like(d_refs[idx])
            d_refs[idx][...] += total

    ct_specs = [pl.BlockSpec((tm, c.shape[1] // ncb), lambda i, j: (i, j)) for c in cts]
    return _call(
        body, name=name + "_bwd", grid=(r_total // tm, ncb),
        in_specs=_rows_specs(rows, tm, ncb, bc) + _param_specs(params) + ct_specs,
        out_specs=_rows_specs(rows, tm, ncb, bc) + _param_specs(params),
        out_shape=[jax.ShapeDtypeStruct(a.shape, a.dtype) for a in list(rows) + list(params)],
        compiler_params=_cparams(("arbitrary", "arbitrary")),
    )(*rows, *params, *cts)


def rowop(fn, name, outs, tm, ncb=1, bc=(), out_dtypes=None):
    fns = list(fn) if isinstance(fn, (list, tuple)) else [fn]

    @jax.custom_vjp
    def op(rows, params):
        return tuple(_rows_fwd_call(fns, rows, params, outs, tm, ncb, bc, name, out_dtypes))

    def fwd(rows, params):
        return op(rows, params), (rows, params)

    def bwd(res, cts):
        rows, params = res
        d = _rows_bwd_call(fns, rows, params, cts, tm, ncb, bc, name)
        return tuple(d[:len(rows)]), tuple(d[len(rows):])

    op.defvjp(fwd, bwd)
    return op


def _sigmoid(x):
    return 1.0 / (1.0 + jnp.exp(-x))


def _silu(x):
    return x * _sigmoid(x)


def _softplus(x):
    return jnp.maximum(x, 0.0) + jnp.log(1.0 + jnp.exp(-jnp.abs(x)))


def _f_rmsnorm(x, g):
    return (x * lax.rsqrt(jnp.mean(x * x, axis=-1, keepdims=True) + EPS) * g,)


def _f_qnorm(x, g):
    return (x * lax.rsqrt(jnp.mean(x * x, axis=-1, keepdims=True) + EPS) * (g * (FOX_HD ** -0.5)),)


def _f_logsig(x, b):
    return (-_softplus(-(x + b)),)


def _f_gdn_q(x):
    y = _silu(x)
    return (y * lax.rsqrt(jnp.sum(y * y, axis=-1, keepdims=True) + EPS) * (GDN_HD ** -0.5),)


def _f_gdn_k(x):
    y = _silu(x)
    return (y * lax.rsqrt(jnp.sum(y * y, axis=-1, keepdims=True) + EPS),)


def _f_gdn_v(x):
    return (_silu(x),)


def _f_gdn_gates(bl, al, a_log, dt_bias):
    return _sigmoid(bl), -jnp.exp(a_log) * _softplus(al + dt_bias)


def _f_gdn_out(o, z, g):
    return (o * lax.rsqrt(jnp.mean(o * o, axis=-1, keepdims=True) + EPS) * g * _silu(z),)


def _f_merge(g0, g1, ya, yb):
    return (_sigmoid(g0) * ya + _sigmoid(g1) * yb,)


def _f_residual(a, b, keep):
    return ((a + b) * keep,)


def _f_residual_norm(a, b, keep, g):
    r = (a + b) * keep
    return r, _f_rmsnorm(r, g)[0]


def _f_glu(a, b):
    return (_silu(a) * b,)


def _glu_call(up, ct):
    t_total, two_f = up.shape
    f = two_f // 2
    tm = _tile(t_total, 128, 16)
    wc = _tile(f, 1408, LANE)

    def body(*refs):
        up_ref, out_ref = refs[0], refs[-1]
        for c0 in range(0, f, wc):
            a, b = up_ref[:, c0:c0 + wc], up_ref[:, f + c0:f + c0 + wc]
            if ct is None:
                out_ref[:, c0:c0 + wc] = _f_glu(a, b)[0].astype(out_ref.dtype)
            else:
                _, vjp = jax.vjp(_f_glu, a, b)
                da, db = vjp((refs[1][:, c0:c0 + wc].astype(F32),))
                out_ref[:, c0:c0 + wc] = da
                out_ref[:, f + c0:f + c0 + wc] = db

    wide = pl.BlockSpec((tm, two_f), lambda i: (i, 0))
    narrow = pl.BlockSpec((tm, f), lambda i: (i, 0))
    return _call(
        body, name="ffn_glu" if ct is None else "ffn_glu_bwd", grid=(t_total // tm,),
        in_specs=[wide] if ct is None else [wide, narrow], out_specs=narrow if ct is None else wide,
        out_shape=jax.ShapeDtypeStruct((t_total, f), BF16) if ct is None else jax.ShapeDtypeStruct(up.shape, F32),
        compiler_params=_cparams(("parallel",)),
    )(*((up,) if ct is None else (up, ct)))


@jax.custom_vjp
def glu(up):
    return _glu_call(up, None)


glu.defvjp(lambda up: (glu(up), up), lambda up, ct: (_glu_call(up, ct),))


def _shift_down(x, halo, s, row8):
    rx = pltpu.roll(x, s, 0)
    top = jnp.where(row8 < s, pltpu.roll(halo, s, 0), rx[:8])
    return jnp.concatenate([top, rx[8:]], axis=0)


def _shift_up(x, nxt, s, row8):
    tm = x.shape[0]
    rx = pltpu.roll(x, tm - s, 0)
    bot = jnp.where(row8 >= 8 - s, pltpu.roll(nxt, 8 - s, 0), rx[tm - 8:])
    return jnp.concatenate([rx[:tm - 8], bot], axis=0)


def _conv_tiles(r_total, c_total):
    return _tile(r_total, 768, 8), _tile(c_total, 1408, LANE)


def _conv_fwd_call(x, w8, k_taps):
    r_total, c_total = x.shape
    tm, tc = _conv_tiles(r_total, c_total)
    hb = tm // 8

    def body(x_ref, halo_ref, w_ref, y_ref):
        i = pl.program_id(1)
        xt = x_ref[...]
        halo = jnp.where(i > 0, halo_ref[...], 0.0)
        row8 = lax.broadcasted_iota(jnp.int32, (8, tc), 0)
        acc = w_ref[k_taps - 1:k_taps, :] * xt
        for k in range(k_taps - 1):
            acc += w_ref[k:k + 1, :] * _shift_down(xt, halo, k_taps - 1 - k, row8)
        y_ref[...] = acc

    return _call(
        body, name="dwconv_fwd", grid=(c_total // tc, r_total // tm),
        in_specs=[pl.BlockSpec((tm, tc), lambda c, i: (i, c)),
                  pl.BlockSpec((8, tc), lambda c, i: (jnp.maximum(i * hb - 1, 0), c)),
                  pl.BlockSpec((8, tc), lambda c, i: (0, c))],
        out_specs=pl.BlockSpec((tm, tc), lambda c, i: (i, c)),
        out_shape=jax.ShapeDtypeStruct(x.shape, F32),
        compiler_params=_cparams(("parallel", "parallel")),
    )(x, x, w8)


def _conv_bwd_call(x, w8, dy, k_taps):
    r_total, c_total = x.shape
    tm, tc = _conv_tiles(r_total, c_total)
    hb = tm // 8
    n_i = r_total // tm

    def body(x_ref, w_ref, dy_ref, nxt_ref, dx_ref, dw_ref):
        i = pl.program_id(1)
        xt, dyt = x_ref[...], dy_ref[...]
        nxt = jnp.where(i < n_i - 1, nxt_ref[...], 0.0)
        row8 = lax.broadcasted_iota(jnp.int32, (8, tc), 0)
        dx = w_ref[k_taps - 1:k_taps, :] * dyt
        upd = jnp.where(row8 == k_taps - 1, jnp.sum(dyt * xt, axis=0, keepdims=True), 0.0)
        for k in range(k_taps - 1):
            dy_ahead = _shift_up(dyt, nxt, k_taps - 1 - k, row8)
            dx += w_ref[k:k + 1, :] * dy_ahead
            upd = jnp.where(row8 == k, jnp.sum(dy_ahead * xt, axis=0, keepdims=True), upd)
        dx_ref[...] = dx

        @pl.when(i == 0)
        def _():
            dw_ref[...] = jnp.zeros_like(dw_ref)

        dw_ref[...] += upd

    return _call(
        body, name="dwconv_bwd", grid=(c_total // tc, n_i),
        in_specs=[pl.BlockSpec((tm, tc), lambda c, i: (i, c)),
                  pl.BlockSpec((8, tc), lambda c, i: (0, c)),
                  pl.BlockSpec((tm, tc), lambda c, i: (i, c)),
                  pl.BlockSpec((8, tc), lambda c, i: (jnp.minimum((i + 1) * hb, r_total // 8 - 1), c))],
        out_specs=[pl.BlockSpec((tm, tc), lambda c, i: (i, c)), pl.BlockSpec((8, tc), lambda c, i: (0, c))],
        out_shape=[jax.ShapeDtypeStruct(x.shape, F32), jax.ShapeDtypeStruct(w8.shape, F32)],
        compiler_params=_cparams(("parallel", "arbitrary")),
    )(x, w8, dy, dy)


def make_dwconv(k_taps):
    @jax.custom_vjp
    def op(x, w8):
        return _conv_fwd_call(x, w8, k_taps)

    def fwd(x, w8):
        return op(x, w8), (x, w8)

    def bwd(res, dy):
        x, w8 = res
        dx, dw = _conv_bwd_call(x, w8, dy, k_taps)
        return dx, dw

    op.defvjp(fwd, bwd)
    return op


def _cumsum_call(x, reverse):
    h, t_total = x.shape
    tb = _tile(t_total, 256, LANE)
    nb = t_total // tb

    def body(x_ref, o_ref, carry_ref):
        i = pl.program_id(0)

        @pl.when(i == 0)
        def _():
            carry_ref[...] = jnp.zeros_like(carry_ref)

        r = lax.broadcasted_iota(jnp.int32, (tb, tb), 0)
        c = lax.broadcasted_iota(jnp.int32, (tb, tb), 1)
        tri = jnp.where((r >= c) if reverse else (r <= c), 1.0, 0.0).astype(F32)
        xv = x_ref[...]
        carry = jnp.max(carry_ref[...], axis=1, keepdims=True)
        o_ref[...] = _raw_dot(xv, tri, 1, 0, True) + carry
        carry_ref[...] = jnp.broadcast_to(carry + jnp.sum(xv, axis=1, keepdims=True), carry_ref.shape)

    imap = (lambda i: (0, nb - 1 - i)) if reverse else (lambda i: (0, i))
    return _call(
        body, name="cumsum_rev" if reverse else "cumsum", grid=(nb,),
        in_specs=[pl.BlockSpec((h, tb), imap)], out_specs=pl.BlockSpec((h, tb), imap),
        out_shape=jax.ShapeDtypeStruct(x.shape, F32), scratch_shapes=[pltpu.VMEM((h, LANE), F32)],
        compiler_params=_cparams(("arbitrary",)),
    )(x)


@jax.custom_vjp
def cumsum_lanes(x):
    return _cumsum_call(x, False)


cumsum_lanes.defvjp(lambda x: (cumsum_lanes(x), None), lambda _, ct: (_cumsum_call(ct, True),))


NEG_BIG = -1e30


def _attn_sub_tiles(nb):
    return max(s for s in (3, 2, 1) if nb % s == 0)


EXP_ZERO = -100.0
SMEM = pl.BlockSpec(memory_space=pltpu.SMEM)


def _max_row_norm_sq(x):
    h_total, t_total, hd = x.shape
    tb = _tile(t_total, 2816, 8)

    def body(x_ref, o_ref):
        @pl.when(pl.program_id(1) == 0)
        def _():
            o_ref[...] = jnp.zeros_like(o_ref)

        xv = x_ref[0]
        top = jnp.max(jnp.sum(xv * xv, axis=1, keepdims=True), axis=0, keepdims=True)
        o_ref[0] = jnp.maximum(o_ref[0], top)

    return _call(
        body, name="max_row_norm", grid=(h_total, t_total // tb),
        in_specs=[pl.BlockSpec((1, tb, hd), lambda h, i: (h, i, 0))],
        out_specs=pl.BlockSpec((1, 8, LANE), lambda h, i: (h, 0, 0)),
        out_shape=jax.ShapeDtypeStruct((h_total, 8, LANE), F32),
        compiler_params=_cparams(("parallel", "arbitrary")),
    )(x)


def _attn_skip_tables(q, k, f_row):
    bound = 2.0 * jnp.sqrt(_max_row_norm_sq(q)[:, 0, :1] * _max_row_norm_sq(k)[:, 0, :1])
    return EXP_ZERO - bound, f_row[:, :, 0, 0], f_row[:, :, 0, -1]


def _attn_fwd_call(q, k, v, f_col, f_row, tables):
    h_total, t_total, hd = q.shape
    blk = f_row.shape[-1]
    nb = t_total // blk
    nsub = _attn_sub_tiles(nb)
    tq = nsub * blk

    def body(thr_ref, first_ref, last_ref, q_ref, k_ref, vt_ref, fc_ref, fr_ref, o_ref, lse_ref):
        h = pl.program_id(0)
        i = pl.program_id(1)
        gap_needed = thr_ref[h, 0]
        f_tile = first_ref[h, i * nsub]
        j_start = lax.while_loop(lambda j: (j < i * nsub) & (f_tile - last_ref[h, j] < gap_needed),
                                 lambda j: j + 1, 0)
        r = lax.broadcasted_iota(jnp.int32, (blk, blk), 0)
        c = lax.broadcasted_iota(jnp.int32, (blk, blk), 1)
        qs = [q_ref[0, s * blk:(s + 1) * blk, :].astype(BF16) for s in range(nsub)]
        fqs = [fr_ref[0, i * nsub + s] for s in range(nsub)]

        def load_kv(j):
            off = pl.multiple_of(j * blk, blk)
            return k_ref[0, pl.ds(off, blk), :], vt_ref[0, j], fc_ref[0, pl.ds(off, blk), :]

        def tile(kv, s, carry, diagonal):
            kj, vtj, fk = kv
            m, l, acc = carry
            st = _raw_dot(kj, qs[s], 1, 1, False) + fqs[s] - fk
            if diagonal:
                st = jnp.where(r <= c, st, NEG_BIG)
            m_new = jnp.maximum(m, jnp.max(st, axis=0, keepdims=True))
            p = jnp.exp(st - m_new)
            alpha = jnp.exp(m - m_new)
            l = alpha * l + jnp.sum(p, axis=0, keepdims=True)
            acc = alpha * acc + _raw_dot(vtj, p, 1, 0, False)
            return m_new, l, acc

        def below_diagonal(j, carry):
            kv = load_kv(j)
            return tuple(tile(kv, s, carry[s], False) for s in range(nsub))

        init = tuple((jnp.full((1, blk), NEG_BIG, F32), jnp.zeros((1, blk), F32), jnp.zeros((hd, blk), F32))
                     for _ in range(nsub))
        carry = list(lax.fori_loop(j_start, i * nsub, below_diagonal, init))
        for d in range(nsub):
            kv = load_kv(i * nsub + d)
            for s in range(d, nsub):
                carry[s] = tile(kv, s, carry[s], s == d)
        for s, (m, l, acc) in enumerate(carry):
            o_ref[0, :, s * blk:(s + 1) * blk] = acc / l
            lse_ref[0, s] = m + jnp.log(l)

    vt = v.reshape(h_total, nb, blk, hd).transpose(0, 1, 3, 2).astype(BF16)
    return _call(
        body, name="fox_fwd", grid=(h_total, nb // nsub),
        in_specs=[SMEM, SMEM, SMEM,
                  pl.BlockSpec((1, tq, hd), lambda h, i: (h, i, 0)),
                  pl.BlockSpec((1, t_total, hd), lambda h, i: (h, 0, 0)),
                  pl.BlockSpec((1, nb, hd, blk), lambda h, i: (h, 0, 0, 0)),
                  pl.BlockSpec((1, t_total, 1), lambda h, i: (h, 0, 0)),
                  pl.BlockSpec((1, nb, 1, blk), lambda h, i: (h, 0, 0, 0))],
        out_specs=[pl.BlockSpec((1, hd, tq), lambda h, i: (h, 0, i)),
                   pl.BlockSpec((1, nsub, 1, blk), lambda h, i: (h, i, 0, 0))],
        out_shape=[jax.ShapeDtypeStruct((h_total, hd, t_total), F32), jax.ShapeDtypeStruct(f_row.shape, F32)],
        compiler_params=_cparams(("parallel", "parallel")),
    )(*tables, q, k.astype(BF16), vt, f_col, f_row)


def _attn_bwd_call(q, k, v, f_col, f_row, tables, lse_row, delta_row, do_blk):
    h_total, t_total, hd = q.shape
    blk = f_row.shape[-1]
    nb = t_total // blk
    nsub = _attn_sub_tiles(nb)
    tkv = nsub * blk

    def body(thr_ref, first_ref, last_ref, q_ref, do_ref, k_ref, v_ref, fc_ref, fr_ref, lse_ref, dl_ref,
             dq_ref, dk_ref, dv_ref, dfk_ref, dfq_ref):
        h = pl.program_id(0)
        j = pl.program_id(1)
        gap_needed = thr_ref[h, 0]
        f_tile = last_ref[h, j * nsub + nsub - 1]
        i_stop = lax.while_loop(lambda i: (i < nb) & (first_ref[h, jnp.minimum(i, nb - 1)] - f_tile >= gap_needed),
                                lambda i: i + 1, (j + 1) * nsub)

        @pl.when(j == 0)
        def _():
            dq_ref[...] = jnp.zeros_like(dq_ref)
            dfq_ref[...] = jnp.zeros_like(dfq_ref)

        ks = [k_ref[0, s * blk:(s + 1) * blk, :].astype(BF16) for s in range(nsub)]
        vs = [v_ref[0, s * blk:(s + 1) * blk, :].astype(BF16) for s in range(nsub)]
        fks = [fc_ref[0, s * blk:(s + 1) * blk, :] for s in range(nsub)]
        r = lax.broadcasted_iota(jnp.int32, (blk, blk), 0)
        c = lax.broadcasted_iota(jnp.int32, (blk, blk), 1)

        def q_step(i, accs, subs):
            off = pl.multiple_of(i * blk, blk)
            qi = q_ref[0, pl.ds(off, blk), :]
            doi = do_ref[0, i]
            fq, lse, dl = fr_ref[0, i], lse_ref[0, i], dl_ref[0, i]
            accs = list(accs)
            dq_i, dfq_i = None, None
            for s, diagonal in subs:
                dk, dv, dfk = accs[s]
                st = _raw_dot(ks[s], qi, 1, 1, False) + fq - fks[s] - lse
                if diagonal:
                    st = jnp.where(r <= c, st, NEG_BIG)
                pt = jnp.exp(st)
                dv = dv + _raw_dot(pt, doi, 1, 1, False)
                dst = pt * (_raw_dot(vs[s], doi, 1, 0, False) - dl)
                dk = dk + _raw_dot(dst, qi, 1, 0, False)
                dfk = dfk - jnp.sum(dst, axis=1, keepdims=True)
                accs[s] = (dk, dv, dfk)
                dq_s = _raw_dot(dst, ks[s], 0, 0, False)
                dfq_s = jnp.sum(dst, axis=0, keepdims=True)
                dq_i = dq_s if dq_i is None else dq_i + dq_s
                dfq_i = dfq_s if dfq_i is None else dfq_i + dfq_s
            dfq_ref[0, i] += dfq_i
            dq_ref[0, pl.ds(off, blk), :] += dq_i
            return tuple(accs)

        accs = tuple((jnp.zeros((blk, hd), F32), jnp.zeros((blk, hd), F32), jnp.zeros((blk, 1), F32))
                     for _ in range(nsub))
        for d in range(nsub):
            accs = q_step(j * nsub + d, accs, [(s, s == d) for s in range(d + 1)])
        accs = lax.fori_loop((j + 1) * nsub, i_stop,
                             lambda i, a: q_step(i, a, [(s, False) for s in range(nsub)]), accs)
        for s, (dk, dv, dfk) in enumerate(accs):
            dk_ref[0, s * blk:(s + 1) * blk, :] = dk
            dv_ref[0, s * blk:(s + 1) * blk, :] = dv
            dfk_ref[0, s * blk:(s + 1) * blk, :] = dfk

    full = pl.BlockSpec((1, t_total, hd), lambda h, j: (h, 0, 0))
    tile = pl.BlockSpec((1, tkv, hd), lambda h, j: (h, j, 0))
    col = pl.BlockSpec((1, tkv, 1), lambda h, j: (h, j, 0))
    rows = pl.BlockSpec((1, nb, 1, blk), lambda h, j: (h, 0, 0, 0))
    do_blocks = pl.BlockSpec((1, nb, hd, blk), lambda h, j: (h, 0, 0, 0))
    return _call(
        body, name="fox_bwd", grid=(h_total, nb // nsub),
        in_specs=[SMEM, SMEM, SMEM, full, do_blocks, tile, tile, col, rows, rows, rows],
        out_specs=[full, tile, tile, col, rows],
        out_shape=[jax.ShapeDtypeStruct(q.shape, F32), jax.ShapeDtypeStruct(q.shape, F32),
                   jax.ShapeDtypeStruct(q.shape, F32), jax.ShapeDtypeStruct(f_col.shape, F32),
                   jax.ShapeDtypeStruct(f_row.shape, F32)],
        compiler_params=_cparams(("parallel", "arbitrary")),
    )(*tables, q.astype(BF16), do_blk, k, v, f_col, f_row, lse_row, delta_row)


def _attn_delta_call(do_t, o_t):
    h_total, hd, t_total = o_t.shape
    tb = _tile(t_total, 2816, LANE)

    def body(do_ref, o_ref, d_ref):
        d_ref[0] = jnp.sum(do_ref[0] * o_ref[0], axis=0, keepdims=True)

    spec = pl.BlockSpec((1, hd, tb), lambda h, i: (h, 0, i))
    return _call(
        body, name="fox_delta", grid=(h_total, t_total // tb), in_specs=[spec, spec],
        out_specs=pl.BlockSpec((1, 1, tb), lambda h, i: (h, 0, i)),
        out_shape=jax.ShapeDtypeStruct((h_total, 1, t_total), F32),
        compiler_params=_cparams(("parallel", "parallel")),
    )(do_t, o_t)


@jax.custom_vjp
def fox_attention(q, k, v, f_col, f_row):
    return _attn_fwd_call(q, k, v, f_col, f_row, _attn_skip_tables(q, k, f_row))[0]


def _fox_fwd(q, k, v, f_col, f_row):
    tables = _attn_skip_tables(q, k, f_row)
    o_t, lse_row = _attn_fwd_call(q, k, v, f_col, f_row, tables)
    return o_t, (q, k, v, f_col, f_row, tables, o_t, lse_row)


def _fox_bwd(res, do_t):
    q, k, v, f_col, f_row, tables, o_t, lse_row = res
    h_total, t_total, hd = q.shape
    nb, blk = f_row.shape[1], f_row.shape[3]
    delta = _attn_delta_call(do_t, o_t).reshape(f_row.shape)
    do_blk = do_t.reshape(h_total, hd, nb, blk).transpose(0, 2, 1, 3).astype(BF16)
    return tuple(_attn_bwd_call(q, k, v, f_col, f_row, tables, lse_row, delta, do_blk))


fox_attention.defvjp(_fox_fwd, _fox_bwd)


def _head_col(blk, h):
    lane = lax.broadcasted_iota(jnp.int32, blk.shape, 1)
    return jnp.sum(jnp.where(lane == h, blk, 0.0), axis=1, keepdims=True)


@jax.custom_vjp
def _cat2(a, b):
    return jnp.concatenate([a, b], axis=1)


_cat2.defvjp(lambda a, b: (_cat2(a, b), a.shape[1]), lambda na, ct: (ct[:, :na], ct[:, na:]))


@jax.custom_vjp
def _split2(x):
    half = x.shape[1] // 2
    return x[:, :half], x[:, half:]


_split2.defvjp(lambda x: (_split2(x), None), lambda _, cts: (jnp.concatenate(cts, axis=1),))


def _neumann_solve(m, b):
    x = b - _raw_dot(m, b, 1, 0, False)
    powers, steps = [m], 1
    while 2 * steps < GDN_CHUNK:
        powers.append(_raw_dot(powers[-1], powers[-1], 1, 0, False))
        x = x + _raw_dot(powers[-1], x, 1, 0, False)
        steps *= 2
    return x, powers


@jax.custom_vjp
def _unit_lower_solve(m, b):
    return _neumann_solve(m, b)[0]


def _unit_lower_solve_fwd(m, b):
    x, powers = _neumann_solve(m, b)
    return x, (powers, x)


def _unit_lower_solve_bwd(res, dx):
    powers, x = res
    db = dx - _raw_dot(powers[0], dx, 0, 0, False)
    for p in powers[1:]:
        db = db + _raw_dot(p, db, 0, 0, False)
    return -_raw_dot(db, x, 1, 1, False), db


_unit_lower_solve.defvjp(_unit_lower_solve_fwd, _unit_lower_solve_bwd)


@jax.custom_vjp
def _unit_lower_solve_known(m, b, x):
    return x


def _unit_lower_solve_known_bwd(res, dx):
    m, x = res
    powers, steps = [m], 1
    while 2 * steps < GDN_CHUNK:
        powers.append(_raw_dot(powers[-1], powers[-1], 1, 0, False))
        steps *= 2
    dm, db = _unit_lower_solve_bwd((powers, x), dx)
    return dm, db, jnp.zeros_like(x)


_unit_lower_solve_known.defvjp(lambda m, b, x: (x, (m, x)), _unit_lower_solve_known_bwd)


def _gdn_intra(h, q, k, v, b_blk, g_blk, uw_known=None):
    n = q.shape[0]
    b, g = _head_col(b_blk, h), _head_col(g_blk, h)
    r = lax.broadcasted_iota(jnp.int32, (n, n), 0)
    c = lax.broadcasted_iota(jnp.int32, (n, n), 1)
    same = (r // GDN_CHUNK) == (c // GDN_CHUNK)
    incl = same & (r >= c)
    g_row = jnp.sum(jnp.where(r == c, g, 0.0), axis=0, keepdims=True)
    big_g = jnp.sum(jnp.where(incl, g_row, 0.0), axis=1, keepdims=True)
    big_g_row = jnp.sum(jnp.where(same & (r <= c), g, 0.0), axis=0, keepdims=True)
    g_tot = jnp.sum(jnp.where(same, g_row, 0.0), axis=1, keepdims=True)
    dec = jnp.where(incl, jnp.exp(jnp.where(incl, big_g - big_g_row, 0.0)), 0.0)
    dec_strict = jnp.where(r > c, dec, 0.0)
    e_g = jnp.exp(big_g)
    kb = k * b
    m = _dot(kb, k, 1, 1) * dec_strict
    rs = lax.broadcasted_iota(jnp.int32, (n, GDN_CHUNK), 0)
    cs = lax.broadcasted_iota(jnp.int32, (n, GDN_CHUNK), 1)
    fold = jnp.where(rs % GDN_CHUNK == cs, 1.0, 0.0).astype(F32)
    aqk = _dot(_dot(q, k, 1, 1) * dec, fold, 1, 0, True)
    rhs = _cat2(v * b, kb * e_g)
    u, w = _split2(_unit_lower_solve(m, rhs) if uw_known is None else
                   _unit_lower_solve_known(m, rhs, jnp.concatenate(uw_known, axis=1)))
    lane = lax.broadcasted_iota(jnp.int32, b_blk.shape, 1)
    return u, w, q * e_g, k * jnp.exp(g_tot - big_g), aqk, jnp.where(lane == h, g_tot, 0.0)


def _gdn_rec(h, s, u, w, qg, kd, aqk, gl_blk):
    g_last = jnp.max(_head_col(gl_blk, h), axis=0, keepdims=True)
    big_u = u - _dot(w, s)
    o = _dot(qg, s) + _dot(aqk, big_u)
    s_next = s * jnp.exp(g_last) + _dot(kd, big_u, 0, 0)
    return o, s_next


GDN_TOK_BLK = 256


def _gdn_layout(t_total, rev):
    tb = _tile(t_total, GDN_TOK_BLK, GDN_CHUNK)
    cb, nblk = tb // GDN_CHUNK, t_total // tb
    pos = (lambda i: nblk - 1 - i) if rev else (lambda i: i)
    specs = dict(
        tok=pl.BlockSpec((tb, GDN_W), lambda i: (pos(i), 0)),
        q=pl.BlockSpec((tb, GDN_W), lambda i: (pos(i), 0)),
        k=pl.BlockSpec((tb, GDN_W), lambda i: (pos(i), 1)),
        v=pl.BlockSpec((tb, GDN_W), lambda i: (pos(i), 2)),
        qkv=pl.BlockSpec((tb, 3 * GDN_W), lambda i: (pos(i), 0)),
        gate=pl.BlockSpec((tb, GDN_HEADS), lambda i: (pos(i), 0)),
        aqk=pl.BlockSpec((GDN_HEADS, tb, GDN_CHUNK), lambda i: (0, pos(i), 0)),
        state=pl.BlockSpec((GDN_HEADS, cb, GDN_HD, GDN_HD), lambda i: (0, pos(i), 0, 0)))
    return cb, nblk, specs


def _gdn_shapes(t_total):
    n_chunks = t_total // GDN_CHUNK
    return dict(tok=jax.ShapeDtypeStruct((t_total, GDN_W), F32),
                gate=jax.ShapeDtypeStruct((t_total, GDN_HEADS), F32),
                aqk=jax.ShapeDtypeStruct((GDN_HEADS, t_total, GDN_CHUNK), F32),
                state=jax.ShapeDtypeStruct((GDN_HEADS, n_chunks, GDN_HD, GDN_HD), F32))


def _chunk_rows(ci):
    return pl.ds(pl.multiple_of(ci * GDN_CHUNK, GDN_CHUNK), GDN_CHUNK)


def _head_cols(h):
    return pl.ds(h * GDN_HD, GDN_HD)


def _gdn_intra_fwd_call(qkv, b, g):
    cb, nblk, sp = _gdn_layout(qkv.shape[0], False)
    sh = _gdn_shapes(qkv.shape[0])

    def body(q_ref, k_ref, v_ref, b_ref, g_ref, u_ref, w_ref, qg_ref, kd_ref, aqk_ref, gl_ref):
        b_blk, g_blk = b_ref[...], g_ref[...]
        gl = jnp.zeros(b_blk.shape, F32)
        for h in range(GDN_HEADS):
            cols = _head_cols(h)
            u, w, qg, kd, aqk, gl_h = _gdn_intra(h, q_ref[:, cols], k_ref[:, cols], v_ref[:, cols], b_blk, g_blk)
            u_ref[:, cols] = u
            w_ref[:, cols] = w
            qg_ref[:, cols] = qg
            kd_ref[:, cols] = kd
            aqk_ref[h] = aqk
            gl = gl + gl_h
        gl_ref[...] = gl

    return _call(
        body, name="gdn_intra_fwd", grid=(nblk,),
        in_specs=[sp["q"], sp["k"], sp["v"]] + [sp["gate"]] * 2,
        out_specs=[sp["tok"]] * 4 + [sp["aqk"], sp["gate"]],
        out_shape=[sh["tok"]] * 4 + [sh["aqk"], sh["gate"]],
        compiler_params=_cparams(("parallel",)),
    )(qkv, qkv, qkv, b, g)


def _gdn_intra_bwd_call(qkv, b, g, u, w, du, dw, dqg, dkd, daqk, dgl):
    cb, nblk, sp = _gdn_layout(qkv.shape[0], False)
    sh = _gdn_shapes(qkv.shape[0])

    def body(q_ref, k_ref, v_ref, b_ref, g_ref, u_ref, w_ref, du_ref, dw_ref, dqg_ref, dkd_ref, daqk_ref, dgl_ref,
             dqkv_ref, db_ref, dg_ref):
        b_blk, g_blk, dgl = b_ref[...], g_ref[...], dgl_ref[...]
        db = jnp.zeros(b_blk.shape, F32)
        dg = jnp.zeros(b_blk.shape, F32)
        for h in range(GDN_HEADS):
            cols = _head_cols(h)
            _, vjp = jax.vjp(functools.partial(_gdn_intra, h, uw_known=(u_ref[:, cols], w_ref[:, cols])),
                             q_ref[:, cols], k_ref[:, cols], v_ref[:, cols], b_blk, g_blk)
            dq, dk, dv, db_h, dg_h = vjp((du_ref[:, cols], dw_ref[:, cols], dqg_ref[:, cols], dkd_ref[:, cols],
                                          daqk_ref[h], dgl))
            dqkv_ref[:, pl.ds(h * GDN_HD, GDN_HD)] = dq
            dqkv_ref[:, pl.ds(GDN_W + h * GDN_HD, GDN_HD)] = dk
            dqkv_ref[:, pl.ds(2 * GDN_W + h * GDN_HD, GDN_HD)] = dv
            db = db + db_h
            dg = dg + dg_h
        db_ref[...] = db
        dg_ref[...] = dg

    return _call(
        body, name="gdn_intra_bwd", grid=(nblk,),
        in_specs=[sp["q"], sp["k"], sp["v"]] + [sp["gate"]] * 2 + [sp["tok"]] * 6 + [sp["aqk"], sp["gate"]],
        out_specs=[sp["qkv"]] + [sp["gate"]] * 2,
        out_shape=[jax.ShapeDtypeStruct(qkv.shape, F32)] + [sh["gate"]] * 2,
        compiler_params=_cparams(("parallel",)),
    )(qkv, qkv, qkv, b, g, u, w, du, dw, dqg, dkd, daqk, dgl)


def _gdn_rec_fwd_call(u, w, qg, kd, aqk, gl):
    cb, nblk, sp = _gdn_layout(u.shape[0], False)
    sh = _gdn_shapes(u.shape[0])

    def body(u_ref, w_ref, qg_ref, kd_ref, aqk_ref, gl_ref, o_ref, s_all_ref, s_ref):
        @pl.when(pl.program_id(0) == 0)
        def _():
            s_ref[...] = jnp.zeros_like(s_ref)

        def chunk(ci, carry):
            rows = _chunk_rows(ci)
            gl_row = gl_ref[rows, :]
            states = [s_ref[h] for h in range(GDN_HEADS)]
            res = [_gdn_rec(h, states[h], u_ref[rows, _head_cols(h)], w_ref[rows, _head_cols(h)],
                            qg_ref[rows, _head_cols(h)], kd_ref[rows, _head_cols(h)], aqk_ref[h, rows, :], gl_row)
                   for h in range(GDN_HEADS)]
            for h, (o, s_next) in enumerate(res):
                s_all_ref[h, ci] = states[h]
                o_ref[rows, _head_cols(h)] = o
                s_ref[h] = s_next
            return carry

        lax.fori_loop(0, cb, chunk, 0)

    return _call(
        body, name="gdn_rec_fwd", grid=(nblk,),
        in_specs=[sp["tok"]] * 4 + [sp["aqk"], sp["gate"]],
        out_specs=[sp["tok"], sp["state"]], out_shape=[sh["tok"], sh["state"]],
        scratch_shapes=[pltpu.VMEM((GDN_HEADS, GDN_HD, GDN_HD), F32)],
        compiler_params=_cparams(("arbitrary",)),
    )(u, w, qg, kd, aqk, gl)


def _gdn_rec_bwd_call(u, w, qg, kd, aqk, gl, s_all, do):
    cb, nblk, sp = _gdn_layout(u.shape[0], True)
    sh = _gdn_shapes(u.shape[0])

    def body(u_ref, w_ref, qg_ref, kd_ref, aqk_ref, gl_ref, s_all_ref, do_ref,
             du_ref, dw_ref, dqg_ref, dkd_ref, daqk_ref, dgl_ref, ds_ref):
        @pl.when(pl.program_id(0) == 0)
        def _():
            ds_ref[...] = jnp.zeros_like(ds_ref)

        def chunk(step, carry):
            ci = cb - 1 - step
            rows = _chunk_rows(ci)
            gl_row = gl_ref[rows, :]
            res = []
            for h in range(GDN_HEADS):
                cols = _head_cols(h)
                _, vjp = jax.vjp(functools.partial(_gdn_rec, h), s_all_ref[h, ci], u_ref[rows, cols],
                                 w_ref[rows, cols], qg_ref[rows, cols], kd_ref[rows, cols], aqk_ref[h, rows, :],
                                 gl_row)
                res.append(vjp((do_ref[rows, cols], ds_ref[h])))
            dgl = jnp.zeros((GDN_CHUNK, GDN_HEADS), F32)
            for h, (ds, du, dw, dqg, dkd, daqk, dgl_h) in enumerate(res):
                cols = _head_cols(h)
                ds_ref[h] = ds
                du_ref[rows, cols] = du
                dw_ref[rows, cols] = dw
                dqg_ref[rows, cols] = dqg
                dkd_ref[rows, cols] = dkd
                daqk_ref[h, rows, :] = daqk
                dgl = dgl + dgl_h
            dgl_ref[rows, :] = dgl
            return carry

        lax.fori_loop(0, cb, chunk, 0)

    return _call(
        body, name="gdn_rec_bwd", grid=(nblk,),
        in_specs=[sp["tok"]] * 4 + [sp["aqk"], sp["gate"], sp["state"], sp["tok"]],
        out_specs=[sp["tok"]] * 4 + [sp["aqk"], sp["gate"]],
        out_shape=[sh["tok"]] * 4 + [sh["aqk"], sh["gate"]],
        scratch_shapes=[pltpu.VMEM((GDN_HEADS, GDN_HD, GDN_HD), F32)],
        compiler_params=_cparams(("arbitrary",)),
    )(u, w, qg, kd, aqk, gl, s_all, do)


@jax.custom_vjp
def gdn_intra(qkv, b, g):
    return tuple(_gdn_intra_fwd_call(qkv, b, g))


def _gdn_intra_fwd(*a):
    outs = gdn_intra(*a)
    return outs, a + (outs[0], outs[1])


gdn_intra.defvjp(_gdn_intra_fwd, lambda res, cts: tuple(_gdn_intra_bwd_call(*res, *cts)))


@jax.custom_vjp
def gdn_rec(u, w, qg, kd, aqk, gl):
    return _gdn_rec_fwd_call(u, w, qg, kd, aqk, gl)[0]


def _gdn_rec_fwd(*a):
    o, s_all = _gdn_rec_fwd_call(*a)
    return o, a + (s_all,)


gdn_rec.defvjp(_gdn_rec_fwd, lambda res, do: tuple(_gdn_rec_bwd_call(*res, do)))


def gated_delta(qkv, b, g):
    return gdn_rec(*gdn_intra(qkv, b, g))


def _loss_call(y, tgt, first, last):
    r_total, d = y.shape
    tm = _tile(r_total, 256, 8)

    def body(y_ref, t_ref, loss_ref, dy_ref):
        i = pl.program_id(0)

        @pl.when(i == 0)
        def _():
            loss_ref[...] = jnp.zeros_like(loss_ref)

        row = lax.broadcasted_iota(jnp.int32, (tm, d), 0) + i * tm
        err = jnp.where((row >= first) & (row < last), y_ref[...] - t_ref[...], 0.0)
        dy_ref[...] = err * (1.0 / d)
        part = jnp.sum(jnp.sum(err * err, axis=1, keepdims=True), axis=0, keepdims=True) * (0.5 / d)
        loss_ref[...] += jnp.broadcast_to(part, loss_ref.shape)

    return _call(
        body, name="loss_head", grid=(r_total // tm,),
        in_specs=[pl.BlockSpec((tm, d), lambda i: (i, 0))] * 2,
        out_specs=[pl.BlockSpec((8, LANE), lambda i: (0, 0)), pl.BlockSpec((tm, d), lambda i: (i, 0))],
        out_shape=[jax.ShapeDtypeStruct((8, LANE), F32), jax.ShapeDtypeStruct(y.shape, F32)],
        compiler_params=_cparams(("arbitrary",)),
    )(y, tgt)


def make_loss(first, last):
    @jax.custom_vjp
    def op(y, tgt):
        return _loss_call(y, tgt, first, last)[0][0, 0]

    def fwd(y, tgt):
        loss, dy = _loss_call(y, tgt, first, last)
        return loss[0, 0], (dy,)

    def bwd(res, ct):
        return res[0] * ct, jnp.zeros_like(res[0])

    op.defvjp(fwd, bwd)
    return op


def _pad_rows8(w):
    return jnp.concatenate([w, jnp.zeros((8 - w.shape[0], w.shape[1]), w.dtype)], axis=0)


def local_loss(wts, x, tgt):
    seq = x.shape[0]
    n_tok = N_META + seq
    t_pad = -(-n_tok // ROW_ALIGN) * ROW_ALIGN
    depth = wts["norm1_g"].shape[0]
    blk = _tile(t_pad, ATT_BLK, LANE)
    nb = t_pad // blk
    tm = _tile(t_pad, 256, 8)

    rms = rowop(_f_rmsnorm, "rmsnorm", (D_MODEL,), tm, out_dtypes=[BF16])
    qnorm = rowop(_f_qnorm, "fox_q_norm", (FOX_HD,), _tile(FOX_HEADS * t_pad, 2048, 8))
    knorm = rowop(_f_rmsnorm, "fox_k_norm", (FOX_HD,), _tile(FOX_HEADS * t_pad, 2048, 8))
    logsig = rowop(_f_logsig, "fox_log_forget", (FOX_HEADS,), tm)
    gdn_act = rowop([_f_gdn_q] * GDN_HEADS + [_f_gdn_k] * GDN_HEADS + [_f_gdn_v] * GDN_HEADS, "gdn_qkv_act",
                    (GDN_HD,), tm)
    gates = rowop(_f_gdn_gates, "gdn_gates", (GDN_HEADS, GDN_HEADS), tm)
    gdn_out = rowop([_f_gdn_out] * GDN_HEADS, "gdn_out_norm", (GDN_HD,), tm, out_dtypes=[BF16])
    merge = rowop(_f_merge, "branch_merge", (D_MODEL,), tm, out_dtypes=[BF16])
    residual = rowop(_f_residual, "residual_add", (D_MODEL,), tm, bc=(2,))
    residual_norm = rowop(_f_residual_norm, "residual_add_norm", (D_MODEL, D_MODEL), tm, bc=(2,),
                          out_dtypes=[F32, BF16])
    keep = (jnp.arange(t_pad)[:, None] < n_tok).astype(F32)
    conv4 = make_dwconv(GDN_CONV)
    conv3 = make_dwconv(FFN_CONV)
    loss_op = make_loss(N_META, n_tok)

    zeros = jnp.zeros((t_pad - n_tok, D_MODEL), F32)
    h_res = jnp.concatenate([wts["meta_tokens"], x, zeros], axis=0)
    tgt_rows = jnp.concatenate([jnp.zeros((N_META, D_MODEL), F32), tgt, zeros], axis=0)

    def heads(a):
        return a.reshape(t_pad, FOX_HEADS, FOX_HD).transpose(1, 0, 2).reshape(FOX_HEADS * t_pad, FOX_HD)

    h = rms((h_res,), (wts["norm1_g"][0][None],))[0]
    for l in range(depth):
        proj = mm(h, wts["w_in"][l])
        qn = qnorm((heads(proj[:, 0:512]),), (wts["fox_q_norm_g"][l][None],))[0]
        kn = knorm((heads(proj[:, 512:1024]),), (wts["fox_k_norm_g"][l][None],))[0]
        vh = heads(proj[:, 1024:1536])
        log_f = logsig((proj[:, 1536:1544],), (wts["fox_f_bias"][l][None],))[0]
        f_cum = cumsum_lanes(log_f.T)
        o_a = fox_attention(qn.reshape(FOX_HEADS, t_pad, FOX_HD), kn.reshape(FOX_HEADS, t_pad, FOX_HD),
                            vh.reshape(FOX_HEADS, t_pad, FOX_HD), f_cum[:, :, None],
                            f_cum.reshape(FOX_HEADS, nb, 1, blk))
        y_a = mm_bf16(o_a.transpose(2, 0, 1).reshape(t_pad, FOX_W).astype(BF16), wts["w_branch_a"][l])
        cv = conv4(proj[:, 1664:4736], _pad_rows8(wts["gdn_conv_w"][l]))
        qkv = gdn_act((cv,), ())[0]
        beta, gdec = gates((proj[:, 4736:4744], proj[:, 4744:4752]),
                           (wts["gdn_a_log"][l][None], wts["gdn_dt_bias"][l][None]))
        o_b = gated_delta(qkv, beta, gdec)
        o_b = gdn_out((o_b, proj[:, 4864:5888]), (wts["gdn_norm_g"][l][None],))[0]
        y_b = mm_bf16(o_b, wts["w_branch_b"][l])
        mixed = merge((proj[:, 5888:6912], proj[:, 6912:7936], y_a, y_b), ())[0]
        h_res, h = residual_norm((h_res, mm(mixed, wts["w_out"][l]), keep), (wts["norm2_g"][l][None],))
        up = conv3(mm(h, wts["w_up"][l]), _pad_rows8(wts["ffn_conv_w"][l]))
        act = glu(up)
        down = mm(act, wts["w_down"][l])
        if l + 1 < depth:
            h_res, h = residual_norm((h_res, down, keep), (wts["norm1_g"][l + 1][None],))
        else:
            h_res = residual((h_res, down, keep), ())[0]
    return loss_op(h_res, tgt_rows)


def pad_w_in(w):
    parts, pos = [], 0
    for src, width, dst in IN_SEGS:
        if dst > pos:
            parts.append(jnp.zeros(w.shape[:-1] + (dst - pos,), w.dtype))
        parts.append(w[..., src:src + width])
        pos = dst + width
    parts.append(jnp.zeros(w.shape[:-1] + (D_IN_PAD - pos,), w.dtype))
    return jnp.concatenate(parts, axis=-1)


def unpad_w_in(w):
    return jnp.concatenate([w[..., dst:dst + width] for _, width, dst in IN_SEGS], axis=-1)


ANY = pl.BlockSpec(memory_space=pl.ANY)
N_CHIPS = 4
N_DEV = 8
COMM_COLS = 1024
COMM_ROW_ALIGN = 512
COMM_ROW_ALIGN_SMALL = 32


def _place():
    return lax.axis_index("x"), lax.axis_index("y"), lax.axis_index("c")


def _other_chips(x, y):
    return [(1 - x, y), (x, 1 - y), (1 - x, 1 - y)]


def _remote(src, dst, send_sem, recv_sem, dev):
    return pltpu.make_async_remote_copy(src_ref=src, dst_ref=dst, send_sem=send_sem, recv_sem=recv_sem,
                                        device_id=dev, device_id_type=MESH)


def chip_all_gather(buf):
    rows, cols = buf.shape
    half = rows // 2

    def body(x_ref, out_ref, send_sems, recv_sems, pass_send, pass_recv):
        x, y, c = _place()
        me = 2 * x + y
        mine, other = pl.ds(c * half, half), pl.ds((1 - c) * half, half)
        sibling = (x, y, 1 - c)
        chips = _other_chips(x, y)
        started = []
        for k, (px, py) in enumerate(chips):
            cp = _remote(x_ref.at[mine], out_ref.at[me, mine], send_sems.at[k], recv_sems.at[k], (px, py, c))
            cp.start()
            started.append(cp)
        for k, (px, py) in enumerate(chips):
            landed = out_ref.at[2 * px + py, mine]
            _remote(landed, landed, send_sems.at[k], recv_sems.at[k], (px, py, c)).wait_recv()
            cp = _remote(landed, landed, pass_send.at[k], pass_recv.at[k], sibling)
            cp.start()
            started.append(cp)
        for k, (px, py) in enumerate(chips):
            passed = out_ref.at[2 * px + py, other]
            _remote(passed, passed, pass_send.at[k], pass_recv.at[k], sibling).wait_recv()
        for cp in started:
            cp.wait_send()

    got = _call(
        body, name="chip_all_gather", in_specs=[ANY], out_specs=ANY,
        out_shape=jax.ShapeDtypeStruct((N_CHIPS, rows, cols), buf.dtype),
        scratch_shapes=[pltpu.SemaphoreType.DMA((3,)), pltpu.SemaphoreType.DMA((3,)),
                        pltpu.SemaphoreType.DMA((3,)), pltpu.SemaphoreType.DMA((3,))],
    )(buf)
    me = 2 * lax.axis_index("x") + lax.axis_index("y")
    return lax.dynamic_update_slice(got, buf[None], (me, 0, 0))


def sibling_swap_halves(g4):
    n, rows, cols = g4.shape
    half = rows // 2

    def body(g_ref, got_ref, send_sem, recv_sem):
        x, y, c = _place()
        cp = _remote(g_ref.at[:, pl.ds((1 - c) * half, half), :], got_ref, send_sem, recv_sem, (x, y, 1 - c))
        cp.start()
        cp.wait()

    return _call(
        body, name="sibling_swap_halves", in_specs=[ANY], out_specs=ANY,
        out_shape=jax.ShapeDtypeStruct((n, half, cols), g4.dtype),
        scratch_shapes=[pltpu.SemaphoreType.DMA, pltpu.SemaphoreType.DMA],
    )(g4)


def add_own_half(g4, got, c):
    n, rows, cols = g4.shape
    half = rows // 2
    tm = _tile(half, 256, 16)
    nt = half // tm

    def body(c_ref, a_ref, b_ref, o_ref):
        o_ref[...] = (a_ref[...] + b_ref[...]).astype(o_ref.dtype)

    return _call(
        body, name="add_own_half",
        grid_spec=pltpu.PrefetchScalarGridSpec(
            num_scalar_prefetch=1, grid=(n, nt),
            in_specs=[pl.BlockSpec((1, tm, cols), lambda j, i, c_ref: (j, c_ref[0] * nt + i, 0)),
                      pl.BlockSpec((1, tm, cols), lambda j, i, c_ref: (j, i, 0))],
            out_specs=pl.BlockSpec((1, tm, cols), lambda j, i, c_ref: (j, i, 0))),
        out_shape=jax.ShapeDtypeStruct(got.shape, BF16),
        compiler_params=_cparams(("parallel", "parallel")),
    )(c.reshape(1).astype(jnp.int32), g4, got)


def chip_scatter(p4):
    n, rows, cols = p4.shape

    def body(p_ref, out_ref, send_sems, recv_sems):
        x, y, c = _place()
        me = 2 * x + y
        chips = _other_chips(x, y)
        started = []
        for k, (px, py) in enumerate(chips):
            cp = _remote(p_ref.at[2 * px + py], out_ref.at[me], send_sems.at[k], recv_sems.at[k], (px, py, c))
            cp.start()
            started.append(cp)
        for k, (px, py) in enumerate(chips):
            landed = out_ref.at[2 * px + py]
            _remote(landed, landed, send_sems.at[k], recv_sems.at[k], (px, py, c)).wait_recv()
        for cp in started:
            cp.wait_send()

    got = _call(
        body, name="chip_scatter", in_specs=[ANY], out_specs=ANY,
        out_shape=jax.ShapeDtypeStruct(p4.shape, p4.dtype),
        scratch_shapes=[pltpu.SemaphoreType.DMA((3,)), pltpu.SemaphoreType.DMA((3,))],
    )(p4)
    me = 2 * lax.axis_index("x") + lax.axis_index("y")
    return lax.dynamic_update_slice(got, lax.dynamic_slice_in_dim(p4, me, 1, axis=0), (me, 0, 0))


def sum_slots(a):
    n, rows, cols = a.shape
    tm = _tile(rows, 256, 16) if rows % 16 == 0 else rows

    def body(a_ref, o_ref):
        acc = a_ref[0].astype(F32)
        for k in range(1, n):
            acc = acc + a_ref[k].astype(F32)
        o_ref[...] = acc

    return _call(
        body, name="sum_slots_%d" % n, grid=(rows // tm,),
        in_specs=[pl.BlockSpec((n, tm, cols), lambda i: (0, i, 0))],
        out_specs=pl.BlockSpec((tm, cols), lambda i: (i, 0)),
        out_shape=jax.ShapeDtypeStruct((rows, cols), F32),
        compiler_params=_cparams(("parallel",)),
    )(a)


def sibling_join(s):
    half, cols = s.shape

    def body(s_ref, got_ref, send_sem, recv_sem):
        x, y, c = _place()
        cp = _remote(s_ref, got_ref, send_sem, recv_sem, (x, y, 1 - c))
        cp.start()
        cp.wait()

    got = _call(
        body, name="sibling_join", in_specs=[ANY], out_specs=ANY,
        out_shape=jax.ShapeDtypeStruct(s.shape, s.dtype),
        scratch_shapes=[pltpu.SemaphoreType.DMA, pltpu.SemaphoreType.DMA],
    )(s)
    c = lax.axis_index("c")
    out = jnp.zeros((2 * half, cols), s.dtype)
    out = lax.dynamic_update_slice(out, s, (c * half, 0))
    return lax.dynamic_update_slice(out, got, ((1 - c) * half, 0))


def all_devices_gather(buf):
    rows, cols = buf.shape

    def body(b_ref, out_ref, send_sems, recv_sems, local_sem):
        x, y, c = _place()
        me = 4 * x + 2 * y + c
        local = pltpu.make_async_copy(b_ref, out_ref.at[me], local_sem)
        local.start()
        peers = [((x + dx) % 2, (y + dy) % 2, (c + dc) % 2)
                 for dx in (0, 1) for dy in (0, 1) for dc in (0, 1) if dx + dy + dc > 0]
        started = []
        for k, peer in enumerate(peers):
            cp = _remote(b_ref, out_ref.at[me], send_sems.at[k], recv_sems.at[k], peer)
            cp.start()
            started.append(cp)
        for k, (px, py, pc) in enumerate(peers):
            landed = out_ref.at[4 * px + 2 * py + pc]
            _remote(landed, landed, send_sems.at[k], recv_sems.at[k], (px, py, pc)).wait_recv()
        for cp in started:
            cp.wait_send()
        local.wait()

    return _call(
        body, name="all_devices_gather", in_specs=[ANY], out_specs=ANY,
        out_shape=jax.ShapeDtypeStruct((N_DEV, rows, cols), buf.dtype),
        scratch_shapes=[pltpu.SemaphoreType.DMA((7,)), pltpu.SemaphoreType.DMA((7,)), pltpu.SemaphoreType.DMA],
    )(buf)


def adamw(w, g, m, v):
    shape = w.shape
    w2, g2, m2, v2 = [a.reshape(-1, shape[-1]) for a in (w, g, m, v)]
    rows, cols = w2.shape
    tm = _tile(rows, 256, 8) if rows % 8 == 0 else rows

    def body(w_ref, g_ref, m_ref, v_ref, d_ref, nm_ref, nv_ref):
        gv = g_ref[...]
        nm = ADAM_B1 * m_ref[...] + (1.0 - ADAM_B1) * gv
        nv = ADAM_B2 * v_ref[...] + (1.0 - ADAM_B2) * (gv * gv)
        m_hat = nm / (1.0 - ADAM_B1 ** ADAM_STEP)
        v_hat = nv / (1.0 - ADAM_B2 ** ADAM_STEP)
        d_ref[...] = -ADAM_LR * (m_hat / (jnp.sqrt(v_hat) + ADAM_EPS) + ADAM_WD * w_ref[...])
        nm_ref[...] = nm
        nv_ref[...] = nv

    spec = pl.BlockSpec((tm, cols), lambda i: (i, 0))
    outs = _call(
        body, name="adamw", grid=(rows // tm,), in_specs=[spec] * 4, out_specs=[spec] * 3,
        out_shape=[jax.ShapeDtypeStruct((rows, cols), F32)] * 3,
        compiler_params=_cparams(("parallel",)),
    )(w2, g2, m2, v2)
    return [o.reshape(shape) for o in outs]


WEIGHTS = ("meta_tokens", "norm1_g", "w_in", "fox_f_bias", "fox_q_norm_g", "fox_k_norm_g", "gdn_conv_w",
           "gdn_a_log", "gdn_dt_bias", "gdn_norm_g", "w_branch_a", "w_branch_b", "w_out", "norm2_g", "w_up",
           "ffn_conv_w", "w_down")
SHARD_AXIS = {"meta_tokens": -1, "w_in": -1, "gdn_conv_w": -1, "w_branch_a": -1, "w_branch_b": -2, "w_out": -2,
              "w_up": -1, "ffn_conv_w": -1, "w_down": -2}
MATMUL_WEIGHTS = ("w_in", "w_branch_a", "w_branch_b", "w_out", "w_up", "w_down")
SMALL_SHARDED = ("meta_tokens", "gdn_conv_w", "ffn_conv_w")
REPLICATED = tuple(n for n in WEIGHTS if n not in SHARD_AXIS)


def _pack(arrays, dtype, row_align):
    flat = jnp.concatenate([a.reshape(-1).astype(dtype) for a in arrays])
    block = row_align * COMM_COLS
    total = -(-flat.shape[0] // block) * block
    flat = jnp.concatenate([flat, jnp.zeros((total - flat.shape[0],), dtype)])
    return flat.reshape(-1, COMM_COLS)


def _unpack(buf, shapes):
    flat, out, pos = buf.reshape(-1), [], 0
    for shape in shapes:
        size = 1
        for d in shape:
            size *= d
        out.append(flat[pos:pos + size].reshape(shape))
        pos += size
    return out


def _gather_full(shards, names, dtype, row_align):
    got = chip_all_gather(_pack([shards[n] for n in names], dtype, row_align))
    per_chip = [_unpack(got[j], [shards[n].shape for n in names]) for j in range(N_CHIPS)]
    return {n: jnp.concatenate([per_chip[j][i] for j in range(N_CHIPS)], axis=SHARD_AXIS[n]).astype(F32)
            for i, n in enumerate(names)}


def _shard_of(full, name, j):
    axis = SHARD_AXIS[name] % full.ndim
    size = full.shape[axis] // N_CHIPS
    return lax.slice_in_dim(full, j * size, (j + 1) * size, axis=axis)


def kernel(x, meta_tokens, norm1_g, w_in, fox_f_bias, fox_q_norm_g, fox_k_norm_g, gdn_conv_w, gdn_a_log, gdn_dt_bias, gdn_norm_g, w_branch_a, w_branch_b, w_out, norm2_g, w_up, ffn_conv_w, w_down, loss_target, m_meta_tokens, m_norm1_g, m_w_in, m_fox_f_bias, m_fox_q_norm_g, m_fox_k_norm_g, m_gdn_conv_w, m_gdn_a_log, m_gdn_dt_bias, m_gdn_norm_g, m_w_branch_a, m_w_branch_b, m_w_out, m_norm2_g, m_w_up, m_ffn_conv_w, m_w_down, v_meta_tokens, v_norm1_g, v_w_in, v_fox_f_bias, v_fox_q_norm_g, v_fox_k_norm_g, v_gdn_conv_w, v_gdn_a_log, v_gdn_dt_bias, v_gdn_norm_g, v_w_branch_a, v_w_branch_b, v_w_out, v_norm2_g, v_w_up, v_ffn_conv_w, v_w_down):
    w_loc = dict(zip(WEIGHTS, (meta_tokens, norm1_g, w_in, fox_f_bias, fox_q_norm_g, fox_k_norm_g, gdn_conv_w,
                               gdn_a_log, gdn_dt_bias, gdn_norm_g, w_branch_a, w_branch_b, w_out, norm2_g, w_up,
                               ffn_conv_w, w_down)))
    m_loc = dict(zip(WEIGHTS, (m_meta_tokens, m_norm1_g, m_w_in, m_fox_f_bias, m_fox_q_norm_g, m_fox_k_norm_g,
                               m_gdn_conv_w, m_gdn_a_log, m_gdn_dt_bias, m_gdn_norm_g, m_w_branch_a, m_w_branch_b,
                               m_w_out, m_norm2_g, m_w_up, m_ffn_conv_w, m_w_down)))
    v_loc = dict(zip(WEIGHTS, (v_meta_tokens, v_norm1_g, v_w_in, v_fox_f_bias, v_fox_q_norm_g, v_fox_k_norm_g,
                               v_gdn_conv_w, v_gdn_a_log, v_gdn_dt_bias, v_gdn_norm_g, v_w_branch_a, v_w_branch_b,
                               v_w_out, v_norm2_g, v_w_up, v_ffn_conv_w, v_w_down)))
    c = lax.axis_index("c")

    full = {n: w_loc[n] for n in REPLICATED}
    full.update(_gather_full(w_loc, MATMUL_WEIGHTS, BF16, COMM_ROW_ALIGN))
    full.update(_gather_full(w_loc, SMALL_SHARDED, F32, COMM_ROW_ALIGN_SMALL))
    full["w_in"] = pad_w_in(full["w_in"])

    loss, (g_full, g_x) = jax.value_and_grad(local_loss, argnums=(0, 1))(full, x[0], loss_target[0])
    g_full = dict(g_full)
    g_full["w_in"] = unpad_w_in(g_full["w_in"])

    sharded = MATMUL_WEIGHTS + SMALL_SHARDED
    g4 = jnp.stack([_pack([_shard_of(g_full[n], n, j) for n in sharded], F32, COMM_ROW_ALIGN)
                    for j in range(N_CHIPS)])
    pair_sum = add_own_half(g4, sibling_swap_halves(g4), c)
    g_shard = sibling_join(sum_slots(chip_scatter(pair_sum)))
    grads = dict(zip(sharded, _unpack(g_shard, [w_loc[n].shape for n in sharded])))
    g_rep = sum_slots(all_devices_gather(_pack([g_full[n] for n in REPLICATED], F32, 8)))
    grads.update(zip(REPLICATED, _unpack(g_rep, [w_loc[n].shape for n in REPLICATED])))

    loss = lax.psum(loss, ("x", "y", "c"))
    upd = {n: adamw(w_loc[n], grads[n], m_loc[n], v_loc[n]) for n in WEIGHTS}
    return (loss, g_x[None], *[grads[n] for n in WEIGHTS], *[upd[n][0] for n in WEIGHTS],
            *[upd[n][1] for n in WEIGHTS], *[upd[n][2] for n in WEIGHTS])
```

```python
import functools

import jax
import jax.numpy as jnp
from jax import lax
from jax.experimental import pallas as pl
from jax.experimental.pallas import tpu as pltpu

F32 = jnp.float32
BF16 = jnp.bfloat16
HI = lax.Precision.HIGHEST
MESH = pl.DeviceIdType.MESH

D_MODEL = 1024
N_META = 16
EPS = 1e-6
FOX_HEADS, FOX_HD = 8, 64
FOX_W = FOX_HEADS * FOX_HD
GDN_HEADS, GDN_HD, GDN_CHUNK, GDN_CONV = 8, 128, 64, 4
GDN_W = GDN_HEADS * GDN_HD
D_FF = 2816
FFN_CONV = 3
D_IN = 7704
D_IN_PAD = 8192
IN_SEGS = ((0, 1536, 0), (1536, 8, 1536), (1544, 3072, 1664), (4616, 16, 4736), (4632, 1024, 4864), (5656, 2048, 5888))
ROW_ALIGN = 256
ATT_BLK = 256
VMEM_LIMIT = 48 * 1024 * 1024
LANE = 128

ADAM_LR, ADAM_B1, ADAM_B2, ADAM_EPS, ADAM_WD, ADAM_STEP = 0.001, 0.9, 0.999, 1e-08, 0.01, 10


def _call(body, **kw):
    return pl.pallas_call(body, **kw)


def _tile(n, target, mult):
    best, t = None, mult
    while t <= min(n, target):
        if n % t == 0:
            best = t
        t += mult
    assert best is not None, (n, target, mult)
    return best


def _cparams(sem):
    return pltpu.CompilerParams(dimension_semantics=sem, vmem_limit_bytes=VMEM_LIMIT)


def _raw_dot(a, b, ca, cb, precise):
    dims = (((ca,), (cb,)), ((), ()))
    a_hi, b_hi = a.astype(BF16), b.astype(BF16)
    out = lax.dot_general(a_hi, b_hi, dims, preferred_element_type=F32)
    if precise:
        a_lo = (a - a_hi.astype(F32)).astype(BF16)
        b_lo = (b - b_hi.astype(F32)).astype(BF16)
        out = out + (lax.dot_general(a_hi, b_lo, dims, preferred_element_type=F32)
                     + lax.dot_general(a_lo, b_hi, dims, preferred_element_type=F32))
    return out


def _make_dot(ca, cb, precise):
    @jax.custom_vjp
    def f(a, b):
        return _raw_dot(a, b, ca, cb, precise)

    def fwd(a, b):
        return f(a, b), (a, b)

    def bwd(res, ct):
        a, b = res
        if ca == 1:
            da = _raw_dot(ct, b, 1, 1 if cb == 0 else 0, precise)
        else:
            da = _raw_dot(b, ct, 1 if cb == 0 else 0, 1, precise)
        if cb == 0:
            db = _raw_dot(a, ct, 0 if ca == 1 else 1, 0, precise)
        else:
            db = _raw_dot(ct, a, 0, 0 if ca == 1 else 1, precise)
        return da, db

    f.defvjp(fwd, bwd)
    return f


_DOTS = {(ca, cb, p): _make_dot(ca, cb, p) for ca in (0, 1) for cb in (0, 1) for p in (False, True)}


def _dot(a, b, ca=1, cb=0, precise=False):
    return _DOTS[(ca, cb, precise)](a, b)


def _mm_call(a, b, name, ta=False, tb=False, out_dtype=F32):
    k, m = a.shape if ta else a.shape[::-1]
    n, kb = b.shape if tb else b.shape[::-1]
    assert k == kb, (a.shape, b.shape)
    tm = _tile(m, 768, LANE if ta else 16)
    tn = _tile(n, 1408, LANE)
    tk = _tile(k, 1408, LANE)
    nk = k // tk
    dims = (((0 if ta else 1,), (1 if tb else 0,)), ((), ()))

    def body(a_ref, b_ref, o_ref, *scratch):
        part = lax.dot_general(a_ref[...], b_ref[...], dims, preferred_element_type=F32)
        if nk == 1:
            o_ref[...] = part.astype(o_ref.dtype)
            return
        acc_ref = scratch[0]
        kk = pl.program_id(2)

        @pl.when(kk == 0)
        def _():
            acc_ref[...] = part

        @pl.when(kk > 0)
        def _():
            acc_ref[...] += part

        @pl.when(kk == nk - 1)
        def _():
            o_ref[...] = acc_ref[...].astype(o_ref.dtype)

    return _call(
        body, name=name, grid=(m // tm, n // tn, nk),
        in_specs=[pl.BlockSpec((tk, tm), lambda i, j, kk: (kk, i)) if ta else
                  pl.BlockSpec((tm, tk), lambda i, j, kk: (i, kk)),
                  pl.BlockSpec((tn, tk), lambda i, j, kk: (j, kk)) if tb else
                  pl.BlockSpec((tk, tn), lambda i, j, kk: (kk, j))],
        out_specs=pl.BlockSpec((tm, tn), lambda i, j, kk: (i, j)),
        out_shape=jax.ShapeDtypeStruct((m, n), out_dtype),
        scratch_shapes=[pltpu.VMEM((tm, tn), F32)] if nk > 1 else [],
        compiler_params=_cparams(("parallel", "parallel", "arbitrary")),
    )(a, b)


def _make_mm(out_dtype):
    @jax.custom_vjp
    def op(a, w):
        return _mm_call(a.astype(BF16), w.astype(BF16), "mm_fwd", out_dtype=out_dtype)

    def fwd(a, w):
        a_b, w_b = a.astype(BF16), w.astype(BF16)
        return _mm_call(a_b, w_b, "mm_fwd", out_dtype=out_dtype), (a_b, w_b, jnp.zeros((), a.dtype))

    def bwd(res, ct):
        a_b, w_b, like_a = res
        ct_b = ct.astype(BF16)
        return (_mm_call(ct_b, w_b, "mm_dx", tb=True, out_dtype=like_a.dtype),
                _mm_call(a_b, ct_b, "mm_dw", ta=True))

    op.defvjp(fwd, bwd)
    return op


mm = _make_mm(F32)
mm_bf16 = _make_mm(BF16)


def _rows_specs(rows, tm, ncb, bc):
    specs = []
    for idx, r in enumerate(rows):
        if idx in bc:
            specs.append(pl.BlockSpec((tm, r.shape[1]), lambda i, j: (i, 0)))
        else:
            specs.append(pl.BlockSpec((tm, r.shape[1] // ncb), lambda i, j: (i, j)))
    return specs


def _param_specs(params):
    return [pl.BlockSpec(p.shape, lambda i, j: (0, 0)) for p in params]


def _group_slices(refs, groups, g, whole):
    out = []
    for idx, r in enumerate(refs):
        w = r.shape[1] // groups
        out.append((r[...] if idx in whole else r[:, g * w:(g + 1) * w]).astype(F32))
    return out


def _rows_fwd_call(fns, rows, params, outs, tm, ncb, bc, name, out_dtypes=None):
    r_total = rows[0].shape[0]
    nr, groups = len(rows), len(fns)
    out_dtypes = out_dtypes or [F32] * len(outs)

    def body(*refs):
        pvals = [r[...] for r in refs[nr:nr + len(params)]]
        for g, fn in enumerate(fns):
            res = fn(*_group_slices(refs[:nr], groups, g, bc), *pvals)
            for o_ref, val in zip(refs[nr + len(params):], res):
                w = o_ref.shape[1] // groups
                o_ref[:, g * w:(g + 1) * w] = val.astype(o_ref.dtype)

    return _call(
        body, name=name, grid=(r_total // tm, ncb),
        in_specs=_rows_specs(rows, tm, ncb, bc) + _param_specs(params),
        out_specs=[pl.BlockSpec((tm, w * groups), lambda i, j: (i, j)) for w in outs],
        out_shape=[jax.ShapeDtypeStruct((r_total, w * groups * ncb), dt) for w, dt in zip(outs, out_dtypes)],
        compiler_params=_cparams(("parallel", "parallel")),
    )(*rows, *params)


def _rows_bwd_call(fns, rows, params, cts, tm, ncb, bc, name):
    r_total = rows[0].shape[0]
    nr, npar, nct, groups = len(rows), len(params), len(cts), len(fns)

    def body(*refs):
        i, j = pl.program_id(0), pl.program_id(1)
        pvals = [r[...] for r in refs[nr:nr + npar]]
        ct_refs = refs[nr + npar:nr + npar + nct]
        d_refs = refs[nr + npar + nct:]
        shared = {idx: None for idx in list(bc) + list(range(nr, nr + npar))}
        for g, fn in enumerate(fns):
            _, vjp = jax.vjp(lambda *a, fn=fn: tuple(fn(*a)), *_group_slices(refs[:nr], groups, g, bc), *pvals)
            grads = vjp(tuple(_group_slices(ct_refs, groups, g, ())))
            for idx in range(nr + npar):
                if idx in shared:
                    shared[idx] = grads[idx] if shared[idx] is None else shared[idx] + grads[idx]
                else:
                    w = d_refs[idx].shape[1] // groups
                    d_refs[idx][:, g * w:(g + 1) * w] = grads[idx].astype(d_refs[idx].dtype)
        for idx, total in shared.items():
            first = (j == 0) if idx < nr else ((i == 0) & (j == 0))

            @pl.when(first)
            def _(idx=idx):
                d_refs[idx][...] = jnp.zeros_like(d_refs[idx])
            d_refs[idx][...] += total

    ct_specs = [pl.BlockSpec((tm, c.shape[1] // ncb), lambda i, j: (i, j)) for c in cts]
    return _call(
        body, name=name + "_bwd", grid=(r_total // tm, ncb),
        in_specs=_rows_specs(rows, tm, ncb, bc) + _param_specs(params) + ct_specs,
        out_specs=_rows_specs(rows, tm, ncb, bc) + _param_specs(params),
        out_shape=[jax.ShapeDtypeStruct(a.shape, a.dtype) for a in list(rows) + list(params)],
        compiler_params=_cparams(("arbitrary", "arbitrary")),
    )(*rows, *params, *cts)


def rowop(fn, name, outs, tm, ncb=1, bc=(), out_dtypes=None):
    fns = list(fn) if isinstance(fn, (list, tuple)) else [fn]

    @jax.custom_vjp
    def op(rows, params):
        return tuple(_rows_fwd_call(fns, rows, params, outs, tm, ncb, bc, name, out_dtypes))

    def fwd(rows, params):
        return op(rows, params), (rows, params)

    def bwd(res, cts):
        rows, params = res
        d = _rows_bwd_call(fns, rows, params, cts, tm, ncb, bc, name)
        return tuple(d[:len(rows)]), tuple(d[len(rows):])

    op.defvjp(fwd, bwd)
    return op


def _sigmoid(x):
    return 1.0 / (1.0 + jnp.exp(-x))


def _silu(x):
    return x * _sigmoid(x)


def _softplus(x):
    return jnp.maximum(x, 0.0) + jnp.log(1.0 + jnp.exp(-jnp.abs(x)))


def _f_rmsnorm(x, g):
    return (x * lax.rsqrt(jnp.mean(x * x, axis=-1, keepdims=True) + EPS) * g,)


def _f_qnorm(x, g):
    return (x * lax.rsqrt(jnp.mean(x * x, axis=-1, keepdims=True) + EPS) * (g * (FOX_HD ** -0.5)),)


def _f_logsig(x, b):
    return (-_softplus(-(x + b)),)


def _f_gdn_q(x):
    y = _silu(x)
    return (y * lax.rsqrt(jnp.sum(y * y, axis=-1, keepdims=True) + EPS) * (GDN_HD ** -0.5),)


def _f_gdn_k(x):
    y = _silu(x)
    return (y * lax.rsqrt(jnp.sum(y * y, axis=-1, keepdims=True) + EPS),)


def _f_gdn_v(x):
    return (_silu(x),)


def _f_gdn_gates(bl, al, a_log, dt_bias):
    return _sigmoid(bl), -jnp.exp(a_log) * _softplus(al + dt_bias)


def _f_gdn_out(o, z, g):
    return (o * lax.rsqrt(jnp.mean(o * o, axis=-1, keepdims=True) + EPS) * g * _silu(z),)


def _f_merge(g0, g1, ya, yb):
    return (_sigmoid(g0) * ya + _sigmoid(g1) * yb,)


def _f_residual(a, b, keep):
    return ((a + b) * keep,)


def _f_residual_norm(a, b, keep, g):
    r = (a + b) * keep
    return r, _f_rmsnorm(r, g)[0]


def _f_glu(a, b):
    return (_silu(a) * b,)


def _glu_call(up, ct):
    t_total, two_f = up.shape
    f = two_f // 2
    tm = _tile(t_total, 128, 16)
    wc = _tile(f, 1408, LANE)

    def body(*refs):
        up_ref, out_ref = refs[0], refs[-1]
        for c0 in range(0, f, wc):
            a, b = up_ref[:, c0:c0 + wc].astype(F32), up_ref[:, f + c0:f + c0 + wc].astype(F32)
            if ct is None:
                out_ref[:, c0:c0 + wc] = _f_glu(a, b)[0].astype(out_ref.dtype)
            else:
                _, vjp = jax.vjp(_f_glu, a, b)
                da, db = vjp((refs[1][:, c0:c0 + wc].astype(F32),))
                out_ref[:, c0:c0 + wc] = da.astype(out_ref.dtype)
                out_ref[:, f + c0:f + c0 + wc] = db.astype(out_ref.dtype)

    wide = pl.BlockSpec((tm, two_f), lambda i: (i, 0))
    narrow = pl.BlockSpec((tm, f), lambda i: (i, 0))
    return _call(
        body, name="ffn_glu" if ct is None else "ffn_glu_bwd", grid=(t_total // tm,),
        in_specs=[wide] if ct is None else [wide, narrow], out_specs=narrow if ct is None else wide,
        out_shape=jax.ShapeDtypeStruct((t_total, f), BF16) if ct is None else jax.ShapeDtypeStruct(up.shape, up.dtype),
        compiler_params=_cparams(("parallel",)),
    )(*((up,) if ct is None else (up, ct)))


@jax.custom_vjp
def glu(up):
    return _glu_call(up, None)


glu.defvjp(lambda up: (glu(up), up), lambda up, ct: (_glu_call(up, ct),))


def _shift_down(x, halo, s, row8):
    rx = pltpu.roll(x, s, 0)
    top = jnp.where(row8 < s, pltpu.roll(halo, s, 0), rx[:8])
    return jnp.concatenate([top, rx[8:]], axis=0)


def _shift_up(x, nxt, s, row8):
    tm = x.shape[0]
    rx = pltpu.roll(x, tm - s, 0)
    bot = jnp.where(row8 >= 8 - s, pltpu.roll(nxt, 8 - s, 0), rx[tm - 8:])
    return jnp.concatenate([rx[:tm - 8], bot], axis=0)


def _conv_tiles(r_total, c_total):
    return _tile(r_total, 768, 8), _tile(c_total, 1408, LANE)


def _halo_rows(dtype):
    return 16 if dtype == BF16 else 8


def _conv_fwd_call(x, w8, k_taps):
    r_total, c_total = x.shape
    tm, tc = _conv_tiles(r_total, c_total)
    hr = _halo_rows(x.dtype)
    hb = tm // hr

    def body(x_ref, halo_ref, w_ref, y_ref):
        i = pl.program_id(1)
        xt = x_ref[...].astype(F32)
        halo = jnp.where(i > 0, halo_ref[...].astype(F32)[hr - 8:hr], 0.0)
        row8 = lax.broadcasted_iota(jnp.int32, (8, tc), 0)
        acc = w_ref[k_taps - 1:k_taps, :] * xt
        for k in range(k_taps - 1):
            acc += w_ref[k:k + 1, :] * _shift_down(xt, halo, k_taps - 1 - k, row8)
        y_ref[...] = acc.astype(y_ref.dtype)

    return _call(
        body, name="dwconv_fwd", grid=(c_total // tc, r_total // tm),
        in_specs=[pl.BlockSpec((tm, tc), lambda c, i: (i, c)),
                  pl.BlockSpec((hr, tc), lambda c, i: (jnp.maximum(i * hb - 1, 0), c)),
                  pl.BlockSpec((8, tc), lambda c, i: (0, c))],
        out_specs=pl.BlockSpec((tm, tc), lambda c, i: (i, c)),
        out_shape=jax.ShapeDtypeStruct(x.shape, x.dtype),
        compiler_params=_cparams(("parallel", "parallel")),
    )(x, x, w8)


def _conv_bwd_call(x, w8, dy, k_taps):
    r_total, c_total = x.shape
    tm, tc = _conv_tiles(r_total, c_total)
    hr = _halo_rows(dy.dtype)
    hb = tm // hr
    n_i = r_total // tm

    def body(x_ref, w_ref, dy_ref, nxt_ref, dx_ref, dw_ref):
        i = pl.program_id(1)
        xt, dyt = x_ref[...].astype(F32), dy_ref[...].astype(F32)
        nxt = jnp.where(i < n_i - 1, nxt_ref[...].astype(F32)[0:8], 0.0)
        row8 = lax.broadcasted_iota(jnp.int32, (8, tc), 0)
        dx = w_ref[k_taps - 1:k_taps, :] * dyt
        upd = jnp.where(row8 == k_taps - 1, jnp.sum(dyt * xt, axis=0, keepdims=True), 0.0)
        for k in range(k_taps - 1):
            dy_ahead = _shift_up(dyt, nxt, k_taps - 1 - k, row8)
            dx += w_ref[k:k + 1, :] * dy_ahead
            upd = jnp.where(row8 == k, jnp.sum(dy_ahead * xt, axis=0, keepdims=True), upd)
        dx_ref[...] = dx.astype(dx_ref.dtype)

        @pl.when(i == 0)
        def _():
            dw_ref[...] = jnp.zeros_like(dw_ref)

        dw_ref[...] += upd

    return _call(
        body, name="dwconv_bwd", grid=(c_total // tc, n_i),
        in_specs=[pl.BlockSpec((tm, tc), lambda c, i: (i, c)),
                  pl.BlockSpec((8, tc), lambda c, i: (0, c)),
                  pl.BlockSpec((tm, tc), lambda c, i: (i, c)),
                  pl.BlockSpec((hr, tc), lambda c, i: (jnp.minimum((i + 1) * hb, r_total // hr - 1), c))],
        out_specs=[pl.BlockSpec((tm, tc), lambda c, i: (i, c)), pl.BlockSpec((8, tc), lambda c, i: (0, c))],
        out_shape=[jax.ShapeDtypeStruct(x.shape, x.dtype), jax.ShapeDtypeStruct(w8.shape, F32)],
        compiler_params=_cparams(("parallel", "arbitrary")),
    )(x, w8, dy, dy)


def make_dwconv(k_taps):
    @jax.custom_vjp
    def op(x, w8):
        return _conv_fwd_call(x, w8, k_taps)

    def fwd(x, w8):
        return op(x, w8), (x, w8)

    def bwd(res, dy):
        x, w8 = res
        dx, dw = _conv_bwd_call(x, w8, dy, k_taps)
        return dx, dw

    op.defvjp(fwd, bwd)
    return op


def _cumsum_call(x, reverse):
    h, t_total = x.shape
    tb = _tile(t_total, 256, LANE)
    nb = t_total // tb

    def body(x_ref, o_ref, carry_ref):
        i = pl.program_id(0)

        @pl.when(i == 0)
        def _():
            carry_ref[...] = jnp.zeros_like(carry_ref)

        r = lax.broadcasted_iota(jnp.int32, (tb, tb), 0)
        c = lax.broadcasted_iota(jnp.int32, (tb, tb), 1)
        tri = jnp.where((r >= c) if reverse else (r <= c), 1.0, 0.0).astype(F32)
        xv = x_ref[...]
        carry = jnp.max(carry_ref[...], axis=1, keepdims=True)
        o_ref[...] = _raw_dot(xv, tri, 1, 0, True) + carry
        carry_ref[...] = jnp.broadcast_to(carry + jnp.sum(xv, axis=1, keepdims=True), carry_ref.shape)

    imap = (lambda i: (0, nb - 1 - i)) if reverse else (lambda i: (0, i))
    return _call(
        body, name="cumsum_rev" if reverse else "cumsum", grid=(nb,),
        in_specs=[pl.BlockSpec((h, tb), imap)], out_specs=pl.BlockSpec((h, tb), imap),
        out_shape=jax.ShapeDtypeStruct(x.shape, F32), scratch_shapes=[pltpu.VMEM((h, LANE), F32)],
        compiler_params=_cparams(("arbitrary",)),
    )(x)


@jax.custom_vjp
def cumsum_lanes(x):
    return _cumsum_call(x, False)


cumsum_lanes.defvjp(lambda x: (cumsum_lanes(x), None), lambda _, ct: (_cumsum_call(ct, True),))


NEG_BIG = -1e30


def _attn_sub_tiles(nb):
    return max(s for s in (3, 2, 1) if nb % s == 0)


EXP_ZERO = -100.0
SMEM = pl.BlockSpec(memory_space=pltpu.SMEM)


def _max_row_norm_sq(x):
    h_total, t_total, hd = x.shape
    tb = _tile(t_total, 2816, 8)

    def body(x_ref, o_ref):
        @pl.when(pl.program_id(1) == 0)
        def _():
            o_ref[...] = jnp.zeros_like(o_ref)

        xv = x_ref[0]
        top = jnp.max(jnp.sum(xv * xv, axis=1, keepdims=True), axis=0, keepdims=True)
        o_ref[0] = jnp.maximum(o_ref[0], top)

    return _call(
        body, name="max_row_norm", grid=(h_total, t_total // tb),
        in_specs=[pl.BlockSpec((1, tb, hd), lambda h, i: (h, i, 0))],
        out_specs=pl.BlockSpec((1, 8, LANE), lambda h, i: (h, 0, 0)),
        out_shape=jax.ShapeDtypeStruct((h_total, 8, LANE), F32),
        compiler_params=_cparams(("parallel", "arbitrary")),
    )(x)


def _attn_skip_tables(q, k, f_row):
    bound = 2.0 * jnp.sqrt(_max_row_norm_sq(q)[:, 0, :1] * _max_row_norm_sq(k)[:, 0, :1])
    return EXP_ZERO - bound, f_row[:, :, 0, 0], f_row[:, :, 0, -1]


def _attn_fwd_call(q, k, v, f_col, f_row, tables):
    h_total, t_total, hd = q.shape
    blk = f_row.shape[-1]
    nb = t_total // blk
    nsub = _attn_sub_tiles(nb)
    tq = nsub * blk

    def body(thr_ref, first_ref, last_ref, q_ref, k_ref, vt_ref, fc_ref, fr_ref, o_ref, lse_ref):
        h = pl.program_id(0)
        i = pl.program_id(1)
        gap_needed = thr_ref[h, 0]
        f_tile = first_ref[h, i * nsub]
        j_start = lax.while_loop(lambda j: (j < i * nsub) & (f_tile - last_ref[h, j] < gap_needed),
                                 lambda j: j + 1, 0)
        r = lax.broadcasted_iota(jnp.int32, (blk, blk), 0)
        c = lax.broadcasted_iota(jnp.int32, (blk, blk), 1)
        qs = [q_ref[0, s * blk:(s + 1) * blk, :].astype(BF16) for s in range(nsub)]
        fqs = [fr_ref[0, i * nsub + s] for s in range(nsub)]

        def load_kv(j):
            off = pl.multiple_of(j * blk, blk)
            return k_ref[0, pl.ds(off, blk), :], vt_ref[0, j], fc_ref[0, pl.ds(off, blk), :]

        def tile(kv, s, carry, diagonal):
            kj, vtj, fk = kv
            m, l, acc = carry
            st = _raw_dot(kj, qs[s], 1, 1, False) + fqs[s] - fk
            if diagonal:
                st = jnp.where(r <= c, st, NEG_BIG)
            m_new = jnp.maximum(m, jnp.max(st, axis=0, keepdims=True))
            p = jnp.exp(st - m_new)
            alpha = jnp.exp(m - m_new)
            l = alpha * l + jnp.sum(p, axis=0, keepdims=True)
            acc = alpha * acc + _raw_dot(vtj, p, 1, 0, False)
            return m_new, l, acc

        def below_diagonal(j, carry):
            kv = load_kv(j)
            return tuple(tile(kv, s, carry[s], False) for s in range(nsub))

        init = tuple((jnp.full((1, blk), NEG_BIG, F32), jnp.zeros((1, blk), F32), jnp.zeros((hd, blk), F32))
                     for _ in range(nsub))
        carry = list(lax.fori_loop(j_start, i * nsub, below_diagonal, init))
        for d in range(nsub):
            kv = load_kv(i * nsub + d)
            for s in range(d, nsub):
                carry[s] = tile(kv, s, carry[s], s == d)
        for s, (m, l, acc) in enumerate(carry):
            o_ref[0, :, s * blk:(s + 1) * blk] = acc / l
            lse_ref[0, s] = m + jnp.log(l)

    vt = v.reshape(h_total, nb, blk, hd).transpose(0, 1, 3, 2).astype(BF16)
    return _call(
        body, name="fox_fwd", grid=(h_total, nb // nsub),
        in_specs=[SMEM, SMEM, SMEM,
                  pl.BlockSpec((1, tq, hd), lambda h, i: (h, i, 0)),
                  pl.BlockSpec((1, t_total, hd), lambda h, i: (h, 0, 0)),
                  pl.BlockSpec((1, nb, hd, blk), lambda h, i: (h, 0, 0, 0)),
                  pl.BlockSpec((1, t_total, 1), lambda h, i: (h, 0, 0)),
                  pl.BlockSpec((1, nb, 1, blk), lambda h, i: (h, 0, 0, 0))],
        out_specs=[pl.BlockSpec((1, hd, tq), lambda h, i: (h, 0, i)),
                   pl.BlockSpec((1, nsub, 1, blk), lambda h, i: (h, i, 0, 0))],
        out_shape=[jax.ShapeDtypeStruct((h_total, hd, t_total), F32), jax.ShapeDtypeStruct(f_row.shape, F32)],
        compiler_params=_cparams(("parallel", "parallel")),
    )(*tables, q, k.astype(BF16), vt, f_col, f_row)


def _attn_bwd_call(q, k, v, f_col, f_row, tables, lse_row, delta_row, do_blk):
    h_total, t_total, hd = q.shape
    blk = f_row.shape[-1]
    nb = t_total // blk
    nsub = _attn_sub_tiles(nb)
    tkv = nsub * blk

    def body(thr_ref, first_ref, last_ref, q_ref, do_ref, k_ref, v_ref, fc_ref, fr_ref, lse_ref, dl_ref,
             dq_ref, dk_ref, dv_ref, dfk_ref, dfq_ref):
        h = pl.program_id(0)
        j = pl.program_id(1)
        gap_needed = thr_ref[h, 0]
        f_tile = last_ref[h, j * nsub + nsub - 1]
        i_stop = lax.while_loop(lambda i: (i < nb) & (first_ref[h, jnp.minimum(i, nb - 1)] - f_tile >= gap_needed),
                                lambda i: i + 1, (j + 1) * nsub)

        @pl.when(j == 0)
        def _():
            dq_ref[...] = jnp.zeros_like(dq_ref)
            dfq_ref[...] = jnp.zeros_like(dfq_ref)

        ks = [k_ref[0, s * blk:(s + 1) * blk, :].astype(BF16) for s in range(nsub)]
        vs = [v_ref[0, s * blk:(s + 1) * blk, :].astype(BF16) for s in range(nsub)]
        fks = [fc_ref[0, s * blk:(s + 1) * blk, :] for s in range(nsub)]
        r = lax.broadcasted_iota(jnp.int32, (blk, blk), 0)
        c = lax.broadcasted_iota(jnp.int32, (blk, blk), 1)

        def q_step(i, accs, subs):
            off = pl.multiple_of(i * blk, blk)
            qi = q_ref[0, pl.ds(off, blk), :]
            doi = do_ref[0, i]
            fq, lse, dl = fr_ref[0, i], lse_ref[0, i], dl_ref[0, i]
            accs = list(accs)
            dq_i, dfq_i = None, None
            for s, diagonal in subs:
                dk, dv, dfk = accs[s]
                st = _raw_dot(ks[s], qi, 1, 1, False) + fq - fks[s] - lse
                if diagonal:
                    st = jnp.where(r <= c, st, NEG_BIG)
                pt = jnp.exp(st)
                dv = dv + _raw_dot(pt, doi, 1, 1, False)
                dst = pt * (_raw_dot(vs[s], doi, 1, 0, False) - dl)
                dk = dk + _raw_dot(dst, qi, 1, 0, False)
                dfk = dfk - jnp.sum(dst, axis=1, keepdims=True)
                accs[s] = (dk, dv, dfk)
                dq_s = _raw_dot(dst, ks[s], 0, 0, False)
                dfq_s = jnp.sum(dst, axis=0, keepdims=True)
                dq_i = dq_s if dq_i is None else dq_i + dq_s
                dfq_i = dfq_s if dfq_i is None else dfq_i + dfq_s
            dfq_ref[0, i] += dfq_i
            dq_ref[0, pl.ds(off, blk), :] += dq_i
            return tuple(accs)

        accs = tuple((jnp.zeros((blk, hd), F32), jnp.zeros((blk, hd), F32), jnp.zeros((blk, 1), F32))
                     for _ in range(nsub))
        for d in range(nsub):
            accs = q_step(j * nsub + d, accs, [(s, s == d) for s in range(d + 1)])
        accs = lax.fori_loop((j + 1) * nsub, i_stop,
                             lambda i, a: q_step(i, a, [(s, False) for s in range(nsub)]), accs)
        for s, (dk, dv, dfk) in enumerate(accs):
            dk_ref[0, s * blk:(s + 1) * blk, :] = dk
            dv_ref[0, s * blk:(s + 1) * blk, :] = dv
            dfk_ref[0, s * blk:(s + 1) * blk, :] = dfk

    full = pl.BlockSpec((1, t_total, hd), lambda h, j: (h, 0, 0))
    tile = pl.BlockSpec((1, tkv, hd), lambda h, j: (h, j, 0))
    col = pl.BlockSpec((1, tkv, 1), lambda h, j: (h, j, 0))
    rows = pl.BlockSpec((1, nb, 1, blk), lambda h, j: (h, 0, 0, 0))
    do_blocks = pl.BlockSpec((1, nb, hd, blk), lambda h, j: (h, 0, 0, 0))
    return _call(
        body, name="fox_bwd", grid=(h_total, nb // nsub),
        in_specs=[SMEM, SMEM, SMEM, full, do_blocks, tile, tile, col, rows, rows, rows],
        out_specs=[full, tile, tile, col, rows],
        out_shape=[jax.ShapeDtypeStruct(q.shape, F32), jax.ShapeDtypeStruct(q.shape, F32),
                   jax.ShapeDtypeStruct(q.shape, F32), jax.ShapeDtypeStruct(f_col.shape, F32),
                   jax.ShapeDtypeStruct(f_row.shape, F32)],
        compiler_params=_cparams(("parallel", "arbitrary")),
    )(*tables, q.astype(BF16), do_blk, k, v, f_col, f_row, lse_row, delta_row)


def _attn_delta_call(do_t, o_t):
    h_total, hd, t_total = o_t.shape
    tb = _tile(t_total, 2816, LANE)

    def body(do_ref, o_ref, d_ref):
        d_ref[0] = jnp.sum(do_ref[0] * o_ref[0], axis=0, keepdims=True)

    spec = pl.BlockSpec((1, hd, tb), lambda h, i: (h, 0, i))
    return _call(
        body, name="fox_delta", grid=(h_total, t_total // tb), in_specs=[spec, spec],
        out_specs=pl.BlockSpec((1, 1, tb), lambda h, i: (h, 0, i)),
        out_shape=jax.ShapeDtypeStruct((h_total, 1, t_total), F32),
        compiler_params=_cparams(("parallel", "parallel")),
    )(do_t, o_t)


@jax.custom_vjp
def fox_attention(q, k, v, f_col, f_row):
    return _attn_fwd_call(q, k, v, f_col, f_row, _attn_skip_tables(q, k, f_row))[0]


def _fox_fwd(q, k, v, f_col, f_row):
    tables = _attn_skip_tables(q, k, f_row)
    o_t, lse_row = _attn_fwd_call(q, k, v, f_col, f_row, tables)
    return o_t, (q, k, v, f_col, f_row, tables, o_t, lse_row)


def _fox_bwd(res, do_t):
    q, k, v, f_col, f_row, tables, o_t, lse_row = res
    h_total, t_total, hd = q.shape
    nb, blk = f_row.shape[1], f_row.shape[3]
    delta = _attn_delta_call(do_t, o_t).reshape(f_row.shape)
    do_blk = do_t.reshape(h_total, hd, nb, blk).transpose(0, 2, 1, 3).astype(BF16)
    return tuple(_attn_bwd_call(q, k, v, f_col, f_row, tables, lse_row, delta, do_blk))


fox_attention.defvjp(_fox_fwd, _fox_bwd)


def _head_col(blk, h):
    lane = lax.broadcasted_iota(jnp.int32, blk.shape, 1)
    return jnp.sum(jnp.where(lane == h, blk, 0.0), axis=1, keepdims=True)


@jax.custom_vjp
def _cat2(a, b):
    return jnp.concatenate([a, b], axis=1)


_cat2.defvjp(lambda a, b: (_cat2(a, b), a.shape[1]), lambda na, ct: (ct[:, :na], ct[:, na:]))


@jax.custom_vjp
def _split2(x):
    half = x.shape[1] // 2
    return x[:, :half], x[:, half:]


_split2.defvjp(lambda x: (_split2(x), None), lambda _, cts: (jnp.concatenate(cts, axis=1),))


def _neumann_solve(m, b):
    x = b - _raw_dot(m, b, 1, 0, False)
    powers, steps = [m], 1
    while 2 * steps < GDN_CHUNK:
        powers.append(_raw_dot(powers[-1], powers[-1], 1, 0, False))
        x = x + _raw_dot(powers[-1], x, 1, 0, False)
        steps *= 2
    return x, powers


@jax.custom_vjp
def _unit_lower_solve(m, b):
    return _neumann_solve(m, b)[0]


def _unit_lower_solve_fwd(m, b):
    x, powers = _neumann_solve(m, b)
    return x, (powers, x)


def _unit_lower_solve_bwd(res, dx):
    powers, x = res
    db = dx - _raw_dot(powers[0], dx, 0, 0, False)
    for p in powers[1:]:
        db = db + _raw_dot(p, db, 0, 0, False)
    return -_raw_dot(db, x, 1, 1, False), db


_unit_lower_solve.defvjp(_unit_lower_solve_fwd, _unit_lower_solve_bwd)


@jax.custom_vjp
def _unit_lower_solve_known(m, b, x):
    return x


def _unit_lower_solve_known_bwd(res, dx):
    m, x = res
    powers, steps = [m], 1
    while 2 * steps < GDN_CHUNK:
        powers.append(_raw_dot(powers[-1], powers[-1], 1, 0, False))
        steps *= 2
    dm, db = _unit_lower_solve_bwd((powers, x), dx)
    return dm, db, jnp.zeros_like(x)


_unit_lower_solve_known.defvjp(lambda m, b, x: (x, (m, x)), _unit_lower_solve_known_bwd)


def _gdn_intra(h, q, k, v, b_blk, g_blk, uw_known=None):
    n = q.shape[0]
    b, g = _head_col(b_blk, h), _head_col(g_blk, h)
    r = lax.broadcasted_iota(jnp.int32, (n, n), 0)
    c = lax.broadcasted_iota(jnp.int32, (n, n), 1)
    same = (r // GDN_CHUNK) == (c // GDN_CHUNK)
    incl = same & (r >= c)
    g_row = jnp.sum(jnp.where(r == c, g, 0.0), axis=0, keepdims=True)
    big_g = jnp.sum(jnp.where(incl, g_row, 0.0), axis=1, keepdims=True)
    big_g_row = jnp.sum(jnp.where(same & (r <= c), g, 0.0), axis=0, keepdims=True)
    g_tot = jnp.sum(jnp.where(same, g_row, 0.0), axis=1, keepdims=True)
    dec = jnp.where(incl, jnp.exp(jnp.where(incl, big_g - big_g_row, 0.0)), 0.0)
    dec_strict = jnp.where(r > c, dec, 0.0)
    e_g = jnp.exp(big_g)
    kb = k * b
    m = _dot(kb, k, 1, 1) * dec_strict
    rs = lax.broadcasted_iota(jnp.int32, (n, GDN_CHUNK), 0)
    cs = lax.broadcasted_iota(jnp.int32, (n, GDN_CHUNK), 1)
    fold = jnp.where(rs % GDN_CHUNK == cs, 1.0, 0.0).astype(F32)
    aqk = _dot(_dot(q, k, 1, 1) * dec, fold, 1, 0, True)
    rhs = _cat2(v * b, kb * e_g)
    u, w = _split2(_unit_lower_solve(m, rhs) if uw_known is None else
                   _unit_lower_solve_known(m, rhs, jnp.concatenate(uw_known, axis=1)))
    lane = lax.broadcasted_iota(jnp.int32, b_blk.shape, 1)
    return u, w, q * e_g, k * jnp.exp(g_tot - big_g), aqk, jnp.where(lane == h, g_tot, 0.0)


def _gdn_rec(h, s, u, w, qg, kd, aqk, gl_blk):
    g_last = jnp.max(_head_col(gl_blk, h), axis=0, keepdims=True)
    big_u = u - _dot(w, s)
    o = _dot(qg, s) + _dot(aqk, big_u)
    s_next = s * jnp.exp(g_last) + _dot(kd, big_u, 0, 0)
    return o, s_next


GDN_TOK_BLK = 256


def _gdn_layout(t_total, rev):
    tb = _tile(t_total, GDN_TOK_BLK, GDN_CHUNK)
    cb, nblk = tb // GDN_CHUNK, t_total // tb
    pos = (lambda i: nblk - 1 - i) if rev else (lambda i: i)
    specs = dict(
        tok=pl.BlockSpec((tb, GDN_W), lambda i: (pos(i), 0)),
        q=pl.BlockSpec((tb, GDN_W), lambda i: (pos(i), 0)),
        k=pl.BlockSpec((tb, GDN_W), lambda i: (pos(i), 1)),
        v=pl.BlockSpec((tb, GDN_W), lambda i: (pos(i), 2)),
        qkv=pl.BlockSpec((tb, 3 * GDN_W), lambda i: (pos(i), 0)),
        gate=pl.BlockSpec((tb, GDN_HEADS), lambda i: (pos(i), 0)),
        aqk=pl.BlockSpec((GDN_HEADS, tb, GDN_CHUNK), lambda i: (0, pos(i), 0)),
        state=pl.BlockSpec((GDN_HEADS, cb, GDN_HD, GDN_HD), lambda i: (0, pos(i), 0, 0)))
    return cb, nblk, specs


def _gdn_shapes(t_total):
    n_chunks = t_total // GDN_CHUNK
    return dict(tok=jax.ShapeDtypeStruct((t_total, GDN_W), F32),
                gate=jax.ShapeDtypeStruct((t_total, GDN_HEADS), F32),
                aqk=jax.ShapeDtypeStruct((GDN_HEADS, t_total, GDN_CHUNK), F32),
                state=jax.ShapeDtypeStruct((GDN_HEADS, n_chunks, GDN_HD, GDN_HD), F32))


def _chunk_rows(ci):
    return pl.ds(pl.multiple_of(ci * GDN_CHUNK, GDN_CHUNK), GDN_CHUNK)


def _head_cols(h):
    return pl.ds(h * GDN_HD, GDN_HD)


def _gdn_intra_fwd_call(qkv, b, g):
    cb, nblk, sp = _gdn_layout(qkv.shape[0], False)
    sh = _gdn_shapes(qkv.shape[0])

    def body(q_ref, k_ref, v_ref, b_ref, g_ref, u_ref, w_ref, qg_ref, kd_ref, aqk_ref, gl_ref):
        b_blk, g_blk = b_ref[...], g_ref[...]
        gl = jnp.zeros(b_blk.shape, F32)
        for h in range(GDN_HEADS):
            cols = _head_cols(h)
            u, w, qg, kd, aqk, gl_h = _gdn_intra(h, q_ref[:, cols], k_ref[:, cols], v_ref[:, cols], b_blk, g_blk)
            u_ref[:, cols] = u
            w_ref[:, cols] = w
            qg_ref[:, cols] = qg
            kd_ref[:, cols] = kd
            aqk_ref[h] = aqk
            gl = gl + gl_h
        gl_ref[...] = gl

    return _call(
        body, name="gdn_intra_fwd", grid=(nblk,),
        in_specs=[sp["q"], sp["k"], sp["v"]] + [sp["gate"]] * 2,
        out_specs=[sp["tok"]] * 4 + [sp["aqk"], sp["gate"]],
        out_shape=[sh["tok"]] * 4 + [sh["aqk"], sh["gate"]],
        compiler_params=_cparams(("parallel",)),
    )(qkv, qkv, qkv, b, g)


def _gdn_intra_bwd_call(qkv, b, g, u, w, du, dw, dqg, dkd, daqk, dgl):
    cb, nblk, sp = _gdn_layout(qkv.shape[0], False)
    sh = _gdn_shapes(qkv.shape[0])

    def body(q_ref, k_ref, v_ref, b_ref, g_ref, u_ref, w_ref, du_ref, dw_ref, dqg_ref, dkd_ref, daqk_ref, dgl_ref,
             dqkv_ref, db_ref, dg_ref):
        b_blk, g_blk, dgl = b_ref[...], g_ref[...], dgl_ref[...]
        db = jnp.zeros(b_blk.shape, F32)
        dg = jnp.zeros(b_blk.shape, F32)
        for h in range(GDN_HEADS):
            cols = _head_cols(h)
            _, vjp = jax.vjp(functools.partial(_gdn_intra, h, uw_known=(u_ref[:, cols], w_ref[:, cols])),
                             q_ref[:, cols], k_ref[:, cols], v_ref[:, cols], b_blk, g_blk)
            dq, dk, dv, db_h, dg_h = vjp((du_ref[:, cols], dw_ref[:, cols], dqg_ref[:, cols], dkd_ref[:, cols],
                                          daqk_ref[h], dgl))
            dqkv_ref[:, pl.ds(h * GDN_HD, GDN_HD)] = dq
            dqkv_ref[:, pl.ds(GDN_W + h * GDN_HD, GDN_HD)] = dk
            dqkv_ref[:, pl.ds(2 * GDN_W + h * GDN_HD, GDN_HD)] = dv
            db = db + db_h
            dg = dg + dg_h
        db_ref[...] = db
        dg_ref[...] = dg

    return _call(
        body, name="gdn_intra_bwd", grid=(nblk,),
        in_specs=[sp["q"], sp["k"], sp["v"]] + [sp["gate"]] * 2 + [sp["tok"]] * 6 + [sp["aqk"], sp["gate"]],
        out_specs=[sp["qkv"]] + [sp["gate"]] * 2,
        out_shape=[jax.ShapeDtypeStruct(qkv.shape, F32)] + [sh["gate"]] * 2,
        compiler_params=_cparams(("parallel",)),
    )(qkv, qkv, qkv, b, g, u, w, du, dw, dqg, dkd, daqk, dgl)


def _gdn_rec_fwd_call(u, w, qg, kd, aqk, gl):
    cb, nblk, sp = _gdn_layout(u.shape[0], False)
    sh = _gdn_shapes(u.shape[0])

    def body(u_ref, w_ref, qg_ref, kd_ref, aqk_ref, gl_ref, o_ref, s_all_ref, s_ref):
        @pl.when(pl.program_id(0) == 0)
        def _():
            s_ref[...] = jnp.zeros_like(s_ref)

        def chunk(ci, carry):
            rows = _chunk_rows(ci)
            gl_row = gl_ref[rows, :]
            states = [s_ref[h] for h in range(GDN_HEADS)]
            res = [_gdn_rec(h, states[h], u_ref[rows, _head_cols(h)], w_ref[rows, _head_cols(h)],
                            qg_ref[rows, _head_cols(h)], kd_ref[rows, _head_cols(h)], aqk_ref[h, rows, :], gl_row)
                   for h in range(GDN_HEADS)]
            for h, (o, s_next) in enumerate(res):
                s_all_ref[h, ci] = states[h]
                o_ref[rows, _head_cols(h)] = o
                s_ref[h] = s_next
            return carry

        lax.fori_loop(0, cb, chunk, 0)

    return _call(
        body, name="gdn_rec_fwd", grid=(nblk,),
        in_specs=[sp["tok"]] * 4 + [sp["aqk"], sp["gate"]],
        out_specs=[sp["tok"], sp["state"]], out_shape=[sh["tok"], sh["state"]],
        scratch_shapes=[pltpu.VMEM((GDN_HEADS, GDN_HD, GDN_HD), F32)],
        compiler_params=_cparams(("arbitrary",)),
    )(u, w, qg, kd, aqk, gl)


def _gdn_rec_bwd_call(u, w, qg, kd, aqk, gl, s_all, do):
    cb, nblk, sp = _gdn_layout(u.shape[0], True)
    sh = _gdn_shapes(u.shape[0])

    def body(u_ref, w_ref, qg_ref, kd_ref, aqk_ref, gl_ref, s_all_ref, do_ref,
             du_ref, dw_ref, dqg_ref, dkd_ref, daqk_ref, dgl_ref, ds_ref):
        @pl.when(pl.program_id(0) == 0)
        def _():
            ds_ref[...] = jnp.zeros_like(ds_ref)

        def chunk(step, carry):
            ci = cb - 1 - step
            rows = _chunk_rows(ci)
            gl_row = gl_ref[rows, :]
            res = []
            for h in range(GDN_HEADS):
                cols = _head_cols(h)
                _, vjp = jax.vjp(functools.partial(_gdn_rec, h), s_all_ref[h, ci], u_ref[rows, cols],
                                 w_ref[rows, cols], qg_ref[rows, cols], kd_ref[rows, cols], aqk_ref[h, rows, :],
                                 gl_row)
                res.append(vjp((do_ref[rows, cols], ds_ref[h])))
            dgl = jnp.zeros((GDN_CHUNK, GDN_HEADS), F32)
            for h, (ds, du, dw, dqg, dkd, daqk, dgl_h) in enumerate(res):
                cols = _head_cols(h)
                ds_ref[h] = ds
                du_ref[rows, cols] = du
                dw_ref[rows, cols] = dw
                dqg_ref[rows, cols] = dqg
                dkd_ref[rows, cols] = dkd
                daqk_ref[h, rows, :] = daqk
                dgl = dgl + dgl_h
            dgl_ref[rows, :] = dgl
            return carry

        lax.fori_loop(0, cb, chunk, 0)

    return _call(
        body, name="gdn_rec_bwd", grid=(nblk,),
        in_specs=[sp["tok"]] * 4 + [sp["aqk"], sp["gate"], sp["state"], sp["tok"]],
        out_specs=[sp["tok"]] * 4 + [sp["aqk"], sp["gate"]],
        out_shape=[sh["tok"]] * 4 + [sh["aqk"], sh["gate"]],
        scratch_shapes=[pltpu.VMEM((GDN_HEADS, GDN_HD, GDN_HD), F32)],
        compiler_params=_cparams(("arbitrary",)),
    )(u, w, qg, kd, aqk, gl, s_all, do)


@jax.custom_vjp
def gdn_intra(qkv, b, g):
    return tuple(_gdn_intra_fwd_call(qkv, b, g))


def _gdn_intra_fwd(*a):
    outs = gdn_intra(*a)
    return outs, a + (outs[0], outs[1])


gdn_intra.defvjp(_gdn_intra_fwd, lambda res, cts: tuple(_gdn_intra_bwd_call(*res, *cts)))


@jax.custom_vjp
def gdn_rec(u, w, qg, kd, aqk, gl):
    return _gdn_rec_fwd_call(u, w, qg, kd, aqk, gl)[0]


def _gdn_rec_fwd(*a):
    o, s_all = _gdn_rec_fwd_call(*a)
    return o, a + (s_all,)


gdn_rec.defvjp(_gdn_rec_fwd, lambda res, do: tuple(_gdn_rec_bwd_call(*res, do)))


def gated_delta(qkv, b, g):
    return gdn_rec(*gdn_intra(qkv, b, g))


def _loss_call(y, tgt, first, last):
    r_total, d = y.shape
    tm = _tile(r_total, 256, 8)

    def body(y_ref, t_ref, loss_ref, dy_ref):
        i = pl.program_id(0)

        @pl.when(i == 0)
        def _():
            loss_ref[...] = jnp.zeros_like(loss_ref)

        row = lax.broadcasted_iota(jnp.int32, (tm, d), 0) + i * tm
        err = jnp.where((row >= first) & (row < last), y_ref[...] - t_ref[...], 0.0)
        dy_ref[...] = err * (1.0 / d)
        part = jnp.sum(jnp.sum(err * err, axis=1, keepdims=True), axis=0, keepdims=True) * (0.5 / d)
        loss_ref[...] += jnp.broadcast_to(part, loss_ref.shape)

    return _call(
        body, name="loss_head", grid=(r_total // tm,),
        in_specs=[pl.BlockSpec((tm, d), lambda i: (i, 0))] * 2,
        out_specs=[pl.BlockSpec((8, LANE), lambda i: (0, 0)), pl.BlockSpec((tm, d), lambda i: (i, 0))],
        out_shape=[jax.ShapeDtypeStruct((8, LANE), F32), jax.ShapeDtypeStruct(y.shape, F32)],
        compiler_params=_cparams(("arbitrary",)),
    )(y, tgt)


def make_loss(first, last):
    @jax.custom_vjp
    def op(y, tgt):
        return _loss_call(y, tgt, first, last)[0][0, 0]

    def fwd(y, tgt):
        loss, dy = _loss_call(y, tgt, first, last)
        return loss[0, 0], (dy,)

    def bwd(res, ct):
        return res[0] * ct, jnp.zeros_like(res[0])

    op.defvjp(fwd, bwd)
    return op


def _pad_rows8(w):
    return jnp.concatenate([w, jnp.zeros((8 - w.shape[0], w.shape[1]), w.dtype)], axis=0)


def local_loss(wts, x, tgt):
    seq = x.shape[0]
    n_tok = N_META + seq
    t_pad = -(-n_tok // ROW_ALIGN) * ROW_ALIGN
    depth = wts["norm1_g"].shape[0]
    blk = _tile(t_pad, ATT_BLK, LANE)
    nb = t_pad // blk
    tm = _tile(t_pad, 256, 8)

    rms = rowop(_f_rmsnorm, "rmsnorm", (D_MODEL,), tm, out_dtypes=[BF16])
    qnorm = rowop(_f_qnorm, "fox_q_norm", (FOX_HD,), _tile(FOX_HEADS * t_pad, 2048, 8))
    knorm = rowop(_f_rmsnorm, "fox_k_norm", (FOX_HD,), _tile(FOX_HEADS * t_pad, 2048, 8))
    logsig = rowop(_f_logsig, "fox_log_forget", (FOX_HEADS,), tm)
    gdn_act = rowop([_f_gdn_q] * GDN_HEADS + [_f_gdn_k] * GDN_HEADS + [_f_gdn_v] * GDN_HEADS, "gdn_qkv_act",
                    (GDN_HD,), tm)
    gates = rowop(_f_gdn_gates, "gdn_gates", (GDN_HEADS, GDN_HEADS), tm)
    gdn_out = rowop([_f_gdn_out] * GDN_HEADS, "gdn_out_norm", (GDN_HD,), tm, out_dtypes=[BF16])
    merge = rowop(_f_merge, "branch_merge", (D_MODEL,), tm, out_dtypes=[BF16])
    residual = rowop(_f_residual, "residual_add", (D_MODEL,), tm, bc=(2,))
    residual_norm = rowop(_f_residual_norm, "residual_add_norm", (D_MODEL, D_MODEL), tm, bc=(2,),
                          out_dtypes=[F32, BF16])
    keep = (jnp.arange(t_pad)[:, None] < n_tok).astype(F32)
    conv4 = make_dwconv(GDN_CONV)
    conv3 = make_dwconv(FFN_CONV)
    loss_op = make_loss(N_META, n_tok)

    zeros = jnp.zeros((t_pad - n_tok, D_MODEL), F32)
    h_res = jnp.concatenate([wts["meta_tokens"], x, zeros], axis=0)
    tgt_rows = jnp.concatenate([jnp.zeros((N_META, D_MODEL), F32), tgt, zeros], axis=0)

    def heads(a):
        return a.reshape(t_pad, FOX_HEADS, FOX_HD).transpose(1, 0, 2).reshape(FOX_HEADS * t_pad, FOX_HD)

    h = rms((h_res,), (wts["norm1_g"][0][None],))[0]
    for l in range(depth):
        proj = mm(h, wts["w_in"][l])
        qn = qnorm((heads(proj[:, 0:512]),), (wts["fox_q_norm_g"][l][None],))[0]
        kn = knorm((heads(proj[:, 512:1024]),), (wts["fox_k_norm_g"][l][None],))[0]
        vh = heads(proj[:, 1024:1536])
        log_f = logsig((proj[:, 1536:1544],), (wts["fox_f_bias"][l][None],))[0]
        f_cum = cumsum_lanes(log_f.T)
        o_a = fox_attention(qn.reshape(FOX_HEADS, t_pad, FOX_HD), kn.reshape(FOX_HEADS, t_pad, FOX_HD),
                            vh.reshape(FOX_HEADS, t_pad, FOX_HD), f_cum[:, :, None],
                            f_cum.reshape(FOX_HEADS, nb, 1, blk))
        y_a = mm_bf16(o_a.transpose(2, 0, 1).reshape(t_pad, FOX_W).astype(BF16), wts["w_branch_a"][l])
        cv = conv4(proj[:, 1664:4736], _pad_rows8(wts["gdn_conv_w"][l]))
        qkv = gdn_act((cv,), ())[0]
        beta, gdec = gates((proj[:, 4736:4744], proj[:, 4744:4752]),
                           (wts["gdn_a_log"][l][None], wts["gdn_dt_bias"][l][None]))
        o_b = gated_delta(qkv, beta, gdec)
        o_b = gdn_out((o_b, proj[:, 4864:5888]), (wts["gdn_norm_g"][l][None],))[0]
        y_b = mm_bf16(o_b, wts["w_branch_b"][l])
        mixed = merge((proj[:, 5888:6912], proj[:, 6912:7936], y_a, y_b), ())[0]
        h_res, h = residual_norm((h_res, mm(mixed, wts["w_out"][l]), keep), (wts["norm2_g"][l][None],))
        up = conv3(mm_bf16(h, wts["w_up"][l]), _pad_rows8(wts["ffn_conv_w"][l]))
        act = glu(up)
        down = mm(act, wts["w_down"][l])
        if l + 1 < depth:
            h_res, h = residual_norm((h_res, down, keep), (wts["norm1_g"][l + 1][None],))
        else:
            h_res = residual((h_res, down, keep), ())[0]
    return loss_op(h_res, tgt_rows)


def pad_w_in(w):
    parts, pos = [], 0
    for src, width, dst in IN_SEGS:
        if dst > pos:
            parts.append(jnp.zeros(w.shape[:-1] + (dst - pos,), w.dtype))
        parts.append(w[..., src:src + width])
        pos = dst + width
    parts.append(jnp.zeros(w.shape[:-1] + (D_IN_PAD - pos,), w.dtype))
    return jnp.concatenate(parts, axis=-1)


def unpad_w_in(w):
    return jnp.concatenate([w[..., dst:dst + width] for _, width, dst in IN_SEGS], axis=-1)


ANY = pl.BlockSpec(memory_space=pl.ANY)
N_CHIPS = 4
N_DEV = 8
COMM_COLS = 1024
COMM_ROW_ALIGN = 512
COMM_ROW_ALIGN_SMALL = 32


def _place():
    return lax.axis_index("x"), lax.axis_index("y"), lax.axis_index("c")


def _other_chips(x, y):
    return [(1 - x, y), (x, 1 - y), (1 - x, 1 - y)]


def _remote(src, dst, send_sem, recv_sem, dev):
    return pltpu.make_async_remote_copy(src_ref=src, dst_ref=dst, send_sem=send_sem, recv_sem=recv_sem,
                                        device_id=dev, device_id_type=MESH)


def chip_all_gather(buf):
    rows, cols = buf.shape
    half = rows // 2

    def body(x_ref, out_ref, send_sems, recv_sems, pass_send, pass_recv):
        x, y, c = _place()
        me = 2 * x + y
        mine, other = pl.ds(c * half, half), pl.ds((1 - c) * half, half)
        sibling = (x, y, 1 - c)
        chips = _other_chips(x, y)
        started = []
        for k, (px, py) in enumerate(chips):
            cp = _remote(x_ref.at[mine], out_ref.at[me, mine], send_sems.at[k], recv_sems.at[k], (px, py, c))
            cp.start()
            started.append(cp)
        for k, (px, py) in enumerate(chips):
            landed = out_ref.at[2 * px + py, mine]
            _remote(landed, landed, send_sems.at[k], recv_sems.at[k], (px, py, c)).wait_recv()
            cp = _remote(landed, landed, pass_send.at[k], pass_recv.at[k], sibling)
            cp.start()
            started.append(cp)
        for k, (px, py) in enumerate(chips):
            passed = out_ref.at[2 * px + py, other]
            _remote(passed, passed, pass_send.at[k], pass_recv.at[k], sibling).wait_recv()
        for cp in started:
            cp.wait_send()

    got = _call(
        body, name="chip_all_gather", in_specs=[ANY], out_specs=ANY,
        out_shape=jax.ShapeDtypeStruct((N_CHIPS, rows, cols), buf.dtype),
        scratch_shapes=[pltpu.SemaphoreType.DMA((3,)), pltpu.SemaphoreType.DMA((3,)),
                        pltpu.SemaphoreType.DMA((3,)), pltpu.SemaphoreType.DMA((3,))],
    )(buf)
    me = 2 * lax.axis_index("x") + lax.axis_index("y")
    return lax.dynamic_update_slice(got, buf[None], (me, 0, 0))


def sibling_swap_halves(g4):
    n, rows, cols = g4.shape
    half = rows // 2

    def body(g_ref, got_ref, send_sem, recv_sem):
        x, y, c = _place()
        cp = _remote(g_ref.at[:, pl.ds((1 - c) * half, half), :], got_ref, send_sem, recv_sem, (x, y, 1 - c))
        cp.start()
        cp.wait()

    return _call(
        body, name="sibling_swap_halves", in_specs=[ANY], out_specs=ANY,
        out_shape=jax.ShapeDtypeStruct((n, half, cols), g4.dtype),
        scratch_shapes=[pltpu.SemaphoreType.DMA, pltpu.SemaphoreType.DMA],
    )(g4)


def add_own_half(g4, got, c):
    n, rows, cols = g4.shape
    half = rows // 2
    tm = _tile(half, 256, 16)
    nt = half // tm

    def body(c_ref, a_ref, b_ref, o_ref):
        o_ref[...] = (a_ref[...] + b_ref[...]).astype(o_ref.dtype)

    return _call(
        body, name="add_own_half",
        grid_spec=pltpu.PrefetchScalarGridSpec(
            num_scalar_prefetch=1, grid=(n, nt),
            in_specs=[pl.BlockSpec((1, tm, cols), lambda j, i, c_ref: (j, c_ref[0] * nt + i, 0)),
                      pl.BlockSpec((1, tm, cols), lambda j, i, c_ref: (j, i, 0))],
            out_specs=pl.BlockSpec((1, tm, cols), lambda j, i, c_ref: (j, i, 0))),
        out_shape=jax.ShapeDtypeStruct(got.shape, BF16),
        compiler_params=_cparams(("parallel", "parallel")),
    )(c.reshape(1).astype(jnp.int32), g4, got)


def chip_scatter(p4):
    n, rows, cols = p4.shape

    def body(p_ref, out_ref, send_sems, recv_sems):
        x, y, c = _place()
        me = 2 * x + y
        chips = _other_chips(x, y)
        started = []
        for k, (px, py) in enumerate(chips):
            cp = _remote(p_ref.at[2 * px + py], out_ref.at[me], send_sems.at[k], recv_sems.at[k], (px, py, c))
            cp.start()
            started.append(cp)
        for k, (px, py) in enumerate(chips):
            landed = out_ref.at[2 * px + py]
            _remote(landed, landed, send_sems.at[k], recv_sems.at[k], (px, py, c)).wait_recv()
        for cp in started:
            cp.wait_send()

    got = _call(
        body, name="chip_scatter", in_specs=[ANY], out_specs=ANY,
        out_shape=jax.ShapeDtypeStruct(p4.shape, p4.dtype),
        scratch_shapes=[pltpu.SemaphoreType.DMA((3,)), pltpu.SemaphoreType.DMA((3,))],
    )(p4)
    me = 2 * lax.axis_index("x") + lax.axis_index("y")
    return lax.dynamic_update_slice(got, lax.dynamic_slice_in_dim(p4, me, 1, axis=0), (me, 0, 0))


def sum_slots(a):
    n, rows, cols = a.shape
    tm = _tile(rows, 256, 16) if rows % 16 == 0 else rows

    def body(a_ref, o_ref):
        acc = a_ref[0].astype(F32)
        for k in range(1, n):
            acc = acc + a_ref[k].astype(F32)
        o_ref[...] = acc

    return _call(
        body, name="sum_slots_%d" % n, grid=(rows // tm,),
        in_specs=[pl.BlockSpec((n, tm, cols), lambda i: (0, i, 0))],
        out_specs=pl.BlockSpec((tm, cols), lambda i: (i, 0)),
        out_shape=jax.ShapeDtypeStruct((rows, cols), F32),
        compiler_params=_cparams(("parallel",)),
    )(a)


def sibling_join(s):
    half, cols = s.shape

    def body(s_ref, got_ref, send_sem, recv_sem):
        x, y, c = _place()
        cp = _remote(s_ref, got_ref, send_sem, recv_sem, (x, y, 1 - c))
        cp.start()
        cp.wait()

    got = _call(
        body, name="sibling_join", in_specs=[ANY], out_specs=ANY,
        out_shape=jax.ShapeDtypeStruct(s.shape, s.dtype),
        scratch_shapes=[pltpu.SemaphoreType.DMA, pltpu.SemaphoreType.DMA],
    )(s)
    c = lax.axis_index("c")
    out = jnp.zeros((2 * half, cols), s.dtype)
    out = lax.dynamic_update_slice(out, s, (c * half, 0))
    return lax.dynamic_update_slice(out, got, ((1 - c) * half, 0))


def all_devices_gather(buf):
    rows, cols = buf.shape

    def body(b_ref, out_ref, send_sems, recv_sems, local_sem):
        x, y, c = _place()
        me = 4 * x + 2 * y + c
        local = pltpu.make_async_copy(b_ref, out_ref.at[me], local_sem)
        local.start()
        peers = [((x + dx) % 2, (y + dy) % 2, (c + dc) % 2)
                 for dx in (0, 1) for dy in (0, 1) for dc in (0, 1) if dx + dy + dc > 0]
        started = []
        for k, peer in enumerate(peers):
            cp = _remote(b_ref, out_ref.at[me], send_sems.at[k], recv_sems.at[k], peer)
            cp.start()
            started.append(cp)
        for k, (px, py, pc) in enumerate(peers):
            landed = out_ref.at[4 * px + 2 * py + pc]
            _remote(landed, landed, send_sems.at[k], recv_sems.at[k], (px, py, pc)).wait_recv()
        for cp in started:
            cp.wait_send()
        local.wait()

    return _call(
        body, name="all_devices_gather", in_specs=[ANY], out_specs=ANY,
        out_shape=jax.ShapeDtypeStruct((N_DEV, rows, cols), buf.dtype),
        scratch_shapes=[pltpu.SemaphoreType.DMA((7,)), pltpu.SemaphoreType.DMA((7,)), pltpu.SemaphoreType.DMA],
    )(buf)


def adamw(w, g, m, v):
    shape = w.shape
    w2, g2, m2, v2 = [a.reshape(-1, shape[-1]) for a in (w, g, m, v)]
    rows, cols = w2.shape
    tm = _tile(rows, 256, 8) if rows % 8 == 0 else rows

    def body(w_ref, g_ref, m_ref, v_ref, d_ref, nm_ref, nv_ref):
        gv = g_ref[...]
        nm = ADAM_B1 * m_ref[...] + (1.0 - ADAM_B1) * gv
        nv = ADAM_B2 * v_ref[...] + (1.0 - ADAM_B2) * (gv * gv)
        m_hat = nm / (1.0 - ADAM_B1 ** ADAM_STEP)
        v_hat = nv / (1.0 - ADAM_B2 ** ADAM_STEP)
        d_ref[...] = -ADAM_LR * (m_hat / (jnp.sqrt(v_hat) + ADAM_EPS) + ADAM_WD * w_ref[...])
        nm_ref[...] = nm
        nv_ref[...] = nv

    spec = pl.BlockSpec((tm, cols), lambda i: (i, 0))
    outs = _call(
        body, name="adamw", grid=(rows // tm,), in_specs=[spec] * 4, out_specs=[spec] * 3,
        out_shape=[jax.ShapeDtypeStruct((rows, cols), F32)] * 3,
        compiler_params=_cparams(("parallel",)),
    )(w2, g2, m2, v2)
    return [o.reshape(shape) for o in outs]


WEIGHTS = ("meta_tokens", "norm1_g", "w_in", "fox_f_bias", "fox_q_norm_g", "fox_k_norm_g", "gdn_conv_w",
           "gdn_a_log", "gdn_dt_bias", "gdn_norm_g", "w_branch_a", "w_branch_b", "w_out", "norm2_g", "w_up",
           "ffn_conv_w", "w_down")
SHARD_AXIS = {"meta_tokens": -1, "w_in": -1, "gdn_conv_w": -1, "w_branch_a": -1, "w_branch_b": -2, "w_out": -2,
              "w_up": -1, "ffn_conv_w": -1, "w_down": -2}
MATMUL_WEIGHTS = ("w_in", "w_branch_a", "w_branch_b", "w_out", "w_up", "w_down")
SMALL_SHARDED = ("meta_tokens", "gdn_conv_w", "ffn_conv_w")
REPLICATED = tuple(n for n in WEIGHTS if n not in SHARD_AXIS)


def _pack(arrays, dtype, row_align):
    flat = jnp.concatenate([a.reshape(-1).astype(dtype) for a in arrays])
    block = row_align * COMM_COLS
    total = -(-flat.shape[0] // block) * block
    flat = jnp.concatenate([flat, jnp.zeros((total - flat.shape[0],), dtype)])
    return flat.reshape(-1, COMM_COLS)


def _unpack(buf, shapes):
    flat, out, pos = buf.reshape(-1), [], 0
    for shape in shapes:
        size = 1
        for d in shape:
            size *= d
        out.append(flat[pos:pos + size].reshape(shape))
        pos += size
    return out


def _gather_full(shards, names, dtype, row_align):
    got = chip_all_gather(_pack([shards[n] for n in names], dtype, row_align))
    per_chip = [_unpack(got[j], [shards[n].shape for n in names]) for j in range(N_CHIPS)]
    return {n: jnp.concatenate([per_chip[j][i] for j in range(N_CHIPS)], axis=SHARD_AXIS[n]).astype(F32)
            for i, n in enumerate(names)}


def _shard_of(full, name, j):
    axis = SHARD_AXIS[name] % full.ndim
    size = full.shape[axis] // N_CHIPS
    return lax.slice_in_dim(full, j * size, (j + 1) * size, axis=axis)


def kernel(x, meta_tokens, norm1_g, w_in, fox_f_bias, fox_q_norm_g, fox_k_norm_g, gdn_conv_w, gdn_a_log, gdn_dt_bias, gdn_norm_g, w_branch_a, w_branch_b, w_out, norm2_g, w_up, ffn_conv_w, w_down, loss_target, m_meta_tokens, m_norm1_g, m_w_in, m_fox_f_bias, m_fox_q_norm_g, m_fox_k_norm_g, m_gdn_conv_w, m_gdn_a_log, m_gdn_dt_bias, m_gdn_norm_g, m_w_branch_a, m_w_branch_b, m_w_out, m_norm2_g, m_w_up, m_ffn_conv_w, m_w_down, v_meta_tokens, v_norm1_g, v_w_in, v_fox_f_bias, v_fox_q_norm_g, v_fox_k_norm_g, v_gdn_conv_w, v_gdn_a_log, v_gdn_dt_bias, v_gdn_norm_g, v_w_branch_a, v_w_branch_b, v_w_out, v_norm2_g, v_w_up, v_ffn_conv_w, v_w_down):
    w_loc = dict(zip(WEIGHTS, (meta_tokens, norm1_g, w_in, fox_f_bias, fox_q_norm_g, fox_k_norm_g, gdn_conv_w,
                               gdn_a_log, gdn_dt_bias, gdn_norm_g, w_branch_a, w_branch_b, w_out, norm2_g, w_up,
                               ffn_conv_w, w_down)))
    m_loc = dict(zip(WEIGHTS, (m_meta_tokens, m_norm1_g, m_w_in, m_fox_f_bias, m_fox_q_norm_g, m_fox_k_norm_g,
                               m_gdn_conv_w, m_gdn_a_log, m_gdn_dt_bias, m_gdn_norm_g, m_w_branch_a, m_w_branch_b,
                               m_w_out, m_norm2_g, m_w_up, m_ffn_conv_w, m_w_down)))
    v_loc = dict(zip(WEIGHTS, (v_meta_tokens, v_norm1_g, v_w_in, v_fox_f_bias, v_fox_q_norm_g, v_fox_k_norm_g,
                               v_gdn_conv_w, v_gdn_a_log, v_gdn_dt_bias, v_gdn_norm_g, v_w_branch_a, v_w_branch_b,
                               v_w_out, v_norm2_g, v_w_up, v_ffn_conv_w, v_w_down)))
    c = lax.axis_index("c")

    full = {n: w_loc[n] for n in REPLICATED}
    full.update(_gather_full(w_loc, MATMUL_WEIGHTS, BF16, COMM_ROW_ALIGN))
    full.update(_gather_full(w_loc, SMALL_SHARDED, F32, COMM_ROW_ALIGN_SMALL))
    full["w_in"] = pad_w_in(full["w_in"])

    loss, (g_full, g_x) = jax.value_and_grad(local_loss, argnums=(0, 1))(full, x[0], loss_target[0])
    g_full = dict(g_full)
    g_full["w_in"] = unpad_w_in(g_full["w_in"])

    sharded = MATMUL_WEIGHTS + SMALL_SHARDED
    g4 = jnp.stack([_pack([_shard_of(g_full[n], n, j) for n in sharded], F32, COMM_ROW_ALIGN)
                    for j in range(N_CHIPS)])
    pair_sum = add_own_half(g4, sibling_swap_halves(g4), c)
    g_shard = sibling_join(sum_slots(chip_scatter(pair_sum)))
    grads = dict(zip(sharded, _unpack(g_shard, [w_loc[n].shape for n in sharded])))
    g_rep = sum_slots(all_devices_gather(_pack([g_full[n] for n in REPLICATED], F32, 8)))
    grads.update(zip(REPLICATED, _unpack(g_rep, [w_loc[n].shape for n in REPLICATED])))

    loss = lax.psum(loss, ("x", "y", "c"))
    upd = {n: adamw(w_loc[n], grads[n], m_loc[n], v_loc[n]) for n in WEIGHTS}
    return (loss, g_x[None], *[grads[n] for n in WEIGHTS], *[upd[n][0] for n in WEIGHTS],
            *[upd[n][1] for n in WEIGHTS], *[upd[n][2] for n in WEIGHTS])
```

```python
import functools

import jax
import jax.numpy as jnp
from jax import lax
from jax.experimental import pallas as pl
from jax.experimental.pallas import tpu as pltpu

F32 = jnp.float32
BF16 = jnp.bfloat16
HI = lax.Precision.HIGHEST
MESH = pl.DeviceIdType.MESH

D_MODEL = 1024
N_META = 16
EPS = 1e-6
FOX_HEADS, FOX_HD = 8, 64
FOX_W = FOX_HEADS * FOX_HD
GDN_HEADS, GDN_HD, GDN_CHUNK, GDN_CONV = 8, 128, 64, 4
GDN_W = GDN_HEADS * GDN_HD
D_FF = 2816
FFN_CONV = 3
D_IN = 7704
D_IN_PAD = 8192
IN_SEGS = ((0, 1536, 0), (1536, 8, 1536), (1544, 3072, 1664), (4616, 16, 4736), (4632, 1024, 4864), (5656, 2048, 5888))
ROW_ALIGN = 256
ATT_BLK = 256
VMEM_LIMIT = 48 * 1024 * 1024
LANE = 128

ADAM_LR, ADAM_B1, ADAM_B2, ADAM_EPS, ADAM_WD, ADAM_STEP = 0.001, 0.9, 0.999, 1e-08, 0.01, 10


def _call(body, **kw):
    return pl.pallas_call(body, **kw)


def _tile(n, target, mult):
    best, t = None, mult
    while t <= min(n, target):
        if n % t == 0:
            best = t
        t += mult
    assert best is not None, (n, target, mult)
    return best


def _cparams(sem):
    return pltpu.CompilerParams(dimension_semantics=sem, vmem_limit_bytes=VMEM_LIMIT)


def _raw_dot(a, b, ca, cb, precise):
    dims = (((ca,), (cb,)), ((), ()))
    a_hi, b_hi = a.astype(BF16), b.astype(BF16)
    out = lax.dot_general(a_hi, b_hi, dims, preferred_element_type=F32)
    if precise:
        a_lo = (a - a_hi.astype(F32)).astype(BF16)
        b_lo = (b - b_hi.astype(F32)).astype(BF16)
        out = out + (lax.dot_general(a_hi, b_lo, dims, preferred_element_type=F32)
                     + lax.dot_general(a_lo, b_hi, dims, preferred_element_type=F32))
    return out


def _make_dot(ca, cb, precise):
    @jax.custom_vjp
    def f(a, b):
        return _raw_dot(a, b, ca, cb, precise)

    def fwd(a, b):
        return f(a, b), (a, b)

    def bwd(res, ct):
        a, b = res
        if ca == 1:
            da = _raw_dot(ct, b, 1, 1 if cb == 0 else 0, precise)
        else:
            da = _raw_dot(b, ct, 1 if cb == 0 else 0, 1, precise)
        if cb == 0:
            db = _raw_dot(a, ct, 0 if ca == 1 else 1, 0, precise)
        else:
            db = _raw_dot(ct, a, 0, 0 if ca == 1 else 1, precise)
        return da, db

    f.defvjp(fwd, bwd)
    return f


_DOTS = {(ca, cb, p): _make_dot(ca, cb, p) for ca in (0, 1) for cb in (0, 1) for p in (False, True)}


def _dot(a, b, ca=1, cb=0, precise=False):
    return _DOTS[(ca, cb, precise)](a, b)


def _mm_call(a, b, name, ta=False, tb=False, out_dtype=F32):
    k, m = a.shape if ta else a.shape[::-1]
    n, kb = b.shape if tb else b.shape[::-1]
    assert k == kb, (a.shape, b.shape)
    tm = _tile(m, 768, LANE if ta else 16)
    tn = _tile(n, 1408, LANE)
    tk = _tile(k, 1408, LANE)
    nk = k // tk
    dims = (((0 if ta else 1,), (1 if tb else 0,)), ((), ()))

    def body(a_ref, b_ref, o_ref, *scratch):
        part = lax.dot_general(a_ref[...], b_ref[...], dims, preferred_element_type=F32)
        if nk == 1:
            o_ref[...] = part.astype(o_ref.dtype)
            return
        acc_ref = scratch[0]
        kk = pl.program_id(2)

        @pl.when(kk == 0)
        def _():
            acc_ref[...] = part

        @pl.when(kk > 0)
        def _():
            acc_ref[...] += part

        @pl.when(kk == nk - 1)
        def _():
            o_ref[...] = acc_ref[...].astype(o_ref.dtype)

    return _call(
        body, name=name, grid=(m // tm, n // tn, nk),
        in_specs=[pl.BlockSpec((tk, tm), lambda i, j, kk: (kk, i)) if ta else
                  pl.BlockSpec((tm, tk), lambda i, j, kk: (i, kk)),
                  pl.BlockSpec((tn, tk), lambda i, j, kk: (j, kk)) if tb else
                  pl.BlockSpec((tk, tn), lambda i, j, kk: (kk, j))],
        out_specs=pl.BlockSpec((tm, tn), lambda i, j, kk: (i, j)),
        out_shape=jax.ShapeDtypeStruct((m, n), out_dtype),
        scratch_shapes=[pltpu.VMEM((tm, tn), F32)] if nk > 1 else [],
        compiler_params=_cparams(("parallel", "parallel", "arbitrary")),
    )(a, b)


def _make_mm(out_dtype):
    @jax.custom_vjp
    def op(a, w):
        return _mm_call(a.astype(BF16), w.astype(BF16), "mm_fwd", out_dtype=out_dtype)

    def fwd(a, w):
        a_b, w_b = a.astype(BF16), w.astype(BF16)
        return _mm_call(a_b, w_b, "mm_fwd", out_dtype=out_dtype), (a_b, w_b, jnp.zeros((), a.dtype))

    def bwd(res, ct):
        a_b, w_b, like_a = res
        ct_b = ct.astype(BF16)
        return (_mm_call(ct_b, w_b, "mm_dx", tb=True, out_dtype=like_a.dtype),
                _mm_call(a_b, ct_b, "mm_dw", ta=True))

    op.defvjp(fwd, bwd)
    return op


mm = _make_mm(F32)
mm_bf16 = _make_mm(BF16)


def _rows_specs(rows, tm, ncb, bc):
    specs = []
    for idx, r in enumerate(rows):
        if idx in bc:
            specs.append(pl.BlockSpec((tm, r.shape[1]), lambda i, j: (i, 0)))
        else:
            specs.append(pl.BlockSpec((tm, r.shape[1] // ncb), lambda i, j: (i, j)))
    return specs


def _param_specs(params):
    return [pl.BlockSpec(p.shape, lambda i, j: (0, 0)) for p in params]


def _group_slices(refs, groups, g, whole):
    out = []
    for idx, r in enumerate(refs):
        w = r.shape[1] // groups
        out.append((r[...] if idx in whole else r[:, g * w:(g + 1) * w]).astype(F32))
    return out


def _rows_fwd_call(fns, rows, params, outs, tm, ncb, bc, name, out_dtypes=None):
    r_total = rows[0].shape[0]
    nr, groups = len(rows), len(fns)
    out_dtypes = out_dtypes or [F32] * len(outs)

    def body(*refs):
        pvals = [r[...] for r in refs[nr:nr + len(params)]]
        for g, fn in enumerate(fns):
            res = fn(*_group_slices(refs[:nr], groups, g, bc), *pvals)
            for o_ref, val in zip(refs[nr + len(params):], res):
                w = o_ref.shape[1] // groups
                o_ref[:, g * w:(g + 1) * w] = val.astype(o_ref.dtype)

    return _call(
        body, name=name, grid=(r_total // tm, ncb),
        in_specs=_rows_specs(rows, tm, ncb, bc) + _param_specs(params),
        out_specs=[pl.BlockSpec((tm, w * groups), lambda i, j: (i, j)) for w in outs],
        out_shape=[jax.ShapeDtypeStruct((r_total, w * groups * ncb), dt) for w, dt in zip(outs, out_dtypes)],
        compiler_params=_cparams(("parallel", "parallel")),
    )(*rows, *params)


def _rows_bwd_call(fns, rows, params, cts, tm, ncb, bc, name):
    r_total = rows[0].shape[0]
    nr, npar, nct, groups = len(rows), len(params), len(cts), len(fns)

    def body(*refs):
        i, j = pl.program_id(0), pl.program_id(1)
        pvals = [r[...] for r in refs[nr:nr + npar]]
        ct_refs = refs[nr + npar:nr + npar + nct]
        d_refs = refs[nr + npar + nct:]
        shared = {idx: None for idx in list(bc) + list(range(nr, nr + npar))}
        for g, fn in enumerate(fns):
            _, vjp = jax.vjp(lambda *a, fn=fn: tuple(fn(*a)), *_group_slices(refs[:nr], groups, g, bc), *pvals)
            grads = vjp(tuple(_group_slices(ct_refs, groups, g, ())))
            for idx in range(nr + npar):
                if idx in shared:
                    shared[idx] = grads[idx] if shared[idx] is None else shared[idx] + grads[idx]
                else:
                    w = d_refs[idx].shape[1] // groups
                    d_refs[idx][:, g * w:(g + 1) * w] = grads[idx].astype(d_refs[idx].dtype)
        for idx, total in shared.items():
            first = (j == 0) if idx < nr else ((i == 0) & (j == 0))

            @pl.when(first)
            def _(idx=idx):
                d_refs[idx][...] = jnp.zeros_like(d_refs[idx])
            d_refs[idx][...] += total

    ct_specs = [pl.BlockSpec((tm, c.shape[1] // ncb), lambda i, j: (i, j)) for c in cts]
    return _call(
        body, name=name + "_bwd", grid=(r_total // tm, ncb),
        in_specs=_rows_specs(rows, tm, ncb, bc) + _param_specs(params) + ct_specs,
        out_specs=_rows_specs(rows, tm, ncb, bc) + _param_specs(params),
        out_shape=[jax.ShapeDtypeStruct(a.shape, a.dtype) for a in list(rows) + list(params)],
        compiler_params=_cparams(("arbitrary", "arbitrary")),
    )(*rows, *params, *cts)


def rowop(fn, name, outs, tm, ncb=1, bc=(), out_dtypes=None):
    fns = list(fn) if isinstance(fn, (list, tuple)) else [fn]

    @jax.custom_vjp
    def op(rows, params):
        return tuple(_rows_fwd_call(fns, rows, params, outs, tm, ncb, bc, name, out_dtypes))

    def fwd(rows, params):
        return op(rows, params), (rows, params)

    def bwd(res, cts):
        rows, params = res
        d = _rows_bwd_call(fns, rows, params, cts, tm, ncb, bc, name)
        return tuple(d[:len(rows)]), tuple(d[len(rows):])

    op.defvjp(fwd, bwd)
    return op


def _sigmoid(x):
    return 1.0 / (1.0 + jnp.exp(-x))


def _silu(x):
    return x * _sigmoid(x)


def _softplus(x):
    return jnp.maximum(x, 0.0) + jnp.log(1.0 + jnp.exp(-jnp.abs(x)))


def _f_rmsnorm(x, g):
    return (x * lax.rsqrt(jnp.mean(x * x, axis=-1, keepdims=True) + EPS) * g,)


def _f_qnorm(x, g):
    return (x * lax.rsqrt(jnp.mean(x * x, axis=-1, keepdims=True) + EPS) * (g * (FOX_HD ** -0.5)),)


def _f_logsig(x, b):
    return (-_softplus(-(x + b)),)


def _f_gdn_q(x):
    y = _silu(x)
    return (y * lax.rsqrt(jnp.sum(y * y, axis=-1, keepdims=True) + EPS) * (GDN_HD ** -0.5),)


def _f_gdn_k(x):
    y = _silu(x)
    return (y * lax.rsqrt(jnp.sum(y * y, axis=-1, keepdims=True) + EPS),)


def _f_gdn_v(x):
    return (_silu(x),)


def _f_gdn_gates(bl, al, a_log, dt_bias):
    return _sigmoid(bl), -jnp.exp(a_log) * _softplus(al + dt_bias)


def _f_gdn_out(o, z, g):
    return (o * lax.rsqrt(jnp.mean(o * o, axis=-1, keepdims=True) + EPS) * g * _silu(z),)


def _f_merge(g0, g1, ya, yb):
    return (_sigmoid(g0) * ya + _sigmoid(g1) * yb,)


def _f_residual(a, b, keep):
    return ((a + b) * keep,)


def _f_residual_norm(a, b, keep, g):
    r = (a + b) * keep
    return r, _f_rmsnorm(r, g)[0]


def _f_glu(a, b):
    return (_silu(a) * b,)


def _glu_call(up, ct):
    t_total, two_f = up.shape
    f = two_f // 2
    tm = _tile(t_total, 128, 16)
    wc = _tile(f, 1408, LANE)

    def body(*refs):
        up_ref, out_ref = refs[0], refs[-1]
        for c0 in range(0, f, wc):
            a, b = up_ref[:, c0:c0 + wc].astype(F32), up_ref[:, f + c0:f + c0 + wc].astype(F32)
            if ct is None:
                out_ref[:, c0:c0 + wc] = _f_glu(a, b)[0].astype(out_ref.dtype)
            else:
                _, vjp = jax.vjp(_f_glu, a, b)
                da, db = vjp((refs[1][:, c0:c0 + wc].astype(F32),))
                out_ref[:, c0:c0 + wc] = da.astype(out_ref.dtype)
                out_ref[:, f + c0:f + c0 + wc] = db.astype(out_ref.dtype)

    wide = pl.BlockSpec((tm, two_f), lambda i: (i, 0))
    narrow = pl.BlockSpec((tm, f), lambda i: (i, 0))
    return _call(
        body, name="ffn_glu" if ct is None else "ffn_glu_bwd", grid=(t_total // tm,),
        in_specs=[wide] if ct is None else [wide, narrow], out_specs=narrow if ct is None else wide,
        out_shape=jax.ShapeDtypeStruct((t_total, f), BF16) if ct is None else jax.ShapeDtypeStruct(up.shape, up.dtype),
        compiler_params=_cparams(("parallel",)),
    )(*((up,) if ct is None else (up, ct)))


@jax.custom_vjp
def glu(up):
    return _glu_call(up, None)


glu.defvjp(lambda up: (glu(up), up), lambda up, ct: (_glu_call(up, ct),))


def _shift_down(x, halo, s, row8):
    rx = pltpu.roll(x, s, 0)
    top = jnp.where(row8 < s, pltpu.roll(halo, s, 0), rx[:8])
    return jnp.concatenate([top, rx[8:]], axis=0)


def _shift_up(x, nxt, s, row8):
    tm = x.shape[0]
    rx = pltpu.roll(x, tm - s, 0)
    bot = jnp.where(row8 >= 8 - s, pltpu.roll(nxt, 8 - s, 0), rx[tm - 8:])
    return jnp.concatenate([rx[:tm - 8], bot], axis=0)


def _conv_tiles(r_total, c_total):
    return _tile(r_total, 768, 8), _tile(c_total, 1408, LANE)


def _halo_rows(dtype):
    return 16 if dtype == BF16 else 8


def _conv_fwd_call(x, w8, k_taps, out_dtype):
    r_total, c_total = x.shape
    tm, tc = _conv_tiles(r_total, c_total)
    hr = _halo_rows(x.dtype)
    hb = tm // hr

    def body(x_ref, halo_ref, w_ref, y_ref):
        i = pl.program_id(1)
        xt = x_ref[...].astype(F32)
        halo = jnp.where(i > 0, halo_ref[...].astype(F32)[hr - 8:hr], 0.0)
        row8 = lax.broadcasted_iota(jnp.int32, (8, tc), 0)
        acc = w_ref[k_taps - 1:k_taps, :] * xt
        for k in range(k_taps - 1):
            acc += w_ref[k:k + 1, :] * _shift_down(xt, halo, k_taps - 1 - k, row8)
        y_ref[...] = acc.astype(y_ref.dtype)

    return _call(
        body, name="dwconv_fwd", grid=(c_total // tc, r_total // tm),
        in_specs=[pl.BlockSpec((tm, tc), lambda c, i: (i, c)),
                  pl.BlockSpec((hr, tc), lambda c, i: (jnp.maximum(i * hb - 1, 0), c)),
                  pl.BlockSpec((8, tc), lambda c, i: (0, c))],
        out_specs=pl.BlockSpec((tm, tc), lambda c, i: (i, c)),
        out_shape=jax.ShapeDtypeStruct(x.shape, out_dtype),
        compiler_params=_cparams(("parallel", "parallel")),
    )(x, x, w8)


def _conv_bwd_call(x, w8, dy, k_taps):
    r_total, c_total = x.shape
    tm, tc = _conv_tiles(r_total, c_total)
    hr = _halo_rows(dy.dtype)
    hb = tm // hr
    n_i = r_total // tm

    def body(x_ref, w_ref, dy_ref, nxt_ref, dx_ref, dw_ref):
        i = pl.program_id(1)
        xt, dyt = x_ref[...].astype(F32), dy_ref[...].astype(F32)
        nxt = jnp.where(i < n_i - 1, nxt_ref[...].astype(F32)[0:8], 0.0)
        row8 = lax.broadcasted_iota(jnp.int32, (8, tc), 0)
        dx = w_ref[k_taps - 1:k_taps, :] * dyt
        upd = jnp.where(row8 == k_taps - 1, jnp.sum(dyt * xt, axis=0, keepdims=True), 0.0)
        for k in range(k_taps - 1):
            dy_ahead = _shift_up(dyt, nxt, k_taps - 1 - k, row8)
            dx += w_ref[k:k + 1, :] * dy_ahead
            upd = jnp.where(row8 == k, jnp.sum(dy_ahead * xt, axis=0, keepdims=True), upd)
        dx_ref[...] = dx.astype(dx_ref.dtype)

        @pl.when(i == 0)
        def _():
            dw_ref[...] = jnp.zeros_like(dw_ref)

        dw_ref[...] += upd

    return _call(
        body, name="dwconv_bwd", grid=(c_total // tc, n_i),
        in_specs=[pl.BlockSpec((tm, tc), lambda c, i: (i, c)),
                  pl.BlockSpec((8, tc), lambda c, i: (0, c)),
                  pl.BlockSpec((tm, tc), lambda c, i: (i, c)),
                  pl.BlockSpec((hr, tc), lambda c, i: (jnp.minimum((i + 1) * hb, r_total // hr - 1), c))],
        out_specs=[pl.BlockSpec((tm, tc), lambda c, i: (i, c)), pl.BlockSpec((8, tc), lambda c, i: (0, c))],
        out_shape=[jax.ShapeDtypeStruct(x.shape, x.dtype), jax.ShapeDtypeStruct(w8.shape, F32)],
        compiler_params=_cparams(("parallel", "arbitrary")),
    )(x, w8, dy, dy)


def make_dwconv(k_taps, out_dtype=None):
    @jax.custom_vjp
    def op(x, w8):
        return _conv_fwd_call(x, w8, k_taps, out_dtype or x.dtype)

    def fwd(x, w8):
        return op(x, w8), (x, w8)

    def bwd(res, dy):
        x, w8 = res
        dx, dw = _conv_bwd_call(x, w8, dy, k_taps)
        return dx, dw

    op.defvjp(fwd, bwd)
    return op


def _cumsum_call(x, reverse):
    h, t_total = x.shape
    tb = _tile(t_total, 256, LANE)
    nb = t_total // tb

    def body(x_ref, o_ref, carry_ref):
        i = pl.program_id(0)

        @pl.when(i == 0)
        def _():
            carry_ref[...] = jnp.zeros_like(carry_ref)

        r = lax.broadcasted_iota(jnp.int32, (tb, tb), 0)
        c = lax.broadcasted_iota(jnp.int32, (tb, tb), 1)
        tri = jnp.where((r >= c) if reverse else (r <= c), 1.0, 0.0).astype(F32)
        xv = x_ref[...]
        carry = jnp.max(carry_ref[...], axis=1, keepdims=True)
        o_ref[...] = _raw_dot(xv, tri, 1, 0, True) + carry
        carry_ref[...] = jnp.broadcast_to(carry + jnp.sum(xv, axis=1, keepdims=True), carry_ref.shape)

    imap = (lambda i: (0, nb - 1 - i)) if reverse else (lambda i: (0, i))
    return _call(
        body, name="cumsum_rev" if reverse else "cumsum", grid=(nb,),
        in_specs=[pl.BlockSpec((h, tb), imap)], out_specs=pl.BlockSpec((h, tb), imap),
        out_shape=jax.ShapeDtypeStruct(x.shape, F32), scratch_shapes=[pltpu.VMEM((h, LANE), F32)],
        compiler_params=_cparams(("arbitrary",)),
    )(x)


@jax.custom_vjp
def cumsum_lanes(x):
    return _cumsum_call(x, False)


cumsum_lanes.defvjp(lambda x: (cumsum_lanes(x), None), lambda _, ct: (_cumsum_call(ct, True),))


NEG_BIG = -1e30


def _attn_sub_tiles(nb):
    return max(s for s in (3, 2, 1) if nb % s == 0)


EXP_ZERO = -100.0
SMEM = pl.BlockSpec(memory_space=pltpu.SMEM)


def _max_row_norm_sq(x):
    h_total, t_total, hd = x.shape
    tb = _tile(t_total, 2816, 8)

    def body(x_ref, o_ref):
        @pl.when(pl.program_id(1) == 0)
        def _():
            o_ref[...] = jnp.zeros_like(o_ref)

        xv = x_ref[0]
        top = jnp.max(jnp.sum(xv * xv, axis=1, keepdims=True), axis=0, keepdims=True)
        o_ref[0] = jnp.maximum(o_ref[0], top)

    return _call(
        body, name="max_row_norm", grid=(h_total, t_total // tb),
        in_specs=[pl.BlockSpec((1, tb, hd), lambda h, i: (h, i, 0))],
        out_specs=pl.BlockSpec((1, 8, LANE), lambda h, i: (h, 0, 0)),
        out_shape=jax.ShapeDtypeStruct((h_total, 8, LANE), F32),
        compiler_params=_cparams(("parallel", "arbitrary")),
    )(x)


def _attn_skip_tables(q, k, f_row):
    bound = 2.0 * jnp.sqrt(_max_row_norm_sq(q)[:, 0, :1] * _max_row_norm_sq(k)[:, 0, :1])
    return EXP_ZERO - bound, f_row[:, :, 0, 0], f_row[:, :, 0, -1]


def _attn_fwd_call(q, k, v, f_col, f_row, tables):
    h_total, t_total, hd = q.shape
    blk = f_row.shape[-1]
    nb = t_total // blk
    nsub = _attn_sub_tiles(nb)
    tq = nsub * blk

    def body(thr_ref, first_ref, last_ref, q_ref, k_ref, vt_ref, fc_ref, fr_ref, o_ref, lse_ref):
        h = pl.program_id(0)
        i = pl.program_id(1)
        gap_needed = thr_ref[h, 0]
        f_tile = first_ref[h, i * nsub]
        j_start = lax.while_loop(lambda j: (j < i * nsub) & (f_tile - last_ref[h, j] < gap_needed),
                                 lambda j: j + 1, 0)
        r = lax.broadcasted_iota(jnp.int32, (blk, blk), 0)
        c = lax.broadcasted_iota(jnp.int32, (blk, blk), 1)
        qs = [q_ref[0, s * blk:(s + 1) * blk, :].astype(BF16) for s in range(nsub)]
        fqs = [fr_ref[0, i * nsub + s] for s in range(nsub)]

        def load_kv(j):
            off = pl.multiple_of(j * blk, blk)
            return k_ref[0, pl.ds(off, blk), :], vt_ref[0, j], fc_ref[0, pl.ds(off, blk), :]

        def tile(kv, s, carry, diagonal):
            kj, vtj, fk = kv
            m, l, acc = carry
            st = _raw_dot(kj, qs[s], 1, 1, False) + fqs[s] - fk
            if diagonal:
                st = jnp.where(r <= c, st, NEG_BIG)
            m_new = jnp.maximum(m, jnp.max(st, axis=0, keepdims=True))
            p = jnp.exp(st - m_new)
            alpha = jnp.exp(m - m_new)
            l = alpha * l + jnp.sum(p, axis=0, keepdims=True)
            acc = alpha * acc + _raw_dot(vtj, p, 1, 0, False)
            return m_new, l, acc

        def below_diagonal(j, carry):
            kv = load_kv(j)
            return tuple(tile(kv, s, carry[s], False) for s in range(nsub))

        init = tuple((jnp.full((1, blk), NEG_BIG, F32), jnp.zeros((1, blk), F32), jnp.zeros((hd, blk), F32))
                     for _ in range(nsub))
        carry = list(lax.fori_loop(j_start, i * nsub, below_diagonal, init))
        for d in range(nsub):
            kv = load_kv(i * nsub + d)
            for s in range(d, nsub):
                carry[s] = tile(kv, s, carry[s], s == d)
        for s, (m, l, acc) in enumerate(carry):
            o_ref[0, :, s * blk:(s + 1) * blk] = acc / l
            lse_ref[0, s] = m + jnp.log(l)

    vt = v.reshape(h_total, nb, blk, hd).transpose(0, 1, 3, 2).astype(BF16)
    return _call(
        body, name="fox_fwd", grid=(h_total, nb // nsub),
        in_specs=[SMEM, SMEM, SMEM,
                  pl.BlockSpec((1, tq, hd), lambda h, i: (h, i, 0)),
                  pl.BlockSpec((1, t_total, hd), lambda h, i: (h, 0, 0)),
                  pl.BlockSpec((1, nb, hd, blk), lambda h, i: (h, 0, 0, 0)),
                  pl.BlockSpec((1, t_total, 1), lambda h, i: (h, 0, 0)),
                  pl.BlockSpec((1, nb, 1, blk), lambda h, i: (h, 0, 0, 0))],
        out_specs=[pl.BlockSpec((1, hd, tq), lambda h, i: (h, 0, i)),
                   pl.BlockSpec((1, nsub, 1, blk), lambda h, i: (h, i, 0, 0))],
        out_shape=[jax.ShapeDtypeStruct((h_total, hd, t_total), F32), jax.ShapeDtypeStruct(f_row.shape, F32)],
        compiler_params=_cparams(("parallel", "parallel")),
    )(*tables, q, k.astype(BF16), vt, f_col, f_row)


def _attn_bwd_call(q, k, v, f_col, f_row, tables, lse_row, delta_row, do_blk):
    h_total, t_total, hd = q.shape
    blk = f_row.shape[-1]
    nb = t_total // blk
    nsub = _attn_sub_tiles(nb)
    tkv = nsub * blk

    def body(thr_ref, first_ref, last_ref, q_ref, do_ref, k_ref, v_ref, fc_ref, fr_ref, lse_ref, dl_ref,
             dq_ref, dk_ref, dv_ref, dfk_ref, dfq_ref):
        h = pl.program_id(0)
        j = pl.program_id(1)
        gap_needed = thr_ref[h, 0]
        f_tile = last_ref[h, j * nsub + nsub - 1]
        i_stop = lax.while_loop(lambda i: (i < nb) & (first_ref[h, jnp.minimum(i, nb - 1)] - f_tile >= gap_needed),
                                lambda i: i + 1, (j + 1) * nsub)

        @pl.when(j == 0)
        def _():
            dq_ref[...] = jnp.zeros_like(dq_ref)
            dfq_ref[...] = jnp.zeros_like(dfq_ref)

        ks = [k_ref[0, s * blk:(s + 1) * blk, :].astype(BF16) for s in range(nsub)]
        vs = [v_ref[0, s * blk:(s + 1) * blk, :].astype(BF16) for s in range(nsub)]
        fks = [fc_ref[0, s * blk:(s + 1) * blk, :] for s in range(nsub)]
        r = lax.broadcasted_iota(jnp.int32, (blk, blk), 0)
        c = lax.broadcasted_iota(jnp.int32, (blk, blk), 1)

        def q_step(i, accs, subs):
            off = pl.multiple_of(i * blk, blk)
            qi = q_ref[0, pl.ds(off, blk), :]
            doi = do_ref[0, i]
            fq, lse, dl = fr_ref[0, i], lse_ref[0, i], dl_ref[0, i]
            accs = list(accs)
            dq_i, dfq_i = None, None
            for s, diagonal in subs:
                dk, dv, dfk = accs[s]
                st = _raw_dot(ks[s], qi, 1, 1, False) + fq - fks[s] - lse
                if diagonal:
                    st = jnp.where(r <= c, st, NEG_BIG)
                pt = jnp.exp(st)
                dv = dv + _raw_dot(pt, doi, 1, 1, False)
                dst = pt * (_raw_dot(vs[s], doi, 1, 0, False) - dl)
                dk = dk + _raw_dot(dst, qi, 1, 0, False)
                dfk = dfk - jnp.sum(dst, axis=1, keepdims=True)
                accs[s] = (dk, dv, dfk)
                dq_s = _raw_dot(dst, ks[s], 0, 0, False)
                dfq_s = jnp.sum(dst, axis=0, keepdims=True)
                dq_i = dq_s if dq_i is None else dq_i + dq_s
                dfq_i = dfq_s if dfq_i is None else dfq_i + dfq_s
            dfq_ref[0, i] += dfq_i
            dq_ref[0, pl.ds(off, blk), :] += dq_i
            return tuple(accs)

        accs = tuple((jnp.zeros((blk, hd), F32), jnp.zeros((blk, hd), F32), jnp.zeros((blk, 1), F32))
                     for _ in range(nsub))
        for d in range(nsub):
            accs = q_step(j * nsub + d, accs, [(s, s == d) for s in range(d + 1)])
        accs = lax.fori_loop((j + 1) * nsub, i_stop,
                             lambda i, a: q_step(i, a, [(s, False) for s in range(nsub)]), accs)
        for s, (dk, dv, dfk) in enumerate(accs):
            dk_ref[0, s * blk:(s + 1) * blk, :] = dk
            dv_ref[0, s * blk:(s + 1) * blk, :] = dv
            dfk_ref[0, s * blk:(s + 1) * blk, :] = dfk

    full = pl.BlockSpec((1, t_total, hd), lambda h, j: (h, 0, 0))
    tile = pl.BlockSpec((1, tkv, hd), lambda h, j: (h, j, 0))
    col = pl.BlockSpec((1, tkv, 1), lambda h, j: (h, j, 0))
    rows = pl.BlockSpec((1, nb, 1, blk), lambda h, j: (h, 0, 0, 0))
    do_blocks = pl.BlockSpec((1, nb, hd, blk), lambda h, j: (h, 0, 0, 0))
    return _call(
        body, name="fox_bwd", grid=(h_total, nb // nsub),
        in_specs=[SMEM, SMEM, SMEM, full, do_blocks, tile, tile, col, rows, rows, rows],
        out_specs=[full, tile, tile, col, rows],
        out_shape=[jax.ShapeDtypeStruct(q.shape, F32), jax.ShapeDtypeStruct(q.shape, F32),
                   jax.ShapeDtypeStruct(q.shape, F32), jax.ShapeDtypeStruct(f_col.shape, F32),
                   jax.ShapeDtypeStruct(f_row.shape, F32)],
        compiler_params=_cparams(("parallel", "arbitrary")),
    )(*tables, q.astype(BF16), do_blk, k, v, f_col, f_row, lse_row, delta_row)


def _attn_delta_call(do_t, o_t):
    h_total, hd, t_total = o_t.shape
    tb = _tile(t_total, 2816, LANE)

    def body(do_ref, o_ref, d_ref):
        d_ref[0] = jnp.sum(do_ref[0] * o_ref[0], axis=0, keepdims=True)

    spec = pl.BlockSpec((1, hd, tb), lambda h, i: (h, 0, i))
    return _call(
        body, name="fox_delta", grid=(h_total, t_total // tb), in_specs=[spec, spec],
        out_specs=pl.BlockSpec((1, 1, tb), lambda h, i: (h, 0, i)),
        out_shape=jax.ShapeDtypeStruct((h_total, 1, t_total), F32),
        compiler_params=_cparams(("parallel", "parallel")),
    )(do_t, o_t)


@jax.custom_vjp
def fox_attention(q, k, v, f_col, f_row):
    return _attn_fwd_call(q, k, v, f_col, f_row, _attn_skip_tables(q, k, f_row))[0]


def _fox_fwd(q, k, v, f_col, f_row):
    tables = _attn_skip_tables(q, k, f_row)
    o_t, lse_row = _attn_fwd_call(q, k, v, f_col, f_row, tables)
    return o_t, (q, k, v, f_col, f_row, tables, o_t, lse_row)


def _fox_bwd(res, do_t):
    q, k, v, f_col, f_row, tables, o_t, lse_row = res
    h_total, t_total, hd = q.shape
    nb, blk = f_row.shape[1], f_row.shape[3]
    delta = _attn_delta_call(do_t, o_t).reshape(f_row.shape)
    do_blk = do_t.reshape(h_total, hd, nb, blk).transpose(0, 2, 1, 3).astype(BF16)
    return tuple(_attn_bwd_call(q, k, v, f_col, f_row, tables, lse_row, delta, do_blk))


fox_attention.defvjp(_fox_fwd, _fox_bwd)


def _head_col(blk, h):
    lane = lax.broadcasted_iota(jnp.int32, blk.shape, 1)
    return jnp.sum(jnp.where(lane == h, blk, 0.0), axis=1, keepdims=True)


@jax.custom_vjp
def _cat2(a, b):
    return jnp.concatenate([a, b], axis=1)


_cat2.defvjp(lambda a, b: (_cat2(a, b), a.shape[1]), lambda na, ct: (ct[:, :na], ct[:, na:]))


@jax.custom_vjp
def _split2(x):
    half = x.shape[1] // 2
    return x[:, :half], x[:, half:]


_split2.defvjp(lambda x: (_split2(x), None), lambda _, cts: (jnp.concatenate(cts, axis=1),))


def _neumann_solve(m, b):
    x = b - _raw_dot(m, b, 1, 0, False)
    powers, steps = [m], 1
    while 2 * steps < GDN_CHUNK:
        powers.append(_raw_dot(powers[-1], powers[-1], 1, 0, False))
        x = x + _raw_dot(powers[-1], x, 1, 0, False)
        steps *= 2
    return x, powers


@jax.custom_vjp
def _unit_lower_solve(m, b):
    return _neumann_solve(m, b)[0]


def _unit_lower_solve_fwd(m, b):
    x, powers = _neumann_solve(m, b)
    return x, (powers, x)


def _unit_lower_solve_bwd(res, dx):
    powers, x = res
    db = dx - _raw_dot(powers[0], dx, 0, 0, False)
    for p in powers[1:]:
        db = db + _raw_dot(p, db, 0, 0, False)
    return -_raw_dot(db, x, 1, 1, False), db


_unit_lower_solve.defvjp(_unit_lower_solve_fwd, _unit_lower_solve_bwd)


@jax.custom_vjp
def _unit_lower_solve_known(m, b, x):
    return x


def _unit_lower_solve_known_bwd(res, dx):
    m, x = res
    powers, steps = [m], 1
    while 2 * steps < GDN_CHUNK:
        powers.append(_raw_dot(powers[-1], powers[-1], 1, 0, False))
        steps *= 2
    dm, db = _unit_lower_solve_bwd((powers, x), dx)
    return dm, db, jnp.zeros_like(x)


_unit_lower_solve_known.defvjp(lambda m, b, x: (x, (m, x)), _unit_lower_solve_known_bwd)


def _gdn_intra(h, q, k, v, b_blk, g_blk, uw_known=None):
    n = q.shape[0]
    b, g = _head_col(b_blk, h), _head_col(g_blk, h)
    r = lax.broadcasted_iota(jnp.int32, (n, n), 0)
    c = lax.broadcasted_iota(jnp.int32, (n, n), 1)
    same = (r // GDN_CHUNK) == (c // GDN_CHUNK)
    incl = same & (r >= c)
    g_row = jnp.sum(jnp.where(r == c, g, 0.0), axis=0, keepdims=True)
    big_g = jnp.sum(jnp.where(incl, g_row, 0.0), axis=1, keepdims=True)
    big_g_row = jnp.sum(jnp.where(same & (r <= c), g, 0.0), axis=0, keepdims=True)
    g_tot = jnp.sum(jnp.where(same, g_row, 0.0), axis=1, keepdims=True)
    dec = jnp.where(incl, jnp.exp(jnp.where(incl, big_g - big_g_row, 0.0)), 0.0)
    dec_strict = jnp.where(r > c, dec, 0.0)
    e_g = jnp.exp(big_g)
    kb = k * b
    m = _dot(kb, k, 1, 1) * dec_strict
    rs = lax.broadcasted_iota(jnp.int32, (n, GDN_CHUNK), 0)
    cs = lax.broadcasted_iota(jnp.int32, (n, GDN_CHUNK), 1)
    fold = jnp.where(rs % GDN_CHUNK == cs, 1.0, 0.0).astype(F32)
    aqk = _dot(_dot(q, k, 1, 1) * dec, fold, 1, 0, True)
    rhs = _cat2(v * b, kb * e_g)
    u, w = _split2(_unit_lower_solve(m, rhs) if uw_known is None else
                   _unit_lower_solve_known(m, rhs, jnp.concatenate(uw_known, axis=1)))
    lane = lax.broadcasted_iota(jnp.int32, b_blk.shape, 1)
    return u, w, q * e_g, k * jnp.exp(g_tot - big_g), aqk, jnp.where(lane == h, g_tot, 0.0)


def _gdn_rec(h, s, u, w, qg, kd, aqk, gl_blk):
    g_last = jnp.max(_head_col(gl_blk, h), axis=0, keepdims=True)
    big_u = u - _dot(w, s)
    o = _dot(qg, s) + _dot(aqk, big_u)
    s_next = s * jnp.exp(g_last) + _dot(kd, big_u, 0, 0)
    return o, s_next


GDN_TOK_BLK = 256


def _gdn_layout(t_total, rev):
    tb = _tile(t_total, GDN_TOK_BLK, GDN_CHUNK)
    cb, nblk = tb // GDN_CHUNK, t_total // tb
    pos = (lambda i: nblk - 1 - i) if rev else (lambda i: i)
    specs = dict(
        tok=pl.BlockSpec((tb, GDN_W), lambda i: (pos(i), 0)),
        q=pl.BlockSpec((tb, GDN_W), lambda i: (pos(i), 0)),
        k=pl.BlockSpec((tb, GDN_W), lambda i: (pos(i), 1)),
        v=pl.BlockSpec((tb, GDN_W), lambda i: (pos(i), 2)),
        qkv=pl.BlockSpec((tb, 3 * GDN_W), lambda i: (pos(i), 0)),
        gate=pl.BlockSpec((tb, GDN_HEADS), lambda i: (pos(i), 0)),
        aqk=pl.BlockSpec((GDN_HEADS, tb, GDN_CHUNK), lambda i: (0, pos(i), 0)),
        state=pl.BlockSpec((GDN_HEADS, cb, GDN_HD, GDN_HD), lambda i: (0, pos(i), 0, 0)))
    return cb, nblk, specs


def _gdn_shapes(t_total):
    n_chunks = t_total // GDN_CHUNK
    return dict(tok=jax.ShapeDtypeStruct((t_total, GDN_W), BF16),
                out=jax.ShapeDtypeStruct((t_total, GDN_W), F32),
                gate=jax.ShapeDtypeStruct((t_total, GDN_HEADS), F32),
                aqk=jax.ShapeDtypeStruct((GDN_HEADS, t_total, GDN_CHUNK), F32),
                state=jax.ShapeDtypeStruct((GDN_HEADS, n_chunks, GDN_HD, GDN_HD), F32))


def _chunk_rows(ci):
    return pl.ds(pl.multiple_of(ci * GDN_CHUNK, GDN_CHUNK), GDN_CHUNK)


def _head_cols(h):
    return pl.ds(h * GDN_HD, GDN_HD)


def _gdn_intra_fwd_call(qkv, b, g):
    cb, nblk, sp = _gdn_layout(qkv.shape[0], False)
    sh = _gdn_shapes(qkv.shape[0])

    def body(q_ref, k_ref, v_ref, b_ref, g_ref, u_ref, w_ref, qg_ref, kd_ref, aqk_ref, gl_ref):
        b_blk, g_blk = b_ref[...], g_ref[...]
        gl = jnp.zeros(b_blk.shape, F32)
        for h in range(GDN_HEADS):
            cols = _head_cols(h)
            u, w, qg, kd, aqk, gl_h = _gdn_intra(h, q_ref[:, cols], k_ref[:, cols], v_ref[:, cols], b_blk, g_blk)
            u_ref[:, cols] = u.astype(u_ref.dtype)
            w_ref[:, cols] = w.astype(w_ref.dtype)
            qg_ref[:, cols] = qg.astype(qg_ref.dtype)
            kd_ref[:, cols] = kd.astype(kd_ref.dtype)
            aqk_ref[h] = aqk
            gl = gl + gl_h
        gl_ref[...] = gl

    return _call(
        body, name="gdn_intra_fwd", grid=(nblk,),
        in_specs=[sp["q"], sp["k"], sp["v"]] + [sp["gate"]] * 2,
        out_specs=[sp["tok"]] * 4 + [sp["aqk"], sp["gate"]],
        out_shape=[sh["tok"]] * 4 + [sh["aqk"], sh["gate"]],
        compiler_params=_cparams(("parallel",)),
    )(qkv, qkv, qkv, b, g)


def _gdn_intra_bwd_call(qkv, b, g, u, w, du, dw, dqg, dkd, daqk, dgl):
    cb, nblk, sp = _gdn_layout(qkv.shape[0], False)
    sh = _gdn_shapes(qkv.shape[0])

    def body(q_ref, k_ref, v_ref, b_ref, g_ref, u_ref, w_ref, du_ref, dw_ref, dqg_ref, dkd_ref, daqk_ref, dgl_ref,
             dqkv_ref, db_ref, dg_ref):
        b_blk, g_blk, dgl = b_ref[...], g_ref[...], dgl_ref[...]
        db = jnp.zeros(b_blk.shape, F32)
        dg = jnp.zeros(b_blk.shape, F32)
        for h in range(GDN_HEADS):
            cols = _head_cols(h)
            known = (u_ref[:, cols].astype(F32), w_ref[:, cols].astype(F32))
            _, vjp = jax.vjp(functools.partial(_gdn_intra, h, uw_known=known),
                             q_ref[:, cols], k_ref[:, cols], v_ref[:, cols], b_blk, g_blk)
            dq, dk, dv, db_h, dg_h = vjp((*[r[:, cols].astype(F32) for r in (du_ref, dw_ref, dqg_ref, dkd_ref)],
                                          daqk_ref[h], dgl))
            dqkv_ref[:, pl.ds(h * GDN_HD, GDN_HD)] = dq
            dqkv_ref[:, pl.ds(GDN_W + h * GDN_HD, GDN_HD)] = dk
            dqkv_ref[:, pl.ds(2 * GDN_W + h * GDN_HD, GDN_HD)] = dv
            db = db + db_h
            dg = dg + dg_h
        db_ref[...] = db
        dg_ref[...] = dg

    return _call(
        body, name="gdn_intra_bwd", grid=(nblk,),
        in_specs=[sp["q"], sp["k"], sp["v"]] + [sp["gate"]] * 2 + [sp["tok"]] * 6 + [sp["aqk"], sp["gate"]],
        out_specs=[sp["qkv"]] + [sp["gate"]] * 2,
        out_shape=[jax.ShapeDtypeStruct(qkv.shape, F32)] + [sh["gate"]] * 2,
        compiler_params=_cparams(("parallel",)),
    )(qkv, qkv, qkv, b, g, u, w, du, dw, dqg, dkd, daqk, dgl)


def _gdn_rec_fwd_call(u, w, qg, kd, aqk, gl):
    cb, nblk, sp = _gdn_layout(u.shape[0], False)
    sh = _gdn_shapes(u.shape[0])

    def body(u_ref, w_ref, qg_ref, kd_ref, aqk_ref, gl_ref, o_ref, s_all_ref, s_ref):
        @pl.when(pl.program_id(0) == 0)
        def _():
            s_ref[...] = jnp.zeros_like(s_ref)

        def chunk(ci, carry):
            rows = _chunk_rows(ci)
            gl_row = gl_ref[rows, :]
            states = [s_ref[h] for h in range(GDN_HEADS)]
            res = [_gdn_rec(h, states[h], *[r[rows, _head_cols(h)].astype(F32) for r in (u_ref, w_ref, qg_ref, kd_ref)],
                            aqk_ref[h, rows, :], gl_row)
                   for h in range(GDN_HEADS)]
            for h, (o, s_next) in enumerate(res):
                s_all_ref[h, ci] = states[h]
                o_ref[rows, _head_cols(h)] = o
                s_ref[h] = s_next
            return carry

        lax.fori_loop(0, cb, chunk, 0)

    return _call(
        body, name="gdn_rec_fwd", grid=(nblk,),
        in_specs=[sp["tok"]] * 4 + [sp["aqk"], sp["gate"]],
        out_specs=[sp["tok"], sp["state"]], out_shape=[sh["out"], sh["state"]],
        scratch_shapes=[pltpu.VMEM((GDN_HEADS, GDN_HD, GDN_HD), F32)],
        compiler_params=_cparams(("arbitrary",)),
    )(u, w, qg, kd, aqk, gl)


def _gdn_rec_bwd_call(u, w, qg, kd, aqk, gl, s_all, do):
    cb, nblk, sp = _gdn_layout(u.shape[0], True)
    sh = _gdn_shapes(u.shape[0])

    def body(u_ref, w_ref, qg_ref, kd_ref, aqk_ref, gl_ref, s_all_ref, do_ref,
             du_ref, dw_ref, dqg_ref, dkd_ref, daqk_ref, dgl_ref, ds_ref):
        @pl.when(pl.program_id(0) == 0)
        def _():
            ds_ref[...] = jnp.zeros_like(ds_ref)

        def chunk(step, carry):
            ci = cb - 1 - step
            rows = _chunk_rows(ci)
            gl_row = gl_ref[rows, :]
            res = []
            for h in range(GDN_HEADS):
                cols = _head_cols(h)
                _, vjp = jax.vjp(functools.partial(_gdn_rec, h), s_all_ref[h, ci],
                                 *[r[rows, cols].astype(F32) for r in (u_ref, w_ref, qg_ref, kd_ref)],
                                 aqk_ref[h, rows, :], gl_row)
                res.append(vjp((do_ref[rows, cols], ds_ref[h])))
            dgl = jnp.zeros((GDN_CHUNK, GDN_HEADS), F32)
            for h, (ds, du, dw, dqg, dkd, daqk, dgl_h) in enumerate(res):
                cols = _head_cols(h)
                ds_ref[h] = ds
                du_ref[rows, cols] = du.astype(du_ref.dtype)
                dw_ref[rows, cols] = dw.astype(dw_ref.dtype)
                dqg_ref[rows, cols] = dqg.astype(dqg_ref.dtype)
                dkd_ref[rows, cols] = dkd.astype(dkd_ref.dtype)
                daqk_ref[h, rows, :] = daqk
                dgl = dgl + dgl_h
            dgl_ref[rows, :] = dgl
            return carry

        lax.fori_loop(0, cb, chunk, 0)

    return _call(
        body, name="gdn_rec_bwd", grid=(nblk,),
        in_specs=[sp["tok"]] * 4 + [sp["aqk"], sp["gate"], sp["state"], sp["tok"]],
        out_specs=[sp["tok"]] * 4 + [sp["aqk"], sp["gate"]],
        out_shape=[sh["tok"]] * 4 + [sh["aqk"], sh["gate"]],
        scratch_shapes=[pltpu.VMEM((GDN_HEADS, GDN_HD, GDN_HD), F32)],
        compiler_params=_cparams(("arbitrary",)),
    )(u, w, qg, kd, aqk, gl, s_all, do)


@jax.custom_vjp
def gdn_intra(qkv, b, g):
    return tuple(_gdn_intra_fwd_call(qkv, b, g))


def _gdn_intra_fwd(*a):
    outs = gdn_intra(*a)
    return outs, a + (outs[0], outs[1])


gdn_intra.defvjp(_gdn_intra_fwd, lambda res, cts: tuple(_gdn_intra_bwd_call(*res, *cts)))


@jax.custom_vjp
def gdn_rec(u, w, qg, kd, aqk, gl):
    return _gdn_rec_fwd_call(u, w, qg, kd, aqk, gl)[0]


def _gdn_rec_fwd(*a):
    o, s_all = _gdn_rec_fwd_call(*a)
    return o, a + (s_all,)


gdn_rec.defvjp(_gdn_rec_fwd, lambda res, do: tuple(_gdn_rec_bwd_call(*res, do)))


def gated_delta(qkv, b, g):
    return gdn_rec(*gdn_intra(qkv, b, g))


def _loss_call(y, tgt, first, last):
    r_total, d = y.shape
    tm = _tile(r_total, 256, 8)

    def body(y_ref, t_ref, loss_ref, dy_ref):
        i = pl.program_id(0)

        @pl.when(i == 0)
        def _():
            loss_ref[...] = jnp.zeros_like(loss_ref)

        row = lax.broadcasted_iota(jnp.int32, (tm, d), 0) + i * tm
        err = jnp.where((row >= first) & (row < last), y_ref[...] - t_ref[...], 0.0)
        dy_ref[...] = err * (1.0 / d)
        part = jnp.sum(jnp.sum(err * err, axis=1, keepdims=True), axis=0, keepdims=True) * (0.5 / d)
        loss_ref[...] += jnp.broadcast_to(part, loss_ref.shape)

    return _call(
        body, name="loss_head", grid=(r_total // tm,),
        in_specs=[pl.BlockSpec((tm, d), lambda i: (i, 0))] * 2,
        out_specs=[pl.BlockSpec((8, LANE), lambda i: (0, 0)), pl.BlockSpec((tm, d), lambda i: (i, 0))],
        out_shape=[jax.ShapeDtypeStruct((8, LANE), F32), jax.ShapeDtypeStruct(y.shape, F32)],
        compiler_params=_cparams(("arbitrary",)),
    )(y, tgt)


def make_loss(first, last):
    @jax.custom_vjp
    def op(y, tgt):
        return _loss_call(y, tgt, first, last)[0][0, 0]

    def fwd(y, tgt):
        loss, dy = _loss_call(y, tgt, first, last)
        return loss[0, 0], (dy,)

    def bwd(res, ct):
        return res[0] * ct, jnp.zeros_like(res[0])

    op.defvjp(fwd, bwd)
    return op


def _pad_rows8(w):
    return jnp.concatenate([w, jnp.zeros((8 - w.shape[0], w.shape[1]), w.dtype)], axis=0)


def local_loss(wts, x, tgt):
    seq = x.shape[0]
    n_tok = N_META + seq
    t_pad = -(-n_tok // ROW_ALIGN) * ROW_ALIGN
    depth = wts["norm1_g"].shape[0]
    blk = _tile(t_pad, ATT_BLK, LANE)
    nb = t_pad // blk
    tm = _tile(t_pad, 256, 8)

    rms = rowop(_f_rmsnorm, "rmsnorm", (D_MODEL,), tm, out_dtypes=[BF16])
    qnorm = rowop(_f_qnorm, "fox_q_norm", (FOX_HD,), _tile(FOX_HEADS * t_pad, 2048, 8))
    knorm = rowop(_f_rmsnorm, "fox_k_norm", (FOX_HD,), _tile(FOX_HEADS * t_pad, 2048, 8))
    logsig = rowop(_f_logsig, "fox_log_forget", (FOX_HEADS,), tm)
    gdn_act = rowop([_f_gdn_q] * GDN_HEADS + [_f_gdn_k] * GDN_HEADS + [_f_gdn_v] * GDN_HEADS, "gdn_qkv_act",
                    (GDN_HD,), tm)
    gates = rowop(_f_gdn_gates, "gdn_gates", (GDN_HEADS, GDN_HEADS), tm)
    gdn_out = rowop([_f_gdn_out] * GDN_HEADS, "gdn_out_norm", (GDN_HD,), tm, out_dtypes=[BF16])
    merge = rowop(_f_merge, "branch_merge", (D_MODEL,), tm, out_dtypes=[BF16])
    residual = rowop(_f_residual, "residual_add", (D_MODEL,), tm, bc=(2,))
    residual_norm = rowop(_f_residual_norm, "residual_add_norm", (D_MODEL, D_MODEL), tm, bc=(2,),
                          out_dtypes=[F32, BF16])
    keep = (jnp.arange(t_pad)[:, None] < n_tok).astype(F32)
    conv4 = make_dwconv(GDN_CONV, BF16)
    conv3 = make_dwconv(FFN_CONV)
    loss_op = make_loss(N_META, n_tok)

    zeros = jnp.zeros((t_pad - n_tok, D_MODEL), F32)
    h_res = jnp.concatenate([wts["meta_tokens"], x, zeros], axis=0)
    tgt_rows = jnp.concatenate([jnp.zeros((N_META, D_MODEL), F32), tgt, zeros], axis=0)

    def heads(a):
        return a.reshape(t_pad, FOX_HEADS, FOX_HD).transpose(1, 0, 2).reshape(FOX_HEADS * t_pad, FOX_HD)

    h = rms((h_res,), (wts["norm1_g"][0][None],))[0]
    for l in range(depth):
        proj = mm(h, wts["w_in"][l])
        qn = qnorm((heads(proj[:, 0:512]),), (wts["fox_q_norm_g"][l][None],))[0]
        kn = knorm((heads(proj[:, 512:1024]),), (wts["fox_k_norm_g"][l][None],))[0]
        vh = heads(proj[:, 1024:1536])
        log_f = logsig((proj[:, 1536:1544],), (wts["fox_f_bias"][l][None],))[0]
        f_cum = cumsum_lanes(log_f.T)
        o_a = fox_attention(qn.reshape(FOX_HEADS, t_pad, FOX_HD), kn.reshape(FOX_HEADS, t_pad, FOX_HD),
                            vh.reshape(FOX_HEADS, t_pad, FOX_HD), f_cum[:, :, None],
                            f_cum.reshape(FOX_HEADS, nb, 1, blk))
        y_a = mm_bf16(o_a.transpose(2, 0, 1).reshape(t_pad, FOX_W).astype(BF16), wts["w_branch_a"][l])
        cv = conv4(proj[:, 1664:4736], _pad_rows8(wts["gdn_conv_w"][l]))
        qkv = gdn_act((cv,), ())[0]
        beta, gdec = gates((proj[:, 4736:4744], proj[:, 4744:4752]),
                           (wts["gdn_a_log"][l][None], wts["gdn_dt_bias"][l][None]))
        o_b = gated_delta(qkv, beta, gdec)
        o_b = gdn_out((o_b, proj[:, 4864:5888]), (wts["gdn_norm_g"][l][None],))[0]
        y_b = mm_bf16(o_b, wts["w_branch_b"][l])
        mixed = merge((proj[:, 5888:6912], proj[:, 6912:7936], y_a, y_b), ())[0]
        h_res, h = residual_norm((h_res, mm(mixed, wts["w_out"][l]), keep), (wts["norm2_g"][l][None],))
        up = conv3(mm_bf16(h, wts["w_up"][l]), _pad_rows8(wts["ffn_conv_w"][l]))
        act = glu(up)
        down = mm(act, wts["w_down"][l])
        if l + 1 < depth:
            h_res, h = residual_norm((h_res, down, keep), (wts["norm1_g"][l + 1][None],))
        else:
            h_res = residual((h_res, down, keep), ())[0]
    return loss_op(h_res, tgt_rows)


def pad_w_in(w):
    parts, pos = [], 0
    for src, width, dst in IN_SEGS:
        if dst > pos:
            parts.append(jnp.zeros(w.shape[:-1] + (dst - pos,), w.dtype))
        parts.append(w[..., src:src + width])
        pos = dst + width
    parts.append(jnp.zeros(w.shape[:-1] + (D_IN_PAD - pos,), w.dtype))
    return jnp.concatenate(parts, axis=-1)


def unpad_w_in(w):
    return jnp.concatenate([w[..., dst:dst + width] for _, width, dst in IN_SEGS], axis=-1)


ANY = pl.BlockSpec(memory_space=pl.ANY)
N_CHIPS = 4
N_DEV = 8
COMM_COLS = 1024
COMM_ROW_ALIGN = 512
COMM_ROW_ALIGN_SMALL = 32


def _place():
    return lax.axis_index("x"), lax.axis_index("y"), lax.axis_index("c")


def _other_chips(x, y):
    return [(1 - x, y), (x, 1 - y), (1 - x, 1 - y)]


def _remote(src, dst, send_sem, recv_sem, dev):
    return pltpu.make_async_remote_copy(src_ref=src, dst_ref=dst, send_sem=send_sem, recv_sem=recv_sem,
                                        device_id=dev, device_id_type=MESH)


def chip_all_gather(buf):
    rows, cols = buf.shape
    half = rows // 2

    def body(x_ref, out_ref, send_sems, recv_sems, pass_send, pass_recv):
        x, y, c = _place()
        me = 2 * x + y
        mine, other = pl.ds(c * half, half), pl.ds((1 - c) * half, half)
        sibling = (x, y, 1 - c)
        chips = _other_chips(x, y)
        started = []
        for k, (px, py) in enumerate(chips):
            cp = _remote(x_ref.at[mine], out_ref.at[me, mine], send_sems.at[k], recv_sems.at[k], (px, py, c))
            cp.start()
            started.append(cp)
        for k, (px, py) in enumerate(chips):
            landed = out_ref.at[2 * px + py, mine]
            _remote(landed, landed, send_sems.at[k], recv_sems.at[k], (px, py, c)).wait_recv()
            cp = _remote(landed, landed, pass_send.at[k], pass_recv.at[k], sibling)
            cp.start()
            started.append(cp)
        for k, (px, py) in enumerate(chips):
            passed = out_ref.at[2 * px + py, other]
            _remote(passed, passed, pass_send.at[k], pass_recv.at[k], sibling).wait_recv()
        for cp in started:
            cp.wait_send()

    got = _call(
        body, name="chip_all_gather", in_specs=[ANY], out_specs=ANY,
        out_shape=jax.ShapeDtypeStruct((N_CHIPS, rows, cols), buf.dtype),
        scratch_shapes=[pltpu.SemaphoreType.DMA((3,)), pltpu.SemaphoreType.DMA((3,)),
                        pltpu.SemaphoreType.DMA((3,)), pltpu.SemaphoreType.DMA((3,))],
    )(buf)
    me = 2 * lax.axis_index("x") + lax.axis_index("y")
    return lax.dynamic_update_slice(got, buf[None], (me, 0, 0))


def sibling_swap_halves(g4):
    n, rows, cols = g4.shape
    half = rows // 2

    def body(g_ref, got_ref, send_sem, recv_sem):
        x, y, c = _place()
        cp = _remote(g_ref.at[:, pl.ds((1 - c) * half, half), :], got_ref, send_sem, recv_sem, (x, y, 1 - c))
        cp.start()
        cp.wait()

    return _call(
        body, name="sibling_swap_halves", in_specs=[ANY], out_specs=ANY,
        out_shape=jax.ShapeDtypeStruct((n, half, cols), g4.dtype),
        scratch_shapes=[pltpu.SemaphoreType.DMA, pltpu.SemaphoreType.DMA],
    )(g4)


def add_own_half(g4, got, c):
    n, rows, cols = g4.shape
    half = rows // 2
    tm = _tile(half, 256, 16)
    nt = half // tm

    def body(c_ref, a_ref, b_ref, o_ref):
        o_ref[...] = (a_ref[...] + b_ref[...]).astype(o_ref.dtype)

    return _call(
        body, name="add_own_half",
        grid_spec=pltpu.PrefetchScalarGridSpec(
            num_scalar_prefetch=1, grid=(n, nt),
            in_specs=[pl.BlockSpec((1, tm, cols), lambda j, i, c_ref: (j, c_ref[0] * nt + i, 0)),
                      pl.BlockSpec((1, tm, cols), lambda j, i, c_ref: (j, i, 0))],
            out_specs=pl.BlockSpec((1, tm, cols), lambda j, i, c_ref: (j, i, 0))),
        out_shape=jax.ShapeDtypeStruct(got.shape, BF16),
        compiler_params=_cparams(("parallel", "parallel")),
    )(c.reshape(1).astype(jnp.int32), g4, got)


def chip_scatter(p4):
    n, rows, cols = p4.shape

    def body(p_ref, out_ref, send_sems, recv_sems):
        x, y, c = _place()
        me = 2 * x + y
        chips = _other_chips(x, y)
        started = []
        for k, (px, py) in enumerate(chips):
            cp = _remote(p_ref.at[2 * px + py], out_ref.at[me], send_sems.at[k], recv_sems.at[k], (px, py, c))
            cp.start()
            started.append(cp)
        for k, (px, py) in enumerate(chips):
            landed = out_ref.at[2 * px + py]
            _remote(landed, landed, send_sems.at[k], recv_sems.at[k], (px, py, c)).wait_recv()
        for cp in started:
            cp.wait_send()

    got = _call(
        body, name="chip_scatter", in_specs=[ANY], out_specs=ANY,
        out_shape=jax.ShapeDtypeStruct(p4.shape, p4.dtype),
        scratch_shapes=[pltpu.SemaphoreType.DMA((3,)), pltpu.SemaphoreType.DMA((3,))],
    )(p4)
    me = 2 * lax.axis_index("x") + lax.axis_index("y")
    return lax.dynamic_update_slice(got, lax.dynamic_slice_in_dim(p4, me, 1, axis=0), (me, 0, 0))


def sum_slots(a):
    n, rows, cols = a.shape
    tm = _tile(rows, 256, 16) if rows % 16 == 0 else rows

    def body(a_ref, o_ref):
        acc = a_ref[0].astype(F32)
        for k in range(1, n):
            acc = acc + a_ref[k].astype(F32)
        o_ref[...] = acc

    return _call(
        body, name="sum_slots_%d" % n, grid=(rows // tm,),
        in_specs=[pl.BlockSpec((n, tm, cols), lambda i: (0, i, 0))],
        out_specs=pl.BlockSpec((tm, cols), lambda i: (i, 0)),
        out_shape=jax.ShapeDtypeStruct((rows, cols), F32),
        compiler_params=_cparams(("parallel",)),
    )(a)


def sibling_join(s):
    half, cols = s.shape

    def body(s_ref, got_ref, send_sem, recv_sem):
        x, y, c = _place()
        cp = _remote(s_ref, got_ref, send_sem, recv_sem, (x, y, 1 - c))
        cp.start()
        cp.wait()

    got = _call(
        body, name="sibling_join", in_specs=[ANY], out_specs=ANY,
        out_shape=jax.ShapeDtypeStruct(s.shape, s.dtype),
        scratch_shapes=[pltpu.SemaphoreType.DMA, pltpu.SemaphoreType.DMA],
    )(s)
    c = lax.axis_index("c")
    out = jnp.zeros((2 * half, cols), s.dtype)
    out = lax.dynamic_update_slice(out, s, (c * half, 0))
    return lax.dynamic_update_slice(out, got, ((1 - c) * half, 0))


def all_devices_gather(buf):
    rows, cols = buf.shape

    def body(b_ref, out_ref, send_sems, recv_sems, local_sem):
        x, y, c = _place()
        me = 4 * x + 2 * y + c
        local = pltpu.make_async_copy(b_ref, out_ref.at[me], local_sem)
        local.start()
        peers = [((x + dx) % 2, (y + dy) % 2, (c + dc) % 2)
                 for dx in (0, 1) for dy in (0, 1) for dc in (0, 1) if dx + dy + dc > 0]
        started = []
        for k, peer in enumerate(peers):
            cp = _remote(b_ref, out_ref.at[me], send_sems.at[k], recv_sems.at[k], peer)
            cp.start()
            started.append(cp)
        for k, (px, py, pc) in enumerate(peers):
            landed = out_ref.at[4 * px + 2 * py + pc]
            _remote(landed, landed, send_sems.at[k], recv_sems.at[k], (px, py, pc)).wait_recv()
        for cp in started:
            cp.wait_send()
        local.wait()

    return _call(
        body, name="all_devices_gather", in_specs=[ANY], out_specs=ANY,
        out_shape=jax.ShapeDtypeStruct((N_DEV, rows, cols), buf.dtype),
        scratch_shapes=[pltpu.SemaphoreType.DMA((7,)), pltpu.SemaphoreType.DMA((7,)), pltpu.SemaphoreType.DMA],
    )(buf)


def adamw(w, g, m, v):
    shape = w.shape
    w2, g2, m2, v2 = [a.reshape(-1, shape[-1]) for a in (w, g, m, v)]
    rows, cols = w2.shape
    tm = _tile(rows, 256, 8) if rows % 8 == 0 else rows

    def body(w_ref, g_ref, m_ref, v_ref, d_ref, nm_ref, nv_ref):
        gv = g_ref[...]
        nm = ADAM_B1 * m_ref[...] + (1.0 - ADAM_B1) * gv
        nv = ADAM_B2 * v_ref[...] + (1.0 - ADAM_B2) * (gv * gv)
        m_hat = nm / (1.0 - ADAM_B1 ** ADAM_STEP)
        v_hat = nv / (1.0 - ADAM_B2 ** ADAM_STEP)
        d_ref[...] = -ADAM_LR * (m_hat / (jnp.sqrt(v_hat) + ADAM_EPS) + ADAM_WD * w_ref[...])
        nm_ref[...] = nm
        nv_ref[...] = nv

    spec = pl.BlockSpec((tm, cols), lambda i: (i, 0))
    outs = _call(
        body, name="adamw", grid=(rows // tm,), in_specs=[spec] * 4, out_specs=[spec] * 3,
        out_shape=[jax.ShapeDtypeStruct((rows, cols), F32)] * 3,
        compiler_params=_cparams(("parallel",)),
    )(w2, g2, m2, v2)
    return [o.reshape(shape) for o in outs]


WEIGHTS = ("meta_tokens", "norm1_g", "w_in", "fox_f_bias", "fox_q_norm_g", "fox_k_norm_g", "gdn_conv_w",
           "gdn_a_log", "gdn_dt_bias", "gdn_norm_g", "w_branch_a", "w_branch_b", "w_out", "norm2_g", "w_up",
           "ffn_conv_w", "w_down")
SHARD_AXIS = {"meta_tokens": -1, "w_in": -1, "gdn_conv_w": -1, "w_branch_a": -1, "w_branch_b": -2, "w_out": -2,
              "w_up": -1, "ffn_conv_w": -1, "w_down": -2}
MATMUL_WEIGHTS = ("w_in", "w_branch_a", "w_branch_b", "w_out", "w_up", "w_down")
SMALL_SHARDED = ("meta_tokens", "gdn_conv_w", "ffn_conv_w")
REPLICATED = tuple(n for n in WEIGHTS if n not in SHARD_AXIS)


def _pack(arrays, dtype, row_align):
    flat = jnp.concatenate([a.reshape(-1).astype(dtype) for a in arrays])
    block = row_align * COMM_COLS
    total = -(-flat.shape[0] // block) * block
    flat = jnp.concatenate([flat, jnp.zeros((total - flat.shape[0],), dtype)])
    return flat.reshape(-1, COMM_COLS)


def _unpack(buf, shapes):
    flat, out, pos = buf.reshape(-1), [], 0
    for shape in shapes:
        size = 1
        for d in shape:
            size *= d
        out.append(flat[pos:pos + size].reshape(shape))
        pos += size
    return out


def _gather_full(shards, names, dtype, row_align):
    got = chip_all_gather(_pack([shards[n] for n in names], dtype, row_align))
    per_chip = [_unpack(got[j], [shards[n].shape for n in names]) for j in range(N_CHIPS)]
    return {n: jnp.concatenate([per_chip[j][i] for j in range(N_CHIPS)], axis=SHARD_AXIS[n]).astype(F32)
            for i, n in enumerate(names)}


def _shard_of(full, name, j):
    axis = SHARD_AXIS[name] % full.ndim
    size = full.shape[axis] // N_CHIPS
    return lax.slice_in_dim(full, j * size, (j + 1) * size, axis=axis)


def kernel(x, meta_tokens, norm1_g, w_in, fox_f_bias, fox_q_norm_g, fox_k_norm_g, gdn_conv_w, gdn_a_log, gdn_dt_bias, gdn_norm_g, w_branch_a, w_branch_b, w_out, norm2_g, w_up, ffn_conv_w, w_down, loss_target, m_meta_tokens, m_norm1_g, m_w_in, m_fox_f_bias, m_fox_q_norm_g, m_fox_k_norm_g, m_gdn_conv_w, m_gdn_a_log, m_gdn_dt_bias, m_gdn_norm_g, m_w_branch_a, m_w_branch_b, m_w_out, m_norm2_g, m_w_up, m_ffn_conv_w, m_w_down, v_meta_tokens, v_norm1_g, v_w_in, v_fox_f_bias, v_fox_q_norm_g, v_fox_k_norm_g, v_gdn_conv_w, v_gdn_a_log, v_gdn_dt_bias, v_gdn_norm_g, v_w_branch_a, v_w_branch_b, v_w_out, v_norm2_g, v_w_up, v_ffn_conv_w, v_w_down):
    w_loc = dict(zip(WEIGHTS, (meta_tokens, norm1_g, w_in, fox_f_bias, fox_q_norm_g, fox_k_norm_g, gdn_conv_w,
                               gdn_a_log, gdn_dt_bias, gdn_norm_g, w_branch_a, w_branch_b, w_out, norm2_g, w_up,
                               ffn_conv_w, w_down)))
    m_loc = dict(zip(WEIGHTS, (m_meta_tokens, m_norm1_g, m_w_in, m_fox_f_bias, m_fox_q_norm_g, m_fox_k_norm_g,
                               m_gdn_conv_w, m_gdn_a_log, m_gdn_dt_bias, m_gdn_norm_g, m_w_branch_a, m_w_branch_b,
                               m_w_out, m_norm2_g, m_w_up, m_ffn_conv_w, m_w_down)))
    v_loc = dict(zip(WEIGHTS, (v_meta_tokens, v_norm1_g, v_w_in, v_fox_f_bias, v_fox_q_norm_g, v_fox_k_norm_g,
                               v_gdn_conv_w, v_gdn_a_log, v_gdn_dt_bias, v_gdn_norm_g, v_w_branch_a, v_w_branch_b,
                               v_w_out, v_norm2_g, v_w_up, v_ffn_conv_w, v_w_down)))
    c = lax.axis_index("c")

    full = {n: w_loc[n] for n in REPLICATED}
    full.update(_gather_full(w_loc, MATMUL_WEIGHTS, BF16, COMM_ROW_ALIGN))
    full.update(_gather_full(w_loc, SMALL_SHARDED, F32, COMM_ROW_ALIGN_SMALL))
    full["w_in"] = pad_w_in(full["w_in"])

    loss, (g_full, g_x) = jax.value_and_grad(local_loss, argnums=(0, 1))(full, x[0], loss_target[0])
    g_full = dict(g_full)
    g_full["w_in"] = unpad_w_in(g_full["w_in"])

    sharded = MATMUL_WEIGHTS + SMALL_SHARDED
    g4 = jnp.stack([_pack([_shard_of(g_full[n], n, j) for n in sharded], F32, COMM_ROW_ALIGN)
                    for j in range(N_CHIPS)])
    pair_sum = add_own_half(g4, sibling_swap_halves(g4), c)
    g_shard = sibling_join(sum_slots(chip_scatter(pair_sum)))
    grads = dict(zip(sharded, _unpack(g_shard, [w_loc[n].shape for n in sharded])))
    g_rep = sum_slots(all_devices_gather(_pack([g_full[n] for n in REPLICATED], F32, 8)))
    grads.update(zip(REPLICATED, _unpack(g_rep, [w_loc[n].shape for n in REPLICATED])))

    loss = lax.psum(loss, ("x", "y", "c"))
    upd = {n: adamw(w_loc[n], grads[n], m_loc[n], v_loc[n]) for n in WEIGHTS}
    return (loss, g_x[None], *[grads[n] for n in WEIGHTS], *[upd[n][0] for n in WEIGHTS],
            *[upd[n][1] for n in WEIGHTS], *[upd[n][2] for n in WEIGHTS])
```

```python
import functools

import jax
import jax.numpy as jnp
from jax import lax
from jax.experimental import pallas as pl
from jax.experimental.pallas import tpu as pltpu

F32 = jnp.float32
BF16 = jnp.bfloat16
HI = lax.Precision.HIGHEST
MESH = pl.DeviceIdType.MESH

D_MODEL = 1024
N_META = 16
EPS = 1e-6
FOX_HEADS, FOX_HD = 8, 64
FOX_W = FOX_HEADS * FOX_HD
GDN_HEADS, GDN_HD, GDN_CHUNK, GDN_CONV = 8, 128, 64, 4
GDN_W = GDN_HEADS * GDN_HD
D_FF = 2816
FFN_CONV = 3
D_IN = 7704
D_IN_PAD = 8192
IN_SEGS = ((0, 1536, 0), (1536, 8, 1536), (1544, 3072, 1664), (4616, 16, 4736), (4632, 1024, 4864), (5656, 2048, 5888))
ROW_ALIGN = 256
ATT_BLK = 256
VMEM_LIMIT = 48 * 1024 * 1024
LANE = 128

ADAM_LR, ADAM_B1, ADAM_B2, ADAM_EPS, ADAM_WD, ADAM_STEP = 0.001, 0.9, 0.999, 1e-08, 0.01, 10


def _call(body, **kw):
    return pl.pallas_call(body, **kw)


def _tile(n, target, mult):
    best, t = None, mult
    while t <= min(n, target):
        if n % t == 0:
            best = t
        t += mult
    assert best is not None, (n, target, mult)
    return best


def _cparams(sem):
    return pltpu.CompilerParams(dimension_semantics=sem, vmem_limit_bytes=VMEM_LIMIT)


def _raw_dot(a, b, ca, cb, precise):
    dims = (((ca,), (cb,)), ((), ()))
    a_hi, b_hi = a.astype(BF16), b.astype(BF16)
    out = lax.dot_general(a_hi, b_hi, dims, preferred_element_type=F32)
    if precise:
        a_lo = (a - a_hi.astype(F32)).astype(BF16)
        b_lo = (b - b_hi.astype(F32)).astype(BF16)
        out = out + (lax.dot_general(a_hi, b_lo, dims, preferred_element_type=F32)
                     + lax.dot_general(a_lo, b_hi, dims, preferred_element_type=F32))
    return out


def _make_dot(ca, cb, precise):
    @jax.custom_vjp
    def f(a, b):
        return _raw_dot(a, b, ca, cb, precise)

    def fwd(a, b):
        return f(a, b), (a, b)

    def bwd(res, ct):
        a, b = res
        if ca == 1:
            da = _raw_dot(ct, b, 1, 1 if cb == 0 else 0, precise)
        else:
            da = _raw_dot(b, ct, 1 if cb == 0 else 0, 1, precise)
        if cb == 0:
            db = _raw_dot(a, ct, 0 if ca == 1 else 1, 0, precise)
        else:
            db = _raw_dot(ct, a, 0, 0 if ca == 1 else 1, precise)
        return da, db

    f.defvjp(fwd, bwd)
    return f


_DOTS = {(ca, cb, p): _make_dot(ca, cb, p) for ca in (0, 1) for cb in (0, 1) for p in (False, True)}


def _dot(a, b, ca=1, cb=0, precise=False):
    return _DOTS[(ca, cb, precise)](a, b)


def _mm_call(a, b, name, ta=False, tb=False, out_dtype=F32):
    k, m = a.shape if ta else a.shape[::-1]
    n, kb = b.shape if tb else b.shape[::-1]
    assert k == kb, (a.shape, b.shape)
    tm = _tile(m, 1408, LANE) if ta else _tile(m, 768, 16)
    tn = _tile(n, 1408, LANE)
    tk = _tile(k, 1408, LANE)
    nk = k // tk
    dims = (((0 if ta else 1,), (1 if tb else 0,)), ((), ()))

    def body(a_ref, b_ref, o_ref, *scratch):
        part = lax.dot_general(a_ref[...], b_ref[...], dims, preferred_element_type=F32)
        if nk == 1:
            o_ref[...] = part.astype(o_ref.dtype)
            return
        acc_ref = scratch[0]
        kk = pl.program_id(2)

        @pl.when(kk == 0)
        def _():
            acc_ref[...] = part

        @pl.when(kk > 0)
        def _():
            acc_ref[...] += part

        @pl.when(kk == nk - 1)
        def _():
            o_ref[...] = acc_ref[...].astype(o_ref.dtype)

    return _call(
        body, name=name, grid=(m // tm, n // tn, nk),
        in_specs=[pl.BlockSpec((tk, tm), lambda i, j, kk: (kk, i)) if ta else
                  pl.BlockSpec((tm, tk), lambda i, j, kk: (i, kk)),
                  pl.BlockSpec((tn, tk), lambda i, j, kk: (j, kk)) if tb else
                  pl.BlockSpec((tk, tn), lambda i, j, kk: (kk, j))],
        out_specs=pl.BlockSpec((tm, tn), lambda i, j, kk: (i, j)),
        out_shape=jax.ShapeDtypeStruct((m, n), out_dtype),
        scratch_shapes=[pltpu.VMEM((tm, tn), F32)] if nk > 1 else [],
        compiler_params=_cparams(("parallel", "parallel", "arbitrary")),
    )(a, b)


def _make_mm(out_dtype):
    @jax.custom_vjp
    def op(a, w):
        return _mm_call(a.astype(BF16), w.astype(BF16), "mm_fwd", out_dtype=out_dtype)

    def fwd(a, w):
        a_b, w_b = a.astype(BF16), w.astype(BF16)
        return _mm_call(a_b, w_b, "mm_fwd", out_dtype=out_dtype), (a_b, w_b, jnp.zeros((), a.dtype))

    def bwd(res, ct):
        a_b, w_b, like_a = res
        ct_b = ct.astype(BF16)
        return (_mm_call(ct_b, w_b, "mm_dx", tb=True, out_dtype=like_a.dtype),
                _mm_call(a_b, ct_b, "mm_dw", ta=True))

    op.defvjp(fwd, bwd)
    return op


mm = _make_mm(F32)
mm_bf16 = _make_mm(BF16)


def _rows_specs(rows, tm, ncb, bc):
    specs = []
    for idx, r in enumerate(rows):
        if idx in bc:
            specs.append(pl.BlockSpec((tm, r.shape[1]), lambda i, j: (i, 0)))
        else:
            specs.append(pl.BlockSpec((tm, r.shape[1] // ncb), lambda i, j: (i, j)))
    return specs


def _param_specs(params):
    return [pl.BlockSpec(p.shape, lambda i, j: (0, 0)) for p in params]


def _group_slices(refs, groups, g, whole):
    out = []
    for idx, r in enumerate(refs):
        w = r.shape[1] // groups
        out.append((r[...] if idx in whole else r[:, g * w:(g + 1) * w]).astype(F32))
    return out


def _rows_fwd_call(fns, rows, params, outs, tm, ncb, bc, name, out_dtypes=None):
    r_total = rows[0].shape[0]
    nr, groups = len(rows), len(fns)
    out_dtypes = out_dtypes or [F32] * len(outs)

    def body(*refs):
        pvals = [r[...] for r in refs[nr:nr + len(params)]]
        for g, fn in enumerate(fns):
            res = fn(*_group_slices(refs[:nr], groups, g, bc), *pvals)
            for o_ref, val in zip(refs[nr + len(params):], res):
                w = o_ref.shape[1] // groups
                o_ref[:, g * w:(g + 1) * w] = val.astype(o_ref.dtype)

    return _call(
        body, name=name, grid=(r_total // tm, ncb),
        in_specs=_rows_specs(rows, tm, ncb, bc) + _param_specs(params),
        out_specs=[pl.BlockSpec((tm, w * groups), lambda i, j: (i, j)) for w in outs],
        out_shape=[jax.ShapeDtypeStruct((r_total, w * groups * ncb), dt) for w, dt in zip(outs, out_dtypes)],
        compiler_params=_cparams(("parallel", "parallel")),
    )(*rows, *params)


def _rows_bwd_call(fns, rows, params, cts, tm, ncb, bc, name):
    r_total = rows[0].shape[0]
    nr, npar, nct, groups = len(rows), len(params), len(cts), len(fns)

    def body(*refs):
        i, j = pl.program_id(0), pl.program_id(1)
        pvals = [r[...] for r in refs[nr:nr + npar]]
        ct_refs = refs[nr + npar:nr + npar + nct]
        d_refs = refs[nr + npar + nct:]
        shared = {idx: None for idx in list(bc) + list(range(nr, nr + npar))}
        for g, fn in enumerate(fns):
            _, vjp = jax.vjp(lambda *a, fn=fn: tuple(fn(*a)), *_group_slices(refs[:nr], groups, g, bc), *pvals)
            grads = vjp(tuple(_group_slices(ct_refs, groups, g, ())))
            for idx in range(nr + npar):
                if idx in shared:
                    shared[idx] = grads[idx] if shared[idx] is None else shared[idx] + grads[idx]
                else:
                    w = d_refs[idx].shape[1] // groups
                    d_refs[idx][:, g * w:(g + 1) * w] = grads[idx].astype(d_refs[idx].dtype)
        for idx, total in shared.items():
            first = (j == 0) if idx < nr else ((i == 0) & (j == 0))

            @pl.when(first)
            def _(idx=idx):
                d_refs[idx][...] = jnp.zeros_like(d_refs[idx])
            d_refs[idx][...] += total

    ct_specs = [pl.BlockSpec((tm, c.shape[1] // ncb), lambda i, j: (i, j)) for c in cts]
    return _call(
        body, name=name + "_bwd", grid=(r_total // tm, ncb),
        in_specs=_rows_specs(rows, tm, ncb, bc) + _param_specs(params) + ct_specs,
        out_specs=_rows_specs(rows, tm, ncb, bc) + _param_specs(params),
        out_shape=[jax.ShapeDtypeStruct(a.shape, a.dtype) for a in list(rows) + list(params)],
        compiler_params=_cparams(("arbitrary", "arbitrary")),
    )(*rows, *params, *cts)


def rowop(fn, name, outs, tm, ncb=1, bc=(), out_dtypes=None):
    fns = list(fn) if isinstance(fn, (list, tuple)) else [fn]

    @jax.custom_vjp
    def op(rows, params):
        return tuple(_rows_fwd_call(fns, rows, params, outs, tm, ncb, bc, name, out_dtypes))

    def fwd(rows, params):
        return op(rows, params), (rows, params)

    def bwd(res, cts):
        rows, params = res
        d = _rows_bwd_call(fns, rows, params, cts, tm, ncb, bc, name)
        return tuple(d[:len(rows)]), tuple(d[len(rows):])

    op.defvjp(fwd, bwd)
    return op


def _sigmoid(x):
    return 1.0 / (1.0 + jnp.exp(-x))


def _silu(x):
    return x * _sigmoid(x)


def _softplus(x):
    return jnp.maximum(x, 0.0) + jnp.log(1.0 + jnp.exp(-jnp.abs(x)))


def _f_rmsnorm(x, g):
    return (x * lax.rsqrt(jnp.mean(x * x, axis=-1, keepdims=True) + EPS) * g,)


def _f_qnorm(x, g):
    return (x * lax.rsqrt(jnp.mean(x * x, axis=-1, keepdims=True) + EPS) * (g * (FOX_HD ** -0.5)),)


def _f_logsig(x, b):
    return (-_softplus(-(x + b)),)


def _f_gdn_q(x):
    y = _silu(x)
    return (y * lax.rsqrt(jnp.sum(y * y, axis=-1, keepdims=True) + EPS) * (GDN_HD ** -0.5),)


def _f_gdn_k(x):
    y = _silu(x)
    return (y * lax.rsqrt(jnp.sum(y * y, axis=-1, keepdims=True) + EPS),)


def _f_gdn_v(x):
    return (_silu(x),)


def _f_gdn_gates(bl, al, a_log, dt_bias):
    return _sigmoid(bl), -jnp.exp(a_log) * _softplus(al + dt_bias)


def _f_gdn_out(o, z, g):
    return (o * lax.rsqrt(jnp.mean(o * o, axis=-1, keepdims=True) + EPS) * g * _silu(z),)


def _f_merge(g0, g1, ya, yb):
    return (_sigmoid(g0) * ya + _sigmoid(g1) * yb,)


def _f_residual(a, b, keep):
    return ((a + b) * keep,)


def _f_residual_norm(a, b, keep, g):
    r = (a + b) * keep
    return r, _f_rmsnorm(r, g)[0]


def _f_glu(a, b):
    return (_silu(a) * b,)


def _glu_call(up, ct):
    t_total, two_f = up.shape
    f = two_f // 2
    tm = _tile(t_total, 128, 16)
    wc = _tile(f, 1408, LANE)

    def body(*refs):
        up_ref, out_ref = refs[0], refs[-1]
        for c0 in range(0, f, wc):
            a, b = up_ref[:, c0:c0 + wc].astype(F32), up_ref[:, f + c0:f + c0 + wc].astype(F32)
            if ct is None:
                out_ref[:, c0:c0 + wc] = _f_glu(a, b)[0].astype(out_ref.dtype)
            else:
                _, vjp = jax.vjp(_f_glu, a, b)
                da, db = vjp((refs[1][:, c0:c0 + wc].astype(F32),))
                out_ref[:, c0:c0 + wc] = da.astype(out_ref.dtype)
                out_ref[:, f + c0:f + c0 + wc] = db.astype(out_ref.dtype)

    wide = pl.BlockSpec((tm, two_f), lambda i: (i, 0))
    narrow = pl.BlockSpec((tm, f), lambda i: (i, 0))
    return _call(
        body, name="ffn_glu" if ct is None else "ffn_glu_bwd", grid=(t_total // tm,),
        in_specs=[wide] if ct is None else [wide, narrow], out_specs=narrow if ct is None else wide,
        out_shape=jax.ShapeDtypeStruct((t_total, f), BF16) if ct is None else jax.ShapeDtypeStruct(up.shape, up.dtype),
        compiler_params=_cparams(("parallel",)),
    )(*((up,) if ct is None else (up, ct)))


@jax.custom_vjp
def glu(up):
    return _glu_call(up, None)


glu.defvjp(lambda up: (glu(up), up), lambda up, ct: (_glu_call(up, ct),))


def _shift_down(x, halo, s, row8):
    rx = pltpu.roll(x, s, 0)
    top = jnp.where(row8 < s, pltpu.roll(halo, s, 0), rx[:8])
    return jnp.concatenate([top, rx[8:]], axis=0)


def _shift_up(x, nxt, s, row8):
    tm = x.shape[0]
    rx = pltpu.roll(x, tm - s, 0)
    bot = jnp.where(row8 >= 8 - s, pltpu.roll(nxt, 8 - s, 0), rx[tm - 8:])
    return jnp.concatenate([rx[:tm - 8], bot], axis=0)


def _conv_tiles(r_total, c_total):
    return _tile(r_total, 768, 8), _tile(c_total, 1408, LANE)


def _halo_rows(dtype):
    return 16 if dtype == BF16 else 8


def _conv_fwd_call(x, w8, k_taps, out_dtype):
    r_total, c_total = x.shape
    tm, tc = _conv_tiles(r_total, c_total)
    hr = _halo_rows(x.dtype)
    hb = tm // hr

    def body(x_ref, halo_ref, w_ref, y_ref):
        i = pl.program_id(1)
        xt = x_ref[...].astype(F32)
        halo = jnp.where(i > 0, halo_ref[...].astype(F32)[hr - 8:hr], 0.0)
        row8 = lax.broadcasted_iota(jnp.int32, (8, tc), 0)
        acc = w_ref[k_taps - 1:k_taps, :] * xt
        for k in range(k_taps - 1):
            acc += w_ref[k:k + 1, :] * _shift_down(xt, halo, k_taps - 1 - k, row8)
        y_ref[...] = acc.astype(y_ref.dtype)

    return _call(
        body, name="dwconv_fwd", grid=(c_total // tc, r_total // tm),
        in_specs=[pl.BlockSpec((tm, tc), lambda c, i: (i, c)),
                  pl.BlockSpec((hr, tc), lambda c, i: (jnp.maximum(i * hb - 1, 0), c)),
                  pl.BlockSpec((8, tc), lambda c, i: (0, c))],
        out_specs=pl.BlockSpec((tm, tc), lambda c, i: (i, c)),
        out_shape=jax.ShapeDtypeStruct(x.shape, out_dtype),
        compiler_params=_cparams(("parallel", "parallel")),
    )(x, x, w8)


def _conv_bwd_call(x, w8, dy, k_taps):
    r_total, c_total = x.shape
    tm, tc = _conv_tiles(r_total, c_total)
    hr = _halo_rows(dy.dtype)
    hb = tm // hr
    n_i = r_total // tm

    def body(x_ref, w_ref, dy_ref, nxt_ref, dx_ref, dw_ref):
        i = pl.program_id(1)
        xt, dyt = x_ref[...].astype(F32), dy_ref[...].astype(F32)
        nxt = jnp.where(i < n_i - 1, nxt_ref[...].astype(F32)[0:8], 0.0)
        row8 = lax.broadcasted_iota(jnp.int32, (8, tc), 0)
        dx = w_ref[k_taps - 1:k_taps, :] * dyt
        upd = jnp.where(row8 == k_taps - 1, jnp.sum(dyt * xt, axis=0, keepdims=True), 0.0)
        for k in range(k_taps - 1):
            dy_ahead = _shift_up(dyt, nxt, k_taps - 1 - k, row8)
            dx += w_ref[k:k + 1, :] * dy_ahead
            upd = jnp.where(row8 == k, jnp.sum(dy_ahead * xt, axis=0, keepdims=True), upd)
        dx_ref[...] = dx.astype(dx_ref.dtype)

        @pl.when(i == 0)
        def _():
            dw_ref[...] = jnp.zeros_like(dw_ref)

        dw_ref[...] += upd

    return _call(
        body, name="dwconv_bwd", grid=(c_total // tc, n_i),
        in_specs=[pl.BlockSpec((tm, tc), lambda c, i: (i, c)),
                  pl.BlockSpec((8, tc), lambda c, i: (0, c)),
                  pl.BlockSpec((tm, tc), lambda c, i: (i, c)),
                  pl.BlockSpec((hr, tc), lambda c, i: (jnp.minimum((i + 1) * hb, r_total // hr - 1), c))],
        out_specs=[pl.BlockSpec((tm, tc), lambda c, i: (i, c)), pl.BlockSpec((8, tc), lambda c, i: (0, c))],
        out_shape=[jax.ShapeDtypeStruct(x.shape, x.dtype), jax.ShapeDtypeStruct(w8.shape, F32)],
        compiler_params=_cparams(("parallel", "arbitrary")),
    )(x, w8, dy, dy)


def make_dwconv(k_taps, out_dtype=None):
    @jax.custom_vjp
    def op(x, w8):
        return _conv_fwd_call(x, w8, k_taps, out_dtype or x.dtype)

    def fwd(x, w8):
        return op(x, w8), (x, w8)

    def bwd(res, dy):
        x, w8 = res
        dx, dw = _conv_bwd_call(x, w8, dy, k_taps)
        return dx, dw

    op.defvjp(fwd, bwd)
    return op


def _cumsum_call(x, reverse):
    h, t_total = x.shape
    tb = _tile(t_total, 256, LANE)
    nb = t_total // tb

    def body(x_ref, o_ref, carry_ref):
        i = pl.program_id(0)

        @pl.when(i == 0)
        def _():
            carry_ref[...] = jnp.zeros_like(carry_ref)

        r = lax.broadcasted_iota(jnp.int32, (tb, tb), 0)
        c = lax.broadcasted_iota(jnp.int32, (tb, tb), 1)
        tri = jnp.where((r >= c) if reverse else (r <= c), 1.0, 0.0).astype(F32)
        xv = x_ref[...]
        carry = jnp.max(carry_ref[...], axis=1, keepdims=True)
        o_ref[...] = _raw_dot(xv, tri, 1, 0, True) + carry
        carry_ref[...] = jnp.broadcast_to(carry + jnp.sum(xv, axis=1, keepdims=True), carry_ref.shape)

    imap = (lambda i: (0, nb - 1 - i)) if reverse else (lambda i: (0, i))
    return _call(
        body, name="cumsum_rev" if reverse else "cumsum", grid=(nb,),
        in_specs=[pl.BlockSpec((h, tb), imap)], out_specs=pl.BlockSpec((h, tb), imap),
        out_shape=jax.ShapeDtypeStruct(x.shape, F32), scratch_shapes=[pltpu.VMEM((h, LANE), F32)],
        compiler_params=_cparams(("arbitrary",)),
    )(x)


@jax.custom_vjp
def cumsum_lanes(x):
    return _cumsum_call(x, False)


cumsum_lanes.defvjp(lambda x: (cumsum_lanes(x), None), lambda _, ct: (_cumsum_call(ct, True),))


NEG_BIG = -1e30


def _attn_sub_tiles(nb):
    return max(s for s in (3, 2, 1) if nb % s == 0)


EXP_ZERO = -92.0
SMEM = pl.BlockSpec(memory_space=pltpu.SMEM)


def _max_row_norm_sq(x):
    h_total, t_total, hd = x.shape
    tb = _tile(t_total, 2816, 8)

    def body(x_ref, o_ref):
        @pl.when(pl.program_id(1) == 0)
        def _():
            o_ref[...] = jnp.zeros_like(o_ref)

        xv = x_ref[0]
        top = jnp.max(jnp.sum(xv * xv, axis=1, keepdims=True), axis=0, keepdims=True)
        o_ref[0] = jnp.maximum(o_ref[0], top)

    return _call(
        body, name="max_row_norm", grid=(h_total, t_total // tb),
        in_specs=[pl.BlockSpec((1, tb, hd), lambda h, i: (h, i, 0))],
        out_specs=pl.BlockSpec((1, 8, LANE), lambda h, i: (h, 0, 0)),
        out_shape=jax.ShapeDtypeStruct((h_total, 8, LANE), F32),
        compiler_params=_cparams(("parallel", "arbitrary")),
    )(x)


def _attn_skip_tables(q, k, f_row):
    bound = 2.0 * jnp.sqrt(_max_row_norm_sq(q)[:, 0, :1] * _max_row_norm_sq(k)[:, 0, :1])
    return EXP_ZERO - bound, f_row[:, :, 0, 0], f_row[:, :, 0, -1]


def _attn_fwd_call(q, k, v, f_col, f_row, tables):
    h_total, t_total, hd = q.shape
    blk = f_row.shape[-1]
    nb = t_total // blk
    nsub = _attn_sub_tiles(nb)
    tq = nsub * blk

    def body(thr_ref, first_ref, last_ref, q_ref, k_ref, vt_ref, fc_ref, fr_ref, o_ref, lse_ref):
        h = pl.program_id(0)
        i = pl.program_id(1)
        gap_needed = thr_ref[h, 0]
        f_tile = first_ref[h, i * nsub]
        j_start = lax.while_loop(lambda j: (j < i * nsub) & (f_tile - last_ref[h, j] < gap_needed),
                                 lambda j: j + 1, 0)
        r = lax.broadcasted_iota(jnp.int32, (blk, blk), 0)
        c = lax.broadcasted_iota(jnp.int32, (blk, blk), 1)
        qs = [q_ref[0, s * blk:(s + 1) * blk, :].astype(BF16) for s in range(nsub)]
        fqs = [fr_ref[0, i * nsub + s] for s in range(nsub)]

        def load_kv(j):
            off = pl.multiple_of(j * blk, blk)
            return k_ref[0, pl.ds(off, blk), :], vt_ref[0, j], fc_ref[0, pl.ds(off, blk), :]

        def tile(kv, s, carry, diagonal):
            kj, vtj, fk = kv
            m, l, acc = carry
            st = _raw_dot(kj, qs[s], 1, 1, False) + fqs[s] - fk
            if diagonal:
                st = jnp.where(r <= c, st, NEG_BIG)
            m_new = jnp.maximum(m, jnp.max(st, axis=0, keepdims=True))
            p = jnp.exp(st - m_new)
            alpha = jnp.exp(m - m_new)
            l = alpha * l + jnp.sum(p, axis=0, keepdims=True)
            acc = alpha * acc + _raw_dot(vtj, p, 1, 0, False)
            return m_new, l, acc

        def below_diagonal(j, carry):
            kv = load_kv(j)
            return tuple(tile(kv, s, carry[s], False) for s in range(nsub))

        init = tuple((jnp.full((1, blk), NEG_BIG, F32), jnp.zeros((1, blk), F32), jnp.zeros((hd, blk), F32))
                     for _ in range(nsub))
        carry = list(lax.fori_loop(j_start, i * nsub, below_diagonal, init))
        for d in range(nsub):
            kv = load_kv(i * nsub + d)
            for s in range(d, nsub):
                carry[s] = tile(kv, s, carry[s], s == d)
        for s, (m, l, acc) in enumerate(carry):
            o_ref[0, :, s * blk:(s + 1) * blk] = acc / l
            lse_ref[0, s] = m + jnp.log(l)

    vt = v.reshape(h_total, nb, blk, hd).transpose(0, 1, 3, 2).astype(BF16)
    return _call(
        body, name="fox_fwd", grid=(h_total, nb // nsub),
        in_specs=[SMEM, SMEM, SMEM,
                  pl.BlockSpec((1, tq, hd), lambda h, i: (h, i, 0)),
                  pl.BlockSpec((1, t_total, hd), lambda h, i: (h, 0, 0)),
                  pl.BlockSpec((1, nb, hd, blk), lambda h, i: (h, 0, 0, 0)),
                  pl.BlockSpec((1, t_total, 1), lambda h, i: (h, 0, 0)),
                  pl.BlockSpec((1, nb, 1, blk), lambda h, i: (h, 0, 0, 0))],
        out_specs=[pl.BlockSpec((1, hd, tq), lambda h, i: (h, 0, i)),
                   pl.BlockSpec((1, nsub, 1, blk), lambda h, i: (h, i, 0, 0))],
        out_shape=[jax.ShapeDtypeStruct((h_total, hd, t_total), F32), jax.ShapeDtypeStruct(f_row.shape, F32)],
        compiler_params=_cparams(("parallel", "parallel")),
    )(*tables, q, k.astype(BF16), vt, f_col, f_row)


def _attn_bwd_call(q, k, v, f_col, f_row, tables, lse_row, delta_row, do_blk):
    h_total, t_total, hd = q.shape
    blk = f_row.shape[-1]
    nb = t_total // blk
    nsub = _attn_sub_tiles(nb)
    tkv = nsub * blk

    def body(thr_ref, first_ref, last_ref, q_ref, do_ref, k_ref, v_ref, fc_ref, fr_ref, lse_ref, dl_ref,
             dq_ref, dk_ref, dv_ref, dfk_ref, dfq_ref):
        h = pl.program_id(0)
        j = pl.program_id(1)
        gap_needed = thr_ref[h, 0]
        f_tile = last_ref[h, j * nsub + nsub - 1]
        i_stop = lax.while_loop(lambda i: (i < nb) & (first_ref[h, jnp.minimum(i, nb - 1)] - f_tile >= gap_needed),
                                lambda i: i + 1, (j + 1) * nsub)

        @pl.when(j == 0)
        def _():
            dq_ref[...] = jnp.zeros_like(dq_ref)
            dfq_ref[...] = jnp.zeros_like(dfq_ref)

        ks = [k_ref[0, s * blk:(s + 1) * blk, :].astype(BF16) for s in range(nsub)]
        vs = [v_ref[0, s * blk:(s + 1) * blk, :].astype(BF16) for s in range(nsub)]
        fks = [fc_ref[0, s * blk:(s + 1) * blk, :] for s in range(nsub)]
        r = lax.broadcasted_iota(jnp.int32, (blk, blk), 0)
        c = lax.broadcasted_iota(jnp.int32, (blk, blk), 1)

        def q_step(i, accs, subs):
            off = pl.multiple_of(i * blk, blk)
            qi = q_ref[0, pl.ds(off, blk), :]
            doi = do_ref[0, i]
            fq, lse, dl = fr_ref[0, i], lse_ref[0, i], dl_ref[0, i]
            accs = list(accs)
            dq_i, dfq_i = None, None
            for s, diagonal in subs:
                dk, dv, dfk = accs[s]
                st = _raw_dot(ks[s], qi, 1, 1, False) + fq - fks[s] - lse
                if diagonal:
                    st = jnp.where(r <= c, st, NEG_BIG)
                pt = jnp.exp(st)
                dv = dv + _raw_dot(pt, doi, 1, 1, False)
                dst = pt * (_raw_dot(vs[s], doi, 1, 0, False) - dl)
                dk = dk + _raw_dot(dst, qi, 1, 0, False)
                dfk = dfk - jnp.sum(dst, axis=1, keepdims=True)
                accs[s] = (dk, dv, dfk)
                dq_s = _raw_dot(dst, ks[s], 0, 0, False)
                dfq_s = jnp.sum(dst, axis=0, keepdims=True)
                dq_i = dq_s if dq_i is None else dq_i + dq_s
                dfq_i = dfq_s if dfq_i is None else dfq_i + dfq_s
            dfq_ref[0, i] += dfq_i
            dq_ref[0, pl.ds(off, blk), :] += dq_i
            return tuple(accs)

        accs = tuple((jnp.zeros((blk, hd), F32), jnp.zeros((blk, hd), F32), jnp.zeros((blk, 1), F32))
                     for _ in range(nsub))
        for d in range(nsub):
            accs = q_step(j * nsub + d, accs, [(s, s == d) for s in range(d + 1)])
        accs = lax.fori_loop((j + 1) * nsub, i_stop,
                             lambda i, a: q_step(i, a, [(s, False) for s in range(nsub)]), accs)
        for s, (dk, dv, dfk) in enumerate(accs):
            dk_ref[0, s * blk:(s + 1) * blk, :] = dk
            dv_ref[0, s * blk:(s + 1) * blk, :] = dv
            dfk_ref[0, s * blk:(s + 1) * blk, :] = dfk

    full = pl.BlockSpec((1, t_total, hd), lambda h, j: (h, 0, 0))
    tile = pl.BlockSpec((1, tkv, hd), lambda h, j: (h, j, 0))
    col = pl.BlockSpec((1, tkv, 1), lambda h, j: (h, j, 0))
    rows = pl.BlockSpec((1, nb, 1, blk), lambda h, j: (h, 0, 0, 0))
    do_blocks = pl.BlockSpec((1, nb, hd, blk), lambda h, j: (h, 0, 0, 0))
    return _call(
        body, name="fox_bwd", grid=(h_total, nb // nsub),
        in_specs=[SMEM, SMEM, SMEM, full, do_blocks, tile, tile, col, rows, rows, rows],
        out_specs=[full, tile, tile, col, rows],
        out_shape=[jax.ShapeDtypeStruct(q.shape, F32), jax.ShapeDtypeStruct(q.shape, F32),
                   jax.ShapeDtypeStruct(q.shape, F32), jax.ShapeDtypeStruct(f_col.shape, F32),
                   jax.ShapeDtypeStruct(f_row.shape, F32)],
        compiler_params=_cparams(("parallel", "arbitrary")),
    )(*tables, q.astype(BF16), do_blk, k, v, f_col, f_row, lse_row, delta_row)


def _attn_delta_call(do_t, o_t):
    h_total, hd, t_total = o_t.shape
    tb = _tile(t_total, 2816, LANE)

    def body(do_ref, o_ref, d_ref):
        d_ref[0] = jnp.sum(do_ref[0] * o_ref[0], axis=0, keepdims=True)

    spec = pl.BlockSpec((1, hd, tb), lambda h, i: (h, 0, i))
    return _call(
        body, name="fox_delta", grid=(h_total, t_total // tb), in_specs=[spec, spec],
        out_specs=pl.BlockSpec((1, 1, tb), lambda h, i: (h, 0, i)),
        out_shape=jax.ShapeDtypeStruct((h_total, 1, t_total), F32),
        compiler_params=_cparams(("parallel", "parallel")),
    )(do_t, o_t)


@jax.custom_vjp
def fox_attention(q, k, v, f_col, f_row):
    return _attn_fwd_call(q, k, v, f_col, f_row, _attn_skip_tables(q, k, f_row))[0]


def _fox_fwd(q, k, v, f_col, f_row):
    tables = _attn_skip_tables(q, k, f_row)
    o_t, lse_row = _attn_fwd_call(q, k, v, f_col, f_row, tables)
    return o_t, (q, k, v, f_col, f_row, tables, o_t, lse_row)


def _fox_bwd(res, do_t):
    q, k, v, f_col, f_row, tables, o_t, lse_row = res
    h_total, t_total, hd = q.shape
    nb, blk = f_row.shape[1], f_row.shape[3]
    delta = _attn_delta_call(do_t, o_t).reshape(f_row.shape)
    do_blk = do_t.reshape(h_total, hd, nb, blk).transpose(0, 2, 1, 3).astype(BF16)
    grads = _attn_bwd_call(q, k, v, f_col, f_row, tables, lse_row, delta, do_blk)
    return tuple(g.astype(p.dtype) for g, p in zip(grads, (q, k, v, f_col, f_row)))


fox_attention.defvjp(_fox_fwd, _fox_bwd)


def _head_col(blk, h):
    lane = lax.broadcasted_iota(jnp.int32, blk.shape, 1)
    return jnp.sum(jnp.where(lane == h, blk, 0.0), axis=1, keepdims=True)


@jax.custom_vjp
def _cat2(a, b):
    return jnp.concatenate([a, b], axis=1)


_cat2.defvjp(lambda a, b: (_cat2(a, b), a.shape[1]), lambda na, ct: (ct[:, :na], ct[:, na:]))


@jax.custom_vjp
def _split2(x):
    half = x.shape[1] // 2
    return x[:, :half], x[:, half:]


_split2.defvjp(lambda x: (_split2(x), None), lambda _, cts: (jnp.concatenate(cts, axis=1),))


def _neumann_solve(m, b):
    x = b - _raw_dot(m, b, 1, 0, False)
    powers, steps = [m], 1
    while 2 * steps < GDN_CHUNK:
        powers.append(_raw_dot(powers[-1], powers[-1], 1, 0, False))
        x = x + _raw_dot(powers[-1], x, 1, 0, False)
        steps *= 2
    return x, powers


@jax.custom_vjp
def _unit_lower_solve(m, b):
    return _neumann_solve(m, b)[0]


def _unit_lower_solve_fwd(m, b):
    x, powers = _neumann_solve(m, b)
    return x, (powers, x)


def _unit_lower_solve_bwd(res, dx):
    powers, x = res
    db = dx - _raw_dot(powers[0], dx, 0, 0, False)
    for p in powers[1:]:
        db = db + _raw_dot(p, db, 0, 0, False)
    return -_raw_dot(db, x, 1, 1, False), db


_unit_lower_solve.defvjp(_unit_lower_solve_fwd, _unit_lower_solve_bwd)


@jax.custom_vjp
def _unit_lower_solve_known(m, b, x):
    return x


def _unit_lower_solve_known_bwd(res, dx):
    m, x = res
    powers, steps = [m], 1
    while 2 * steps < GDN_CHUNK:
        powers.append(_raw_dot(powers[-1], powers[-1], 1, 0, False))
        steps *= 2
    dm, db = _unit_lower_solve_bwd((powers, x), dx)
    return dm, db, jnp.zeros_like(x)


_unit_lower_solve_known.defvjp(lambda m, b, x: (x, (m, x)), _unit_lower_solve_known_bwd)


def _gdn_intra(h, q, k, v, b_blk, g_blk, uw_known=None):
    n = q.shape[0]
    b, g = _head_col(b_blk, h), _head_col(g_blk, h)
    r = lax.broadcasted_iota(jnp.int32, (n, n), 0)
    c = lax.broadcasted_iota(jnp.int32, (n, n), 1)
    same = (r // GDN_CHUNK) == (c // GDN_CHUNK)
    incl = same & (r >= c)
    g_row = jnp.sum(jnp.where(r == c, g, 0.0), axis=0, keepdims=True)
    big_g = jnp.sum(jnp.where(incl, g_row, 0.0), axis=1, keepdims=True)
    big_g_row = jnp.sum(jnp.where(same & (r <= c), g, 0.0), axis=0, keepdims=True)
    g_tot = jnp.sum(jnp.where(same, g_row, 0.0), axis=1, keepdims=True)
    dec = jnp.where(incl, jnp.exp(jnp.where(incl, big_g - big_g_row, 0.0)), 0.0)
    dec_strict = jnp.where(r > c, dec, 0.0)
    e_g = jnp.exp(big_g)
    kb = k * b
    m = _dot(kb, k, 1, 1) * dec_strict
    rs = lax.broadcasted_iota(jnp.int32, (n, GDN_CHUNK), 0)
    cs = lax.broadcasted_iota(jnp.int32, (n, GDN_CHUNK), 1)
    fold = jnp.where(rs % GDN_CHUNK == cs, 1.0, 0.0).astype(F32)
    aqk = _dot(_dot(q, k, 1, 1) * dec, fold, 1, 0, True)
    rhs = _cat2(v * b, kb * e_g)
    u, w = _split2(_unit_lower_solve(m, rhs) if uw_known is None else
                   _unit_lower_solve_known(m, rhs, jnp.concatenate(uw_known, axis=1)))
    lane = lax.broadcasted_iota(jnp.int32, b_blk.shape, 1)
    return u, w, q * e_g, k * jnp.exp(g_tot - big_g), aqk, jnp.where(lane == h, g_tot, 0.0)


def _gdn_rec(h, s, u, w, qg, kd, aqk, gl_blk):
    g_last = jnp.max(_head_col(gl_blk, h), axis=0, keepdims=True)
    big_u = u - _dot(w, s)
    o = _dot(qg, s) + _dot(aqk, big_u)
    s_next = s * jnp.exp(g_last) + _dot(kd, big_u, 0, 0)
    return o, s_next


GDN_TOK_BLK = 256


def _gdn_layout(t_total, rev):
    tb = _tile(t_total, GDN_TOK_BLK, GDN_CHUNK)
    cb, nblk = tb // GDN_CHUNK, t_total // tb
    pos = (lambda i: nblk - 1 - i) if rev else (lambda i: i)
    specs = dict(
        tok=pl.BlockSpec((tb, GDN_W), lambda i: (pos(i), 0)),
        q=pl.BlockSpec((tb, GDN_W), lambda i: (pos(i), 0)),
        k=pl.BlockSpec((tb, GDN_W), lambda i: (pos(i), 1)),
        v=pl.BlockSpec((tb, GDN_W), lambda i: (pos(i), 2)),
        qkv=pl.BlockSpec((tb, 3 * GDN_W), lambda i: (pos(i), 0)),
        gate=pl.BlockSpec((tb, GDN_HEADS), lambda i: (pos(i), 0)),
        aqk=pl.BlockSpec((GDN_HEADS, tb, GDN_CHUNK), lambda i: (0, pos(i), 0)),
        state=pl.BlockSpec((GDN_HEADS, cb, GDN_HD, GDN_HD), lambda i: (0, pos(i), 0, 0)))
    return cb, nblk, specs


def _gdn_shapes(t_total):
    n_chunks = t_total // GDN_CHUNK
    return dict(tok=jax.ShapeDtypeStruct((t_total, GDN_W), BF16),
                out=jax.ShapeDtypeStruct((t_total, GDN_W), F32),
                gate=jax.ShapeDtypeStruct((t_total, GDN_HEADS), F32),
                aqk=jax.ShapeDtypeStruct((GDN_HEADS, t_total, GDN_CHUNK), F32),
                state=jax.ShapeDtypeStruct((GDN_HEADS, n_chunks, GDN_HD, GDN_HD), F32))


def _chunk_rows(ci):
    return pl.ds(pl.multiple_of(ci * GDN_CHUNK, GDN_CHUNK), GDN_CHUNK)


def _head_cols(h):
    return pl.ds(h * GDN_HD, GDN_HD)


def _gdn_intra_fwd_call(qkv, b, g):
    cb, nblk, sp = _gdn_layout(qkv.shape[0], False)
    sh = _gdn_shapes(qkv.shape[0])

    def body(q_ref, k_ref, v_ref, b_ref, g_ref, u_ref, w_ref, qg_ref, kd_ref, aqk_ref, gl_ref):
        b_blk, g_blk = b_ref[...], g_ref[...]
        gl = jnp.zeros(b_blk.shape, F32)
        for h in range(GDN_HEADS):
            cols = _head_cols(h)
            u, w, qg, kd, aqk, gl_h = _gdn_intra(h, q_ref[:, cols], k_ref[:, cols], v_ref[:, cols], b_blk, g_blk)
            u_ref[:, cols] = u.astype(u_ref.dtype)
            w_ref[:, cols] = w.astype(w_ref.dtype)
            qg_ref[:, cols] = qg.astype(qg_ref.dtype)
            kd_ref[:, cols] = kd.astype(kd_ref.dtype)
            aqk_ref[h] = aqk
            gl = gl + gl_h
        gl_ref[...] = gl

    return _call(
        body, name="gdn_intra_fwd", grid=(nblk,),
        in_specs=[sp["q"], sp["k"], sp["v"]] + [sp["gate"]] * 2,
        out_specs=[sp["tok"]] * 4 + [sp["aqk"], sp["gate"]],
        out_shape=[sh["tok"]] * 4 + [sh["aqk"], sh["gate"]],
        compiler_params=_cparams(("parallel",)),
    )(qkv, qkv, qkv, b, g)


def _gdn_intra_bwd_call(qkv, b, g, u, w, du, dw, dqg, dkd, daqk, dgl):
    cb, nblk, sp = _gdn_layout(qkv.shape[0], False)
    sh = _gdn_shapes(qkv.shape[0])

    def body(q_ref, k_ref, v_ref, b_ref, g_ref, u_ref, w_ref, du_ref, dw_ref, dqg_ref, dkd_ref, daqk_ref, dgl_ref,
             dqkv_ref, db_ref, dg_ref):
        b_blk, g_blk, dgl = b_ref[...], g_ref[...], dgl_ref[...]
        db = jnp.zeros(b_blk.shape, F32)
        dg = jnp.zeros(b_blk.shape, F32)
        for h in range(GDN_HEADS):
            cols = _head_cols(h)
            known = (u_ref[:, cols].astype(F32), w_ref[:, cols].astype(F32))
            _, vjp = jax.vjp(functools.partial(_gdn_intra, h, uw_known=known),
                             q_ref[:, cols], k_ref[:, cols], v_ref[:, cols], b_blk, g_blk)
            dq, dk, dv, db_h, dg_h = vjp((*[r[:, cols].astype(F32) for r in (du_ref, dw_ref, dqg_ref, dkd_ref)],
                                          daqk_ref[h], dgl))
            dqkv_ref[:, pl.ds(h * GDN_HD, GDN_HD)] = dq
            dqkv_ref[:, pl.ds(GDN_W + h * GDN_HD, GDN_HD)] = dk
            dqkv_ref[:, pl.ds(2 * GDN_W + h * GDN_HD, GDN_HD)] = dv
            db = db + db_h
            dg = dg + dg_h
        db_ref[...] = db
        dg_ref[...] = dg

    return _call(
        body, name="gdn_intra_bwd", grid=(nblk,),
        in_specs=[sp["q"], sp["k"], sp["v"]] + [sp["gate"]] * 2 + [sp["tok"]] * 6 + [sp["aqk"], sp["gate"]],
        out_specs=[sp["qkv"]] + [sp["gate"]] * 2,
        out_shape=[jax.ShapeDtypeStruct(qkv.shape, F32)] + [sh["gate"]] * 2,
        compiler_params=_cparams(("parallel",)),
    )(qkv, qkv, qkv, b, g, u, w, du, dw, dqg, dkd, daqk, dgl)


def _gdn_rec_fwd_call(u, w, qg, kd, aqk, gl):
    cb, nblk, sp = _gdn_layout(u.shape[0], False)
    sh = _gdn_shapes(u.shape[0])

    def body(u_ref, w_ref, qg_ref, kd_ref, aqk_ref, gl_ref, o_ref, s_all_ref, s_ref):
        @pl.when(pl.program_id(0) == 0)
        def _():
            s_ref[...] = jnp.zeros_like(s_ref)

        def chunk(ci, carry):
            rows = _chunk_rows(ci)
            gl_row = gl_ref[rows, :]
            states = [s_ref[h] for h in range(GDN_HEADS)]
            res = [_gdn_rec(h, states[h], *[r[rows, _head_cols(h)].astype(F32) for r in (u_ref, w_ref, qg_ref, kd_ref)],
                            aqk_ref[h, rows, :], gl_row)
                   for h in range(GDN_HEADS)]
            for h, (o, s_next) in enumerate(res):
                s_all_ref[h, ci] = states[h]
                o_ref[rows, _head_cols(h)] = o
                s_ref[h] = s_next
            return carry

        lax.fori_loop(0, cb, chunk, 0)

    return _call(
        body, name="gdn_rec_fwd", grid=(nblk,),
        in_specs=[sp["tok"]] * 4 + [sp["aqk"], sp["gate"]],
        out_specs=[sp["tok"], sp["state"]], out_shape=[sh["out"], sh["state"]],
        scratch_shapes=[pltpu.VMEM((GDN_HEADS, GDN_HD, GDN_HD), F32)],
        compiler_params=_cparams(("arbitrary",)),
    )(u, w, qg, kd, aqk, gl)


def _gdn_rec_bwd_call(u, w, qg, kd, aqk, gl, s_all, do):
    cb, nblk, sp = _gdn_layout(u.shape[0], True)
    sh = _gdn_shapes(u.shape[0])

    def body(u_ref, w_ref, qg_ref, kd_ref, aqk_ref, gl_ref, s_all_ref, do_ref,
             du_ref, dw_ref, dqg_ref, dkd_ref, daqk_ref, dgl_ref, ds_ref):
        @pl.when(pl.program_id(0) == 0)
        def _():
            ds_ref[...] = jnp.zeros_like(ds_ref)

        def chunk(step, carry):
            ci = cb - 1 - step
            rows = _chunk_rows(ci)
            gl_row = gl_ref[rows, :]
            res = []
            for h in range(GDN_HEADS):
                cols = _head_cols(h)
                _, vjp = jax.vjp(functools.partial(_gdn_rec, h), s_all_ref[h, ci],
                                 *[r[rows, cols].astype(F32) for r in (u_ref, w_ref, qg_ref, kd_ref)],
                                 aqk_ref[h, rows, :], gl_row)
                res.append(vjp((do_ref[rows, cols], ds_ref[h])))
            dgl = jnp.zeros((GDN_CHUNK, GDN_HEADS), F32)
            for h, (ds, du, dw, dqg, dkd, daqk, dgl_h) in enumerate(res):
                cols = _head_cols(h)
                ds_ref[h] = ds
                du_ref[rows, cols] = du.astype(du_ref.dtype)
                dw_ref[rows, cols] = dw.astype(dw_ref.dtype)
                dqg_ref[rows, cols] = dqg.astype(dqg_ref.dtype)
                dkd_ref[rows, cols] = dkd.astype(dkd_ref.dtype)
                daqk_ref[h, rows, :] = daqk
                dgl = dgl + dgl_h
            dgl_ref[rows, :] = dgl
            return carry

        lax.fori_loop(0, cb, chunk, 0)

    return _call(
        body, name="gdn_rec_bwd", grid=(nblk,),
        in_specs=[sp["tok"]] * 4 + [sp["aqk"], sp["gate"], sp["state"], sp["tok"]],
        out_specs=[sp["tok"]] * 4 + [sp["aqk"], sp["gate"]],
        out_shape=[sh["tok"]] * 4 + [sh["aqk"], sh["gate"]],
        scratch_shapes=[pltpu.VMEM((GDN_HEADS, GDN_HD, GDN_HD), F32)],
        compiler_params=_cparams(("arbitrary",)),
    )(u, w, qg, kd, aqk, gl, s_all, do)


@jax.custom_vjp
def gdn_intra(qkv, b, g):
    return tuple(_gdn_intra_fwd_call(qkv, b, g))


def _gdn_intra_fwd(*a):
    outs = gdn_intra(*a)
    return outs, a + (outs[0], outs[1])


gdn_intra.defvjp(_gdn_intra_fwd, lambda res, cts: tuple(_gdn_intra_bwd_call(*res, *cts)))


@jax.custom_vjp
def gdn_rec(u, w, qg, kd, aqk, gl):
    return _gdn_rec_fwd_call(u, w, qg, kd, aqk, gl)[0]


def _gdn_rec_fwd(*a):
    o, s_all = _gdn_rec_fwd_call(*a)
    return o, a + (s_all,)


gdn_rec.defvjp(_gdn_rec_fwd, lambda res, do: tuple(_gdn_rec_bwd_call(*res, do)))


def gated_delta(qkv, b, g):
    return gdn_rec(*gdn_intra(qkv, b, g))


def _loss_call(y, tgt, first, last):
    r_total, d = y.shape
    tm = _tile(r_total, 256, 8)

    def body(y_ref, t_ref, loss_ref, dy_ref):
        i = pl.program_id(0)

        @pl.when(i == 0)
        def _():
            loss_ref[...] = jnp.zeros_like(loss_ref)

        row = lax.broadcasted_iota(jnp.int32, (tm, d), 0) + i * tm
        err = jnp.where((row >= first) & (row < last), y_ref[...] - t_ref[...], 0.0)
        dy_ref[...] = err * (1.0 / d)
        part = jnp.sum(jnp.sum(err * err, axis=1, keepdims=True), axis=0, keepdims=True) * (0.5 / d)
        loss_ref[...] += jnp.broadcast_to(part, loss_ref.shape)

    return _call(
        body, name="loss_head", grid=(r_total // tm,),
        in_specs=[pl.BlockSpec((tm, d), lambda i: (i, 0))] * 2,
        out_specs=[pl.BlockSpec((8, LANE), lambda i: (0, 0)), pl.BlockSpec((tm, d), lambda i: (i, 0))],
        out_shape=[jax.ShapeDtypeStruct((8, LANE), F32), jax.ShapeDtypeStruct(y.shape, F32)],
        compiler_params=_cparams(("arbitrary",)),
    )(y, tgt)


def make_loss(first, last):
    @jax.custom_vjp
    def op(y, tgt):
        return _loss_call(y, tgt, first, last)[0][0, 0]

    def fwd(y, tgt):
        loss, dy = _loss_call(y, tgt, first, last)
        return loss[0, 0], (dy,)

    def bwd(res, ct):
        return res[0] * ct, jnp.zeros_like(res[0])

    op.defvjp(fwd, bwd)
    return op


def _pad_rows8(w):
    return jnp.concatenate([w, jnp.zeros((8 - w.shape[0], w.shape[1]), w.dtype)], axis=0)


def local_loss(wts, x, tgt):
    seq = x.shape[0]
    n_tok = N_META + seq
    t_pad = -(-n_tok // ROW_ALIGN) * ROW_ALIGN
    depth = wts["norm1_g"].shape[0]
    blk = _tile(t_pad, ATT_BLK, LANE)
    nb = t_pad // blk
    tm = _tile(t_pad, 256, 8)

    rms = rowop(_f_rmsnorm, "rmsnorm", (D_MODEL,), tm, out_dtypes=[BF16])
    qnorm = rowop(_f_qnorm, "fox_q_norm", (FOX_HD,), _tile(FOX_HEADS * t_pad, 2048, 8))
    knorm = rowop(_f_rmsnorm, "fox_k_norm", (FOX_HD,), _tile(FOX_HEADS * t_pad, 2048, 8))
    logsig = rowop(_f_logsig, "fox_log_forget", (FOX_HEADS,), tm)
    gdn_act = rowop([_f_gdn_q] * GDN_HEADS + [_f_gdn_k] * GDN_HEADS + [_f_gdn_v] * GDN_HEADS, "gdn_qkv_act",
                    (GDN_HD,), tm)
    gates = rowop(_f_gdn_gates, "gdn_gates", (GDN_HEADS, GDN_HEADS), tm)
    gdn_out = rowop([_f_gdn_out] * GDN_HEADS, "gdn_out_norm", (GDN_HD,), tm, out_dtypes=[BF16])
    merge = rowop(_f_merge, "branch_merge", (D_MODEL,), tm, out_dtypes=[BF16])
    residual = rowop(_f_residual, "residual_add", (D_MODEL,), tm, bc=(2,))
    residual_norm = rowop(_f_residual_norm, "residual_add_norm", (D_MODEL, D_MODEL), tm, bc=(2,),
                          out_dtypes=[F32, BF16])
    keep = (jnp.arange(t_pad)[:, None] < n_tok).astype(F32)
    conv4 = make_dwconv(GDN_CONV, BF16)
    conv3 = make_dwconv(FFN_CONV)
    loss_op = make_loss(N_META, n_tok)

    zeros = jnp.zeros((t_pad - n_tok, D_MODEL), F32)
    h_res = jnp.concatenate([wts["meta_tokens"], x, zeros], axis=0)
    tgt_rows = jnp.concatenate([jnp.zeros((N_META, D_MODEL), F32), tgt, zeros], axis=0)

    def heads(a):
        return a.reshape(t_pad, FOX_HEADS, FOX_HD).transpose(1, 0, 2).reshape(FOX_HEADS * t_pad, FOX_HD)

    h = rms((h_res,), (wts["norm1_g"][0][None],))[0]
    for l in range(depth):
        proj = mm_bf16(h, wts["w_in"][l])
        gate_logits = mm(h, jnp.concatenate([wts["w_in"][l][:, 1536:1536 + LANE],
                                             wts["w_in"][l][:, 4736:4736 + LANE]], axis=1))
        qn = qnorm((heads(proj[:, 0:512]),), (wts["fox_q_norm_g"][l][None],))[0]
        kn = knorm((heads(proj[:, 512:1024]),), (wts["fox_k_norm_g"][l][None],))[0]
        vh = heads(proj[:, 1024:1536])
        log_f = logsig((gate_logits[:, 0:FOX_HEADS],), (wts["fox_f_bias"][l][None],))[0]
        f_cum = cumsum_lanes(log_f.T)
        o_a = fox_attention(qn.reshape(FOX_HEADS, t_pad, FOX_HD), kn.reshape(FOX_HEADS, t_pad, FOX_HD),
                            vh.reshape(FOX_HEADS, t_pad, FOX_HD), f_cum[:, :, None],
                            f_cum.reshape(FOX_HEADS, nb, 1, blk))
        y_a = mm_bf16(o_a.transpose(2, 0, 1).reshape(t_pad, FOX_W).astype(BF16), wts["w_branch_a"][l])
        cv = conv4(proj[:, 1664:4736], _pad_rows8(wts["gdn_conv_w"][l]))
        qkv = gdn_act((cv,), ())[0]
        beta, gdec = gates((gate_logits[:, LANE:LANE + GDN_HEADS], gate_logits[:, LANE + GDN_HEADS:LANE + 2 * GDN_HEADS]),
                           (wts["gdn_a_log"][l][None], wts["gdn_dt_bias"][l][None]))
        o_b = gated_delta(qkv, beta, gdec)
        o_b = gdn_out((o_b, proj[:, 4864:5888]), (wts["gdn_norm_g"][l][None],))[0]
        y_b = mm_bf16(o_b, wts["w_branch_b"][l])
        mixed = merge((proj[:, 5888:6912], proj[:, 6912:7936], y_a, y_b), ())[0]
        h_res, h = residual_norm((h_res, mm(mixed, wts["w_out"][l]), keep), (wts["norm2_g"][l][None],))
        up = conv3(mm_bf16(h, wts["w_up"][l]), _pad_rows8(wts["ffn_conv_w"][l]))
        act = glu(up)
        down = mm(act, wts["w_down"][l])
        if l + 1 < depth:
            h_res, h = residual_norm((h_res, down, keep), (wts["norm1_g"][l + 1][None],))
        else:
            h_res = residual((h_res, down, keep), ())[0]
    return loss_op(h_res, tgt_rows)


def pad_w_in(w):
    parts, pos = [], 0
    for src, width, dst in IN_SEGS:
        if dst > pos:
            parts.append(jnp.zeros(w.shape[:-1] + (dst - pos,), w.dtype))
        parts.append(w[..., src:src + width])
        pos = dst + width
    parts.append(jnp.zeros(w.shape[:-1] + (D_IN_PAD - pos,), w.dtype))
    return jnp.concatenate(parts, axis=-1)


def unpad_w_in(w):
    return jnp.concatenate([w[..., dst:dst + width] for _, width, dst in IN_SEGS], axis=-1)


ANY = pl.BlockSpec(memory_space=pl.ANY)
N_CHIPS = 4
N_DEV = 8
COMM_COLS = 1024
COMM_ROW_ALIGN = 512
COMM_ROW_ALIGN_SMALL = 32


def _place():
    return lax.axis_index("x"), lax.axis_index("y"), lax.axis_index("c")


def _other_chips(x, y):
    return [(1 - x, y), (x, 1 - y), (1 - x, 1 - y)]


def _remote(src, dst, send_sem, recv_sem, dev):
    return pltpu.make_async_remote_copy(src_ref=src, dst_ref=dst, send_sem=send_sem, recv_sem=recv_sem,
                                        device_id=dev, device_id_type=MESH)


def chip_all_gather(buf):
    rows, cols = buf.shape
    half = rows // 2

    def body(x_ref, out_ref, send_sems, recv_sems, pass_send, pass_recv):
        x, y, c = _place()
        me = 2 * x + y
        mine, other = pl.ds(c * half, half), pl.ds((1 - c) * half, half)
        sibling = (x, y, 1 - c)
        chips = _other_chips(x, y)
        started = []
        for k, (px, py) in enumerate(chips):
            cp = _remote(x_ref.at[mine], out_ref.at[me, mine], send_sems.at[k], recv_sems.at[k], (px, py, c))
            cp.start()
            started.append(cp)
        for k, (px, py) in enumerate(chips):
            landed = out_ref.at[2 * px + py, mine]
            _remote(landed, landed, send_sems.at[k], recv_sems.at[k], (px, py, c)).wait_recv()
            cp = _remote(landed, landed, pass_send.at[k], pass_recv.at[k], sibling)
            cp.start()
            started.append(cp)
        for k, (px, py) in enumerate(chips):
            passed = out_ref.at[2 * px + py, other]
            _remote(passed, passed, pass_send.at[k], pass_recv.at[k], sibling).wait_recv()
        for cp in started:
            cp.wait_send()

    got = _call(
        body, name="chip_all_gather", in_specs=[ANY], out_specs=ANY,
        out_shape=jax.ShapeDtypeStruct((N_CHIPS, rows, cols), buf.dtype),
        scratch_shapes=[pltpu.SemaphoreType.DMA((3,)), pltpu.SemaphoreType.DMA((3,)),
                        pltpu.SemaphoreType.DMA((3,)), pltpu.SemaphoreType.DMA((3,))],
    )(buf)
    me = 2 * lax.axis_index("x") + lax.axis_index("y")
    return lax.dynamic_update_slice(got, buf[None], (me, 0, 0))


def sibling_swap_halves(g4):
    n, rows, cols = g4.shape
    half = rows // 2

    def body(g_ref, got_ref, send_sem, recv_sem):
        x, y, c = _place()
        cp = _remote(g_ref.at[:, pl.ds((1 - c) * half, half), :], got_ref, send_sem, recv_sem, (x, y, 1 - c))
        cp.start()
        cp.wait()

    return _call(
        body, name="sibling_swap_halves", in_specs=[ANY], out_specs=ANY,
        out_shape=jax.ShapeDtypeStruct((n, half, cols), g4.dtype),
        scratch_shapes=[pltpu.SemaphoreType.DMA, pltpu.SemaphoreType.DMA],
    )(g4)


def add_own_half(g4, got, c):
    n, rows, cols = g4.shape
    half = rows // 2
    tm = _tile(half, 256, 16)
    nt = half // tm

    def body(c_ref, a_ref, b_ref, o_ref):
        o_ref[...] = (a_ref[...] + b_ref[...]).astype(o_ref.dtype)

    return _call(
        body, name="add_own_half",
        grid_spec=pltpu.PrefetchScalarGridSpec(
            num_scalar_prefetch=1, grid=(n, nt),
            in_specs=[pl.BlockSpec((1, tm, cols), lambda j, i, c_ref: (j, c_ref[0] * nt + i, 0)),
                      pl.BlockSpec((1, tm, cols), lambda j, i, c_ref: (j, i, 0))],
            out_specs=pl.BlockSpec((1, tm, cols), lambda j, i, c_ref: (j, i, 0))),
        out_shape=jax.ShapeDtypeStruct(got.shape, BF16),
        compiler_params=_cparams(("parallel", "parallel")),
    )(c.reshape(1).astype(jnp.int32), g4, got)


def chip_scatter(p4):
    n, rows, cols = p4.shape

    def body(p_ref, out_ref, send_sems, recv_sems):
        x, y, c = _place()
        me = 2 * x + y
        chips = _other_chips(x, y)
        started = []
        for k, (px, py) in enumerate(chips):
            cp = _remote(p_ref.at[2 * px + py], out_ref.at[me], send_sems.at[k], recv_sems.at[k], (px, py, c))
            cp.start()
            started.append(cp)
        for k, (px, py) in enumerate(chips):
            landed = out_ref.at[2 * px + py]
            _remote(landed, landed, send_sems.at[k], recv_sems.at[k], (px, py, c)).wait_recv()
        for cp in started:
            cp.wait_send()

    got = _call(
        body, name="chip_scatter", in_specs=[ANY], out_specs=ANY,
        out_shape=jax.ShapeDtypeStruct(p4.shape, p4.dtype),
        scratch_shapes=[pltpu.SemaphoreType.DMA((3,)), pltpu.SemaphoreType.DMA((3,))],
    )(p4)
    me = 2 * lax.axis_index("x") + lax.axis_index("y")
    return lax.dynamic_update_slice(got, lax.dynamic_slice_in_dim(p4, me, 1, axis=0), (me, 0, 0))


def sum_slots(a):
    n, rows, cols = a.shape
    tm = _tile(rows, 256, 16) if rows % 16 == 0 else rows

    def body(a_ref, o_ref):
        acc = a_ref[0].astype(F32)
        for k in range(1, n):
            acc = acc + a_ref[k].astype(F32)
        o_ref[...] = acc

    return _call(
        body, name="sum_slots_%d" % n, grid=(rows // tm,),
        in_specs=[pl.BlockSpec((n, tm, cols), lambda i: (0, i, 0))],
        out_specs=pl.BlockSpec((tm, cols), lambda i: (i, 0)),
        out_shape=jax.ShapeDtypeStruct((rows, cols), F32),
        compiler_params=_cparams(("parallel",)),
    )(a)


def sibling_join(s):
    half, cols = s.shape

    def body(s_ref, got_ref, send_sem, recv_sem):
        x, y, c = _place()
        cp = _remote(s_ref, got_ref, send_sem, recv_sem, (x, y, 1 - c))
        cp.start()
        cp.wait()

    got = _call(
        body, name="sibling_join", in_specs=[ANY], out_specs=ANY,
        out_shape=jax.ShapeDtypeStruct(s.shape, s.dtype),
        scratch_shapes=[pltpu.SemaphoreType.DMA, pltpu.SemaphoreType.DMA],
    )(s)
    c = lax.axis_index("c")
    out = jnp.zeros((2 * half, cols), s.dtype)
    out = lax.dynamic_update_slice(out, s, (c * half, 0))
    return lax.dynamic_update_slice(out, got, ((1 - c) * half, 0))


def all_devices_gather(buf):
    rows, cols = buf.shape

    def body(b_ref, out_ref, send_sems, recv_sems, local_sem):
        x, y, c = _place()
        me = 4 * x + 2 * y + c
        local = pltpu.make_async_copy(b_ref, out_ref.at[me], local_sem)
        local.start()
        peers = [((x + dx) % 2, (y + dy) % 2, (c + dc) % 2)
                 for dx in (0, 1) for dy in (0, 1) for dc in (0, 1) if dx + dy + dc > 0]
        started = []
        for k, peer in enumerate(peers):
            cp = _remote(b_ref, out_ref.at[me], send_sems.at[k], recv_sems.at[k], peer)
            cp.start()
            started.append(cp)
        for k, (px, py, pc) in enumerate(peers):
            landed = out_ref.at[4 * px + 2 * py + pc]
            _remote(landed, landed, send_sems.at[k], recv_sems.at[k], (px, py, pc)).wait_recv()
        for cp in started:
            cp.wait_send()
        local.wait()

    return _call(
        body, name="all_devices_gather", in_specs=[ANY], out_specs=ANY,
        out_shape=jax.ShapeDtypeStruct((N_DEV, rows, cols), buf.dtype),
        scratch_shapes=[pltpu.SemaphoreType.DMA((7,)), pltpu.SemaphoreType.DMA((7,)), pltpu.SemaphoreType.DMA],
    )(buf)


def adamw(w, g, m, v):
    shape = w.shape
    w2, g2, m2, v2 = [a.reshape(-1, shape[-1]) for a in (w, g, m, v)]
    rows, cols = w2.shape
    tm = _tile(rows, 256, 8) if rows % 8 == 0 else rows

    def body(w_ref, g_ref, m_ref, v_ref, d_ref, nm_ref, nv_ref):
        gv = g_ref[...]
        nm = ADAM_B1 * m_ref[...] + (1.0 - ADAM_B1) * gv
        nv = ADAM_B2 * v_ref[...] + (1.0 - ADAM_B2) * (gv * gv)
        m_hat = nm / (1.0 - ADAM_B1 ** ADAM_STEP)
        v_hat = nv / (1.0 - ADAM_B2 ** ADAM_STEP)
        d_ref[...] = -ADAM_LR * (m_hat / (jnp.sqrt(v_hat) + ADAM_EPS) + ADAM_WD * w_ref[...])
        nm_ref[...] = nm
        nv_ref[...] = nv

    spec = pl.BlockSpec((tm, cols), lambda i: (i, 0))
    outs = _call(
        body, name="adamw", grid=(rows // tm,), in_specs=[spec] * 4, out_specs=[spec] * 3,
        out_shape=[jax.ShapeDtypeStruct((rows, cols), F32)] * 3,
        compiler_params=_cparams(("parallel",)),
    )(w2, g2, m2, v2)
    return [o.reshape(shape) for o in outs]


WEIGHTS = ("meta_tokens", "norm1_g", "w_in", "fox_f_bias", "fox_q_norm_g", "fox_k_norm_g", "gdn_conv_w",
           "gdn_a_log", "gdn_dt_bias", "gdn_norm_g", "w_branch_a", "w_branch_b", "w_out", "norm2_g", "w_up",
           "ffn_conv_w", "w_down")
SHARD_AXIS = {"meta_tokens": -1, "w_in": -1, "gdn_conv_w": -1, "w_branch_a": -1, "w_branch_b": -2, "w_out": -2,
              "w_up": -1, "ffn_conv_w": -1, "w_down": -2}
MATMUL_WEIGHTS = ("w_in", "w_branch_a", "w_branch_b", "w_out", "w_up", "w_down")
SMALL_SHARDED = ("meta_tokens", "gdn_conv_w", "ffn_conv_w")
REPLICATED = tuple(n for n in WEIGHTS if n not in SHARD_AXIS)


def _pack(arrays, dtype, row_align):
    flat = jnp.concatenate([a.reshape(-1).astype(dtype) for a in arrays])
    block = row_align * COMM_COLS
    total = -(-flat.shape[0] // block) * block
    flat = jnp.concatenate([flat, jnp.zeros((total - flat.shape[0],), dtype)])
    return flat.reshape(-1, COMM_COLS)


def _unpack(buf, shapes):
    flat, out, pos = buf.reshape(-1), [], 0
    for shape in shapes:
        size = 1
        for d in shape:
            size *= d
        out.append(flat[pos:pos + size].reshape(shape))
        pos += size
    return out


def _gather_full(shards, names, dtype, row_align):
    got = chip_all_gather(_pack([shards[n] for n in names], dtype, row_align))
    per_chip = [_unpack(got[j], [shards[n].shape for n in names]) for j in range(N_CHIPS)]
    return {n: jnp.concatenate([per_chip[j][i] for j in range(N_CHIPS)], axis=SHARD_AXIS[n]).astype(F32)
            for i, n in enumerate(names)}


def _shard_of(full, name, j):
    axis = SHARD_AXIS[name] % full.ndim
    size = full.shape[axis] // N_CHIPS
    return lax.slice_in_dim(full, j * size, (j + 1) * size, axis=axis)


def kernel(x, meta_tokens, norm1_g, w_in, fox_f_bias, fox_q_norm_g, fox_k_norm_g, gdn_conv_w, gdn_a_log, gdn_dt_bias, gdn_norm_g, w_branch_a, w_branch_b, w_out, norm2_g, w_up, ffn_conv_w, w_down, loss_target, m_meta_tokens, m_norm1_g, m_w_in, m_fox_f_bias, m_fox_q_norm_g, m_fox_k_norm_g, m_gdn_conv_w, m_gdn_a_log, m_gdn_dt_bias, m_gdn_norm_g, m_w_branch_a, m_w_branch_b, m_w_out, m_norm2_g, m_w_up, m_ffn_conv_w, m_w_down, v_meta_tokens, v_norm1_g, v_w_in, v_fox_f_bias, v_fox_q_norm_g, v_fox_k_norm_g, v_gdn_conv_w, v_gdn_a_log, v_gdn_dt_bias, v_gdn_norm_g, v_w_branch_a, v_w_branch_b, v_w_out, v_norm2_g, v_w_up, v_ffn_conv_w, v_w_down):
    w_loc = dict(zip(WEIGHTS, (meta_tokens, norm1_g, w_in, fox_f_bias, fox_q_norm_g, fox_k_norm_g, gdn_conv_w,
                               gdn_a_log, gdn_dt_bias, gdn_norm_g, w_branch_a, w_branch_b, w_out, norm2_g, w_up,
                               ffn_conv_w, w_down)))
    m_loc = dict(zip(WEIGHTS, (m_meta_tokens, m_norm1_g, m_w_in, m_fox_f_bias, m_fox_q_norm_g, m_fox_k_norm_g,
                               m_gdn_conv_w, m_gdn_a_log, m_gdn_dt_bias, m_gdn_norm_g, m_w_branch_a, m_w_branch_b,
                               m_w_out, m_norm2_g, m_w_up, m_ffn_conv_w, m_w_down)))
    v_loc = dict(zip(WEIGHTS, (v_meta_tokens, v_norm1_g, v_w_in, v_fox_f_bias, v_fox_q_norm_g, v_fox_k_norm_g,
                               v_gdn_conv_w, v_gdn_a_log, v_gdn_dt_bias, v_gdn_norm_g, v_w_branch_a, v_w_branch_b,
                               v_w_out, v_norm2_g, v_w_up, v_ffn_conv_w, v_w_down)))
    c = lax.axis_index("c")

    full = {n: w_loc[n] for n in REPLICATED}
    full.update(_gather_full(w_loc, MATMUL_WEIGHTS, BF16, COMM_ROW_ALIGN))
    full.update(_gather_full(w_loc, SMALL_SHARDED, F32, COMM_ROW_ALIGN_SMALL))
    full["w_in"] = pad_w_in(full["w_in"])

    loss, (g_full, g_x) = jax.value_and_grad(local_loss, argnums=(0, 1))(full, x[0], loss_target[0])
    g_full = dict(g_full)
    g_full["w_in"] = unpad_w_in(g_full["w_in"])

    sharded = MATMUL_WEIGHTS + SMALL_SHARDED
    g4 = jnp.stack([_pack([_shard_of(g_full[n], n, j) for n in sharded], F32, COMM_ROW_ALIGN)
                    for j in range(N_CHIPS)])
    pair_sum = add_own_half(g4, sibling_swap_halves(g4), c)
    g_shard = sibling_join(sum_slots(chip_scatter(pair_sum)))
    grads = dict(zip(sharded, _unpack(g_shard, [w_loc[n].shape for n in sharded])))
    g_rep = sum_slots(all_devices_gather(_pack([g_full[n] for n in REPLICATED], F32, 8)))
    grads.update(zip(REPLICATED, _unpack(g_rep, [w_loc[n].shape for n in REPLICATED])))

    loss = lax.psum(loss, ("x", "y", "c"))
    upd = {n: adamw(w_loc[n], grads[n], m_loc[n], v_loc[n]) for n in WEIGHTS}
    return (loss, g_x[None], *[grads[n] for n in WEIGHTS], *[upd[n][0] for n in WEIGHTS],
            *[upd[n][1] for n in WEIGHTS], *[upd[n][2] for n in WEIGHTS])
```

```python
import functools

import jax
import jax.numpy as jnp
from jax import lax
from jax.experimental import pallas as pl
from jax.experimental.pallas import tpu as pltpu

F32 = jnp.float32
BF16 = jnp.bfloat16
HI = lax.Precision.HIGHEST
MESH = pl.DeviceIdType.MESH

D_MODEL = 1024
N_META = 16
EPS = 1e-6
FOX_HEADS, FOX_HD = 8, 64
FOX_W = FOX_HEADS * FOX_HD
GDN_HEADS, GDN_HD, GDN_CHUNK, GDN_CONV = 8, 128, 64, 4
GDN_W = GDN_HEADS * GDN_HD
D_FF = 2816
FFN_CONV = 3
D_IN = 7704
D_IN_PAD = 8192
IN_SEGS = ((0, 1536, 0), (1536, 8, 1536), (1544, 3072, 1664), (4616, 16, 4736), (4632, 1024, 4864), (5656, 2048, 5888))
ROW_ALIGN = 256
ATT_BLK = 256
VMEM_LIMIT = 48 * 1024 * 1024
LANE = 128

ADAM_LR, ADAM_B1, ADAM_B2, ADAM_EPS, ADAM_WD, ADAM_STEP = 0.001, 0.9, 0.999, 1e-08, 0.01, 10


def _call(body, **kw):
    return pl.pallas_call(body, **kw)


def _tile(n, target, mult):
    best, t = None, mult
    while t <= min(n, target):
        if n % t == 0:
            best = t
        t += mult
    assert best is not None, (n, target, mult)
    return best


def _cparams(sem):
    return pltpu.CompilerParams(dimension_semantics=sem, vmem_limit_bytes=VMEM_LIMIT)


def _raw_dot(a, b, ca, cb, precise):
    dims = (((ca,), (cb,)), ((), ()))
    a_hi, b_hi = a.astype(BF16), b.astype(BF16)
    out = lax.dot_general(a_hi, b_hi, dims, preferred_element_type=F32)
    if precise:
        a_lo = (a - a_hi.astype(F32)).astype(BF16)
        b_lo = (b - b_hi.astype(F32)).astype(BF16)
        out = out + (lax.dot_general(a_hi, b_lo, dims, preferred_element_type=F32)
                     + lax.dot_general(a_lo, b_hi, dims, preferred_element_type=F32))
    return out


def _make_dot(ca, cb, precise):
    @jax.custom_vjp
    def f(a, b):
        return _raw_dot(a, b, ca, cb, precise)

    def fwd(a, b):
        return f(a, b), (a, b)

    def bwd(res, ct):
        a, b = res
        if ca == 1:
            da = _raw_dot(ct, b, 1, 1 if cb == 0 else 0, precise)
        else:
            da = _raw_dot(b, ct, 1 if cb == 0 else 0, 1, precise)
        if cb == 0:
            db = _raw_dot(a, ct, 0 if ca == 1 else 1, 0, precise)
        else:
            db = _raw_dot(ct, a, 0, 0 if ca == 1 else 1, precise)
        return da, db

    f.defvjp(fwd, bwd)
    return f


_DOTS = {(ca, cb, p): _make_dot(ca, cb, p) for ca in (0, 1) for cb in (0, 1) for p in (False, True)}


def _dot(a, b, ca=1, cb=0, precise=False):
    return _DOTS[(ca, cb, precise)](a, b)


def _mm_call(a, b, name, ta=False, tb=False, out_dtype=F32):
    k, m = a.shape if ta else a.shape[::-1]
    n, kb = b.shape if tb else b.shape[::-1]
    assert k == kb, (a.shape, b.shape)
    tm = _tile(m, 1408, LANE if ta else 16)
    tn = _tile(n, 1408, LANE)
    tk = _tile(k, 1408, LANE)
    nk = k // tk
    dims = (((0 if ta else 1,), (1 if tb else 0,)), ((), ()))

    def body(a_ref, b_ref, o_ref, *scratch):
        part = lax.dot_general(a_ref[...], b_ref[...], dims, preferred_element_type=F32)
        if nk == 1:
            o_ref[...] = part.astype(o_ref.dtype)
            return
        acc_ref = scratch[0]
        kk = pl.program_id(2)

        @pl.when(kk == 0)
        def _():
            acc_ref[...] = part

        @pl.when(kk > 0)
        def _():
            acc_ref[...] += part

        @pl.when(kk == nk - 1)
        def _():
            o_ref[...] = acc_ref[...].astype(o_ref.dtype)

    return _call(
        body, name=name, grid=(m // tm, n // tn, nk),
        in_specs=[pl.BlockSpec((tk, tm), lambda i, j, kk: (kk, i)) if ta else
                  pl.BlockSpec((tm, tk), lambda i, j, kk: (i, kk)),
                  pl.BlockSpec((tn, tk), lambda i, j, kk: (j, kk)) if tb else
                  pl.BlockSpec((tk, tn), lambda i, j, kk: (kk, j))],
        out_specs=pl.BlockSpec((tm, tn), lambda i, j, kk: (i, j)),
        out_shape=jax.ShapeDtypeStruct((m, n), out_dtype),
        scratch_shapes=[pltpu.VMEM((tm, tn), F32)] if nk > 1 else [],
        compiler_params=_cparams(("parallel", "parallel", "arbitrary")),
    )(a, b)


def _make_mm(out_dtype):
    @jax.custom_vjp
    def op(a, w):
        return _mm_call(a.astype(BF16), w.astype(BF16), "mm_fwd", out_dtype=out_dtype)

    def fwd(a, w):
        a_b, w_b = a.astype(BF16), w.astype(BF16)
        return _mm_call(a_b, w_b, "mm_fwd", out_dtype=out_dtype), (a_b, w_b, jnp.zeros((), a.dtype))

    def bwd(res, ct):
        a_b, w_b, like_a = res
        ct_b = ct.astype(BF16)
        return (_mm_call(ct_b, w_b, "mm_dx", tb=True, out_dtype=like_a.dtype),
                _mm_call(a_b, ct_b, "mm_dw", ta=True))

    op.defvjp(fwd, bwd)
    return op


mm = _make_mm(F32)
mm_bf16 = _make_mm(BF16)


def _rows_specs(rows, tm, ncb, bc):
    specs = []
    for idx, r in enumerate(rows):
        if idx in bc:
            specs.append(pl.BlockSpec((tm, r.shape[1]), lambda i, j: (i, 0)))
        else:
            specs.append(pl.BlockSpec((tm, r.shape[1] // ncb), lambda i, j: (i, j)))
    return specs


def _param_specs(params):
    return [pl.BlockSpec(p.shape, lambda i, j: (0, 0)) for p in params]


def _group_slices(refs, groups, g, whole):
    out = []
    for idx, r in enumerate(refs):
        w = r.shape[1] // groups
        out.append((r[...] if idx in whole else r[:, g * w:(g + 1) * w]).astype(F32))
    return out


def _rows_fwd_call(fns, rows, params, outs, tm, ncb, bc, name, out_dtypes=None):
    r_total = rows[0].shape[0]
    nr, groups = len(rows), len(fns)
    out_dtypes = out_dtypes or [F32] * len(outs)

    def body(*refs):
        pvals = [r[...] for r in refs[nr:nr + len(params)]]
        for g, fn in enumerate(fns):
            res = fn(*_group_slices(refs[:nr], groups, g, bc), *pvals)
            for o_ref, val in zip(refs[nr + len(params):], res):
                w = o_ref.shape[1] // groups
                o_ref[:, g * w:(g + 1) * w] = val.astype(o_ref.dtype)

    return _call(
        body, name=name, grid=(r_total // tm, ncb),
        in_specs=_rows_specs(rows, tm, ncb, bc) + _param_specs(params),
        out_specs=[pl.BlockSpec((tm, w * groups), lambda i, j: (i, j)) for w in outs],
        out_shape=[jax.ShapeDtypeStruct((r_total, w * groups * ncb), dt) for w, dt in zip(outs, out_dtypes)],
        compiler_params=_cparams(("parallel", "parallel")),
    )(*rows, *params)


def _rows_bwd_call(fns, rows, params, cts, tm, ncb, bc, name):
    r_total = rows[0].shape[0]
    nr, npar, nct, groups = len(rows), len(params), len(cts), len(fns)

    def body(*refs):
        i, j = pl.program_id(0), pl.program_id(1)
        pvals = [r[...] for r in refs[nr:nr + npar]]
        ct_refs = refs[nr + npar:nr + npar + nct]
        d_refs = refs[nr + npar + nct:]
        shared = {idx: None for idx in list(bc) + list(range(nr, nr + npar))}
        for g, fn in enumerate(fns):
            _, vjp = jax.vjp(lambda *a, fn=fn: tuple(fn(*a)), *_group_slices(refs[:nr], groups, g, bc), *pvals)
            grads = vjp(tuple(_group_slices(ct_refs, groups, g, ())))
            for idx in range(nr + npar):
                if idx in shared:
                    shared[idx] = grads[idx] if shared[idx] is None else shared[idx] + grads[idx]
                else:
                    w = d_refs[idx].shape[1] // groups
                    d_refs[idx][:, g * w:(g + 1) * w] = grads[idx].astype(d_refs[idx].dtype)
        for idx, total in shared.items():
            first = (j == 0) if idx < nr else ((i == 0) & (j == 0))

            @pl.when(first)
            def _(idx=idx):
                d_refs[idx][...] = jnp.zeros_like(d_refs[idx])
            d_refs[idx][...] += total

    ct_specs = [pl.BlockSpec((tm, c.shape[1] // ncb), lambda i, j: (i, j)) for c in cts]
    return _call(
        body, name=name + "_bwd", grid=(r_total // tm, ncb),
        in_specs=_rows_specs(rows, tm, ncb, bc) + _param_specs(params) + ct_specs,
        out_specs=_rows_specs(rows, tm, ncb, bc) + _param_specs(params),
        out_shape=[jax.ShapeDtypeStruct(a.shape, a.dtype) for a in list(rows) + list(params)],
        compiler_params=_cparams(("arbitrary", "arbitrary")),
    )(*rows, *params, *cts)


def rowop(fn, name, outs, tm, ncb=1, bc=(), out_dtypes=None):
    fns = list(fn) if isinstance(fn, (list, tuple)) else [fn]

    @jax.custom_vjp
    def op(rows, params):
        return tuple(_rows_fwd_call(fns, rows, params, outs, tm, ncb, bc, name, out_dtypes))

    def fwd(rows, params):
        return op(rows, params), (rows, params)

    def bwd(res, cts):
        rows, params = res
        d = _rows_bwd_call(fns, rows, params, cts, tm, ncb, bc, name)
        return tuple(d[:len(rows)]), tuple(d[len(rows):])

    op.defvjp(fwd, bwd)
    return op


def _sigmoid(x):
    return 1.0 / (1.0 + jnp.exp(-x))


def _silu(x):
    return x * _sigmoid(x)


def _softplus(x):
    return jnp.maximum(x, 0.0) + jnp.log(1.0 + jnp.exp(-jnp.abs(x)))


def _f_rmsnorm(x, g):
    return (x * lax.rsqrt(jnp.mean(x * x, axis=-1, keepdims=True) + EPS) * g,)


def _f_qnorm(x, g):
    return (x * lax.rsqrt(jnp.mean(x * x, axis=-1, keepdims=True) + EPS) * (g * (FOX_HD ** -0.5)),)


def _f_logsig(x, b):
    return (-_softplus(-(x + b)),)


def _f_gdn_q(x):
    y = _silu(x)
    return (y * lax.rsqrt(jnp.sum(y * y, axis=-1, keepdims=True) + EPS) * (GDN_HD ** -0.5),)


def _f_gdn_k(x):
    y = _silu(x)
    return (y * lax.rsqrt(jnp.sum(y * y, axis=-1, keepdims=True) + EPS),)


def _f_gdn_v(x):
    return (_silu(x),)


def _f_gdn_gates(bl, al, a_log, dt_bias):
    return _sigmoid(bl), -jnp.exp(a_log) * _softplus(al + dt_bias)


def _f_gdn_out(o, z, g):
    return (o * lax.rsqrt(jnp.mean(o * o, axis=-1, keepdims=True) + EPS) * g * _silu(z),)


def _f_merge(g0, g1, ya, yb):
    return (_sigmoid(g0) * ya + _sigmoid(g1) * yb,)


def _f_residual(a, b, keep):
    return ((a + b) * keep,)


def _f_residual_norm(a, b, keep, g):
    r = (a + b) * keep
    return r, _f_rmsnorm(r, g)[0]


def _f_glu(a, b):
    return (_silu(a) * b,)


def _glu_call(up, ct):
    t_total, two_f = up.shape
    f = two_f // 2
    tm = _tile(t_total, 128, 16)
    wc = _tile(f, 1408, LANE)

    def body(*refs):
        up_ref, out_ref = refs[0], refs[-1]
        for c0 in range(0, f, wc):
            a, b = up_ref[:, c0:c0 + wc].astype(F32), up_ref[:, f + c0:f + c0 + wc].astype(F32)
            if ct is None:
                out_ref[:, c0:c0 + wc] = _f_glu(a, b)[0].astype(out_ref.dtype)
            else:
                _, vjp = jax.vjp(_f_glu, a, b)
                da, db = vjp((refs[1][:, c0:c0 + wc].astype(F32),))
                out_ref[:, c0:c0 + wc] = da.astype(out_ref.dtype)
                out_ref[:, f + c0:f + c0 + wc] = db.astype(out_ref.dtype)

    wide = pl.BlockSpec((tm, two_f), lambda i: (i, 0))
    narrow = pl.BlockSpec((tm, f), lambda i: (i, 0))
    return _call(
        body, name="ffn_glu" if ct is None else "ffn_glu_bwd", grid=(t_total // tm,),
        in_specs=[wide] if ct is None else [wide, narrow], out_specs=narrow if ct is None else wide,
        out_shape=jax.ShapeDtypeStruct((t_total, f), BF16) if ct is None else jax.ShapeDtypeStruct(up.shape, up.dtype),
        compiler_params=_cparams(("parallel",)),
    )(*((up,) if ct is None else (up, ct)))


@jax.custom_vjp
def glu(up):
    return _glu_call(up, None)


glu.defvjp(lambda up: (glu(up), up), lambda up, ct: (_glu_call(up, ct),))


def _shift_down(x, halo, s, row8):
    rx = pltpu.roll(x, s, 0)
    top = jnp.where(row8 < s, pltpu.roll(halo, s, 0), rx[:8])
    return jnp.concatenate([top, rx[8:]], axis=0)


def _shift_up(x, nxt, s, row8):
    tm = x.shape[0]
    rx = pltpu.roll(x, tm - s, 0)
    bot = jnp.where(row8 >= 8 - s, pltpu.roll(nxt, 8 - s, 0), rx[tm - 8:])
    return jnp.concatenate([rx[:tm - 8], bot], axis=0)


def _conv_tiles(r_total, c_total):
    return _tile(r_total, 768, 8), _tile(c_total, 1408, LANE)


def _halo_rows(dtype):
    return 16 if dtype == BF16 else 8


def _conv_fwd_call(x, w8, k_taps, out_dtype):
    r_total, c_total = x.shape
    tm, tc = _conv_tiles(r_total, c_total)
    hr = _halo_rows(x.dtype)
    hb = tm // hr

    def body(x_ref, halo_ref, w_ref, y_ref):
        i = pl.program_id(1)
        xt = x_ref[...].astype(F32)
        halo = jnp.where(i > 0, halo_ref[...].astype(F32)[hr - 8:hr], 0.0)
        row8 = lax.broadcasted_iota(jnp.int32, (8, tc), 0)
        acc = w_ref[k_taps - 1:k_taps, :] * xt
        for k in range(k_taps - 1):
            acc += w_ref[k:k + 1, :] * _shift_down(xt, halo, k_taps - 1 - k, row8)
        y_ref[...] = acc.astype(y_ref.dtype)

    return _call(
        body, name="dwconv_fwd", grid=(c_total // tc, r_total // tm),
        in_specs=[pl.BlockSpec((tm, tc), lambda c, i: (i, c)),
                  pl.BlockSpec((hr, tc), lambda c, i: (jnp.maximum(i * hb - 1, 0), c)),
                  pl.BlockSpec((8, tc), lambda c, i: (0, c))],
        out_specs=pl.BlockSpec((tm, tc), lambda c, i: (i, c)),
        out_shape=jax.ShapeDtypeStruct(x.shape, out_dtype),
        compiler_params=_cparams(("parallel", "parallel")),
    )(x, x, w8)


def _conv_bwd_call(x, w8, dy, k_taps):
    r_total, c_total = x.shape
    tm, tc = _conv_tiles(r_total, c_total)
    hr = _halo_rows(dy.dtype)
    hb = tm // hr
    n_i = r_total // tm

    def body(x_ref, w_ref, dy_ref, nxt_ref, dx_ref, dw_ref):
        i = pl.program_id(1)
        xt, dyt = x_ref[...].astype(F32), dy_ref[...].astype(F32)
        nxt = jnp.where(i < n_i - 1, nxt_ref[...].astype(F32)[0:8], 0.0)
        row8 = lax.broadcasted_iota(jnp.int32, (8, tc), 0)
        dx = w_ref[k_taps - 1:k_taps, :] * dyt
        upd = jnp.where(row8 == k_taps - 1, jnp.sum(dyt * xt, axis=0, keepdims=True), 0.0)
        for k in range(k_taps - 1):
            dy_ahead = _shift_up(dyt, nxt, k_taps - 1 - k, row8)
            dx += w_ref[k:k + 1, :] * dy_ahead
            upd = jnp.where(row8 == k, jnp.sum(dy_ahead * xt, axis=0, keepdims=True), upd)
        dx_ref[...] = dx.astype(dx_ref.dtype)

        @pl.when(i == 0)
        def _():
            dw_ref[...] = jnp.zeros_like(dw_ref)

        dw_ref[...] += upd

    return _call(
        body, name="dwconv_bwd", grid=(c_total // tc, n_i),
        in_specs=[pl.BlockSpec((tm, tc), lambda c, i: (i, c)),
                  pl.BlockSpec((8, tc), lambda c, i: (0, c)),
                  pl.BlockSpec((tm, tc), lambda c, i: (i, c)),
                  pl.BlockSpec((hr, tc), lambda c, i: (jnp.minimum((i + 1) * hb, r_total // hr - 1), c))],
        out_specs=[pl.BlockSpec((tm, tc), lambda c, i: (i, c)), pl.BlockSpec((8, tc), lambda c, i: (0, c))],
        out_shape=[jax.ShapeDtypeStruct(x.shape, x.dtype), jax.ShapeDtypeStruct(w8.shape, F32)],
        compiler_params=_cparams(("parallel", "arbitrary")),
    )(x, w8, dy, dy)


def make_dwconv(k_taps, out_dtype=None):
    @jax.custom_vjp
    def op(x, w8):
        return _conv_fwd_call(x, w8, k_taps, out_dtype or x.dtype)

    def fwd(x, w8):
        return op(x, w8), (x, w8)

    def bwd(res, dy):
        x, w8 = res
        dx, dw = _conv_bwd_call(x, w8, dy, k_taps)
        return dx, dw

    op.defvjp(fwd, bwd)
    return op


def _cumsum_call(x, reverse):
    h, t_total = x.shape
    tb = _tile(t_total, 256, LANE)
    nb = t_total // tb

    def body(x_ref, o_ref, carry_ref):
        i = pl.program_id(0)

        @pl.when(i == 0)
        def _():
            carry_ref[...] = jnp.zeros_like(carry_ref)

        r = lax.broadcasted_iota(jnp.int32, (tb, tb), 0)
        c = lax.broadcasted_iota(jnp.int32, (tb, tb), 1)
        tri = jnp.where((r >= c) if reverse else (r <= c), 1.0, 0.0).astype(F32)
        xv = x_ref[...]
        carry = jnp.max(carry_ref[...], axis=1, keepdims=True)
        o_ref[...] = _raw_dot(xv, tri, 1, 0, True) + carry
        carry_ref[...] = jnp.broadcast_to(carry + jnp.sum(xv, axis=1, keepdims=True), carry_ref.shape)

    imap = (lambda i: (0, nb - 1 - i)) if reverse else (lambda i: (0, i))
    return _call(
        body, name="cumsum_rev" if reverse else "cumsum", grid=(nb,),
        in_specs=[pl.BlockSpec((h, tb), imap)], out_specs=pl.BlockSpec((h, tb), imap),
        out_shape=jax.ShapeDtypeStruct(x.shape, F32), scratch_shapes=[pltpu.VMEM((h, LANE), F32)],
        compiler_params=_cparams(("arbitrary",)),
    )(x)


@jax.custom_vjp
def cumsum_lanes(x):
    return _cumsum_call(x, False)


cumsum_lanes.defvjp(lambda x: (cumsum_lanes(x), None), lambda _, ct: (_cumsum_call(ct, True),))


NEG_BIG = -1e30


def _attn_sub_tiles(nb):
    return max(s for s in (3, 2, 1) if nb % s == 0)


EXP_ZERO = -92.0
SMEM = pl.BlockSpec(memory_space=pltpu.SMEM)


def _max_row_norm_sq(x):
    h_total, t_total, hd = x.shape
    tb = _tile(t_total, 2816, 8)

    def body(x_ref, o_ref):
        @pl.when(pl.program_id(1) == 0)
        def _():
            o_ref[...] = jnp.zeros_like(o_ref)

        xv = x_ref[0]
        top = jnp.max(jnp.sum(xv * xv, axis=1, keepdims=True), axis=0, keepdims=True)
        o_ref[0] = jnp.maximum(o_ref[0], top)

    return _call(
        body, name="max_row_norm", grid=(h_total, t_total // tb),
        in_specs=[pl.BlockSpec((1, tb, hd), lambda h, i: (h, i, 0))],
        out_specs=pl.BlockSpec((1, 8, LANE), lambda h, i: (h, 0, 0)),
        out_shape=jax.ShapeDtypeStruct((h_total, 8, LANE), F32),
        compiler_params=_cparams(("parallel", "arbitrary")),
    )(x)


def _attn_skip_tables(q, k, f_row):
    bound = 2.0 * jnp.sqrt(_max_row_norm_sq(q)[:, 0, :1] * _max_row_norm_sq(k)[:, 0, :1])
    return EXP_ZERO - bound, f_row[:, :, 0, 0], f_row[:, :, 0, -1]


def _attn_fwd_call(q, k, v, f_col, f_row, tables):
    h_total, t_total, hd = q.shape
    blk = f_row.shape[-1]
    nb = t_total // blk
    nsub = _attn_sub_tiles(nb)
    tq = nsub * blk

    def body(thr_ref, first_ref, last_ref, q_ref, k_ref, vt_ref, fc_ref, fr_ref, o_ref, lse_ref):
        h = pl.program_id(0)
        i = pl.program_id(1)
        gap_needed = thr_ref[h, 0]
        f_tile = first_ref[h, i * nsub]
        j_start = lax.while_loop(lambda j: (j < i * nsub) & (f_tile - last_ref[h, j] < gap_needed),
                                 lambda j: j + 1, 0)
        r = lax.broadcasted_iota(jnp.int32, (blk, blk), 0)
        c = lax.broadcasted_iota(jnp.int32, (blk, blk), 1)
        qs = [q_ref[0, s * blk:(s + 1) * blk, :].astype(BF16) for s in range(nsub)]
        fqs = [fr_ref[0, i * nsub + s] for s in range(nsub)]

        def load_kv(j):
            off = pl.multiple_of(j * blk, blk)
            return k_ref[0, pl.ds(off, blk), :], vt_ref[0, j], fc_ref[0, pl.ds(off, blk), :]

        def tile(kv, s, carry, diagonal):
            kj, vtj, fk = kv
            m, l, acc = carry
            st = _raw_dot(kj, qs[s], 1, 1, False) + fqs[s] - fk
            if diagonal:
                st = jnp.where(r <= c, st, NEG_BIG)
            m_new = jnp.maximum(m, jnp.max(st, axis=0, keepdims=True))
            p = jnp.exp(st - m_new)
            alpha = jnp.exp(m - m_new)
            l = alpha * l + jnp.sum(p, axis=0, keepdims=True)
            acc = alpha * acc + _raw_dot(vtj, p, 1, 0, False)
            return m_new, l, acc

        def below_diagonal(j, carry):
            kv = load_kv(j)
            return tuple(tile(kv, s, carry[s], False) for s in range(nsub))

        init = tuple((jnp.full((1, blk), NEG_BIG, F32), jnp.zeros((1, blk), F32), jnp.zeros((hd, blk), F32))
                     for _ in range(nsub))
        carry = list(lax.fori_loop(j_start, i * nsub, below_diagonal, init))
        for d in range(nsub):
            kv = load_kv(i * nsub + d)
            for s in range(d, nsub):
                carry[s] = tile(kv, s, carry[s], s == d)
        for s, (m, l, acc) in enumerate(carry):
            o_ref[0, :, s * blk:(s + 1) * blk] = acc / l
            lse_ref[0, s] = m + jnp.log(l)

    vt = v.reshape(h_total, nb, blk, hd).transpose(0, 1, 3, 2).astype(BF16)
    return _call(
        body, name="fox_fwd", grid=(h_total, nb // nsub),
        in_specs=[SMEM, SMEM, SMEM,
                  pl.BlockSpec((1, tq, hd), lambda h, i: (h, i, 0)),
                  pl.BlockSpec((1, t_total, hd), lambda h, i: (h, 0, 0)),
                  pl.BlockSpec((1, nb, hd, blk), lambda h, i: (h, 0, 0, 0)),
                  pl.BlockSpec((1, t_total, 1), lambda h, i: (h, 0, 0)),
                  pl.BlockSpec((1, nb, 1, blk), lambda h, i: (h, 0, 0, 0))],
        out_specs=[pl.BlockSpec((1, hd, tq), lambda h, i: (h, 0, i)),
                   pl.BlockSpec((1, nsub, 1, blk), lambda h, i: (h, i, 0, 0))],
        out_shape=[jax.ShapeDtypeStruct((h_total, hd, t_total), F32), jax.ShapeDtypeStruct(f_row.shape, F32)],
        compiler_params=_cparams(("parallel", "parallel")),
    )(*tables, q, k.astype(BF16), vt, f_col, f_row)


def _attn_bwd_call(q, k, v, f_col, f_row, tables, lse_row, delta_row, do_blk):
    h_total, t_total, hd = q.shape
    blk = f_row.shape[-1]
    nb = t_total // blk
    nsub = _attn_sub_tiles(nb)
    tkv = nsub * blk

    def body(thr_ref, first_ref, last_ref, q_ref, do_ref, k_ref, v_ref, fc_ref, fr_ref, lse_ref, dl_ref,
             dq_ref, dk_ref, dv_ref, dfk_ref, dfq_ref):
        h = pl.program_id(0)
        j = pl.program_id(1)
        gap_needed = thr_ref[h, 0]
        f_tile = last_ref[h, j * nsub + nsub - 1]
        i_stop = lax.while_loop(lambda i: (i < nb) & (first_ref[h, jnp.minimum(i, nb - 1)] - f_tile >= gap_needed),
                                lambda i: i + 1, (j + 1) * nsub)

        @pl.when(j == 0)
        def _():
            dq_ref[...] = jnp.zeros_like(dq_ref)
            dfq_ref[...] = jnp.zeros_like(dfq_ref)

        ks = [k_ref[0, s * blk:(s + 1) * blk, :].astype(BF16) for s in range(nsub)]
        vs = [v_ref[0, s * blk:(s + 1) * blk, :].astype(BF16) for s in range(nsub)]
        fks = [fc_ref[0, s * blk:(s + 1) * blk, :] for s in range(nsub)]
        r = lax.broadcasted_iota(jnp.int32, (blk, blk), 0)
        c = lax.broadcasted_iota(jnp.int32, (blk, blk), 1)

        def q_step(i, accs, subs):
            off = pl.multiple_of(i * blk, blk)
            qi = q_ref[0, pl.ds(off, blk), :]
            doi = do_ref[0, i]
            fq, lse, dl = fr_ref[0, i], lse_ref[0, i], dl_ref[0, i]
            accs = list(accs)
            dq_i, dfq_i = None, None
            for s, diagonal in subs:
                dk, dv, dfk = accs[s]
                st = _raw_dot(ks[s], qi, 1, 1, False) + fq - fks[s] - lse
                if diagonal:
                    st = jnp.where(r <= c, st, NEG_BIG)
                pt = jnp.exp(st)
                dv = dv + _raw_dot(pt, doi, 1, 1, False)
                dst = pt * (_raw_dot(vs[s], doi, 1, 0, False) - dl)
                dk = dk + _raw_dot(dst, qi, 1, 0, False)
                dfk = dfk - jnp.sum(dst, axis=1, keepdims=True)
                accs[s] = (dk, dv, dfk)
                dq_s = _raw_dot(dst, ks[s], 0, 0, False)
                dfq_s = jnp.sum(dst, axis=0, keepdims=True)
                dq_i = dq_s if dq_i is None else dq_i + dq_s
                dfq_i = dfq_s if dfq_i is None else dfq_i + dfq_s
            dfq_ref[0, i] += dfq_i
            dq_ref[0, pl.ds(off, blk), :] += dq_i
            return tuple(accs)

        accs = tuple((jnp.zeros((blk, hd), F32), jnp.zeros((blk, hd), F32), jnp.zeros((blk, 1), F32))
                     for _ in range(nsub))
        for d in range(nsub):
            accs = q_step(j * nsub + d, accs, [(s, s == d) for s in range(d + 1)])
        accs = lax.fori_loop((j + 1) * nsub, i_stop,
                             lambda i, a: q_step(i, a, [(s, False) for s in range(nsub)]), accs)
        for s, (dk, dv, dfk) in enumerate(accs):
            dk_ref[0, s * blk:(s + 1) * blk, :] = dk
            dv_ref[0, s * blk:(s + 1) * blk, :] = dv
            dfk_ref[0, s * blk:(s + 1) * blk, :] = dfk

    full = pl.BlockSpec((1, t_total, hd), lambda h, j: (h, 0, 0))
    tile = pl.BlockSpec((1, tkv, hd), lambda h, j: (h, j, 0))
    col = pl.BlockSpec((1, tkv, 1), lambda h, j: (h, j, 0))
    rows = pl.BlockSpec((1, nb, 1, blk), lambda h, j: (h, 0, 0, 0))
    do_blocks = pl.BlockSpec((1, nb, hd, blk), lambda h, j: (h, 0, 0, 0))
    return _call(
        body, name="fox_bwd", grid=(h_total, nb // nsub),
        in_specs=[SMEM, SMEM, SMEM, full, do_blocks, tile, tile, col, rows, rows, rows],
        out_specs=[full, tile, tile, col, rows],
        out_shape=[jax.ShapeDtypeStruct(q.shape, F32), jax.ShapeDtypeStruct(q.shape, F32),
                   jax.ShapeDtypeStruct(q.shape, F32), jax.ShapeDtypeStruct(f_col.shape, F32),
                   jax.ShapeDtypeStruct(f_row.shape, F32)],
        compiler_params=_cparams(("parallel", "arbitrary")),
    )(*tables, q.astype(BF16), do_blk, k, v, f_col, f_row, lse_row, delta_row)


def _attn_delta_call(do_t, o_t):
    h_total, hd, t_total = o_t.shape
    tb = _tile(t_total, 2816, LANE)

    def body(do_ref, o_ref, d_ref):
        d_ref[0] = jnp.sum(do_ref[0] * o_ref[0], axis=0, keepdims=True)

    spec = pl.BlockSpec((1, hd, tb), lambda h, i: (h, 0, i))
    return _call(
        body, name="fox_delta", grid=(h_total, t_total // tb), in_specs=[spec, spec],
        out_specs=pl.BlockSpec((1, 1, tb), lambda h, i: (h, 0, i)),
        out_shape=jax.ShapeDtypeStruct((h_total, 1, t_total), F32),
        compiler_params=_cparams(("parallel", "parallel")),
    )(do_t, o_t)


@jax.custom_vjp
def fox_attention(q, k, v, f_col, f_row):
    return _attn_fwd_call(q, k, v, f_col, f_row, _attn_skip_tables(q, k, f_row))[0]


def _fox_fwd(q, k, v, f_col, f_row):
    tables = _attn_skip_tables(q, k, f_row)
    o_t, lse_row = _attn_fwd_call(q, k, v, f_col, f_row, tables)
    return o_t, (q, k, v, f_col, f_row, tables, o_t, lse_row)


def _fox_bwd(res, do_t):
    q, k, v, f_col, f_row, tables, o_t, lse_row = res
    h_total, t_total, hd = q.shape
    nb, blk = f_row.shape[1], f_row.shape[3]
    delta = _attn_delta_call(do_t, o_t).reshape(f_row.shape)
    do_blk = do_t.reshape(h_total, hd, nb, blk).transpose(0, 2, 1, 3).astype(BF16)
    grads = _attn_bwd_call(q, k, v, f_col, f_row, tables, lse_row, delta, do_blk)
    return tuple(g.astype(p.dtype) for g, p in zip(grads, (q, k, v, f_col, f_row)))


fox_attention.defvjp(_fox_fwd, _fox_bwd)


def _head_col(blk, h):
    lane = lax.broadcasted_iota(jnp.int32, blk.shape, 1)
    return jnp.sum(jnp.where(lane == h, blk, 0.0), axis=1, keepdims=True)


@jax.custom_vjp
def _cat2(a, b):
    return jnp.concatenate([a, b], axis=1)


_cat2.defvjp(lambda a, b: (_cat2(a, b), a.shape[1]), lambda na, ct: (ct[:, :na], ct[:, na:]))


@jax.custom_vjp
def _split2(x):
    half = x.shape[1] // 2
    return x[:, :half], x[:, half:]


_split2.defvjp(lambda x: (_split2(x), None), lambda _, cts: (jnp.concatenate(cts, axis=1),))


def _neumann_solve(m, b):
    x = b - _raw_dot(m, b, 1, 0, False)
    powers, steps = [m], 1
    while 2 * steps < GDN_CHUNK:
        powers.append(_raw_dot(powers[-1], powers[-1], 1, 0, False))
        x = x + _raw_dot(powers[-1], x, 1, 0, False)
        steps *= 2
    return x, powers


@jax.custom_vjp
def _unit_lower_solve(m, b):
    return _neumann_solve(m, b)[0]


def _unit_lower_solve_fwd(m, b):
    x, powers = _neumann_solve(m, b)
    return x, (powers, x)


def _unit_lower_solve_bwd(res, dx):
    powers, x = res
    db = dx - _raw_dot(powers[0], dx, 0, 0, False)
    for p in powers[1:]:
        db = db + _raw_dot(p, db, 0, 0, False)
    return -_raw_dot(db, x, 1, 1, False), db


_unit_lower_solve.defvjp(_unit_lower_solve_fwd, _unit_lower_solve_bwd)


@jax.custom_vjp
def _unit_lower_solve_known(m, b, x):
    return x


def _unit_lower_solve_known_bwd(res, dx):
    m, x = res
    powers, steps = [m], 1
    while 2 * steps < GDN_CHUNK:
        powers.append(_raw_dot(powers[-1], powers[-1], 1, 0, False))
        steps *= 2
    dm, db = _unit_lower_solve_bwd((powers, x), dx)
    return dm, db, jnp.zeros_like(x)


_unit_lower_solve_known.defvjp(lambda m, b, x: (x, (m, x)), _unit_lower_solve_known_bwd)


def _gdn_intra(h, q, k, v, b_blk, g_blk, uw_known=None):
    n = q.shape[0]
    b, g = _head_col(b_blk, h), _head_col(g_blk, h)
    r = lax.broadcasted_iota(jnp.int32, (n, n), 0)
    c = lax.broadcasted_iota(jnp.int32, (n, n), 1)
    same = (r // GDN_CHUNK) == (c // GDN_CHUNK)
    incl = same & (r >= c)
    g_row = jnp.sum(jnp.where(r == c, g, 0.0), axis=0, keepdims=True)
    big_g = jnp.sum(jnp.where(incl, g_row, 0.0), axis=1, keepdims=True)
    big_g_row = jnp.sum(jnp.where(same & (r <= c), g, 0.0), axis=0, keepdims=True)
    g_tot = jnp.sum(jnp.where(same, g_row, 0.0), axis=1, keepdims=True)
    dec = jnp.where(incl, jnp.exp(jnp.where(incl, big_g - big_g_row, 0.0)), 0.0)
    dec_strict = jnp.where(r > c, dec, 0.0)
    e_g = jnp.exp(big_g)
    kb = k * b
    m = _dot(kb, k, 1, 1) * dec_strict
    rs = lax.broadcasted_iota(jnp.int32, (n, GDN_CHUNK), 0)
    cs = lax.broadcasted_iota(jnp.int32, (n, GDN_CHUNK), 1)
    fold = jnp.where(rs % GDN_CHUNK == cs, 1.0, 0.0).astype(F32)
    aqk = _dot(_dot(q, k, 1, 1) * dec, fold, 1, 0, True)
    rhs = _cat2(v * b, kb * e_g)
    u, w = _split2(_unit_lower_solve(m, rhs) if uw_known is None else
                   _unit_lower_solve_known(m, rhs, jnp.concatenate(uw_known, axis=1)))
    lane = lax.broadcasted_iota(jnp.int32, b_blk.shape, 1)
    return u, w, q * e_g, k * jnp.exp(g_tot - big_g), aqk, jnp.where(lane == h, g_tot, 0.0)


def _gdn_rec(h, s, u, w, qg, kd, aqk, gl_blk):
    g_last = jnp.max(_head_col(gl_blk, h), axis=0, keepdims=True)
    big_u = u - _dot(w, s)
    o = _dot(qg, s) + _dot(aqk, big_u)
    s_next = s * jnp.exp(g_last) + _dot(kd, big_u, 0, 0)
    return o, s_next


GDN_TOK_BLK = 256


def _gdn_layout(t_total, rev):
    tb = _tile(t_total, GDN_TOK_BLK, GDN_CHUNK)
    cb, nblk = tb // GDN_CHUNK, t_total // tb
    pos = (lambda i: nblk - 1 - i) if rev else (lambda i: i)
    specs = dict(
        tok=pl.BlockSpec((tb, GDN_W), lambda i: (pos(i), 0)),
        q=pl.BlockSpec((tb, GDN_W), lambda i: (pos(i), 0)),
        k=pl.BlockSpec((tb, GDN_W), lambda i: (pos(i), 1)),
        v=pl.BlockSpec((tb, GDN_W), lambda i: (pos(i), 2)),
        qkv=pl.BlockSpec((tb, 3 * GDN_W), lambda i: (pos(i), 0)),
        gate=pl.BlockSpec((tb, GDN_HEADS), lambda i: (pos(i), 0)),
        aqk=pl.BlockSpec((GDN_HEADS, tb, GDN_CHUNK), lambda i: (0, pos(i), 0)),
        state=pl.BlockSpec((GDN_HEADS, cb, GDN_HD, GDN_HD), lambda i: (0, pos(i), 0, 0)))
    return cb, nblk, specs


def _gdn_shapes(t_total):
    n_chunks = t_total // GDN_CHUNK
    return dict(tok=jax.ShapeDtypeStruct((t_total, GDN_W), BF16),
                out=jax.ShapeDtypeStruct((t_total, GDN_W), F32),
                gate=jax.ShapeDtypeStruct((t_total, GDN_HEADS), F32),
                aqk=jax.ShapeDtypeStruct((GDN_HEADS, t_total, GDN_CHUNK), F32),
                state=jax.ShapeDtypeStruct((GDN_HEADS, n_chunks, GDN_HD, GDN_HD), F32))


def _chunk_rows(ci):
    return pl.ds(pl.multiple_of(ci * GDN_CHUNK, GDN_CHUNK), GDN_CHUNK)


def _head_cols(h):
    return pl.ds(h * GDN_HD, GDN_HD)


def _gdn_intra_fwd_call(qkv, b, g):
    cb, nblk, sp = _gdn_layout(qkv.shape[0], False)
    sh = _gdn_shapes(qkv.shape[0])

    def body(q_ref, k_ref, v_ref, b_ref, g_ref, u_ref, w_ref, qg_ref, kd_ref, aqk_ref, gl_ref):
        b_blk, g_blk = b_ref[...], g_ref[...]
        gl = jnp.zeros(b_blk.shape, F32)
        for h in range(GDN_HEADS):
            cols = _head_cols(h)
            u, w, qg, kd, aqk, gl_h = _gdn_intra(h, q_ref[:, cols], k_ref[:, cols], v_ref[:, cols], b_blk, g_blk)
            u_ref[:, cols] = u.astype(u_ref.dtype)
            w_ref[:, cols] = w.astype(w_ref.dtype)
            qg_ref[:, cols] = qg.astype(qg_ref.dtype)
            kd_ref[:, cols] = kd.astype(kd_ref.dtype)
            aqk_ref[h] = aqk
            gl = gl + gl_h
        gl_ref[...] = gl

    return _call(
        body, name="gdn_intra_fwd", grid=(nblk,),
        in_specs=[sp["q"], sp["k"], sp["v"]] + [sp["gate"]] * 2,
        out_specs=[sp["tok"]] * 4 + [sp["aqk"], sp["gate"]],
        out_shape=[sh["tok"]] * 4 + [sh["aqk"], sh["gate"]],
        compiler_params=_cparams(("parallel",)),
    )(qkv, qkv, qkv, b, g)


def _gdn_intra_bwd_call(qkv, b, g, u, w, du, dw, dqg, dkd, daqk, dgl):
    cb, nblk, sp = _gdn_layout(qkv.shape[0], False)
    sh = _gdn_shapes(qkv.shape[0])

    def body(q_ref, k_ref, v_ref, b_ref, g_ref, u_ref, w_ref, du_ref, dw_ref, dqg_ref, dkd_ref, daqk_ref, dgl_ref,
             dqkv_ref, db_ref, dg_ref):
        b_blk, g_blk, dgl = b_ref[...], g_ref[...], dgl_ref[...]
        db = jnp.zeros(b_blk.shape, F32)
        dg = jnp.zeros(b_blk.shape, F32)
        for h in range(GDN_HEADS):
            cols = _head_cols(h)
            known = (u_ref[:, cols].astype(F32), w_ref[:, cols].astype(F32))
            _, vjp = jax.vjp(functools.partial(_gdn_intra, h, uw_known=known),
                             q_ref[:, cols], k_ref[:, cols], v_ref[:, cols], b_blk, g_blk)
            dq, dk, dv, db_h, dg_h = vjp((*[r[:, cols].astype(F32) for r in (du_ref, dw_ref, dqg_ref, dkd_ref)],
                                          daqk_ref[h], dgl))
            dqkv_ref[:, pl.ds(h * GDN_HD, GDN_HD)] = dq
            dqkv_ref[:, pl.ds(GDN_W + h * GDN_HD, GDN_HD)] = dk
            dqkv_ref[:, pl.ds(2 * GDN_W + h * GDN_HD, GDN_HD)] = dv
            db = db + db_h
            dg = dg + dg_h
        db_ref[...] = db
        dg_ref[...] = dg

    return _call(
        body, name="gdn_intra_bwd", grid=(nblk,),
        in_specs=[sp["q"], sp["k"], sp["v"]] + [sp["gate"]] * 2 + [sp["tok"]] * 6 + [sp["aqk"], sp["gate"]],
        out_specs=[sp["qkv"]] + [sp["gate"]] * 2,
        out_shape=[jax.ShapeDtypeStruct(qkv.shape, F32)] + [sh["gate"]] * 2,
        compiler_params=_cparams(("parallel",)),
    )(qkv, qkv, qkv, b, g, u, w, du, dw, dqg, dkd, daqk, dgl)


def _gdn_rec_fwd_call(u, w, qg, kd, aqk, gl):
    cb, nblk, sp = _gdn_layout(u.shape[0], False)
    sh = _gdn_shapes(u.shape[0])

    def body(u_ref, w_ref, qg_ref, kd_ref, aqk_ref, gl_ref, o_ref, s_all_ref, s_ref):
        @pl.when(pl.program_id(0) == 0)
        def _():
            s_ref[...] = jnp.zeros_like(s_ref)

        def chunk(ci, carry):
            rows = _chunk_rows(ci)
            gl_row = gl_ref[rows, :]
            states = [s_ref[h] for h in range(GDN_HEADS)]
            res = [_gdn_rec(h, states[h], *[r[rows, _head_cols(h)].astype(F32) for r in (u_ref, w_ref, qg_ref, kd_ref)],
                            aqk_ref[h, rows, :], gl_row)
                   for h in range(GDN_HEADS)]
            for h, (o, s_next) in enumerate(res):
                s_all_ref[h, ci] = states[h]
                o_ref[rows, _head_cols(h)] = o
                s_ref[h] = s_next
            return carry

        lax.fori_loop(0, cb, chunk, 0)

    return _call(
        body, name="gdn_rec_fwd", grid=(nblk,),
        in_specs=[sp["tok"]] * 4 + [sp["aqk"], sp["gate"]],
        out_specs=[sp["tok"], sp["state"]], out_shape=[sh["out"], sh["state"]],
        scratch_shapes=[pltpu.VMEM((GDN_HEADS, GDN_HD, GDN_HD), F32)],
        compiler_params=_cparams(("arbitrary",)),
    )(u, w, qg, kd, aqk, gl)


def _gdn_rec_bwd_call(u, w, qg, kd, aqk, gl, s_all, do):
    cb, nblk, sp = _gdn_layout(u.shape[0], True)
    sh = _gdn_shapes(u.shape[0])

    def body(u_ref, w_ref, qg_ref, kd_ref, aqk_ref, gl_ref, s_all_ref, do_ref,
             du_ref, dw_ref, dqg_ref, dkd_ref, daqk_ref, dgl_ref, ds_ref):
        @pl.when(pl.program_id(0) == 0)
        def _():
            ds_ref[...] = jnp.zeros_like(ds_ref)

        def chunk(step, carry):
            ci = cb - 1 - step
            rows = _chunk_rows(ci)
            gl_row = gl_ref[rows, :]
            res = []
            for h in range(GDN_HEADS):
                cols = _head_cols(h)
                _, vjp = jax.vjp(functools.partial(_gdn_rec, h), s_all_ref[h, ci],
                                 *[r[rows, cols].astype(F32) for r in (u_ref, w_ref, qg_ref, kd_ref)],
                                 aqk_ref[h, rows, :], gl_row)
                res.append(vjp((do_ref[rows, cols], ds_ref[h])))
            dgl = jnp.zeros((GDN_CHUNK, GDN_HEADS), F32)
            for h, (ds, du, dw, dqg, dkd, daqk, dgl_h) in enumerate(res):
                cols = _head_cols(h)
                ds_ref[h] = ds
                du_ref[rows, cols] = du.astype(du_ref.dtype)
                dw_ref[rows, cols] = dw.astype(dw_ref.dtype)
                dqg_ref[rows, cols] = dqg.astype(dqg_ref.dtype)
                dkd_ref[rows, cols] = dkd.astype(dkd_ref.dtype)
                daqk_ref[h, rows, :] = daqk
                dgl = dgl + dgl_h
            dgl_ref[rows, :] = dgl
            return carry

        lax.fori_loop(0, cb, chunk, 0)

    return _call(
        body, name="gdn_rec_bwd", grid=(nblk,),
        in_specs=[sp["tok"]] * 4 + [sp["aqk"], sp["gate"], sp["state"], sp["tok"]],
        out_specs=[sp["tok"]] * 4 + [sp["aqk"], sp["gate"]],
        out_shape=[sh["tok"]] * 4 + [sh["aqk"], sh["gate"]],
        scratch_shapes=[pltpu.VMEM((GDN_HEADS, GDN_HD, GDN_HD), F32)],
        compiler_params=_cparams(("arbitrary",)),
    )(u, w, qg, kd, aqk, gl, s_all, do)


@jax.custom_vjp
def gdn_intra(qkv, b, g):
    return tuple(_gdn_intra_fwd_call(qkv, b, g))


def _gdn_intra_fwd(*a):
    outs = gdn_intra(*a)
    return outs, a + (outs[0], outs[1])


gdn_intra.defvjp(_gdn_intra_fwd, lambda res, cts: tuple(_gdn_intra_bwd_call(*res, *cts)))


@jax.custom_vjp
def gdn_rec(u, w, qg, kd, aqk, gl):
    return _gdn_rec_fwd_call(u, w, qg, kd, aqk, gl)[0]


def _gdn_rec_fwd(*a):
    o, s_all = _gdn_rec_fwd_call(*a)
    return o, a + (s_all,)


gdn_rec.defvjp(_gdn_rec_fwd, lambda res, do: tuple(_gdn_rec_bwd_call(*res, do)))


def gated_delta(qkv, b, g):
    return gdn_rec(*gdn_intra(qkv, b, g))


def _loss_call(y, tgt, first, last):
    r_total, d = y.shape
    tm = _tile(r_total, 256, 8)

    def body(y_ref, t_ref, loss_ref, dy_ref):
        i = pl.program_id(0)

        @pl.when(i == 0)
        def _():
            loss_ref[...] = jnp.zeros_like(loss_ref)

        row = lax.broadcasted_iota(jnp.int32, (tm, d), 0) + i * tm
        err = jnp.where((row >= first) & (row < last), y_ref[...] - t_ref[...], 0.0)
        dy_ref[...] = err * (1.0 / d)
        part = jnp.sum(jnp.sum(err * err, axis=1, keepdims=True), axis=0, keepdims=True) * (0.5 / d)
        loss_ref[...] += jnp.broadcast_to(part, loss_ref.shape)

    return _call(
        body, name="loss_head", grid=(r_total // tm,),
        in_specs=[pl.BlockSpec((tm, d), lambda i: (i, 0))] * 2,
        out_specs=[pl.BlockSpec((8, LANE), lambda i: (0, 0)), pl.BlockSpec((tm, d), lambda i: (i, 0))],
        out_shape=[jax.ShapeDtypeStruct((8, LANE), F32), jax.ShapeDtypeStruct(y.shape, F32)],
        compiler_params=_cparams(("arbitrary",)),
    )(y, tgt)


def make_loss(first, last):
    @jax.custom_vjp
    def op(y, tgt):
        return _loss_call(y, tgt, first, last)[0][0, 0]

    def fwd(y, tgt):
        loss, dy = _loss_call(y, tgt, first, last)
        return loss[0, 0], (dy,)

    def bwd(res, ct):
        return res[0] * ct, jnp.zeros_like(res[0])

    op.defvjp(fwd, bwd)
    return op


def _pad_rows8(w):
    return jnp.concatenate([w, jnp.zeros((8 - w.shape[0], w.shape[1]), w.dtype)], axis=0)


def local_loss(wts, x, tgt):
    seq = x.shape[0]
    n_tok = N_META + seq
    t_pad = -(-n_tok // ROW_ALIGN) * ROW_ALIGN
    depth = wts["norm1_g"].shape[0]
    blk = _tile(t_pad, ATT_BLK, LANE)
    nb = t_pad // blk
    tm = _tile(t_pad, 256, 8)

    rms = rowop(_f_rmsnorm, "rmsnorm", (D_MODEL,), tm, out_dtypes=[BF16])
    qnorm = rowop(_f_qnorm, "fox_q_norm", (FOX_HD,), _tile(FOX_HEADS * t_pad, 2048, 8))
    knorm = rowop(_f_rmsnorm, "fox_k_norm", (FOX_HD,), _tile(FOX_HEADS * t_pad, 2048, 8))
    logsig = rowop(_f_logsig, "fox_log_forget", (FOX_HEADS,), tm)
    gdn_act = rowop([_f_gdn_q] * GDN_HEADS + [_f_gdn_k] * GDN_HEADS + [_f_gdn_v] * GDN_HEADS, "gdn_qkv_act",
                    (GDN_HD,), tm)
    gates = rowop(_f_gdn_gates, "gdn_gates", (GDN_HEADS, GDN_HEADS), tm)
    gdn_out = rowop([_f_gdn_out] * GDN_HEADS, "gdn_out_norm", (GDN_HD,), tm, out_dtypes=[BF16])
    merge = rowop(_f_merge, "branch_merge", (D_MODEL,), tm, out_dtypes=[BF16])
    residual = rowop(_f_residual, "residual_add", (D_MODEL,), tm, bc=(2,))
    residual_norm = rowop(_f_residual_norm, "residual_add_norm", (D_MODEL, D_MODEL), tm, bc=(2,),
                          out_dtypes=[F32, BF16])
    keep = (jnp.arange(t_pad)[:, None] < n_tok).astype(F32)
    conv4 = make_dwconv(GDN_CONV, BF16)
    conv3 = make_dwconv(FFN_CONV)
    loss_op = make_loss(N_META, n_tok)

    zeros = jnp.zeros((t_pad - n_tok, D_MODEL), F32)
    h_res = jnp.concatenate([wts["meta_tokens"], x, zeros], axis=0)
    tgt_rows = jnp.concatenate([jnp.zeros((N_META, D_MODEL), F32), tgt, zeros], axis=0)

    def heads(a):
        return a.reshape(t_pad, FOX_HEADS, FOX_HD).transpose(1, 0, 2).reshape(FOX_HEADS * t_pad, FOX_HD)

    h = rms((h_res,), (wts["norm1_g"][0][None],))[0]
    for l in range(depth):
        proj = mm_bf16(h, wts["w_in"][l])
        gate_logits = mm(h, jnp.concatenate([wts["w_in"][l][:, 1536:1536 + LANE],
                                             wts["w_in"][l][:, 4736:4736 + LANE]], axis=1))
        qn = qnorm((heads(proj[:, 0:512]),), (wts["fox_q_norm_g"][l][None],))[0]
        kn = knorm((heads(proj[:, 512:1024]),), (wts["fox_k_norm_g"][l][None],))[0]
        vh = heads(proj[:, 1024:1536])
        log_f = logsig((gate_logits[:, 0:FOX_HEADS],), (wts["fox_f_bias"][l][None],))[0]
        f_cum = cumsum_lanes(log_f.T)
        o_a = fox_attention(qn.reshape(FOX_HEADS, t_pad, FOX_HD), kn.reshape(FOX_HEADS, t_pad, FOX_HD),
                            vh.reshape(FOX_HEADS, t_pad, FOX_HD), f_cum[:, :, None],
                            f_cum.reshape(FOX_HEADS, nb, 1, blk))
        y_a = mm_bf16(o_a.transpose(2, 0, 1).reshape(t_pad, FOX_W).astype(BF16), wts["w_branch_a"][l])
        cv = conv4(proj[:, 1664:4736], _pad_rows8(wts["gdn_conv_w"][l]))
        qkv = gdn_act((cv,), ())[0]
        beta, gdec = gates((gate_logits[:, LANE:LANE + GDN_HEADS], gate_logits[:, LANE + GDN_HEADS:LANE + 2 * GDN_HEADS]),
                           (wts["gdn_a_log"][l][None], wts["gdn_dt_bias"][l][None]))
        o_b = gated_delta(qkv, beta, gdec)
        o_b = gdn_out((o_b, proj[:, 4864:5888]), (wts["gdn_norm_g"][l][None],))[0]
        y_b = mm_bf16(o_b, wts["w_branch_b"][l])
        mixed = merge((proj[:, 5888:6912], proj[:, 6912:7936], y_a, y_b), ())[0]
        h_res, h = residual_norm((h_res, mm(mixed, wts["w_out"][l]), keep), (wts["norm2_g"][l][None],))
        up = conv3(mm_bf16(h, wts["w_up"][l]), _pad_rows8(wts["ffn_conv_w"][l]))
        act = glu(up)
        down = mm(act, wts["w_down"][l])
        if l + 1 < depth:
            h_res, h = residual_norm((h_res, down, keep), (wts["norm1_g"][l + 1][None],))
        else:
            h_res = residual((h_res, down, keep), ())[0]
    return loss_op(h_res, tgt_rows)


def pad_w_in(w):
    parts, pos = [], 0
    for src, width, dst in IN_SEGS:
        if dst > pos:
            parts.append(jnp.zeros(w.shape[:-1] + (dst - pos,), w.dtype))
        parts.append(w[..., src:src + width])
        pos = dst + width
    parts.append(jnp.zeros(w.shape[:-1] + (D_IN_PAD - pos,), w.dtype))
    return jnp.concatenate(parts, axis=-1)


def unpad_w_in(w):
    return jnp.concatenate([w[..., dst:dst + width] for _, width, dst in IN_SEGS], axis=-1)


ANY = pl.BlockSpec(memory_space=pl.ANY)
N_CHIPS = 4
N_DEV = 8
COMM_COLS = 1024
COMM_ROW_ALIGN = 512
COMM_ROW_ALIGN_SMALL = 32


def _place():
    return lax.axis_index("x"), lax.axis_index("y"), lax.axis_index("c")


def _other_chips(x, y):
    return [(1 - x, y), (x, 1 - y), (1 - x, 1 - y)]


def _remote(src, dst, send_sem, recv_sem, dev):
    return pltpu.make_async_remote_copy(src_ref=src, dst_ref=dst, send_sem=send_sem, recv_sem=recv_sem,
                                        device_id=dev, device_id_type=MESH)


def chip_all_gather(buf):
    rows, cols = buf.shape
    half = rows // 2

    def body(x_ref, out_ref, send_sems, recv_sems, pass_send, pass_recv):
        x, y, c = _place()
        me = 2 * x + y
        mine, other = pl.ds(c * half, half), pl.ds((1 - c) * half, half)
        sibling = (x, y, 1 - c)
        chips = _other_chips(x, y)
        started = []
        for k, (px, py) in enumerate(chips):
            cp = _remote(x_ref.at[mine], out_ref.at[me, mine], send_sems.at[k], recv_sems.at[k], (px, py, c))
            cp.start()
            started.append(cp)
        for k, (px, py) in enumerate(chips):
            landed = out_ref.at[2 * px + py, mine]
            _remote(landed, landed, send_sems.at[k], recv_sems.at[k], (px, py, c)).wait_recv()
            cp = _remote(landed, landed, pass_send.at[k], pass_recv.at[k], sibling)
            cp.start()
            started.append(cp)
        for k, (px, py) in enumerate(chips):
            passed = out_ref.at[2 * px + py, other]
            _remote(passed, passed, pass_send.at[k], pass_recv.at[k], sibling).wait_recv()
        for cp in started:
            cp.wait_send()

    got = _call(
        body, name="chip_all_gather", in_specs=[ANY], out_specs=ANY,
        out_shape=jax.ShapeDtypeStruct((N_CHIPS, rows, cols), buf.dtype),
        scratch_shapes=[pltpu.SemaphoreType.DMA((3,)), pltpu.SemaphoreType.DMA((3,)),
                        pltpu.SemaphoreType.DMA((3,)), pltpu.SemaphoreType.DMA((3,))],
    )(buf)
    me = 2 * lax.axis_index("x") + lax.axis_index("y")
    return lax.dynamic_update_slice(got, buf[None], (me, 0, 0))


def sibling_swap_halves(g4):
    n, rows, cols = g4.shape
    half = rows // 2

    def body(g_ref, got_ref, send_sem, recv_sem):
        x, y, c = _place()
        cp = _remote(g_ref.at[:, pl.ds((1 - c) * half, half), :], got_ref, send_sem, recv_sem, (x, y, 1 - c))
        cp.start()
        cp.wait()

    return _call(
        body, name="sibling_swap_halves", in_specs=[ANY], out_specs=ANY,
        out_shape=jax.ShapeDtypeStruct((n, half, cols), g4.dtype),
        scratch_shapes=[pltpu.SemaphoreType.DMA, pltpu.SemaphoreType.DMA],
    )(g4)


def add_own_half(g4, got, c):
    n, rows, cols = g4.shape
    half = rows // 2
    tm = _tile(half, 256, 16)
    nt = half // tm

    def body(c_ref, a_ref, b_ref, o_ref):
        o_ref[...] = (a_ref[...] + b_ref[...]).astype(o_ref.dtype)

    return _call(
        body, name="add_own_half",
        grid_spec=pltpu.PrefetchScalarGridSpec(
            num_scalar_prefetch=1, grid=(n, nt),
            in_specs=[pl.BlockSpec((1, tm, cols), lambda j, i, c_ref: (j, c_ref[0] * nt + i, 0)),
                      pl.BlockSpec((1, tm, cols), lambda j, i, c_ref: (j, i, 0))],
            out_specs=pl.BlockSpec((1, tm, cols), lambda j, i, c_ref: (j, i, 0))),
        out_shape=jax.ShapeDtypeStruct(got.shape, BF16),
        compiler_params=_cparams(("parallel", "parallel")),
    )(c.reshape(1).astype(jnp.int32), g4, got)


def chip_scatter(p4):
    n, rows, cols = p4.shape

    def body(p_ref, out_ref, send_sems, recv_sems):
        x, y, c = _place()
        me = 2 * x + y
        chips = _other_chips(x, y)
        started = []
        for k, (px, py) in enumerate(chips):
            cp = _remote(p_ref.at[2 * px + py], out_ref.at[me], send_sems.at[k], recv_sems.at[k], (px, py, c))
            cp.start()
            started.append(cp)
        for k, (px, py) in enumerate(chips):
            landed = out_ref.at[2 * px + py]
            _remote(landed, landed, send_sems.at[k], recv_sems.at[k], (px, py, c)).wait_recv()
        for cp in started:
            cp.wait_send()

    got = _call(
        body, name="chip_scatter", in_specs=[ANY], out_specs=ANY,
        out_shape=jax.ShapeDtypeStruct(p4.shape, p4.dtype),
        scratch_shapes=[pltpu.SemaphoreType.DMA((3,)), pltpu.SemaphoreType.DMA((3,))],
    )(p4)
    me = 2 * lax.axis_index("x") + lax.axis_index("y")
    return lax.dynamic_update_slice(got, lax.dynamic_slice_in_dim(p4, me, 1, axis=0), (me, 0, 0))


def sum_slots(a):
    n, rows, cols = a.shape
    tm = _tile(rows, 256, 16) if rows % 16 == 0 else rows

    def body(a_ref, o_ref):
        acc = a_ref[0].astype(F32)
        for k in range(1, n):
            acc = acc + a_ref[k].astype(F32)
        o_ref[...] = acc

    return _call(
        body, name="sum_slots_%d" % n, grid=(rows // tm,),
        in_specs=[pl.BlockSpec((n, tm, cols), lambda i: (0, i, 0))],
        out_specs=pl.BlockSpec((tm, cols), lambda i: (i, 0)),
        out_shape=jax.ShapeDtypeStruct((rows, cols), F32),
        compiler_params=_cparams(("parallel",)),
    )(a)


def sibling_join(s):
    half, cols = s.shape

    def body(s_ref, got_ref, send_sem, recv_sem):
        x, y, c = _place()
        cp = _remote(s_ref, got_ref, send_sem, recv_sem, (x, y, 1 - c))
        cp.start()
        cp.wait()

    got = _call(
        body, name="sibling_join", in_specs=[ANY], out_specs=ANY,
        out_shape=jax.ShapeDtypeStruct(s.shape, s.dtype),
        scratch_shapes=[pltpu.SemaphoreType.DMA, pltpu.SemaphoreType.DMA],
    )(s)
    c = lax.axis_index("c")
    out = jnp.zeros((2 * half, cols), s.dtype)
    out = lax.dynamic_update_slice(out, s, (c * half, 0))
    return lax.dynamic_update_slice(out, got, ((1 - c) * half, 0))


def all_devices_gather(buf):
    rows, cols = buf.shape

    def body(b_ref, out_ref, send_sems, recv_sems, local_sem):
        x, y, c = _place()
        me = 4 * x + 2 * y + c
        local = pltpu.make_async_copy(b_ref, out_ref.at[me], local_sem)
        local.start()
        peers = [((x + dx) % 2, (y + dy) % 2, (c + dc) % 2)
                 for dx in (0, 1) for dy in (0, 1) for dc in (0, 1) if dx + dy + dc > 0]
        started = []
        for k, peer in enumerate(peers):
            cp = _remote(b_ref, out_ref.at[me], send_sems.at[k], recv_sems.at[k], peer)
            cp.start()
            started.append(cp)
        for k, (px, py, pc) in enumerate(peers):
            landed = out_ref.at[4 * px + 2 * py + pc]
            _remote(landed, landed, send_sems.at[k], recv_sems.at[k], (px, py, pc)).wait_recv()
        for cp in started:
            cp.wait_send()
        local.wait()

    return _call(
        body, name="all_devices_gather", in_specs=[ANY], out_specs=ANY,
        out_shape=jax.ShapeDtypeStruct((N_DEV, rows, cols), buf.dtype),
        scratch_shapes=[pltpu.SemaphoreType.DMA((7,)), pltpu.SemaphoreType.DMA((7,)), pltpu.SemaphoreType.DMA],
    )(buf)


def adamw(w, g, m, v):
    shape = w.shape
    w2, g2, m2, v2 = [a.reshape(-1, shape[-1]) for a in (w, g, m, v)]
    rows, cols = w2.shape
    tm = _tile(rows, 256, 8) if rows % 8 == 0 else rows

    def body(w_ref, g_ref, m_ref, v_ref, d_ref, nm_ref, nv_ref):
        gv = g_ref[...]
        nm = ADAM_B1 * m_ref[...] + (1.0 - ADAM_B1) * gv
        nv = ADAM_B2 * v_ref[...] + (1.0 - ADAM_B2) * (gv * gv)
        m_hat = nm / (1.0 - ADAM_B1 ** ADAM_STEP)
        v_hat = nv / (1.0 - ADAM_B2 ** ADAM_STEP)
        d_ref[...] = -ADAM_LR * (m_hat / (jnp.sqrt(v_hat) + ADAM_EPS) + ADAM_WD * w_ref[...])
        nm_ref[...] = nm
        nv_ref[...] = nv

    spec = pl.BlockSpec((tm, cols), lambda i: (i, 0))
    outs = _call(
        body, name="adamw", grid=(rows // tm,), in_specs=[spec] * 4, out_specs=[spec] * 3,
        out_shape=[jax.ShapeDtypeStruct((rows, cols), F32)] * 3,
        compiler_params=_cparams(("parallel",)),
    )(w2, g2, m2, v2)
    return [o.reshape(shape) for o in outs]


WEIGHTS = ("meta_tokens", "norm1_g", "w_in", "fox_f_bias", "fox_q_norm_g", "fox_k_norm_g", "gdn_conv_w",
           "gdn_a_log", "gdn_dt_bias", "gdn_norm_g", "w_branch_a", "w_branch_b", "w_out", "norm2_g", "w_up",
           "ffn_conv_w", "w_down")
SHARD_AXIS = {"meta_tokens": -1, "w_in": -1, "gdn_conv_w": -1, "w_branch_a": -1, "w_branch_b": -2, "w_out": -2,
              "w_up": -1, "ffn_conv_w": -1, "w_down": -2}
MATMUL_WEIGHTS = ("w_in", "w_branch_a", "w_branch_b", "w_out", "w_up", "w_down")
SMALL_SHARDED = ("meta_tokens", "gdn_conv_w", "ffn_conv_w")
REPLICATED = tuple(n for n in WEIGHTS if n not in SHARD_AXIS)


def _pack(arrays, dtype, row_align):
    flat = jnp.concatenate([a.reshape(-1).astype(dtype) for a in arrays])
    block = row_align * COMM_COLS
    total = -(-flat.shape[0] // block) * block
    flat = jnp.concatenate([flat, jnp.zeros((total - flat.shape[0],), dtype)])
    return flat.reshape(-1, COMM_COLS)


def _unpack(buf, shapes):
    flat, out, pos = buf.reshape(-1), [], 0
    for shape in shapes:
        size = 1
        for d in shape:
            size *= d
        out.append(flat[pos:pos + size].reshape(shape))
        pos += size
    return out


def _gather_full(shards, names, dtype, row_align):
    got = chip_all_gather(_pack([shards[n] for n in names], dtype, row_align))
    per_chip = [_unpack(got[j], [shards[n].shape for n in names]) for j in range(N_CHIPS)]
    return {n: jnp.concatenate([per_chip[j][i] for j in range(N_CHIPS)], axis=SHARD_AXIS[n]).astype(F32)
            for i, n in enumerate(names)}


def _shard_of(full, name, j):
    axis = SHARD_AXIS[name] % full.ndim
    size = full.shape[axis] // N_CHIPS
    return lax.slice_in_dim(full, j * size, (j + 1) * size, axis=axis)


def kernel(x, meta_tokens, norm1_g, w_in, fox_f_bias, fox_q_norm_g, fox_k_norm_g, gdn_conv_w, gdn_a_log, gdn_dt_bias, gdn_norm_g, w_branch_a, w_branch_b, w_out, norm2_g, w_up, ffn_conv_w, w_down, loss_target, m_meta_tokens, m_norm1_g, m_w_in, m_fox_f_bias, m_fox_q_norm_g, m_fox_k_norm_g, m_gdn_conv_w, m_gdn_a_log, m_gdn_dt_bias, m_gdn_norm_g, m_w_branch_a, m_w_branch_b, m_w_out, m_norm2_g, m_w_up, m_ffn_conv_w, m_w_down, v_meta_tokens, v_norm1_g, v_w_in, v_fox_f_bias, v_fox_q_norm_g, v_fox_k_norm_g, v_gdn_conv_w, v_gdn_a_log, v_gdn_dt_bias, v_gdn_norm_g, v_w_branch_a, v_w_branch_b, v_w_out, v_norm2_g, v_w_up, v_ffn_conv_w, v_w_down):
    w_loc = dict(zip(WEIGHTS, (meta_tokens, norm1_g, w_in, fox_f_bias, fox_q_norm_g, fox_k_norm_g, gdn_conv_w,
                               gdn_a_log, gdn_dt_bias, gdn_norm_g, w_branch_a, w_branch_b, w_out, norm2_g, w_up,
                               ffn_conv_w, w_down)))
    m_loc = dict(zip(WEIGHTS, (m_meta_tokens, m_norm1_g, m_w_in, m_fox_f_bias, m_fox_q_norm_g, m_fox_k_norm_g,
                               m_gdn_conv_w, m_gdn_a_log, m_gdn_dt_bias, m_gdn_norm_g, m_w_branch_a, m_w_branch_b,
                               m_w_out, m_norm2_g, m_w_up, m_ffn_conv_w, m_w_down)))
    v_loc = dict(zip(WEIGHTS, (v_meta_tokens, v_norm1_g, v_w_in, v_fox_f_bias, v_fox_q_norm_g, v_fox_k_norm_g,
                               v_gdn_conv_w, v_gdn_a_log, v_gdn_dt_bias, v_gdn_norm_g, v_w_branch_a, v_w_branch_b,
                               v_w_out, v_norm2_g, v_w_up, v_ffn_conv_w, v_w_down)))
    c = lax.axis_index("c")

    full = {n: w_loc[n] for n in REPLICATED}
    full.update(_gather_full(w_loc, MATMUL_WEIGHTS, BF16, COMM_ROW_ALIGN))
    full.update(_gather_full(w_loc, SMALL_SHARDED, F32, COMM_ROW_ALIGN_SMALL))
    full["w_in"] = pad_w_in(full["w_in"])

    loss, (g_full, g_x) = jax.value_and_grad(local_loss, argnums=(0, 1))(full, x[0], loss_target[0])
    g_full = dict(g_full)
    g_full["w_in"] = unpad_w_in(g_full["w_in"])

    sharded = MATMUL_WEIGHTS + SMALL_SHARDED
    g4 = jnp.stack([_pack([_shard_of(g_full[n], n, j) for n in sharded], F32, COMM_ROW_ALIGN)
                    for j in range(N_CHIPS)])
    pair_sum = add_own_half(g4, sibling_swap_halves(g4), c)
    g_shard = sibling_join(sum_slots(chip_scatter(pair_sum)))
    grads = dict(zip(sharded, _unpack(g_shard, [w_loc[n].shape for n in sharded])))
    g_rep = sum_slots(all_devices_gather(_pack([g_full[n] for n in REPLICATED], F32, 8)))
    grads.update(zip(REPLICATED, _unpack(g_rep, [w_loc[n].shape for n in REPLICATED])))

    loss = lax.psum(loss, ("x", "y", "c"))
    upd = {n: adamw(w_loc[n], grads[n], m_loc[n], v_loc[n]) for n in WEIGHTS}
    return (loss, g_x[None], *[grads[n] for n in WEIGHTS], *[upd[n][0] for n in WEIGHTS],
            *[upd[n][1] for n in WEIGHTS], *[upd[n][2] for n in WEIGHTS])
```

```python
import functools

import jax
import jax.numpy as jnp
from jax import lax
from jax.experimental import pallas as pl
from jax.experimental.pallas import tpu as pltpu

F32 = jnp.float32
BF16 = jnp.bfloat16
HI = lax.Precision.HIGHEST
MESH = pl.DeviceIdType.MESH

D_MODEL = 1024
N_META = 16
EPS = 1e-6
FOX_HEADS, FOX_HD = 8, 64
FOX_W = FOX_HEADS * FOX_HD
GDN_HEADS, GDN_HD, GDN_CHUNK, GDN_CONV = 8, 128, 64, 4
GDN_W = GDN_HEADS * GDN_HD
D_FF = 2816
FFN_CONV = 3
D_IN = 7704
D_IN_PAD = 8192
IN_SEGS = ((0, 1536, 0), (1536, 8, 1536), (1544, 3072, 1664), (4616, 16, 4736), (4632, 1024, 4864), (5656, 2048, 5888))
ROW_ALIGN = 256
ATT_BLK = 256
VMEM_LIMIT = 48 * 1024 * 1024
LANE = 128

ADAM_LR, ADAM_B1, ADAM_B2, ADAM_EPS, ADAM_WD, ADAM_STEP = 0.001, 0.9, 0.999, 1e-08, 0.01, 10


def _call(body, **kw):
    return pl.pallas_call(body, **kw)


def _tile(n, target, mult):
    best, t = None, mult
    while t <= min(n, target):
        if n % t == 0:
            best = t
        t += mult
    assert best is not None, (n, target, mult)
    return best


def _cparams(sem):
    return pltpu.CompilerParams(dimension_semantics=sem, vmem_limit_bytes=VMEM_LIMIT)


def _raw_dot(a, b, ca, cb, precise):
    dims = (((ca,), (cb,)), ((), ()))
    a_hi, b_hi = a.astype(BF16), b.astype(BF16)
    out = lax.dot_general(a_hi, b_hi, dims, preferred_element_type=F32)
    if precise:
        a_lo = (a - a_hi.astype(F32)).astype(BF16)
        b_lo = (b - b_hi.astype(F32)).astype(BF16)
        out = out + (lax.dot_general(a_hi, b_lo, dims, preferred_element_type=F32)
                     + lax.dot_general(a_lo, b_hi, dims, preferred_element_type=F32))
    return out


def _make_dot(ca, cb, precise):
    @jax.custom_vjp
    def f(a, b):
        return _raw_dot(a, b, ca, cb, precise)

    def fwd(a, b):
        return f(a, b), (a, b)

    def bwd(res, ct):
        a, b = res
        if ca == 1:
            da = _raw_dot(ct, b, 1, 1 if cb == 0 else 0, precise)
        else:
            da = _raw_dot(b, ct, 1 if cb == 0 else 0, 1, precise)
        if cb == 0:
            db = _raw_dot(a, ct, 0 if ca == 1 else 1, 0, precise)
        else:
            db = _raw_dot(ct, a, 0, 0 if ca == 1 else 1, precise)
        return da, db

    f.defvjp(fwd, bwd)
    return f


_DOTS = {(ca, cb, p): _make_dot(ca, cb, p) for ca in (0, 1) for cb in (0, 1) for p in (False, True)}


def _dot(a, b, ca=1, cb=0, precise=False):
    return _DOTS[(ca, cb, precise)](a, b)


def _mm_call(a, b, name, ta=False, tb=False, out_dtype=F32):
    k, m = a.shape if ta else a.shape[::-1]
    n, kb = b.shape if tb else b.shape[::-1]
    assert k == kb, (a.shape, b.shape)
    tm = _tile(m, 1408, LANE if ta else 16)
    tn = _tile(n, 1408, LANE)
    tk = _tile(k, 1408, LANE)
    nk = k // tk
    dims = (((0 if ta else 1,), (1 if tb else 0,)), ((), ()))

    def body(a_ref, b_ref, o_ref, *scratch):
        part = lax.dot_general(a_ref[...], b_ref[...], dims, preferred_element_type=F32)
        if nk == 1:
            o_ref[...] = part.astype(o_ref.dtype)
            return
        acc_ref = scratch[0]
        kk = pl.program_id(2)

        @pl.when(kk == 0)
        def _():
            acc_ref[...] = part

        @pl.when(kk > 0)
        def _():
            acc_ref[...] += part

        @pl.when(kk == nk - 1)
        def _():
            o_ref[...] = acc_ref[...].astype(o_ref.dtype)

    return _call(
        body, name=name, grid=(m // tm, n // tn, nk),
        in_specs=[pl.BlockSpec((tk, tm), lambda i, j, kk: (kk, i)) if ta else
                  pl.BlockSpec((tm, tk), lambda i, j, kk: (i, kk)),
                  pl.BlockSpec((tn, tk), lambda i, j, kk: (j, kk)) if tb else
                  pl.BlockSpec((tk, tn), lambda i, j, kk: (kk, j))],
        out_specs=pl.BlockSpec((tm, tn), lambda i, j, kk: (i, j)),
        out_shape=jax.ShapeDtypeStruct((m, n), out_dtype),
        scratch_shapes=[pltpu.VMEM((tm, tn), F32)] if nk > 1 else [],
        compiler_params=_cparams(("parallel", "parallel", "arbitrary")),
    )(a, b)


def _make_mm(out_dtype):
    @jax.custom_vjp
    def op(a, w):
        return _mm_call(a.astype(BF16), w.astype(BF16), "mm_fwd", out_dtype=out_dtype)

    def fwd(a, w):
        a_b, w_b = a.astype(BF16), w.astype(BF16)
        return _mm_call(a_b, w_b.T, "mm_fwd", tb=True, out_dtype=out_dtype), (a_b, w_b, jnp.zeros((), a.dtype))

    def bwd(res, ct):
        a_b, w_b, like_a = res
        ct_b = ct.astype(BF16)
        return (_mm_call(ct_b, w_b, "mm_dx", tb=True, out_dtype=like_a.dtype),
                _mm_call(a_b, ct_b, "mm_dw", ta=True))

    op.defvjp(fwd, bwd)
    return op


mm = _make_mm(F32)
mm_bf16 = _make_mm(BF16)


def _rows_specs(rows, tm, ncb, bc):
    specs = []
    for idx, r in enumerate(rows):
        if idx in bc:
            specs.append(pl.BlockSpec((tm, r.shape[1]), lambda i, j: (i, 0)))
        else:
            specs.append(pl.BlockSpec((tm, r.shape[1] // ncb), lambda i, j: (i, j)))
    return specs


def _param_specs(params):
    return [pl.BlockSpec(p.shape, lambda i, j: (0, 0)) for p in params]


def _group_slices(refs, groups, g, whole):
    out = []
    for idx, r in enumerate(refs):
        w = r.shape[1] // groups
        out.append((r[...] if idx in whole else r[:, g * w:(g + 1) * w]).astype(F32))
    return out


def _rows_fwd_call(fns, rows, params, outs, tm, ncb, bc, name, out_dtypes=None):
    r_total = rows[0].shape[0]
    nr, groups = len(rows), len(fns)
    out_dtypes = out_dtypes or [F32] * len(outs)

    def body(*refs):
        pvals = [r[...] for r in refs[nr:nr + len(params)]]
        for g, fn in enumerate(fns):
            res = fn(*_group_slices(refs[:nr], groups, g, bc), *pvals)
            for o_ref, val in zip(refs[nr + len(params):], res):
                w = o_ref.shape[1] // groups
                o_ref[:, g * w:(g + 1) * w] = val.astype(o_ref.dtype)

    return _call(
        body, name=name, grid=(r_total // tm, ncb),
        in_specs=_rows_specs(rows, tm, ncb, bc) + _param_specs(params),
        out_specs=[pl.BlockSpec((tm, w * groups), lambda i, j: (i, j)) for w in outs],
        out_shape=[jax.ShapeDtypeStruct((r_total, w * groups * ncb), dt) for w, dt in zip(outs, out_dtypes)],
        compiler_params=_cparams(("parallel", "parallel")),
    )(*rows, *params)


def _rows_bwd_call(fns, rows, params, cts, tm, ncb, bc, name):
    r_total = rows[0].shape[0]
    nr, npar, nct, groups = len(rows), len(params), len(cts), len(fns)

    def body(*refs):
        i, j = pl.program_id(0), pl.program_id(1)
        pvals = [r[...] for r in refs[nr:nr + npar]]
        ct_refs = refs[nr + npar:nr + npar + nct]
        d_refs = refs[nr + npar + nct:]
        shared = {idx: None for idx in list(bc) + list(range(nr, nr + npar))}
        for g, fn in enumerate(fns):
            _, vjp = jax.vjp(lambda *a, fn=fn: tuple(fn(*a)), *_group_slices(refs[:nr], groups, g, bc), *pvals)
            grads = vjp(tuple(_group_slices(ct_refs, groups, g, ())))
            for idx in range(nr + npar):
                if idx in shared:
                    shared[idx] = grads[idx] if shared[idx] is None else shared[idx] + grads[idx]
                else:
                    w = d_refs[idx].shape[1] // groups
                    d_refs[idx][:, g * w:(g + 1) * w] = grads[idx].astype(d_refs[idx].dtype)
        for idx, total in shared.items():
            first = (j == 0) if idx < nr else ((i == 0) & (j == 0))

            @pl.when(first)
            def _(idx=idx):
                d_refs[idx][...] = jnp.zeros_like(d_refs[idx])
            d_refs[idx][...] += total

    ct_specs = [pl.BlockSpec((tm, c.shape[1] // ncb), lambda i, j: (i, j)) for c in cts]
    return _call(
        body, name=name + "_bwd", grid=(r_total // tm, ncb),
        in_specs=_rows_specs(rows, tm, ncb, bc) + _param_specs(params) + ct_specs,
        out_specs=_rows_specs(rows, tm, ncb, bc) + _param_specs(params),
        out_shape=[jax.ShapeDtypeStruct(a.shape, a.dtype) for a in list(rows) + list(params)],
        compiler_params=_cparams(("arbitrary", "arbitrary")),
    )(*rows, *params, *cts)


def rowop(fn, name, outs, tm, ncb=1, bc=(), out_dtypes=None):
    fns = list(fn) if isinstance(fn, (list, tuple)) else [fn]

    @jax.custom_vjp
    def op(rows, params):
        return tuple(_rows_fwd_call(fns, rows, params, outs, tm, ncb, bc, name, out_dtypes))

    def fwd(rows, params):
        return op(rows, params), (rows, params)

    def bwd(res, cts):
        rows, params = res
        d = _rows_bwd_call(fns, rows, params, cts, tm, ncb, bc, name)
        return tuple(d[:len(rows)]), tuple(d[len(rows):])

    op.defvjp(fwd, bwd)
    return op


def _sigmoid(x):
    return 1.0 / (1.0 + jnp.exp(-x))


def _silu(x):
    return x * _sigmoid(x)


def _softplus(x):
    return jnp.maximum(x, 0.0) + jnp.log(1.0 + jnp.exp(-jnp.abs(x)))


def _f_rmsnorm(x, g):
    return (x * lax.rsqrt(jnp.mean(x * x, axis=-1, keepdims=True) + EPS) * g,)


def _f_qnorm(x, g):
    return (x * lax.rsqrt(jnp.mean(x * x, axis=-1, keepdims=True) + EPS) * (g * (FOX_HD ** -0.5)),)


def _f_logsig(x, b):
    return (-_softplus(-(x + b)),)


def _f_gdn_q(x):
    y = _silu(x)
    return (y * lax.rsqrt(jnp.sum(y * y, axis=-1, keepdims=True) + EPS) * (GDN_HD ** -0.5),)


def _f_gdn_k(x):
    y = _silu(x)
    return (y * lax.rsqrt(jnp.sum(y * y, axis=-1, keepdims=True) + EPS),)


def _f_gdn_v(x):
    return (_silu(x),)


def _f_gdn_gates(bl, al, a_log, dt_bias):
    return _sigmoid(bl), -jnp.exp(a_log) * _softplus(al + dt_bias)


def _f_gdn_out(o, z, g):
    return (o * lax.rsqrt(jnp.mean(o * o, axis=-1, keepdims=True) + EPS) * g * _silu(z),)


def _f_merge(g0, g1, ya, yb):
    return (_sigmoid(g0) * ya + _sigmoid(g1) * yb,)


def _f_residual(a, b, keep):
    return ((a + b) * keep,)


def _f_residual_norm(a, b, keep, g):
    r = (a + b) * keep
    return r, _f_rmsnorm(r, g)[0]


def _f_glu(a, b):
    return (_silu(a) * b,)


def _glu_call(up, ct):
    t_total, two_f = up.shape
    f = two_f // 2
    tm = _tile(t_total, 128, 16)
    wc = _tile(f, 1408, LANE)

    def body(*refs):
        up_ref, out_ref = refs[0], refs[-1]
        for c0 in range(0, f, wc):
            a, b = up_ref[:, c0:c0 + wc].astype(F32), up_ref[:, f + c0:f + c0 + wc].astype(F32)
            if ct is None:
                out_ref[:, c0:c0 + wc] = _f_glu(a, b)[0].astype(out_ref.dtype)
            else:
                _, vjp = jax.vjp(_f_glu, a, b)
                da, db = vjp((refs[1][:, c0:c0 + wc].astype(F32),))
                out_ref[:, c0:c0 + wc] = da.astype(out_ref.dtype)
                out_ref[:, f + c0:f + c0 + wc] = db.astype(out_ref.dtype)

    wide = pl.BlockSpec((tm, two_f), lambda i: (i, 0))
    narrow = pl.BlockSpec((tm, f), lambda i: (i, 0))
    return _call(
        body, name="ffn_glu" if ct is None else "ffn_glu_bwd", grid=(t_total // tm,),
        in_specs=[wide] if ct is None else [wide, narrow], out_specs=narrow if ct is None else wide,
        out_shape=jax.ShapeDtypeStruct((t_total, f), BF16) if ct is None else jax.ShapeDtypeStruct(up.shape, up.dtype),
        compiler_params=_cparams(("parallel",)),
    )(*((up,) if ct is None else (up, ct)))


@jax.custom_vjp
def glu(up):
    return _glu_call(up, None)


glu.defvjp(lambda up: (glu(up), up), lambda up, ct: (_glu_call(up, ct),))


def _shift_down(x, halo, s, row8):
    rx = pltpu.roll(x, s, 0)
    top = jnp.where(row8 < s, pltpu.roll(halo, s, 0), rx[:8])
    return jnp.concatenate([top, rx[8:]], axis=0)


def _shift_up(x, nxt, s, row8):
    tm = x.shape[0]
    rx = pltpu.roll(x, tm - s, 0)
    bot = jnp.where(row8 >= 8 - s, pltpu.roll(nxt, 8 - s, 0), rx[tm - 8:])
    return jnp.concatenate([rx[:tm - 8], bot], axis=0)


def _conv_tiles(r_total, c_total):
    return _tile(r_total, 768, 8), _tile(c_total, 1408, LANE)


def _halo_rows(dtype):
    return 16 if dtype == BF16 else 8


def _conv_fwd_call(x, w8, k_taps, out_dtype):
    r_total, c_total = x.shape
    tm, tc = _conv_tiles(r_total, c_total)
    hr = _halo_rows(x.dtype)
    hb = tm // hr

    def body(x_ref, halo_ref, w_ref, y_ref):
        i = pl.program_id(1)
        xt = x_ref[...].astype(F32)
        halo = jnp.where(i > 0, halo_ref[...].astype(F32)[hr - 8:hr], 0.0)
        row8 = lax.broadcasted_iota(jnp.int32, (8, tc), 0)
        acc = w_ref[k_taps - 1:k_taps, :] * xt
        for k in range(k_taps - 1):
            acc += w_ref[k:k + 1, :] * _shift_down(xt, halo, k_taps - 1 - k, row8)
        y_ref[...] = acc.astype(y_ref.dtype)

    return _call(
        body, name="dwconv_fwd", grid=(c_total // tc, r_total // tm),
        in_specs=[pl.BlockSpec((tm, tc), lambda c, i: (i, c)),
                  pl.BlockSpec((hr, tc), lambda c, i: (jnp.maximum(i * hb - 1, 0), c)),
                  pl.BlockSpec((8, tc), lambda c, i: (0, c))],
        out_specs=pl.BlockSpec((tm, tc), lambda c, i: (i, c)),
        out_shape=jax.ShapeDtypeStruct(x.shape, out_dtype),
        compiler_params=_cparams(("parallel", "parallel")),
    )(x, x, w8)


def _conv_bwd_call(x, w8, dy, k_taps):
    r_total, c_total = x.shape
    tm, tc = _conv_tiles(r_total, c_total)
    hr = _halo_rows(dy.dtype)
    hb = tm // hr
    n_i = r_total // tm

    def body(x_ref, w_ref, dy_ref, nxt_ref, dx_ref, dw_ref):
        i = pl.program_id(1)
        xt, dyt = x_ref[...].astype(F32), dy_ref[...].astype(F32)
        nxt = jnp.where(i < n_i - 1, nxt_ref[...].astype(F32)[0:8], 0.0)
        row8 = lax.broadcasted_iota(jnp.int32, (8, tc), 0)
        dx = w_ref[k_taps - 1:k_taps, :] * dyt
        upd = jnp.where(row8 == k_taps - 1, jnp.sum(dyt * xt, axis=0, keepdims=True), 0.0)
        for k in range(k_taps - 1):
            dy_ahead = _shift_up(dyt, nxt, k_taps - 1 - k, row8)
            dx += w_ref[k:k + 1, :] * dy_ahead
            upd = jnp.where(row8 == k, jnp.sum(dy_ahead * xt, axis=0, keepdims=True), upd)
        dx_ref[...] = dx.astype(dx_ref.dtype)

        @pl.when(i == 0)
        def _():
            dw_ref[...] = jnp.zeros_like(dw_ref)

        dw_ref[...] += upd

    return _call(
        body, name="dwconv_bwd", grid=(c_total // tc, n_i),
        in_specs=[pl.BlockSpec((tm, tc), lambda c, i: (i, c)),
                  pl.BlockSpec((8, tc), lambda c, i: (0, c)),
                  pl.BlockSpec((tm, tc), lambda c, i: (i, c)),
                  pl.BlockSpec((hr, tc), lambda c, i: (jnp.minimum((i + 1) * hb, r_total // hr - 1), c))],
        out_specs=[pl.BlockSpec((tm, tc), lambda c, i: (i, c)), pl.BlockSpec((8, tc), lambda c, i: (0, c))],
        out_shape=[jax.ShapeDtypeStruct(x.shape, x.dtype), jax.ShapeDtypeStruct(w8.shape, F32)],
        compiler_params=_cparams(("parallel", "arbitrary")),
    )(x, w8, dy, dy)


def make_dwconv(k_taps, out_dtype=None):
    @jax.custom_vjp
    def op(x, w8):
        return _conv_fwd_call(x, w8, k_taps, out_dtype or x.dtype)

    def fwd(x, w8):
        return op(x, w8), (x, w8)

    def bwd(res, dy):
        x, w8 = res
        dx, dw = _conv_bwd_call(x, w8, dy, k_taps)
        return dx, dw

    op.defvjp(fwd, bwd)
    return op


def _cumsum_call(x, reverse):
    h, t_total = x.shape
    tb = _tile(t_total, 256, LANE)
    nb = t_total // tb

    def body(x_ref, o_ref, carry_ref):
        i = pl.program_id(0)

        @pl.when(i == 0)
        def _():
            carry_ref[...] = jnp.zeros_like(carry_ref)

        r = lax.broadcasted_iota(jnp.int32, (tb, tb), 0)
        c = lax.broadcasted_iota(jnp.int32, (tb, tb), 1)
        tri = jnp.where((r >= c) if reverse else (r <= c), 1.0, 0.0).astype(F32)
        xv = x_ref[...]
        carry = jnp.max(carry_ref[...], axis=1, keepdims=True)
        o_ref[...] = _raw_dot(xv, tri, 1, 0, True) + carry
        carry_ref[...] = jnp.broadcast_to(carry + jnp.sum(xv, axis=1, keepdims=True), carry_ref.shape)

    imap = (lambda i: (0, nb - 1 - i)) if reverse else (lambda i: (0, i))
    return _call(
        body, name="cumsum_rev" if reverse else "cumsum", grid=(nb,),
        in_specs=[pl.BlockSpec((h, tb), imap)], out_specs=pl.BlockSpec((h, tb), imap),
        out_shape=jax.ShapeDtypeStruct(x.shape, F32), scratch_shapes=[pltpu.VMEM((h, LANE), F32)],
        compiler_params=_cparams(("arbitrary",)),
    )(x)


@jax.custom_vjp
def cumsum_lanes(x):
    return _cumsum_call(x, False)


cumsum_lanes.defvjp(lambda x: (cumsum_lanes(x), None), lambda _, ct: (_cumsum_call(ct, True),))


NEG_BIG = -1e30


def _attn_sub_tiles(nb):
    return max(s for s in (3, 2, 1) if nb % s == 0)


EXP_ZERO = -92.0
SMEM = pl.BlockSpec(memory_space=pltpu.SMEM)


def _max_row_norm_sq(x):
    h_total, t_total, hd = x.shape
    tb = _tile(t_total, 2816, 8)

    def body(x_ref, o_ref):
        @pl.when(pl.program_id(1) == 0)
        def _():
            o_ref[...] = jnp.zeros_like(o_ref)

        xv = x_ref[0]
        top = jnp.max(jnp.sum(xv * xv, axis=1, keepdims=True), axis=0, keepdims=True)
        o_ref[0] = jnp.maximum(o_ref[0], top)

    return _call(
        body, name="max_row_norm", grid=(h_total, t_total // tb),
        in_specs=[pl.BlockSpec((1, tb, hd), lambda h, i: (h, i, 0))],
        out_specs=pl.BlockSpec((1, 8, LANE), lambda h, i: (h, 0, 0)),
        out_shape=jax.ShapeDtypeStruct((h_total, 8, LANE), F32),
        compiler_params=_cparams(("parallel", "arbitrary")),
    )(x)


def _attn_skip_tables(q, k, f_row):
    bound = 2.0 * jnp.sqrt(_max_row_norm_sq(q)[:, 0, :1] * _max_row_norm_sq(k)[:, 0, :1])
    return EXP_ZERO - bound, f_row[:, :, 0, 0], f_row[:, :, 0, -1]


def _attn_fwd_call(q, k, v, f_col, f_row, tables):
    h_total, t_total, hd = q.shape
    blk = f_row.shape[-1]
    nb = t_total // blk
    nsub = _attn_sub_tiles(nb)
    tq = nsub * blk

    def body(thr_ref, first_ref, last_ref, q_ref, k_ref, vt_ref, fc_ref, fr_ref, o_ref, lse_ref):
        h = pl.program_id(0)
        i = pl.program_id(1)
        gap_needed = thr_ref[h, 0]
        f_tile = first_ref[h, i * nsub]
        j_start = lax.while_loop(lambda j: (j < i * nsub) & (f_tile - last_ref[h, j] < gap_needed),
                                 lambda j: j + 1, 0)
        r = lax.broadcasted_iota(jnp.int32, (blk, blk), 0)
        c = lax.broadcasted_iota(jnp.int32, (blk, blk), 1)
        qs = [q_ref[0, s * blk:(s + 1) * blk, :].astype(BF16) for s in range(nsub)]
        fqs = [fr_ref[0, i * nsub + s] for s in range(nsub)]

        def load_kv(j):
            off = pl.multiple_of(j * blk, blk)
            return k_ref[0, pl.ds(off, blk), :], vt_ref[0, j], fc_ref[0, pl.ds(off, blk), :]

        def tile(kv, s, carry, diagonal):
            kj, vtj, fk = kv
            m, l, acc = carry
            st = _raw_dot(kj, qs[s], 1, 1, False) + fqs[s] - fk
            if diagonal:
                st = jnp.where(r <= c, st, NEG_BIG)
            m_new = jnp.maximum(m, jnp.max(st, axis=0, keepdims=True))
            p = jnp.exp(st - m_new)
            alpha = jnp.exp(m - m_new)
            l = alpha * l + jnp.sum(p, axis=0, keepdims=True)
            acc = alpha * acc + _raw_dot(vtj, p, 1, 0, False)
            return m_new, l, acc

        def below_diagonal(j, carry):
            kv = load_kv(j)
            return tuple(tile(kv, s, carry[s], False) for s in range(nsub))

        init = tuple((jnp.full((1, blk), NEG_BIG, F32), jnp.zeros((1, blk), F32), jnp.zeros((hd, blk), F32))
                     for _ in range(nsub))
        carry = list(lax.fori_loop(j_start, i * nsub, below_diagonal, init))
        for d in range(nsub):
            kv = load_kv(i * nsub + d)
            for s in range(d, nsub):
                carry[s] = tile(kv, s, carry[s], s == d)
        for s, (m, l, acc) in enumerate(carry):
            o_ref[0, :, s * blk:(s + 1) * blk] = acc / l
            lse_ref[0, s] = m + jnp.log(l)

    vt = v.reshape(h_total, nb, blk, hd).transpose(0, 1, 3, 2).astype(BF16)
    return _call(
        body, name="fox_fwd", grid=(h_total, nb // nsub),
        in_specs=[SMEM, SMEM, SMEM,
                  pl.BlockSpec((1, tq, hd), lambda h, i: (h, i, 0)),
                  pl.BlockSpec((1, t_total, hd), lambda h, i: (h, 0, 0)),
                  pl.BlockSpec((1, nb, hd, blk), lambda h, i: (h, 0, 0, 0)),
                  pl.BlockSpec((1, t_total, 1), lambda h, i: (h, 0, 0)),
                  pl.BlockSpec((1, nb, 1, blk), lambda h, i: (h, 0, 0, 0))],
        out_specs=[pl.BlockSpec((1, hd, tq), lambda h, i: (h, 0, i)),
                   pl.BlockSpec((1, nsub, 1, blk), lambda h, i: (h, i, 0, 0))],
        out_shape=[jax.ShapeDtypeStruct((h_total, hd, t_total), F32), jax.ShapeDtypeStruct(f_row.shape, F32)],
        compiler_params=_cparams(("parallel", "parallel")),
    )(*tables, q, k.astype(BF16), vt, f_col, f_row)


def _attn_bwd_call(q, k, v, f_col, f_row, tables, lse_row, delta_row, do_blk):
    h_total, t_total, hd = q.shape
    blk = f_row.shape[-1]
    nb = t_total // blk
    nsub = _attn_sub_tiles(nb)
    tkv = nsub * blk

    def body(thr_ref, first_ref, last_ref, q_ref, do_ref, k_ref, v_ref, fc_ref, fr_ref, lse_ref, dl_ref,
             dq_ref, dk_ref, dv_ref, dfk_ref, dfq_ref):
        h = pl.program_id(0)
        j = pl.program_id(1)
        gap_needed = thr_ref[h, 0]
        f_tile = last_ref[h, j * nsub + nsub - 1]
        i_stop = lax.while_loop(lambda i: (i < nb) & (first_ref[h, jnp.minimum(i, nb - 1)] - f_tile >= gap_needed),
                                lambda i: i + 1, (j + 1) * nsub)

        @pl.when(j == 0)
        def _():
            dq_ref[...] = jnp.zeros_like(dq_ref)
            dfq_ref[...] = jnp.zeros_like(dfq_ref)

        ks = [k_ref[0, s * blk:(s + 1) * blk, :].astype(BF16) for s in range(nsub)]
        vs = [v_ref[0, s * blk:(s + 1) * blk, :].astype(BF16) for s in range(nsub)]
        fks = [fc_ref[0, s * blk:(s + 1) * blk, :] for s in range(nsub)]
        r = lax.broadcasted_iota(jnp.int32, (blk, blk), 0)
        c = lax.broadcasted_iota(jnp.int32, (blk, blk), 1)

        def q_step(i, accs, subs):
            off = pl.multiple_of(i * blk, blk)
            qi = q_ref[0, pl.ds(off, blk), :]
            doi = do_ref[0, i]
            fq, lse, dl = fr_ref[0, i], lse_ref[0, i], dl_ref[0, i]
            accs = list(accs)
            dq_i, dfq_i = None, None
            for s, diagonal in subs:
                dk, dv, dfk = accs[s]
                st = _raw_dot(ks[s], qi, 1, 1, False) + fq - fks[s] - lse
                if diagonal:
                    st = jnp.where(r <= c, st, NEG_BIG)
                pt = jnp.exp(st)
                dv = dv + _raw_dot(pt, doi, 1, 1, False)
                dst = pt * (_raw_dot(vs[s], doi, 1, 0, False) - dl)
                dk = dk + _raw_dot(dst, qi, 1, 0, False)
                dfk = dfk - jnp.sum(dst, axis=1, keepdims=True)
                accs[s] = (dk, dv, dfk)
                dq_s = _raw_dot(dst, ks[s], 0, 0, False)
                dfq_s = jnp.sum(dst, axis=0, keepdims=True)
                dq_i = dq_s if dq_i is None else dq_i + dq_s
                dfq_i = dfq_s if dfq_i is None else dfq_i + dfq_s
            dfq_ref[0, i] += dfq_i
            dq_ref[0, pl.ds(off, blk), :] += dq_i
            return tuple(accs)

        accs = tuple((jnp.zeros((blk, hd), F32), jnp.zeros((blk, hd), F32), jnp.zeros((blk, 1), F32))
                     for _ in range(nsub))
        for d in range(nsub):
            accs = q_step(j * nsub + d, accs, [(s, s == d) for s in range(d + 1)])
        accs = lax.fori_loop((j + 1) * nsub, i_stop,
                             lambda i, a: q_step(i, a, [(s, False) for s in range(nsub)]), accs)
        for s, (dk, dv, dfk) in enumerate(accs):
            dk_ref[0, s * blk:(s + 1) * blk, :] = dk
            dv_ref[0, s * blk:(s + 1) * blk, :] = dv
            dfk_ref[0, s * blk:(s + 1) * blk, :] = dfk

    full = pl.BlockSpec((1, t_total, hd), lambda h, j: (h, 0, 0))
    tile = pl.BlockSpec((1, tkv, hd), lambda h, j: (h, j, 0))
    col = pl.BlockSpec((1, tkv, 1), lambda h, j: (h, j, 0))
    rows = pl.BlockSpec((1, nb, 1, blk), lambda h, j: (h, 0, 0, 0))
    do_blocks = pl.BlockSpec((1, nb, hd, blk), lambda h, j: (h, 0, 0, 0))
    return _call(
        body, name="fox_bwd", grid=(h_total, nb // nsub),
        in_specs=[SMEM, SMEM, SMEM, full, do_blocks, tile, tile, col, rows, rows, rows],
        out_specs=[full, tile, tile, col, rows],
        out_shape=[jax.ShapeDtypeStruct(q.shape, F32), jax.ShapeDtypeStruct(q.shape, F32),
                   jax.ShapeDtypeStruct(q.shape, F32), jax.ShapeDtypeStruct(f_col.shape, F32),
                   jax.ShapeDtypeStruct(f_row.shape, F32)],
        compiler_params=_cparams(("parallel", "arbitrary")),
    )(*tables, q.astype(BF16), do_blk, k, v, f_col, f_row, lse_row, delta_row)


def _attn_delta_call(do_t, o_t):
    h_total, hd, t_total = o_t.shape
    tb = _tile(t_total, 2816, LANE)

    def body(do_ref, o_ref, d_ref):
        d_ref[0] = jnp.sum(do_ref[0] * o_ref[0], axis=0, keepdims=True)

    spec = pl.BlockSpec((1, hd, tb), lambda h, i: (h, 0, i))
    return _call(
        body, name="fox_delta", grid=(h_total, t_total // tb), in_specs=[spec, spec],
        out_specs=pl.BlockSpec((1, 1, tb), lambda h, i: (h, 0, i)),
        out_shape=jax.ShapeDtypeStruct((h_total, 1, t_total), F32),
        compiler_params=_cparams(("parallel", "parallel")),
    )(do_t, o_t)


@jax.custom_vjp
def fox_attention(q, k, v, f_col, f_row):
    return _attn_fwd_call(q, k, v, f_col, f_row, _attn_skip_tables(q, k, f_row))[0]


def _fox_fwd(q, k, v, f_col, f_row):
    tables = _attn_skip_tables(q, k, f_row)
    o_t, lse_row = _attn_fwd_call(q, k, v, f_col, f_row, tables)
    return o_t, (q, k, v, f_col, f_row, tables, o_t, lse_row)


def _fox_bwd(res, do_t):
    q, k, v, f_col, f_row, tables, o_t, lse_row = res
    h_total, t_total, hd = q.shape
    nb, blk = f_row.shape[1], f_row.shape[3]
    delta = _attn_delta_call(do_t, o_t).reshape(f_row.shape)
    do_blk = do_t.reshape(h_total, hd, nb, blk).transpose(0, 2, 1, 3).astype(BF16)
    grads = _attn_bwd_call(q, k, v, f_col, f_row, tables, lse_row, delta, do_blk)
    return tuple(g.astype(p.dtype) for g, p in zip(grads, (q, k, v, f_col, f_row)))


fox_attention.defvjp(_fox_fwd, _fox_bwd)


def _head_col(blk, h):
    lane = lax.broadcasted_iota(jnp.int32, blk.shape, 1)
    return jnp.sum(jnp.where(lane == h, blk, 0.0), axis=1, keepdims=True)


@jax.custom_vjp
def _cat2(a, b):
    return jnp.concatenate([a, b], axis=1)


_cat2.defvjp(lambda a, b: (_cat2(a, b), a.shape[1]), lambda na, ct: (ct[:, :na], ct[:, na:]))


@jax.custom_vjp
def _split2(x):
    half = x.shape[1] // 2
    return x[:, :half], x[:, half:]


_split2.defvjp(lambda x: (_split2(x), None), lambda _, cts: (jnp.concatenate(cts, axis=1),))


def _neumann_solve(m, b):
    x = b - _raw_dot(m, b, 1, 0, False)
    powers, steps = [m], 1
    while 2 * steps < GDN_CHUNK:
        powers.append(_raw_dot(powers[-1], powers[-1], 1, 0, False))
        x = x + _raw_dot(powers[-1], x, 1, 0, False)
        steps *= 2
    return x, powers


@jax.custom_vjp
def _unit_lower_solve(m, b):
    return _neumann_solve(m, b)[0]


def _unit_lower_solve_fwd(m, b):
    x, powers = _neumann_solve(m, b)
    return x, (powers, x)


def _unit_lower_solve_bwd(res, dx):
    powers, x = res
    db = dx - _raw_dot(powers[0], dx, 0, 0, False)
    for p in powers[1:]:
        db = db + _raw_dot(p, db, 0, 0, False)
    return -_raw_dot(db, x, 1, 1, False), db


_unit_lower_solve.defvjp(_unit_lower_solve_fwd, _unit_lower_solve_bwd)


@jax.custom_vjp
def _unit_lower_solve_known(m, b, x):
    return x


def _unit_lower_solve_known_bwd(res, dx):
    m, x = res
    powers, steps = [m], 1
    while 2 * steps < GDN_CHUNK:
        powers.append(_raw_dot(powers[-1], powers[-1], 1, 0, False))
        steps *= 2
    dm, db = _unit_lower_solve_bwd((powers, x), dx)
    return dm, db, jnp.zeros_like(x)


_unit_lower_solve_known.defvjp(lambda m, b, x: (x, (m, x)), _unit_lower_solve_known_bwd)


def _gdn_intra(h, q, k, v, b_blk, g_blk, uw_known=None):
    n = q.shape[0]
    b, g = _head_col(b_blk, h), _head_col(g_blk, h)
    r = lax.broadcasted_iota(jnp.int32, (n, n), 0)
    c = lax.broadcasted_iota(jnp.int32, (n, n), 1)
    same = (r // GDN_CHUNK) == (c // GDN_CHUNK)
    incl = same & (r >= c)
    g_row = jnp.sum(jnp.where(r == c, g, 0.0), axis=0, keepdims=True)
    big_g = jnp.sum(jnp.where(incl, g_row, 0.0), axis=1, keepdims=True)
    big_g_row = jnp.sum(jnp.where(same & (r <= c), g, 0.0), axis=0, keepdims=True)
    g_tot = jnp.sum(jnp.where(same, g_row, 0.0), axis=1, keepdims=True)
    dec = jnp.where(incl, jnp.exp(jnp.where(incl, big_g - big_g_row, 0.0)), 0.0)
    dec_strict = jnp.where(r > c, dec, 0.0)
    e_g = jnp.exp(big_g)
    kb = k * b
    m = _dot(kb, k, 1, 1) * dec_strict
    rs = lax.broadcasted_iota(jnp.int32, (n, GDN_CHUNK), 0)
    cs = lax.broadcasted_iota(jnp.int32, (n, GDN_CHUNK), 1)
    fold = jnp.where(rs % GDN_CHUNK == cs, 1.0, 0.0).astype(F32)
    aqk = _dot(_dot(q, k, 1, 1) * dec, fold, 1, 0, True)
    rhs = _cat2(v * b, kb * e_g)
    u, w = _split2(_unit_lower_solve(m, rhs) if uw_known is None else
                   _unit_lower_solve_known(m, rhs, jnp.concatenate(uw_known, axis=1)))
    lane = lax.broadcasted_iota(jnp.int32, b_blk.shape, 1)
    return u, w, q * e_g, k * jnp.exp(g_tot - big_g), aqk, jnp.where(lane == h, g_tot, 0.0)


def _gdn_rec(h, s, u, w, qg, kd, aqk, gl_blk):
    g_last = jnp.max(_head_col(gl_blk, h), axis=0, keepdims=True)
    big_u = u - _dot(w, s)
    o = _dot(qg, s) + _dot(aqk, big_u)
    s_next = s * jnp.exp(g_last) + _dot(kd, big_u, 0, 0)
    return o, s_next


GDN_TOK_BLK = 256


def _gdn_layout(t_total, rev):
    tb = _tile(t_total, GDN_TOK_BLK, GDN_CHUNK)
    cb, nblk = tb // GDN_CHUNK, t_total // tb
    pos = (lambda i: nblk - 1 - i) if rev else (lambda i: i)
    specs = dict(
        tok=pl.BlockSpec((tb, GDN_W), lambda i: (pos(i), 0)),
        q=pl.BlockSpec((tb, GDN_W), lambda i: (pos(i), 0)),
        k=pl.BlockSpec((tb, GDN_W), lambda i: (pos(i), 1)),
        v=pl.BlockSpec((tb, GDN_W), lambda i: (pos(i), 2)),
        qkv=pl.BlockSpec((tb, 3 * GDN_W), lambda i: (pos(i), 0)),
        gate=pl.BlockSpec((tb, GDN_HEADS), lambda i: (pos(i), 0)),
        aqk=pl.BlockSpec((GDN_HEADS, tb, GDN_CHUNK), lambda i: (0, pos(i), 0)),
        state=pl.BlockSpec((GDN_HEADS, cb, GDN_HD, GDN_HD), lambda i: (0, pos(i), 0, 0)))
    return cb, nblk, specs


def _gdn_shapes(t_total):
    n_chunks = t_total // GDN_CHUNK
    return dict(tok=jax.ShapeDtypeStruct((t_total, GDN_W), BF16),
                out=jax.ShapeDtypeStruct((t_total, GDN_W), F32),
                gate=jax.ShapeDtypeStruct((t_total, GDN_HEADS), F32),
                aqk=jax.ShapeDtypeStruct((GDN_HEADS, t_total, GDN_CHUNK), F32),
                state=jax.ShapeDtypeStruct((GDN_HEADS, n_chunks, GDN_HD, GDN_HD), F32))


def _chunk_rows(ci):
    return pl.ds(pl.multiple_of(ci * GDN_CHUNK, GDN_CHUNK), GDN_CHUNK)


def _head_cols(h):
    return pl.ds(h * GDN_HD, GDN_HD)


def _gdn_intra_fwd_call(qkv, b, g):
    cb, nblk, sp = _gdn_layout(qkv.shape[0], False)
    sh = _gdn_shapes(qkv.shape[0])

    def body(q_ref, k_ref, v_ref, b_ref, g_ref, u_ref, w_ref, qg_ref, kd_ref, aqk_ref, gl_ref):
        b_blk, g_blk = b_ref[...], g_ref[...]
        gl = jnp.zeros(b_blk.shape, F32)
        for h in range(GDN_HEADS):
            cols = _head_cols(h)
            u, w, qg, kd, aqk, gl_h = _gdn_intra(h, q_ref[:, cols], k_ref[:, cols], v_ref[:, cols], b_blk, g_blk)
            u_ref[:, cols] = u.astype(u_ref.dtype)
            w_ref[:, cols] = w.astype(w_ref.dtype)
            qg_ref[:, cols] = qg.astype(qg_ref.dtype)
            kd_ref[:, cols] = kd.astype(kd_ref.dtype)
            aqk_ref[h] = aqk
            gl = gl + gl_h
        gl_ref[...] = gl

    return _call(
        body, name="gdn_intra_fwd", grid=(nblk,),
        in_specs=[sp["q"], sp["k"], sp["v"]] + [sp["gate"]] * 2,
        out_specs=[sp["tok"]] * 4 + [sp["aqk"], sp["gate"]],
        out_shape=[sh["tok"]] * 4 + [sh["aqk"], sh["gate"]],
        compiler_params=_cparams(("parallel",)),
    )(qkv, qkv, qkv, b, g)


def _gdn_intra_bwd_call(qkv, b, g, u, w, du, dw, dqg, dkd, daqk, dgl):
    cb, nblk, sp = _gdn_layout(qkv.shape[0], False)
    sh = _gdn_shapes(qkv.shape[0])

    def body(q_ref, k_ref, v_ref, b_ref, g_ref, u_ref, w_ref, du_ref, dw_ref, dqg_ref, dkd_ref, daqk_ref, dgl_ref,
             dqkv_ref, db_ref, dg_ref):
        b_blk, g_blk, dgl = b_ref[...], g_ref[...], dgl_ref[...]
        db = jnp.zeros(b_blk.shape, F32)
        dg = jnp.zeros(b_blk.shape, F32)
        for h in range(GDN_HEADS):
            cols = _head_cols(h)
            known = (u_ref[:, cols].astype(F32), w_ref[:, cols].astype(F32))
            _, vjp = jax.vjp(functools.partial(_gdn_intra, h, uw_known=known),
                             q_ref[:, cols], k_ref[:, cols], v_ref[:, cols], b_blk, g_blk)
            dq, dk, dv, db_h, dg_h = vjp((*[r[:, cols].astype(F32) for r in (du_ref, dw_ref, dqg_ref, dkd_ref)],
                                          daqk_ref[h], dgl))
            dqkv_ref[:, pl.ds(h * GDN_HD, GDN_HD)] = dq
            dqkv_ref[:, pl.ds(GDN_W + h * GDN_HD, GDN_HD)] = dk
            dqkv_ref[:, pl.ds(2 * GDN_W + h * GDN_HD, GDN_HD)] = dv
            db = db + db_h
            dg = dg + dg_h
        db_ref[...] = db
        dg_ref[...] = dg

    return _call(
        body, name="gdn_intra_bwd", grid=(nblk,),
        in_specs=[sp["q"], sp["k"], sp["v"]] + [sp["gate"]] * 2 + [sp["tok"]] * 6 + [sp["aqk"], sp["gate"]],
        out_specs=[sp["qkv"]] + [sp["gate"]] * 2,
        out_shape=[jax.ShapeDtypeStruct(qkv.shape, F32)] + [sh["gate"]] * 2,
        compiler_params=_cparams(("parallel",)),
    )(qkv, qkv, qkv, b, g, u, w, du, dw, dqg, dkd, daqk, dgl)


def _gdn_rec_fwd_call(u, w, qg, kd, aqk, gl):
    cb, nblk, sp = _gdn_layout(u.shape[0], False)
    sh = _gdn_shapes(u.shape[0])

    def body(u_ref, w_ref, qg_ref, kd_ref, aqk_ref, gl_ref, o_ref, s_all_ref, s_ref):
        @pl.when(pl.program_id(0) == 0)
        def _():
            s_ref[...] = jnp.zeros_like(s_ref)

        def chunk(ci, carry):
            rows = _chunk_rows(ci)
            gl_row = gl_ref[rows, :]
            states = [s_ref[h] for h in range(GDN_HEADS)]
            res = [_gdn_rec(h, states[h], *[r[rows, _head_cols(h)].astype(F32) for r in (u_ref, w_ref, qg_ref, kd_ref)],
                            aqk_ref[h, rows, :], gl_row)
                   for h in range(GDN_HEADS)]
            for h, (o, s_next) in enumerate(res):
                s_all_ref[h, ci] = states[h]
                o_ref[rows, _head_cols(h)] = o
                s_ref[h] = s_next
            return carry

        lax.fori_loop(0, cb, chunk, 0)

    return _call(
        body, name="gdn_rec_fwd", grid=(nblk,),
        in_specs=[sp["tok"]] * 4 + [sp["aqk"], sp["gate"]],
        out_specs=[sp["tok"], sp["state"]], out_shape=[sh["out"], sh["state"]],
        scratch_shapes=[pltpu.VMEM((GDN_HEADS, GDN_HD, GDN_HD), F32)],
        compiler_params=_cparams(("arbitrary",)),
    )(u, w, qg, kd, aqk, gl)


def _gdn_rec_bwd_call(u, w, qg, kd, aqk, gl, s_all, do):
    cb, nblk, sp = _gdn_layout(u.shape[0], True)
    sh = _gdn_shapes(u.shape[0])

    def body(u_ref, w_ref, qg_ref, kd_ref, aqk_ref, gl_ref, s_all_ref, do_ref,
             du_ref, dw_ref, dqg_ref, dkd_ref, daqk_ref, dgl_ref, ds_ref):
        @pl.when(pl.program_id(0) == 0)
        def _():
            ds_ref[...] = jnp.zeros_like(ds_ref)

        def chunk(step, carry):
            ci = cb - 1 - step
            rows = _chunk_rows(ci)
            gl_row = gl_ref[rows, :]
            res = []
            for h in range(GDN_HEADS):
                cols = _head_cols(h)
                _, vjp = jax.vjp(functools.partial(_gdn_rec, h), s_all_ref[h, ci],
                                 *[r[rows, cols].astype(F32) for r in (u_ref, w_ref, qg_ref, kd_ref)],
                                 aqk_ref[h, rows, :], gl_row)
                res.append(vjp((do_ref[rows, cols], ds_ref[h])))
            dgl = jnp.zeros((GDN_CHUNK, GDN_HEADS), F32)
            for h, (ds, du, dw, dqg, dkd, daqk, dgl_h) in enumerate(res):
                cols = _head_cols(h)
                ds_ref[h] = ds
                du_ref[rows, cols] = du.astype(du_ref.dtype)
                dw_ref[rows, cols] = dw.astype(dw_ref.dtype)
                dqg_ref[rows, cols] = dqg.astype(dqg_ref.dtype)
                dkd_ref[rows, cols] = dkd.astype(dkd_ref.dtype)
                daqk_ref[h, rows, :] = daqk
                dgl = dgl + dgl_h
            dgl_ref[rows, :] = dgl
            return carry

        lax.fori_loop(0, cb, chunk, 0)

    return _call(
        body, name="gdn_rec_bwd", grid=(nblk,),
        in_specs=[sp["tok"]] * 4 + [sp["aqk"], sp["gate"], sp["state"], sp["tok"]],
        out_specs=[sp["tok"]] * 4 + [sp["aqk"], sp["gate"]],
        out_shape=[sh["tok"]] * 4 + [sh["aqk"], sh["gate"]],
        scratch_shapes=[pltpu.VMEM((GDN_HEADS, GDN_HD, GDN_HD), F32)],
        compiler_params=_cparams(("arbitrary",)),
    )(u, w, qg, kd, aqk, gl, s_all, do)


@jax.custom_vjp
def gdn_intra(qkv, b, g):
    return tuple(_gdn_intra_fwd_call(qkv, b, g))


def _gdn_intra_fwd(*a):
    outs = gdn_intra(*a)
    return outs, a + (outs[0], outs[1])


gdn_intra.defvjp(_gdn_intra_fwd, lambda res, cts: tuple(_gdn_intra_bwd_call(*res, *cts)))


@jax.custom_vjp
def gdn_rec(u, w, qg, kd, aqk, gl):
    return _gdn_rec_fwd_call(u, w, qg, kd, aqk, gl)[0]


def _gdn_rec_fwd(*a):
    o, s_all = _gdn_rec_fwd_call(*a)
    return o, a + (s_all,)


gdn_rec.defvjp(_gdn_rec_fwd, lambda res, do: tuple(_gdn_rec_bwd_call(*res, do)))


def gated_delta(qkv, b, g):
    return gdn_rec(*gdn_intra(qkv, b, g))


def _loss_call(y, tgt, first, last):
    r_total, d = y.shape
    tm = _tile(r_total, 256, 8)

    def body(y_ref, t_ref, loss_ref, dy_ref):
        i = pl.program_id(0)

        @pl.when(i == 0)
        def _():
            loss_ref[...] = jnp.zeros_like(loss_ref)

        row = lax.broadcasted_iota(jnp.int32, (tm, d), 0) + i * tm
        err = jnp.where((row >= first) & (row < last), y_ref[...] - t_ref[...], 0.0)
        dy_ref[...] = err * (1.0 / d)
        part = jnp.sum(jnp.sum(err * err, axis=1, keepdims=True), axis=0, keepdims=True) * (0.5 / d)
        loss_ref[...] += jnp.broadcast_to(part, loss_ref.shape)

    return _call(
        body, name="loss_head", grid=(r_total // tm,),
        in_specs=[pl.BlockSpec((tm, d), lambda i: (i, 0))] * 2,
        out_specs=[pl.BlockSpec((8, LANE), lambda i: (0, 0)), pl.BlockSpec((tm, d), lambda i: (i, 0))],
        out_shape=[jax.ShapeDtypeStruct((8, LANE), F32), jax.ShapeDtypeStruct(y.shape, F32)],
        compiler_params=_cparams(("arbitrary",)),
    )(y, tgt)


def make_loss(first, last):
    @jax.custom_vjp
    def op(y, tgt):
        return _loss_call(y, tgt, first, last)[0][0, 0]

    def fwd(y, tgt):
        loss, dy = _loss_call(y, tgt, first, last)
        return loss[0, 0], (dy,)

    def bwd(res, ct):
        return res[0] * ct, jnp.zeros_like(res[0])

    op.defvjp(fwd, bwd)
    return op


def _pad_rows8(w):
    return jnp.concatenate([w, jnp.zeros((8 - w.shape[0], w.shape[1]), w.dtype)], axis=0)


def local_loss(wts, x, tgt):
    seq = x.shape[0]
    n_tok = N_META + seq
    t_pad = -(-n_tok // ROW_ALIGN) * ROW_ALIGN
    depth = wts["norm1_g"].shape[0]
    blk = _tile(t_pad, ATT_BLK, LANE)
    nb = t_pad // blk
    tm = _tile(t_pad, 256, 8)

    rms = rowop(_f_rmsnorm, "rmsnorm", (D_MODEL,), tm, out_dtypes=[BF16])
    qnorm = rowop(_f_qnorm, "fox_q_norm", (FOX_HD,), _tile(FOX_HEADS * t_pad, 2048, 8))
    knorm = rowop(_f_rmsnorm, "fox_k_norm", (FOX_HD,), _tile(FOX_HEADS * t_pad, 2048, 8))
    logsig = rowop(_f_logsig, "fox_log_forget", (FOX_HEADS,), tm)
    gdn_act = rowop([_f_gdn_q] * GDN_HEADS + [_f_gdn_k] * GDN_HEADS + [_f_gdn_v] * GDN_HEADS, "gdn_qkv_act",
                    (GDN_HD,), tm)
    gates = rowop(_f_gdn_gates, "gdn_gates", (GDN_HEADS, GDN_HEADS), tm)
    gdn_out = rowop([_f_gdn_out] * GDN_HEADS, "gdn_out_norm", (GDN_HD,), tm, out_dtypes=[BF16])
    merge = rowop(_f_merge, "branch_merge", (D_MODEL,), tm, out_dtypes=[BF16])
    residual = rowop(_f_residual, "residual_add", (D_MODEL,), tm, bc=(2,))
    residual_norm = rowop(_f_residual_norm, "residual_add_norm", (D_MODEL, D_MODEL), tm, bc=(2,),
                          out_dtypes=[F32, BF16])
    keep = (jnp.arange(t_pad)[:, None] < n_tok).astype(F32)
    conv4 = make_dwconv(GDN_CONV, BF16)
    conv3 = make_dwconv(FFN_CONV)
    loss_op = make_loss(N_META, n_tok)

    zeros = jnp.zeros((t_pad - n_tok, D_MODEL), F32)
    h_res = jnp.concatenate([wts["meta_tokens"], x, zeros], axis=0)
    tgt_rows = jnp.concatenate([jnp.zeros((N_META, D_MODEL), F32), tgt, zeros], axis=0)

    def heads(a):
        return a.reshape(t_pad, FOX_HEADS, FOX_HD).transpose(1, 0, 2).reshape(FOX_HEADS * t_pad, FOX_HD)

    h = rms((h_res,), (wts["norm1_g"][0][None],))[0]
    for l in range(depth):
        proj = mm_bf16(h, wts["w_in"][l])
        gate_logits = mm(h, jnp.concatenate([wts["w_in"][l][:, 1536:1536 + LANE],
                                             wts["w_in"][l][:, 4736:4736 + LANE]], axis=1))
        qn = qnorm((heads(proj[:, 0:512]),), (wts["fox_q_norm_g"][l][None],))[0]
        kn = knorm((heads(proj[:, 512:1024]),), (wts["fox_k_norm_g"][l][None],))[0]
        vh = heads(proj[:, 1024:1536])
        log_f = logsig((gate_logits[:, 0:FOX_HEADS],), (wts["fox_f_bias"][l][None],))[0]
        f_cum = cumsum_lanes(log_f.T)
        o_a = fox_attention(qn.reshape(FOX_HEADS, t_pad, FOX_HD), kn.reshape(FOX_HEADS, t_pad, FOX_HD),
                            vh.reshape(FOX_HEADS, t_pad, FOX_HD), f_cum[:, :, None],
                            f_cum.reshape(FOX_HEADS, nb, 1, blk))
        y_a = mm_bf16(o_a.transpose(2, 0, 1).reshape(t_pad, FOX_W).astype(BF16), wts["w_branch_a"][l])
        cv = conv4(proj[:, 1664:4736], _pad_rows8(wts["gdn_conv_w"][l]))
        qkv = gdn_act((cv,), ())[0]
        beta, gdec = gates((gate_logits[:, LANE:LANE + GDN_HEADS], gate_logits[:, LANE + GDN_HEADS:LANE + 2 * GDN_HEADS]),
                           (wts["gdn_a_log"][l][None], wts["gdn_dt_bias"][l][None]))
        o_b = gated_delta(qkv, beta, gdec)
        o_b = gdn_out((o_b, proj[:, 4864:5888]), (wts["gdn_norm_g"][l][None],))[0]
        y_b = mm_bf16(o_b, wts["w_branch_b"][l])
        mixed = merge((proj[:, 5888:6912], proj[:, 6912:7936], y_a, y_b), ())[0]
        h_res, h = residual_norm((h_res, mm(mixed, wts["w_out"][l]), keep), (wts["norm2_g"][l][None],))
        up = conv3(mm_bf16(h, wts["w_up"][l]), _pad_rows8(wts["ffn_conv_w"][l]))
        act = glu(up)
        down = mm(act, wts["w_down"][l])
        if l + 1 < depth:
            h_res, h = residual_norm((h_res, down, keep), (wts["norm1_g"][l + 1][None],))
        else:
            h_res = residual((h_res, down, keep), ())[0]
    return loss_op(h_res, tgt_rows)


def pad_w_in(w):
    parts, pos = [], 0
    for src, width, dst in IN_SEGS:
        if dst > pos:
            parts.append(jnp.zeros(w.shape[:-1] + (dst - pos,), w.dtype))
        parts.append(w[..., src:src + width])
        pos = dst + width
    parts.append(jnp.zeros(w.shape[:-1] + (D_IN_PAD - pos,), w.dtype))
    return jnp.concatenate(parts, axis=-1)


def unpad_w_in(w):
    return jnp.concatenate([w[..., dst:dst + width] for _, width, dst in IN_SEGS], axis=-1)


ANY = pl.BlockSpec(memory_space=pl.ANY)
N_CHIPS = 4
N_DEV = 8
COMM_COLS = 1024
COMM_ROW_ALIGN = 512
COMM_ROW_ALIGN_SMALL = 32


def _place():
    return lax.axis_index("x"), lax.axis_index("y"), lax.axis_index("c")


def _other_chips(x, y):
    return [(1 - x, y), (x, 1 - y), (1 - x, 1 - y)]


def _remote(src, dst, send_sem, recv_sem, dev):
    return pltpu.make_async_remote_copy(src_ref=src, dst_ref=dst, send_sem=send_sem, recv_sem=recv_sem,
                                        device_id=dev, device_id_type=MESH)


def chip_all_gather(buf):
    rows, cols = buf.shape
    half = rows // 2

    def body(x_ref, out_ref, send_sems, recv_sems, pass_send, pass_recv):
        x, y, c = _place()
        me = 2 * x + y
        mine, other = pl.ds(c * half, half), pl.ds((1 - c) * half, half)
        sibling = (x, y, 1 - c)
        chips = _other_chips(x, y)
        started = []
        for k, (px, py) in enumerate(chips):
            cp = _remote(x_ref.at[mine], out_ref.at[me, mine], send_sems.at[k], recv_sems.at[k], (px, py, c))
            cp.start()
            started.append(cp)
        for k, (px, py) in enumerate(chips):
            landed = out_ref.at[2 * px + py, mine]
            _remote(landed, landed, send_sems.at[k], recv_sems.at[k], (px, py, c)).wait_recv()
            cp = _remote(landed, landed, pass_send.at[k], pass_recv.at[k], sibling)
            cp.start()
            started.append(cp)
        for k, (px, py) in enumerate(chips):
            passed = out_ref.at[2 * px + py, other]
            _remote(passed, passed, pass_send.at[k], pass_recv.at[k], sibling).wait_recv()
        for cp in started:
            cp.wait_send()

    got = _call(
        body, name="chip_all_gather", in_specs=[ANY], out_specs=ANY,
        out_shape=jax.ShapeDtypeStruct((N_CHIPS, rows, cols), buf.dtype),
        scratch_shapes=[pltpu.SemaphoreType.DMA((3,)), pltpu.SemaphoreType.DMA((3,)),
                        pltpu.SemaphoreType.DMA((3,)), pltpu.SemaphoreType.DMA((3,))],
    )(buf)
    me = 2 * lax.axis_index("x") + lax.axis_index("y")
    return lax.dynamic_update_slice(got, buf[None], (me, 0, 0))


def sibling_swap_halves(g4):
    n, rows, cols = g4.shape
    half = rows // 2

    def body(g_ref, got_ref, send_sem, recv_sem):
        x, y, c = _place()
        cp = _remote(g_ref.at[:, pl.ds((1 - c) * half, half), :], got_ref, send_sem, recv_sem, (x, y, 1 - c))
        cp.start()
        cp.wait()

    return _call(
        body, name="sibling_swap_halves", in_specs=[ANY], out_specs=ANY,
        out_shape=jax.ShapeDtypeStruct((n, half, cols), g4.dtype),
        scratch_shapes=[pltpu.SemaphoreType.DMA, pltpu.SemaphoreType.DMA],
    )(g4)


def add_own_half(g4, got, c):
    n, rows, cols = g4.shape
    half = rows // 2
    tm = _tile(half, 256, 16)
    nt = half // tm

    def body(c_ref, a_ref, b_ref, o_ref):
        o_ref[...] = (a_ref[...] + b_ref[...]).astype(o_ref.dtype)

    return _call(
        body, name="add_own_half",
        grid_spec=pltpu.PrefetchScalarGridSpec(
            num_scalar_prefetch=1, grid=(n, nt),
            in_specs=[pl.BlockSpec((1, tm, cols), lambda j, i, c_ref: (j, c_ref[0] * nt + i, 0)),
                      pl.BlockSpec((1, tm, cols), lambda j, i, c_ref: (j, i, 0))],
            out_specs=pl.BlockSpec((1, tm, cols), lambda j, i, c_ref: (j, i, 0))),
        out_shape=jax.ShapeDtypeStruct(got.shape, BF16),
        compiler_params=_cparams(("parallel", "parallel")),
    )(c.reshape(1).astype(jnp.int32), g4, got)


def chip_scatter(p4):
    n, rows, cols = p4.shape

    def body(p_ref, out_ref, send_sems, recv_sems):
        x, y, c = _place()
        me = 2 * x + y
        chips = _other_chips(x, y)
        started = []
        for k, (px, py) in enumerate(chips):
            cp = _remote(p_ref.at[2 * px + py], out_ref.at[me], send_sems.at[k], recv_sems.at[k], (px, py, c))
            cp.start()
            started.append(cp)
        for k, (px, py) in enumerate(chips):
            landed = out_ref.at[2 * px + py]
            _remote(landed, landed, send_sems.at[k], recv_sems.at[k], (px, py, c)).wait_recv()
        for cp in started:
            cp.wait_send()

    got = _call(
        body, name="chip_scatter", in_specs=[ANY], out_specs=ANY,
        out_shape=jax.ShapeDtypeStruct(p4.shape, p4.dtype),
        scratch_shapes=[pltpu.SemaphoreType.DMA((3,)), pltpu.SemaphoreType.DMA((3,))],
    )(p4)
    me = 2 * lax.axis_index("x") + lax.axis_index("y")
    return lax.dynamic_update_slice(got, lax.dynamic_slice_in_dim(p4, me, 1, axis=0), (me, 0, 0))


def sum_slots(a):
    n, rows, cols = a.shape
    tm = _tile(rows, 256, 16) if rows % 16 == 0 else rows

    def body(a_ref, o_ref):
        acc = a_ref[0].astype(F32)
        for k in range(1, n):
            acc = acc + a_ref[k].astype(F32)
        o_ref[...] = acc

    return _call(
        body, name="sum_slots_%d" % n, grid=(rows // tm,),
        in_specs=[pl.BlockSpec((n, tm, cols), lambda i: (0, i, 0))],
        out_specs=pl.BlockSpec((tm, cols), lambda i: (i, 0)),
        out_shape=jax.ShapeDtypeStruct((rows, cols), F32),
        compiler_params=_cparams(("parallel",)),
    )(a)


def sibling_join(s):
    half, cols = s.shape

    def body(s_ref, got_ref, send_sem, recv_sem):
        x, y, c = _place()
        cp = _remote(s_ref, got_ref, send_sem, recv_sem, (x, y, 1 - c))
        cp.start()
        cp.wait()

    got = _call(
        body, name="sibling_join", in_specs=[ANY], out_specs=ANY,
        out_shape=jax.ShapeDtypeStruct(s.shape, s.dtype),
        scratch_shapes=[pltpu.SemaphoreType.DMA, pltpu.SemaphoreType.DMA],
    )(s)
    c = lax.axis_index("c")
    out = jnp.zeros((2 * half, cols), s.dtype)
    out = lax.dynamic_update_slice(out, s, (c * half, 0))
    return lax.dynamic_update_slice(out, got, ((1 - c) * half, 0))


def all_devices_gather(buf):
    rows, cols = buf.shape

    def body(b_ref, out_ref, send_sems, recv_sems, local_sem):
        x, y, c = _place()
        me = 4 * x + 2 * y + c
        local = pltpu.make_async_copy(b_ref, out_ref.at[me], local_sem)
        local.start()
        peers = [((x + dx) % 2, (y + dy) % 2, (c + dc) % 2)
                 for dx in (0, 1) for dy in (0, 1) for dc in (0, 1) if dx + dy + dc > 0]
        started = []
        for k, peer in enumerate(peers):
            cp = _remote(b_ref, out_ref.at[me], send_sems.at[k], recv_sems.at[k], peer)
            cp.start()
            started.append(cp)
        for k, (px, py, pc) in enumerate(peers):
            landed = out_ref.at[4 * px + 2 * py + pc]
            _remote(landed, landed, send_sems.at[k], recv_sems.at[k], (px, py, pc)).wait_recv()
        for cp in started:
            cp.wait_send()
        local.wait()

    return _call(
        body, name="all_devices_gather", in_specs=[ANY], out_specs=ANY,
        out_shape=jax.ShapeDtypeStruct((N_DEV, rows, cols), buf.dtype),
        scratch_shapes=[pltpu.SemaphoreType.DMA((7,)), pltpu.SemaphoreType.DMA((7,)), pltpu.SemaphoreType.DMA],
    )(buf)


def adamw(w, g, m, v):
    shape = w.shape
    w2, g2, m2, v2 = [a.reshape(-1, shape[-1]) for a in (w, g, m, v)]
    rows, cols = w2.shape
    tm = _tile(rows, 256, 8) if rows % 8 == 0 else rows

    def body(w_ref, g_ref, m_ref, v_ref, d_ref, nm_ref, nv_ref):
        gv = g_ref[...]
        nm = ADAM_B1 * m_ref[...] + (1.0 - ADAM_B1) * gv
        nv = ADAM_B2 * v_ref[...] + (1.0 - ADAM_B2) * (gv * gv)
        m_hat = nm / (1.0 - ADAM_B1 ** ADAM_STEP)
        v_hat = nv / (1.0 - ADAM_B2 ** ADAM_STEP)
        d_ref[...] = -ADAM_LR * (m_hat / (jnp.sqrt(v_hat) + ADAM_EPS) + ADAM_WD * w_ref[...])
        nm_ref[...] = nm
        nv_ref[...] = nv

    spec = pl.BlockSpec((tm, cols), lambda i: (i, 0))
    outs = _call(
        body, name="adamw", grid=(rows // tm,), in_specs=[spec] * 4, out_specs=[spec] * 3,
        out_shape=[jax.ShapeDtypeStruct((rows, cols), F32)] * 3,
        compiler_params=_cparams(("parallel",)),
    )(w2, g2, m2, v2)
    return [o.reshape(shape) for o in outs]


WEIGHTS = ("meta_tokens", "norm1_g", "w_in", "fox_f_bias", "fox_q_norm_g", "fox_k_norm_g", "gdn_conv_w",
           "gdn_a_log", "gdn_dt_bias", "gdn_norm_g", "w_branch_a", "w_branch_b", "w_out", "norm2_g", "w_up",
           "ffn_conv_w", "w_down")
SHARD_AXIS = {"meta_tokens": -1, "w_in": -1, "gdn_conv_w": -1, "w_branch_a": -1, "w_branch_b": -2, "w_out": -2,
              "w_up": -1, "ffn_conv_w": -1, "w_down": -2}
MATMUL_WEIGHTS = ("w_in", "w_branch_a", "w_branch_b", "w_out", "w_up", "w_down")
SMALL_SHARDED = ("meta_tokens", "gdn_conv_w", "ffn_conv_w")
REPLICATED = tuple(n for n in WEIGHTS if n not in SHARD_AXIS)


def _pack(arrays, dtype, row_align):
    flat = jnp.concatenate([a.reshape(-1).astype(dtype) for a in arrays])
    block = row_align * COMM_COLS
    total = -(-flat.shape[0] // block) * block
    flat = jnp.concatenate([flat, jnp.zeros((total - flat.shape[0],), dtype)])
    return flat.reshape(-1, COMM_COLS)


def _unpack(buf, shapes):
    flat, out, pos = buf.reshape(-1), [], 0
    for shape in shapes:
        size = 1
        for d in shape:
            size *= d
        out.append(flat[pos:pos + size].reshape(shape))
        pos += size
    return out


def _gather_full(shards, names, dtype, row_align):
    got = chip_all_gather(_pack([shards[n] for n in names], dtype, row_align))
    per_chip = [_unpack(got[j], [shards[n].shape for n in names]) for j in range(N_CHIPS)]
    return {n: jnp.concatenate([per_chip[j][i] for j in range(N_CHIPS)], axis=SHARD_AXIS[n]).astype(F32)
            for i, n in enumerate(names)}


def _shard_of(full, name, j):
    axis = SHARD_AXIS[name] % full.ndim
    size = full.shape[axis] // N_CHIPS
    return lax.slice_in_dim(full, j * size, (j + 1) * size, axis=axis)


def kernel(x, meta_tokens, norm1_g, w_in, fox_f_bias, fox_q_norm_g, fox_k_norm_g, gdn_conv_w, gdn_a_log, gdn_dt_bias, gdn_norm_g, w_branch_a, w_branch_b, w_out, norm2_g, w_up, ffn_conv_w, w_down, loss_target, m_meta_tokens, m_norm1_g, m_w_in, m_fox_f_bias, m_fox_q_norm_g, m_fox_k_norm_g, m_gdn_conv_w, m_gdn_a_log, m_gdn_dt_bias, m_gdn_norm_g, m_w_branch_a, m_w_branch_b, m_w_out, m_norm2_g, m_w_up, m_ffn_conv_w, m_w_down, v_meta_tokens, v_norm1_g, v_w_in, v_fox_f_bias, v_fox_q_norm_g, v_fox_k_norm_g, v_gdn_conv_w, v_gdn_a_log, v_gdn_dt_bias, v_gdn_norm_g, v_w_branch_a, v_w_branch_b, v_w_out, v_norm2_g, v_w_up, v_ffn_conv_w, v_w_down):
    w_loc = dict(zip(WEIGHTS, (meta_tokens, norm1_g, w_in, fox_f_bias, fox_q_norm_g, fox_k_norm_g, gdn_conv_w,
                               gdn_a_log, gdn_dt_bias, gdn_norm_g, w_branch_a, w_branch_b, w_out, norm2_g, w_up,
                               ffn_conv_w, w_down)))
    m_loc = dict(zip(WEIGHTS, (m_meta_tokens, m_norm1_g, m_w_in, m_fox_f_bias, m_fox_q_norm_g, m_fox_k_norm_g,
                               m_gdn_conv_w, m_gdn_a_log, m_gdn_dt_bias, m_gdn_norm_g, m_w_branch_a, m_w_branch_b,
                               m_w_out, m_norm2_g, m_w_up, m_ffn_conv_w, m_w_down)))
    v_loc = dict(zip(WEIGHTS, (v_meta_tokens, v_norm1_g, v_w_in, v_fox_f_bias, v_fox_q_norm_g, v_fox_k_norm_g,
                               v_gdn_conv_w, v_gdn_a_log, v_gdn_dt_bias, v_gdn_norm_g, v_w_branch_a, v_w_branch_b,
                               v_w_out, v_norm2_g, v_w_up, v_ffn_conv_w, v_w_down)))
    c = lax.axis_index("c")

    full = {n: w_loc[n] for n in REPLICATED}
    full.update(_gather_full(w_loc, MATMUL_WEIGHTS, BF16, COMM_ROW_ALIGN))
    full.update(_gather_full(w_loc, SMALL_SHARDED, F32, COMM_ROW_ALIGN_SMALL))
    full["w_in"] = pad_w_in(full["w_in"])

    loss, (g_full, g_x) = jax.value_and_grad(local_loss, argnums=(0, 1))(full, x[0], loss_target[0])
    g_full = dict(g_full)
    g_full["w_in"] = unpad_w_in(g_full["w_in"])

    sharded = MATMUL_WEIGHTS + SMALL_SHARDED
    g4 = jnp.stack([_pack([_shard_of(g_full[n], n, j) for n in sharded], F32, COMM_ROW_ALIGN)
                    for j in range(N_CHIPS)])
    pair_sum = add_own_half(g4, sibling_swap_halves(g4), c)
    g_shard = sibling_join(sum_slots(chip_scatter(pair_sum)))
    grads = dict(zip(sharded, _unpack(g_shard, [w_loc[n].shape for n in sharded])))
    g_rep = sum_slots(all_devices_gather(_pack([g_full[n] for n in REPLICATED], F32, 8)))
    grads.update(zip(REPLICATED, _unpack(g_rep, [w_loc[n].shape for n in REPLICATED])))

    loss = lax.psum(loss, ("x", "y", "c"))
    upd = {n: adamw(w_loc[n], grads[n], m_loc[n], v_loc[n]) for n in WEIGHTS}
    return (loss, g_x[None], *[grads[n] for n in WEIGHTS], *[upd[n][0] for n in WEIGHTS],
            *[upd[n][1] for n in WEIGHTS], *[upd[n][2] for n in WEIGHTS])
```

```python
import functools

import jax
import jax.numpy as jnp
from jax import lax
from jax.experimental import pallas as pl
from jax.experimental.pallas import tpu as pltpu

F32 = jnp.float32
BF16 = jnp.bfloat16
HI = lax.Precision.HIGHEST
MESH = pl.DeviceIdType.MESH

D_MODEL = 1024
N_META = 16
EPS = 1e-6
FOX_HEADS, FOX_HD = 8, 64
FOX_W = FOX_HEADS * FOX_HD
GDN_HEADS, GDN_HD, GDN_CHUNK, GDN_CONV = 8, 128, 64, 4
GDN_W = GDN_HEADS * GDN_HD
D_FF = 2816
FFN_CONV = 3
D_IN = 7704
D_IN_PAD = 8192
IN_SEGS = ((0, 1536, 0), (1536, 8, 1536), (1544, 3072, 1664), (4616, 16, 4736), (4632, 1024, 4864), (5656, 2048, 5888))
ROW_ALIGN = 256
ATT_BLK = 256
VMEM_LIMIT = 48 * 1024 * 1024
LANE = 128

ADAM_LR, ADAM_B1, ADAM_B2, ADAM_EPS, ADAM_WD, ADAM_STEP = 0.001, 0.9, 0.999, 1e-08, 0.01, 10


def _call(body, **kw):
    return pl.pallas_call(body, **kw)


def _tile(n, target, mult):
    best, t = None, mult
    while t <= min(n, target):
        if n % t == 0:
            best = t
        t += mult
    assert best is not None, (n, target, mult)
    return best


def _cparams(sem):
    return pltpu.CompilerParams(dimension_semantics=sem, vmem_limit_bytes=VMEM_LIMIT)


def _raw_dot(a, b, ca, cb, precise):
    dims = (((ca,), (cb,)), ((), ()))
    a_hi, b_hi = a.astype(BF16), b.astype(BF16)
    out = lax.dot_general(a_hi, b_hi, dims, preferred_element_type=F32)
    if precise:
        a_lo = (a - a_hi.astype(F32)).astype(BF16)
        b_lo = (b - b_hi.astype(F32)).astype(BF16)
        out = out + (lax.dot_general(a_hi, b_lo, dims, preferred_element_type=F32)
                     + lax.dot_general(a_lo, b_hi, dims, preferred_element_type=F32))
    return out


def _make_dot(ca, cb, precise):
    @jax.custom_vjp
    def f(a, b):
        return _raw_dot(a, b, ca, cb, precise)

    def fwd(a, b):
        return f(a, b), (a, b)

    def bwd(res, ct):
        a, b = res
        if ca == 1:
            da = _raw_dot(ct, b, 1, 1 if cb == 0 else 0, precise)
        else:
            da = _raw_dot(b, ct, 1 if cb == 0 else 0, 1, precise)
        if cb == 0:
            db = _raw_dot(a, ct, 0 if ca == 1 else 1, 0, precise)
        else:
            db = _raw_dot(ct, a, 0, 0 if ca == 1 else 1, precise)
        return da, db

    f.defvjp(fwd, bwd)
    return f


_DOTS = {(ca, cb, p): _make_dot(ca, cb, p) for ca in (0, 1) for cb in (0, 1) for p in (False, True)}


def _dot(a, b, ca=1, cb=0, precise=False):
    return _DOTS[(ca, cb, precise)](a, b)


def _mm_call(a, b, name, ta=False, tb=False, out_dtype=F32):
    k, m = a.shape if ta else a.shape[::-1]
    n, kb = b.shape if tb else b.shape[::-1]
    assert k == kb, (a.shape, b.shape)
    tm = _tile(m, 1408, LANE if ta else 16)
    tn = _tile(n, 1408, LANE)
    tk = _tile(k, 1408, LANE)
    nk = k // tk
    dims = (((0 if ta else 1,), (1 if tb else 0,)), ((), ()))

    def body(a_ref, b_ref, o_ref, *scratch):
        part = lax.dot_general(a_ref[...], b_ref[...], dims, preferred_element_type=F32)
        if nk == 1:
            o_ref[...] = part.astype(o_ref.dtype)
            return
        acc_ref = scratch[0]
        kk = pl.program_id(2)

        @pl.when(kk == 0)
        def _():
            acc_ref[...] = part

        @pl.when(kk > 0)
        def _():
            acc_ref[...] += part

        @pl.when(kk == nk - 1)
        def _():
            o_ref[...] = acc_ref[...].astype(o_ref.dtype)

    return _call(
        body, name=name, grid=(m // tm, n // tn, nk),
        in_specs=[pl.BlockSpec((tk, tm), lambda i, j, kk: (kk, i)) if ta else
                  pl.BlockSpec((tm, tk), lambda i, j, kk: (i, kk)),
                  pl.BlockSpec((tn, tk), lambda i, j, kk: (j, kk)) if tb else
                  pl.BlockSpec((tk, tn), lambda i, j, kk: (kk, j))],
        out_specs=pl.BlockSpec((tm, tn), lambda i, j, kk: (i, j)),
        out_shape=jax.ShapeDtypeStruct((m, n), out_dtype),
        scratch_shapes=[pltpu.VMEM((tm, tn), F32)] if nk > 1 else [],
        compiler_params=_cparams(("parallel", "parallel", "arbitrary")),
    )(a, b)


def _make_mm(out_dtype):
    @jax.custom_vjp
    def op(a, w):
        return _mm_call(a.astype(BF16), w.astype(BF16), "mm_fwd", out_dtype=out_dtype)

    def fwd(a, w):
        a_b, w_b = a.astype(BF16), w.astype(BF16)
        return _mm_call(a_b, w_b, "mm_fwd", out_dtype=out_dtype), (a_b, w_b, jnp.zeros((), a.dtype))

    def bwd(res, ct):
        a_b, w_b, like_a = res
        ct_b = ct.astype(BF16)
        return (_mm_call(ct_b, w_b, "mm_dx", tb=True, out_dtype=like_a.dtype),
                _mm_call(a_b, ct_b, "mm_dw", ta=True))

    op.defvjp(fwd, bwd)
    return op


mm = _make_mm(F32)
mm_bf16 = _make_mm(BF16)


def _rows_specs(rows, tm, ncb, bc):
    specs = []
    for idx, r in enumerate(rows):
        if idx in bc:
            specs.append(pl.BlockSpec((tm, r.shape[1]), lambda i, j: (i, 0)))
        else:
            specs.append(pl.BlockSpec((tm, r.shape[1] // ncb), lambda i, j: (i, j)))
    return specs


def _param_specs(params):
    return [pl.BlockSpec(p.shape, lambda i, j: (0, 0)) for p in params]


def _group_slices(refs, groups, g, whole):
    out = []
    for idx, r in enumerate(refs):
        w = r.shape[1] // groups
        out.append((r[...] if idx in whole else r[:, g * w:(g + 1) * w]).astype(F32))
    return out


def _rows_fwd_call(fns, rows, params, outs, tm, ncb, bc, name, out_dtypes=None):
    r_total = rows[0].shape[0]
    nr, groups = len(rows), len(fns)
    out_dtypes = out_dtypes or [F32] * len(outs)

    def body(*refs):
        pvals = [r[...] for r in refs[nr:nr + len(params)]]
        for g, fn in enumerate(fns):
            res = fn(*_group_slices(refs[:nr], groups, g, bc), *pvals)
            for o_ref, val in zip(refs[nr + len(params):], res):
                w = o_ref.shape[1] // groups
                o_ref[:, g * w:(g + 1) * w] = val.astype(o_ref.dtype)

    return _call(
        body, name=name, grid=(r_total // tm, ncb),
        in_specs=_rows_specs(rows, tm, ncb, bc) + _param_specs(params),
        out_specs=[pl.BlockSpec((tm, w * groups), lambda i, j: (i, j)) for w in outs],
        out_shape=[jax.ShapeDtypeStruct((r_total, w * groups * ncb), dt) for w, dt in zip(outs, out_dtypes)],
        compiler_params=_cparams(("parallel", "parallel")),
    )(*rows, *params)


def _rows_bwd_call(fns, rows, params, cts, tm, ncb, bc, name):
    r_total = rows[0].shape[0]
    nr, npar, nct, groups = len(rows), len(params), len(cts), len(fns)

    def body(*refs):
        i, j = pl.program_id(0), pl.program_id(1)
        pvals = [r[...] for r in refs[nr:nr + npar]]
        ct_refs = refs[nr + npar:nr + npar + nct]
        d_refs = refs[nr + npar + nct:]
        shared = {idx: None for idx in list(bc) + list(range(nr, nr + npar))}
        for g, fn in enumerate(fns):
            _, vjp = jax.vjp(lambda *a, fn=fn: tuple(fn(*a)), *_group_slices(refs[:nr], groups, g, bc), *pvals)
            grads = vjp(tuple(_group_slices(ct_refs, groups, g, ())))
            for idx in range(nr + npar):
                if idx in shared:
                    shared[idx] = grads[idx] if shared[idx] is None else shared[idx] + grads[idx]
                else:
                    w = d_refs[idx].shape[1] // groups
                    d_refs[idx][:, g * w:(g + 1) * w] = grads[idx].astype(d_refs[idx].dtype)
        for idx, total in shared.items():
            first = (j == 0) if idx < nr else ((i == 0) & (j == 0))

            @pl.when(first)
            def _(idx=idx):
                d_refs[idx][...] = jnp.zeros_like(d_refs[idx])
            d_refs[idx][...] += total

    ct_specs = [pl.BlockSpec((tm, c.shape[1] // ncb), lambda i, j: (i, j)) for c in cts]
    return _call(
        body, name=name + "_bwd", grid=(r_total // tm, ncb),
        in_specs=_rows_specs(rows, tm, ncb, bc) + _param_specs(params) + ct_specs,
        out_specs=_rows_specs(rows, tm, ncb, bc) + _param_specs(params),
        out_shape=[jax.ShapeDtypeStruct(a.shape, a.dtype) for a in list(rows) + list(params)],
        compiler_params=_cparams(("arbitrary", "arbitrary")),
    )(*rows, *params, *cts)


def rowop(fn, name, outs, tm, ncb=1, bc=(), out_dtypes=None):
    fns = list(fn) if isinstance(fn, (list, tuple)) else [fn]

    @jax.custom_vjp
    def op(rows, params):
        return tuple(_rows_fwd_call(fns, rows, params, outs, tm, ncb, bc, name, out_dtypes))

    def fwd(rows, params):
        return op(rows, params), (rows, params)

    def bwd(res, cts):
        rows, params = res
        d = _rows_bwd_call(fns, rows, params, cts, tm, ncb, bc, name)
        return tuple(d[:len(rows)]), tuple(d[len(rows):])

    op.defvjp(fwd, bwd)
    return op


def _sigmoid(x):
    return 1.0 / (1.0 + jnp.exp(-x))


def _silu(x):
    return x * _sigmoid(x)


def _softplus(x):
    return jnp.maximum(x, 0.0) + jnp.log(1.0 + jnp.exp(-jnp.abs(x)))


def _f_rmsnorm(x, g):
    return (x * lax.rsqrt(jnp.mean(x * x, axis=-1, keepdims=True) + EPS) * g,)


def _f_qnorm(x, g):
    return (x * lax.rsqrt(jnp.mean(x * x, axis=-1, keepdims=True) + EPS) * (g * (FOX_HD ** -0.5)),)


def _f_logsig(x, b):
    return (-_softplus(-(x + b)),)


def _f_gdn_q(x):
    y = _silu(x)
    return (y * lax.rsqrt(jnp.sum(y * y, axis=-1, keepdims=True) + EPS) * (GDN_HD ** -0.5),)


def _f_gdn_k(x):
    y = _silu(x)
    return (y * lax.rsqrt(jnp.sum(y * y, axis=-1, keepdims=True) + EPS),)


def _f_gdn_v(x):
    return (_silu(x),)


def _f_gdn_gates(bl, al, a_log, dt_bias):
    return _sigmoid(bl), -jnp.exp(a_log) * _softplus(al + dt_bias)


def _f_gdn_out(o, z, g):
    return (o * lax.rsqrt(jnp.mean(o * o, axis=-1, keepdims=True) + EPS) * g * _silu(z),)


def _f_merge(g0, g1, ya, yb):
    return (_sigmoid(g0) * ya + _sigmoid(g1) * yb,)


def _f_residual(a, b, keep):
    return ((a + b) * keep,)


def _f_residual_norm(a, b, keep, g):
    r = (a + b) * keep
    return r, _f_rmsnorm(r, g)[0]


def _f_glu(a, b):
    return (_silu(a) * b,)


def _glu_call(up, ct):
    t_total, two_f = up.shape
    f = two_f // 2
    tm = _tile(t_total, 128, 16)
    wc = _tile(f, 1408, LANE)

    def body(*refs):
        up_ref, out_ref = refs[0], refs[-1]
        for c0 in range(0, f, wc):
            a, b = up_ref[:, c0:c0 + wc].astype(F32), up_ref[:, f + c0:f + c0 + wc].astype(F32)
            if ct is None:
                out_ref[:, c0:c0 + wc] = _f_glu(a, b)[0].astype(out_ref.dtype)
            else:
                _, vjp = jax.vjp(_f_glu, a, b)
                da, db = vjp((refs[1][:, c0:c0 + wc].astype(F32),))
                out_ref[:, c0:c0 + wc] = da.astype(out_ref.dtype)
                out_ref[:, f + c0:f + c0 + wc] = db.astype(out_ref.dtype)

    wide = pl.BlockSpec((tm, two_f), lambda i: (i, 0))
    narrow = pl.BlockSpec((tm, f), lambda i: (i, 0))
    return _call(
        body, name="ffn_glu" if ct is None else "ffn_glu_bwd", grid=(t_total // tm,),
        in_specs=[wide] if ct is None else [wide, narrow], out_specs=narrow if ct is None else wide,
        out_shape=jax.ShapeDtypeStruct((t_total, f), BF16) if ct is None else jax.ShapeDtypeStruct(up.shape, up.dtype),
        compiler_params=_cparams(("parallel",)),
    )(*((up,) if ct is None else (up, ct)))


@jax.custom_vjp
def glu(up):
    return _glu_call(up, None)


glu.defvjp(lambda up: (glu(up), up), lambda up, ct: (_glu_call(up, ct),))


def _shift_down(x, halo, s, row8):
    rx = pltpu.roll(x, s, 0)
    top = jnp.where(row8 < s, pltpu.roll(halo, s, 0), rx[:8])
    return jnp.concatenate([top, rx[8:]], axis=0)


def _shift_up(x, nxt, s, row8):
    tm = x.shape[0]
    rx = pltpu.roll(x, tm - s, 0)
    bot = jnp.where(row8 >= 8 - s, pltpu.roll(nxt, 8 - s, 0), rx[tm - 8:])
    return jnp.concatenate([rx[:tm - 8], bot], axis=0)


def _conv_tiles(r_total, c_total):
    return _tile(r_total, 768, 8), _tile(c_total, 1408, LANE)


def _halo_rows(dtype):
    return 16 if dtype == BF16 else 8


def _conv_fwd_call(x, w8, k_taps, out_dtype):
    r_total, c_total = x.shape
    tm, tc = _conv_tiles(r_total, c_total)
    hr = _halo_rows(x.dtype)
    hb = tm // hr

    def body(x_ref, halo_ref, w_ref, y_ref):
        i = pl.program_id(1)
        xt = x_ref[...].astype(F32)
        halo = jnp.where(i > 0, halo_ref[...].astype(F32)[hr - 8:hr], 0.0)
        row8 = lax.broadcasted_iota(jnp.int32, (8, tc), 0)
        acc = w_ref[k_taps - 1:k_taps, :] * xt
        for k in range(k_taps - 1):
            acc += w_ref[k:k + 1, :] * _shift_down(xt, halo, k_taps - 1 - k, row8)
        y_ref[...] = acc.astype(y_ref.dtype)

    return _call(
        body, name="dwconv_fwd", grid=(c_total // tc, r_total // tm),
        in_specs=[pl.BlockSpec((tm, tc), lambda c, i: (i, c)),
                  pl.BlockSpec((hr, tc), lambda c, i: (jnp.maximum(i * hb - 1, 0), c)),
                  pl.BlockSpec((8, tc), lambda c, i: (0, c))],
        out_specs=pl.BlockSpec((tm, tc), lambda c, i: (i, c)),
        out_shape=jax.ShapeDtypeStruct(x.shape, out_dtype),
        compiler_params=_cparams(("parallel", "parallel")),
    )(x, x, w8)


def _conv_bwd_call(x, w8, dy, k_taps):
    r_total, c_total = x.shape
    tm, tc = _conv_tiles(r_total, c_total)
    hr = _halo_rows(dy.dtype)
    hb = tm // hr
    n_i = r_total // tm

    def body(x_ref, w_ref, dy_ref, nxt_ref, dx_ref, dw_ref):
        i = pl.program_id(1)
        xt, dyt = x_ref[...].astype(F32), dy_ref[...].astype(F32)
        nxt = jnp.where(i < n_i - 1, nxt_ref[...].astype(F32)[0:8], 0.0)
        row8 = lax.broadcasted_iota(jnp.int32, (8, tc), 0)
        dx = w_ref[k_taps - 1:k_taps, :] * dyt
        upd = jnp.where(row8 == k_taps - 1, jnp.sum(dyt * xt, axis=0, keepdims=True), 0.0)
        for k in range(k_taps - 1):
            dy_ahead = _shift_up(dyt, nxt, k_taps - 1 - k, row8)
            dx += w_ref[k:k + 1, :] * dy_ahead
            upd = jnp.where(row8 == k, jnp.sum(dy_ahead * xt, axis=0, keepdims=True), upd)
        dx_ref[...] = dx.astype(dx_ref.dtype)

        @pl.when(i == 0)
        def _():
            dw_ref[...] = jnp.zeros_like(dw_ref)

        dw_ref[...] += upd

    return _call(
        body, name="dwconv_bwd", grid=(c_total // tc, n_i),
        in_specs=[pl.BlockSpec((tm, tc), lambda c, i: (i, c)),
                  pl.BlockSpec((8, tc), lambda c, i: (0, c)),
                  pl.BlockSpec((tm, tc), lambda c, i: (i, c)),
                  pl.BlockSpec((hr, tc), lambda c, i: (jnp.minimum((i + 1) * hb, r_total // hr - 1), c))],
        out_specs=[pl.BlockSpec((tm, tc), lambda c, i: (i, c)), pl.BlockSpec((8, tc), lambda c, i: (0, c))],
        out_shape=[jax.ShapeDtypeStruct(x.shape, x.dtype), jax.ShapeDtypeStruct(w8.shape, F32)],
        compiler_params=_cparams(("parallel", "arbitrary")),
    )(x, w8, dy, dy)


def make_dwconv(k_taps, out_dtype=None):
    @jax.custom_vjp
    def op(x, w8):
        return _conv_fwd_call(x, w8, k_taps, out_dtype or x.dtype)

    def fwd(x, w8):
        return op(x, w8), (x, w8)

    def bwd(res, dy):
        x, w8 = res
        dx, dw = _conv_bwd_call(x, w8, dy, k_taps)
        return dx, dw

    op.defvjp(fwd, bwd)
    return op


def _cumsum_call(x, reverse):
    h, t_total = x.shape
    tb = _tile(t_total, 256, LANE)
    nb = t_total // tb

    def body(x_ref, o_ref, carry_ref):
        i = pl.program_id(0)

        @pl.when(i == 0)
        def _():
            carry_ref[...] = jnp.zeros_like(carry_ref)

        r = lax.broadcasted_iota(jnp.int32, (tb, tb), 0)
        c = lax.broadcasted_iota(jnp.int32, (tb, tb), 1)
        tri = jnp.where((r >= c) if reverse else (r <= c), 1.0, 0.0).astype(F32)
        xv = x_ref[...]
        carry = jnp.max(carry_ref[...], axis=1, keepdims=True)
        o_ref[...] = _raw_dot(xv, tri, 1, 0, True) + carry
        carry_ref[...] = jnp.broadcast_to(carry + jnp.sum(xv, axis=1, keepdims=True), carry_ref.shape)

    imap = (lambda i: (0, nb - 1 - i)) if reverse else (lambda i: (0, i))
    return _call(
        body, name="cumsum_rev" if reverse else "cumsum", grid=(nb,),
        in_specs=[pl.BlockSpec((h, tb), imap)], out_specs=pl.BlockSpec((h, tb), imap),
        out_shape=jax.ShapeDtypeStruct(x.shape, F32), scratch_shapes=[pltpu.VMEM((h, LANE), F32)],
        compiler_params=_cparams(("arbitrary",)),
    )(x)


@jax.custom_vjp
def cumsum_lanes(x):
    return _cumsum_call(x, False)


cumsum_lanes.defvjp(lambda x: (cumsum_lanes(x), None), lambda _, ct: (_cumsum_call(ct, True),))


NEG_BIG = -1e30


def _attn_sub_tiles(nb):
    return max(s for s in (3, 2, 1) if nb % s == 0)


EXP_ZERO = -92.0
SMEM = pl.BlockSpec(memory_space=pltpu.SMEM)


def _max_row_norm_sq(x):
    h_total, t_total, hd = x.shape
    tb = _tile(t_total, 2816, 8)

    def body(x_ref, o_ref):
        @pl.when(pl.program_id(1) == 0)
        def _():
            o_ref[...] = jnp.zeros_like(o_ref)

        xv = x_ref[0]
        top = jnp.max(jnp.sum(xv * xv, axis=1, keepdims=True), axis=0, keepdims=True)
        o_ref[0] = jnp.maximum(o_ref[0], top)

    return _call(
        body, name="max_row_norm", grid=(h_total, t_total // tb),
        in_specs=[pl.BlockSpec((1, tb, hd), lambda h, i: (h, i, 0))],
        out_specs=pl.BlockSpec((1, 8, LANE), lambda h, i: (h, 0, 0)),
        out_shape=jax.ShapeDtypeStruct((h_total, 8, LANE), F32),
        compiler_params=_cparams(("parallel", "arbitrary")),
    )(x)


def _attn_skip_tables(q, k, f_row):
    bound = 2.0 * jnp.sqrt(_max_row_norm_sq(q)[:, 0, :1] * _max_row_norm_sq(k)[:, 0, :1])
    return EXP_ZERO - bound, f_row[:, :, 0, 0], f_row[:, :, 0, -1]


def _attn_fwd_call(q, k, v, f_col, f_row, tables):
    h_total, t_total, hd = q.shape
    blk = f_row.shape[-1]
    nb = t_total // blk
    nsub = _attn_sub_tiles(nb)
    tq = nsub * blk

    def body(thr_ref, first_ref, last_ref, q_ref, k_ref, vt_ref, fc_ref, fr_ref, o_ref, lse_ref):
        h = pl.program_id(0)
        i = pl.program_id(1)
        gap_needed = thr_ref[h, 0]
        f_tile = first_ref[h, i * nsub]
        j_start = lax.while_loop(lambda j: (j < i * nsub) & (f_tile - last_ref[h, j] < gap_needed),
                                 lambda j: j + 1, 0)
        r = lax.broadcasted_iota(jnp.int32, (blk, blk), 0)
        c = lax.broadcasted_iota(jnp.int32, (blk, blk), 1)
        qs = [q_ref[0, s * blk:(s + 1) * blk, :].astype(BF16) for s in range(nsub)]
        fqs = [fr_ref[0, i * nsub + s] for s in range(nsub)]

        def load_kv(j):
            off = pl.multiple_of(j * blk, blk)
            return k_ref[0, pl.ds(off, blk), :], vt_ref[0, j], fc_ref[0, pl.ds(off, blk), :]

        def tile(kv, s, carry, diagonal):
            kj, vtj, fk = kv
            m, l, acc = carry
            st = _raw_dot(kj, qs[s], 1, 1, False) + fqs[s] - fk
            if diagonal:
                st = jnp.where(r <= c, st, NEG_BIG)
            m_new = jnp.maximum(m, jnp.max(st, axis=0, keepdims=True))
            p = jnp.exp(st - m_new)
            alpha = jnp.exp(m - m_new)
            l = alpha * l + jnp.sum(p, axis=0, keepdims=True)
            acc = alpha * acc + _raw_dot(vtj, p, 1, 0, False)
            return m_new, l, acc

        starts = [j_start]
        for s in range(1, nsub):
            f_sub = first_ref[h, i * nsub + s]
            starts.append(lax.while_loop(lambda j, f_sub=f_sub: (j < i * nsub) & (f_sub - last_ref[h, j] < gap_needed),
                                         lambda j: j + 1, starts[-1]))
        starts.append(i * nsub)

        carry = tuple((jnp.full((1, blk), NEG_BIG, F32), jnp.zeros((1, blk), F32), jnp.zeros((hd, blk), F32))
                      for _ in range(nsub))
        for phase in range(nsub):
            def below_diagonal(j, carry, phase=phase):
                kv = load_kv(j)
                return tuple(tile(kv, s, carry[s], False) if s <= phase else carry[s] for s in range(nsub))

            carry = lax.fori_loop(starts[phase], starts[phase + 1], below_diagonal, carry)
        carry = list(carry)
        for d in range(nsub):
            kv = load_kv(i * nsub + d)
            for s in range(d, nsub):
                carry[s] = tile(kv, s, carry[s], s == d)
        for s, (m, l, acc) in enumerate(carry):
            o_ref[0, :, s * blk:(s + 1) * blk] = acc / l
            lse_ref[0, s] = m + jnp.log(l)

    vt = v.reshape(h_total, nb, blk, hd).transpose(0, 1, 3, 2).astype(BF16)
    return _call(
        body, name="fox_fwd", grid=(h_total, nb // nsub),
        in_specs=[SMEM, SMEM, SMEM,
                  pl.BlockSpec((1, tq, hd), lambda h, i: (h, i, 0)),
                  pl.BlockSpec((1, t_total, hd), lambda h, i: (h, 0, 0)),
                  pl.BlockSpec((1, nb, hd, blk), lambda h, i: (h, 0, 0, 0)),
                  pl.BlockSpec((1, t_total, 1), lambda h, i: (h, 0, 0)),
                  pl.BlockSpec((1, nb, 1, blk), lambda h, i: (h, 0, 0, 0))],
        out_specs=[pl.BlockSpec((1, hd, tq), lambda h, i: (h, 0, i)),
                   pl.BlockSpec((1, nsub, 1, blk), lambda h, i: (h, i, 0, 0))],
        out_shape=[jax.ShapeDtypeStruct((h_total, hd, t_total), F32), jax.ShapeDtypeStruct(f_row.shape, F32)],
        compiler_params=_cparams(("parallel", "parallel")),
    )(*tables, q, k.astype(BF16), vt, f_col, f_row)


def _attn_bwd_call(q, k, v, f_col, f_row, tables, lse_row, delta_row, do_blk):
    h_total, t_total, hd = q.shape
    blk = f_row.shape[-1]
    nb = t_total // blk
    nsub = _attn_sub_tiles(nb)
    tkv = nsub * blk

    def body(thr_ref, first_ref, last_ref, q_ref, do_ref, k_ref, v_ref, fc_ref, fr_ref, lse_ref, dl_ref,
             dq_ref, dk_ref, dv_ref, dfk_ref, dfq_ref):
        h = pl.program_id(0)
        j = pl.program_id(1)
        gap_needed = thr_ref[h, 0]
        f_tile = last_ref[h, j * nsub + nsub - 1]
        i_stop = lax.while_loop(lambda i: (i < nb) & (first_ref[h, jnp.minimum(i, nb - 1)] - f_tile >= gap_needed),
                                lambda i: i + 1, (j + 1) * nsub)

        @pl.when(j == 0)
        def _():
            dq_ref[...] = jnp.zeros_like(dq_ref)
            dfq_ref[...] = jnp.zeros_like(dfq_ref)

        ks = [k_ref[0, s * blk:(s + 1) * blk, :].astype(BF16) for s in range(nsub)]
        vs = [v_ref[0, s * blk:(s + 1) * blk, :].astype(BF16) for s in range(nsub)]
        fks = [fc_ref[0, s * blk:(s + 1) * blk, :] for s in range(nsub)]
        r = lax.broadcasted_iota(jnp.int32, (blk, blk), 0)
        c = lax.broadcasted_iota(jnp.int32, (blk, blk), 1)

        def q_step(i, accs, subs):
            off = pl.multiple_of(i * blk, blk)
            qi = q_ref[0, pl.ds(off, blk), :]
            doi = do_ref[0, i]
            fq, lse, dl = fr_ref[0, i], lse_ref[0, i], dl_ref[0, i]
            accs = list(accs)
            dq_i, dfq_i = None, None
            for s, diagonal in subs:
                dk, dv, dfk = accs[s]
                st = _raw_dot(ks[s], qi, 1, 1, False) + fq - fks[s] - lse
                if diagonal:
                    st = jnp.where(r <= c, st, NEG_BIG)
                pt = jnp.exp(st)
                dv = dv + _raw_dot(pt, doi, 1, 1, False)
                dst = pt * (_raw_dot(vs[s], doi, 1, 0, False) - dl)
                dk = dk + _raw_dot(dst, qi, 1, 0, False)
                dfk = dfk - jnp.sum(dst, axis=1, keepdims=True)
                accs[s] = (dk, dv, dfk)
                dq_s = _raw_dot(dst, ks[s], 0, 0, False)
                dfq_s = jnp.sum(dst, axis=0, keepdims=True)
                dq_i = dq_s if dq_i is None else dq_i + dq_s
                dfq_i = dfq_s if dfq_i is None else dfq_i + dfq_s
            dfq_ref[0, i] += dfq_i
            dq_ref[0, pl.ds(off, blk), :] += dq_i
            return tuple(accs)

        accs = tuple((jnp.zeros((blk, hd), F32), jnp.zeros((blk, hd), F32), jnp.zeros((blk, 1), F32))
                     for _ in range(nsub))
        for d in range(nsub):
            accs = q_step(j * nsub + d, accs, [(s, s == d) for s in range(d + 1)])
        accs = lax.fori_loop((j + 1) * nsub, i_stop,
                             lambda i, a: q_step(i, a, [(s, False) for s in range(nsub)]), accs)
        for s, (dk, dv, dfk) in enumerate(accs):
            dk_ref[0, s * blk:(s + 1) * blk, :] = dk
            dv_ref[0, s * blk:(s + 1) * blk, :] = dv
            dfk_ref[0, s * blk:(s + 1) * blk, :] = dfk

    full = pl.BlockSpec((1, t_total, hd), lambda h, j: (h, 0, 0))
    tile = pl.BlockSpec((1, tkv, hd), lambda h, j: (h, j, 0))
    col = pl.BlockSpec((1, tkv, 1), lambda h, j: (h, j, 0))
    rows = pl.BlockSpec((1, nb, 1, blk), lambda h, j: (h, 0, 0, 0))
    do_blocks = pl.BlockSpec((1, nb, hd, blk), lambda h, j: (h, 0, 0, 0))
    return _call(
        body, name="fox_bwd", grid=(h_total, nb // nsub),
        in_specs=[SMEM, SMEM, SMEM, full, do_blocks, tile, tile, col, rows, rows, rows],
        out_specs=[full, tile, tile, col, rows],
        out_shape=[jax.ShapeDtypeStruct(q.shape, F32), jax.ShapeDtypeStruct(q.shape, F32),
                   jax.ShapeDtypeStruct(q.shape, F32), jax.ShapeDtypeStruct(f_col.shape, F32),
                   jax.ShapeDtypeStruct(f_row.shape, F32)],
        compiler_params=_cparams(("parallel", "arbitrary")),
    )(*tables, q.astype(BF16), do_blk, k, v, f_col, f_row, lse_row, delta_row)


def _attn_delta_call(do_t, o_t):
    h_total, hd, t_total = o_t.shape
    tb = _tile(t_total, 2816, LANE)

    def body(do_ref, o_ref, d_ref):
        d_ref[0] = jnp.sum(do_ref[0] * o_ref[0], axis=0, keepdims=True)

    spec = pl.BlockSpec((1, hd, tb), lambda h, i: (h, 0, i))
    return _call(
        body, name="fox_delta", grid=(h_total, t_total // tb), in_specs=[spec, spec],
        out_specs=pl.BlockSpec((1, 1, tb), lambda h, i: (h, 0, i)),
        out_shape=jax.ShapeDtypeStruct((h_total, 1, t_total), F32),
        compiler_params=_cparams(("parallel", "parallel")),
    )(do_t, o_t)


@jax.custom_vjp
def fox_attention(q, k, v, f_col, f_row):
    return _attn_fwd_call(q, k, v, f_col, f_row, _attn_skip_tables(q, k, f_row))[0]


def _fox_fwd(q, k, v, f_col, f_row):
    tables = _attn_skip_tables(q, k, f_row)
    o_t, lse_row = _attn_fwd_call(q, k, v, f_col, f_row, tables)
    return o_t, (q, k, v, f_col, f_row, tables, o_t, lse_row)


def _fox_bwd(res, do_t):
    q, k, v, f_col, f_row, tables, o_t, lse_row = res
    h_total, t_total, hd = q.shape
    nb, blk = f_row.shape[1], f_row.shape[3]
    delta = _attn_delta_call(do_t, o_t).reshape(f_row.shape)
    do_blk = do_t.reshape(h_total, hd, nb, blk).transpose(0, 2, 1, 3).astype(BF16)
    grads = _attn_bwd_call(q, k, v, f_col, f_row, tables, lse_row, delta, do_blk)
    return tuple(g.astype(p.dtype) for g, p in zip(grads, (q, k, v, f_col, f_row)))


fox_attention.defvjp(_fox_fwd, _fox_bwd)


def _head_col(blk, h):
    lane = lax.broadcasted_iota(jnp.int32, blk.shape, 1)
    return jnp.sum(jnp.where(lane == h, blk, 0.0), axis=1, keepdims=True)


@jax.custom_vjp
def _cat2(a, b):
    return jnp.concatenate([a, b], axis=1)


_cat2.defvjp(lambda a, b: (_cat2(a, b), a.shape[1]), lambda na, ct: (ct[:, :na], ct[:, na:]))


@jax.custom_vjp
def _split2(x):
    half = x.shape[1] // 2
    return x[:, :half], x[:, half:]


_split2.defvjp(lambda x: (_split2(x), None), lambda _, cts: (jnp.concatenate(cts, axis=1),))


def _neumann_solve(m, b):
    x = b - _raw_dot(m, b, 1, 0, False)
    powers, steps = [m], 1
    while 2 * steps < GDN_CHUNK:
        powers.append(_raw_dot(powers[-1], powers[-1], 1, 0, False))
        x = x + _raw_dot(powers[-1], x, 1, 0, False)
        steps *= 2
    return x, powers


@jax.custom_vjp
def _unit_lower_solve(m, b):
    return _neumann_solve(m, b)[0]


def _unit_lower_solve_fwd(m, b):
    x, powers = _neumann_solve(m, b)
    return x, (powers, x)


def _unit_lower_solve_bwd(res, dx):
    powers, x = res
    db = dx - _raw_dot(powers[0], dx, 0, 0, False)
    for p in powers[1:]:
        db = db + _raw_dot(p, db, 0, 0, False)
    return -_raw_dot(db, x, 1, 1, False), db


_unit_lower_solve.defvjp(_unit_lower_solve_fwd, _unit_lower_solve_bwd)


@jax.custom_vjp
def _unit_lower_solve_known(m, b, x):
    return x


def _unit_lower_solve_known_bwd(res, dx):
    m, x = res
    powers, steps = [m], 1
    while 2 * steps < GDN_CHUNK:
        powers.append(_raw_dot(powers[-1], powers[-1], 1, 0, False))
        steps *= 2
    dm, db = _unit_lower_solve_bwd((powers, x), dx)
    return dm, db, jnp.zeros_like(x)


_unit_lower_solve_known.defvjp(lambda m, b, x: (x, (m, x)), _unit_lower_solve_known_bwd)


def _gdn_intra(h, q, k, v, b_blk, g_blk, uw_known=None):
    n = q.shape[0]
    b, g = _head_col(b_blk, h), _head_col(g_blk, h)
    r = lax.broadcasted_iota(jnp.int32, (n, n), 0)
    c = lax.broadcasted_iota(jnp.int32, (n, n), 1)
    same = (r // GDN_CHUNK) == (c // GDN_CHUNK)
    incl = same & (r >= c)
    g_row = jnp.sum(jnp.where(r == c, g, 0.0), axis=0, keepdims=True)
    big_g = jnp.sum(jnp.where(incl, g_row, 0.0), axis=1, keepdims=True)
    big_g_row = jnp.sum(jnp.where(same & (r <= c), g, 0.0), axis=0, keepdims=True)
    g_tot = jnp.sum(jnp.where(same, g_row, 0.0), axis=1, keepdims=True)
    dec = jnp.where(incl, jnp.exp(jnp.where(incl, big_g - big_g_row, 0.0)), 0.0)
    dec_strict = jnp.where(r > c, dec, 0.0)
    e_g = jnp.exp(big_g)
    kb = k * b
    m = _dot(kb, k, 1, 1) * dec_strict
    rs = lax.broadcasted_iota(jnp.int32, (n, GDN_CHUNK), 0)
    cs = lax.broadcasted_iota(jnp.int32, (n, GDN_CHUNK), 1)
    fold = jnp.where(rs % GDN_CHUNK == cs, 1.0, 0.0).astype(F32)
    aqk = _dot(_dot(q, k, 1, 1) * dec, fold, 1, 0, True)
    rhs = _cat2(v * b, kb * e_g)
    u, w = _split2(_unit_lower_solve(m, rhs) if uw_known is None else
                   _unit_lower_solve_known(m, rhs, jnp.concatenate(uw_known, axis=1)))
    lane = lax.broadcasted_iota(jnp.int32, b_blk.shape, 1)
    return u, w, q * e_g, k * jnp.exp(g_tot - big_g), aqk, jnp.where(lane == h, g_tot, 0.0)


def _gdn_rec(h, s, u, w, qg, kd, aqk, gl_blk):
    g_last = jnp.max(_head_col(gl_blk, h), axis=0, keepdims=True)
    big_u = u - _dot(w, s)
    o = _dot(qg, s) + _dot(aqk, big_u)
    s_next = s * jnp.exp(g_last) + _dot(kd, big_u, 0, 0)
    return o, s_next


GDN_TOK_BLK = 256


def _gdn_layout(t_total, rev):
    tb = _tile(t_total, GDN_TOK_BLK, GDN_CHUNK)
    cb, nblk = tb // GDN_CHUNK, t_total // tb
    pos = (lambda i: nblk - 1 - i) if rev else (lambda i: i)
    specs = dict(
        tok=pl.BlockSpec((tb, GDN_W), lambda i: (pos(i), 0)),
        q=pl.BlockSpec((tb, GDN_W), lambda i: (pos(i), 0)),
        k=pl.BlockSpec((tb, GDN_W), lambda i: (pos(i), 1)),
        v=pl.BlockSpec((tb, GDN_W), lambda i: (pos(i), 2)),
        qkv=pl.BlockSpec((tb, 3 * GDN_W), lambda i: (pos(i), 0)),
        gate=pl.BlockSpec((tb, GDN_HEADS), lambda i: (pos(i), 0)),
        aqk=pl.BlockSpec((GDN_HEADS, tb, GDN_CHUNK), lambda i: (0, pos(i), 0)),
        state=pl.BlockSpec((GDN_HEADS, cb, GDN_HD, GDN_HD), lambda i: (0, pos(i), 0, 0)))
    return cb, nblk, specs


def _gdn_shapes(t_total):
    n_chunks = t_total // GDN_CHUNK
    return dict(tok=jax.ShapeDtypeStruct((t_total, GDN_W), BF16),
                out=jax.ShapeDtypeStruct((t_total, GDN_W), F32),
                gate=jax.ShapeDtypeStruct((t_total, GDN_HEADS), F32),
                aqk=jax.ShapeDtypeStruct((GDN_HEADS, t_total, GDN_CHUNK), F32),
                state=jax.ShapeDtypeStruct((GDN_HEADS, n_chunks, GDN_HD, GDN_HD), F32))


def _chunk_rows(ci):
    return pl.ds(pl.multiple_of(ci * GDN_CHUNK, GDN_CHUNK), GDN_CHUNK)


def _head_cols(h):
    return pl.ds(h * GDN_HD, GDN_HD)


def _gdn_intra_fwd_call(qkv, b, g):
    cb, nblk, sp = _gdn_layout(qkv.shape[0], False)
    sh = _gdn_shapes(qkv.shape[0])

    def body(q_ref, k_ref, v_ref, b_ref, g_ref, u_ref, w_ref, qg_ref, kd_ref, aqk_ref, gl_ref):
        b_blk, g_blk = b_ref[...], g_ref[...]
        gl = jnp.zeros(b_blk.shape, F32)
        for h in range(GDN_HEADS):
            cols = _head_cols(h)
            u, w, qg, kd, aqk, gl_h = _gdn_intra(h, q_ref[:, cols], k_ref[:, cols], v_ref[:, cols], b_blk, g_blk)
            u_ref[:, cols] = u.astype(u_ref.dtype)
            w_ref[:, cols] = w.astype(w_ref.dtype)
            qg_ref[:, cols] = qg.astype(qg_ref.dtype)
            kd_ref[:, cols] = kd.astype(kd_ref.dtype)
            aqk_ref[h] = aqk
            gl = gl + gl_h
        gl_ref[...] = gl

    return _call(
        body, name="gdn_intra_fwd", grid=(nblk,),
        in_specs=[sp["q"], sp["k"], sp["v"]] + [sp["gate"]] * 2,
        out_specs=[sp["tok"]] * 4 + [sp["aqk"], sp["gate"]],
        out_shape=[sh["tok"]] * 4 + [sh["aqk"], sh["gate"]],
        compiler_params=_cparams(("parallel",)),
    )(qkv, qkv, qkv, b, g)


def _gdn_intra_bwd_call(qkv, b, g, u, w, du, dw, dqg, dkd, daqk, dgl):
    cb, nblk, sp = _gdn_layout(qkv.shape[0], False)
    sh = _gdn_shapes(qkv.shape[0])

    def body(q_ref, k_ref, v_ref, b_ref, g_ref, u_ref, w_ref, du_ref, dw_ref, dqg_ref, dkd_ref, daqk_ref, dgl_ref,
             dqkv_ref, db_ref, dg_ref):
        b_blk, g_blk, dgl = b_ref[...], g_ref[...], dgl_ref[...]
        db = jnp.zeros(b_blk.shape, F32)
        dg = jnp.zeros(b_blk.shape, F32)
        for h in range(GDN_HEADS):
            cols = _head_cols(h)
            known = (u_ref[:, cols].astype(F32), w_ref[:, cols].astype(F32))
            _, vjp = jax.vjp(functools.partial(_gdn_intra, h, uw_known=known),
                             q_ref[:, cols], k_ref[:, cols], v_ref[:, cols], b_blk, g_blk)
            dq, dk, dv, db_h, dg_h = vjp((*[r[:, cols].astype(F32) for r in (du_ref, dw_ref, dqg_ref, dkd_ref)],
                                          daqk_ref[h], dgl))
            dqkv_ref[:, pl.ds(h * GDN_HD, GDN_HD)] = dq
            dqkv_ref[:, pl.ds(GDN_W + h * GDN_HD, GDN_HD)] = dk
            dqkv_ref[:, pl.ds(2 * GDN_W + h * GDN_HD, GDN_HD)] = dv
            db = db + db_h
            dg = dg + dg_h
        db_ref[...] = db
        dg_ref[...] = dg

    return _call(
        body, name="gdn_intra_bwd", grid=(nblk,),
        in_specs=[sp["q"], sp["k"], sp["v"]] + [sp["gate"]] * 2 + [sp["tok"]] * 6 + [sp["aqk"], sp["gate"]],
        out_specs=[sp["qkv"]] + [sp["gate"]] * 2,
        out_shape=[jax.ShapeDtypeStruct(qkv.shape, F32)] + [sh["gate"]] * 2,
        compiler_params=_cparams(("parallel",)),
    )(qkv, qkv, qkv, b, g, u, w, du, dw, dqg, dkd, daqk, dgl)


def _gdn_rec_fwd_call(u, w, qg, kd, aqk, gl):
    cb, nblk, sp = _gdn_layout(u.shape[0], False)
    sh = _gdn_shapes(u.shape[0])

    def body(u_ref, w_ref, qg_ref, kd_ref, aqk_ref, gl_ref, o_ref, s_all_ref, s_ref):
        @pl.when(pl.program_id(0) == 0)
        def _():
            s_ref[...] = jnp.zeros_like(s_ref)

        def chunk(ci, carry):
            rows = _chunk_rows(ci)
            gl_row = gl_ref[rows, :]
            states = [s_ref[h] for h in range(GDN_HEADS)]
            res = [_gdn_rec(h, states[h], *[r[rows, _head_cols(h)].astype(F32) for r in (u_ref, w_ref, qg_ref, kd_ref)],
                            aqk_ref[h, rows, :], gl_row)
                   for h in range(GDN_HEADS)]
            for h, (o, s_next) in enumerate(res):
                s_all_ref[h, ci] = states[h]
                o_ref[rows, _head_cols(h)] = o
                s_ref[h] = s_next
            return carry

        lax.fori_loop(0, cb, chunk, 0)

    return _call(
        body, name="gdn_rec_fwd", grid=(nblk,),
        in_specs=[sp["tok"]] * 4 + [sp["aqk"], sp["gate"]],
        out_specs=[sp["tok"], sp["state"]], out_shape=[sh["out"], sh["state"]],
        scratch_shapes=[pltpu.VMEM((GDN_HEADS, GDN_HD, GDN_HD), F32)],
        compiler_params=_cparams(("arbitrary",)),
    )(u, w, qg, kd, aqk, gl)


def _gdn_rec_bwd_call(u, w, qg, kd, aqk, gl, s_all, do):
    cb, nblk, sp = _gdn_layout(u.shape[0], True)
    sh = _gdn_shapes(u.shape[0])

    def body(u_ref, w_ref, qg_ref, kd_ref, aqk_ref, gl_ref, s_all_ref, do_ref,
             du_ref, dw_ref, dqg_ref, dkd_ref, daqk_ref, dgl_ref, ds_ref):
        @pl.when(pl.program_id(0) == 0)
        def _():
            ds_ref[...] = jnp.zeros_like(ds_ref)

        def chunk(step, carry):
            ci = cb - 1 - step
            rows = _chunk_rows(ci)
            gl_row = gl_ref[rows, :]
            res = []
            for h in range(GDN_HEADS):
                cols = _head_cols(h)
                _, vjp = jax.vjp(functools.partial(_gdn_rec, h), s_all_ref[h, ci],
                                 *[r[rows, cols].astype(F32) for r in (u_ref, w_ref, qg_ref, kd_ref)],
                                 aqk_ref[h, rows, :], gl_row)
                res.append(vjp((do_ref[rows, cols], ds_ref[h])))
            dgl = jnp.zeros((GDN_CHUNK, GDN_HEADS), F32)
            for h, (ds, du, dw, dqg, dkd, daqk, dgl_h) in enumerate(res):
                cols = _head_cols(h)
                ds_ref[h] = ds
                du_ref[rows, cols] = du.astype(du_ref.dtype)
                dw_ref[rows, cols] = dw.astype(dw_ref.dtype)
                dqg_ref[rows, cols] = dqg.astype(dqg_ref.dtype)
                dkd_ref[rows, cols] = dkd.astype(dkd_ref.dtype)
                daqk_ref[h, rows, :] = daqk
                dgl = dgl + dgl_h
            dgl_ref[rows, :] = dgl
            return carry

        lax.fori_loop(0, cb, chunk, 0)

    return _call(
        body, name="gdn_rec_bwd", grid=(nblk,),
        in_specs=[sp["tok"]] * 4 + [sp["aqk"], sp["gate"], sp["state"], sp["tok"]],
        out_specs=[sp["tok"]] * 4 + [sp["aqk"], sp["gate"]],
        out_shape=[sh["tok"]] * 4 + [sh["aqk"], sh["gate"]],
        scratch_shapes=[pltpu.VMEM((GDN_HEADS, GDN_HD, GDN_HD), F32)],
        compiler_params=_cparams(("arbitrary",)),
    )(u, w, qg, kd, aqk, gl, s_all, do)


@jax.custom_vjp
def gdn_intra(qkv, b, g):
    return tuple(_gdn_intra_fwd_call(qkv, b, g))


def _gdn_intra_fwd(*a):
    outs = gdn_intra(*a)
    return outs, a + (outs[0], outs[1])


gdn_intra.defvjp(_gdn_intra_fwd, lambda res, cts: tuple(_gdn_intra_bwd_call(*res, *cts)))


@jax.custom_vjp
def gdn_rec(u, w, qg, kd, aqk, gl):
    return _gdn_rec_fwd_call(u, w, qg, kd, aqk, gl)[0]


def _gdn_rec_fwd(*a):
    o, s_all = _gdn_rec_fwd_call(*a)
    return o, a + (s_all,)


gdn_rec.defvjp(_gdn_rec_fwd, lambda res, do: tuple(_gdn_rec_bwd_call(*res, do)))


def gated_delta(qkv, b, g):
    return gdn_rec(*gdn_intra(qkv, b, g))


def _loss_call(y, tgt, first, last):
    r_total, d = y.shape
    tm = _tile(r_total, 256, 8)

    def body(y_ref, t_ref, loss_ref, dy_ref):
        i = pl.program_id(0)

        @pl.when(i == 0)
        def _():
            loss_ref[...] = jnp.zeros_like(loss_ref)

        row = lax.broadcasted_iota(jnp.int32, (tm, d), 0) + i * tm
        err = jnp.where((row >= first) & (row < last), y_ref[...] - t_ref[...], 0.0)
        dy_ref[...] = err * (1.0 / d)
        part = jnp.sum(jnp.sum(err * err, axis=1, keepdims=True), axis=0, keepdims=True) * (0.5 / d)
        loss_ref[...] += jnp.broadcast_to(part, loss_ref.shape)

    return _call(
        body, name="loss_head", grid=(r_total // tm,),
        in_specs=[pl.BlockSpec((tm, d), lambda i: (i, 0))] * 2,
        out_specs=[pl.BlockSpec((8, LANE), lambda i: (0, 0)), pl.BlockSpec((tm, d), lambda i: (i, 0))],
        out_shape=[jax.ShapeDtypeStruct((8, LANE), F32), jax.ShapeDtypeStruct(y.shape, F32)],
        compiler_params=_cparams(("arbitrary",)),
    )(y, tgt)


def make_loss(first, last):
    @jax.custom_vjp
    def op(y, tgt):
        return _loss_call(y, tgt, first, last)[0][0, 0]

    def fwd(y, tgt):
        loss, dy = _loss_call(y, tgt, first, last)
        return loss[0, 0], (dy,)

    def bwd(res, ct):
        return res[0] * ct, jnp.zeros_like(res[0])

    op.defvjp(fwd, bwd)
    return op


def _pad_rows8(w):
    return jnp.concatenate([w, jnp.zeros((8 - w.shape[0], w.shape[1]), w.dtype)], axis=0)


def local_loss(wts, x, tgt):
    seq = x.shape[0]
    n_tok = N_META + seq
    t_pad = -(-n_tok // ROW_ALIGN) * ROW_ALIGN
    depth = wts["norm1_g"].shape[0]
    blk = _tile(t_pad, ATT_BLK, LANE)
    nb = t_pad // blk
    tm = _tile(t_pad, 256, 8)

    rms = rowop(_f_rmsnorm, "rmsnorm", (D_MODEL,), tm, out_dtypes=[BF16])
    qnorm = rowop(_f_qnorm, "fox_q_norm", (FOX_HD,), _tile(FOX_HEADS * t_pad, 2048, 8))
    knorm = rowop(_f_rmsnorm, "fox_k_norm", (FOX_HD,), _tile(FOX_HEADS * t_pad, 2048, 8))
    logsig = rowop(_f_logsig, "fox_log_forget", (FOX_HEADS,), tm)
    gdn_act = rowop([_f_gdn_q] * GDN_HEADS + [_f_gdn_k] * GDN_HEADS + [_f_gdn_v] * GDN_HEADS, "gdn_qkv_act",
                    (GDN_HD,), tm)
    gates = rowop(_f_gdn_gates, "gdn_gates", (GDN_HEADS, GDN_HEADS), tm)
    gdn_out = rowop([_f_gdn_out] * GDN_HEADS, "gdn_out_norm", (GDN_HD,), tm, out_dtypes=[BF16])
    merge = rowop(_f_merge, "branch_merge", (D_MODEL,), tm, out_dtypes=[BF16])
    residual = rowop(_f_residual, "residual_add", (D_MODEL,), tm, bc=(2,))
    residual_norm = rowop(_f_residual_norm, "residual_add_norm", (D_MODEL, D_MODEL), tm, bc=(2,),
                          out_dtypes=[F32, BF16])
    keep = (jnp.arange(t_pad)[:, None] < n_tok).astype(F32)
    conv4 = make_dwconv(GDN_CONV, BF16)
    conv3 = make_dwconv(FFN_CONV)
    loss_op = make_loss(N_META, n_tok)

    zeros = jnp.zeros((t_pad - n_tok, D_MODEL), F32)
    h_res = jnp.concatenate([wts["meta_tokens"], x, zeros], axis=0)
    tgt_rows = jnp.concatenate([jnp.zeros((N_META, D_MODEL), F32), tgt, zeros], axis=0)

    def heads(a):
        return a.reshape(t_pad, FOX_HEADS, FOX_HD).transpose(1, 0, 2).reshape(FOX_HEADS * t_pad, FOX_HD)

    h = rms((h_res,), (wts["norm1_g"][0][None],))[0]
    for l in range(depth):
        proj = mm_bf16(h, wts["w_in"][l])
        gate_logits = mm(h, jnp.concatenate([wts["w_in"][l][:, 1536:1536 + LANE],
                                             wts["w_in"][l][:, 4736:4736 + LANE]], axis=1))
        qn = qnorm((heads(proj[:, 0:512]),), (wts["fox_q_norm_g"][l][None],))[0]
        kn = knorm((heads(proj[:, 512:1024]),), (wts["fox_k_norm_g"][l][None],))[0]
        vh = heads(proj[:, 1024:1536])
        log_f = logsig((gate_logits[:, 0:FOX_HEADS],), (wts["fox_f_bias"][l][None],))[0]
        f_cum = cumsum_lanes(log_f.T)
        o_a = fox_attention(qn.reshape(FOX_HEADS, t_pad, FOX_HD), kn.reshape(FOX_HEADS, t_pad, FOX_HD),
                            vh.reshape(FOX_HEADS, t_pad, FOX_HD), f_cum[:, :, None],
                            f_cum.reshape(FOX_HEADS, nb, 1, blk))
        y_a = mm_bf16(o_a.transpose(2, 0, 1).reshape(t_pad, FOX_W).astype(BF16), wts["w_branch_a"][l])
        cv = conv4(proj[:, 1664:4736], _pad_rows8(wts["gdn_conv_w"][l]))
        qkv = gdn_act((cv,), ())[0]
        beta, gdec = gates((gate_logits[:, LANE:LANE + GDN_HEADS], gate_logits[:, LANE + GDN_HEADS:LANE + 2 * GDN_HEADS]),
                           (wts["gdn_a_log"][l][None], wts["gdn_dt_bias"][l][None]))
        o_b = gated_delta(qkv, beta, gdec)
        o_b = gdn_out((o_b, proj[:, 4864:5888]), (wts["gdn_norm_g"][l][None],))[0]
        y_b = mm_bf16(o_b, wts["w_branch_b"][l])
        mixed = merge((proj[:, 5888:6912], proj[:, 6912:7936], y_a, y_b), ())[0]
        h_res, h = residual_norm((h_res, mm(mixed, wts["w_out"][l]), keep), (wts["norm2_g"][l][None],))
        up = conv3(mm_bf16(h, wts["w_up"][l]), _pad_rows8(wts["ffn_conv_w"][l]))
        act = glu(up)
        down = mm(act, wts["w_down"][l])
        if l + 1 < depth:
            h_res, h = residual_norm((h_res, down, keep), (wts["norm1_g"][l + 1][None],))
        else:
            h_res = residual((h_res, down, keep), ())[0]
    return loss_op(h_res, tgt_rows)


def pad_w_in(w):
    parts, pos = [], 0
    for src, width, dst in IN_SEGS:
        if dst > pos:
            parts.append(jnp.zeros(w.shape[:-1] + (dst - pos,), w.dtype))
        parts.append(w[..., src:src + width])
        pos = dst + width
    parts.append(jnp.zeros(w.shape[:-1] + (D_IN_PAD - pos,), w.dtype))
    return jnp.concatenate(parts, axis=-1)


def unpad_w_in(w):
    return jnp.concatenate([w[..., dst:dst + width] for _, width, dst in IN_SEGS], axis=-1)


ANY = pl.BlockSpec(memory_space=pl.ANY)
N_CHIPS = 4
N_DEV = 8
COMM_COLS = 1024
COMM_ROW_ALIGN = 512
COMM_ROW_ALIGN_SMALL = 32


def _place():
    return lax.axis_index("x"), lax.axis_index("y"), lax.axis_index("c")


def _other_chips(x, y):
    return [(1 - x, y), (x, 1 - y), (1 - x, 1 - y)]


def _remote(src, dst, send_sem, recv_sem, dev):
    return pltpu.make_async_remote_copy(src_ref=src, dst_ref=dst, send_sem=send_sem, recv_sem=recv_sem,
                                        device_id=dev, device_id_type=MESH)


def chip_all_gather(buf):
    rows, cols = buf.shape
    half = rows // 2

    def body(x_ref, out_ref, send_sems, recv_sems, pass_send, pass_recv):
        x, y, c = _place()
        me = 2 * x + y
        mine, other = pl.ds(c * half, half), pl.ds((1 - c) * half, half)
        sibling = (x, y, 1 - c)
        chips = _other_chips(x, y)
        started = []
        for k, (px, py) in enumerate(chips):
            cp = _remote(x_ref.at[mine], out_ref.at[me, mine], send_sems.at[k], recv_sems.at[k], (px, py, c))
            cp.start()
            started.append(cp)
        for k, (px, py) in enumerate(chips):
            landed = out_ref.at[2 * px + py, mine]
            _remote(landed, landed, send_sems.at[k], recv_sems.at[k], (px, py, c)).wait_recv()
            cp = _remote(landed, landed, pass_send.at[k], pass_recv.at[k], sibling)
            cp.start()
            started.append(cp)
        for k, (px, py) in enumerate(chips):
            passed = out_ref.at[2 * px + py, other]
            _remote(passed, passed, pass_send.at[k], pass_recv.at[k], sibling).wait_recv()
        for cp in started:
            cp.wait_send()

    got = _call(
        body, name="chip_all_gather", in_specs=[ANY], out_specs=ANY,
        out_shape=jax.ShapeDtypeStruct((N_CHIPS, rows, cols), buf.dtype),
        scratch_shapes=[pltpu.SemaphoreType.DMA((3,)), pltpu.SemaphoreType.DMA((3,)),
                        pltpu.SemaphoreType.DMA((3,)), pltpu.SemaphoreType.DMA((3,))],
    )(buf)
    me = 2 * lax.axis_index("x") + lax.axis_index("y")
    return lax.dynamic_update_slice(got, buf[None], (me, 0, 0))


def sibling_swap_halves(g4):
    n, rows, cols = g4.shape
    half = rows // 2

    def body(g_ref, got_ref, send_sem, recv_sem):
        x, y, c = _place()
        cp = _remote(g_ref.at[:, pl.ds((1 - c) * half, half), :], got_ref, send_sem, recv_sem, (x, y, 1 - c))
        cp.start()
        cp.wait()

    return _call(
        body, name="sibling_swap_halves", in_specs=[ANY], out_specs=ANY,
        out_shape=jax.ShapeDtypeStruct((n, half, cols), g4.dtype),
        scratch_shapes=[pltpu.SemaphoreType.DMA, pltpu.SemaphoreType.DMA],
    )(g4)


def add_own_half(g4, got, c):
    n, rows, cols = g4.shape
    half = rows // 2
    tm = _tile(half, 256, 16)
    nt = half // tm

    def body(c_ref, a_ref, b_ref, o_ref):
        o_ref[...] = (a_ref[...] + b_ref[...]).astype(o_ref.dtype)

    return _call(
        body, name="add_own_half",
        grid_spec=pltpu.PrefetchScalarGridSpec(
            num_scalar_prefetch=1, grid=(n, nt),
            in_specs=[pl.BlockSpec((1, tm, cols), lambda j, i, c_ref: (j, c_ref[0] * nt + i, 0)),
                      pl.BlockSpec((1, tm, cols), lambda j, i, c_ref: (j, i, 0))],
            out_specs=pl.BlockSpec((1, tm, cols), lambda j, i, c_ref: (j, i, 0))),
        out_shape=jax.ShapeDtypeStruct(got.shape, BF16),
        compiler_params=_cparams(("parallel", "parallel")),
    )(c.reshape(1).astype(jnp.int32), g4, got)


def chip_scatter(p4):
    n, rows, cols = p4.shape

    def body(p_ref, out_ref, send_sems, recv_sems):
        x, y, c = _place()
        me = 2 * x + y
        chips = _other_chips(x, y)
        started = []
        for k, (px, py) in enumerate(chips):
            cp = _remote(p_ref.at[2 * px + py], out_ref.at[me], send_sems.at[k], recv_sems.at[k], (px, py, c))
            cp.start()
            started.append(cp)
        for k, (px, py) in enumerate(chips):
            landed = out_ref.at[2 * px + py]
            _remote(landed, landed, send_sems.at[k], recv_sems.at[k], (px, py, c)).wait_recv()
        for cp in started:
            cp.wait_send()

    got = _call(
        body, name="chip_scatter", in_specs=[ANY], out_specs=ANY,
        out_shape=jax.ShapeDtypeStruct(p4.shape, p4.dtype),
        scratch_shapes=[pltpu.SemaphoreType.DMA((3,)), pltpu.SemaphoreType.DMA((3,))],
    )(p4)
    me = 2 * lax.axis_index("x") + lax.axis_index("y")
    return lax.dynamic_update_slice(got, lax.dynamic_slice_in_dim(p4, me, 1, axis=0), (me, 0, 0))


def sum_slots(a):
    n, rows, cols = a.shape
    tm = _tile(rows, 256, 16) if rows % 16 == 0 else rows

    def body(a_ref, o_ref):
        acc = a_ref[0].astype(F32)
        for k in range(1, n):
            acc = acc + a_ref[k].astype(F32)
        o_ref[...] = acc

    return _call(
        body, name="sum_slots_%d" % n, grid=(rows // tm,),
        in_specs=[pl.BlockSpec((n, tm, cols), lambda i: (0, i, 0))],
        out_specs=pl.BlockSpec((tm, cols), lambda i: (i, 0)),
        out_shape=jax.ShapeDtypeStruct((rows, cols), F32),
        compiler_params=_cparams(("parallel",)),
    )(a)


def sibling_join(s):
    half, cols = s.shape

    def body(s_ref, got_ref, send_sem, recv_sem):
        x, y, c = _place()
        cp = _remote(s_ref, got_ref, send_sem, recv_sem, (x, y, 1 - c))
        cp.start()
        cp.wait()

    got = _call(
        body, name="sibling_join", in_specs=[ANY], out_specs=ANY,
        out_shape=jax.ShapeDtypeStruct(s.shape, s.dtype),
        scratch_shapes=[pltpu.SemaphoreType.DMA, pltpu.SemaphoreType.DMA],
    )(s)
    c = lax.axis_index("c")
    out = jnp.zeros((2 * half, cols), s.dtype)
    out = lax.dynamic_update_slice(out, s, (c * half, 0))
    return lax.dynamic_update_slice(out, got, ((1 - c) * half, 0))


def all_devices_gather(buf):
    rows, cols = buf.shape

    def body(b_ref, out_ref, send_sems, recv_sems, local_sem):
        x, y, c = _place()
        me = 4 * x + 2 * y + c
        local = pltpu.make_async_copy(b_ref, out_ref.at[me], local_sem)
        local.start()
        peers = [((x + dx) % 2, (y + dy) % 2, (c + dc) % 2)
                 for dx in (0, 1) for dy in (0, 1) for dc in (0, 1) if dx + dy + dc > 0]
        started = []
        for k, peer in enumerate(peers):
            cp = _remote(b_ref, out_ref.at[me], send_sems.at[k], recv_sems.at[k], peer)
            cp.start()
            started.append(cp)
        for k, (px, py, pc) in enumerate(peers):
            landed = out_ref.at[4 * px + 2 * py + pc]
            _remote(landed, landed, send_sems.at[k], recv_sems.at[k], (px, py, pc)).wait_recv()
        for cp in started:
            cp.wait_send()
        local.wait()

    return _call(
        body, name="all_devices_gather", in_specs=[ANY], out_specs=ANY,
        out_shape=jax.ShapeDtypeStruct((N_DEV, rows, cols), buf.dtype),
        scratch_shapes=[pltpu.SemaphoreType.DMA((7,)), pltpu.SemaphoreType.DMA((7,)), pltpu.SemaphoreType.DMA],
    )(buf)


def adamw(w, g, m, v):
    shape = w.shape
    w2, g2, m2, v2 = [a.reshape(-1, shape[-1]) for a in (w, g, m, v)]
    rows, cols = w2.shape
    tm = _tile(rows, 256, 8) if rows % 8 == 0 else rows

    def body(w_ref, g_ref, m_ref, v_ref, d_ref, nm_ref, nv_ref):
        gv = g_ref[...]
        nm = ADAM_B1 * m_ref[...] + (1.0 - ADAM_B1) * gv
        nv = ADAM_B2 * v_ref[...] + (1.0 - ADAM_B2) * (gv * gv)
        m_hat = nm / (1.0 - ADAM_B1 ** ADAM_STEP)
        v_hat = nv / (1.0 - ADAM_B2 ** ADAM_STEP)
        d_ref[...] = -ADAM_LR * (m_hat / (jnp.sqrt(v_hat) + ADAM_EPS) + ADAM_WD * w_ref[...])
        nm_ref[...] = nm
        nv_ref[...] = nv

    spec = pl.BlockSpec((tm, cols), lambda i: (i, 0))
    outs = _call(
        body, name="adamw", grid=(rows // tm,), in_specs=[spec] * 4, out_specs=[spec] * 3,
        out_shape=[jax.ShapeDtypeStruct((rows, cols), F32)] * 3,
        compiler_params=_cparams(("parallel",)),
    )(w2, g2, m2, v2)
    return [o.reshape(shape) for o in outs]


WEIGHTS = ("meta_tokens", "norm1_g", "w_in", "fox_f_bias", "fox_q_norm_g", "fox_k_norm_g", "gdn_conv_w",
           "gdn_a_log", "gdn_dt_bias", "gdn_norm_g", "w_branch_a", "w_branch_b", "w_out", "norm2_g", "w_up",
           "ffn_conv_w", "w_down")
SHARD_AXIS = {"meta_tokens": -1, "w_in": -1, "gdn_conv_w": -1, "w_branch_a": -1, "w_branch_b": -2, "w_out": -2,
              "w_up": -1, "ffn_conv_w": -1, "w_down": -2}
MATMUL_WEIGHTS = ("w_in", "w_branch_a", "w_branch_b", "w_out", "w_up", "w_down")
SMALL_SHARDED = ("meta_tokens", "gdn_conv_w", "ffn_conv_w")
REPLICATED = tuple(n for n in WEIGHTS if n not in SHARD_AXIS)


def _pack(arrays, dtype, row_align):
    flat = jnp.concatenate([a.reshape(-1).astype(dtype) for a in arrays])
    block = row_align * COMM_COLS
    total = -(-flat.shape[0] // block) * block
    flat = jnp.concatenate([flat, jnp.zeros((total - flat.shape[0],), dtype)])
    return flat.reshape(-1, COMM_COLS)


def _unpack(buf, shapes):
    flat, out, pos = buf.reshape(-1), [], 0
    for shape in shapes:
        size = 1
        for d in shape:
            size *= d
        out.append(flat[pos:pos + size].reshape(shape))
        pos += size
    return out


def _gather_full(shards, names, dtype, row_align):
    got = chip_all_gather(_pack([shards[n] for n in names], dtype, row_align))
    per_chip = [_unpack(got[j], [shards[n].shape for n in names]) for j in range(N_CHIPS)]
    return {n: jnp.concatenate([per_chip[j][i] for j in range(N_CHIPS)], axis=SHARD_AXIS[n]).astype(F32)
            for i, n in enumerate(names)}


def _shard_of(full, name, j):
    axis = SHARD_AXIS[name] % full.ndim
    size = full.shape[axis] // N_CHIPS
    return lax.slice_in_dim(full, j * size, (j + 1) * size, axis=axis)


def kernel(x, meta_tokens, norm1_g, w_in, fox_f_bias, fox_q_norm_g, fox_k_norm_g, gdn_conv_w, gdn_a_log, gdn_dt_bias, gdn_norm_g, w_branch_a, w_branch_b, w_out, norm2_g, w_up, ffn_conv_w, w_down, loss_target, m_meta_tokens, m_norm1_g, m_w_in, m_fox_f_bias, m_fox_q_norm_g, m_fox_k_norm_g, m_gdn_conv_w, m_gdn_a_log, m_gdn_dt_bias, m_gdn_norm_g, m_w_branch_a, m_w_branch_b, m_w_out, m_norm2_g, m_w_up, m_ffn_conv_w, m_w_down, v_meta_tokens, v_norm1_g, v_w_in, v_fox_f_bias, v_fox_q_norm_g, v_fox_k_norm_g, v_gdn_conv_w, v_gdn_a_log, v_gdn_dt_bias, v_gdn_norm_g, v_w_branch_a, v_w_branch_b, v_w_out, v_norm2_g, v_w_up, v_ffn_conv_w, v_w_down):
    w_loc = dict(zip(WEIGHTS, (meta_tokens, norm1_g, w_in, fox_f_bias, fox_q_norm_g, fox_k_norm_g, gdn_conv_w,
                               gdn_a_log, gdn_dt_bias, gdn_norm_g, w_branch_a, w_branch_b, w_out, norm2_g, w_up,
                               ffn_conv_w, w_down)))
    m_loc = dict(zip(WEIGHTS, (m_meta_tokens, m_norm1_g, m_w_in, m_fox_f_bias, m_fox_q_norm_g, m_fox_k_norm_g,
                               m_gdn_conv_w, m_gdn_a_log, m_gdn_dt_bias, m_gdn_norm_g, m_w_branch_a, m_w_branch_b,
                               m_w_out, m_norm2_g, m_w_up, m_ffn_conv_w, m_w_down)))
    v_loc = dict(zip(WEIGHTS, (v_meta_tokens, v_norm1_g, v_w_in, v_fox_f_bias, v_fox_q_norm_g, v_fox_k_norm_g,
                               v_gdn_conv_w, v_gdn_a_log, v_gdn_dt_bias, v_gdn_norm_g, v_w_branch_a, v_w_branch_b,
                               v_w_out, v_norm2_g, v_w_up, v_ffn_conv_w, v_w_down)))
    c = lax.axis_index("c")

    full = {n: w_loc[n] for n in REPLICATED}
    full.update(_gather_full(w_loc, MATMUL_WEIGHTS, BF16, COMM_ROW_ALIGN))
    full.update(_gather_full(w_loc, SMALL_SHARDED, F32, COMM_ROW_ALIGN_SMALL))
    full["w_in"] = pad_w_in(full["w_in"])

    loss, (g_full, g_x) = jax.value_and_grad(local_loss, argnums=(0, 1))(full, x[0], loss_target[0])
    g_full = dict(g_full)
    g_full["w_in"] = unpad_w_in(g_full["w_in"])

    sharded = MATMUL_WEIGHTS + SMALL_SHARDED
    g4 = jnp.stack([_pack([_shard_of(g_full[n], n, j) for n in sharded], F32, COMM_ROW_ALIGN)
                    for j in range(N_CHIPS)])
    pair_sum = add_own_half(g4, sibling_swap_halves(g4), c)
    g_shard = sibling_join(sum_slots(chip_scatter(pair_sum)))
    grads = dict(zip(sharded, _unpack(g_shard, [w_loc[n].shape for n in sharded])))
    g_rep = sum_slots(all_devices_gather(_pack([g_full[n] for n in REPLICATED], F32, 8)))
    grads.update(zip(REPLICATED, _unpack(g_rep, [w_loc[n].shape for n in REPLICATED])))

    loss = lax.psum(loss, ("x", "y", "c"))
    upd = {n: adamw(w_loc[n], grads[n], m_loc[n], v_loc[n]) for n in WEIGHTS}
    return (loss, g_x[None], *[grads[n] for n in WEIGHTS], *[upd[n][0] for n in WEIGHTS],
            *[upd[n][1] for n in WEIGHTS], *[upd[n][2] for n in WEIGHTS])
```

```python
import functools

import jax
import jax.numpy as jnp
from jax import lax
from jax.experimental import pallas as pl
from jax.experimental.pallas import tpu as pltpu

F32 = jnp.float32
BF16 = jnp.bfloat16
HI = lax.Precision.HIGHEST
MESH = pl.DeviceIdType.MESH

D_MODEL = 1024
N_META = 16
EPS = 1e-6
FOX_HEADS, FOX_HD = 8, 64
FOX_W = FOX_HEADS * FOX_HD
GDN_HEADS, GDN_HD, GDN_CHUNK, GDN_CONV = 8, 128, 64, 4
GDN_W = GDN_HEADS * GDN_HD
D_FF = 2816
FFN_CONV = 3
D_IN = 7704
D_IN_PAD = 8192
IN_SEGS = ((0, 1536, 0), (1536, 8, 1536), (1544, 3072, 1664), (4616, 16, 4736), (4632, 1024, 4864), (5656, 2048, 5888))
ROW_ALIGN = 256
ATT_BLK = 256
VMEM_LIMIT = 48 * 1024 * 1024
LANE = 128

ADAM_LR, ADAM_B1, ADAM_B2, ADAM_EPS, ADAM_WD, ADAM_STEP = 0.001, 0.9, 0.999, 1e-08, 0.01, 10


def _call(body, **kw):
    return pl.pallas_call(body, **kw)


def _tile(n, target, mult):
    best, t = None, mult
    while t <= min(n, target):
        if n % t == 0:
            best = t
        t += mult
    assert best is not None, (n, target, mult)
    return best


def _cparams(sem):
    return pltpu.CompilerParams(dimension_semantics=sem, vmem_limit_bytes=VMEM_LIMIT)


def _raw_dot(a, b, ca, cb, precise):
    dims = (((ca,), (cb,)), ((), ()))
    a_hi, b_hi = a.astype(BF16), b.astype(BF16)
    out = lax.dot_general(a_hi, b_hi, dims, preferred_element_type=F32)
    if precise:
        a_lo = (a - a_hi.astype(F32)).astype(BF16)
        b_lo = (b - b_hi.astype(F32)).astype(BF16)
        out = out + (lax.dot_general(a_hi, b_lo, dims, preferred_element_type=F32)
                     + lax.dot_general(a_lo, b_hi, dims, preferred_element_type=F32))
    return out


def _make_dot(ca, cb, precise):
    @jax.custom_vjp
    def f(a, b):
        return _raw_dot(a, b, ca, cb, precise)

    def fwd(a, b):
        return f(a, b), (a, b)

    def bwd(res, ct):
        a, b = res
        if ca == 1:
            da = _raw_dot(ct, b, 1, 1 if cb == 0 else 0, precise)
        else:
            da = _raw_dot(b, ct, 1 if cb == 0 else 0, 1, precise)
        if cb == 0:
            db = _raw_dot(a, ct, 0 if ca == 1 else 1, 0, precise)
        else:
            db = _raw_dot(ct, a, 0, 0 if ca == 1 else 1, precise)
        return da, db

    f.defvjp(fwd, bwd)
    return f


_DOTS = {(ca, cb, p): _make_dot(ca, cb, p) for ca in (0, 1) for cb in (0, 1) for p in (False, True)}


def _dot(a, b, ca=1, cb=0, precise=False):
    return _DOTS[(ca, cb, precise)](a, b)


def _mm_call(a, b, name, ta=False, tb=False, out_dtype=F32):
    k, m = a.shape if ta else a.shape[::-1]
    n, kb = b.shape if tb else b.shape[::-1]
    assert k == kb, (a.shape, b.shape)
    tm = _tile(m, 1408, LANE if ta else 16)
    tn = _tile(n, 1408, LANE)
    tk = _tile(k, 1408, LANE)
    nk = k // tk
    dims = (((0 if ta else 1,), (1 if tb else 0,)), ((), ()))

    def body(a_ref, b_ref, o_ref, *scratch):
        part = lax.dot_general(a_ref[...], b_ref[...], dims, preferred_element_type=F32)
        if nk == 1:
            o_ref[...] = part.astype(o_ref.dtype)
            return
        acc_ref = scratch[0]
        kk = pl.program_id(2)

        @pl.when(kk == 0)
        def _():
            acc_ref[...] = part

        @pl.when(kk > 0)
        def _():
            acc_ref[...] += part

        @pl.when(kk == nk - 1)
        def _():
            o_ref[...] = acc_ref[...].astype(o_ref.dtype)

    return _call(
        body, name=name, grid=(m // tm, n // tn, nk),
        in_specs=[pl.BlockSpec((tk, tm), lambda i, j, kk: (kk, i)) if ta else
                  pl.BlockSpec((tm, tk), lambda i, j, kk: (i, kk)),
                  pl.BlockSpec((tn, tk), lambda i, j, kk: (j, kk)) if tb else
                  pl.BlockSpec((tk, tn), lambda i, j, kk: (kk, j))],
        out_specs=pl.BlockSpec((tm, tn), lambda i, j, kk: (i, j)),
        out_shape=jax.ShapeDtypeStruct((m, n), out_dtype),
        scratch_shapes=[pltpu.VMEM((tm, tn), F32)] if nk > 1 else [],
        compiler_params=_cparams(("parallel", "parallel", "arbitrary")),
    )(a, b)


def _make_mm(out_dtype):
    @jax.custom_vjp
    def op(a, w):
        return _mm_call(a.astype(BF16), w.astype(BF16), "mm_fwd", out_dtype=out_dtype)

    def fwd(a, w):
        a_b, w_b = a.astype(BF16), w.astype(BF16)
        return _mm_call(a_b, w_b, "mm_fwd", out_dtype=out_dtype), (a_b, w_b, jnp.zeros((), a.dtype))

    def bwd(res, ct):
        a_b, w_b, like_a = res
        ct_b = ct.astype(BF16)
        return (_mm_call(ct_b, w_b, "mm_dx", tb=True, out_dtype=like_a.dtype),
                _mm_call(a_b, ct_b, "mm_dw", ta=True))

    op.defvjp(fwd, bwd)
    return op


mm = _make_mm(F32)
mm_bf16 = _make_mm(BF16)


def _rows_specs(rows, tm, ncb, bc):
    specs = []
    for idx, r in enumerate(rows):
        if idx in bc:
            specs.append(pl.BlockSpec((tm, r.shape[1]), lambda i, j: (i, 0)))
        else:
            specs.append(pl.BlockSpec((tm, r.shape[1] // ncb), lambda i, j: (i, j)))
    return specs


def _param_specs(params):
    return [pl.BlockSpec(p.shape, lambda i, j: (0, 0)) for p in params]


def _group_slices(refs, groups, g, whole):
    out = []
    for idx, r in enumerate(refs):
        w = r.shape[1] // groups
        out.append((r[...] if idx in whole else r[:, g * w:(g + 1) * w]).astype(F32))
    return out


def _rows_fwd_call(fns, rows, params, outs, tm, ncb, bc, name, out_dtypes=None):
    r_total = rows[0].shape[0]
    nr, groups = len(rows), len(fns)
    out_dtypes = out_dtypes or [F32] * len(outs)

    def body(*refs):
        pvals = [r[...] for r in refs[nr:nr + len(params)]]
        for g, fn in enumerate(fns):
            res = fn(*_group_slices(refs[:nr], groups, g, bc), *pvals)
            for o_ref, val in zip(refs[nr + len(params):], res):
                w = o_ref.shape[1] // groups
                o_ref[:, g * w:(g + 1) * w] = val.astype(o_ref.dtype)

    return _call(
        body, name=name, grid=(r_total // tm, ncb),
        in_specs=_rows_specs(rows, tm, ncb, bc) + _param_specs(params),
        out_specs=[pl.BlockSpec((tm, w * groups), lambda i, j: (i, j)) for w in outs],
        out_shape=[jax.ShapeDtypeStruct((r_total, w * groups * ncb), dt) for w, dt in zip(outs, out_dtypes)],
        compiler_params=_cparams(("parallel", "parallel")),
    )(*rows, *params)


def _rows_bwd_call(fns, rows, params, cts, tm, ncb, bc, name):
    r_total = rows[0].shape[0]
    nr, npar, nct, groups = len(rows), len(params), len(cts), len(fns)

    def body(*refs):
        i, j = pl.program_id(0), pl.program_id(1)
        pvals = [r[...] for r in refs[nr:nr + npar]]
        ct_refs = refs[nr + npar:nr + npar + nct]
        d_refs = refs[nr + npar + nct:]
        shared = {idx: None for idx in list(bc) + list(range(nr, nr + npar))}
        for g, fn in enumerate(fns):
            _, vjp = jax.vjp(lambda *a, fn=fn: tuple(fn(*a)), *_group_slices(refs[:nr], groups, g, bc), *pvals)
            grads = vjp(tuple(_group_slices(ct_refs, groups, g, ())))
            for idx in range(nr + npar):
                if idx in shared:
                    shared[idx] = grads[idx] if shared[idx] is None else shared[idx] + grads[idx]
                else:
                    w = d_refs[idx].shape[1] // groups
                    d_refs[idx][:, g * w:(g + 1) * w] = grads[idx].astype(d_refs[idx].dtype)
        for idx, total in shared.items():
            first = (j == 0) if idx < nr else ((i == 0) & (j == 0))

            @pl.when(first)
            def _(idx=idx):
                d_refs[idx][...] = jnp.zeros_like(d_refs[idx])
            d_refs[idx][...] += total

    ct_specs = [pl.BlockSpec((tm, c.shape[1] // ncb), lambda i, j: (i, j)) for c in cts]
    return _call(
        body, name=name + "_bwd", grid=(r_total // tm, ncb),
        in_specs=_rows_specs(rows, tm, ncb, bc) + _param_specs(params) + ct_specs,
        out_specs=_rows_specs(rows, tm, ncb, bc) + _param_specs(params),
        out_shape=[jax.ShapeDtypeStruct(a.shape, a.dtype) for a in list(rows) + list(params)],
        compiler_params=_cparams(("arbitrary", "arbitrary")),
    )(*rows, *params, *cts)


def rowop(fn, name, outs, tm, ncb=1, bc=(), out_dtypes=None):
    fns = list(fn) if isinstance(fn, (list, tuple)) else [fn]

    @jax.custom_vjp
    def op(rows, params):
        return tuple(_rows_fwd_call(fns, rows, params, outs, tm, ncb, bc, name, out_dtypes))

    def fwd(rows, params):
        return op(rows, params), (rows, params)

    def bwd(res, cts):
        rows, params = res
        d = _rows_bwd_call(fns, rows, params, cts, tm, ncb, bc, name)
        return tuple(d[:len(rows)]), tuple(d[len(rows):])

    op.defvjp(fwd, bwd)
    return op


def _sigmoid(x):
    return 1.0 / (1.0 + jnp.exp(-x))


def _silu(x):
    return x * _sigmoid(x)


def _softplus(x):
    return jnp.maximum(x, 0.0) + jnp.log(1.0 + jnp.exp(-jnp.abs(x)))


def _f_rmsnorm(x, g):
    return (x * lax.rsqrt(jnp.mean(x * x, axis=-1, keepdims=True) + EPS) * g,)


def _f_qnorm(x, g):
    return (x * lax.rsqrt(jnp.mean(x * x, axis=-1, keepdims=True) + EPS) * (g * (FOX_HD ** -0.5)),)


def _f_logsig(x, b):
    return (-_softplus(-(x + b)),)


def _f_gdn_q(x):
    y = _silu(x)
    return (y * lax.rsqrt(jnp.sum(y * y, axis=-1, keepdims=True) + EPS) * (GDN_HD ** -0.5),)


def _f_gdn_k(x):
    y = _silu(x)
    return (y * lax.rsqrt(jnp.sum(y * y, axis=-1, keepdims=True) + EPS),)


def _f_gdn_v(x):
    return (_silu(x),)


def _f_gdn_gates(bl, al, a_log, dt_bias):
    return _sigmoid(bl), -jnp.exp(a_log) * _softplus(al + dt_bias)


def _f_gdn_out(o, z, g):
    return (o * lax.rsqrt(jnp.mean(o * o, axis=-1, keepdims=True) + EPS) * g * _silu(z),)


def _f_merge(g0, g1, ya, yb):
    return (_sigmoid(g0) * ya + _sigmoid(g1) * yb,)


def _f_residual(a, b, keep):
    return ((a + b) * keep,)


def _f_residual_norm(a, b, keep, g):
    r = (a + b) * keep
    return r, _f_rmsnorm(r, g)[0]


def _f_glu(a, b):
    return (_silu(a) * b,)


def _glu_call(up, ct):
    t_total, two_f = up.shape
    f = two_f // 2
    tm = _tile(t_total, 128, 16)
    wc = _tile(f, 1408, LANE)

    def body(*refs):
        up_ref, out_ref = refs[0], refs[-1]
        for c0 in range(0, f, wc):
            a, b = up_ref[:, c0:c0 + wc].astype(F32), up_ref[:, f + c0:f + c0 + wc].astype(F32)
            if ct is None:
                out_ref[:, c0:c0 + wc] = _f_glu(a, b)[0].astype(out_ref.dtype)
            else:
                _, vjp = jax.vjp(_f_glu, a, b)
                da, db = vjp((refs[1][:, c0:c0 + wc].astype(F32),))
                out_ref[:, c0:c0 + wc] = da.astype(out_ref.dtype)
                out_ref[:, f + c0:f + c0 + wc] = db.astype(out_ref.dtype)

    wide = pl.BlockSpec((tm, two_f), lambda i: (i, 0))
    narrow = pl.BlockSpec((tm, f), lambda i: (i, 0))
    return _call(
        body, name="ffn_glu" if ct is None else "ffn_glu_bwd", grid=(t_total // tm,),
        in_specs=[wide] if ct is None else [wide, narrow], out_specs=narrow if ct is None else wide,
        out_shape=jax.ShapeDtypeStruct((t_total, f), BF16) if ct is None else jax.ShapeDtypeStruct(up.shape, up.dtype),
        compiler_params=_cparams(("parallel",)),
    )(*((up,) if ct is None else (up, ct)))


@jax.custom_vjp
def glu(up):
    return _glu_call(up, None)


glu.defvjp(lambda up: (glu(up), up), lambda up, ct: (_glu_call(up, ct),))


def _shift_down(x, halo, s, row8):
    rx = pltpu.roll(x, s, 0)
    top = jnp.where(row8 < s, pltpu.roll(halo, s, 0), rx[:8])
    return jnp.concatenate([top, rx[8:]], axis=0)


def _shift_up(x, nxt, s, row8):
    tm = x.shape[0]
    rx = pltpu.roll(x, tm - s, 0)
    bot = jnp.where(row8 >= 8 - s, pltpu.roll(nxt, 8 - s, 0), rx[tm - 8:])
    return jnp.concatenate([rx[:tm - 8], bot], axis=0)


def _conv_tiles(r_total, c_total):
    return _tile(r_total, 768, 8), _tile(c_total, 1408, LANE)


def _halo_rows(dtype):
    return 16 if dtype == BF16 else 8


def _conv_fwd_call(x, w8, k_taps, out_dtype):
    r_total, c_total = x.shape
    tm, tc = _conv_tiles(r_total, c_total)
    hr = _halo_rows(x.dtype)
    hb = tm // hr

    def body(x_ref, halo_ref, w_ref, y_ref):
        i = pl.program_id(1)
        xt = x_ref[...].astype(F32)
        halo = jnp.where(i > 0, halo_ref[...].astype(F32)[hr - 8:hr], 0.0)
        row8 = lax.broadcasted_iota(jnp.int32, (8, tc), 0)
        acc = w_ref[k_taps - 1:k_taps, :] * xt
        for k in range(k_taps - 1):
            acc += w_ref[k:k + 1, :] * _shift_down(xt, halo, k_taps - 1 - k, row8)
        y_ref[...] = acc.astype(y_ref.dtype)

    return _call(
        body, name="dwconv_fwd", grid=(c_total // tc, r_total // tm),
        in_specs=[pl.BlockSpec((tm, tc), lambda c, i: (i, c)),
                  pl.BlockSpec((hr, tc), lambda c, i: (jnp.maximum(i * hb - 1, 0), c)),
                  pl.BlockSpec((8, tc), lambda c, i: (0, c))],
        out_specs=pl.BlockSpec((tm, tc), lambda c, i: (i, c)),
        out_shape=jax.ShapeDtypeStruct(x.shape, out_dtype),
        compiler_params=_cparams(("parallel", "parallel")),
    )(x, x, w8)


def _conv_bwd_call(x, w8, dy, k_taps):
    r_total, c_total = x.shape
    tm, tc = _conv_tiles(r_total, c_total)
    hr = _halo_rows(dy.dtype)
    hb = tm // hr
    n_i = r_total // tm

    def body(x_ref, w_ref, dy_ref, nxt_ref, dx_ref, dw_ref):
        i = pl.program_id(1)
        xt, dyt = x_ref[...].astype(F32), dy_ref[...].astype(F32)
        nxt = jnp.where(i < n_i - 1, nxt_ref[...].astype(F32)[0:8], 0.0)
        row8 = lax.broadcasted_iota(jnp.int32, (8, tc), 0)
        dx = w_ref[k_taps - 1:k_taps, :] * dyt
        upd = jnp.where(row8 == k_taps - 1, jnp.sum(dyt * xt, axis=0, keepdims=True), 0.0)
        for k in range(k_taps - 1):
            dy_ahead = _shift_up(dyt, nxt, k_taps - 1 - k, row8)
            dx += w_ref[k:k + 1, :] * dy_ahead
            upd = jnp.where(row8 == k, jnp.sum(dy_ahead * xt, axis=0, keepdims=True), upd)
        dx_ref[...] = dx.astype(dx_ref.dtype)

        @pl.when(i == 0)
        def _():
            dw_ref[...] = jnp.zeros_like(dw_ref)

        dw_ref[...] += upd

    return _call(
        body, name="dwconv_bwd", grid=(c_total // tc, n_i),
        in_specs=[pl.BlockSpec((tm, tc), lambda c, i: (i, c)),
                  pl.BlockSpec((8, tc), lambda c, i: (0, c)),
                  pl.BlockSpec((tm, tc), lambda c, i: (i, c)),
                  pl.BlockSpec((hr, tc), lambda c, i: (jnp.minimum((i + 1) * hb, r_total // hr - 1), c))],
        out_specs=[pl.BlockSpec((tm, tc), lambda c, i: (i, c)), pl.BlockSpec((8, tc), lambda c, i: (0, c))],
        out_shape=[jax.ShapeDtypeStruct(x.shape, x.dtype), jax.ShapeDtypeStruct(w8.shape, F32)],
        compiler_params=_cparams(("parallel", "arbitrary")),
    )(x, w8, dy, dy)


def make_dwconv(k_taps, out_dtype=None):
    @jax.custom_vjp
    def op(x, w8):
        return _conv_fwd_call(x, w8, k_taps, out_dtype or x.dtype)

    def fwd(x, w8):
        return op(x, w8), (x, w8)

    def bwd(res, dy):
        x, w8 = res
        dx, dw = _conv_bwd_call(x, w8, dy, k_taps)
        return dx, dw

    op.defvjp(fwd, bwd)
    return op


def _cumsum_call(x, reverse):
    h, t_total = x.shape
    tb = _tile(t_total, 256, LANE)
    nb = t_total // tb

    def body(x_ref, o_ref, carry_ref):
        i = pl.program_id(0)

        @pl.when(i == 0)
        def _():
            carry_ref[...] = jnp.zeros_like(carry_ref)

        r = lax.broadcasted_iota(jnp.int32, (tb, tb), 0)
        c = lax.broadcasted_iota(jnp.int32, (tb, tb), 1)
        tri = jnp.where((r >= c) if reverse else (r <= c), 1.0, 0.0).astype(F32)
        xv = x_ref[...]
        carry = jnp.max(carry_ref[...], axis=1, keepdims=True)
        o_ref[...] = _raw_dot(xv, tri, 1, 0, True) + carry
        carry_ref[...] = jnp.broadcast_to(carry + jnp.sum(xv, axis=1, keepdims=True), carry_ref.shape)

    imap = (lambda i: (0, nb - 1 - i)) if reverse else (lambda i: (0, i))
    return _call(
        body, name="cumsum_rev" if reverse else "cumsum", grid=(nb,),
        in_specs=[pl.BlockSpec((h, tb), imap)], out_specs=pl.BlockSpec((h, tb), imap),
        out_shape=jax.ShapeDtypeStruct(x.shape, F32), scratch_shapes=[pltpu.VMEM((h, LANE), F32)],
        compiler_params=_cparams(("arbitrary",)),
    )(x)


@jax.custom_vjp
def cumsum_lanes(x):
    return _cumsum_call(x, False)


cumsum_lanes.defvjp(lambda x: (cumsum_lanes(x), None), lambda _, ct: (_cumsum_call(ct, True),))


NEG_BIG = -1e30


def _attn_sub_tiles(nb):
    return max(s for s in (3, 2, 1) if nb % s == 0)


EXP_ZERO = -92.0
SMEM = pl.BlockSpec(memory_space=pltpu.SMEM)


def _max_row_norm_sq(x):
    h_total, t_total, hd = x.shape
    tb = _tile(t_total, 2816, 8)

    def body(x_ref, o_ref):
        @pl.when(pl.program_id(1) == 0)
        def _():
            o_ref[...] = jnp.zeros_like(o_ref)

        xv = x_ref[0]
        top = jnp.max(jnp.sum(xv * xv, axis=1, keepdims=True), axis=0, keepdims=True)
        o_ref[0] = jnp.maximum(o_ref[0], top)

    return _call(
        body, name="max_row_norm", grid=(h_total, t_total // tb),
        in_specs=[pl.BlockSpec((1, tb, hd), lambda h, i: (h, i, 0))],
        out_specs=pl.BlockSpec((1, 8, LANE), lambda h, i: (h, 0, 0)),
        out_shape=jax.ShapeDtypeStruct((h_total, 8, LANE), F32),
        compiler_params=_cparams(("parallel", "arbitrary")),
    )(x)


def _attn_skip_tables(q, k, f_row):
    bound = 2.0 * jnp.sqrt(_max_row_norm_sq(q)[:, 0, :1] * _max_row_norm_sq(k)[:, 0, :1])
    return EXP_ZERO - bound, f_row[:, :, 0, 0], f_row[:, :, 0, -1]


def _attn_fwd_call(q, k, v, f_col, f_row, tables):
    h_total, t_total, hd = q.shape
    blk = f_row.shape[-1]
    nb = t_total // blk
    nsub = _attn_sub_tiles(nb)
    tq = nsub * blk

    def body(thr_ref, first_ref, last_ref, q_ref, k_ref, vt_ref, fc_ref, fr_ref, o_ref, lse_ref):
        h = pl.program_id(0)
        i = pl.program_id(1)
        gap_needed = thr_ref[h, 0]
        f_tile = first_ref[h, i * nsub]
        j_start = lax.while_loop(lambda j: (j < i * nsub) & (f_tile - last_ref[h, j] < gap_needed),
                                 lambda j: j + 1, 0)
        r = lax.broadcasted_iota(jnp.int32, (blk, blk), 0)
        c = lax.broadcasted_iota(jnp.int32, (blk, blk), 1)
        qs = [q_ref[0, s * blk:(s + 1) * blk, :].astype(BF16) for s in range(nsub)]
        fqs = [fr_ref[0, i * nsub + s] for s in range(nsub)]

        def load_kv(j):
            off = pl.multiple_of(j * blk, blk)
            return k_ref[0, pl.ds(off, blk), :], vt_ref[0, j], fc_ref[0, pl.ds(off, blk), :]

        def tile(kv, s, carry, diagonal):
            kj, vtj, fk = kv
            m, l, acc = carry
            st = _raw_dot(kj, qs[s], 1, 1, False) + fqs[s] - fk
            if diagonal:
                st = jnp.where(r <= c, st, NEG_BIG)
            m_new = jnp.maximum(m, jnp.max(st, axis=0, keepdims=True))
            p = jnp.exp(st - m_new)
            alpha = jnp.exp(m - m_new)
            l = alpha * l + jnp.sum(p, axis=0, keepdims=True)
            acc = alpha * acc + _raw_dot(vtj, p, 1, 0, False)
            return m_new, l, acc

        starts = [j_start]
        for s in range(1, nsub):
            f_sub = first_ref[h, i * nsub + s]
            starts.append(lax.while_loop(lambda j, f_sub=f_sub: (j < i * nsub) & (f_sub - last_ref[h, j] < gap_needed),
                                         lambda j: j + 1, starts[-1]))
        starts.append(i * nsub)

        carry = tuple((jnp.full((1, blk), NEG_BIG, F32), jnp.zeros((1, blk), F32), jnp.zeros((hd, blk), F32))
                      for _ in range(nsub))
        for phase in range(nsub):
            def below_diagonal(j, carry, phase=phase):
                kv = load_kv(j)
                return tuple(tile(kv, s, carry[s], False) if s <= phase else carry[s] for s in range(nsub))

            carry = lax.fori_loop(starts[phase], starts[phase + 1], below_diagonal, carry)
        carry = list(carry)
        for d in range(nsub):
            kv = load_kv(i * nsub + d)
            for s in range(d, nsub):
                carry[s] = tile(kv, s, carry[s], s == d)
        for s, (m, l, acc) in enumerate(carry):
            o_ref[0, :, s * blk:(s + 1) * blk] = acc / l
            lse_ref[0, s] = m + jnp.log(l)

    vt = v.reshape(h_total, nb, blk, hd).transpose(0, 1, 3, 2).astype(BF16)
    return _call(
        body, name="fox_fwd", grid=(h_total, nb // nsub),
        in_specs=[SMEM, SMEM, SMEM,
                  pl.BlockSpec((1, tq, hd), lambda h, i: (h, i, 0)),
                  pl.BlockSpec((1, t_total, hd), lambda h, i: (h, 0, 0)),
                  pl.BlockSpec((1, nb, hd, blk), lambda h, i: (h, 0, 0, 0)),
                  pl.BlockSpec((1, t_total, 1), lambda h, i: (h, 0, 0)),
                  pl.BlockSpec((1, nb, 1, blk), lambda h, i: (h, 0, 0, 0))],
        out_specs=[pl.BlockSpec((1, hd, tq), lambda h, i: (h, 0, i)),
                   pl.BlockSpec((1, nsub, 1, blk), lambda h, i: (h, i, 0, 0))],
        out_shape=[jax.ShapeDtypeStruct((h_total, hd, t_total), F32), jax.ShapeDtypeStruct(f_row.shape, F32)],
        compiler_params=_cparams(("parallel", "parallel")),
    )(*tables, q, k.astype(BF16), vt, f_col, f_row)


def _attn_bwd_call(q, k, v, f_col, f_row, tables, lse_row, delta_row, do_blk):
    h_total, t_total, hd = q.shape
    blk = f_row.shape[-1]
    nb = t_total // blk
    nsub = _attn_sub_tiles(nb)
    tkv = nsub * blk

    def body(thr_ref, first_ref, last_ref, q_ref, do_ref, k_ref, v_ref, fc_ref, fr_ref, lse_ref, dl_ref,
             dq_ref, dk_ref, dv_ref, dfk_ref, dfq_ref):
        h = pl.program_id(0)
        j = pl.program_id(1)
        gap_needed = thr_ref[h, 0]
        stops = [(j + 1) * nsub]
        for s in range(nsub):
            f_sub = last_ref[h, j * nsub + s]
            stops.append(lax.while_loop(
                lambda i, f_sub=f_sub: (i < nb) & (first_ref[h, jnp.minimum(i, nb - 1)] - f_sub >= gap_needed),
                lambda i: i + 1, stops[-1]))

        @pl.when(j == 0)
        def _():
            dq_ref[...] = jnp.zeros_like(dq_ref)
            dfq_ref[...] = jnp.zeros_like(dfq_ref)

        ks = [k_ref[0, s * blk:(s + 1) * blk, :].astype(BF16) for s in range(nsub)]
        vs = [v_ref[0, s * blk:(s + 1) * blk, :].astype(BF16) for s in range(nsub)]
        fks = [fc_ref[0, s * blk:(s + 1) * blk, :] for s in range(nsub)]
        r = lax.broadcasted_iota(jnp.int32, (blk, blk), 0)
        c = lax.broadcasted_iota(jnp.int32, (blk, blk), 1)

        def q_step(i, accs, subs):
            off = pl.multiple_of(i * blk, blk)
            qi = q_ref[0, pl.ds(off, blk), :]
            doi = do_ref[0, i]
            fq, lse, dl = fr_ref[0, i], lse_ref[0, i], dl_ref[0, i]
            accs = list(accs)
            dq_i, dfq_i = None, None
            for s, diagonal in subs:
                dk, dv, dfk = accs[s]
                st = _raw_dot(ks[s], qi, 1, 1, False) + fq - fks[s] - lse
                if diagonal:
                    st = jnp.where(r <= c, st, NEG_BIG)
                pt = jnp.exp(st)
                dv = dv + _raw_dot(pt, doi, 1, 1, False)
                dst = pt * (_raw_dot(vs[s], doi, 1, 0, False) - dl)
                dk = dk + _raw_dot(dst, qi, 1, 0, False)
                dfk = dfk - jnp.sum(dst, axis=1, keepdims=True)
                accs[s] = (dk, dv, dfk)
                dq_s = _raw_dot(dst, ks[s], 0, 0, False)
                dfq_s = jnp.sum(dst, axis=0, keepdims=True)
                dq_i = dq_s if dq_i is None else dq_i + dq_s
                dfq_i = dfq_s if dfq_i is None else dfq_i + dfq_s
            dfq_ref[0, i] += dfq_i
            dq_ref[0, pl.ds(off, blk), :] += dq_i
            return tuple(accs)

        accs = tuple((jnp.zeros((blk, hd), F32), jnp.zeros((blk, hd), F32), jnp.zeros((blk, 1), F32))
                     for _ in range(nsub))
        for d in range(nsub):
            accs = q_step(j * nsub + d, accs, [(s, s == d) for s in range(d + 1)])
        for phase in range(nsub):
            accs = lax.fori_loop(stops[phase], stops[phase + 1],
                                 lambda i, a, phase=phase: q_step(i, a, [(s, False) for s in range(phase, nsub)]), accs)
        for s, (dk, dv, dfk) in enumerate(accs):
            dk_ref[0, s * blk:(s + 1) * blk, :] = dk
            dv_ref[0, s * blk:(s + 1) * blk, :] = dv
            dfk_ref[0, s * blk:(s + 1) * blk, :] = dfk

    full = pl.BlockSpec((1, t_total, hd), lambda h, j: (h, 0, 0))
    tile = pl.BlockSpec((1, tkv, hd), lambda h, j: (h, j, 0))
    col = pl.BlockSpec((1, tkv, 1), lambda h, j: (h, j, 0))
    rows = pl.BlockSpec((1, nb, 1, blk), lambda h, j: (h, 0, 0, 0))
    do_blocks = pl.BlockSpec((1, nb, hd, blk), lambda h, j: (h, 0, 0, 0))
    return _call(
        body, name="fox_bwd", grid=(h_total, nb // nsub),
        in_specs=[SMEM, SMEM, SMEM, full, do_blocks, tile, tile, col, rows, rows, rows],
        out_specs=[full, tile, tile, col, rows],
        out_shape=[jax.ShapeDtypeStruct(q.shape, F32), jax.ShapeDtypeStruct(q.shape, F32),
                   jax.ShapeDtypeStruct(q.shape, F32), jax.ShapeDtypeStruct(f_col.shape, F32),
                   jax.ShapeDtypeStruct(f_row.shape, F32)],
        compiler_params=_cparams(("parallel", "arbitrary")),
    )(*tables, q.astype(BF16), do_blk, k, v, f_col, f_row, lse_row, delta_row)


def _attn_delta_call(do_t, o_t):
    h_total, hd, t_total = o_t.shape
    tb = _tile(t_total, 2816, LANE)

    def body(do_ref, o_ref, d_ref):
        d_ref[0] = jnp.sum(do_ref[0] * o_ref[0], axis=0, keepdims=True)

    spec = pl.BlockSpec((1, hd, tb), lambda h, i: (h, 0, i))
    return _call(
        body, name="fox_delta", grid=(h_total, t_total // tb), in_specs=[spec, spec],
        out_specs=pl.BlockSpec((1, 1, tb), lambda h, i: (h, 0, i)),
        out_shape=jax.ShapeDtypeStruct((h_total, 1, t_total), F32),
        compiler_params=_cparams(("parallel", "parallel")),
    )(do_t, o_t)


@jax.custom_vjp
def fox_attention(q, k, v, f_col, f_row):
    return _attn_fwd_call(q, k, v, f_col, f_row, _attn_skip_tables(q, k, f_row))[0]


def _fox_fwd(q, k, v, f_col, f_row):
    tables = _attn_skip_tables(q, k, f_row)
    o_t, lse_row = _attn_fwd_call(q, k, v, f_col, f_row, tables)
    return o_t, (q, k, v, f_col, f_row, tables, o_t, lse_row)


def _fox_bwd(res, do_t):
    q, k, v, f_col, f_row, tables, o_t, lse_row = res
    h_total, t_total, hd = q.shape
    nb, blk = f_row.shape[1], f_row.shape[3]
    delta = _attn_delta_call(do_t, o_t).reshape(f_row.shape)
    do_blk = do_t.reshape(h_total, hd, nb, blk).transpose(0, 2, 1, 3).astype(BF16)
    grads = _attn_bwd_call(q, k, v, f_col, f_row, tables, lse_row, delta, do_blk)
    return tuple(g.astype(p.dtype) for g, p in zip(grads, (q, k, v, f_col, f_row)))


fox_attention.defvjp(_fox_fwd, _fox_bwd)


def _head_col(blk, h):
    lane = lax.broadcasted_iota(jnp.int32, blk.shape, 1)
    return jnp.sum(jnp.where(lane == h, blk, 0.0), axis=1, keepdims=True)


@jax.custom_vjp
def _cat2(a, b):
    return jnp.concatenate([a, b], axis=1)


_cat2.defvjp(lambda a, b: (_cat2(a, b), a.shape[1]), lambda na, ct: (ct[:, :na], ct[:, na:]))


@jax.custom_vjp
def _split2(x):
    half = x.shape[1] // 2
    return x[:, :half], x[:, half:]


_split2.defvjp(lambda x: (_split2(x), None), lambda _, cts: (jnp.concatenate(cts, axis=1),))


def _neumann_solve(m, b):
    x = b - _raw_dot(m, b, 1, 0, False)
    powers, steps = [m], 1
    while 2 * steps < GDN_CHUNK:
        powers.append(_raw_dot(powers[-1], powers[-1], 1, 0, False))
        x = x + _raw_dot(powers[-1], x, 1, 0, False)
        steps *= 2
    return x, powers


@jax.custom_vjp
def _unit_lower_solve(m, b):
    return _neumann_solve(m, b)[0]


def _unit_lower_solve_fwd(m, b):
    x, powers = _neumann_solve(m, b)
    return x, (powers, x)


def _unit_lower_solve_bwd(res, dx):
    powers, x = res
    db = dx - _raw_dot(powers[0], dx, 0, 0, False)
    for p in powers[1:]:
        db = db + _raw_dot(p, db, 0, 0, False)
    return -_raw_dot(db, x, 1, 1, False), db


_unit_lower_solve.defvjp(_unit_lower_solve_fwd, _unit_lower_solve_bwd)


@jax.custom_vjp
def _unit_lower_solve_known(m, b, x):
    return x


def _unit_lower_solve_known_bwd(res, dx):
    m, x = res
    powers, steps = [m], 1
    while 2 * steps < GDN_CHUNK:
        powers.append(_raw_dot(powers[-1], powers[-1], 1, 0, False))
        steps *= 2
    dm, db = _unit_lower_solve_bwd((powers, x), dx)
    return dm, db, jnp.zeros_like(x)


_unit_lower_solve_known.defvjp(lambda m, b, x: (x, (m, x)), _unit_lower_solve_known_bwd)


def _gdn_intra(h, q, k, v, b_blk, g_blk, uw_known=None):
    n = q.shape[0]
    b, g = _head_col(b_blk, h), _head_col(g_blk, h)
    r = lax.broadcasted_iota(jnp.int32, (n, n), 0)
    c = lax.broadcasted_iota(jnp.int32, (n, n), 1)
    same = (r // GDN_CHUNK) == (c // GDN_CHUNK)
    incl = same & (r >= c)
    g_row = jnp.sum(jnp.where(r == c, g, 0.0), axis=0, keepdims=True)
    big_g = jnp.sum(jnp.where(incl, g_row, 0.0), axis=1, keepdims=True)
    big_g_row = jnp.sum(jnp.where(same & (r <= c), g, 0.0), axis=0, keepdims=True)
    g_tot = jnp.sum(jnp.where(same, g_row, 0.0), axis=1, keepdims=True)
    dec = jnp.where(incl, jnp.exp(jnp.where(incl, big_g - big_g_row, 0.0)), 0.0)
    dec_strict = jnp.where(r > c, dec, 0.0)
    e_g = jnp.exp(big_g)
    kb = k * b
    m = _dot(kb, k, 1, 1) * dec_strict
    rs = lax.broadcasted_iota(jnp.int32, (n, GDN_CHUNK), 0)
    cs = lax.broadcasted_iota(jnp.int32, (n, GDN_CHUNK), 1)
    fold = jnp.where(rs % GDN_CHUNK == cs, 1.0, 0.0).astype(F32)
    aqk = _dot(_dot(q, k, 1, 1) * dec, fold, 1, 0, True)
    rhs = _cat2(v * b, kb * e_g)
    u, w = _split2(_unit_lower_solve(m, rhs) if uw_known is None else
                   _unit_lower_solve_known(m, rhs, jnp.concatenate(uw_known, axis=1)))
    lane = lax.broadcasted_iota(jnp.int32, b_blk.shape, 1)
    return u, w, q * e_g, k * jnp.exp(g_tot - big_g), aqk, jnp.where(lane == h, g_tot, 0.0)


def _gdn_rec(h, s, u, w, qg, kd, aqk, gl_blk):
    g_last = jnp.max(_head_col(gl_blk, h), axis=0, keepdims=True)
    big_u = u - _dot(w, s)
    o = _dot(qg, s) + _dot(aqk, big_u)
    s_next = s * jnp.exp(g_last) + _dot(kd, big_u, 0, 0)
    return o, s_next


GDN_TOK_BLK = 256


def _gdn_layout(t_total, rev):
    tb = _tile(t_total, GDN_TOK_BLK, GDN_CHUNK)
    cb, nblk = tb // GDN_CHUNK, t_total // tb
    pos = (lambda i: nblk - 1 - i) if rev else (lambda i: i)
    specs = dict(
        tok=pl.BlockSpec((tb, GDN_W), lambda i: (pos(i), 0)),
        q=pl.BlockSpec((tb, GDN_W), lambda i: (pos(i), 0)),
        k=pl.BlockSpec((tb, GDN_W), lambda i: (pos(i), 1)),
        v=pl.BlockSpec((tb, GDN_W), lambda i: (pos(i), 2)),
        qkv=pl.BlockSpec((tb, 3 * GDN_W), lambda i: (pos(i), 0)),
        gate=pl.BlockSpec((tb, GDN_HEADS), lambda i: (pos(i), 0)),
        aqk=pl.BlockSpec((GDN_HEADS, tb, GDN_CHUNK), lambda i: (0, pos(i), 0)),
        state=pl.BlockSpec((GDN_HEADS, cb, GDN_HD, GDN_HD), lambda i: (0, pos(i), 0, 0)))
    return cb, nblk, specs


def _gdn_shapes(t_total):
    n_chunks = t_total // GDN_CHUNK
    return dict(tok=jax.ShapeDtypeStruct((t_total, GDN_W), BF16),
                out=jax.ShapeDtypeStruct((t_total, GDN_W), F32),
                gate=jax.ShapeDtypeStruct((t_total, GDN_HEADS), F32),
                aqk=jax.ShapeDtypeStruct((GDN_HEADS, t_total, GDN_CHUNK), F32),
                state=jax.ShapeDtypeStruct((GDN_HEADS, n_chunks, GDN_HD, GDN_HD), F32))


def _chunk_rows(ci):
    return pl.ds(pl.multiple_of(ci * GDN_CHUNK, GDN_CHUNK), GDN_CHUNK)


def _head_cols(h):
    return pl.ds(h * GDN_HD, GDN_HD)


def _gdn_intra_fwd_call(qkv, b, g):
    cb, nblk, sp = _gdn_layout(qkv.shape[0], False)
    sh = _gdn_shapes(qkv.shape[0])

    def body(q_ref, k_ref, v_ref, b_ref, g_ref, u_ref, w_ref, qg_ref, kd_ref, aqk_ref, gl_ref):
        b_blk, g_blk = b_ref[...], g_ref[...]
        gl = jnp.zeros(b_blk.shape, F32)
        for h in range(GDN_HEADS):
            cols = _head_cols(h)
            u, w, qg, kd, aqk, gl_h = _gdn_intra(h, q_ref[:, cols], k_ref[:, cols], v_ref[:, cols], b_blk, g_blk)
            u_ref[:, cols] = u.astype(u_ref.dtype)
            w_ref[:, cols] = w.astype(w_ref.dtype)
            qg_ref[:, cols] = qg.astype(qg_ref.dtype)
            kd_ref[:, cols] = kd.astype(kd_ref.dtype)
            aqk_ref[h] = aqk
            gl = gl + gl_h
        gl_ref[...] = gl

    return _call(
        body, name="gdn_intra_fwd", grid=(nblk,),
        in_specs=[sp["q"], sp["k"], sp["v"]] + [sp["gate"]] * 2,
        out_specs=[sp["tok"]] * 4 + [sp["aqk"], sp["gate"]],
        out_shape=[sh["tok"]] * 4 + [sh["aqk"], sh["gate"]],
        compiler_params=_cparams(("parallel",)),
    )(qkv, qkv, qkv, b, g)


def _gdn_intra_bwd_call(qkv, b, g, u, w, du, dw, dqg, dkd, daqk, dgl):
    cb, nblk, sp = _gdn_layout(qkv.shape[0], False)
    sh = _gdn_shapes(qkv.shape[0])

    def body(q_ref, k_ref, v_ref, b_ref, g_ref, u_ref, w_ref, du_ref, dw_ref, dqg_ref, dkd_ref, daqk_ref, dgl_ref,
             dqkv_ref, db_ref, dg_ref):
        b_blk, g_blk, dgl = b_ref[...], g_ref[...], dgl_ref[...]
        db = jnp.zeros(b_blk.shape, F32)
        dg = jnp.zeros(b_blk.shape, F32)
        for h in range(GDN_HEADS):
            cols = _head_cols(h)
            known = (u_ref[:, cols].astype(F32), w_ref[:, cols].astype(F32))
            _, vjp = jax.vjp(functools.partial(_gdn_intra, h, uw_known=known),
                             q_ref[:, cols], k_ref[:, cols], v_ref[:, cols], b_blk, g_blk)
            dq, dk, dv, db_h, dg_h = vjp((*[r[:, cols].astype(F32) for r in (du_ref, dw_ref, dqg_ref, dkd_ref)],
                                          daqk_ref[h], dgl))
            dqkv_ref[:, pl.ds(h * GDN_HD, GDN_HD)] = dq
            dqkv_ref[:, pl.ds(GDN_W + h * GDN_HD, GDN_HD)] = dk
            dqkv_ref[:, pl.ds(2 * GDN_W + h * GDN_HD, GDN_HD)] = dv
            db = db + db_h
            dg = dg + dg_h
        db_ref[...] = db
        dg_ref[...] = dg

    return _call(
        body, name="gdn_intra_bwd", grid=(nblk,),
        in_specs=[sp["q"], sp["k"], sp["v"]] + [sp["gate"]] * 2 + [sp["tok"]] * 6 + [sp["aqk"], sp["gate"]],
        out_specs=[sp["qkv"]] + [sp["gate"]] * 2,
        out_shape=[jax.ShapeDtypeStruct(qkv.shape, F32)] + [sh["gate"]] * 2,
        compiler_params=_cparams(("parallel",)),
    )(qkv, qkv, qkv, b, g, u, w, du, dw, dqg, dkd, daqk, dgl)


def _gdn_rec_fwd_call(u, w, qg, kd, aqk, gl):
    cb, nblk, sp = _gdn_layout(u.shape[0], False)
    sh = _gdn_shapes(u.shape[0])

    def body(u_ref, w_ref, qg_ref, kd_ref, aqk_ref, gl_ref, o_ref, s_all_ref, s_ref):
        @pl.when(pl.program_id(0) == 0)
        def _():
            s_ref[...] = jnp.zeros_like(s_ref)

        def chunk(ci, carry):
            rows = _chunk_rows(ci)
            gl_row = gl_ref[rows, :]
            states = [s_ref[h] for h in range(GDN_HEADS)]
            res = [_gdn_rec(h, states[h], *[r[rows, _head_cols(h)].astype(F32) for r in (u_ref, w_ref, qg_ref, kd_ref)],
                            aqk_ref[h, rows, :], gl_row)
                   for h in range(GDN_HEADS)]
            for h, (o, s_next) in enumerate(res):
                s_all_ref[h, ci] = states[h]
                o_ref[rows, _head_cols(h)] = o
                s_ref[h] = s_next
            return carry

        lax.fori_loop(0, cb, chunk, 0)

    return _call(
        body, name="gdn_rec_fwd", grid=(nblk,),
        in_specs=[sp["tok"]] * 4 + [sp["aqk"], sp["gate"]],
        out_specs=[sp["tok"], sp["state"]], out_shape=[sh["out"], sh["state"]],
        scratch_shapes=[pltpu.VMEM((GDN_HEADS, GDN_HD, GDN_HD), F32)],
        compiler_params=_cparams(("arbitrary",)),
    )(u, w, qg, kd, aqk, gl)


def _gdn_rec_bwd_call(u, w, qg, kd, aqk, gl, s_all, do):
    cb, nblk, sp = _gdn_layout(u.shape[0], True)
    sh = _gdn_shapes(u.shape[0])

    def body(u_ref, w_ref, qg_ref, kd_ref, aqk_ref, gl_ref, s_all_ref, do_ref,
             du_ref, dw_ref, dqg_ref, dkd_ref, daqk_ref, dgl_ref, ds_ref):
        @pl.when(pl.program_id(0) == 0)
        def _():
            ds_ref[...] = jnp.zeros_like(ds_ref)

        def chunk(step, carry):
            ci = cb - 1 - step
            rows = _chunk_rows(ci)
            gl_row = gl_ref[rows, :]
            res = []
            for h in range(GDN_HEADS):
                cols = _head_cols(h)
                _, vjp = jax.vjp(functools.partial(_gdn_rec, h), s_all_ref[h, ci],
                                 *[r[rows, cols].astype(F32) for r in (u_ref, w_ref, qg_ref, kd_ref)],
                                 aqk_ref[h, rows, :], gl_row)
                res.append(vjp((do_ref[rows, cols], ds_ref[h])))
            dgl = jnp.zeros((GDN_CHUNK, GDN_HEADS), F32)
            for h, (ds, du, dw, dqg, dkd, daqk, dgl_h) in enumerate(res):
                cols = _head_cols(h)
                ds_ref[h] = ds
                du_ref[rows, cols] = du.astype(du_ref.dtype)
                dw_ref[rows, cols] = dw.astype(dw_ref.dtype)
                dqg_ref[rows, cols] = dqg.astype(dqg_ref.dtype)
                dkd_ref[rows, cols] = dkd.astype(dkd_ref.dtype)
                daqk_ref[h, rows, :] = daqk
                dgl = dgl + dgl_h
            dgl_ref[rows, :] = dgl
            return carry

        lax.fori_loop(0, cb, chunk, 0)

    return _call(
        body, name="gdn_rec_bwd", grid=(nblk,),
        in_specs=[sp["tok"]] * 4 + [sp["aqk"], sp["gate"], sp["state"], sp["tok"]],
        out_specs=[sp["tok"]] * 4 + [sp["aqk"], sp["gate"]],
        out_shape=[sh["tok"]] * 4 + [sh["aqk"], sh["gate"]],
        scratch_shapes=[pltpu.VMEM((GDN_HEADS, GDN_HD, GDN_HD), F32)],
        compiler_params=_cparams(("arbitrary",)),
    )(u, w, qg, kd, aqk, gl, s_all, do)


@jax.custom_vjp
def gdn_intra(qkv, b, g):
    return tuple(_gdn_intra_fwd_call(qkv, b, g))


def _gdn_intra_fwd(*a):
    outs = gdn_intra(*a)
    return outs, a + (outs[0], outs[1])


gdn_intra.defvjp(_gdn_intra_fwd, lambda res, cts: tuple(_gdn_intra_bwd_call(*res, *cts)))


@jax.custom_vjp
def gdn_rec(u, w, qg, kd, aqk, gl):
    return _gdn_rec_fwd_call(u, w, qg, kd, aqk, gl)[0]


def _gdn_rec_fwd(*a):
    o, s_all = _gdn_rec_fwd_call(*a)
    return o, a + (s_all,)


gdn_rec.defvjp(_gdn_rec_fwd, lambda res, do: tuple(_gdn_rec_bwd_call(*res, do)))


def gated_delta(qkv, b, g):
    return gdn_rec(*gdn_intra(qkv, b, g))


def _loss_call(y, tgt, first, last):
    r_total, d = y.shape
    tm = _tile(r_total, 256, 8)

    def body(y_ref, t_ref, loss_ref, dy_ref):
        i = pl.program_id(0)

        @pl.when(i == 0)
        def _():
            loss_ref[...] = jnp.zeros_like(loss_ref)

        row = lax.broadcasted_iota(jnp.int32, (tm, d), 0) + i * tm
        err = jnp.where((row >= first) & (row < last), y_ref[...] - t_ref[...], 0.0)
        dy_ref[...] = err * (1.0 / d)
        part = jnp.sum(jnp.sum(err * err, axis=1, keepdims=True), axis=0, keepdims=True) * (0.5 / d)
        loss_ref[...] += jnp.broadcast_to(part, loss_ref.shape)

    return _call(
        body, name="loss_head", grid=(r_total // tm,),
        in_specs=[pl.BlockSpec((tm, d), lambda i: (i, 0))] * 2,
        out_specs=[pl.BlockSpec((8, LANE), lambda i: (0, 0)), pl.BlockSpec((tm, d), lambda i: (i, 0))],
        out_shape=[jax.ShapeDtypeStruct((8, LANE), F32), jax.ShapeDtypeStruct(y.shape, F32)],
        compiler_params=_cparams(("arbitrary",)),
    )(y, tgt)


def make_loss(first, last):
    @jax.custom_vjp
    def op(y, tgt):
        return _loss_call(y, tgt, first, last)[0][0, 0]

    def fwd(y, tgt):
        loss, dy = _loss_call(y, tgt, first, last)
        return loss[0, 0], (dy,)

    def bwd(res, ct):
        return res[0] * ct, jnp.zeros_like(res[0])

    op.defvjp(fwd, bwd)
    return op


def _pad_rows8(w):
    return jnp.concatenate([w, jnp.zeros((8 - w.shape[0], w.shape[1]), w.dtype)], axis=0)


def local_loss(wts, x, tgt):
    seq = x.shape[0]
    n_tok = N_META + seq
    t_pad = -(-n_tok // ROW_ALIGN) * ROW_ALIGN
    depth = wts["norm1_g"].shape[0]
    blk = _tile(t_pad, ATT_BLK, LANE)
    nb = t_pad // blk
    tm = _tile(t_pad, 256, 8)

    rms = rowop(_f_rmsnorm, "rmsnorm", (D_MODEL,), tm, out_dtypes=[BF16])
    qnorm = rowop(_f_qnorm, "fox_q_norm", (FOX_HD,), _tile(FOX_HEADS * t_pad, 2048, 8))
    knorm = rowop(_f_rmsnorm, "fox_k_norm", (FOX_HD,), _tile(FOX_HEADS * t_pad, 2048, 8))
    logsig = rowop(_f_logsig, "fox_log_forget", (FOX_HEADS,), tm)
    gdn_act = rowop([_f_gdn_q] * GDN_HEADS + [_f_gdn_k] * GDN_HEADS + [_f_gdn_v] * GDN_HEADS, "gdn_qkv_act",
                    (GDN_HD,), tm)
    gates = rowop(_f_gdn_gates, "gdn_gates", (GDN_HEADS, GDN_HEADS), tm)
    gdn_out = rowop([_f_gdn_out] * GDN_HEADS, "gdn_out_norm", (GDN_HD,), tm, out_dtypes=[BF16])
    merge = rowop(_f_merge, "branch_merge", (D_MODEL,), tm, out_dtypes=[BF16])
    residual = rowop(_f_residual, "residual_add", (D_MODEL,), tm, bc=(2,))
    residual_norm = rowop(_f_residual_norm, "residual_add_norm", (D_MODEL, D_MODEL), tm, bc=(2,),
                          out_dtypes=[F32, BF16])
    keep = (jnp.arange(t_pad)[:, None] < n_tok).astype(F32)
    conv4 = make_dwconv(GDN_CONV, BF16)
    conv3 = make_dwconv(FFN_CONV)
    loss_op = make_loss(N_META, n_tok)

    zeros = jnp.zeros((t_pad - n_tok, D_MODEL), F32)
    h_res = jnp.concatenate([wts["meta_tokens"], x, zeros], axis=0)
    tgt_rows = jnp.concatenate([jnp.zeros((N_META, D_MODEL), F32), tgt, zeros], axis=0)

    def heads(a):
        return a.reshape(t_pad, FOX_HEADS, FOX_HD).transpose(1, 0, 2).reshape(FOX_HEADS * t_pad, FOX_HD)

    h = rms((h_res,), (wts["norm1_g"][0][None],))[0]
    for l in range(depth):
        proj = mm_bf16(h, wts["w_in"][l])
        gate_logits = mm(h, jnp.concatenate([wts["w_in"][l][:, 1536:1536 + LANE],
                                             wts["w_in"][l][:, 4736:4736 + LANE]], axis=1))
        qn = qnorm((heads(proj[:, 0:512]),), (wts["fox_q_norm_g"][l][None],))[0]
        kn = knorm((heads(proj[:, 512:1024]),), (wts["fox_k_norm_g"][l][None],))[0]
        vh = heads(proj[:, 1024:1536])
        log_f = logsig((gate_logits[:, 0:FOX_HEADS],), (wts["fox_f_bias"][l][None],))[0]
        f_cum = cumsum_lanes(log_f.T)
        o_a = fox_attention(qn.reshape(FOX_HEADS, t_pad, FOX_HD), kn.reshape(FOX_HEADS, t_pad, FOX_HD),
                            vh.reshape(FOX_HEADS, t_pad, FOX_HD), f_cum[:, :, None],
                            f_cum.reshape(FOX_HEADS, nb, 1, blk))
        y_a = mm_bf16(o_a.transpose(2, 0, 1).reshape(t_pad, FOX_W).astype(BF16), wts["w_branch_a"][l])
        cv = conv4(proj[:, 1664:4736], _pad_rows8(wts["gdn_conv_w"][l]))
        qkv = gdn_act((cv,), ())[0]
        beta, gdec = gates((gate_logits[:, LANE:LANE + GDN_HEADS], gate_logits[:, LANE + GDN_HEADS:LANE + 2 * GDN_HEADS]),
                           (wts["gdn_a_log"][l][None], wts["gdn_dt_bias"][l][None]))
        o_b = gated_delta(qkv, beta, gdec)
        o_b = gdn_out((o_b, proj[:, 4864:5888]), (wts["gdn_norm_g"][l][None],))[0]
        y_b = mm_bf16(o_b, wts["w_branch_b"][l])
        mixed = merge((proj[:, 5888:6912], proj[:, 6912:7936], y_a, y_b), ())[0]
        h_res, h = residual_norm((h_res, mm(mixed, wts["w_out"][l]), keep), (wts["norm2_g"][l][None],))
        up = conv3(mm_bf16(h, wts["w_up"][l]), _pad_rows8(wts["ffn_conv_w"][l]))
        act = glu(up)
        down = mm(act, wts["w_down"][l])
        if l + 1 < depth:
            h_res, h = residual_norm((h_res, down, keep), (wts["norm1_g"][l + 1][None],))
        else:
            h_res = residual((h_res, down, keep), ())[0]
    return loss_op(h_res, tgt_rows)


def pad_w_in(w):
    parts, pos = [], 0
    for src, width, dst in IN_SEGS:
        if dst > pos:
            parts.append(jnp.zeros(w.shape[:-1] + (dst - pos,), w.dtype))
        parts.append(w[..., src:src + width])
        pos = dst + width
    parts.append(jnp.zeros(w.shape[:-1] + (D_IN_PAD - pos,), w.dtype))
    return jnp.concatenate(parts, axis=-1)


def unpad_w_in(w):
    return jnp.concatenate([w[..., dst:dst + width] for _, width, dst in IN_SEGS], axis=-1)


ANY = pl.BlockSpec(memory_space=pl.ANY)
N_CHIPS = 4
N_DEV = 8
COMM_COLS = 1024
COMM_ROW_ALIGN = 512
COMM_ROW_ALIGN_SMALL = 32


def _place():
    return lax.axis_index("x"), lax.axis_index("y"), lax.axis_index("c")


def _other_chips(x, y):
    return [(1 - x, y), (x, 1 - y), (1 - x, 1 - y)]


def _remote(src, dst, send_sem, recv_sem, dev):
    return pltpu.make_async_remote_copy(src_ref=src, dst_ref=dst, send_sem=send_sem, recv_sem=recv_sem,
                                        device_id=dev, device_id_type=MESH)


def chip_all_gather(buf):
    rows, cols = buf.shape
    half = rows // 2

    def body(x_ref, out_ref, send_sems, recv_sems, pass_send, pass_recv):
        x, y, c = _place()
        me = 2 * x + y
        mine, other = pl.ds(c * half, half), pl.ds((1 - c) * half, half)
        sibling = (x, y, 1 - c)
        chips = _other_chips(x, y)
        started = []
        for k, (px, py) in enumerate(chips):
            cp = _remote(x_ref.at[mine], out_ref.at[me, mine], send_sems.at[k], recv_sems.at[k], (px, py, c))
            cp.start()
            started.append(cp)
        for k, (px, py) in enumerate(chips):
            landed = out_ref.at[2 * px + py, mine]
            _remote(landed, landed, send_sems.at[k], recv_sems.at[k], (px, py, c)).wait_recv()
            cp = _remote(landed, landed, pass_send.at[k], pass_recv.at[k], sibling)
            cp.start()
            started.append(cp)
        for k, (px, py) in enumerate(chips):
            passed = out_ref.at[2 * px + py, other]
            _remote(passed, passed, pass_send.at[k], pass_recv.at[k], sibling).wait_recv()
        for cp in started:
            cp.wait_send()

    got = _call(
        body, name="chip_all_gather", in_specs=[ANY], out_specs=ANY,
        out_shape=jax.ShapeDtypeStruct((N_CHIPS, rows, cols), buf.dtype),
        scratch_shapes=[pltpu.SemaphoreType.DMA((3,)), pltpu.SemaphoreType.DMA((3,)),
                        pltpu.SemaphoreType.DMA((3,)), pltpu.SemaphoreType.DMA((3,))],
    )(buf)
    me = 2 * lax.axis_index("x") + lax.axis_index("y")
    return lax.dynamic_update_slice(got, buf[None], (me, 0, 0))


def sibling_swap_halves(g4):
    n, rows, cols = g4.shape
    half = rows // 2

    def body(g_ref, got_ref, send_sem, recv_sem):
        x, y, c = _place()
        cp = _remote(g_ref.at[:, pl.ds((1 - c) * half, half), :], got_ref, send_sem, recv_sem, (x, y, 1 - c))
        cp.start()
        cp.wait()

    return _call(
        body, name="sibling_swap_halves", in_specs=[ANY], out_specs=ANY,
        out_shape=jax.ShapeDtypeStruct((n, half, cols), g4.dtype),
        scratch_shapes=[pltpu.SemaphoreType.DMA, pltpu.SemaphoreType.DMA],
    )(g4)


def add_own_half(g4, got, c):
    n, rows, cols = g4.shape
    half = rows // 2
    tm = _tile(half, 256, 16)
    nt = half // tm

    def body(c_ref, a_ref, b_ref, o_ref):
        o_ref[...] = (a_ref[...] + b_ref[...]).astype(o_ref.dtype)

    return _call(
        body, name="add_own_half",
        grid_spec=pltpu.PrefetchScalarGridSpec(
            num_scalar_prefetch=1, grid=(n, nt),
            in_specs=[pl.BlockSpec((1, tm, cols), lambda j, i, c_ref: (j, c_ref[0] * nt + i, 0)),
                      pl.BlockSpec((1, tm, cols), lambda j, i, c_ref: (j, i, 0))],
            out_specs=pl.BlockSpec((1, tm, cols), lambda j, i, c_ref: (j, i, 0))),
        out_shape=jax.ShapeDtypeStruct(got.shape, BF16),
        compiler_params=_cparams(("parallel", "parallel")),
    )(c.reshape(1).astype(jnp.int32), g4, got)


def chip_scatter(p4):
    n, rows, cols = p4.shape

    def body(p_ref, out_ref, send_sems, recv_sems):
        x, y, c = _place()
        me = 2 * x + y
        chips = _other_chips(x, y)
        started = []
        for k, (px, py) in enumerate(chips):
            cp = _remote(p_ref.at[2 * px + py], out_ref.at[me], send_sems.at[k], recv_sems.at[k], (px, py, c))
            cp.start()
            started.append(cp)
        for k, (px, py) in enumerate(chips):
            landed = out_ref.at[2 * px + py]
            _remote(landed, landed, send_sems.at[k], recv_sems.at[k], (px, py, c)).wait_recv()
        for cp in started:
            cp.wait_send()

    got = _call(
        body, name="chip_scatter", in_specs=[ANY], out_specs=ANY,
        out_shape=jax.ShapeDtypeStruct(p4.shape, p4.dtype),
        scratch_shapes=[pltpu.SemaphoreType.DMA((3,)), pltpu.SemaphoreType.DMA((3,))],
    )(p4)
    me = 2 * lax.axis_index("x") + lax.axis_index("y")
    return lax.dynamic_update_slice(got, lax.dynamic_slice_in_dim(p4, me, 1, axis=0), (me, 0, 0))


def sum_slots(a):
    n, rows, cols = a.shape
    tm = _tile(rows, 256, 16) if rows % 16 == 0 else rows

    def body(a_ref, o_ref):
        acc = a_ref[0].astype(F32)
        for k in range(1, n):
            acc = acc + a_ref[k].astype(F32)
        o_ref[...] = acc

    return _call(
        body, name="sum_slots_%d" % n, grid=(rows // tm,),
        in_specs=[pl.BlockSpec((n, tm, cols), lambda i: (0, i, 0))],
        out_specs=pl.BlockSpec((tm, cols), lambda i: (i, 0)),
        out_shape=jax.ShapeDtypeStruct((rows, cols), F32),
        compiler_params=_cparams(("parallel",)),
    )(a)


def sibling_join(s):
    half, cols = s.shape

    def body(s_ref, got_ref, send_sem, recv_sem):
        x, y, c = _place()
        cp = _remote(s_ref, got_ref, send_sem, recv_sem, (x, y, 1 - c))
        cp.start()
        cp.wait()

    got = _call(
        body, name="sibling_join", in_specs=[ANY], out_specs=ANY,
        out_shape=jax.ShapeDtypeStruct(s.shape, s.dtype),
        scratch_shapes=[pltpu.SemaphoreType.DMA, pltpu.SemaphoreType.DMA],
    )(s)
    c = lax.axis_index("c")
    out = jnp.zeros((2 * half, cols), s.dtype)
    out = lax.dynamic_update_slice(out, s, (c * half, 0))
    return lax.dynamic_update_slice(out, got, ((1 - c) * half, 0))


def all_devices_gather(buf):
    rows, cols = buf.shape

    def body(b_ref, out_ref, send_sems, recv_sems, local_sem):
        x, y, c = _place()
        me = 4 * x + 2 * y + c
        local = pltpu.make_async_copy(b_ref, out_ref.at[me], local_sem)
        local.start()
        peers = [((x + dx) % 2, (y + dy) % 2, (c + dc) % 2)
                 for dx in (0, 1) for dy in (0, 1) for dc in (0, 1) if dx + dy + dc > 0]
        started = []
        for k, peer in enumerate(peers):
            cp = _remote(b_ref, out_ref.at[me], send_sems.at[k], recv_sems.at[k], peer)
            cp.start()
            started.append(cp)
        for k, (px, py, pc) in enumerate(peers):
            landed = out_ref.at[4 * px + 2 * py + pc]
            _remote(landed, landed, send_sems.at[k], recv_sems.at[k], (px, py, pc)).wait_recv()
        for cp in started:
            cp.wait_send()
        local.wait()

    return _call(
        body, name="all_devices_gather", in_specs=[ANY], out_specs=ANY,
        out_shape=jax.ShapeDtypeStruct((N_DEV, rows, cols), buf.dtype),
        scratch_shapes=[pltpu.SemaphoreType.DMA((7,)), pltpu.SemaphoreType.DMA((7,)), pltpu.SemaphoreType.DMA],
    )(buf)


def adamw(w, g, m, v):
    shape = w.shape
    w2, g2, m2, v2 = [a.reshape(-1, shape[-1]) for a in (w, g, m, v)]
    rows, cols = w2.shape
    tm = _tile(rows, 256, 8) if rows % 8 == 0 else rows

    def body(w_ref, g_ref, m_ref, v_ref, d_ref, nm_ref, nv_ref):
        gv = g_ref[...]
        nm = ADAM_B1 * m_ref[...] + (1.0 - ADAM_B1) * gv
        nv = ADAM_B2 * v_ref[...] + (1.0 - ADAM_B2) * (gv * gv)
        m_hat = nm / (1.0 - ADAM_B1 ** ADAM_STEP)
        v_hat = nv / (1.0 - ADAM_B2 ** ADAM_STEP)
        d_ref[...] = -ADAM_LR * (m_hat / (jnp.sqrt(v_hat) + ADAM_EPS) + ADAM_WD * w_ref[...])
        nm_ref[...] = nm
        nv_ref[...] = nv

    spec = pl.BlockSpec((tm, cols), lambda i: (i, 0))
    outs = _call(
        body, name="adamw", grid=(rows // tm,), in_specs=[spec] * 4, out_specs=[spec] * 3,
        out_shape=[jax.ShapeDtypeStruct((rows, cols), F32)] * 3,
        compiler_params=_cparams(("parallel",)),
    )(w2, g2, m2, v2)
    return [o.reshape(shape) for o in outs]


WEIGHTS = ("meta_tokens", "norm1_g", "w_in", "fox_f_bias", "fox_q_norm_g", "fox_k_norm_g", "gdn_conv_w",
           "gdn_a_log", "gdn_dt_bias", "gdn_norm_g", "w_branch_a", "w_branch_b", "w_out", "norm2_g", "w_up",
           "ffn_conv_w", "w_down")
SHARD_AXIS = {"meta_tokens": -1, "w_in": -1, "gdn_conv_w": -1, "w_branch_a": -1, "w_branch_b": -2, "w_out": -2,
              "w_up": -1, "ffn_conv_w": -1, "w_down": -2}
MATMUL_WEIGHTS = ("w_in", "w_branch_a", "w_branch_b", "w_out", "w_up", "w_down")
SMALL_SHARDED = ("meta_tokens", "gdn_conv_w", "ffn_conv_w")
REPLICATED = tuple(n for n in WEIGHTS if n not in SHARD_AXIS)


def _pack(arrays, dtype, row_align):
    flat = jnp.concatenate([a.reshape(-1).astype(dtype) for a in arrays])
    block = row_align * COMM_COLS
    total = -(-flat.shape[0] // block) * block
    flat = jnp.concatenate([flat, jnp.zeros((total - flat.shape[0],), dtype)])
    return flat.reshape(-1, COMM_COLS)


def _unpack(buf, shapes):
    flat, out, pos = buf.reshape(-1), [], 0
    for shape in shapes:
        size = 1
        for d in shape:
            size *= d
        out.append(flat[pos:pos + size].reshape(shape))
        pos += size
    return out


def _gather_full(shards, names, dtype, row_align):
    got = chip_all_gather(_pack([shards[n] for n in names], dtype, row_align))
    per_chip = [_unpack(got[j], [shards[n].shape for n in names]) for j in range(N_CHIPS)]
    return {n: jnp.concatenate([per_chip[j][i] for j in range(N_CHIPS)], axis=SHARD_AXIS[n]).astype(F32)
            for i, n in enumerate(names)}


def _shard_of(full, name, j):
    axis = SHARD_AXIS[name] % full.ndim
    size = full.shape[axis] // N_CHIPS
    return lax.slice_in_dim(full, j * size, (j + 1) * size, axis=axis)


def kernel(x, meta_tokens, norm1_g, w_in, fox_f_bias, fox_q_norm_g, fox_k_norm_g, gdn_conv_w, gdn_a_log, gdn_dt_bias, gdn_norm_g, w_branch_a, w_branch_b, w_out, norm2_g, w_up, ffn_conv_w, w_down, loss_target, m_meta_tokens, m_norm1_g, m_w_in, m_fox_f_bias, m_fox_q_norm_g, m_fox_k_norm_g, m_gdn_conv_w, m_gdn_a_log, m_gdn_dt_bias, m_gdn_norm_g, m_w_branch_a, m_w_branch_b, m_w_out, m_norm2_g, m_w_up, m_ffn_conv_w, m_w_down, v_meta_tokens, v_norm1_g, v_w_in, v_fox_f_bias, v_fox_q_norm_g, v_fox_k_norm_g, v_gdn_conv_w, v_gdn_a_log, v_gdn_dt_bias, v_gdn_norm_g, v_w_branch_a, v_w_branch_b, v_w_out, v_norm2_g, v_w_up, v_ffn_conv_w, v_w_down):
    w_loc = dict(zip(WEIGHTS, (meta_tokens, norm1_g, w_in, fox_f_bias, fox_q_norm_g, fox_k_norm_g, gdn_conv_w,
                               gdn_a_log, gdn_dt_bias, gdn_norm_g, w_branch_a, w_branch_b, w_out, norm2_g, w_up,
                               ffn_conv_w, w_down)))
    m_loc = dict(zip(WEIGHTS, (m_meta_tokens, m_norm1_g, m_w_in, m_fox_f_bias, m_fox_q_norm_g, m_fox_k_norm_g,
                               m_gdn_conv_w, m_gdn_a_log, m_gdn_dt_bias, m_gdn_norm_g, m_w_branch_a, m_w_branch_b,
                               m_w_out, m_norm2_g, m_w_up, m_ffn_conv_w, m_w_down)))
    v_loc = dict(zip(WEIGHTS, (v_meta_tokens, v_norm1_g, v_w_in, v_fox_f_bias, v_fox_q_norm_g, v_fox_k_norm_g,
                               v_gdn_conv_w, v_gdn_a_log, v_gdn_dt_bias, v_gdn_norm_g, v_w_branch_a, v_w_branch_b,
                               v_w_out, v_norm2_g, v_w_up, v_ffn_conv_w, v_w_down)))
    c = lax.axis_index("c")

    full = {n: w_loc[n] for n in REPLICATED}
    full.update(_gather_full(w_loc, MATMUL_WEIGHTS, BF16, COMM_ROW_ALIGN))
    full.update(_gather_full(w_loc, SMALL_SHARDED, F32, COMM_ROW_ALIGN_SMALL))
    full["w_in"] = pad_w_in(full["w_in"])

    loss, (g_full, g_x) = jax.value_and_grad(local_loss, argnums=(0, 1))(full, x[0], loss_target[0])
    g_full = dict(g_full)
    g_full["w_in"] = unpad_w_in(g_full["w_in"])

    sharded = MATMUL_WEIGHTS + SMALL_SHARDED
    g4 = jnp.stack([_pack([_shard_of(g_full[n], n, j) for n in sharded], F32, COMM_ROW_ALIGN)
                    for j in range(N_CHIPS)])
    pair_sum = add_own_half(g4, sibling_swap_halves(g4), c)
    g_shard = sibling_join(sum_slots(chip_scatter(pair_sum)))
    grads = dict(zip(sharded, _unpack(g_shard, [w_loc[n].shape for n in sharded])))
    g_rep = sum_slots(all_devices_gather(_pack([g_full[n] for n in REPLICATED], F32, 8)))
    grads.update(zip(REPLICATED, _unpack(g_rep, [w_loc[n].shape for n in REPLICATED])))

    loss = lax.psum(loss, ("x", "y", "c"))
    upd = {n: adamw(w_loc[n], grads[n], m_loc[n], v_loc[n]) for n in WEIGHTS}
    return (loss, g_x[None], *[grads[n] for n in WEIGHTS], *[upd[n][0] for n in WEIGHTS],
            *[upd[n][1] for n in WEIGHTS], *[upd[n][2] for n in WEIGHTS])
```
